```python
import jax, jax.numpy as jnp
from jax import lax
import numpy as np

D_MODEL = 2048
BATCH = 8
SEQ = 2048
DEPTH = 1

CHUNK = 64
N_META = 16
D_MIX = D_MODEL
D_POOL = D_MIX // 2
POOL_WINDOWS = (2, 4, 8, 16)
N_POOL_GROUPS = len(POOL_WINDOWS)
POOL_GROUP = D_POOL // N_POOL_GROUPS
D_ATT = D_MIX - D_POOL
HEAD_DIM = 128
N_HEADS = D_ATT // HEAD_DIM
D_IN = D_POOL + 3 * D_ATT + N_HEADS
D_FF = ((8 * D_MODEL // 3 + 255) // 256) * 256
Q_BLOCK = 128
EPS = 1e-6

kernel_name = "hymba_pool_fox_macaron_block"


def rmsnorm(x, g):
    xf = x.astype(jnp.float32)
    y = xf * lax.rsqrt(jnp.mean(xf * xf, axis=-1, keepdims=True) + EPS)
    return (y * g.astype(jnp.float32)).astype(x.dtype)


def swiglu(x, w_gate, w_up, w_down):
    return (jax.nn.silu(x @ w_gate) * (x @ w_up)) @ w_down


def pool_mixer(p, pool_w, pool_scale):
    B, L, _ = p.shape
    pg = p.reshape(B, L, N_POOL_GROUPS, POOL_GROUP)
    c = jnp.cumsum(pg.astype(jnp.float32), axis=1)
    c = jnp.pad(c, ((0, 0), (1, 0), (0, 0), (0, 0)))
    win = jnp.array(POOL_WINDOWS, dtype=jnp.int32)
    end = jnp.arange(1, L + 1, dtype=jnp.int32)[:, None]
    start = jnp.maximum(end - win[None, :], 0)
    gidx = jnp.arange(N_POOL_GROUPS, dtype=jnp.int32)[None, :]
    window_sum = c[:, end, gidx] - c[:, start, gidx]
    count = (end - start).astype(jnp.float32)[None, :, :, None]
    pooled = (window_sum / count - pg.astype(jnp.float32)).astype(p.dtype)
    mixed = jnp.einsum('blgc,gcd->blgd', pooled, pool_w)
    return mixed.reshape(B, L, D_POOL) * pool_scale


def fox_attention(q, k, v, log_f):
    B, L, H, Dh = q.shape
    scale = 1.0 / np.sqrt(Dh).astype(np.float32)
    cum = jnp.cumsum(log_f, axis=-1)
    n_blocks = -(-L // Q_BLOCK)
    Lp = n_blocks * Q_BLOCK
    qp = jnp.pad(q, ((0, 0), (0, Lp - L), (0, 0), (0, 0)))
    cqp = jnp.pad(cum, ((0, 0), (0, 0), (0, Lp - L)))
    qb = qp.reshape(B, n_blocks, Q_BLOCK, H, Dh).transpose(1, 0, 2, 3, 4)
    cqb = cqp.reshape(B, H, n_blocks, Q_BLOCK).transpose(2, 0, 1, 3)
    kpos = jnp.arange(L, dtype=jnp.int32)

    def one_block(args):
        qi, cqi, bi = args
        qpos = bi * Q_BLOCK + jnp.arange(Q_BLOCK, dtype=jnp.int32)
        s = jnp.einsum('bqhd,bkhd->bhqk', qi, k).astype(jnp.float32) * scale
        s = s + (cqi[:, :, :, None] - cum[:, :, None, :])
        s = jnp.where(qpos[:, None] >= kpos[None, :], s, -jnp.inf)
        pr = jax.nn.softmax(s, axis=-1)
        return jnp.einsum('bhqk,bkhd->bqhd', pr.astype(v.dtype), v)

    out = lax.map(one_block, (qb, cqb, jnp.arange(n_blocks, dtype=jnp.int32)))
    out = out.transpose(1, 0, 2, 3, 4).reshape(B, Lp, H * Dh)
    return out[:, :L]


def _fwd_setup_inputs(seed: int = 0) -> dict:
    key = jax.random.key(seed)
    ks = jax.random.split(key, 20)
    f32 = jnp.float32

    def nrm(k, shape, s):
        return jax.random.normal(k, shape, f32) * s

    def gain(k, shape):
        return 1.0 + 0.05 * jax.random.normal(k, shape, f32)

    return {
        "x": jax.random.normal(ks[0], (BATCH, SEQ, D_MODEL), f32),
        "meta_tokens": nrm(ks[1], (N_META, D_MODEL), 1.0),
        "ffn1_norm": gain(ks[2], (DEPTH, D_MODEL)),
        "ffn1_w_gate": nrm(ks[3], (DEPTH, D_MODEL, D_FF), D_MODEL ** -0.5),
        "ffn1_w_up": nrm(ks[4], (DEPTH, D_MODEL, D_FF), D_MODEL ** -0.5),
        "ffn1_w_down": nrm(ks[5], (DEPTH, D_FF, D_MODEL), D_FF ** -0.5),
        "mix_norm": gain(ks[6], (DEPTH, D_MODEL)),
        "w_in": nrm(ks[7], (DEPTH, D_MODEL, D_IN), D_MODEL ** -0.5),
        "b_forget": jax.random.uniform(ks[8], (DEPTH, N_HEADS), f32, minval=1.0, maxval=5.0),
        "q_norm": gain(ks[9], (DEPTH, HEAD_DIM)),
        "k_norm": gain(ks[10], (DEPTH, HEAD_DIM)),
        "pool_w": nrm(ks[11], (DEPTH, N_POOL_GROUPS, POOL_GROUP, POOL_GROUP), POOL_GROUP ** -0.5),
        "pool_scale": 1.0 + 0.1 * jax.random.normal(ks[12], (DEPTH, D_POOL), f32),
        "w_out": nrm(ks[13], (DEPTH, D_MIX, D_MODEL), D_MIX ** -0.5),
        "ffn2_norm": gain(ks[14], (DEPTH, D_MODEL)),
        "ffn2_w_gate": nrm(ks[15], (DEPTH, D_MODEL, D_FF), D_MODEL ** -0.5),
        "ffn2_w_up": nrm(ks[16], (DEPTH, D_MODEL, D_FF), D_MODEL ** -0.5),
        "ffn2_w_down": nrm(ks[17], (DEPTH, D_FF, D_MODEL), D_FF ** -0.5),
    }


def _fwd_reference(x, meta_tokens, ffn1_norm, ffn1_w_gate, ffn1_w_up, ffn1_w_down, mix_norm, w_in,
              b_forget, q_norm, k_norm, pool_w, pool_scale, w_out, ffn2_norm, ffn2_w_gate,
              ffn2_w_up, ffn2_w_down):
    B = x.shape[0]
    meta = jnp.broadcast_to(meta_tokens[None].astype(x.dtype), (B, N_META, D_MODEL))
    h = jnp.concatenate([meta, x], axis=1)
    L = h.shape[1]
    for i in range(DEPTH):
        h = h + 0.5 * swiglu(rmsnorm(h, ffn1_norm[i]), ffn1_w_gate[i], ffn1_w_up[i], ffn1_w_down[i])

        u = rmsnorm(h, mix_norm[i])
        z = u @ w_in[i]
        o = D_POOL
        p = z[..., :o]
        q = z[..., o:o + D_ATT].reshape(B, L, N_HEADS, HEAD_DIM)
        k = z[..., o + D_ATT:o + 2 * D_ATT].reshape(B, L, N_HEADS, HEAD_DIM)
        v = z[..., o + 2 * D_ATT:o + 3 * D_ATT].reshape(B, L, N_HEADS, HEAD_DIM)
        f_logit = z[..., o + 3 * D_ATT:]

        pool_out = pool_mixer(p, pool_w[i], pool_scale[i])

        q = rmsnorm(q, q_norm[i])
        k = rmsnorm(k, k_norm[i])
        log_f = jax.nn.log_sigmoid(f_logit.astype(jnp.float32) + b_forget[i].astype(jnp.float32))
        att_out = fox_attention(q, k, v, log_f.transpose(0, 2, 1))

        mix = jnp.concatenate([pool_out, att_out.astype(pool_out.dtype)], axis=-1)
        h = h + mix @ w_out[i]

        h = h + 0.5 * swiglu(rmsnorm(h, ffn2_norm[i]), ffn2_w_gate[i], ffn2_w_up[i], ffn2_w_down[i])
    return h[:, N_META:]


import jax as _jax
import jax.numpy as _jnp

TWIN_FORMAT = 'train_step'
FWD_PARAMS = ['x', 'meta_tokens', 'ffn1_norm', 'ffn1_w_gate', 'ffn1_w_up', 'ffn1_w_down', 'mix_norm', 'w_in', 'b_forget', 'q_norm', 'k_norm', 'pool_w', 'pool_scale', 'w_out', 'ffn2_norm', 'ffn2_w_gate', 'ffn2_w_up', 'ffn2_w_down']
TWIN_WEIGHTS = ['meta_tokens', 'ffn1_norm', 'ffn1_w_gate', 'ffn1_w_up', 'ffn1_w_down', 'mix_norm', 'w_in', 'b_forget', 'q_norm', 'k_norm', 'pool_w', 'pool_scale', 'w_out', 'ffn2_norm', 'ffn2_w_gate', 'ffn2_w_up', 'ffn2_w_down']
TWIN_DIFF_INPUT = 'x'
TWIN_INPUTS = ['x', 'meta_tokens', 'ffn1_norm', 'ffn1_w_gate', 'ffn1_w_up', 'ffn1_w_down', 'mix_norm', 'w_in', 'b_forget', 'q_norm', 'k_norm', 'pool_w', 'pool_scale', 'w_out', 'ffn2_norm', 'ffn2_w_gate', 'ffn2_w_up', 'ffn2_w_down', 'loss_target', 'm_meta_tokens', 'm_ffn1_norm', 'm_ffn1_w_gate', 'm_ffn1_w_up', 'm_ffn1_w_down', 'm_mix_norm', 'm_w_in', 'm_b_forget', 'm_q_norm', 'm_k_norm', 'm_pool_w', 'm_pool_scale', 'm_w_out', 'm_ffn2_norm', 'm_ffn2_w_gate', 'm_ffn2_w_up', 'm_ffn2_w_down', 'v_meta_tokens', 'v_ffn1_norm', 'v_ffn1_w_gate', 'v_ffn1_w_up', 'v_ffn1_w_down', 'v_mix_norm', 'v_w_in', 'v_b_forget', 'v_q_norm', 'v_k_norm', 'v_pool_w', 'v_pool_scale', 'v_w_out', 'v_ffn2_norm', 'v_ffn2_w_gate', 'v_ffn2_w_up', 'v_ffn2_w_down']
TWIN_OUTPUTS = ['loss', 'grad_x', 'grad_meta_tokens', 'grad_ffn1_norm', 'grad_ffn1_w_gate', 'grad_ffn1_w_up', 'grad_ffn1_w_down', 'grad_mix_norm', 'grad_w_in', 'grad_b_forget', 'grad_q_norm', 'grad_k_norm', 'grad_pool_w', 'grad_pool_scale', 'grad_w_out', 'grad_ffn2_norm', 'grad_ffn2_w_gate', 'grad_ffn2_w_up', 'grad_ffn2_w_down', 'delta_meta_tokens', 'delta_ffn1_norm', 'delta_ffn1_w_gate', 'delta_ffn1_w_up', 'delta_ffn1_w_down', 'delta_mix_norm', 'delta_w_in', 'delta_b_forget', 'delta_q_norm', 'delta_k_norm', 'delta_pool_w', 'delta_pool_scale', 'delta_w_out', 'delta_ffn2_norm', 'delta_ffn2_w_gate', 'delta_ffn2_w_up', 'delta_ffn2_w_down', 'new_m_meta_tokens', 'new_m_ffn1_norm', 'new_m_ffn1_w_gate', 'new_m_ffn1_w_up', 'new_m_ffn1_w_down', 'new_m_mix_norm', 'new_m_w_in', 'new_m_b_forget', 'new_m_q_norm', 'new_m_k_norm', 'new_m_pool_w', 'new_m_pool_scale', 'new_m_w_out', 'new_m_ffn2_norm', 'new_m_ffn2_w_gate', 'new_m_ffn2_w_up', 'new_m_ffn2_w_down', 'new_v_meta_tokens', 'new_v_ffn1_norm', 'new_v_ffn1_w_gate', 'new_v_ffn1_w_up', 'new_v_ffn1_w_down', 'new_v_mix_norm', 'new_v_w_in', 'new_v_b_forget', 'new_v_q_norm', 'new_v_k_norm', 'new_v_pool_w', 'new_v_pool_scale', 'new_v_w_out', 'new_v_ffn2_norm', 'new_v_ffn2_w_gate', 'new_v_ffn2_w_up', 'new_v_ffn2_w_down']
TWIN_LEAF_KINDS = {'loss': 'loss', 'grad_x': 'grad_x', 'grad_meta_tokens': 'grad_w', 'grad_ffn1_norm': 'grad_w', 'grad_ffn1_w_gate': 'grad_w', 'grad_ffn1_w_up': 'grad_w', 'grad_ffn1_w_down': 'grad_w', 'grad_mix_norm': 'grad_w', 'grad_w_in': 'grad_w', 'grad_b_forget': 'grad_w', 'grad_q_norm': 'grad_w', 'grad_k_norm': 'grad_w', 'grad_pool_w': 'grad_w', 'grad_pool_scale': 'grad_w', 'grad_w_out': 'grad_w', 'grad_ffn2_norm': 'grad_w', 'grad_ffn2_w_gate': 'grad_w', 'grad_ffn2_w_up': 'grad_w', 'grad_ffn2_w_down': 'grad_w', 'delta_meta_tokens': 'delta_w', 'delta_ffn1_norm': 'delta_w', 'delta_ffn1_w_gate': 'delta_w', 'delta_ffn1_w_up': 'delta_w', 'delta_ffn1_w_down': 'delta_w', 'delta_mix_norm': 'delta_w', 'delta_w_in': 'delta_w', 'delta_b_forget': 'delta_w', 'delta_q_norm': 'delta_w', 'delta_k_norm': 'delta_w', 'delta_pool_w': 'delta_w', 'delta_pool_scale': 'delta_w', 'delta_w_out': 'delta_w', 'delta_ffn2_norm': 'delta_w', 'delta_ffn2_w_gate': 'delta_w', 'delta_ffn2_w_up': 'delta_w', 'delta_ffn2_w_down': 'delta_w', 'new_m_meta_tokens': 'new_m', 'new_m_ffn1_norm': 'new_m', 'new_m_ffn1_w_gate': 'new_m', 'new_m_ffn1_w_up': 'new_m', 'new_m_ffn1_w_down': 'new_m', 'new_m_mix_norm': 'new_m', 'new_m_w_in': 'new_m', 'new_m_b_forget': 'new_m', 'new_m_q_norm': 'new_m', 'new_m_k_norm': 'new_m', 'new_m_pool_w': 'new_m', 'new_m_pool_scale': 'new_m', 'new_m_w_out': 'new_m', 'new_m_ffn2_norm': 'new_m', 'new_m_ffn2_w_gate': 'new_m', 'new_m_ffn2_w_up': 'new_m', 'new_m_ffn2_w_down': 'new_m', 'new_v_meta_tokens': 'new_v', 'new_v_ffn1_norm': 'new_v', 'new_v_ffn1_w_gate': 'new_v', 'new_v_ffn1_w_up': 'new_v', 'new_v_ffn1_w_down': 'new_v', 'new_v_mix_norm': 'new_v', 'new_v_w_in': 'new_v', 'new_v_b_forget': 'new_v', 'new_v_q_norm': 'new_v', 'new_v_k_norm': 'new_v', 'new_v_pool_w': 'new_v', 'new_v_pool_scale': 'new_v', 'new_v_w_out': 'new_v', 'new_v_ffn2_norm': 'new_v', 'new_v_ffn2_w_gate': 'new_v', 'new_v_ffn2_w_up': 'new_v', 'new_v_ffn2_w_down': 'new_v'}


def _forward(args):
    return _fwd_reference(*[args[k] for k in FWD_PARAMS])


def _output_shape():
    out = _jax.eval_shape(lambda: _forward(_fwd_setup_inputs(0)))
    return out.shape, out.dtype

N_MICROBATCH = 1
ADAM_LR = 0.001
ADAM_B1 = 0.9
ADAM_B2 = 0.999
ADAM_EPS = 1e-08
ADAM_WD = 0.01
ADAM_STEP = 10
PER_EXAMPLE_BATCH_AXIS = {'x': 0, 'loss_target': 0}
SHARED_INPUTS = []
_WEIGHT_DTYPES = {'meta_tokens': _jnp.float32, 'ffn1_norm': _jnp.float32, 'ffn1_w_gate': _jnp.float32, 'ffn1_w_up': _jnp.float32, 'ffn1_w_down': _jnp.float32, 'mix_norm': _jnp.float32, 'w_in': _jnp.float32, 'b_forget': _jnp.float32, 'q_norm': _jnp.float32, 'k_norm': _jnp.float32, 'pool_w': _jnp.float32, 'pool_scale': _jnp.float32, 'w_out': _jnp.float32, 'ffn2_norm': _jnp.float32, 'ffn2_w_gate': _jnp.float32, 'ffn2_w_up': _jnp.float32, 'ffn2_w_down': _jnp.float32}
MOMENT_SCALE = {'meta_tokens': 2.601989e-03, 'ffn1_norm': 1.541540e+00, 'ffn1_w_gate': 2.996254e-02, 'ffn1_w_up': 3.117594e-02, 'ffn1_w_down': 5.141772e-02, 'mix_norm': 3.837722e+00, 'w_in': 1.306197e-01, 'b_forget': 5.010922e+01, 'q_norm': 3.292401e+00, 'k_norm': 3.302097e+00, 'pool_w': 4.380934e-01, 'pool_scale': 6.383811e+00, 'w_out': 1.672104e-01, 'ffn2_norm': 1.537686e+00, 'ffn2_w_gate': 2.060761e-02, 'ffn2_w_up': 2.320158e-02, 'ffn2_w_down': 3.705052e-02}


def _to_microbatches(a, axis):
    t = _jnp.moveaxis(a, axis, 0)
    t = t.reshape((N_MICROBATCH, t.shape[0] // N_MICROBATCH) + t.shape[1:])
    return _jnp.moveaxis(t, 1, axis + 1)


def setup_inputs(seed: int = 0) -> dict:
    inp = _fwd_setup_inputs(seed)
    key = _jax.random.fold_in(_jax.random.key(seed), 7919)
    shape, _ = _output_shape()
    out = dict(inp)
    out["loss_target"] = _jax.random.normal(_jax.random.fold_in(key, 0), shape, _jnp.float32)
    for i, name in enumerate(TWIN_WEIGHTS):
        w = inp[name].astype(_jnp.float32)
        if MOMENT_SCALE is None:
            s = _jnp.sqrt(_jnp.mean(_jnp.square(w)) + 1e-30)
        else:
            s = MOMENT_SCALE[name]
        km, kv = _jax.random.split(_jax.random.fold_in(key, i + 1))
        out[name] = w
        out["m_" + name] = s * _jax.random.normal(km, w.shape, _jnp.float32)
        out["v_" + name] = (s * s) * _jax.random.uniform(kv, w.shape, _jnp.float32, 0.5, 1.5)
    if N_MICROBATCH > 1:
        for name, axis in PER_EXAMPLE_BATCH_AXIS.items():
            out[name] = _to_microbatches(out[name], axis)
    return {'x': out['x'], 'meta_tokens': out['meta_tokens'], 'ffn1_norm': out['ffn1_norm'], 'ffn1_w_gate': out['ffn1_w_gate'], 'ffn1_w_up': out['ffn1_w_up'], 'ffn1_w_down': out['ffn1_w_down'], 'mix_norm': out['mix_norm'], 'w_in': out['w_in'], 'b_forget': out['b_forget'], 'q_norm': out['q_norm'], 'k_norm': out['k_norm'], 'pool_w': out['pool_w'], 'pool_scale': out['pool_scale'], 'w_out': out['w_out'], 'ffn2_norm': out['ffn2_norm'], 'ffn2_w_gate': out['ffn2_w_gate'], 'ffn2_w_up': out['ffn2_w_up'], 'ffn2_w_down': out['ffn2_w_down'], 'loss_target': out['loss_target'], 'm_meta_tokens': out['m_meta_tokens'], 'm_ffn1_norm': out['m_ffn1_norm'], 'm_ffn1_w_gate': out['m_ffn1_w_gate'], 'm_ffn1_w_up': out['m_ffn1_w_up'], 'm_ffn1_w_down': out['m_ffn1_w_down'], 'm_mix_norm': out['m_mix_norm'], 'm_w_in': out['m_w_in'], 'm_b_forget': out['m_b_forget'], 'm_q_norm': out['m_q_norm'], 'm_k_norm': out['m_k_norm'], 'm_pool_w': out['m_pool_w'], 'm_pool_scale': out['m_pool_scale'], 'm_w_out': out['m_w_out'], 'm_ffn2_norm': out['m_ffn2_norm'], 'm_ffn2_w_gate': out['m_ffn2_w_gate'], 'm_ffn2_w_up': out['m_ffn2_w_up'], 'm_ffn2_w_down': out['m_ffn2_w_down'], 'v_meta_tokens': out['v_meta_tokens'], 'v_ffn1_norm': out['v_ffn1_norm'], 'v_ffn1_w_gate': out['v_ffn1_w_gate'], 'v_ffn1_w_up': out['v_ffn1_w_up'], 'v_ffn1_w_down': out['v_ffn1_w_down'], 'v_mix_norm': out['v_mix_norm'], 'v_w_in': out['v_w_in'], 'v_b_forget': out['v_b_forget'], 'v_q_norm': out['v_q_norm'], 'v_k_norm': out['v_k_norm'], 'v_pool_w': out['v_pool_w'], 'v_pool_scale': out['v_pool_scale'], 'v_w_out': out['v_w_out'], 'v_ffn2_norm': out['v_ffn2_norm'], 'v_ffn2_w_gate': out['v_ffn2_w_gate'], 'v_ffn2_w_up': out['v_ffn2_w_up'], 'v_ffn2_w_down': out['v_ffn2_w_down']}


def _loss(weights, diff, rest, loss_target):
    with _jax.named_scope("forward"):
        args = {**rest, TWIN_DIFF_INPUT: diff, **{k: w.astype(_WEIGHT_DTYPES[k]) for k, w in weights.items()}}
        y = _forward(args)
    with _jax.named_scope("loss_head"):
        err = _jnp.square(y.astype(_jnp.float32) - loss_target)
        return 0.5 * _jnp.sum(_jnp.mean(err, axis=-1)) if err.ndim else 0.5 * err


def _adamw(w, g, m, v):
    m = ADAM_B1 * m + (1.0 - ADAM_B1) * g
    v = ADAM_B2 * v + (1.0 - ADAM_B2) * _jnp.square(g)
    m_hat = m / (1.0 - ADAM_B1 ** ADAM_STEP)
    v_hat = v / (1.0 - ADAM_B2 ** ADAM_STEP)
    delta = -ADAM_LR * (m_hat / (_jnp.sqrt(v_hat) + ADAM_EPS) + ADAM_WD * w)
    return delta, m, v


def reference(x, meta_tokens, ffn1_norm, ffn1_w_gate, ffn1_w_up, ffn1_w_down, mix_norm, w_in, b_forget, q_norm, k_norm, pool_w, pool_scale, w_out, ffn2_norm, ffn2_w_gate, ffn2_w_up, ffn2_w_down, loss_target, m_meta_tokens, m_ffn1_norm, m_ffn1_w_gate, m_ffn1_w_up, m_ffn1_w_down, m_mix_norm, m_w_in, m_b_forget, m_q_norm, m_k_norm, m_pool_w, m_pool_scale, m_w_out, m_ffn2_norm, m_ffn2_w_gate, m_ffn2_w_up, m_ffn2_w_down, v_meta_tokens, v_ffn1_norm, v_ffn1_w_gate, v_ffn1_w_up, v_ffn1_w_down, v_mix_norm, v_w_in, v_b_forget, v_q_norm, v_k_norm, v_pool_w, v_pool_scale, v_w_out, v_ffn2_norm, v_ffn2_w_gate, v_ffn2_w_up, v_ffn2_w_down):
    given = dict(x=x, meta_tokens=meta_tokens, ffn1_norm=ffn1_norm, ffn1_w_gate=ffn1_w_gate, ffn1_w_up=ffn1_w_up, ffn1_w_down=ffn1_w_down, mix_norm=mix_norm, w_in=w_in, b_forget=b_forget, q_norm=q_norm, k_norm=k_norm, pool_w=pool_w, pool_scale=pool_scale, w_out=w_out, ffn2_norm=ffn2_norm, ffn2_w_gate=ffn2_w_gate, ffn2_w_up=ffn2_w_up, ffn2_w_down=ffn2_w_down, loss_target=loss_target, m_meta_tokens=m_meta_tokens, m_ffn1_norm=m_ffn1_norm, m_ffn1_w_gate=m_ffn1_w_gate, m_ffn1_w_up=m_ffn1_w_up, m_ffn1_w_down=m_ffn1_w_down, m_mix_norm=m_mix_norm, m_w_in=m_w_in, m_b_forget=m_b_forget, m_q_norm=m_q_norm, m_k_norm=m_k_norm, m_pool_w=m_pool_w, m_pool_scale=m_pool_scale, m_w_out=m_w_out, m_ffn2_norm=m_ffn2_norm, m_ffn2_w_gate=m_ffn2_w_gate, m_ffn2_w_up=m_ffn2_w_up, m_ffn2_w_down=m_ffn2_w_down, v_meta_tokens=v_meta_tokens, v_ffn1_norm=v_ffn1_norm, v_ffn1_w_gate=v_ffn1_w_gate, v_ffn1_w_up=v_ffn1_w_up, v_ffn1_w_down=v_ffn1_w_down, v_mix_norm=v_mix_norm, v_w_in=v_w_in, v_b_forget=v_b_forget, v_q_norm=v_q_norm, v_k_norm=v_k_norm, v_pool_w=v_pool_w, v_pool_scale=v_pool_scale, v_w_out=v_w_out, v_ffn2_norm=v_ffn2_norm, v_ffn2_w_gate=v_ffn2_w_gate, v_ffn2_w_up=v_ffn2_w_up, v_ffn2_w_down=v_ffn2_w_down)
    weights = {n: given[n] for n in TWIN_WEIGHTS}
    shared = {n: given[n] for n in SHARED_INPUTS}
    per_example = {n: given[n] for n in ['x']}
    grad_fn = _jax.value_and_grad(_loss, argnums=(0, 1))

    def one_microbatch(ex, loss_target):
        ex = dict(ex)
        diff = ex.pop(TWIN_DIFF_INPUT)
        return grad_fn(weights, diff, {**shared, **ex}, loss_target)

    if N_MICROBATCH == 1:
        loss, (grad_w, grad_x) = one_microbatch(per_example, given["loss_target"])
    else:
        def body(carry, xs):
            loss_sum, grad_sum = carry
            l_k, (gw_k, gx_k) = one_microbatch(xs[0], xs[1])
            with _jax.named_scope("update"):
                return (loss_sum + l_k, _jax.tree.map(_jnp.add, grad_sum, gw_k)), gx_k

        init = (_jnp.zeros((), _jnp.float32), _jax.tree.map(_jnp.zeros_like, weights))
        (loss, grad_w), grad_x = _jax.lax.scan(body, init, (per_example, given["loss_target"]))
    with _jax.named_scope("update"):
        delta_w, new_m, new_v = {}, {}, {}
        for n in TWIN_WEIGHTS:
            delta_w[n], new_m[n], new_v[n] = _adamw(weights[n], grad_w[n], given["m_" + n], given["v_" + n])
    return (loss, grad_x, *[grad_w[n] for n in TWIN_WEIGHTS], *[delta_w[n] for n in TWIN_WEIGHTS],
            *[new_m[n] for n in TWIN_WEIGHTS], *[new_v[n] for n in TWIN_WEIGHTS])
```

```python
import functools

import jax
import jax.numpy as jnp
from jax import lax
from jax.experimental import pallas as pl
from jax.experimental.pallas import tpu as pltpu

F32 = jnp.float32
BF = jnp.bfloat16
SDS = jax.ShapeDtypeStruct

N_DEV = 8
LANES = 128
SUBLANES = 8
HEAD_DIM = 128
POOL_WINDOWS = (2, 4, 8, 16)
RMS_EPS = 1e-6
NEG_BIG = -1e30
MIB = 1024 * 1024

ADAM_LR = 0.001
ADAM_B1 = 0.9
ADAM_B2 = 0.999
ADAM_EPS = 1e-08
ADAM_WD = 0.01
ADAM_STEP = 10


def _params(vmem_mib, n_grid):
    return pltpu.CompilerParams(dimension_semantics=("arbitrary",) * n_grid, vmem_limit_bytes=vmem_mib * MIB)


def _largest_tile(n, cap, mult):
    if n <= cap:
        return n
    best = None
    for t in range(mult, cap + 1, mult):
        if n % t == 0:
            best = t
    assert best is not None, (n, cap, mult)
    return best


def _dot(a, b):
    return jnp.dot(a, b, preferred_element_type=F32)


def _dot_nt(a, b):
    return lax.dot_general(a, b, (((1,), (1,)), ((), ())), preferred_element_type=F32)


def _dot_tn(a, b):
    return lax.dot_general(a, b, (((0,), (0,)), ((), ())), preferred_element_type=F32)


def _rows8(x):
    t, c = x.shape
    return jnp.sum(x.reshape(t // SUBLANES, SUBLANES, c), axis=0)


def _rstd(x):
    return lax.rsqrt(jnp.mean(x * x, axis=-1, keepdims=True) + RMS_EPS)


def _ffn_fwd(h, g, wg, wu, wd, tm, name):
    lp, d = h.shape
    ns, _, fs = wg.shape

    def body(h_ref, g_ref, wg_ref, wu_ref, wd_ref, out_ref, a_ref, b_ref, u_ref, acc_ref):
        j = pl.program_id(1)

        @pl.when(j == 0)
        def _():
            hh = h_ref[...]
            u_ref[...] = (hh * _rstd(hh) * g_ref[...]).astype(BF)
            acc_ref[...] = jnp.zeros_like(acc_ref)

        u = u_ref[...]
        a = _dot(u, wg_ref[...])
        b = _dot(u, wu_ref[...])
        a_ref[...] = a.astype(BF)
        b_ref[...] = b.astype(BF)
        hid = (a * jax.nn.sigmoid(a) * b).astype(BF)
        acc_ref[...] += _dot(hid, wd_ref[...])

        @pl.when(j == ns - 1)
        def _():
            out_ref[...] = h_ref[...] + 0.5 * acc_ref[...]

    row = pl.BlockSpec((tm, d), lambda i, j: (i, 0))
    act = pl.BlockSpec((None, tm, fs), lambda i, j: (j, i, 0))
    return pl.pallas_call(
        body, grid=(lp // tm, ns),
        in_specs=[row, pl.BlockSpec((1, d), lambda i, j: (0, 0)),
                  pl.BlockSpec((None, d, fs), lambda i, j: (j, 0, 0)),
                  pl.BlockSpec((None, d, fs), lambda i, j: (j, 0, 0)),
                  pl.BlockSpec((None, fs, d), lambda i, j: (j, 0, 0))],
        out_specs=[row, act, act, row],
        out_shape=[SDS((lp, d), F32), SDS((ns, lp, fs), BF), SDS((ns, lp, fs), BF), SDS((lp, d), BF)],
        scratch_shapes=[pltpu.VMEM((tm, d), F32)],
        compiler_params=_params(56, 2), name=name)(h, g, wg, wu, wd)


def _ffn_bwd_dx(dob, a, b, wg, wu, wd, tm, name):
    lp, d = dob.shape
    ns, _, fs = wg.shape

    def body(do_ref, a_ref, b_ref, wg_ref, wu_ref, wd_ref, du_ref, da_ref, db_ref, hid_ref):
        j = pl.program_id(1)

        @pl.when(j == 0)
        def _():
            du_ref[...] = jnp.zeros_like(du_ref)

        dhid = _dot_nt(do_ref[...], wd_ref[...])
        av = a_ref[...].astype(F32)
        bv = b_ref[...].astype(F32)
        sig = jax.nn.sigmoid(av)
        sil = av * sig
        dbv = (dhid * sil).astype(BF)
        dav = (dhid * bv * (sig * (1.0 + av * (1.0 - sig)))).astype(BF)
        hid_ref[...] = (sil * bv).astype(BF)
        da_ref[...] = dav
        db_ref[...] = dbv
        du_ref[...] += _dot_nt(dav, wg_ref[...]) + _dot_nt(dbv, wu_ref[...])

    row = pl.BlockSpec((tm, d), lambda i, j: (i, 0))
    act = pl.BlockSpec((None, tm, fs), lambda i, j: (j, i, 0))
    return pl.pallas_call(
        body, grid=(lp // tm, ns),
        in_specs=[row, act, act,
                  pl.BlockSpec((None, d, fs), lambda i, j: (j, 0, 0)),
                  pl.BlockSpec((None, d, fs), lambda i, j: (j, 0, 0)),
                  pl.BlockSpec((None, fs, d), lambda i, j: (j, 0, 0))],
        out_specs=[row, act, act, act],
        out_shape=[SDS((lp, d), F32)] + [SDS((ns, lp, fs), BF)] * 3,
        compiler_params=_params(56, 2), name=name)(dob, a, b, wg, wu, wd)


def _rms_bwd(du, h, g, dres, bscale, tm, name):
    lp, d = h.shape

    def body(du_ref, h_ref, g_ref, dres_ref, dh_ref, dhb_ref, dg_ref):
        @pl.when(pl.program_id(0) == 0)
        def _():
            dg_ref[...] = jnp.zeros_like(dg_ref)

        hh = h_ref[...]
        r = _rstd(hh)
        xhat = hh * r
        duv = du_ref[...]
        dg_ref[...] += _rows8(duv * xhat)
        dxh = duv * g_ref[...]
        dh = dres_ref[...] + r * (dxh - xhat * jnp.mean(dxh * xhat, axis=-1, keepdims=True))
        dh_ref[...] = dh
        dhb_ref[...] = (bscale * dh).astype(BF)

    row = pl.BlockSpec((tm, d), lambda i: (i, 0))
    return pl.pallas_call(
        body, grid=(lp // tm,),
        in_specs=[row, row, pl.BlockSpec((1, d), lambda i: (0, 0)), row],
        out_specs=[row, row, pl.BlockSpec((SUBLANES, d), lambda i: (0, 0))],
        out_shape=[SDS((lp, d), F32), SDS((lp, d), BF), SDS((SUBLANES, d), F32)],
        compiler_params=_params(48, 1), name=name)(du, h, g, dres)


def _matmul_tn(a, b, tm, tn, name):
    a_b, b_b = a.ndim == 3, b.ndim == 3
    ns = a.shape[0] if a_b else (b.shape[0] if b_b else 1)
    l, m = a.shape[-2:]
    n = b.shape[-1]

    def body(a_ref, b_ref, o_ref):
        o_ref[...] = _dot_tn(a_ref[...], b_ref[...]).astype(o_ref.dtype)

    a_spec = (pl.BlockSpec((None, l, tm), lambda s, i, j: (s, 0, i)) if a_b
              else pl.BlockSpec((l, tm), lambda s, i, j: (0, i)))
    b_spec = (pl.BlockSpec((None, l, tn), lambda s, i, j: (s, 0, j)) if b_b
              else pl.BlockSpec((l, tn), lambda s, i, j: (0, j)))
    batched = a_b or b_b
    o_spec = (pl.BlockSpec((None, tm, tn), lambda s, i, j: (s, i, j)) if batched
              else pl.BlockSpec((tm, tn), lambda s, i, j: (i, j)))
    o_shape = SDS((ns, m, n), BF) if batched else SDS((m, n), BF)
    return pl.pallas_call(
        body, grid=(ns, m // tm, n // tn), in_specs=[a_spec, b_spec], out_specs=o_spec, out_shape=o_shape,
        compiler_params=_params(48, 3), name=name)(a, b)


def _matmul_nt(x, w, tm, tk, out_dtype, name):
    l, k = x.shape
    n = w.shape[0]
    nk = k // tk

    def body(x_ref, w_ref, o_ref, acc_ref):
        kk = pl.program_id(1)

        @pl.when(kk == 0)
        def _():
            acc_ref[...] = jnp.zeros_like(acc_ref)

        acc_ref[...] += _dot_nt(x_ref[...], w_ref[...])

        @pl.when(kk == nk - 1)
        def _():
            o_ref[...] = acc_ref[...].astype(o_ref.dtype)

    return pl.pallas_call(
        body, grid=(l // tm, nk),
        in_specs=[pl.BlockSpec((tm, tk), lambda i, kk: (i, kk)), pl.BlockSpec((n, tk), lambda i, kk: (0, kk))],
        out_specs=pl.BlockSpec((tm, n), lambda i, kk: (i, 0)),
        out_shape=SDS((l, n), out_dtype),
        scratch_shapes=[pltpu.VMEM((tm, n), F32)],
        compiler_params=_params(48, 2), name=name)(x, w)


def _norm_matmul(h, g, w, tm, tn, name):
    lp, d = h.shape
    n = w.shape[1]

    def body(h_ref, g_ref, w_ref, z_ref, u_ref):
        @pl.when(pl.program_id(1) == 0)
        def _():
            hh = h_ref[...]
            u_ref[...] = (hh * _rstd(hh) * g_ref[...]).astype(BF)

        z_ref[...] = _dot(u_ref[...], w_ref[...])

    row = pl.BlockSpec((tm, d), lambda i, j: (i, 0))
    return pl.pallas_call(
        body, grid=(lp // tm, n // tn),
        in_specs=[row, pl.BlockSpec((1, d), lambda i, j: (0, 0)), pl.BlockSpec((d, tn), lambda i, j: (0, j))],
        out_specs=[pl.BlockSpec((tm, tn), lambda i, j: (i, j)), row],
        out_shape=[SDS((lp, n), F32), SDS((lp, d), BF)],
        compiler_params=_params(48, 2), name=name)(h, g, w)


def _out_proj(h, pool_o, att_o, w_out, tm, name):
    lp, d = h.shape
    p = pool_o.shape[1]
    dm = w_out.shape[0]

    def body(h_ref, p_ref, a_ref, w_ref, o_ref):
        o_ref[...] = h_ref[...] + _dot(p_ref[...], w_ref[0:p, :]) + _dot(a_ref[...], w_ref[p:dm, :])

    row = pl.BlockSpec((tm, d), lambda i: (i, 0))
    return pl.pallas_call(
        body, grid=(lp // tm,),
        in_specs=[row, pl.BlockSpec((tm, p), lambda i: (i, 0)), pl.BlockSpec((tm, dm - p), lambda i: (i, 0)),
                  pl.BlockSpec((dm, d), lambda i: (0, 0))],
        out_specs=row, out_shape=SDS((lp, d), F32),
        compiler_params=_params(48, 1), name=name)(h, pool_o, att_o, w_out)


def _loss_head(y, tpad, row0, row1, tm, name):
    lp, d = y.shape

    def body(y_ref, t_ref, dy_ref, dob_ref, ls_ref):
        i = pl.program_id(0)

        @pl.when(i == 0)
        def _():
            ls_ref[...] = jnp.zeros_like(ls_ref)

        rows = i * tm + lax.broadcasted_iota(jnp.int32, (tm, d), 0)
        err = jnp.where((rows >= row0) & (rows < row1), y_ref[...] - t_ref[...], 0.0)
        dy = err * (1.0 / d)
        dy_ref[...] = dy
        dob_ref[...] = (0.5 * dy).astype(BF)
        sq = _rows8(err * err)
        acc = sq[:, 0:LANES]
        for c in range(1, d // LANES):
            acc = acc + sq[:, c * LANES:(c + 1) * LANES]
        ls_ref[...] += acc

    row = pl.BlockSpec((tm, d), lambda i: (i, 0))
    return pl.pallas_call(
        body, grid=(lp // tm,), in_specs=[row, row],
        out_specs=[row, row, pl.BlockSpec((SUBLANES, LANES), lambda i: (0, 0))],
        out_shape=[SDS((lp, d), F32), SDS((lp, d), BF), SDS((SUBLANES, LANES), F32)],
        compiler_params=_params(48, 1), name=name)(y, tpad)


def _window_select(levels, gidx):
    out = levels[-1]
    for k in range(len(levels) - 2, -1, -1):
        out = jnp.where(gidx == k, levels[k], out)
    return out


def _pool_window_mean_minus_id(x, gidx):
    rows = lax.broadcasted_iota(jnp.int32, x.shape, 0)
    levels = []
    s = x
    shift = 1
    while shift < POOL_WINDOWS[-1]:
        s = s + jnp.where(rows >= shift, pltpu.roll(s, shift, 0), 0.0)
        shift *= 2
        if shift in POOL_WINDOWS:
            levels.append(s)
    win = _window_select(levels, gidx)
    cnt = jnp.minimum(rows + 1, _window_select(list(POOL_WINDOWS), gidx)).astype(F32)
    return win / cnt - x, cnt


def _pool_window_transpose(dy, cnt, gidx):
    lp = dy.shape[0]
    rows = lax.broadcasted_iota(jnp.int32, dy.shape, 0)
    levels = []
    s = dy / cnt
    shift = 1
    while shift < POOL_WINDOWS[-1]:
        s = s + jnp.where(rows < lp - shift, pltpu.roll(s, lp - shift, 0), 0.0)
        shift *= 2
        if shift in POOL_WINDOWS:
            levels.append(s)
    return _window_select(levels, gidx) - dy


def _pool_fwd(z, pool_w, pool_scale, name):
    lp = z.shape[0]
    ng, gw, _ = pool_w.shape

    def body(p_ref, w_ref, s_ref, o_ref):
        pooled, _ = _pool_window_mean_minus_id(p_ref[...], pl.program_id(0))
        o_ref[...] = (_dot(pooled.astype(BF), w_ref[...]) * s_ref[...]).astype(BF)

    return pl.pallas_call(
        body, grid=(ng,),
        in_specs=[pl.BlockSpec((lp, gw), lambda g: (0, g)), pl.BlockSpec((None, gw, gw), lambda g: (g, 0, 0)),
                  pl.BlockSpec((1, gw), lambda g: (0, g))],
        out_specs=pl.BlockSpec((lp, gw), lambda g: (0, g)), out_shape=SDS((lp, ng * gw), BF),
        compiler_params=_params(48, 1), name=name)(z, pool_w, pool_scale)


def _pool_bwd(z, dmix, pool_w, pool_scale, name):
    lp = z.shape[0]
    ng, gw, _ = pool_w.shape

    def body(p_ref, d_ref, w_ref, s_ref, dz_ref, dw_ref, ds_ref):
        g = pl.program_id(0)
        pooled, cnt = _pool_window_mean_minus_id(p_ref[...], g)
        pooled_b = pooled.astype(BF)
        w = w_ref[...]
        mixed = _dot(pooled_b, w)
        dpo = d_ref[...].astype(F32)
        ds_ref[...] = _rows8(dpo * mixed)
        dmixed = (dpo * s_ref[...]).astype(BF)
        dw_ref[...] = _dot_tn(pooled_b, dmixed)
        dpooled = _dot_nt(dmixed, w)
        dz_ref[...] = _pool_window_transpose(dpooled, cnt, g).astype(BF)

    return pl.pallas_call(
        body, grid=(ng,),
        in_specs=[pl.BlockSpec((lp, gw), lambda g: (0, g)), pl.BlockSpec((lp, gw), lambda g: (0, g)),
                  pl.BlockSpec((None, gw, gw), lambda g: (g, 0, 0)), pl.BlockSpec((1, gw), lambda g: (0, g))],
        out_specs=[pl.BlockSpec((lp, gw), lambda g: (0, g)), pl.BlockSpec((None, gw, gw), lambda g: (g, 0, 0)),
                   pl.BlockSpec((SUBLANES, gw), lambda g: (0, g))],
        out_shape=[SDS((lp, ng * gw), BF), SDS((ng, gw, gw), F32), SDS((SUBLANES, ng * gw), F32)],
        compiler_params=_params(48, 1), name=name)(z, dmix, pool_w, pool_scale)


def _log_sigmoid(x):
    return jnp.minimum(x, 0.0) - jnp.log(1.0 + jnp.exp(-jnp.abs(x)))


def _fox_prep(z, bfp, fblk, name):
    lp = z.shape[0]
    nb = lp // LANES

    def body(f_ref, b_ref, cum_ref):
        r = lax.broadcasted_iota(jnp.int32, (LANES, LANES), 0)
        c = lax.broadcasted_iota(jnp.int32, (LANES, LANES), 1)
        tri = (r >= c).astype(F32)
        carry = jnp.zeros((1, LANES), F32)
        for blk in range(nb):
            sl = slice(blk * LANES, (blk + 1) * LANES)
            lf = _log_sigmoid(f_ref[sl, :] + b_ref[...])
            cb = jnp.dot(tri, lf, preferred_element_type=F32, precision=lax.Precision.HIGHEST) + carry
            cum_ref[sl, :] = cb
            carry = cb[LANES - 1:LANES, :]

    return pl.pallas_call(
        body, grid=(1,),
        in_specs=[pl.BlockSpec((lp, LANES), lambda i: (0, fblk)), pl.BlockSpec((1, LANES), lambda i: (0, 0))],
        out_specs=pl.BlockSpec((lp, LANES), lambda i: (0, 0)), out_shape=SDS((lp, LANES), F32),
        compiler_params=_params(32, 1), name=name)(z, bfp)


def _fox_bwd(z, bfp, dcum, fblk, name):
    lp = z.shape[0]
    nb = lp // LANES

    def body(f_ref, b_ref, dc_ref, dz_ref, db_ref):
        r = lax.broadcasted_iota(jnp.int32, (LANES, LANES), 0)
        c = lax.broadcasted_iota(jnp.int32, (LANES, LANES), 1)
        tri = (r <= c).astype(F32)
        carry = jnp.zeros((1, LANES), F32)
        acc = jnp.zeros((SUBLANES, LANES), F32)
        for blk in range(nb - 1, -1, -1):
            sl = slice(blk * LANES, (blk + 1) * LANES)
            dlf = jnp.dot(tri, dc_ref[sl, :], preferred_element_type=F32, precision=lax.Precision.HIGHEST) + carry
            carry = dlf[0:1, :]
            df = dlf * jax.nn.sigmoid(-(f_ref[sl, :] + b_ref[...]))
            dz_ref[sl, :] = df.astype(BF)
            acc = acc + _rows8(df)
        db_ref[...] = acc

    return pl.pallas_call(
        body, grid=(1,),
        in_specs=[pl.BlockSpec((lp, LANES), lambda i: (0, fblk)), pl.BlockSpec((1, LANES), lambda i: (0, 0)),
                  pl.BlockSpec((lp, LANES), lambda i: (0, 0))],
        out_specs=[pl.BlockSpec((lp, LANES), lambda i: (0, 0)), pl.BlockSpec((SUBLANES, LANES), lambda i: (0, 0))],
        out_shape=[SDS((lp, LANES), BF), SDS((SUBLANES, LANES), F32)],
        compiler_params=_params(32, 1), name=name)(z, bfp, dcum)


def _att_scores(q_ref, cum_ref, cumt_ref, qw_ref, kn_s, h, i, tq, lp):
    scale = 1.0 / (HEAD_DIM ** 0.5)
    q = q_ref[...]
    rq = _rstd(q)
    qhat = q * rq
    qn = (qhat * qw_ref[...]).astype(BF)
    s = _dot_nt(qn, kn_s[...]) * scale
    lane = lax.broadcasted_iota(jnp.int32, (tq, LANES), 1)
    cq = jnp.sum(jnp.where(lane == h, cum_ref[...], 0.0), axis=1, keepdims=True)
    ck = cumt_ref[pl.ds(h, 1), :]
    s = s + (cq - ck)
    qpos = i * tq + lax.broadcasted_iota(jnp.int32, (tq, lp), 0)
    kpos = lax.broadcasted_iota(jnp.int32, (tq, lp), 1)
    s = jnp.where(qpos >= kpos, s, NEG_BIG)
    e = jnp.exp(s - jnp.max(s, axis=1, keepdims=True))
    p = e / jnp.sum(e, axis=1, keepdims=True)
    return p, qn, qhat, rq


def _att_fwd(z, cum, cumt, qw, kw, n_heads, qblk0, tq, name):
    lp = z.shape[0]
    nh = n_heads

    def body(q_ref, k_ref, v_ref, cum_ref, cumt_ref, qw_ref, kw_ref, o_ref, kn_s, vb_s):
        h, i = pl.program_id(0), pl.program_id(1)

        @pl.when(i == 0)
        def _():
            k = k_ref[...]
            kn_s[...] = (k * _rstd(k) * kw_ref[...]).astype(BF)
            vb_s[...] = v_ref[...].astype(BF)

        p, _, _, _ = _att_scores(q_ref, cum_ref, cumt_ref, qw_ref, kn_s, h, i, tq, lp)
        o_ref[...] = _dot(p.astype(BF), vb_s[...]).astype(BF)

    vec = pl.BlockSpec((1, HEAD_DIM), lambda h, i: (0, 0))
    return pl.pallas_call(
        body, grid=(nh, lp // tq),
        in_specs=[pl.BlockSpec((tq, HEAD_DIM), lambda h, i: (i, qblk0 + h)),
                  pl.BlockSpec((lp, HEAD_DIM), lambda h, i: (0, qblk0 + nh + h)),
                  pl.BlockSpec((lp, HEAD_DIM), lambda h, i: (0, qblk0 + 2 * nh + h)),
                  pl.BlockSpec((tq, LANES), lambda h, i: (i, 0)),
                  pl.BlockSpec((nh, lp), lambda h, i: (0, 0)), vec, vec],
        out_specs=pl.BlockSpec((tq, HEAD_DIM), lambda h, i: (i, h)),
        out_shape=SDS((lp, nh * HEAD_DIM), BF),
        scratch_shapes=[pltpu.VMEM((lp, HEAD_DIM), BF), pltpu.VMEM((lp, HEAD_DIM), BF)],
        compiler_params=_params(48, 2), name=name)(z, z, z, cum, cumt, qw, kw)


def _att_bwd(z, cum, cumt, qw, kw, dmix, n_heads, qblk0, oblk0, tq, name):
    lp = z.shape[0]
    nh = n_heads
    nq = lp // tq
    scale = 1.0 / (HEAD_DIM ** 0.5)

    def body(q_ref, k_ref, v_ref, cum_ref, cumt_ref, qw_ref, kw_ref, do_ref,
             dq_ref, dk_ref, dv_ref, dcq_ref, dck_ref, dqw_ref, dkw_ref,
             kn_s, vb_s, dkn_s, dv_s, dck_s):
        h, i = pl.program_id(0), pl.program_id(1)

        @pl.when((h == 0) & (i == 0))
        def _():
            dqw_ref[...] = jnp.zeros_like(dqw_ref)
            dkw_ref[...] = jnp.zeros_like(dkw_ref)

        @pl.when(i == 0)
        def _():
            k = k_ref[...]
            kn_s[...] = (k * _rstd(k) * kw_ref[...]).astype(BF)
            vb_s[...] = v_ref[...].astype(BF)
            dkn_s[...] = jnp.zeros_like(dkn_s)
            dv_s[...] = jnp.zeros_like(dv_s)
            dck_s[...] = jnp.zeros_like(dck_s)

        p, qn, qhat, rq = _att_scores(q_ref, cum_ref, cumt_ref, qw_ref, kn_s, h, i, tq, lp)
        dob = do_ref[...]
        dp = _dot_nt(dob, vb_s[...])
        ds = p * (dp - jnp.sum(p * dp, axis=1, keepdims=True))
        dsb = ds.astype(BF)
        dv_s[...] += _dot_tn(p.astype(BF), dob)
        dkn_s[...] += _dot_tn(dsb, qn)
        dcq_ref[...] = jnp.sum(ds, axis=1, keepdims=True)
        dck_s[...] += jnp.sum(ds, axis=0, keepdims=True)
        dqn = _dot(dsb, kn_s[...]) * scale
        gq = dqn * qw_ref[...]
        dq_ref[...] = (rq * (gq - qhat * jnp.mean(gq * qhat, axis=-1, keepdims=True))).astype(BF)
        dqw_ref[...] += _rows8(dqn * qhat)

        @pl.when(i == nq - 1)
        def _():
            k = k_ref[...]
            rk = _rstd(k)
            khat = k * rk
            dkn = dkn_s[...] * scale
            gk = dkn * kw_ref[...]
            dk_ref[...] = (rk * (gk - khat * jnp.mean(gk * khat, axis=-1, keepdims=True))).astype(BF)
            dkw_ref[...] += _rows8(dkn * khat)
            dv_ref[...] = dv_s[...].astype(BF)
            dck_ref[...] = dck_s[...]

    vec = pl.BlockSpec((1, HEAD_DIM), lambda h, i: (0, 0))
    part = pl.BlockSpec((SUBLANES, LANES), lambda h, i: (0, 0))
    return pl.pallas_call(
        body, grid=(nh, nq),
        in_specs=[pl.BlockSpec((tq, HEAD_DIM), lambda h, i: (i, qblk0 + h)),
                  pl.BlockSpec((lp, HEAD_DIM), lambda h, i: (0, qblk0 + nh + h)),
                  pl.BlockSpec((lp, HEAD_DIM), lambda h, i: (0, qblk0 + 2 * nh + h)),
                  pl.BlockSpec((tq, LANES), lambda h, i: (i, 0)),
                  pl.BlockSpec((nh, lp), lambda h, i: (0, 0)), vec, vec,
                  pl.BlockSpec((tq, HEAD_DIM), lambda h, i: (i, oblk0 + h))],
        out_specs=[pl.BlockSpec((tq, HEAD_DIM), lambda h, i: (i, h)),
                   pl.BlockSpec((lp, HEAD_DIM), lambda h, i: (0, h)),
                   pl.BlockSpec((lp, HEAD_DIM), lambda h, i: (0, h)),
                   pl.BlockSpec((None, tq, 1), lambda h, i: (h, i, 0)),
                   pl.BlockSpec((None, 1, lp), lambda h, i: (h, 0, 0)),
                   part, part],
        out_shape=[SDS((lp, nh * HEAD_DIM), BF)] * 3
        + [SDS((nh, lp, 1), F32), SDS((nh, 1, lp), F32), SDS((SUBLANES, LANES), F32), SDS((SUBLANES, LANES), F32)],
        scratch_shapes=[pltpu.VMEM((lp, HEAD_DIM), BF), pltpu.VMEM((lp, HEAD_DIM), BF),
                        pltpu.VMEM((lp, HEAD_DIM), F32), pltpu.VMEM((lp, HEAD_DIM), F32),
                        pltpu.VMEM((1, lp), F32)],
        compiler_params=_params(56, 2), name=name)(z, z, z, cum, cumt, qw, kw, dmix)


def _adamw_math(w, g, m, v):
    m2 = ADAM_B1 * m + (1.0 - ADAM_B1) * g
    v2 = ADAM_B2 * v + (1.0 - ADAM_B2) * (g * g)
    m_hat = m2 / (1.0 - ADAM_B1 ** ADAM_STEP)
    v_hat = v2 / (1.0 - ADAM_B2 ** ADAM_STEP)
    delta = -ADAM_LR * (m_hat / (jnp.sqrt(v_hat) + ADAM_EPS) + ADAM_WD * w)
    return delta, m2, v2


def _adamw(g_in, w, m, v, name):
    r, c = w.shape
    partial_sum = g_in.ndim == 3
    tr = _largest_tile(r, 256, 16)

    def body(g_ref, w_ref, m_ref, v_ref, go_ref, d_ref, mo_ref, vo_ref):
        if partial_sum:
            g = g_ref[0].astype(F32)
            for k in range(1, g_in.shape[0]):
                g = g + g_ref[k].astype(F32)
        else:
            g = g_ref[...]
        delta, m2, v2 = _adamw_math(w_ref[...], g, m_ref[...], v_ref[...])
        go_ref[...] = g
        d_ref[...] = delta
        mo_ref[...] = m2
        vo_ref[...] = v2

    blk = pl.BlockSpec((tr, c), lambda i: (i, 0))
    g_spec = pl.BlockSpec((g_in.shape[0], tr, c), lambda i: (0, i, 0)) if partial_sum else blk
    return pl.pallas_call(
        body, grid=(r // tr,), in_specs=[g_spec, blk, blk, blk], out_specs=[blk] * 4,
        out_shape=[SDS((r, c), F32)] * 4, compiler_params=_params(40, 1), name=name)(g_in, w, m, v)


def _peer(x, y, c, k):
    return (1 - x if k & 4 else x, 1 - y if k & 2 else y, 1 - c if k & 1 else c)


def _exchange(arrs, scatter, name):
    n = len(arrs)

    def body(*refs):
        ins, outs = refs[:n], refs[n:2 * n]
        send_sems, recv_sems, local_sems = refs[2 * n:]
        x, y, c = lax.axis_index("x"), lax.axis_index("y"), lax.axis_index("c")
        me = 4 * x + 2 * y + c

        def src(t, dev):
            return ins[t].at[dev] if scatter else ins[t]

        def copy(t, k, arrival):
            px, py, pc = _peer(x, y, c, k)
            dev = 4 * px + 2 * py + pc
            return pltpu.make_async_remote_copy(
                src_ref=src(t, dev), dst_ref=outs[t].at[dev if arrival else me],
                send_sem=send_sems.at[t, k - 1], recv_sem=recv_sems.at[t, k - 1],
                device_id=(px, py, pc), device_id_type=pl.DeviceIdType.MESH)

        local = [pltpu.make_async_copy(src(t, me), outs[t].at[me], local_sems.at[t]) for t in range(n)]
        for cp in local:
            cp.start()
        pairs = [(t, k) for k in range(1, N_DEV) for t in range(n)]
        for t, k in pairs:
            copy(t, k, False).start()
        for cp in local:
            cp.wait()
        for t, k in pairs:
            copy(t, k, True).wait_recv()
        for t, k in pairs:
            copy(t, k, False).wait_send()

    out_shape = [SDS(a.shape if scatter else (N_DEV,) + a.shape, a.dtype) for a in arrs]
    anyspec = pl.BlockSpec(memory_space=pl.ANY)
    return pl.pallas_call(
        body, in_specs=[anyspec] * n, out_specs=[anyspec] * n, out_shape=out_shape,
        scratch_shapes=[pltpu.SemaphoreType.DMA((n, N_DEV - 1)), pltpu.SemaphoreType.DMA((n, N_DEV - 1)),
                        pltpu.SemaphoreType.DMA((n,))],
        name=name)(*arrs)


def _small_reduce(pack_g, meta_g, loss_scale, name):
    w = pack_g.shape[2]

    def body(p_ref, m_ref, tot_ref, meta_ref, loss_ref):
        acc = p_ref[0]
        macc = m_ref[0]
        for k in range(1, N_DEV):
            acc = acc + p_ref[k]
            macc = macc + m_ref[k]
        tot = jnp.sum(acc, axis=0, keepdims=True)
        tot_ref[...] = tot
        meta_ref[...] = macc
        loss_ref[...] = jnp.full((1, LANES), loss_scale * jnp.sum(tot[:, w - LANES:w]), F32)

    return pl.pallas_call(
        body, out_shape=[SDS((1, w), F32), SDS(meta_g.shape[1:], F32), SDS((1, LANES), F32)],
        compiler_params=pltpu.CompilerParams(vmem_limit_bytes=32 * MIB), name=name)(pack_g, meta_g)


def _local_step(x, target, meta, n1, wg1, wu1, wd1, nm, win, bfp, qw, kw, pw, ps, wout, n2, wg2, wu2, wd2, n_heads):
    s_len, d = x.shape
    n_meta = meta.shape[0]
    l = n_meta + s_len
    lp = -(-l // LANES) * LANES
    nz = win.shape[1]
    fs = wg1.shape[2]
    p_w = ps.shape[1]
    npb = p_w // LANES
    fblk = nz // LANES - 1
    tm = _largest_tile(lp, 544, 16)
    tq = _largest_tile(lp, 272, 16)
    te = _largest_tile(lp, 272, 16)
    tnz =_largest_tile(nz, 1408, LANES)
    tmd = _largest_tile(d, 512, LANES)

    zpad = jnp.zeros((lp - l, d), F32)
    h0 = jnp.concatenate([meta, x, zpad], axis=0)
    tpad = jnp.concatenate([jnp.zeros((n_meta, d), F32), target, zpad], axis=0)

    h1, a1, b1, u1 = _ffn_fwd(h0, n1, wg1, wu1, wd1, tm, "ffn1_fwd")
    z, u2 = _norm_matmul(h1, nm, win, tm, tnz, "mix_in")
    cum = _fox_prep(z, bfp, fblk, "fox_prep")
    cumt = cum[:, :n_heads].T
    pool_o = _pool_fwd(z, pw, ps, "pool_fwd")
    att_o = _att_fwd(z, cum, cumt, qw, kw, n_heads, npb, tq, "att_fwd")
    h2 = _out_proj(h1, pool_o, att_o, wout, tm, "out_proj")
    h3, a2, b2, u3 = _ffn_fwd(h2, n2, wg2, wu2, wd2, tm, "ffn2_fwd")
    dy, dob3, lsq = _loss_head(h3, tpad, n_meta, l, te, "loss_head")

    du3, da2, db2, hid2 = _ffn_bwd_dx(dob3, a2, b2, wg2, wu2, wd2, tm, "ffn2_bwd_dx")
    dh2, dh2b, dn2 = _rms_bwd(du3, h2, n2, dy, 1.0, te, "ffn2_rms_bwd")
    dwg2 = _matmul_tn(u3, da2, tmd, fs, "ffn2_dwg")
    dwu2 = _matmul_tn(u3, db2, tmd, fs, "ffn2_dwu")
    dwd2 = _matmul_tn(hid2, dob3, fs, d, "ffn2_dwd")

    dmix = _matmul_nt(dh2b, wout, tm, d, BF, "out_proj_bwd")
    tmp = _largest_tile(p_w, 512, LANES)
    dwout = jnp.concatenate([_matmul_tn(pool_o, dh2b, tmp, d, "dwout_pool"),
                             _matmul_tn(att_o, dh2b, tmp, d, "dwout_att")], axis=0)
    dzp, dpw, dps = _pool_bwd(z, dmix, pw, ps, "pool_bwd")
    dq, dk, dv, dcq, dck, dqw, dkw = _att_bwd(z, cum, cumt, qw, kw, dmix, n_heads, npb, npb, tq, "att_bwd")
    dcum = dcq[:, :, 0].T - dck[:, 0, :].T
    dcum = jnp.pad(dcum, ((0, 0), (0, LANES - n_heads)))
    dzf, dbf = _fox_bwd(z, bfp, dcum, fblk, "fox_bwd")
    dz = jnp.concatenate([dzp, dq, dk, dv, dzf], axis=1)
    du2 = _matmul_nt(dz, win, tm, tnz, F32, "mix_in_bwd")
    dh1, dob1, dnm = _rms_bwd(du2, h1, nm, dh2, 0.5, te, "mix_rms_bwd")
    dwin = _matmul_tn(u2, dz, tmd, tnz, "dwin")

    du1, da1, db1, hid1 = _ffn_bwd_dx(dob1, a1, b1, wg1, wu1, wd1, tm, "ffn1_bwd_dx")
    dh0, _, dn1 = _rms_bwd(du1, h0, n1, dh1, 1.0, te, "ffn1_rms_bwd")
    dwg1 = _matmul_tn(u1, da1, tmd, fs, "ffn1_dwg")
    dwu1 = _matmul_tn(u1, db1, tmd, fs, "ffn1_dwu")
    dwd1 = _matmul_tn(hid1, dob1, fs, d, "ffn1_dwd")

    big = dict(ffn1_w_gate=dwg1, ffn1_w_up=dwu1, ffn1_w_down=dwd1, w_in=dwin, pool_w=dpw, w_out=dwout,
               ffn2_w_gate=dwg2, ffn2_w_up=dwu2, ffn2_w_down=dwd2)
    small = [dn1, dnm, dn2, dps, dqw, dkw, dbf, lsq]
    return dh0[n_meta:l], dh0[:n_meta], big, small


_BIG = ("ffn1_w_gate", "ffn1_w_up", "ffn1_w_down", "w_in", "pool_w", "w_out", "ffn2_w_gate", "ffn2_w_up", "ffn2_w_down")
_SMALL = ("ffn1_norm", "mix_norm", "ffn2_norm", "pool_scale", "q_norm", "k_norm", "b_forget")
_ORDER = ("meta_tokens", "ffn1_norm", "ffn1_w_gate", "ffn1_w_up", "ffn1_w_down", "mix_norm", "w_in", "b_forget",
          "q_norm", "k_norm", "pool_w", "pool_scale", "w_out", "ffn2_norm", "ffn2_w_gate", "ffn2_w_up", "ffn2_w_down")


def _as2d(a):
    return a.reshape(-1, a.shape[-1])


def kernel(x, meta_tokens, ffn1_norm, ffn1_w_gate, ffn1_w_up, ffn1_w_down, mix_norm, w_in, b_forget, q_norm, k_norm, pool_w, pool_scale, w_out, ffn2_norm, ffn2_w_gate, ffn2_w_up, ffn2_w_down, loss_target, m_meta_tokens, m_ffn1_norm, m_ffn1_w_gate, m_ffn1_w_up, m_ffn1_w_down, m_mix_norm, m_w_in, m_b_forget, m_q_norm, m_k_norm, m_pool_w, m_pool_scale, m_w_out, m_ffn2_norm, m_ffn2_w_gate, m_ffn2_w_up, m_ffn2_w_down, v_meta_tokens, v_ffn1_norm, v_ffn1_w_gate, v_ffn1_w_up, v_ffn1_w_down, v_mix_norm, v_w_in, v_b_forget, v_q_norm, v_k_norm, v_pool_w, v_pool_scale, v_w_out, v_ffn2_norm, v_ffn2_w_gate, v_ffn2_w_up, v_ffn2_w_down):
    w = dict(meta_tokens=meta_tokens, ffn1_norm=ffn1_norm, ffn1_w_gate=ffn1_w_gate, ffn1_w_up=ffn1_w_up,
             ffn1_w_down=ffn1_w_down, mix_norm=mix_norm, w_in=w_in, b_forget=b_forget, q_norm=q_norm, k_norm=k_norm,
             pool_w=pool_w, pool_scale=pool_scale, w_out=w_out, ffn2_norm=ffn2_norm, ffn2_w_gate=ffn2_w_gate,
             ffn2_w_up=ffn2_w_up, ffn2_w_down=ffn2_w_down)
    m = dict(meta_tokens=m_meta_tokens, ffn1_norm=m_ffn1_norm, ffn1_w_gate=m_ffn1_w_gate, ffn1_w_up=m_ffn1_w_up,
             ffn1_w_down=m_ffn1_w_down, mix_norm=m_mix_norm, w_in=m_w_in, b_forget=m_b_forget, q_norm=m_q_norm,
             k_norm=m_k_norm, pool_w=m_pool_w, pool_scale=m_pool_scale, w_out=m_w_out, ffn2_norm=m_ffn2_norm,
             ffn2_w_gate=m_ffn2_w_gate, ffn2_w_up=m_ffn2_w_up, ffn2_w_down=m_ffn2_w_down)
    v = dict(meta_tokens=v_meta_tokens, ffn1_norm=v_ffn1_norm, ffn1_w_gate=v_ffn1_w_gate, ffn1_w_up=v_ffn1_w_up,
             ffn1_w_down=v_ffn1_w_down, mix_norm=v_mix_norm, w_in=v_w_in, b_forget=v_b_forget, q_norm=v_q_norm,
             k_norm=v_k_norm, pool_w=v_pool_w, pool_scale=v_pool_scale, w_out=v_w_out, ffn2_norm=v_ffn2_norm,
             ffn2_w_gate=v_ffn2_w_gate, ffn2_w_up=v_ffn2_w_up, ffn2_w_down=v_ffn2_w_down)

    d = x.shape[-1]
    n_heads = b_forget.shape[-1]
    me = 4 * lax.axis_index("x") + 2 * lax.axis_index("y") + lax.axis_index("c")

    shards = [w[k][0].astype(BF) for k in _BIG] + [meta_tokens]
    gathered = _exchange(shards, False, "gather_weights")
    full = dict(zip(_BIG, gathered[:-1]))
    meta_full = gathered[-1].transpose(1, 0, 2).reshape(meta_tokens.shape[0], d)
    d_in = N_DEV * w_in.shape[-1]
    win_full = full["w_in"].transpose(1, 0, 2).reshape(d, d_in)
    n_main = d_in - n_heads
    win_full = jnp.concatenate(
        [win_full[:, :n_main], jnp.pad(win_full[:, n_main:], ((0, 0), (0, LANES - n_heads)))], axis=1)
    gw = pool_w.shape[-1]
    pw_full = full["pool_w"].transpose(1, 0, 2, 3).reshape(pool_w.shape[1], gw, gw)
    wout_full = full["w_out"].reshape(-1, d)
    bfp = jnp.pad(b_forget, ((0, 0), (0, LANES - n_heads)))

    dx, dmeta, big, small = _local_step(
        x[0], loss_target[0], meta_full, ffn1_norm, full["ffn1_w_gate"], full["ffn1_w_up"], full["ffn1_w_down"],
        mix_norm, win_full, bfp, q_norm, k_norm, pw_full, pool_scale, wout_full,
        ffn2_norm, full["ffn2_w_gate"], full["ffn2_w_up"], full["ffn2_w_down"], n_heads)

    ng = pool_w.shape[1]
    slots = dict(big)
    slots["w_in"] = big["w_in"][:, :d_in].reshape(d, N_DEV, -1).transpose(1, 0, 2)
    slots["pool_w"] = big["pool_w"].astype(BF).reshape(ng, N_DEV, -1, gw).transpose(1, 0, 2, 3).reshape(N_DEV, -1, gw)
    slots["w_out"] = big["w_out"].reshape(N_DEV, -1, d)
    received = dict(zip(_BIG, _exchange([slots[k] for k in _BIG], True, "scatter_grads")))

    pack = jnp.concatenate(small, axis=1)
    pack_g, meta_g = _exchange([pack, dmeta], False, "gather_small")
    tot, dmeta_tot, loss_row = _small_reduce(pack_g, meta_g, 0.5 / d, "small_reduce")

    res = {}
    for k in _BIG:
        res[k] = _adamw(received[k], _as2d(w[k]), _as2d(m[k]), _as2d(v[k]), "adamw_" + k)
    mcols = meta_tokens.shape[1]
    g_meta = lax.dynamic_slice_in_dim(dmeta_tot, me * mcols, mcols, axis=1)
    res["meta_tokens"] = _adamw(g_meta, meta_tokens, m_meta_tokens, v_meta_tokens, "adamw_meta_tokens")

    def packed(src):
        return jnp.concatenate([src[k] for k in _SMALL[:-1]] + [jnp.pad(src["b_forget"], ((0, 0), (0, LANES - n_heads)))],
                               axis=1)

    wp = packed(w)
    sm = _adamw(tot[:, :wp.shape[1]], wp, packed(m), packed(v), "adamw_small")
    off = 0
    for k in _SMALL:
        width = w[k].shape[1]
        res[k] = tuple(o[:, off:off + width] for o in sm)
        off += width if k != "b_forget" else LANES

    outs = [loss_row[0, 0], dx[None]]
    for idx in range(4):
        outs += [res[k][idx].reshape(w[k].shape) for k in _ORDER]
    return tuple(outs)
```

```python
import functools

import jax
import jax.numpy as jnp
from jax import lax
from jax.experimental import pallas as pl
from jax.experimental.pallas import tpu as pltpu

F32 = jnp.float32
BF = jnp.bfloat16
SDS = jax.ShapeDtypeStruct

N_DEV = 8
LANES = 128
SUBLANES = 8
HEAD_DIM = 128
POOL_WINDOWS = (2, 4, 8, 16)
RMS_EPS = 1e-6
NEG_BIG = -1e30
MIB = 1024 * 1024

ADAM_LR = 0.001
ADAM_B1 = 0.9
ADAM_B2 = 0.999
ADAM_EPS = 1e-08
ADAM_WD = 0.01
ADAM_STEP = 10


class _Comm:
    def __init__(self, arrs, out_shape, sems, start, finish, aliases=None):
        self.arrs, self.out_shape, self.sems = list(arrs), list(out_shape), list(sems)
        self.start, self.finish, self.aliases = start, finish, dict(aliases or {})
        self.results = None


def _merge_comm(ops):
    ops = [op for op in ops if op is not None]
    if not ops:
        return None
    na, no, ns = [0], [0], [0]
    for op in ops:
        na.append(na[-1] + len(op.arrs))
        no.append(no[-1] + len(op.out_shape))
        ns.append(ns[-1] + len(op.sems))

    def parts(i, ins, outs, sems):
        return ins[na[i]:na[i + 1]], outs[no[i]:no[i + 1]], sems[ns[i]:ns[i + 1]]

    def start(ins, outs, sems):
        for i, op in enumerate(ops):
            op.start(*parts(i, ins, outs, sems))

    def finish(ins, outs, sems):
        for i, op in enumerate(ops):
            op.finish(*parts(i, ins, outs, sems))

    aliases = {}
    for i, op in enumerate(ops):
        for a, o in op.aliases.items():
            aliases[na[i] + a] = no[i] + o
    merged = _Comm([a for op in ops for a in op.arrs], [s for op in ops for s in op.out_shape],
                   [s for op in ops for s in op.sems], start, finish, aliases)
    merged.children = (ops, no)
    return merged


def _deliver(comm, results):
    comm.results = list(results)
    if hasattr(comm, "children"):
        ops, no = comm.children
        for i, op in enumerate(ops):
            _deliver(op, results[no[i]:no[i + 1]])


def _call(body, *, grid, in_specs, out_specs, out_shape, scratch_shapes=(), vmem_mib, name, comm=None):
    single = not isinstance(out_shape, (list, tuple))
    out_specs = [out_specs] if single else list(out_specs)
    out_shape = [out_shape] if single else list(out_shape)
    in_specs, scratch_shapes = list(in_specs), list(scratch_shapes)
    params = pltpu.CompilerParams(dimension_semantics=("arbitrary",) * len(grid), vmem_limit_bytes=vmem_mib * MIB)
    n_in, n_out, n_scr = len(in_specs), len(out_specs), len(scratch_shapes)

    def run(*args):
        if comm is None:
            res = pl.pallas_call(body, grid=grid, in_specs=in_specs, out_specs=out_specs, out_shape=out_shape,
                                 scratch_shapes=scratch_shapes, compiler_params=params, name=name)(*args)
            return res[0] if single else res
        ci, co = len(comm.arrs), len(comm.out_shape)

        def with_comm(*refs):
            ins, cins = refs[:n_in], refs[n_in:n_in + ci]
            o0 = n_in + ci
            outs, couts = refs[o0:o0 + n_out], refs[o0 + n_out:o0 + n_out + co]
            s0 = o0 + n_out + co
            scr, csems = refs[s0:s0 + n_scr], refs[s0 + n_scr:]
            ids = [pl.program_id(a) for a in range(len(grid))]
            first = functools.reduce(jnp.logical_and, [i == 0 for i in ids])
            last = functools.reduce(jnp.logical_and, [i == g - 1 for i, g in zip(ids, grid)])

            @pl.when(first)
            def _():
                comm.start(cins, couts, csems)

            body(*ins, *outs, *scr)

            @pl.when(last)
            def _():
                comm.finish(cins, couts, csems)

        anyspec = pl.BlockSpec(memory_space=pl.ANY)
        res = pl.pallas_call(
            with_comm, grid=grid, in_specs=in_specs + [anyspec] * ci, out_specs=out_specs + [anyspec] * co,
            out_shape=out_shape + comm.out_shape, scratch_shapes=scratch_shapes + comm.sems,
            input_output_aliases={n_in + a: n_out + o for a, o in comm.aliases.items()},
            compiler_params=params, name=name)(*args, *comm.arrs)
        _deliver(comm, res[n_out:])
        return res[0] if single else res[:n_out]

    return run


def _comm_alone(comm, name):
    def body(*refs):
        ci, co = len(comm.arrs), len(comm.out_shape)
        ins, outs, sems = refs[:ci], refs[ci:ci + co], refs[ci + co:]
        comm.start(ins, outs, sems)
        comm.finish(ins, outs, sems)

    anyspec = pl.BlockSpec(memory_space=pl.ANY)
    res = pl.pallas_call(
        body, in_specs=[anyspec] * len(comm.arrs), out_specs=[anyspec] * len(comm.out_shape),
        out_shape=comm.out_shape, scratch_shapes=comm.sems, input_output_aliases=comm.aliases, name=name)(*comm.arrs)
    _deliver(comm, res)


def _largest_tile(n, cap, mult):
    if n <= cap:
        return n
    best = None
    for t in range(mult, cap + 1, mult):
        if n % t == 0:
            best = t
    assert best is not None, (n, cap, mult)
    return best


def _dot(a, b):
    return jnp.dot(a, b, preferred_element_type=F32)


def _dot_nt(a, b):
    return lax.dot_general(a, b, (((1,), (1,)), ((), ())), preferred_element_type=F32)


def _dot_tn(a, b):
    return lax.dot_general(a, b, (((0,), (0,)), ((), ())), preferred_element_type=F32)


def _rows8(x):
    t, c = x.shape
    return jnp.sum(x.reshape(t // SUBLANES, SUBLANES, c), axis=0)


def _rstd(x):
    return lax.rsqrt(jnp.mean(x * x, axis=-1, keepdims=True) + RMS_EPS)


def _ffn_fwd(h, g, wg, wu, wd, tm, name, comm=None):
    lp, d = h.shape
    ns, _, fs = wg.shape

    def body(h_ref, g_ref, wg_ref, wu_ref, wd_ref, out_ref, a_ref, b_ref, u_ref, acc_ref):
        j = pl.program_id(1)

        @pl.when(j == 0)
        def _():
            hh = h_ref[...]
            u_ref[...] = (hh * _rstd(hh) * g_ref[...]).astype(BF)
            acc_ref[...] = jnp.zeros_like(acc_ref)

        u = u_ref[...]
        a = _dot(u, wg_ref[...])
        b = _dot(u, wu_ref[...])
        a_ref[...] = a.astype(BF)
        b_ref[...] = b.astype(BF)
        hid = (a * jax.nn.sigmoid(a) * b).astype(BF)
        acc_ref[...] += _dot(hid, wd_ref[...])

        @pl.when(j == ns - 1)
        def _():
            out_ref[...] = h_ref[...] + 0.5 * acc_ref[...]

    row = pl.BlockSpec((tm, d), lambda i, j: (i, 0))
    act = pl.BlockSpec((None, tm, fs), lambda i, j: (j, i, 0))
    return _call(
        body, grid=(lp // tm, ns),
        in_specs=[row, pl.BlockSpec((1, d), lambda i, j: (0, 0)),
                  pl.BlockSpec((None, d, fs), lambda i, j: (j, 0, 0)),
                  pl.BlockSpec((None, d, fs), lambda i, j: (j, 0, 0)),
                  pl.BlockSpec((None, fs, d), lambda i, j: (j, 0, 0))],
        out_specs=[row, act, act, row],
        out_shape=[SDS((lp, d), F32), SDS((ns, lp, fs), BF), SDS((ns, lp, fs), BF), SDS((lp, d), BF)],
        scratch_shapes=[pltpu.VMEM((tm, d), F32)],
        vmem_mib=56, name=name, comm=comm)(h, g, wg, wu, wd)


def _ffn_bwd_dx(dob, a, b, wg, wu, wd, tm, name, comm=None):
    lp, d = dob.shape
    ns, _, fs = wg.shape

    def body(do_ref, a_ref, b_ref, wg_ref, wu_ref, wd_ref, du_ref, da_ref, db_ref, hid_ref):
        j = pl.program_id(1)

        @pl.when(j == 0)
        def _():
            du_ref[...] = jnp.zeros_like(du_ref)

        dhid = _dot_nt(do_ref[...], wd_ref[...])
        av = a_ref[...].astype(F32)
        bv = b_ref[...].astype(F32)
        sig = jax.nn.sigmoid(av)
        sil = av * sig
        dbv = (dhid * sil).astype(BF)
        dav = (dhid * bv * (sig * (1.0 + av * (1.0 - sig)))).astype(BF)
        hid_ref[...] = (sil * bv).astype(BF)
        da_ref[...] = dav
        db_ref[...] = dbv
        du_ref[...] += _dot_nt(dav, wg_ref[...]) + _dot_nt(dbv, wu_ref[...])

    row = pl.BlockSpec((tm, d), lambda i, j: (i, 0))
    act = pl.BlockSpec((None, tm, fs), lambda i, j: (j, i, 0))
    return _call(
        body, grid=(lp // tm, ns),
        in_specs=[row, act, act,
                  pl.BlockSpec((None, d, fs), lambda i, j: (j, 0, 0)),
                  pl.BlockSpec((None, d, fs), lambda i, j: (j, 0, 0)),
                  pl.BlockSpec((None, fs, d), lambda i, j: (j, 0, 0))],
        out_specs=[row, act, act, act],
        out_shape=[SDS((lp, d), F32)] + [SDS((ns, lp, fs), BF)] * 3,
        vmem_mib=56, name=name, comm=comm)(dob, a, b, wg, wu, wd)


def _rms_bwd(du, h, g, dres, bscale, tm, name, comm=None):
    lp, d = h.shape

    def body(du_ref, h_ref, g_ref, dres_ref, dh_ref, dhb_ref, dg_ref):
        @pl.when(pl.program_id(0) == 0)
        def _():
            dg_ref[...] = jnp.zeros_like(dg_ref)

        hh = h_ref[...]
        r = _rstd(hh)
        xhat = hh * r
        duv = du_ref[...]
        dg_ref[...] += _rows8(duv * xhat)
        dxh = duv * g_ref[...]
        dh = dres_ref[...] + r * (dxh - xhat * jnp.mean(dxh * xhat, axis=-1, keepdims=True))
        dh_ref[...] = dh
        dhb_ref[...] = (bscale * dh).astype(BF)

    row = pl.BlockSpec((tm, d), lambda i: (i, 0))
    return _call(
        body, grid=(lp // tm,),
        in_specs=[row, row, pl.BlockSpec((1, d), lambda i: (0, 0)), row],
        out_specs=[row, row, pl.BlockSpec((SUBLANES, d), lambda i: (0, 0))],
        out_shape=[SDS((lp, d), F32), SDS((lp, d), BF), SDS((SUBLANES, d), F32)],
        vmem_mib=48, name=name, comm=comm)(du, h, g, dres)


def _matmul_tn(a, b, tm, tn, name, comm=None):
    a_b, b_b = a.ndim == 3, b.ndim == 3
    ns = a.shape[0] if a_b else (b.shape[0] if b_b else 1)
    l, m = a.shape[-2:]
    n = b.shape[-1]

    def body(a_ref, b_ref, o_ref):
        o_ref[...] = _dot_tn(a_ref[...], b_ref[...]).astype(o_ref.dtype)

    a_spec = (pl.BlockSpec((None, l, tm), lambda s, i, j: (s, 0, i)) if a_b
              else pl.BlockSpec((l, tm), lambda s, i, j: (0, i)))
    b_spec = (pl.BlockSpec((None, l, tn), lambda s, i, j: (s, 0, j)) if b_b
              else pl.BlockSpec((l, tn), lambda s, i, j: (0, j)))
    batched = a_b or b_b
    o_spec = (pl.BlockSpec((None, tm, tn), lambda s, i, j: (s, i, j)) if batched
              else pl.BlockSpec((tm, tn), lambda s, i, j: (i, j)))
    o_shape = SDS((ns, m, n), BF) if batched else SDS((m, n), BF)
    return _call(
        body, grid=(ns, m // tm, n // tn), in_specs=[a_spec, b_spec], out_specs=o_spec, out_shape=o_shape,
        vmem_mib=48, name=name, comm=comm)(a, b)


def _matmul_nt(x, w, tm, tk, out_dtype, name, comm=None):
    l, k = x.shape
    n = w.shape[0]
    nk = k // tk

    def body(x_ref, w_ref, o_ref, acc_ref):
        kk = pl.program_id(1)

        @pl.when(kk == 0)
        def _():
            acc_ref[...] = jnp.zeros_like(acc_ref)

        acc_ref[...] += _dot_nt(x_ref[...], w_ref[...])

        @pl.when(kk == nk - 1)
        def _():
            o_ref[...] = acc_ref[...].astype(o_ref.dtype)

    return _call(
        body, grid=(l // tm, nk),
        in_specs=[pl.BlockSpec((tm, tk), lambda i, kk: (i, kk)), pl.BlockSpec((n, tk), lambda i, kk: (0, kk))],
        out_specs=pl.BlockSpec((tm, n), lambda i, kk: (i, 0)),
        out_shape=SDS((l, n), out_dtype),
        scratch_shapes=[pltpu.VMEM((tm, n), F32)],
        vmem_mib=48, name=name, comm=comm)(x, w)


def _norm_matmul(h, g, w, tm, tn, name, comm=None):
    lp, d = h.shape
    n = w.shape[1]

    def body(h_ref, g_ref, w_ref, z_ref, u_ref):
        @pl.when(pl.program_id(1) == 0)
        def _():
            hh = h_ref[...]
            u_ref[...] = (hh * _rstd(hh) * g_ref[...]).astype(BF)

        z_ref[...] = _dot(u_ref[...], w_ref[...])

    row = pl.BlockSpec((tm, d), lambda i, j: (i, 0))
    return _call(
        body, grid=(lp // tm, n // tn),
        in_specs=[row, pl.BlockSpec((1, d), lambda i, j: (0, 0)), pl.BlockSpec((d, tn), lambda i, j: (0, j))],
        out_specs=[pl.BlockSpec((tm, tn), lambda i, j: (i, j)), row],
        out_shape=[SDS((lp, n), F32), SDS((lp, d), BF)],
        vmem_mib=48, name=name, comm=comm)(h, g, w)


def _out_proj(h, pool_o, att_o, w_out, tm, name, comm=None):
    lp, d = h.shape
    p = pool_o.shape[1]
    dm = w_out.shape[0]

    def body(h_ref, p_ref, a_ref, w_ref, o_ref):
        o_ref[...] = h_ref[...] + _dot(p_ref[...], w_ref[0:p, :]) + _dot(a_ref[...], w_ref[p:dm, :])

    row = pl.BlockSpec((tm, d), lambda i: (i, 0))
    return _call(
        body, grid=(lp // tm,),
        in_specs=[row, pl.BlockSpec((tm, p), lambda i: (i, 0)), pl.BlockSpec((tm, dm - p), lambda i: (i, 0)),
                  pl.BlockSpec((dm, d), lambda i: (0, 0))],
        out_specs=row, out_shape=SDS((lp, d), F32),
        vmem_mib=48, name=name, comm=comm)(h, pool_o, att_o, w_out)


def _loss_head(y, tpad, row0, row1, tm, name, comm=None):
    lp, d = y.shape

    def body(y_ref, t_ref, dy_ref, dob_ref, ls_ref):
        i = pl.program_id(0)

        @pl.when(i == 0)
        def _():
            ls_ref[...] = jnp.zeros_like(ls_ref)

        rows = i * tm + lax.broadcasted_iota(jnp.int32, (tm, d), 0)
        err = jnp.where((rows >= row0) & (rows < row1), y_ref[...] - t_ref[...], 0.0)
        dy = err * (1.0 / d)
        dy_ref[...] = dy
        dob_ref[...] = (0.5 * dy).astype(BF)
        sq = _rows8(err * err)
        acc = sq[:, 0:LANES]
        for c in range(1, d // LANES):
            acc = acc + sq[:, c * LANES:(c + 1) * LANES]
        ls_ref[...] += acc

    row = pl.BlockSpec((tm, d), lambda i: (i, 0))
    return _call(
        body, grid=(lp // tm,), in_specs=[row, row],
        out_specs=[row, row, pl.BlockSpec((SUBLANES, LANES), lambda i: (0, 0))],
        out_shape=[SDS((lp, d), F32), SDS((lp, d), BF), SDS((SUBLANES, LANES), F32)],
        vmem_mib=48, name=name, comm=comm)(y, tpad)


def _window_select(levels, gidx):
    out = levels[-1]
    for k in range(len(levels) - 2, -1, -1):
        out = jnp.where(gidx == k, levels[k], out)
    return out


def _pool_window_mean_minus_id(x, gidx):
    rows = lax.broadcasted_iota(jnp.int32, x.shape, 0)
    levels = []
    s = x
    shift = 1
    while shift < POOL_WINDOWS[-1]:
        s = s + jnp.where(rows >= shift, pltpu.roll(s, shift, 0), 0.0)
        shift *= 2
        if shift in POOL_WINDOWS:
            levels.append(s)
    win = _window_select(levels, gidx)
    cnt = jnp.minimum(rows + 1, _window_select(list(POOL_WINDOWS), gidx)).astype(F32)
    return win / cnt - x, cnt


def _pool_window_transpose(dy, cnt, gidx):
    lp = dy.shape[0]
    rows = lax.broadcasted_iota(jnp.int32, dy.shape, 0)
    levels = []
    s = dy / cnt
    shift = 1
    while shift < POOL_WINDOWS[-1]:
        s = s + jnp.where(rows < lp - shift, pltpu.roll(s, lp - shift, 0), 0.0)
        shift *= 2
        if shift in POOL_WINDOWS:
            levels.append(s)
    return _window_select(levels, gidx) - dy


def _pool_fwd(z, pool_w, pool_scale, name, comm=None):
    lp = z.shape[0]
    ng, gw, _ = pool_w.shape

    def body(p_ref, w_ref, s_ref, o_ref):
        pooled, _ = _pool_window_mean_minus_id(p_ref[...], pl.program_id(0))
        o_ref[...] = (_dot(pooled.astype(BF), w_ref[...]) * s_ref[...]).astype(BF)

    return _call(
        body, grid=(ng,),
        in_specs=[pl.BlockSpec((lp, gw), lambda g: (0, g)), pl.BlockSpec((None, gw, gw), lambda g: (g, 0, 0)),
                  pl.BlockSpec((1, gw), lambda g: (0, g))],
        out_specs=pl.BlockSpec((lp, gw), lambda g: (0, g)), out_shape=SDS((lp, ng * gw), BF),
        vmem_mib=48, name=name, comm=comm)(z, pool_w, pool_scale)


def _pool_bwd(z, dmix, pool_w, pool_scale, name, comm=None):
    lp = z.shape[0]
    ng, gw, _ = pool_w.shape

    def body(p_ref, d_ref, w_ref, s_ref, dz_ref, dw_ref, ds_ref):
        g = pl.program_id(0)
        pooled, cnt = _pool_window_mean_minus_id(p_ref[...], g)
        pooled_b = pooled.astype(BF)
        w = w_ref[...]
        mixed = _dot(pooled_b, w)
        dpo = d_ref[...].astype(F32)
        ds_ref[...] = _rows8(dpo * mixed)
        dmixed = (dpo * s_ref[...]).astype(BF)
        dw_ref[...] = _dot_tn(pooled_b, dmixed)
        dpooled = _dot_nt(dmixed, w)
        dz_ref[...] = _pool_window_transpose(dpooled, cnt, g).astype(BF)

    return _call(
        body, grid=(ng,),
        in_specs=[pl.BlockSpec((lp, gw), lambda g: (0, g)), pl.BlockSpec((lp, gw), lambda g: (0, g)),
                  pl.BlockSpec((None, gw, gw), lambda g: (g, 0, 0)), pl.BlockSpec((1, gw), lambda g: (0, g))],
        out_specs=[pl.BlockSpec((lp, gw), lambda g: (0, g)), pl.BlockSpec((None, gw, gw), lambda g: (g, 0, 0)),
                   pl.BlockSpec((SUBLANES, gw), lambda g: (0, g))],
        out_shape=[SDS((lp, ng * gw), BF), SDS((ng, gw, gw), F32), SDS((SUBLANES, ng * gw), F32)],
        vmem_mib=48, name=name, comm=comm)(z, dmix, pool_w, pool_scale)


def _log_sigmoid(x):
    return jnp.minimum(x, 0.0) - jnp.log(1.0 + jnp.exp(-jnp.abs(x)))


def _fox_prep(z, bfp, fblk, name, comm=None):
    lp = z.shape[0]
    nb = lp // LANES

    def body(f_ref, b_ref, cum_ref):
        r = lax.broadcasted_iota(jnp.int32, (LANES, LANES), 0)
        c = lax.broadcasted_iota(jnp.int32, (LANES, LANES), 1)
        tri = (r >= c).astype(F32)
        carry = jnp.zeros((1, LANES), F32)
        for blk in range(nb):
            sl = slice(blk * LANES, (blk + 1) * LANES)
            lf = _log_sigmoid(f_ref[sl, :] + b_ref[...])
            cb = jnp.dot(tri, lf, preferred_element_type=F32, precision=lax.Precision.HIGHEST) + carry
            cum_ref[sl, :] = cb
            carry = cb[LANES - 1:LANES, :]

    return _call(
        body, grid=(1,),
        in_specs=[pl.BlockSpec((lp, LANES), lambda i: (0, fblk)), pl.BlockSpec((1, LANES), lambda i: (0, 0))],
        out_specs=pl.BlockSpec((lp, LANES), lambda i: (0, 0)), out_shape=SDS((lp, LANES), F32),
        vmem_mib=32, name=name, comm=comm)(z, bfp)


def _fox_bwd(z, bfp, dcum, fblk, name, comm=None):
    lp = z.shape[0]
    nb = lp // LANES

    def body(f_ref, b_ref, dc_ref, dz_ref, db_ref):
        r = lax.broadcasted_iota(jnp.int32, (LANES, LANES), 0)
        c = lax.broadcasted_iota(jnp.int32, (LANES, LANES), 1)
        tri = (r <= c).astype(F32)
        carry = jnp.zeros((1, LANES), F32)
        acc = jnp.zeros((SUBLANES, LANES), F32)
        for blk in range(nb - 1, -1, -1):
            sl = slice(blk * LANES, (blk + 1) * LANES)
            dlf = jnp.dot(tri, dc_ref[sl, :], preferred_element_type=F32, precision=lax.Precision.HIGHEST) + carry
            carry = dlf[0:1, :]
            df = dlf * jax.nn.sigmoid(-(f_ref[sl, :] + b_ref[...]))
            dz_ref[sl, :] = df.astype(BF)
            acc = acc + _rows8(df)
        db_ref[...] = acc

    return _call(
        body, grid=(1,),
        in_specs=[pl.BlockSpec((lp, LANES), lambda i: (0, fblk)), pl.BlockSpec((1, LANES), lambda i: (0, 0)),
                  pl.BlockSpec((lp, LANES), lambda i: (0, 0))],
        out_specs=[pl.BlockSpec((lp, LANES), lambda i: (0, 0)), pl.BlockSpec((SUBLANES, LANES), lambda i: (0, 0))],
        out_shape=[SDS((lp, LANES), BF), SDS((SUBLANES, LANES), F32)],
        vmem_mib=32, name=name, comm=comm)(z, bfp, dcum)


def _att_scores(q_ref, cum_ref, cumt_ref, qw_ref, kn_s, h, i, tq, lp):
    scale = 1.0 / (HEAD_DIM ** 0.5)
    q = q_ref[...]
    rq = _rstd(q)
    qhat = q * rq
    qn = (qhat * qw_ref[...]).astype(BF)
    s = _dot_nt(qn, kn_s[...]) * scale
    lane = lax.broadcasted_iota(jnp.int32, (tq, LANES), 1)
    cq = jnp.sum(jnp.where(lane == h, cum_ref[...], 0.0), axis=1, keepdims=True)
    ck = cumt_ref[pl.ds(h, 1), :]
    s = s + (cq - ck)
    qpos = i * tq + lax.broadcasted_iota(jnp.int32, (tq, lp), 0)
    kpos = lax.broadcasted_iota(jnp.int32, (tq, lp), 1)
    s = jnp.where(qpos >= kpos, s, NEG_BIG)
    e = jnp.exp(s - jnp.max(s, axis=1, keepdims=True))
    p = e / jnp.sum(e, axis=1, keepdims=True)
    return p, qn, qhat, rq


def _att_fwd(z, cum, cumt, qw, kw, n_heads, qblk0, tq, name, comm=None):
    lp = z.shape[0]
    nh = n_heads

    def body(q_ref, k_ref, v_ref, cum_ref, cumt_ref, qw_ref, kw_ref, o_ref, kn_s, vb_s):
        h, i = pl.program_id(0), pl.program_id(1)

        @pl.when(i == 0)
        def _():
            k = k_ref[...]
            kn_s[...] = (k * _rstd(k) * kw_ref[...]).astype(BF)
            vb_s[...] = v_ref[...].astype(BF)

        p, _, _, _ = _att_scores(q_ref, cum_ref, cumt_ref, qw_ref, kn_s, h, i, tq, lp)
        o_ref[...] = _dot(p.astype(BF), vb_s[...]).astype(BF)

    vec = pl.BlockSpec((1, HEAD_DIM), lambda h, i: (0, 0))
    return _call(
        body, grid=(nh, lp // tq),
        in_specs=[pl.BlockSpec((tq, HEAD_DIM), lambda h, i: (i, qblk0 + h)),
                  pl.BlockSpec((lp, HEAD_DIM), lambda h, i: (0, qblk0 + nh + h)),
                  pl.BlockSpec((lp, HEAD_DIM), lambda h, i: (0, qblk0 + 2 * nh + h)),
                  pl.BlockSpec((tq, LANES), lambda h, i: (i, 0)),
                  pl.BlockSpec((nh, lp), lambda h, i: (0, 0)), vec, vec],
        out_specs=pl.BlockSpec((tq, HEAD_DIM), lambda h, i: (i, h)),
        out_shape=SDS((lp, nh * HEAD_DIM), BF),
        scratch_shapes=[pltpu.VMEM((lp, HEAD_DIM), BF), pltpu.VMEM((lp, HEAD_DIM), BF)],
        vmem_mib=48, name=name, comm=comm)(z, z, z, cum, cumt, qw, kw)


def _att_bwd(z, cum, cumt, qw, kw, dmix, n_heads, qblk0, oblk0, tq, name, comm=None):
    lp = z.shape[0]
    nh = n_heads
    nq = lp // tq
    scale = 1.0 / (HEAD_DIM ** 0.5)

    def body(q_ref, k_ref, v_ref, cum_ref, cumt_ref, qw_ref, kw_ref, do_ref,
             dq_ref, dk_ref, dv_ref, dcq_ref, dck_ref, dqw_ref, dkw_ref,
             kn_s, vb_s, dkn_s, dv_s, dck_s):
        h, i = pl.program_id(0), pl.program_id(1)

        @pl.when((h == 0) & (i == 0))
        def _():
            dqw_ref[...] = jnp.zeros_like(dqw_ref)
            dkw_ref[...] = jnp.zeros_like(dkw_ref)

        @pl.when(i == 0)
        def _():
            k = k_ref[...]
            kn_s[...] = (k * _rstd(k) * kw_ref[...]).astype(BF)
            vb_s[...] = v_ref[...].astype(BF)
            dkn_s[...] = jnp.zeros_like(dkn_s)
            dv_s[...] = jnp.zeros_like(dv_s)
            dck_s[...] = jnp.zeros_like(dck_s)

        p, qn, qhat, rq = _att_scores(q_ref, cum_ref, cumt_ref, qw_ref, kn_s, h, i, tq, lp)
        dob = do_ref[...]
        dp = _dot_nt(dob, vb_s[...])
        ds = p * (dp - jnp.sum(p * dp, axis=1, keepdims=True))
        dsb = ds.astype(BF)
        dv_s[...] += _dot_tn(p.astype(BF), dob)
        dkn_s[...] += _dot_tn(dsb, qn)
        dcq_ref[...] = jnp.sum(ds, axis=1, keepdims=True)
        dck_s[...] += jnp.sum(ds, axis=0, keepdims=True)
        dqn = _dot(dsb, kn_s[...]) * scale
        gq = dqn * qw_ref[...]
        dq_ref[...] = (rq * (gq - qhat * jnp.mean(gq * qhat, axis=-1, keepdims=True))).astype(BF)
        dqw_ref[...] += _rows8(dqn * qhat)

        @pl.when(i == nq - 1)
        def _():
            k = k_ref[...]
            rk = _rstd(k)
            khat = k * rk
            dkn = dkn_s[...] * scale
            gk = dkn * kw_ref[...]
            dk_ref[...] = (rk * (gk - khat * jnp.mean(gk * khat, axis=-1, keepdims=True))).astype(BF)
            dkw_ref[...] += _rows8(dkn * khat)
            dv_ref[...] = dv_s[...].astype(BF)
            dck_ref[...] = dck_s[...]

    vec = pl.BlockSpec((1, HEAD_DIM), lambda h, i: (0, 0))
    part = pl.BlockSpec((SUBLANES, LANES), lambda h, i: (0, 0))
    return _call(
        body, grid=(nh, nq),
        in_specs=[pl.BlockSpec((tq, HEAD_DIM), lambda h, i: (i, qblk0 + h)),
                  pl.BlockSpec((lp, HEAD_DIM), lambda h, i: (0, qblk0 + nh + h)),
                  pl.BlockSpec((lp, HEAD_DIM), lambda h, i: (0, qblk0 + 2 * nh + h)),
                  pl.BlockSpec((tq, LANES), lambda h, i: (i, 0)),
                  pl.BlockSpec((nh, lp), lambda h, i: (0, 0)), vec, vec,
                  pl.BlockSpec((tq, HEAD_DIM), lambda h, i: (i, oblk0 + h))],
        out_specs=[pl.BlockSpec((tq, HEAD_DIM), lambda h, i: (i, h)),
                   pl.BlockSpec((lp, HEAD_DIM), lambda h, i: (0, h)),
                   pl.BlockSpec((lp, HEAD_DIM), lambda h, i: (0, h)),
                   pl.BlockSpec((None, tq, 1), lambda h, i: (h, i, 0)),
                   pl.BlockSpec((None, 1, lp), lambda h, i: (h, 0, 0)),
                   part, part],
        out_shape=[SDS((lp, nh * HEAD_DIM), BF)] * 3
        + [SDS((nh, lp, 1), F32), SDS((nh, 1, lp), F32), SDS((SUBLANES, LANES), F32), SDS((SUBLANES, LANES), F32)],
        scratch_shapes=[pltpu.VMEM((lp, HEAD_DIM), BF), pltpu.VMEM((lp, HEAD_DIM), BF),
                        pltpu.VMEM((lp, HEAD_DIM), F32), pltpu.VMEM((lp, HEAD_DIM), F32),
                        pltpu.VMEM((1, lp), F32)],
        vmem_mib=56, name=name, comm=comm)(z, z, z, cum, cumt, qw, kw, dmix)


def _adamw_math(w, g, m, v):
    m2 = ADAM_B1 * m + (1.0 - ADAM_B1) * g
    v2 = ADAM_B2 * v + (1.0 - ADAM_B2) * (g * g)
    m_hat = m2 / (1.0 - ADAM_B1 ** ADAM_STEP)
    v_hat = v2 / (1.0 - ADAM_B2 ** ADAM_STEP)
    delta = -ADAM_LR * (m_hat / (jnp.sqrt(v_hat) + ADAM_EPS) + ADAM_WD * w)
    return delta, m2, v2


def _adamw(g_in, w, m, v, name, comm=None):
    r, c = w.shape
    partial_sum = g_in.ndim == 3
    tr = _largest_tile(r, 256, 16)

    def body(g_ref, w_ref, m_ref, v_ref, go_ref, d_ref, mo_ref, vo_ref):
        if partial_sum:
            g = g_ref[0].astype(F32)
            for k in range(1, g_in.shape[0]):
                g = g + g_ref[k].astype(F32)
        else:
            g = g_ref[...]
        delta, m2, v2 = _adamw_math(w_ref[...], g, m_ref[...], v_ref[...])
        go_ref[...] = g
        d_ref[...] = delta
        mo_ref[...] = m2
        vo_ref[...] = v2

    blk = pl.BlockSpec((tr, c), lambda i: (i, 0))
    g_spec = pl.BlockSpec((g_in.shape[0], tr, c), lambda i: (0, i, 0)) if partial_sum else blk
    return _call(
        body, grid=(r // tr,), in_specs=[g_spec, blk, blk, blk], out_specs=[blk] * 4,
        out_shape=[SDS((r, c), F32)] * 4, vmem_mib=40, name=name, comm=comm)(g_in, w, m, v)


def _peer(x, y, c, k):
    return (1 - x if k & 4 else x, 1 - y if k & 2 else y, 1 - c if k & 1 else c)


def _exchange(arrs, scatter, name, comm=None):
    n = len(arrs)

    def body(*refs):
        ins, outs = refs[:n], refs[n:2 * n]
        send_sems, recv_sems, local_sems = refs[2 * n:]
        x, y, c = lax.axis_index("x"), lax.axis_index("y"), lax.axis_index("c")
        me = 4 * x + 2 * y + c

        def src(t, dev):
            return ins[t].at[dev] if scatter else ins[t]

        def copy(t, k, arrival):
            px, py, pc = _peer(x, y, c, k)
            dev = 4 * px + 2 * py + pc
            return pltpu.make_async_remote_copy(
                src_ref=src(t, dev), dst_ref=outs[t].at[dev if arrival else me],
                send_sem=send_sems.at[t, k - 1], recv_sem=recv_sems.at[t, k - 1],
                device_id=(px, py, pc), device_id_type=pl.DeviceIdType.MESH)

        local = [pltpu.make_async_copy(src(t, me), outs[t].at[me], local_sems.at[t]) for t in range(n)]
        for cp in local:
            cp.start()
        pairs = [(t, k) for k in range(1, N_DEV) for t in range(n)]
        for t, k in pairs:
            copy(t, k, False).start()
        for cp in local:
            cp.wait()
        for t, k in pairs:
            copy(t, k, True).wait_recv()
        for t, k in pairs:
            copy(t, k, False).wait_send()

    out_shape = [SDS(a.shape if scatter else (N_DEV,) + a.shape, a.dtype) for a in arrs]
    anyspec = pl.BlockSpec(memory_space=pl.ANY)
    return pl.pallas_call(
        body, in_specs=[anyspec] * n, out_specs=[anyspec] * n, out_shape=out_shape,
        scratch_shapes=[pltpu.SemaphoreType.DMA((n, N_DEV - 1)), pltpu.SemaphoreType.DMA((n, N_DEV - 1)),
                        pltpu.SemaphoreType.DMA((n,))],
        name=name)(*arrs)


_SIBLING = 1
_ICI_RELS = (2, 4, 6)


def _mesh_pos():
    return lax.axis_index("x"), lax.axis_index("y"), lax.axis_index("c")


def _dev(pos):
    return 4 * pos[0] + 2 * pos[1] + pos[2]


def _gather_ici(shards):
    n = len(shards)
    rels = (_SIBLING,) + _ICI_RELS

    def remote(ins, outs, sems, arrival):
        x, y, c = _mesh_pos()
        cps = []
        for j, k in enumerate(rels):
            peer = _peer(x, y, c, k)
            slot = _dev(peer) if arrival else _dev((x, y, c))
            for t in range(n):
                cps.append(pltpu.make_async_remote_copy(
                    src_ref=ins[t], dst_ref=outs[t].at[slot], send_sem=sems[0].at[t, j], recv_sem=sems[1].at[t, j],
                    device_id=peer, device_id_type=pl.DeviceIdType.MESH))
        return cps

    def local(ins, outs, sems):
        me = _dev(_mesh_pos())
        return [pltpu.make_async_copy(ins[t], outs[t].at[me], sems[2].at[t]) for t in range(n)]

    def start(ins, outs, sems):
        for cp in local(ins, outs, sems) + remote(ins, outs, sems, False):
            cp.start()

    def finish(ins, outs, sems):
        for cp in local(ins, outs, sems):
            cp.wait()
        for cp in remote(ins, outs, sems, True):
            cp.wait_recv()
        for cp in remote(ins, outs, sems, False):
            cp.wait_send()

    return _Comm(shards, [SDS((N_DEV,) + s.shape, s.dtype) for s in shards],
                 [pltpu.SemaphoreType.DMA((n, len(rels))), pltpu.SemaphoreType.DMA((n, len(rels))),
                  pltpu.SemaphoreType.DMA((n,))], start, finish)


def _gather_fwd(partial):
    n = len(partial)

    def copies(ins, outs, sems, arrival):
        x, y, c = _mesh_pos()
        sibling = _peer(x, y, c, _SIBLING)
        cps = []
        for j, k in enumerate(_ICI_RELS):
            slot = _dev(_peer(x, y, c, k | _SIBLING if arrival else k))
            for t in range(n):
                cps.append(pltpu.make_async_remote_copy(
                    src_ref=ins[t].at[slot], dst_ref=outs[t].at[slot], send_sem=sems[0].at[t, j],
                    recv_sem=sems[1].at[t, j], device_id=sibling, device_id_type=pl.DeviceIdType.MESH))
        return cps

    def start(ins, outs, sems):
        for cp in copies(ins, outs, sems, False):
            cp.start()

    def finish(ins, outs, sems):
        for cp in copies(ins, outs, sems, True):
            cp.wait_recv()
        for cp in copies(ins, outs, sems, False):
            cp.wait_send()

    return _Comm(partial, [SDS(a.shape, a.dtype) for a in partial],
                 [pltpu.SemaphoreType.DMA((n, len(_ICI_RELS)))] * 2, start, finish,
                 aliases={t: t for t in range(n)})


def _scatter_sibling(slots):
    n = len(slots)

    def copies(ins, outs, sems):
        x, y, c = _mesh_pos()
        return [pltpu.make_async_remote_copy(
            src_ref=ins[t].at[:, 1 - c], dst_ref=outs[t], send_sem=sems[0].at[t], recv_sem=sems[1].at[t],
            device_id=_peer(x, y, c, _SIBLING), device_id_type=pl.DeviceIdType.MESH) for t in range(n)]

    def start(ins, outs, sems):
        for cp in copies(ins, outs, sems):
            cp.start()

    def finish(ins, outs, sems):
        for cp in copies(ins, outs, sems):
            cp.wait()

    return _Comm(slots, [SDS((s.shape[0],) + s.shape[2:], s.dtype) for s in slots],
                 [pltpu.SemaphoreType.DMA((n,))] * 2, start, finish)


def _scatter_ici(chip_sums):
    n = len(chip_sums)

    def remote(ins, outs, sems, arrival):
        x, y, c = _mesh_pos()
        cps = []
        for j, k in enumerate(_ICI_RELS):
            peer = _peer(x, y, c, k)
            theirs, mine = 2 * peer[0] + peer[1], 2 * x + y
            for t in range(n):
                cps.append(pltpu.make_async_remote_copy(
                    src_ref=ins[t].at[theirs], dst_ref=outs[t].at[theirs if arrival else mine],
                    send_sem=sems[0].at[t, j], recv_sem=sems[1].at[t, j],
                    device_id=peer, device_id_type=pl.DeviceIdType.MESH))
        return cps

    def local(ins, outs, sems):
        x, y, _ = _mesh_pos()
        return [pltpu.make_async_copy(ins[t].at[2 * x + y], outs[t].at[2 * x + y], sems[2].at[t]) for t in range(n)]

    def start(ins, outs, sems):
        for cp in local(ins, outs, sems) + remote(ins, outs, sems, False):
            cp.start()

    def finish(ins, outs, sems):
        for cp in local(ins, outs, sems):
            cp.wait()
        for cp in remote(ins, outs, sems, True):
            cp.wait_recv()
        for cp in remote(ins, outs, sems, False):
            cp.wait_send()

    return _Comm(chip_sums, [SDS(a.shape, a.dtype) for a in chip_sums],
                 [pltpu.SemaphoreType.DMA((n, len(_ICI_RELS))), pltpu.SemaphoreType.DMA((n, len(_ICI_RELS))),
                  pltpu.SemaphoreType.DMA((n,))], start, finish)


def _chip_sum(slots, from_sibling, core, name):
    nq, _, r, c = slots.shape
    tr = _largest_tile(r, 256, 16)

    def body(core_ref, a_ref, b_ref, o_ref):
        o_ref[...] = (a_ref[...].astype(F32) + b_ref[...].astype(F32)).astype(BF)

    return pl.pallas_call(
        body,
        grid_spec=pltpu.PrefetchScalarGridSpec(
            num_scalar_prefetch=1, grid=(nq, r // tr),
            in_specs=[pl.BlockSpec((None, None, tr, c), lambda q, i, core_ref: (q, core_ref[0], i, 0)),
                      pl.BlockSpec((None, tr, c), lambda q, i, core_ref: (q, i, 0))],
            out_specs=pl.BlockSpec((None, tr, c), lambda q, i, core_ref: (q, i, 0))),
        out_shape=SDS((nq, r, c), BF), name=name)(core, slots, from_sibling)


def _small_reduce(pack_g, meta_g, loss_scale, name, comm=None):
    w = pack_g.shape[2]

    def body(p_ref, m_ref, tot_ref, meta_ref, loss_ref):
        acc = p_ref[0]
        macc = m_ref[0]
        for k in range(1, N_DEV):
            acc = acc + p_ref[k]
            macc = macc + m_ref[k]
        tot = jnp.sum(acc, axis=0, keepdims=True)
        tot_ref[...] = tot
        meta_ref[...] = macc
        loss_ref[...] = jnp.full((1, LANES), loss_scale * jnp.sum(tot[:, w - LANES:w]), F32)

    return pl.pallas_call(
        body, out_shape=[SDS((1, w), F32), SDS(meta_g.shape[1:], F32), SDS((1, LANES), F32)],
        compiler_params=pltpu.CompilerParams(vmem_limit_bytes=32 * MIB), name=name)(pack_g, meta_g)


def _local_step(x, target, sw, plan):
    s_len, d = x.shape
    n_heads = plan.n_heads
    plan.at("start")
    meta = plan.weights("meta")
    n_meta = meta.shape[0]
    l = n_meta + s_len
    lp = -(-l // LANES) * LANES
    tm = _largest_tile(lp, 544, 16)
    tq = _largest_tile(lp, 272, 16)
    te = _largest_tile(lp, 272, 16)
    tmd = _largest_tile(d, 512, LANES)

    zpad = jnp.zeros((lp - l, d), F32)
    h0 = jnp.concatenate([meta, x, zpad], axis=0)
    tpad = jnp.concatenate([jnp.zeros((n_meta, d), F32), target, zpad], axis=0)

    wg1, wu1, wd1 = plan.weights("ffn1")
    fs = wg1.shape[2]
    h1, a1, b1, u1 = _ffn_fwd(h0, sw["ffn1_norm"], wg1, wu1, wd1, tm, "ffn1_fwd", plan.comm("ffn1_fwd"))
    plan.at("after_ffn1_fwd")
    win, pw, wout = plan.weights("mix")
    nz = win.shape[1]
    p_w = sw["pool_scale"].shape[1]
    npb = p_w // LANES
    fblk = nz // LANES - 1
    tnz = _largest_tile(nz, 1408, LANES)
    qw, kw, bfp, ps = sw["q_norm"], sw["k_norm"], sw["b_forget"], sw["pool_scale"]
    z, u2 = _norm_matmul(h1, sw["mix_norm"], win, tm, tnz, "mix_in", plan.comm("mix_in"))
    cum = _fox_prep(z, bfp, fblk, "fox_prep")
    cumt = cum[:, :n_heads].T
    pool_o = _pool_fwd(z, pw, ps, "pool_fwd")
    att_o = _att_fwd(z, cum, cumt, qw, kw, n_heads, npb, tq, "att_fwd", plan.comm("att_fwd"))
    h2 = _out_proj(h1, pool_o, att_o, wout, tm, "out_proj", plan.comm("out_proj"))
    wg2, wu2, wd2 = plan.weights("ffn2")
    h3, a2, b2, u3 = _ffn_fwd(h2, sw["ffn2_norm"], wg2, wu2, wd2, tm, "ffn2_fwd", plan.comm("ffn2_fwd"))
    dy, dob3, lsq = _loss_head(h3, tpad, n_meta, l, te, "loss_head")

    du3, da2, db2, hid2 = _ffn_bwd_dx(dob3, a2, b2, wg2, wu2, wd2, tm, "ffn2_bwd_dx", plan.comm("ffn2_bwd_dx"))
    dh2, dh2b, dn2 = _rms_bwd(du3, h2, sw["ffn2_norm"], dy, 1.0, te, "ffn2_rms_bwd")
    plan.grad("ffn2_w_gate", _matmul_tn(u3, da2, tmd, fs, "ffn2_dwg", plan.comm("ffn2_dwg")))
    plan.grad("ffn2_w_up", _matmul_tn(u3, db2, tmd, fs, "ffn2_dwu", plan.comm("ffn2_dwu")))
    plan.grad("ffn2_w_down", _matmul_tn(hid2, dob3, fs, d, "ffn2_dwd", plan.comm("ffn2_dwd")))

    dmix = _matmul_nt(dh2b, wout, tm, d, BF, "out_proj_bwd", plan.comm("out_proj_bwd"))
    tmp = _largest_tile(p_w, 512, LANES)
    plan.grad("w_out", jnp.concatenate([_matmul_tn(pool_o, dh2b, tmp, d, "dwout_pool"),
                                        _matmul_tn(att_o, dh2b, tmp, d, "dwout_att")], axis=0))
    dzp, dpw, dps = _pool_bwd(z, dmix, pw, ps, "pool_bwd")
    plan.grad("pool_w", dpw)
    plan.at("before_att_bwd")
    dq, dk, dv, dcq, dck, dqw, dkw = _att_bwd(z, cum, cumt, qw, kw, dmix, n_heads, npb, npb, tq, "att_bwd",
                                              plan.comm("att_bwd"))
    dcum = dcq[:, :, 0].T - dck[:, 0, :].T
    dcum = jnp.pad(dcum, ((0, 0), (0, LANES - n_heads)))
    dzf, dbf = _fox_bwd(z, bfp, dcum, fblk, "fox_bwd")
    dz = jnp.concatenate([dzp, dq, dk, dv, dzf], axis=1)
    plan.grad("w_in", _matmul_tn(u2, dz, tmd, tnz, "dwin", plan.comm("dwin")))
    du2 = _matmul_nt(dz, win, tm, tnz, F32, "mix_in_bwd", plan.comm("mix_in_bwd"))
    plan.at("before_ffn1_bwd_dx")
    dh1, dob1, dnm = _rms_bwd(du2, h1, sw["mix_norm"], dh2, 0.5, te, "mix_rms_bwd")

    du1, da1, db1, hid1 = _ffn_bwd_dx(dob1, a1, b1, wg1, wu1, wd1, tm, "ffn1_bwd_dx", plan.comm("ffn1_bwd_dx"))
    dh0, _, dn1 = _rms_bwd(du1, h0, sw["ffn1_norm"], dh1, 1.0, te, "ffn1_rms_bwd")
    plan.grad("ffn1_w_gate", _matmul_tn(u1, da1, tmd, fs, "ffn1_dwg", plan.comm("ffn1_dwg")))
    plan.grad("ffn1_w_up", _matmul_tn(u1, db1, tmd, fs, "ffn1_dwu", plan.comm("ffn1_dwu")))
    plan.at("before_ffn1_dwd")
    plan.grad("ffn1_w_down", _matmul_tn(hid1, dob1, fs, d, "ffn1_dwd", plan.comm("ffn1_dwd")))

    small = [dn1, dnm, dn2, dps, dqw, dkw, dbf, lsq]
    return dh0[n_meta:l], dh0[:n_meta], small


_BIG = ("ffn1_w_gate", "ffn1_w_up", "ffn1_w_down", "w_in", "pool_w", "w_out", "ffn2_w_gate", "ffn2_w_up", "ffn2_w_down")
_SMALL = ("ffn1_norm", "mix_norm", "ffn2_norm", "pool_scale", "q_norm", "k_norm", "b_forget")
_ORDER = ("meta_tokens", "ffn1_norm", "ffn1_w_gate", "ffn1_w_up", "ffn1_w_down", "mix_norm", "w_in", "b_forget",
          "q_norm", "k_norm", "pool_w", "pool_scale", "w_out", "ffn2_norm", "ffn2_w_gate", "ffn2_w_up", "ffn2_w_down")


_FFN1 = ("ffn1_w_gate", "ffn1_w_up", "ffn1_w_down")
_FFN2 = ("ffn2_w_gate", "ffn2_w_up", "ffn2_w_down")
_MIX = ("w_in", "pool_w", "w_out")

_RIDES = {
    "ffn1_fwd": (("g1", _MIX + ("ffn2_w_down",)),),
    "mix_in": (("g1", ("ffn2_w_up",)), ("g2", ("ffn2_w_down",))),
    "att_fwd": (("g1", ("ffn2_w_gate",)),),
    "out_proj": (("g2", ("ffn2_w_gate", "ffn2_w_up")),),
    "ffn2_dwu": (("s1", ("ffn2_w_gate",)),),
    "ffn2_dwd": (("s1", ("ffn2_w_up",)),),
    "out_proj_bwd": (("s1", ("ffn2_w_down",)),),
    "att_bwd": (("s2", ("ffn2_w_gate", "ffn2_w_up")),),
    "mix_in_bwd": (("s1", _MIX),),
    "ffn1_bwd_dx": (("s2", ("ffn2_w_down",) + _MIX),),
    "ffn1_dwu": (("s1", ("ffn1_w_gate",)),),
    "ffn1_dwd": (("s2", ("ffn1_w_gate",)), ("s1", ("ffn1_w_up",))),
    "adamw_ffn2_w_gate": (("s2", ("ffn1_w_up",)), ("s1", ("ffn1_w_down",))),
    "adamw_ffn2_w_up": (("s2", ("ffn1_w_down",)),),
}
_POINTS = {
    "start": (("alone", "g1", _FFN1 + ("meta_tokens",)), ("alone", "g2", _FFN1 + ("meta_tokens",))),
    "after_ffn1_fwd": (("alone", "g2", _MIX),),
    "before_att_bwd": (("sum", ("ffn2_w_gate", "ffn2_w_up")),),
    "before_ffn1_bwd_dx": (("sum", ("ffn2_w_down",) + _MIX),),
    "before_ffn1_dwd": (("sum", ("ffn1_w_gate",)),),
    "before_adamw_ffn2_w_gate": (("sum", ("ffn1_w_up",)),),
    "before_adamw_ffn2_w_up": (("sum", ("ffn1_w_down",)),),
}


class _MeshPlan:
    def __init__(self, shards, core, d, d_in, n_heads):
        self.shard, self.core = dict(shards), core
        self.d, self.d_in, self.n_heads = d, d_in, n_heads
        self.partial, self.full, self.slots, self.from_sibling, self.chip_sum, self.received = {}, {}, {}, {}, {}, {}
        self.pending = []

    def _phase(self, kind, names):
        src, dst, make = {"g1": (self.shard, self.partial, _gather_ici), "g2": (self.partial, self.full, _gather_fwd),
                          "s1": (self.slots, self.from_sibling, _scatter_sibling),
                          "s2": (self.chip_sum, self.received, _scatter_ici)}[kind]
        op = make([src[n] for n in names])
        self.pending.append((op, dst, names))
        return op

    def _settle(self):
        for op, dst, names in self.pending:
            dst.update(zip(names, op.results))
        self.pending = []

    def comm(self, kernel_name):
        self._settle()
        return _merge_comm([self._phase(kind, names) for kind, names in _RIDES.get(kernel_name, ())])

    def at(self, point):
        for step in _POINTS.get(point, ()):
            self._settle()
            if step[0] == "alone":
                _comm_alone(self._phase(step[1], step[2]), "_".join((step[1], point)))
            else:
                for n in step[1]:
                    self.chip_sum[n] = _chip_sum(self.slots[n], self.from_sibling[n], self.core, "chip_sum_" + n)

    def weights(self, group):
        self._settle()
        f, d = self.full, self.d
        if group == "meta":
            g = f["meta_tokens"]
            return g.transpose(1, 0, 2).reshape(g.shape[1], d)
        if group == "ffn1":
            return tuple(f[n] for n in _FFN1)
        if group == "ffn2":
            return tuple(f[n] for n in _FFN2)
        n_main = self.d_in - self.n_heads
        win = f["w_in"].transpose(1, 0, 2).reshape(d, self.d_in)
        win = jnp.concatenate([win[:, :n_main], jnp.pad(win[:, n_main:], ((0, 0), (0, LANES - self.n_heads)))], axis=1)
        pw = f["pool_w"]
        pw = pw.transpose(1, 0, 2, 3).reshape(pw.shape[1], pw.shape[3], pw.shape[3])
        return win, pw, f["w_out"].reshape(-1, d)

    def grad(self, name, g):
        d = self.d
        if name == "w_in":
            g = g[:, :self.d_in].reshape(d, N_DEV, -1).transpose(1, 0, 2)
        elif name == "pool_w":
            ng, gw = g.shape[0], g.shape[2]
            g = g.astype(BF).reshape(ng, N_DEV, -1, gw).transpose(1, 0, 2, 3).reshape(N_DEV, -1, gw)
        elif name == "w_out":
            g = g.reshape(N_DEV, -1, d)
        self.slots[name] = g.reshape((N_DEV // 2, 2) + g.shape[1:])

    def gradient_parts(self, name):
        self._settle()
        return self.received[name]


def _as2d(a):
    return a.reshape(-1, a.shape[-1])


def kernel(x, meta_tokens, ffn1_norm, ffn1_w_gate, ffn1_w_up, ffn1_w_down, mix_norm, w_in, b_forget, q_norm, k_norm, pool_w, pool_scale, w_out, ffn2_norm, ffn2_w_gate, ffn2_w_up, ffn2_w_down, loss_target, m_meta_tokens, m_ffn1_norm, m_ffn1_w_gate, m_ffn1_w_up, m_ffn1_w_down, m_mix_norm, m_w_in, m_b_forget, m_q_norm, m_k_norm, m_pool_w, m_pool_scale, m_w_out, m_ffn2_norm, m_ffn2_w_gate, m_ffn2_w_up, m_ffn2_w_down, v_meta_tokens, v_ffn1_norm, v_ffn1_w_gate, v_ffn1_w_up, v_ffn1_w_down, v_mix_norm, v_w_in, v_b_forget, v_q_norm, v_k_norm, v_pool_w, v_pool_scale, v_w_out, v_ffn2_norm, v_ffn2_w_gate, v_ffn2_w_up, v_ffn2_w_down):
    w = dict(meta_tokens=meta_tokens, ffn1_norm=ffn1_norm, ffn1_w_gate=ffn1_w_gate, ffn1_w_up=ffn1_w_up,
             ffn1_w_down=ffn1_w_down, mix_norm=mix_norm, w_in=w_in, b_forget=b_forget, q_norm=q_norm, k_norm=k_norm,
             pool_w=pool_w, pool_scale=pool_scale, w_out=w_out, ffn2_norm=ffn2_norm, ffn2_w_gate=ffn2_w_gate,
             ffn2_w_up=ffn2_w_up, ffn2_w_down=ffn2_w_down)
    m = dict(meta_tokens=m_meta_tokens, ffn1_norm=m_ffn1_norm, ffn1_w_gate=m_ffn1_w_gate, ffn1_w_up=m_ffn1_w_up,
             ffn1_w_down=m_ffn1_w_down, mix_norm=m_mix_norm, w_in=m_w_in, b_forget=m_b_forget, q_norm=m_q_norm,
             k_norm=m_k_norm, pool_w=m_pool_w, pool_scale=m_pool_scale, w_out=m_w_out, ffn2_norm=m_ffn2_norm,
             ffn2_w_gate=m_ffn2_w_gate, ffn2_w_up=m_ffn2_w_up, ffn2_w_down=m_ffn2_w_down)
    v = dict(meta_tokens=v_meta_tokens, ffn1_norm=v_ffn1_norm, ffn1_w_gate=v_ffn1_w_gate, ffn1_w_up=v_ffn1_w_up,
             ffn1_w_down=v_ffn1_w_down, mix_norm=v_mix_norm, w_in=v_w_in, b_forget=v_b_forget, q_norm=v_q_norm,
             k_norm=v_k_norm, pool_w=v_pool_w, pool_scale=v_pool_scale, w_out=v_w_out, ffn2_norm=v_ffn2_norm,
             ffn2_w_gate=v_ffn2_w_gate, ffn2_w_up=v_ffn2_w_up, ffn2_w_down=v_ffn2_w_down)

    d = x.shape[-1]
    n_heads = b_forget.shape[-1]
    me = 4 * lax.axis_index("x") + 2 * lax.axis_index("y") + lax.axis_index("c")
    core = lax.axis_index("c").astype(jnp.int32).reshape(1)

    shards = {k: w[k][0].astype(BF) for k in _BIG}
    shards["meta_tokens"] = meta_tokens
    plan = _MeshPlan(shards, core, d, N_DEV * w_in.shape[-1], n_heads)
    sw = {k: w[k] for k in _SMALL}
    sw["b_forget"] = jnp.pad(b_forget, ((0, 0), (0, LANES - n_heads)))
    dx, dmeta, small = _local_step(x[0], loss_target[0], sw, plan)

    res = {}
    for k in _FFN2 + _MIX + _FFN1:
        plan.at("before_adamw_" + k)
        res[k] = _adamw(plan.gradient_parts(k), _as2d(w[k]), _as2d(m[k]), _as2d(v[k]), "adamw_" + k,
                        plan.comm("adamw_" + k))

    pack = jnp.concatenate(small, axis=1)
    pack_g, meta_g = _exchange([pack, dmeta], False, "gather_small")
    tot, dmeta_tot, loss_row = _small_reduce(pack_g, meta_g, 0.5 / d, "small_reduce")

    mcols = meta_tokens.shape[1]
    g_meta = lax.dynamic_slice_in_dim(dmeta_tot, me * mcols, mcols, axis=1)
    res["meta_tokens"] = _adamw(g_meta, meta_tokens, m_meta_tokens, v_meta_tokens, "adamw_meta_tokens")

    def packed(src):
        return jnp.concatenate([src[k] for k in _SMALL[:-1]] + [jnp.pad(src["b_forget"], ((0, 0), (0, LANES - n_heads)))],
                               axis=1)

    wp = packed(w)
    sm = _adamw(tot[:, :wp.shape[1]], wp, packed(m), packed(v), "adamw_small")
    off = 0
    for k in _SMALL:
        width = w[k].shape[1]
        res[k] = tuple(o[:, off:off + width] for o in sm)
        off += width if k != "b_forget" else LANES

    outs = [loss_row[0, 0], dx[None]]
    for idx in range(4):
        outs += [res[k][idx].reshape(w[k].shape) for k in _ORDER]
    return tuple(outs)
```

```python
import functools

import jax
import jax.numpy as jnp
from jax import lax
from jax.experimental import pallas as pl
from jax.experimental.pallas import tpu as pltpu

F32 = jnp.float32
BF = jnp.bfloat16
SDS = jax.ShapeDtypeStruct

N_DEV = 8
LANES = 128
SUBLANES = 8
HEAD_DIM = 128
POOL_WINDOWS = (2, 4, 8, 16)
RMS_EPS = 1e-6
NEG_BIG = -1e30
MIB = 1024 * 1024

ADAM_LR = 0.001
ADAM_B1 = 0.9
ADAM_B2 = 0.999
ADAM_EPS = 1e-08
ADAM_WD = 0.01
ADAM_STEP = 10


class _Comm:
    def __init__(self, arrs, out_shape, sems, start, finish, aliases=None):
        self.arrs, self.out_shape, self.sems = list(arrs), list(out_shape), list(sems)
        self.start, self.finish, self.aliases = start, finish, dict(aliases or {})
        self.results = None


def _merge_comm(ops):
    ops = [op for op in ops if op is not None]
    if not ops:
        return None
    na, no, ns = [0], [0], [0]
    for op in ops:
        na.append(na[-1] + len(op.arrs))
        no.append(no[-1] + len(op.out_shape))
        ns.append(ns[-1] + len(op.sems))

    def parts(i, ins, outs, sems):
        return ins[na[i]:na[i + 1]], outs[no[i]:no[i + 1]], sems[ns[i]:ns[i + 1]]

    def start(ins, outs, sems):
        for i, op in enumerate(ops):
            op.start(*parts(i, ins, outs, sems))

    def finish(ins, outs, sems):
        for i, op in enumerate(ops):
            op.finish(*parts(i, ins, outs, sems))

    aliases = {}
    for i, op in enumerate(ops):
        for a, o in op.aliases.items():
            aliases[na[i] + a] = no[i] + o
    merged = _Comm([a for op in ops for a in op.arrs], [s for op in ops for s in op.out_shape],
                   [s for op in ops for s in op.sems], start, finish, aliases)
    merged.children = (ops, no)
    return merged


def _deliver(comm, results):
    comm.results = list(results)
    if hasattr(comm, "children"):
        ops, no = comm.children
        for i, op in enumerate(ops):
            _deliver(op, results[no[i]:no[i + 1]])


def _call(body, *, grid, in_specs, out_specs, out_shape, scratch_shapes=(), vmem_mib, name, comm=None):
    single = not isinstance(out_shape, (list, tuple))
    out_specs = [out_specs] if single else list(out_specs)
    out_shape = [out_shape] if single else list(out_shape)
    in_specs, scratch_shapes = list(in_specs), list(scratch_shapes)
    params = pltpu.CompilerParams(dimension_semantics=("arbitrary",) * len(grid), vmem_limit_bytes=vmem_mib * MIB)
    n_in, n_out, n_scr = len(in_specs), len(out_specs), len(scratch_shapes)

    def run(*args):
        if comm is None:
            res = pl.pallas_call(body, grid=grid, in_specs=in_specs, out_specs=out_specs, out_shape=out_shape,
                                 scratch_shapes=scratch_shapes, compiler_params=params, name=name)(*args)
            return res[0] if single else res
        ci, co = len(comm.arrs), len(comm.out_shape)

        def with_comm(*refs):
            ins, cins = refs[:n_in], refs[n_in:n_in + ci]
            o0 = n_in + ci
            outs, couts = refs[o0:o0 + n_out], refs[o0 + n_out:o0 + n_out + co]
            s0 = o0 + n_out + co
            scr, csems = refs[s0:s0 + n_scr], refs[s0 + n_scr:]
            ids = [pl.program_id(a) for a in range(len(grid))]
            first = functools.reduce(jnp.logical_and, [i == 0 for i in ids])
            last = functools.reduce(jnp.logical_and, [i == g - 1 for i, g in zip(ids, grid)])

            @pl.when(first)
            def _():
                comm.start(cins, couts, csems)

            body(*ins, *outs, *scr)

            @pl.when(last)
            def _():
                comm.finish(cins, couts, csems)

        anyspec = pl.BlockSpec(memory_space=pl.ANY)
        res = pl.pallas_call(
            with_comm, grid=grid, in_specs=in_specs + [anyspec] * ci, out_specs=out_specs + [anyspec] * co,
            out_shape=out_shape + comm.out_shape, scratch_shapes=scratch_shapes + comm.sems,
            input_output_aliases={n_in + a: n_out + o for a, o in comm.aliases.items()},
            compiler_params=params, name=name)(*args, *comm.arrs)
        _deliver(comm, res[n_out:])
        return res[0] if single else res[:n_out]

    return run


def _comm_alone(comm, name):
    def body(*refs):
        ci, co = len(comm.arrs), len(comm.out_shape)
        ins, outs, sems = refs[:ci], refs[ci:ci + co], refs[ci + co:]
        comm.start(ins, outs, sems)
        comm.finish(ins, outs, sems)

    anyspec = pl.BlockSpec(memory_space=pl.ANY)
    res = pl.pallas_call(
        body, in_specs=[anyspec] * len(comm.arrs), out_specs=[anyspec] * len(comm.out_shape),
        out_shape=comm.out_shape, scratch_shapes=comm.sems, input_output_aliases=comm.aliases, name=name)(*comm.arrs)
    _deliver(comm, res)


def _largest_tile(n, cap, mult):
    if n <= cap:
        return n
    best = None
    for t in range(mult, cap + 1, mult):
        if n % t == 0:
            best = t
    assert best is not None, (n, cap, mult)
    return best


def _dot(a, b):
    return jnp.dot(a, b, preferred_element_type=F32)


def _dot_nt(a, b):
    return lax.dot_general(a, b, (((1,), (1,)), ((), ())), preferred_element_type=F32)


def _dot_tn(a, b):
    return lax.dot_general(a, b, (((0,), (0,)), ((), ())), preferred_element_type=F32)


def _rows8(x):
    t, c = x.shape
    return jnp.sum(x.reshape(t // SUBLANES, SUBLANES, c), axis=0)


def _rstd(x):
    return lax.rsqrt(jnp.mean(x * x, axis=-1, keepdims=True) + RMS_EPS)


def _ffn_fwd(h, g, wg, wu, wd, tm, name, comm=None):
    lp, d = h.shape
    ns, fs, _ = wg.shape

    def body(h_ref, g_ref, wg_ref, wu_ref, wd_ref, out_ref, a_ref, b_ref, u_ref, acc_ref):
        j = pl.program_id(1)

        @pl.when(j == 0)
        def _():
            hh = h_ref[...]
            u_ref[...] = (hh * _rstd(hh) * g_ref[...]).astype(BF)
            acc_ref[...] = jnp.zeros_like(acc_ref)

        u = u_ref[...]
        a = _dot_nt(u, wg_ref[...])
        b = _dot_nt(u, wu_ref[...])
        a_ref[...] = a.astype(BF)
        b_ref[...] = b.astype(BF)
        hid = (a * jax.nn.sigmoid(a) * b).astype(BF)
        acc_ref[...] += _dot(hid, wd_ref[...])

        @pl.when(j == ns - 1)
        def _():
            out_ref[...] = h_ref[...] + 0.5 * acc_ref[...]

    row = pl.BlockSpec((tm, d), lambda i, j: (i, 0))
    act = pl.BlockSpec((None, tm, fs), lambda i, j: (j, i, 0))
    return _call(
        body, grid=(lp // tm, ns),
        in_specs=[row, pl.BlockSpec((1, d), lambda i, j: (0, 0)),
                  pl.BlockSpec((None, fs, d), lambda i, j: (j, 0, 0)),
                  pl.BlockSpec((None, fs, d), lambda i, j: (j, 0, 0)),
                  pl.BlockSpec((None, fs, d), lambda i, j: (j, 0, 0))],
        out_specs=[row, act, act, row],
        out_shape=[SDS((lp, d), F32), SDS((ns, lp, fs), BF), SDS((ns, lp, fs), BF), SDS((lp, d), BF)],
        scratch_shapes=[pltpu.VMEM((tm, d), F32)],
        vmem_mib=56, name=name, comm=comm)(h, g, wg, wu, wd)


def _ffn_bwd_dx(dob, a, b, wg, wu, wd, tm, name, comm=None):
    lp, d = dob.shape
    ns, fs, _ = wg.shape

    def body(do_ref, a_ref, b_ref, wg_ref, wu_ref, wd_ref, du_ref, da_ref, db_ref, hid_ref):
        j = pl.program_id(1)

        @pl.when(j == 0)
        def _():
            du_ref[...] = jnp.zeros_like(du_ref)

        dhid = _dot_nt(do_ref[...], wd_ref[...])
        av = a_ref[...].astype(F32)
        bv = b_ref[...].astype(F32)
        sig = jax.nn.sigmoid(av)
        sil = av * sig
        dbv = (dhid * sil).astype(BF)
        dav = (dhid * bv * (sig * (1.0 + av * (1.0 - sig)))).astype(BF)
        hid_ref[...] = (sil * bv).astype(BF)
        da_ref[...] = dav
        db_ref[...] = dbv
        du_ref[...] += _dot(dav, wg_ref[...]) + _dot(dbv, wu_ref[...])

    row = pl.BlockSpec((tm, d), lambda i, j: (i, 0))
    act = pl.BlockSpec((None, tm, fs), lambda i, j: (j, i, 0))
    return _call(
        body, grid=(lp // tm, ns),
        in_specs=[row, act, act,
                  pl.BlockSpec((None, fs, d), lambda i, j: (j, 0, 0)),
                  pl.BlockSpec((None, fs, d), lambda i, j: (j, 0, 0)),
                  pl.BlockSpec((None, fs, d), lambda i, j: (j, 0, 0))],
        out_specs=[row, act, act, act],
        out_shape=[SDS((lp, d), F32)] + [SDS((ns, lp, fs), BF)] * 3,
        vmem_mib=56, name=name, comm=comm)(dob, a, b, wg, wu, wd)


def _rms_bwd(du, h, g, dres, bscale, tm, name, comm=None):
    lp, d = h.shape

    def body(du_ref, h_ref, g_ref, dres_ref, dh_ref, dhb_ref, dg_ref):
        @pl.when(pl.program_id(0) == 0)
        def _():
            dg_ref[...] = jnp.zeros_like(dg_ref)

        hh = h_ref[...]
        r = _rstd(hh)
        xhat = hh * r
        duv = du_ref[...]
        dg_ref[...] += _rows8(duv * xhat)
        dxh = duv * g_ref[...]
        dh = dres_ref[...] + r * (dxh - xhat * jnp.mean(dxh * xhat, axis=-1, keepdims=True))
        dh_ref[...] = dh
        dhb_ref[...] = (bscale * dh).astype(BF)

    row = pl.BlockSpec((tm, d), lambda i: (i, 0))
    return _call(
        body, grid=(lp // tm,),
        in_specs=[row, row, pl.BlockSpec((1, d), lambda i: (0, 0)), row],
        out_specs=[row, row, pl.BlockSpec((SUBLANES, d), lambda i: (0, 0))],
        out_shape=[SDS((lp, d), F32), SDS((lp, d), BF), SDS((SUBLANES, d), F32)],
        vmem_mib=48, name=name, comm=comm)(du, h, g, dres)


def _matmul_tn(a, b, tm, tn, name, comm=None):
    a_b, b_b = a.ndim == 3, b.ndim == 3
    ns = a.shape[0] if a_b else (b.shape[0] if b_b else 1)
    l, m = a.shape[-2:]
    n = b.shape[-1]

    def body(a_ref, b_ref, o_ref):
        o_ref[...] = _dot_tn(a_ref[...], b_ref[...]).astype(o_ref.dtype)

    a_spec = (pl.BlockSpec((None, l, tm), lambda s, i, j: (s, 0, i)) if a_b
              else pl.BlockSpec((l, tm), lambda s, i, j: (0, i)))
    b_spec = (pl.BlockSpec((None, l, tn), lambda s, i, j: (s, 0, j)) if b_b
              else pl.BlockSpec((l, tn), lambda s, i, j: (0, j)))
    batched = a_b or b_b
    o_spec = (pl.BlockSpec((None, tm, tn), lambda s, i, j: (s, i, j)) if batched
              else pl.BlockSpec((tm, tn), lambda s, i, j: (i, j)))
    o_shape = SDS((ns, m, n), BF) if batched else SDS((m, n), BF)
    return _call(
        body, grid=(ns, m // tm, n // tn), in_specs=[a_spec, b_spec], out_specs=o_spec, out_shape=o_shape,
        vmem_mib=48, name=name, comm=comm)(a, b)


def _matmul_nt(x, w, tm, tk, out_dtype, name, comm=None):
    l, k = x.shape
    n = w.shape[0]
    nk = k // tk

    def body(x_ref, w_ref, o_ref, acc_ref):
        kk = pl.program_id(1)

        @pl.when(kk == 0)
        def _():
            acc_ref[...] = jnp.zeros_like(acc_ref)

        acc_ref[...] += _dot_nt(x_ref[...], w_ref[...])

        @pl.when(kk == nk - 1)
        def _():
            o_ref[...] = acc_ref[...].astype(o_ref.dtype)

    return _call(
        body, grid=(l // tm, nk),
        in_specs=[pl.BlockSpec((tm, tk), lambda i, kk: (i, kk)), pl.BlockSpec((n, tk), lambda i, kk: (0, kk))],
        out_specs=pl.BlockSpec((tm, n), lambda i, kk: (i, 0)),
        out_shape=SDS((l, n), out_dtype),
        scratch_shapes=[pltpu.VMEM((tm, n), F32)],
        vmem_mib=48, name=name, comm=comm)(x, w)


def _norm_matmul(h, g, w, tm, tn, name, comm=None):
    lp, d = h.shape
    n = w.shape[1]

    def body(h_ref, g_ref, w_ref, z_ref, u_ref):
        @pl.when(pl.program_id(1) == 0)
        def _():
            hh = h_ref[...]
            u_ref[...] = (hh * _rstd(hh) * g_ref[...]).astype(BF)

        z_ref[...] = _dot(u_ref[...], w_ref[...])

    row = pl.BlockSpec((tm, d), lambda i, j: (i, 0))
    return _call(
        body, grid=(lp // tm, n // tn),
        in_specs=[row, pl.BlockSpec((1, d), lambda i, j: (0, 0)), pl.BlockSpec((d, tn), lambda i, j: (0, j))],
        out_specs=[pl.BlockSpec((tm, tn), lambda i, j: (i, j)), row],
        out_shape=[SDS((lp, n), F32), SDS((lp, d), BF)],
        vmem_mib=48, name=name, comm=comm)(h, g, w)


def _out_proj(h, pool_o, att_o, w_out, tm, name, comm=None):
    lp, d = h.shape
    p = pool_o.shape[1]
    dm = w_out.shape[0]

    def body(h_ref, p_ref, a_ref, w_ref, o_ref):
        o_ref[...] = h_ref[...] + _dot(p_ref[...], w_ref[0:p, :]) + _dot(a_ref[...], w_ref[p:dm, :])

    row = pl.BlockSpec((tm, d), lambda i: (i, 0))
    return _call(
        body, grid=(lp // tm,),
        in_specs=[row, pl.BlockSpec((tm, p), lambda i: (i, 0)), pl.BlockSpec((tm, dm - p), lambda i: (i, 0)),
                  pl.BlockSpec((dm, d), lambda i: (0, 0))],
        out_specs=row, out_shape=SDS((lp, d), F32),
        vmem_mib=48, name=name, comm=comm)(h, pool_o, att_o, w_out)


def _loss_head(y, tpad, row0, row1, tm, name, comm=None):
    lp, d = y.shape

    def body(y_ref, t_ref, dy_ref, dob_ref, ls_ref):
        i = pl.program_id(0)

        @pl.when(i == 0)
        def _():
            ls_ref[...] = jnp.zeros_like(ls_ref)

        rows = i * tm + lax.broadcasted_iota(jnp.int32, (tm, d), 0)
        err = jnp.where((rows >= row0) & (rows < row1), y_ref[...] - t_ref[...], 0.0)
        dy = err * (1.0 / d)
        dy_ref[...] = dy
        dob_ref[...] = (0.5 * dy).astype(BF)
        sq = _rows8(err * err)
        acc = sq[:, 0:LANES]
        for c in range(1, d // LANES):
            acc = acc + sq[:, c * LANES:(c + 1) * LANES]
        ls_ref[...] += acc

    row = pl.BlockSpec((tm, d), lambda i: (i, 0))
    return _call(
        body, grid=(lp // tm,), in_specs=[row, row],
        out_specs=[row, row, pl.BlockSpec((SUBLANES, LANES), lambda i: (0, 0))],
        out_shape=[SDS((lp, d), F32), SDS((lp, d), BF), SDS((SUBLANES, LANES), F32)],
        vmem_mib=48, name=name, comm=comm)(y, tpad)


def _window_select(levels, gidx):
    out = levels[-1]
    for k in range(len(levels) - 2, -1, -1):
        out = jnp.where(gidx == k, levels[k], out)
    return out


def _pool_window_mean_minus_id(x, gidx):
    rows = lax.broadcasted_iota(jnp.int32, x.shape, 0)
    levels = []
    s = x
    shift = 1
    while shift < POOL_WINDOWS[-1]:
        s = s + jnp.where(rows >= shift, pltpu.roll(s, shift, 0), 0.0)
        shift *= 2
        if shift in POOL_WINDOWS:
            levels.append(s)
    win = _window_select(levels, gidx)
    cnt = jnp.minimum(rows + 1, _window_select(list(POOL_WINDOWS), gidx)).astype(F32)
    return win / cnt - x, cnt


def _pool_window_transpose(dy, cnt, gidx):
    lp = dy.shape[0]
    rows = lax.broadcasted_iota(jnp.int32, dy.shape, 0)
    levels = []
    s = dy / cnt
    shift = 1
    while shift < POOL_WINDOWS[-1]:
        s = s + jnp.where(rows < lp - shift, pltpu.roll(s, lp - shift, 0), 0.0)
        shift *= 2
        if shift in POOL_WINDOWS:
            levels.append(s)
    return _window_select(levels, gidx) - dy


def _pool_fwd(z, pool_w, pool_scale, name, comm=None):
    lp = z.shape[0]
    ng, gw, _ = pool_w.shape

    def body(p_ref, w_ref, s_ref, o_ref):
        pooled, _ = _pool_window_mean_minus_id(p_ref[...], pl.program_id(0))
        o_ref[...] = (_dot(pooled.astype(BF), w_ref[...]) * s_ref[...]).astype(BF)

    return _call(
        body, grid=(ng,),
        in_specs=[pl.BlockSpec((lp, gw), lambda g: (0, g)), pl.BlockSpec((None, gw, gw), lambda g: (g, 0, 0)),
                  pl.BlockSpec((1, gw), lambda g: (0, g))],
        out_specs=pl.BlockSpec((lp, gw), lambda g: (0, g)), out_shape=SDS((lp, ng * gw), BF),
        vmem_mib=48, name=name, comm=comm)(z, pool_w, pool_scale)


def _pool_bwd(z, dmix, pool_w, pool_scale, name, comm=None):
    lp = z.shape[0]
    ng, gw, _ = pool_w.shape

    def body(p_ref, d_ref, w_ref, s_ref, dz_ref, dw_ref, ds_ref):
        g = pl.program_id(0)
        pooled, cnt = _pool_window_mean_minus_id(p_ref[...], g)
        pooled_b = pooled.astype(BF)
        w = w_ref[...]
        mixed = _dot(pooled_b, w)
        dpo = d_ref[...].astype(F32)
        ds_ref[...] = _rows8(dpo * mixed)
        dmixed = (dpo * s_ref[...]).astype(BF)
        dw_ref[...] = _dot_tn(pooled_b, dmixed)
        dpooled = _dot_nt(dmixed, w)
        dz_ref[...] = _pool_window_transpose(dpooled, cnt, g).astype(BF)

    return _call(
        body, grid=(ng,),
        in_specs=[pl.BlockSpec((lp, gw), lambda g: (0, g)), pl.BlockSpec((lp, gw), lambda g: (0, g)),
                  pl.BlockSpec((None, gw, gw), lambda g: (g, 0, 0)), pl.BlockSpec((1, gw), lambda g: (0, g))],
        out_specs=[pl.BlockSpec((lp, gw), lambda g: (0, g)), pl.BlockSpec((None, gw, gw), lambda g: (g, 0, 0)),
                   pl.BlockSpec((SUBLANES, gw), lambda g: (0, g))],
        out_shape=[SDS((lp, ng * gw), BF), SDS((ng, gw, gw), F32), SDS((SUBLANES, ng * gw), F32)],
        vmem_mib=48, name=name, comm=comm)(z, dmix, pool_w, pool_scale)


def _log_sigmoid(x):
    return jnp.minimum(x, 0.0) - jnp.log(1.0 + jnp.exp(-jnp.abs(x)))


def _fox_prep(z, bfp, fblk, name, comm=None):
    lp = z.shape[0]
    nb = lp // LANES

    def body(f_ref, b_ref, cum_ref):
        r = lax.broadcasted_iota(jnp.int32, (LANES, LANES), 0)
        c = lax.broadcasted_iota(jnp.int32, (LANES, LANES), 1)
        tri = (r >= c).astype(F32)
        carry = jnp.zeros((1, LANES), F32)
        for blk in range(nb):
            sl = slice(blk * LANES, (blk + 1) * LANES)
            lf = _log_sigmoid(f_ref[sl, :] + b_ref[...])
            cb = jnp.dot(tri, lf, preferred_element_type=F32, precision=lax.Precision.HIGHEST) + carry
            cum_ref[sl, :] = cb
            carry = cb[LANES - 1:LANES, :]

    return _call(
        body, grid=(1,),
        in_specs=[pl.BlockSpec((lp, LANES), lambda i: (0, fblk)), pl.BlockSpec((1, LANES), lambda i: (0, 0))],
        out_specs=pl.BlockSpec((lp, LANES), lambda i: (0, 0)), out_shape=SDS((lp, LANES), F32),
        vmem_mib=32, name=name, comm=comm)(z, bfp)


def _fox_bwd(z, bfp, dcum, fblk, name, comm=None):
    lp = z.shape[0]
    nb = lp // LANES

    def body(f_ref, b_ref, dc_ref, dz_ref, db_ref):
        r = lax.broadcasted_iota(jnp.int32, (LANES, LANES), 0)
        c = lax.broadcasted_iota(jnp.int32, (LANES, LANES), 1)
        tri = (r <= c).astype(F32)
        carry = jnp.zeros((1, LANES), F32)
        acc = jnp.zeros((SUBLANES, LANES), F32)
        for blk in range(nb - 1, -1, -1):
            sl = slice(blk * LANES, (blk + 1) * LANES)
            dlf = jnp.dot(tri, dc_ref[sl, :], preferred_element_type=F32, precision=lax.Precision.HIGHEST) + carry
            carry = dlf[0:1, :]
            df = dlf * jax.nn.sigmoid(-(f_ref[sl, :] + b_ref[...]))
            dz_ref[sl, :] = df.astype(BF)
            acc = acc + _rows8(df)
        db_ref[...] = acc

    return _call(
        body, grid=(1,),
        in_specs=[pl.BlockSpec((lp, LANES), lambda i: (0, fblk)), pl.BlockSpec((1, LANES), lambda i: (0, 0)),
                  pl.BlockSpec((lp, LANES), lambda i: (0, 0))],
        out_specs=[pl.BlockSpec((lp, LANES), lambda i: (0, 0)), pl.BlockSpec((SUBLANES, LANES), lambda i: (0, 0))],
        out_shape=[SDS((lp, LANES), BF), SDS((SUBLANES, LANES), F32)],
        vmem_mib=32, name=name, comm=comm)(z, bfp, dcum)


def _att_scores(q_ref, cum_ref, cumt_ref, qw_ref, kn_s, h, i, tq, lk):
    scale = 1.0 / (HEAD_DIM ** 0.5)
    q = q_ref[...]
    rq = _rstd(q)
    qhat = q * rq
    qn = (qhat * qw_ref[...]).astype(BF)
    s = _dot_nt(qn, kn_s[0:lk, :]) * scale
    lane = lax.broadcasted_iota(jnp.int32, (tq, LANES), 1)
    cq = jnp.sum(jnp.where(lane == h, cum_ref[...], 0.0), axis=1, keepdims=True)
    ck = cumt_ref[pl.ds(h, 1), 0:lk]
    s = s + (cq - ck)
    qpos = i * tq + lax.broadcasted_iota(jnp.int32, (tq, lk), 0)
    kpos = lax.broadcasted_iota(jnp.int32, (tq, lk), 1)
    s = jnp.where(qpos >= kpos, s, NEG_BIG)
    e = jnp.exp(s - jnp.max(s, axis=1, keepdims=True))
    p = e / jnp.sum(e, axis=1, keepdims=True)
    return p, qn, qhat, rq


def _per_query_tile(i, nq, tq, lp, fn):
    for t in range(nq):
        lk = min(lp, -(-((t + 1) * tq) // LANES) * LANES)
        pl.when(i == t)(functools.partial(fn, lk))


def _att_fwd(z, cum, cumt, qw, kw, n_heads, qblk0, tq, name, comm=None):
    lp = z.shape[0]
    nh = n_heads

    def body(q_ref, k_ref, v_ref, cum_ref, cumt_ref, qw_ref, kw_ref, o_ref, kn_s, vb_s):
        h, i = pl.program_id(0), pl.program_id(1)

        @pl.when(i == 0)
        def _():
            k = k_ref[...]
            kn_s[...] = (k * _rstd(k) * kw_ref[...]).astype(BF)
            vb_s[...] = v_ref[...].astype(BF)

        def tile(lk):
            p, _, _, _ = _att_scores(q_ref, cum_ref, cumt_ref, qw_ref, kn_s, h, i, tq, lk)
            o_ref[...] = _dot(p.astype(BF), vb_s[0:lk, :]).astype(BF)

        _per_query_tile(i, lp // tq, tq, lp, tile)

    vec = pl.BlockSpec((1, HEAD_DIM), lambda h, i: (0, 0))
    return _call(
        body, grid=(nh, lp // tq),
        in_specs=[pl.BlockSpec((tq, HEAD_DIM), lambda h, i: (i, qblk0 + h)),
                  pl.BlockSpec((lp, HEAD_DIM), lambda h, i: (0, qblk0 + nh + h)),
                  pl.BlockSpec((lp, HEAD_DIM), lambda h, i: (0, qblk0 + 2 * nh + h)),
                  pl.BlockSpec((tq, LANES), lambda h, i: (i, 0)),
                  pl.BlockSpec((nh, lp), lambda h, i: (0, 0)), vec, vec],
        out_specs=pl.BlockSpec((tq, HEAD_DIM), lambda h, i: (i, h)),
        out_shape=SDS((lp, nh * HEAD_DIM), BF),
        scratch_shapes=[pltpu.VMEM((lp, HEAD_DIM), BF), pltpu.VMEM((lp, HEAD_DIM), BF)],
        vmem_mib=48, name=name, comm=comm)(z, z, z, cum, cumt, qw, kw)


def _att_bwd(z, cum, cumt, qw, kw, dmix, n_heads, qblk0, oblk0, tq, name, comm=None):
    lp = z.shape[0]
    nh = n_heads
    nq = lp // tq
    scale = 1.0 / (HEAD_DIM ** 0.5)

    def body(q_ref, k_ref, v_ref, cum_ref, cumt_ref, qw_ref, kw_ref, do_ref,
             dq_ref, dk_ref, dv_ref, dcq_ref, dck_ref, dqw_ref, dkw_ref,
             kn_s, vb_s, dkn_s, dv_s, dck_s):
        h, i = pl.program_id(0), pl.program_id(1)

        @pl.when((h == 0) & (i == 0))
        def _():
            dqw_ref[...] = jnp.zeros_like(dqw_ref)
            dkw_ref[...] = jnp.zeros_like(dkw_ref)

        @pl.when(i == 0)
        def _():
            k = k_ref[...]
            kn_s[...] = (k * _rstd(k) * kw_ref[...]).astype(BF)
            vb_s[...] = v_ref[...].astype(BF)
            dkn_s[...] = jnp.zeros_like(dkn_s)
            dv_s[...] = jnp.zeros_like(dv_s)
            dck_s[...] = jnp.zeros_like(dck_s)

        def tile(lk):
            p, qn, qhat, rq = _att_scores(q_ref, cum_ref, cumt_ref, qw_ref, kn_s, h, i, tq, lk)
            dob = do_ref[...]
            dp = _dot_nt(dob, vb_s[0:lk, :])
            ds = p * (dp - jnp.sum(p * dp, axis=1, keepdims=True))
            dsb = ds.astype(BF)
            dv_s[0:lk, :] += _dot_tn(p.astype(BF), dob)
            dkn_s[0:lk, :] += _dot_tn(dsb, qn)
            dcq_ref[...] = jnp.sum(ds, axis=1, keepdims=True)
            dck_s[:, 0:lk] += jnp.sum(ds, axis=0, keepdims=True)
            dqn = _dot(dsb, kn_s[0:lk, :]) * scale
            gq = dqn * qw_ref[...]
            dq_ref[...] = (rq * (gq - qhat * jnp.mean(gq * qhat, axis=-1, keepdims=True))).astype(BF)
            dqw_ref[...] += _rows8(dqn * qhat)

        _per_query_tile(i, nq, tq, lp, tile)

        @pl.when(i == nq - 1)
        def _():
            k = k_ref[...]
            rk = _rstd(k)
            khat = k * rk
            dkn = dkn_s[...] * scale
            gk = dkn * kw_ref[...]
            dk_ref[...] = (rk * (gk - khat * jnp.mean(gk * khat, axis=-1, keepdims=True))).astype(BF)
            dkw_ref[...] += _rows8(dkn * khat)
            dv_ref[...] = dv_s[...].astype(BF)
            dck_ref[...] = dck_s[...]

    vec = pl.BlockSpec((1, HEAD_DIM), lambda h, i: (0, 0))
    part = pl.BlockSpec((SUBLANES, LANES), lambda h, i: (0, 0))
    return _call(
        body, grid=(nh, nq),
        in_specs=[pl.BlockSpec((tq, HEAD_DIM), lambda h, i: (i, qblk0 + h)),
                  pl.BlockSpec((lp, HEAD_DIM), lambda h, i: (0, qblk0 + nh + h)),
                  pl.BlockSpec((lp, HEAD_DIM), lambda h, i: (0, qblk0 + 2 * nh + h)),
                  pl.BlockSpec((tq, LANES), lambda h, i: (i, 0)),
                  pl.BlockSpec((nh, lp), lambda h, i: (0, 0)), vec, vec,
                  pl.BlockSpec((tq, HEAD_DIM), lambda h, i: (i, oblk0 + h))],
        out_specs=[pl.BlockSpec((tq, HEAD_DIM), lambda h, i: (i, h)),
                   pl.BlockSpec((lp, HEAD_DIM), lambda h, i: (0, h)),
                   pl.BlockSpec((lp, HEAD_DIM), lambda h, i: (0, h)),
                   pl.BlockSpec((None, tq, 1), lambda h, i: (h, i, 0)),
                   pl.BlockSpec((None, 1, lp), lambda h, i: (h, 0, 0)),
                   part, part],
        out_shape=[SDS((lp, nh * HEAD_DIM), BF)] * 3
        + [SDS((nh, lp, 1), F32), SDS((nh, 1, lp), F32), SDS((SUBLANES, LANES), F32), SDS((SUBLANES, LANES), F32)],
        scratch_shapes=[pltpu.VMEM((lp, HEAD_DIM), BF), pltpu.VMEM((lp, HEAD_DIM), BF),
                        pltpu.VMEM((lp, HEAD_DIM), F32), pltpu.VMEM((lp, HEAD_DIM), F32),
                        pltpu.VMEM((1, lp), F32)],
        vmem_mib=56, name=name, comm=comm)(z, z, z, cum, cumt, qw, kw, dmix)


def _adamw_math(w, g, m, v):
    m2 = ADAM_B1 * m + (1.0 - ADAM_B1) * g
    v2 = ADAM_B2 * v + (1.0 - ADAM_B2) * (g * g)
    m_hat = m2 / (1.0 - ADAM_B1 ** ADAM_STEP)
    v_hat = v2 / (1.0 - ADAM_B2 ** ADAM_STEP)
    delta = -ADAM_LR * (m_hat / (jnp.sqrt(v_hat) + ADAM_EPS) + ADAM_WD * w)
    return delta, m2, v2


def _adamw(g_in, w, m, v, name, comm=None):
    r, c = w.shape
    partial_sum = g_in.ndim == 3
    lane_padded = -(-c // LANES) * LANES
    tr = _largest_tile(r, max(16, MIB // (4 * lane_padded) // 16 * 16), 16)

    def body(g_ref, w_ref, m_ref, v_ref, go_ref, d_ref, mo_ref, vo_ref):
        if partial_sum:
            g = g_ref[0].astype(F32)
            for k in range(1, g_in.shape[0]):
                g = g + g_ref[k].astype(F32)
        else:
            g = g_ref[...]
        delta, m2, v2 = _adamw_math(w_ref[...], g, m_ref[...], v_ref[...])
        go_ref[...] = g
        d_ref[...] = delta
        mo_ref[...] = m2
        vo_ref[...] = v2

    blk = pl.BlockSpec((tr, c), lambda i: (i, 0))
    g_spec = pl.BlockSpec((g_in.shape[0], tr, c), lambda i: (0, i, 0)) if partial_sum else blk
    return _call(
        body, grid=(r // tr,), in_specs=[g_spec, blk, blk, blk], out_specs=[blk] * 4,
        out_shape=[SDS((r, c), F32)] * 4, vmem_mib=40, name=name, comm=comm)(g_in, w, m, v)


def _peer(x, y, c, k):
    return (1 - x if k & 4 else x, 1 - y if k & 2 else y, 1 - c if k & 1 else c)


def _exchange(arrs, scatter, name, comm=None):
    n = len(arrs)

    def body(*refs):
        ins, outs = refs[:n], refs[n:2 * n]
        send_sems, recv_sems, local_sems = refs[2 * n:]
        x, y, c = lax.axis_index("x"), lax.axis_index("y"), lax.axis_index("c")
        me = 4 * x + 2 * y + c

        def src(t, dev):
            return ins[t].at[dev] if scatter else ins[t]

        def copy(t, k, arrival):
            px, py, pc = _peer(x, y, c, k)
            dev = 4 * px + 2 * py + pc
            return pltpu.make_async_remote_copy(
                src_ref=src(t, dev), dst_ref=outs[t].at[dev if arrival else me],
                send_sem=send_sems.at[t, k - 1], recv_sem=recv_sems.at[t, k - 1],
                device_id=(px, py, pc), device_id_type=pl.DeviceIdType.MESH)

        local = [pltpu.make_async_copy(src(t, me), outs[t].at[me], local_sems.at[t]) for t in range(n)]
        for cp in local:
            cp.start()
        pairs = [(t, k) for k in range(1, N_DEV) for t in range(n)]
        for t, k in pairs:
            copy(t, k, False).start()
        for cp in local:
            cp.wait()
        for t, k in pairs:
            copy(t, k, True).wait_recv()
        for t, k in pairs:
            copy(t, k, False).wait_send()

    out_shape = [SDS(a.shape if scatter else (N_DEV,) + a.shape, a.dtype) for a in arrs]
    anyspec = pl.BlockSpec(memory_space=pl.ANY)
    return pl.pallas_call(
        body, in_specs=[anyspec] * n, out_specs=[anyspec] * n, out_shape=out_shape,
        scratch_shapes=[pltpu.SemaphoreType.DMA((n, N_DEV - 1)), pltpu.SemaphoreType.DMA((n, N_DEV - 1)),
                        pltpu.SemaphoreType.DMA((n,))],
        name=name)(*arrs)


_SIBLING = 1
_ICI_RELS = (2, 4, 6)


def _mesh_pos():
    return lax.axis_index("x"), lax.axis_index("y"), lax.axis_index("c")


def _dev(pos):
    return 4 * pos[0] + 2 * pos[1] + pos[2]


def _gather_ici(shards):
    n = len(shards)
    rels = (_SIBLING,) + _ICI_RELS

    def remote(ins, outs, sems, arrival):
        x, y, c = _mesh_pos()
        cps = []
        for j, k in enumerate(rels):
            peer = _peer(x, y, c, k)
            slot = _dev(peer) if arrival else _dev((x, y, c))
            for t in range(n):
                cps.append(pltpu.make_async_remote_copy(
                    src_ref=ins[t], dst_ref=outs[t].at[slot], send_sem=sems[0].at[t, j], recv_sem=sems[1].at[t, j],
                    device_id=peer, device_id_type=pl.DeviceIdType.MESH))
        return cps

    def local(ins, outs, sems):
        me = _dev(_mesh_pos())
        return [pltpu.make_async_copy(ins[t], outs[t].at[me], sems[2].at[t]) for t in range(n)]

    def start(ins, outs, sems):
        for cp in local(ins, outs, sems) + remote(ins, outs, sems, False):
            cp.start()

    def finish(ins, outs, sems):
        for cp in local(ins, outs, sems):
            cp.wait()
        for cp in remote(ins, outs, sems, True):
            cp.wait_recv()
        for cp in remote(ins, outs, sems, False):
            cp.wait_send()

    return _Comm(shards, [SDS((N_DEV,) + s.shape, s.dtype) for s in shards],
                 [pltpu.SemaphoreType.DMA((n, len(rels))), pltpu.SemaphoreType.DMA((n, len(rels))),
                  pltpu.SemaphoreType.DMA((n,))], start, finish)


def _gather_fwd(partial):
    n = len(partial)

    def copies(ins, outs, sems, arrival):
        x, y, c = _mesh_pos()
        sibling = _peer(x, y, c, _SIBLING)
        cps = []
        for j, k in enumerate(_ICI_RELS):
            slot = _dev(_peer(x, y, c, k | _SIBLING if arrival else k))
            for t in range(n):
                cps.append(pltpu.make_async_remote_copy(
                    src_ref=ins[t].at[slot], dst_ref=outs[t].at[slot], send_sem=sems[0].at[t, j],
                    recv_sem=sems[1].at[t, j], device_id=sibling, device_id_type=pl.DeviceIdType.MESH))
        return cps

    def start(ins, outs, sems):
        for cp in copies(ins, outs, sems, False):
            cp.start()

    def finish(ins, outs, sems):
        for cp in copies(ins, outs, sems, True):
            cp.wait_recv()
        for cp in copies(ins, outs, sems, False):
            cp.wait_send()

    return _Comm(partial, [SDS(a.shape, a.dtype) for a in partial],
                 [pltpu.SemaphoreType.DMA((n, len(_ICI_RELS)))] * 2, start, finish,
                 aliases={t: t for t in range(n)})


def _scatter_sibling(slots):
    n = len(slots)

    def copies(ins, outs, sems):
        x, y, c = _mesh_pos()
        return [pltpu.make_async_remote_copy(
            src_ref=ins[t].at[:, 1 - c], dst_ref=outs[t], send_sem=sems[0].at[t], recv_sem=sems[1].at[t],
            device_id=_peer(x, y, c, _SIBLING), device_id_type=pl.DeviceIdType.MESH) for t in range(n)]

    def start(ins, outs, sems):
        for cp in copies(ins, outs, sems):
            cp.start()

    def finish(ins, outs, sems):
        for cp in copies(ins, outs, sems):
            cp.wait()

    return _Comm(slots, [SDS((s.shape[0],) + s.shape[2:], s.dtype) for s in slots],
                 [pltpu.SemaphoreType.DMA((n,))] * 2, start, finish)


def _scatter_ici(chip_sums):
    n = len(chip_sums)

    def remote(ins, outs, sems, arrival):
        x, y, c = _mesh_pos()
        cps = []
        for j, k in enumerate(_ICI_RELS):
            peer = _peer(x, y, c, k)
            theirs, mine = 2 * peer[0] + peer[1], 2 * x + y
            for t in range(n):
                cps.append(pltpu.make_async_remote_copy(
                    src_ref=ins[t].at[theirs], dst_ref=outs[t].at[theirs if arrival else mine],
                    send_sem=sems[0].at[t, j], recv_sem=sems[1].at[t, j],
                    device_id=peer, device_id_type=pl.DeviceIdType.MESH))
        return cps

    def local(ins, outs, sems):
        x, y, _ = _mesh_pos()
        return [pltpu.make_async_copy(ins[t].at[2 * x + y], outs[t].at[2 * x + y], sems[2].at[t]) for t in range(n)]

    def start(ins, outs, sems):
        for cp in local(ins, outs, sems) + remote(ins, outs, sems, False):
            cp.start()

    def finish(ins, outs, sems):
        for cp in local(ins, outs, sems):
            cp.wait()
        for cp in remote(ins, outs, sems, True):
            cp.wait_recv()
        for cp in remote(ins, outs, sems, False):
            cp.wait_send()

    return _Comm(chip_sums, [SDS(a.shape, a.dtype) for a in chip_sums],
                 [pltpu.SemaphoreType.DMA((n, len(_ICI_RELS))), pltpu.SemaphoreType.DMA((n, len(_ICI_RELS))),
                  pltpu.SemaphoreType.DMA((n,))], start, finish)


def _chip_sum(slots, from_sibling, core, name):
    nq, _, r, c = slots.shape
    tr = _largest_tile(r, 1024, 16)

    def body(core_ref, a_ref, b_ref, o_ref):
        o_ref[...] = (a_ref[...].astype(F32) + b_ref[...].astype(F32)).astype(BF)

    return pl.pallas_call(
        body,
        grid_spec=pltpu.PrefetchScalarGridSpec(
            num_scalar_prefetch=1, grid=(nq, r // tr),
            in_specs=[pl.BlockSpec((None, None, tr, c), lambda q, i, core_ref: (q, core_ref[0], i, 0)),
                      pl.BlockSpec((None, tr, c), lambda q, i, core_ref: (q, i, 0))],
            out_specs=pl.BlockSpec((None, tr, c), lambda q, i, core_ref: (q, i, 0))),
        out_shape=SDS((nq, r, c), BF), compiler_params=pltpu.CompilerParams(vmem_limit_bytes=40 * MIB),
        name=name)(core, slots, from_sibling)


def _small_reduce(pack_g, meta_g, loss_scale, name, comm=None):
    w = pack_g.shape[2]

    def body(p_ref, m_ref, tot_ref, meta_ref, loss_ref):
        acc = p_ref[0]
        macc = m_ref[0]
        for k in range(1, N_DEV):
            acc = acc + p_ref[k]
            macc = macc + m_ref[k]
        tot = jnp.sum(acc, axis=0, keepdims=True)
        tot_ref[...] = tot
        meta_ref[...] = macc
        loss_ref[...] = jnp.full((1, LANES), loss_scale * jnp.sum(tot[:, w - LANES:w]), F32)

    return pl.pallas_call(
        body, out_shape=[SDS((1, w), F32), SDS(meta_g.shape[1:], F32), SDS((1, LANES), F32)],
        compiler_params=pltpu.CompilerParams(vmem_limit_bytes=32 * MIB), name=name)(pack_g, meta_g)


def _local_step(x, target, sw, plan):
    s_len, d = x.shape
    n_heads = plan.n_heads
    plan.at("start")
    meta = plan.weights("meta")
    n_meta = meta.shape[0]
    l = n_meta + s_len
    lp = -(-l // LANES) * LANES
    tm = _largest_tile(lp, 544, 16)
    tq = _largest_tile(lp, 272, 16)
    te = _largest_tile(lp, 272, 16)
    tmd = _largest_tile(d, 512, LANES)

    zpad = jnp.zeros((lp - l, d), F32)
    h0 = jnp.concatenate([meta, x, zpad], axis=0)
    tpad = jnp.concatenate([jnp.zeros((n_meta, d), F32), target, zpad], axis=0)

    wg1, wu1, wd1 = plan.weights("ffn1")
    fs = wg1.shape[1]
    h1, a1, b1, u1 = _ffn_fwd(h0, sw["ffn1_norm"], wg1, wu1, wd1, tm, "ffn1_fwd", plan.comm("ffn1_fwd"))
    plan.at("after_ffn1_fwd")
    win, pw, wout = plan.weights("mix")
    nz = win.shape[1]
    p_w = sw["pool_scale"].shape[1]
    npb = p_w // LANES
    fblk = nz // LANES - 1
    tnz = _largest_tile(nz, 1408, LANES)
    qw, kw, bfp, ps = sw["q_norm"], sw["k_norm"], sw["b_forget"], sw["pool_scale"]
    z, u2 = _norm_matmul(h1, sw["mix_norm"], win, tm, tnz, "mix_in", plan.comm("mix_in"))
    cum = _fox_prep(z, bfp, fblk, "fox_prep")
    cumt = cum[:, :n_heads].T
    pool_o = _pool_fwd(z, pw, ps, "pool_fwd")
    att_o = _att_fwd(z, cum, cumt, qw, kw, n_heads, npb, tq, "att_fwd", plan.comm("att_fwd"))
    h2 = _out_proj(h1, pool_o, att_o, wout, tm, "out_proj", plan.comm("out_proj"))
    wg2, wu2, wd2 = plan.weights("ffn2")
    h3, a2, b2, u3 = _ffn_fwd(h2, sw["ffn2_norm"], wg2, wu2, wd2, tm, "ffn2_fwd", plan.comm("ffn2_fwd"))
    dy, dob3, lsq = _loss_head(h3, tpad, n_meta, l, te, "loss_head")

    du3, da2, db2, hid2 = _ffn_bwd_dx(dob3, a2, b2, wg2, wu2, wd2, tm, "ffn2_bwd_dx", plan.comm("ffn2_bwd_dx"))
    dh2, dh2b, dn2 = _rms_bwd(du3, h2, sw["ffn2_norm"], dy, 1.0, te, "ffn2_rms_bwd")
    plan.grad("ffn2_w_gate", _matmul_tn(da2, u3, fs, d, "ffn2_dwg", plan.comm("ffn2_dwg")))
    plan.grad("ffn2_w_up", _matmul_tn(db2, u3, fs, d, "ffn2_dwu", plan.comm("ffn2_dwu")))
    plan.grad("ffn2_w_down", _matmul_tn(hid2, dob3, fs, d, "ffn2_dwd", plan.comm("ffn2_dwd")))

    dmix = _matmul_nt(dh2b, wout, tm, d, BF, "out_proj_bwd", plan.comm("out_proj_bwd"))
    tmp = _largest_tile(p_w, 512, LANES)
    plan.grad("w_out", jnp.concatenate([_matmul_tn(pool_o, dh2b, tmp, d, "dwout_pool"),
                                        _matmul_tn(att_o, dh2b, tmp, d, "dwout_att")], axis=0))
    dzp, dpw, dps = _pool_bwd(z, dmix, pw, ps, "pool_bwd")
    plan.grad("pool_w", dpw)
    plan.at("before_att_bwd")
    dq, dk, dv, dcq, dck, dqw, dkw = _att_bwd(z, cum, cumt, qw, kw, dmix, n_heads, npb, npb, tq, "att_bwd",
                                              plan.comm("att_bwd"))
    dcum = dcq[:, :, 0].T - dck[:, 0, :].T
    dcum = jnp.pad(dcum, ((0, 0), (0, LANES - n_heads)))
    dzf, dbf = _fox_bwd(z, bfp, dcum, fblk, "fox_bwd")
    dz = jnp.concatenate([dzp, dq, dk, dv, dzf], axis=1)
    plan.grad("w_in", _matmul_tn(u2, dz, tmd, tnz, "dwin", plan.comm("dwin")))
    du2 = _matmul_nt(dz, win, tm, tnz, F32, "mix_in_bwd", plan.comm("mix_in_bwd"))
    plan.at("before_ffn1_bwd_dx")
    dh1, dob1, dnm = _rms_bwd(du2, h1, sw["mix_norm"], dh2, 0.5, te, "mix_rms_bwd")

    du1, da1, db1, hid1 = _ffn_bwd_dx(dob1, a1, b1, wg1, wu1, wd1, tm, "ffn1_bwd_dx", plan.comm("ffn1_bwd_dx"))
    dh0, _, dn1 = _rms_bwd(du1, h0, sw["ffn1_norm"], dh1, 1.0, te, "ffn1_rms_bwd")
    plan.grad("ffn1_w_gate", _matmul_tn(da1, u1, fs, d, "ffn1_dwg", plan.comm("ffn1_dwg")))
    plan.grad("ffn1_w_up", _matmul_tn(db1, u1, fs, d, "ffn1_dwu", plan.comm("ffn1_dwu")))
    plan.at("before_ffn1_dwd")
    plan.grad("ffn1_w_down", _matmul_tn(hid1, dob1, fs, d, "ffn1_dwd", plan.comm("ffn1_dwd")))

    small = [dn1, dnm, dn2, dps, dqw, dkw, dbf, lsq]
    return dh0[n_meta:l], dh0[:n_meta], small


_BIG = ("ffn1_w_gate", "ffn1_w_up", "ffn1_w_down", "w_in", "pool_w", "w_out", "ffn2_w_gate", "ffn2_w_up", "ffn2_w_down")
_SMALL = ("ffn1_norm", "mix_norm", "ffn2_norm", "pool_scale", "q_norm", "k_norm", "b_forget")
_ORDER = ("meta_tokens", "ffn1_norm", "ffn1_w_gate", "ffn1_w_up", "ffn1_w_down", "mix_norm", "w_in", "b_forget",
          "q_norm", "k_norm", "pool_w", "pool_scale", "w_out", "ffn2_norm", "ffn2_w_gate", "ffn2_w_up", "ffn2_w_down")


_FFN1 = ("ffn1_w_gate", "ffn1_w_up", "ffn1_w_down")
_FFN2 = ("ffn2_w_gate", "ffn2_w_up", "ffn2_w_down")
_MIX = ("w_in", "pool_w", "w_out")

_RIDES = {
    "ffn1_fwd": (("g1", _MIX + ("ffn2_w_down",)),),
    "mix_in": (("g1", ("ffn2_w_up",)), ("g2", ("ffn2_w_down",))),
    "att_fwd": (("g1", ("ffn2_w_gate",)),),
    "out_proj": (("g2", ("ffn2_w_gate", "ffn2_w_up")),),
    "ffn2_dwu": (("s1", ("ffn2_w_gate",)),),
    "ffn2_dwd": (("s1", ("ffn2_w_up",)),),
    "out_proj_bwd": (("s1", ("ffn2_w_down",)),),
    "att_bwd": (("s2", ("ffn2_w_gate", "ffn2_w_up")),),
    "mix_in_bwd": (("s1", _MIX),),
    "ffn1_bwd_dx": (("s2", ("ffn2_w_down",) + _MIX),),
    "ffn1_dwu": (("s1", ("ffn1_w_gate",)),),
    "ffn1_dwd": (("s2", ("ffn1_w_gate",)), ("s1", ("ffn1_w_up",))),
    "adamw_ffn2_w_gate": (("s2", ("ffn1_w_up",)), ("s1", ("ffn1_w_down",))),
    "adamw_ffn2_w_up": (("s2", ("ffn1_w_down",)),),
}
_POINTS = {
    "start": (("alone", "g1", _FFN1 + ("meta_tokens",)), ("alone", "g2", _FFN1 + ("meta_tokens",))),
    "after_ffn1_fwd": (("alone", "g2", _MIX),),
    "before_att_bwd": (("sum", ("ffn2_w_gate", "ffn2_w_up")),),
    "before_ffn1_bwd_dx": (("sum", ("ffn2_w_down",) + _MIX),),
    "before_ffn1_dwd": (("sum", ("ffn1_w_gate",)),),
    "before_adamw_ffn2_w_gate": (("sum", ("ffn1_w_up",)),),
    "before_adamw_ffn2_w_up": (("sum", ("ffn1_w_down",)),),
}


class _MeshPlan:
    def __init__(self, shards, core, d, d_in, n_heads):
        self.shard, self.core = dict(shards), core
        self.d, self.d_in, self.n_heads = d, d_in, n_heads
        self.partial, self.full, self.slots, self.from_sibling, self.chip_sum, self.received = {}, {}, {}, {}, {}, {}
        self.pending = []

    def _phase(self, kind, names):
        src, dst, make = {"g1": (self.shard, self.partial, _gather_ici), "g2": (self.partial, self.full, _gather_fwd),
                          "s1": (self.slots, self.from_sibling, _scatter_sibling),
                          "s2": (self.chip_sum, self.received, _scatter_ici)}[kind]
        op = make([src[n] for n in names])
        self.pending.append((op, dst, names))
        return op

    def _settle(self):
        for op, dst, names in self.pending:
            dst.update(zip(names, op.results))
        self.pending = []

    def comm(self, kernel_name):
        self._settle()
        return _merge_comm([self._phase(kind, names) for kind, names in _RIDES.get(kernel_name, ())])

    def at(self, point):
        for step in _POINTS.get(point, ()):
            self._settle()
            if step[0] == "alone":
                _comm_alone(self._phase(step[1], step[2]), "_".join((step[1], point)))
            else:
                for n in step[1]:
                    self.chip_sum[n] = _chip_sum(self.slots[n], self.from_sibling[n], self.core, "chip_sum_" + n)

    def weights(self, group):
        self._settle()
        f, d = self.full, self.d
        if group == "meta":
            g = f["meta_tokens"]
            return g.transpose(1, 0, 2).reshape(g.shape[1], d)
        if group == "ffn1":
            return tuple(f[n] for n in _FFN1)
        if group == "ffn2":
            return tuple(f[n] for n in _FFN2)
        n_main = self.d_in - self.n_heads
        win = f["w_in"].transpose(1, 0, 2).reshape(d, self.d_in)
        win = jnp.concatenate([win[:, :n_main], jnp.pad(win[:, n_main:], ((0, 0), (0, LANES - self.n_heads)))], axis=1)
        pw = f["pool_w"]
        pw = pw.transpose(1, 0, 2, 3).reshape(pw.shape[1], pw.shape[3], pw.shape[3])
        return win, pw, f["w_out"].reshape(-1, d)

    def grad(self, name, g):
        d = self.d
        if name == "w_in":
            g = g[:, :self.d_in].reshape(d, N_DEV, -1).transpose(1, 0, 2)
        elif name == "pool_w":
            ng, gw = g.shape[0], g.shape[2]
            g = g.astype(BF).reshape(ng, N_DEV, -1, gw).transpose(1, 0, 2, 3).reshape(N_DEV, -1, gw)
        elif name == "w_out":
            g = g.reshape(N_DEV, -1, d)
        self.slots[name] = g.reshape((N_DEV // 2, 2) + g.shape[1:])

    def gradient_parts(self, name):
        self._settle()
        return self.received[name]


_TRANSPOSED = ("ffn1_w_gate", "ffn1_w_up", "ffn2_w_gate", "ffn2_w_up")


def _as2d(name, a):
    return a[0].T if name in _TRANSPOSED else a.reshape(-1, a.shape[-1])


def _from2d(name, a2d, shape):
    return a2d.T.reshape(shape) if name in _TRANSPOSED else a2d.reshape(shape)


def kernel(x, meta_tokens, ffn1_norm, ffn1_w_gate, ffn1_w_up, ffn1_w_down, mix_norm, w_in, b_forget, q_norm, k_norm, pool_w, pool_scale, w_out, ffn2_norm, ffn2_w_gate, ffn2_w_up, ffn2_w_down, loss_target, m_meta_tokens, m_ffn1_norm, m_ffn1_w_gate, m_ffn1_w_up, m_ffn1_w_down, m_mix_norm, m_w_in, m_b_forget, m_q_norm, m_k_norm, m_pool_w, m_pool_scale, m_w_out, m_ffn2_norm, m_ffn2_w_gate, m_ffn2_w_up, m_ffn2_w_down, v_meta_tokens, v_ffn1_norm, v_ffn1_w_gate, v_ffn1_w_up, v_ffn1_w_down, v_mix_norm, v_w_in, v_b_forget, v_q_norm, v_k_norm, v_pool_w, v_pool_scale, v_w_out, v_ffn2_norm, v_ffn2_w_gate, v_ffn2_w_up, v_ffn2_w_down):
    w = dict(meta_tokens=meta_tokens, ffn1_norm=ffn1_norm, ffn1_w_gate=ffn1_w_gate, ffn1_w_up=ffn1_w_up,
             ffn1_w_down=ffn1_w_down, mix_norm=mix_norm, w_in=w_in, b_forget=b_forget, q_norm=q_norm, k_norm=k_norm,
             pool_w=pool_w, pool_scale=pool_scale, w_out=w_out, ffn2_norm=ffn2_norm, ffn2_w_gate=ffn2_w_gate,
             ffn2_w_up=ffn2_w_up, ffn2_w_down=ffn2_w_down)
    m = dict(meta_tokens=m_meta_tokens, ffn1_norm=m_ffn1_norm, ffn1_w_gate=m_ffn1_w_gate, ffn1_w_up=m_ffn1_w_up,
             ffn1_w_down=m_ffn1_w_down, mix_norm=m_mix_norm, w_in=m_w_in, b_forget=m_b_forget, q_norm=m_q_norm,
             k_norm=m_k_norm, pool_w=m_pool_w, pool_scale=m_pool_scale, w_out=m_w_out, ffn2_norm=m_ffn2_norm,
             ffn2_w_gate=m_ffn2_w_gate, ffn2_w_up=m_ffn2_w_up, ffn2_w_down=m_ffn2_w_down)
    v = dict(meta_tokens=v_meta_tokens, ffn1_norm=v_ffn1_norm, ffn1_w_gate=v_ffn1_w_gate, ffn1_w_up=v_ffn1_w_up,
             ffn1_w_down=v_ffn1_w_down, mix_norm=v_mix_norm, w_in=v_w_in, b_forget=v_b_forget, q_norm=v_q_norm,
             k_norm=v_k_norm, pool_w=v_pool_w, pool_scale=v_pool_scale, w_out=v_w_out, ffn2_norm=v_ffn2_norm,
             ffn2_w_gate=v_ffn2_w_gate, ffn2_w_up=v_ffn2_w_up, ffn2_w_down=v_ffn2_w_down)

    d = x.shape[-1]
    n_heads = b_forget.shape[-1]
    me = 4 * lax.axis_index("x") + 2 * lax.axis_index("y") + lax.axis_index("c")
    core = lax.axis_index("c").astype(jnp.int32).reshape(1)

    shards = {k: (_as2d(k, w[k]) if k in _TRANSPOSED else w[k][0]).astype(BF) for k in _BIG}
    shards["meta_tokens"] = meta_tokens
    plan = _MeshPlan(shards, core, d, N_DEV * w_in.shape[-1], n_heads)
    sw = {k: w[k] for k in _SMALL}
    sw["b_forget"] = jnp.pad(b_forget, ((0, 0), (0, LANES - n_heads)))
    dx, dmeta, small = _local_step(x[0], loss_target[0], sw, plan)

    res = {}
    for k in _FFN2 + _MIX + _FFN1:
        plan.at("before_adamw_" + k)
        res[k] = _adamw(plan.gradient_parts(k), _as2d(k, w[k]), _as2d(k, m[k]), _as2d(k, v[k]), "adamw_" + k,
                        plan.comm("adamw_" + k))

    pack = jnp.concatenate(small, axis=1)
    pack_g, meta_g = _exchange([pack, dmeta], False, "gather_small")
    tot, dmeta_tot, loss_row = _small_reduce(pack_g, meta_g, 0.5 / d, "small_reduce")

    mcols = meta_tokens.shape[1]
    g_meta = lax.dynamic_slice_in_dim(dmeta_tot, me * mcols, mcols, axis=1)
    res["meta_tokens"] = _adamw(g_meta, meta_tokens, m_meta_tokens, v_meta_tokens, "adamw_meta_tokens")

    def packed(src):
        return jnp.concatenate([src[k] for k in _SMALL[:-1]] + [jnp.pad(src["b_forget"], ((0, 0), (0, LANES - n_heads)))],
                               axis=1)

    wp = packed(w)
    sm = _adamw(tot[:, :wp.shape[1]], wp, packed(m), packed(v), "adamw_small")
    off = 0
    for k in _SMALL:
        width = w[k].shape[1]
        res[k] = tuple(o[:, off:off + width] for o in sm)
        off += width if k != "b_forget" else LANES

    outs = [loss_row[0, 0], dx[None]]
    for idx in range(4):
        outs += [_from2d(k, res[k][idx], w[k].shape) for k in _ORDER]
    return tuple(outs)
```

```python
import functools

import jax
import jax.numpy as jnp
from jax import lax
from jax.experimental import pallas as pl
from jax.experimental.pallas import tpu as pltpu

F32 = jnp.float32
BF = jnp.bfloat16
SDS = jax.ShapeDtypeStruct

N_DEV = 8
LANES = 128
SUBLANES = 8
HEAD_DIM = 128
POOL_WINDOWS = (2, 4, 8, 16)
RMS_EPS = 1e-6
NEG_BIG = -1e30
MIB = 1024 * 1024

ADAM_LR = 0.001
ADAM_B1 = 0.9
ADAM_B2 = 0.999
ADAM_EPS = 1e-08
ADAM_WD = 0.01
ADAM_STEP = 10


class _Comm:
    def __init__(self, arrs, out_shape, sems, start, finish, aliases=None):
        self.arrs, self.out_shape, self.sems = list(arrs), list(out_shape), list(sems)
        self.start, self.finish, self.aliases = start, finish, dict(aliases or {})
        self.results = None


def _merge_comm(ops):
    ops = [op for op in ops if op is not None]
    if not ops:
        return None
    na, no, ns = [0], [0], [0]
    for op in ops:
        na.append(na[-1] + len(op.arrs))
        no.append(no[-1] + len(op.out_shape))
        ns.append(ns[-1] + len(op.sems))

    def parts(i, ins, outs, sems):
        return ins[na[i]:na[i + 1]], outs[no[i]:no[i + 1]], sems[ns[i]:ns[i + 1]]

    def start(ins, outs, sems):
        for i, op in enumerate(ops):
            op.start(*parts(i, ins, outs, sems))

    def finish(ins, outs, sems):
        for i, op in enumerate(ops):
            op.finish(*parts(i, ins, outs, sems))

    aliases = {}
    for i, op in enumerate(ops):
        for a, o in op.aliases.items():
            aliases[na[i] + a] = no[i] + o
    merged = _Comm([a for op in ops for a in op.arrs], [s for op in ops for s in op.out_shape],
                   [s for op in ops for s in op.sems], start, finish, aliases)
    merged.children = (ops, no)
    return merged


def _deliver(comm, results):
    comm.results = list(results)
    if hasattr(comm, "children"):
        ops, no = comm.children
        for i, op in enumerate(ops):
            _deliver(op, results[no[i]:no[i + 1]])


def _call(body, *, grid, in_specs, out_specs, out_shape, scratch_shapes=(), vmem_mib, name, comm=None):
    single = not isinstance(out_shape, (list, tuple))
    out_specs = [out_specs] if single else list(out_specs)
    out_shape = [out_shape] if single else list(out_shape)
    in_specs, scratch_shapes = list(in_specs), list(scratch_shapes)
    params = pltpu.CompilerParams(dimension_semantics=("arbitrary",) * len(grid), vmem_limit_bytes=vmem_mib * MIB)
    n_in, n_out, n_scr = len(in_specs), len(out_specs), len(scratch_shapes)

    def run(*args):
        if comm is None:
            res = pl.pallas_call(body, grid=grid, in_specs=in_specs, out_specs=out_specs, out_shape=out_shape,
                                 scratch_shapes=scratch_shapes, compiler_params=params, name=name)(*args)
            return res[0] if single else res
        ci, co = len(comm.arrs), len(comm.out_shape)

        def with_comm(*refs):
            ins, cins = refs[:n_in], refs[n_in:n_in + ci]
            o0 = n_in + ci
            outs, couts = refs[o0:o0 + n_out], refs[o0 + n_out:o0 + n_out + co]
            s0 = o0 + n_out + co
            scr, csems = refs[s0:s0 + n_scr], refs[s0 + n_scr:]
            ids = [pl.program_id(a) for a in range(len(grid))]
            first = functools.reduce(jnp.logical_and, [i == 0 for i in ids])
            last = functools.reduce(jnp.logical_and, [i == g - 1 for i, g in zip(ids, grid)])

            @pl.when(first)
            def _():
                comm.start(cins, couts, csems)

            body(*ins, *outs, *scr)

            @pl.when(last)
            def _():
                comm.finish(cins, couts, csems)

        anyspec = pl.BlockSpec(memory_space=pl.ANY)
        res = pl.pallas_call(
            with_comm, grid=grid, in_specs=in_specs + [anyspec] * ci, out_specs=out_specs + [anyspec] * co,
            out_shape=out_shape + comm.out_shape, scratch_shapes=scratch_shapes + comm.sems,
            input_output_aliases={n_in + a: n_out + o for a, o in comm.aliases.items()},
            compiler_params=params, name=name)(*args, *comm.arrs)
        _deliver(comm, res[n_out:])
        return res[0] if single else res[:n_out]

    return run


def _comm_alone(comm, name):
    def body(*refs):
        ci, co = len(comm.arrs), len(comm.out_shape)
        ins, outs, sems = refs[:ci], refs[ci:ci + co], refs[ci + co:]
        comm.start(ins, outs, sems)
        comm.finish(ins, outs, sems)

    anyspec = pl.BlockSpec(memory_space=pl.ANY)
    res = pl.pallas_call(
        body, in_specs=[anyspec] * len(comm.arrs), out_specs=[anyspec] * len(comm.out_shape),
        out_shape=comm.out_shape, scratch_shapes=comm.sems, input_output_aliases=comm.aliases, name=name)(*comm.arrs)
    _deliver(comm, res)


def _split_start(comm, name):
    na, ns = len(comm.arrs), len(comm.sems)

    def body(*refs):
        comm.start(refs[:na], None, refs[na:na + ns])
        token = refs[-1]
        token[...] = jnp.zeros_like(token)

    hbm = pl.BlockSpec(memory_space=pltpu.HBM)
    res = pl.pallas_call(
        body, name=name,
        out_shape=tuple(comm.sems) + tuple(pltpu.HBM(a.shape, a.dtype) for a in comm.arrs)
        + (SDS((SUBLANES, LANES), F32),),
        in_specs=[hbm] * na,
        out_specs=[pl.BlockSpec(memory_space=pltpu.SEMAPHORE)] * ns + [hbm] * na + [pl.BlockSpec(memory_space=pltpu.VMEM)],
        input_output_aliases={i: ns + i for i in range(na)},
        compiler_params=pltpu.CompilerParams(has_side_effects=pltpu.SideEffectType.DATAFLOW_SIDE_EFFECTING),
    )(*[pltpu.with_memory_space_constraint(a, pltpu.HBM) for a in comm.arrs])
    return (comm, res[:ns], res[ns:ns + na]), res[-1]


def _split_wait(started, after, name):
    comm, sems, thru = started
    na, ns = len(thru), len(sems)

    def body(*refs):
        comm.finish(refs[:na], None, refs[na:na + ns])

    hbm = pl.BlockSpec(memory_space=pltpu.HBM)
    res = pl.pallas_call(
        body, name=name, out_shape=tuple(pltpu.HBM(a.shape, a.dtype) for a in thru),
        in_specs=[hbm] * na + [pl.BlockSpec(memory_space=pltpu.SEMAPHORE)] * ns + [pl.BlockSpec(memory_space=pl.ANY)],
        out_specs=[hbm] * na, input_output_aliases={i: i for i in range(na)},
        compiler_params=pltpu.CompilerParams(has_side_effects=pltpu.SideEffectType.DATAFLOW_SIDE_EFFECTING),
    )(*thru, *sems, after)
    return res[na - len(comm.out_shape):]


def _largest_tile(n, cap, mult):
    if n <= cap:
        return n
    best = None
    for t in range(mult, cap + 1, mult):
        if n % t == 0:
            best = t
    assert best is not None, (n, cap, mult)
    return best


def _dot(a, b):
    return jnp.dot(a, b, preferred_element_type=F32)


def _dot_nt(a, b):
    return lax.dot_general(a, b, (((1,), (1,)), ((), ())), preferred_element_type=F32)


def _dot_tn(a, b):
    return lax.dot_general(a, b, (((0,), (0,)), ((), ())), preferred_element_type=F32)


def _rows8(x):
    t, c = x.shape
    return jnp.sum(x.reshape(t // SUBLANES, SUBLANES, c), axis=0)


def _rstd(x):
    return lax.rsqrt(jnp.mean(x * x, axis=-1, keepdims=True) + RMS_EPS)


def _ffn_fwd(h, g, wg, wu, wd, tm, name, comm=None):
    lp, d = h.shape
    ns, fs, _ = wg.shape

    def body(h_ref, g_ref, wg_ref, wu_ref, wd_ref, out_ref, a_ref, b_ref, u_ref, acc_ref):
        j = pl.program_id(1)

        @pl.when(j == 0)
        def _():
            hh = h_ref[...]
            u_ref[...] = (hh * _rstd(hh) * g_ref[...]).astype(BF)
            acc_ref[...] = jnp.zeros_like(acc_ref)

        u = u_ref[...]
        a = _dot_nt(u, wg_ref[...])
        b = _dot_nt(u, wu_ref[...])
        a_ref[...] = a.astype(BF)
        b_ref[...] = b.astype(BF)
        hid = (a * jax.nn.sigmoid(a) * b).astype(BF)
        acc_ref[...] += _dot(hid, wd_ref[...])

        @pl.when(j == ns - 1)
        def _():
            out_ref[...] = h_ref[...] + 0.5 * acc_ref[...]

    row = pl.BlockSpec((tm, d), lambda i, j: (i, 0))
    act = pl.BlockSpec((None, tm, fs), lambda i, j: (j, i, 0))
    return _call(
        body, grid=(lp // tm, ns),
        in_specs=[row, pl.BlockSpec((1, d), lambda i, j: (0, 0)),
                  pl.BlockSpec((None, fs, d), lambda i, j: (j, 0, 0)),
                  pl.BlockSpec((None, fs, d), lambda i, j: (j, 0, 0)),
                  pl.BlockSpec((None, fs, d), lambda i, j: (j, 0, 0))],
        out_specs=[row, act, act, row],
        out_shape=[SDS((lp, d), F32), SDS((ns, lp, fs), BF), SDS((ns, lp, fs), BF), SDS((lp, d), BF)],
        scratch_shapes=[pltpu.VMEM((tm, d), F32)],
        vmem_mib=56, name=name, comm=comm)(h, g, wg, wu, wd)


def _ffn_bwd_dx(dob, a, b, wg, wu, wd, tm, name, comm=None):
    lp, d = dob.shape
    ns, fs, _ = wg.shape

    def body(do_ref, a_ref, b_ref, wg_ref, wu_ref, wd_ref, du_ref, da_ref, db_ref, hid_ref):
        j = pl.program_id(1)

        @pl.when(j == 0)
        def _():
            du_ref[...] = jnp.zeros_like(du_ref)

        dhid = _dot_nt(do_ref[...], wd_ref[...])
        av = a_ref[...].astype(F32)
        bv = b_ref[...].astype(F32)
        sig = jax.nn.sigmoid(av)
        sil = av * sig
        dbv = (dhid * sil).astype(BF)
        dav = (dhid * bv * (sig * (1.0 + av * (1.0 - sig)))).astype(BF)
        hid_ref[...] = (sil * bv).astype(BF)
        da_ref[...] = dav
        db_ref[...] = dbv
        du_ref[...] += _dot(dav, wg_ref[...]) + _dot(dbv, wu_ref[...])

    row = pl.BlockSpec((tm, d), lambda i, j: (i, 0))
    act = pl.BlockSpec((None, tm, fs), lambda i, j: (j, i, 0))
    return _call(
        body, grid=(lp // tm, ns),
        in_specs=[row, act, act,
                  pl.BlockSpec((None, fs, d), lambda i, j: (j, 0, 0)),
                  pl.BlockSpec((None, fs, d), lambda i, j: (j, 0, 0)),
                  pl.BlockSpec((None, fs, d), lambda i, j: (j, 0, 0))],
        out_specs=[row, act, act, act],
        out_shape=[SDS((lp, d), F32)] + [SDS((ns, lp, fs), BF)] * 3,
        vmem_mib=56, name=name, comm=comm)(dob, a, b, wg, wu, wd)


def _rms_bwd(du, h, g, dres, bscale, tm, name, comm=None):
    lp, d = h.shape

    def body(du_ref, h_ref, g_ref, dres_ref, dh_ref, dhb_ref, dg_ref):
        @pl.when(pl.program_id(0) == 0)
        def _():
            dg_ref[...] = jnp.zeros_like(dg_ref)

        hh = h_ref[...]
        r = _rstd(hh)
        xhat = hh * r
        duv = du_ref[...]
        dg_ref[...] += _rows8(duv * xhat)
        dxh = duv * g_ref[...]
        dh = dres_ref[...] + r * (dxh - xhat * jnp.mean(dxh * xhat, axis=-1, keepdims=True))
        dh_ref[...] = dh
        dhb_ref[...] = (bscale * dh).astype(BF)

    row = pl.BlockSpec((tm, d), lambda i: (i, 0))
    return _call(
        body, grid=(lp // tm,),
        in_specs=[row, row, pl.BlockSpec((1, d), lambda i: (0, 0)), row],
        out_specs=[row, row, pl.BlockSpec((SUBLANES, d), lambda i: (0, 0))],
        out_shape=[SDS((lp, d), F32), SDS((lp, d), BF), SDS((SUBLANES, d), F32)],
        vmem_mib=48, name=name, comm=comm)(du, h, g, dres)


def _matmul_tn(a, b, tm, tn, name, comm=None):
    a_b, b_b = a.ndim == 3, b.ndim == 3
    ns = a.shape[0] if a_b else (b.shape[0] if b_b else 1)
    l, m = a.shape[-2:]
    n = b.shape[-1]

    def body(a_ref, b_ref, o_ref):
        o_ref[...] = _dot_tn(a_ref[...], b_ref[...]).astype(o_ref.dtype)

    a_spec = (pl.BlockSpec((None, l, tm), lambda s, i, j: (s, 0, i)) if a_b
              else pl.BlockSpec((l, tm), lambda s, i, j: (0, i)))
    b_spec = (pl.BlockSpec((None, l, tn), lambda s, i, j: (s, 0, j)) if b_b
              else pl.BlockSpec((l, tn), lambda s, i, j: (0, j)))
    batched = a_b or b_b
    o_spec = (pl.BlockSpec((None, tm, tn), lambda s, i, j: (s, i, j)) if batched
              else pl.BlockSpec((tm, tn), lambda s, i, j: (i, j)))
    o_shape = SDS((ns, m, n), BF) if batched else SDS((m, n), BF)
    return _call(
        body, grid=(ns, m // tm, n // tn), in_specs=[a_spec, b_spec], out_specs=o_spec, out_shape=o_shape,
        vmem_mib=48, name=name, comm=comm)(a, b)


def _matmul_nt(x, w, tm, tk, out_dtype, name, comm=None):
    l, k = x.shape
    n = w.shape[0]
    nk = k // tk

    def body(x_ref, w_ref, o_ref, acc_ref):
        kk = pl.program_id(1)

        @pl.when(kk == 0)
        def _():
            acc_ref[...] = jnp.zeros_like(acc_ref)

        acc_ref[...] += _dot_nt(x_ref[...], w_ref[...])

        @pl.when(kk == nk - 1)
        def _():
            o_ref[...] = acc_ref[...].astype(o_ref.dtype)

    return _call(
        body, grid=(l // tm, nk),
        in_specs=[pl.BlockSpec((tm, tk), lambda i, kk: (i, kk)), pl.BlockSpec((n, tk), lambda i, kk: (0, kk))],
        out_specs=pl.BlockSpec((tm, n), lambda i, kk: (i, 0)),
        out_shape=SDS((l, n), out_dtype),
        scratch_shapes=[pltpu.VMEM((tm, n), F32)],
        vmem_mib=48, name=name, comm=comm)(x, w)


def _norm_matmul(h, g, w, tm, tn, name, comm=None):
    lp, d = h.shape
    n = w.shape[1]

    def body(h_ref, g_ref, w_ref, z_ref, u_ref):
        @pl.when(pl.program_id(1) == 0)
        def _():
            hh = h_ref[...]
            u_ref[...] = (hh * _rstd(hh) * g_ref[...]).astype(BF)

        z_ref[...] = _dot(u_ref[...], w_ref[...])

    row = pl.BlockSpec((tm, d), lambda i, j: (i, 0))
    return _call(
        body, grid=(lp // tm, n // tn),
        in_specs=[row, pl.BlockSpec((1, d), lambda i, j: (0, 0)), pl.BlockSpec((d, tn), lambda i, j: (0, j))],
        out_specs=[pl.BlockSpec((tm, tn), lambda i, j: (i, j)), row],
        out_shape=[SDS((lp, n), F32), SDS((lp, d), BF)],
        vmem_mib=48, name=name, comm=comm)(h, g, w)


def _out_proj(h, pool_o, att_o, w_out, tm, name, comm=None):
    lp, d = h.shape
    p = pool_o.shape[1]
    dm = w_out.shape[0]

    def body(h_ref, p_ref, a_ref, w_ref, o_ref):
        o_ref[...] = h_ref[...] + _dot(p_ref[...], w_ref[0:p, :]) + _dot(a_ref[...], w_ref[p:dm, :])

    row = pl.BlockSpec((tm, d), lambda i: (i, 0))
    return _call(
        body, grid=(lp // tm,),
        in_specs=[row, pl.BlockSpec((tm, p), lambda i: (i, 0)), pl.BlockSpec((tm, dm - p), lambda i: (i, 0)),
                  pl.BlockSpec((dm, d), lambda i: (0, 0))],
        out_specs=row, out_shape=SDS((lp, d), F32),
        vmem_mib=48, name=name, comm=comm)(h, pool_o, att_o, w_out)


def _loss_head(y, tpad, row0, row1, tm, name, comm=None):
    lp, d = y.shape

    def body(y_ref, t_ref, dy_ref, dob_ref, ls_ref):
        i = pl.program_id(0)

        @pl.when(i == 0)
        def _():
            ls_ref[...] = jnp.zeros_like(ls_ref)

        rows = i * tm + lax.broadcasted_iota(jnp.int32, (tm, d), 0)
        err = jnp.where((rows >= row0) & (rows < row1), y_ref[...] - t_ref[...], 0.0)
        dy = err * (1.0 / d)
        dy_ref[...] = dy
        dob_ref[...] = (0.5 * dy).astype(BF)
        sq = _rows8(err * err)
        acc = sq[:, 0:LANES]
        for c in range(1, d // LANES):
            acc = acc + sq[:, c * LANES:(c + 1) * LANES]
        ls_ref[...] += acc

    row = pl.BlockSpec((tm, d), lambda i: (i, 0))
    return _call(
        body, grid=(lp // tm,), in_specs=[row, row],
        out_specs=[row, row, pl.BlockSpec((SUBLANES, LANES), lambda i: (0, 0))],
        out_shape=[SDS((lp, d), F32), SDS((lp, d), BF), SDS((SUBLANES, LANES), F32)],
        vmem_mib=48, name=name, comm=comm)(y, tpad)


def _window_select(levels, gidx):
    out = levels[-1]
    for k in range(len(levels) - 2, -1, -1):
        out = jnp.where(gidx == k, levels[k], out)
    return out


def _pool_window_mean_minus_id(x, gidx):
    rows = lax.broadcasted_iota(jnp.int32, x.shape, 0)
    levels = []
    s = x
    shift = 1
    while shift < POOL_WINDOWS[-1]:
        s = s + jnp.where(rows >= shift, pltpu.roll(s, shift, 0), 0.0)
        shift *= 2
        if shift in POOL_WINDOWS:
            levels.append(s)
    win = _window_select(levels, gidx)
    cnt = jnp.minimum(rows + 1, _window_select(list(POOL_WINDOWS), gidx)).astype(F32)
    return win / cnt - x, cnt


def _pool_window_transpose(dy, cnt, gidx):
    lp = dy.shape[0]
    rows = lax.broadcasted_iota(jnp.int32, dy.shape, 0)
    levels = []
    s = dy / cnt
    shift = 1
    while shift < POOL_WINDOWS[-1]:
        s = s + jnp.where(rows < lp - shift, pltpu.roll(s, lp - shift, 0), 0.0)
        shift *= 2
        if shift in POOL_WINDOWS:
            levels.append(s)
    return _window_select(levels, gidx) - dy


def _pool_fwd(z, pool_w, pool_scale, name, comm=None):
    lp = z.shape[0]
    ng, gw, _ = pool_w.shape

    def body(p_ref, w_ref, s_ref, o_ref):
        pooled, _ = _pool_window_mean_minus_id(p_ref[...], pl.program_id(0))
        o_ref[...] = (_dot(pooled.astype(BF), w_ref[...]) * s_ref[...]).astype(BF)

    return _call(
        body, grid=(ng,),
        in_specs=[pl.BlockSpec((lp, gw), lambda g: (0, g)), pl.BlockSpec((None, gw, gw), lambda g: (g, 0, 0)),
                  pl.BlockSpec((1, gw), lambda g: (0, g))],
        out_specs=pl.BlockSpec((lp, gw), lambda g: (0, g)), out_shape=SDS((lp, ng * gw), BF),
        vmem_mib=48, name=name, comm=comm)(z, pool_w, pool_scale)


def _pool_bwd(z, dmix, pool_w, pool_scale, name, comm=None):
    lp = z.shape[0]
    ng, gw, _ = pool_w.shape

    def body(p_ref, d_ref, w_ref, s_ref, dz_ref, dw_ref, ds_ref):
        g = pl.program_id(0)
        pooled, cnt = _pool_window_mean_minus_id(p_ref[...], g)
        pooled_b = pooled.astype(BF)
        w = w_ref[...]
        mixed = _dot(pooled_b, w)
        dpo = d_ref[...].astype(F32)
        ds_ref[...] = _rows8(dpo * mixed)
        dmixed = (dpo * s_ref[...]).astype(BF)
        dw_ref[...] = _dot_tn(pooled_b, dmixed)
        dpooled = _dot_nt(dmixed, w)
        dz_ref[...] = _pool_window_transpose(dpooled, cnt, g).astype(BF)

    return _call(
        body, grid=(ng,),
        in_specs=[pl.BlockSpec((lp, gw), lambda g: (0, g)), pl.BlockSpec((lp, gw), lambda g: (0, g)),
                  pl.BlockSpec((None, gw, gw), lambda g: (g, 0, 0)), pl.BlockSpec((1, gw), lambda g: (0, g))],
        out_specs=[pl.BlockSpec((lp, gw), lambda g: (0, g)), pl.BlockSpec((None, gw, gw), lambda g: (g, 0, 0)),
                   pl.BlockSpec((SUBLANES, gw), lambda g: (0, g))],
        out_shape=[SDS((lp, ng * gw), BF), SDS((ng, gw, gw), F32), SDS((SUBLANES, ng * gw), F32)],
        vmem_mib=48, name=name, comm=comm)(z, dmix, pool_w, pool_scale)


def _log_sigmoid(x):
    return jnp.minimum(x, 0.0) - jnp.log(1.0 + jnp.exp(-jnp.abs(x)))


def _fox_prep(z, bfp, fblk, name, comm=None):
    lp = z.shape[0]
    nb = lp // LANES

    def body(f_ref, b_ref, cum_ref):
        r = lax.broadcasted_iota(jnp.int32, (LANES, LANES), 0)
        c = lax.broadcasted_iota(jnp.int32, (LANES, LANES), 1)
        tri = (r >= c).astype(F32)
        carry = jnp.zeros((1, LANES), F32)
        for blk in range(nb):
            sl = slice(blk * LANES, (blk + 1) * LANES)
            lf = _log_sigmoid(f_ref[sl, :] + b_ref[...])
            cb = jnp.dot(tri, lf, preferred_element_type=F32, precision=lax.Precision.HIGHEST) + carry
            cum_ref[sl, :] = cb
            carry = cb[LANES - 1:LANES, :]

    return _call(
        body, grid=(1,),
        in_specs=[pl.BlockSpec((lp, LANES), lambda i: (0, fblk)), pl.BlockSpec((1, LANES), lambda i: (0, 0))],
        out_specs=pl.BlockSpec((lp, LANES), lambda i: (0, 0)), out_shape=SDS((lp, LANES), F32),
        vmem_mib=32, name=name, comm=comm)(z, bfp)


def _fox_bwd(z, bfp, dcum, fblk, name, comm=None):
    lp = z.shape[0]
    nb = lp // LANES

    def body(f_ref, b_ref, dc_ref, dz_ref, db_ref):
        r = lax.broadcasted_iota(jnp.int32, (LANES, LANES), 0)
        c = lax.broadcasted_iota(jnp.int32, (LANES, LANES), 1)
        tri = (r <= c).astype(F32)
        carry = jnp.zeros((1, LANES), F32)
        acc = jnp.zeros((SUBLANES, LANES), F32)
        for blk in range(nb - 1, -1, -1):
            sl = slice(blk * LANES, (blk + 1) * LANES)
            dlf = jnp.dot(tri, dc_ref[sl, :], preferred_element_type=F32, precision=lax.Precision.HIGHEST) + carry
            carry = dlf[0:1, :]
            df = dlf * jax.nn.sigmoid(-(f_ref[sl, :] + b_ref[...]))
            dz_ref[sl, :] = df.astype(BF)
            acc = acc + _rows8(df)
        db_ref[...] = acc

    return _call(
        body, grid=(1,),
        in_specs=[pl.BlockSpec((lp, LANES), lambda i: (0, fblk)), pl.BlockSpec((1, LANES), lambda i: (0, 0)),
                  pl.BlockSpec((lp, LANES), lambda i: (0, 0))],
        out_specs=[pl.BlockSpec((lp, LANES), lambda i: (0, 0)), pl.BlockSpec((SUBLANES, LANES), lambda i: (0, 0))],
        out_shape=[SDS((lp, LANES), BF), SDS((SUBLANES, LANES), F32)],
        vmem_mib=32, name=name, comm=comm)(z, bfp, dcum)


def _att_scores(q_ref, cum_ref, cumt_ref, qw_ref, kn_s, h, i, tq, lk):
    scale = 1.0 / (HEAD_DIM ** 0.5)
    q = q_ref[...]
    rq = _rstd(q)
    qhat = q * rq
    qn = (qhat * qw_ref[...]).astype(BF)
    s = _dot_nt(qn, kn_s[0:lk, :]) * scale
    lane = lax.broadcasted_iota(jnp.int32, (tq, LANES), 1)
    cq = jnp.sum(jnp.where(lane == h, cum_ref[...], 0.0), axis=1, keepdims=True)
    ck = cumt_ref[pl.ds(h, 1), 0:lk]
    s = s + (cq - ck)
    qpos = i * tq + lax.broadcasted_iota(jnp.int32, (tq, lk), 0)
    kpos = lax.broadcasted_iota(jnp.int32, (tq, lk), 1)
    s = jnp.where(qpos >= kpos, s, NEG_BIG)
    e = jnp.exp(s - jnp.max(s, axis=1, keepdims=True))
    p = e / jnp.sum(e, axis=1, keepdims=True)
    return p, qn, qhat, rq


def _per_query_tile(i, nq, tq, lp, fn):
    for t in range(nq):
        lk = min(lp, -(-((t + 1) * tq) // LANES) * LANES)
        pl.when(i == t)(functools.partial(fn, lk))


def _att_fwd(z, cum, cumt, qw, kw, n_heads, qblk0, tq, name, comm=None):
    lp = z.shape[0]
    nh = n_heads

    def body(q_ref, k_ref, v_ref, cum_ref, cumt_ref, qw_ref, kw_ref, o_ref, kn_s, vb_s):
        h, i = pl.program_id(0), pl.program_id(1)

        @pl.when(i == 0)
        def _():
            k = k_ref[...]
            kn_s[...] = (k * _rstd(k) * kw_ref[...]).astype(BF)
            vb_s[...] = v_ref[...].astype(BF)

        def tile(lk):
            p, _, _, _ = _att_scores(q_ref, cum_ref, cumt_ref, qw_ref, kn_s, h, i, tq, lk)
            o_ref[...] = _dot(p.astype(BF), vb_s[0:lk, :]).astype(BF)

        _per_query_tile(i, lp // tq, tq, lp, tile)

    vec = pl.BlockSpec((1, HEAD_DIM), lambda h, i: (0, 0))
    return _call(
        body, grid=(nh, lp // tq),
        in_specs=[pl.BlockSpec((tq, HEAD_DIM), lambda h, i: (i, qblk0 + h)),
                  pl.BlockSpec((lp, HEAD_DIM), lambda h, i: (0, qblk0 + nh + h)),
                  pl.BlockSpec((lp, HEAD_DIM), lambda h, i: (0, qblk0 + 2 * nh + h)),
                  pl.BlockSpec((tq, LANES), lambda h, i: (i, 0)),
                  pl.BlockSpec((nh, lp), lambda h, i: (0, 0)), vec, vec],
        out_specs=pl.BlockSpec((tq, HEAD_DIM), lambda h, i: (i, h)),
        out_shape=SDS((lp, nh * HEAD_DIM), BF),
        scratch_shapes=[pltpu.VMEM((lp, HEAD_DIM), BF), pltpu.VMEM((lp, HEAD_DIM), BF)],
        vmem_mib=48, name=name, comm=comm)(z, z, z, cum, cumt, qw, kw)


def _att_bwd(z, cum, cumt, qw, kw, dmix, n_heads, qblk0, oblk0, tq, name, comm=None):
    lp = z.shape[0]
    nh = n_heads
    nq = lp // tq
    scale = 1.0 / (HEAD_DIM ** 0.5)

    def body(q_ref, k_ref, v_ref, cum_ref, cumt_ref, qw_ref, kw_ref, do_ref,
             dq_ref, dk_ref, dv_ref, dcq_ref, dck_ref, dqw_ref, dkw_ref,
             kn_s, vb_s, dkn_s, dv_s, dck_s):
        h, i = pl.program_id(0), pl.program_id(1)

        @pl.when((h == 0) & (i == 0))
        def _():
            dqw_ref[...] = jnp.zeros_like(dqw_ref)
            dkw_ref[...] = jnp.zeros_like(dkw_ref)

        @pl.when(i == 0)
        def _():
            k = k_ref[...]
            kn_s[...] = (k * _rstd(k) * kw_ref[...]).astype(BF)
            vb_s[...] = v_ref[...].astype(BF)
            dkn_s[...] = jnp.zeros_like(dkn_s)
            dv_s[...] = jnp.zeros_like(dv_s)
            dck_s[...] = jnp.zeros_like(dck_s)

        def tile(lk):
            p, qn, qhat, rq = _att_scores(q_ref, cum_ref, cumt_ref, qw_ref, kn_s, h, i, tq, lk)
            dob = do_ref[...]
            dp = _dot_nt(dob, vb_s[0:lk, :])
            ds = p * (dp - jnp.sum(p * dp, axis=1, keepdims=True))
            dsb = ds.astype(BF)
            dv_s[0:lk, :] += _dot_tn(p.astype(BF), dob)
            dkn_s[0:lk, :] += _dot_tn(dsb, qn)
            dcq_ref[...] = jnp.sum(ds, axis=1, keepdims=True)
            dck_s[:, 0:lk] += jnp.sum(ds, axis=0, keepdims=True)
            dqn = _dot(dsb, kn_s[0:lk, :]) * scale
            gq = dqn * qw_ref[...]
            dq_ref[...] = (rq * (gq - qhat * jnp.mean(gq * qhat, axis=-1, keepdims=True))).astype(BF)
            dqw_ref[...] += _rows8(dqn * qhat)

        _per_query_tile(i, nq, tq, lp, tile)

        @pl.when(i == nq - 1)
        def _():
            k = k_ref[...]
            rk = _rstd(k)
            khat = k * rk
            dkn = dkn_s[...] * scale
            gk = dkn * kw_ref[...]
            dk_ref[...] = (rk * (gk - khat * jnp.mean(gk * khat, axis=-1, keepdims=True))).astype(BF)
            dkw_ref[...] += _rows8(dkn * khat)
            dv_ref[...] = dv_s[...].astype(BF)
            dck_ref[...] = dck_s[...]

    vec = pl.BlockSpec((1, HEAD_DIM), lambda h, i: (0, 0))
    part = pl.BlockSpec((SUBLANES, LANES), lambda h, i: (0, 0))
    return _call(
        body, grid=(nh, nq),
        in_specs=[pl.BlockSpec((tq, HEAD_DIM), lambda h, i: (i, qblk0 + h)),
                  pl.BlockSpec((lp, HEAD_DIM), lambda h, i: (0, qblk0 + nh + h)),
                  pl.BlockSpec((lp, HEAD_DIM), lambda h, i: (0, qblk0 + 2 * nh + h)),
                  pl.BlockSpec((tq, LANES), lambda h, i: (i, 0)),
                  pl.BlockSpec((nh, lp), lambda h, i: (0, 0)), vec, vec,
                  pl.BlockSpec((tq, HEAD_DIM), lambda h, i: (i, oblk0 + h))],
        out_specs=[pl.BlockSpec((tq, HEAD_DIM), lambda h, i: (i, h)),
                   pl.BlockSpec((lp, HEAD_DIM), lambda h, i: (0, h)),
                   pl.BlockSpec((lp, HEAD_DIM), lambda h, i: (0, h)),
                   pl.BlockSpec((None, tq, 1), lambda h, i: (h, i, 0)),
                   pl.BlockSpec((None, 1, lp), lambda h, i: (h, 0, 0)),
                   part, part],
        out_shape=[SDS((lp, nh * HEAD_DIM), BF)] * 3
        + [SDS((nh, lp, 1), F32), SDS((nh, 1, lp), F32), SDS((SUBLANES, LANES), F32), SDS((SUBLANES, LANES), F32)],
        scratch_shapes=[pltpu.VMEM((lp, HEAD_DIM), BF), pltpu.VMEM((lp, HEAD_DIM), BF),
                        pltpu.VMEM((lp, HEAD_DIM), F32), pltpu.VMEM((lp, HEAD_DIM), F32),
                        pltpu.VMEM((1, lp), F32)],
        vmem_mib=56, name=name, comm=comm)(z, z, z, cum, cumt, qw, kw, dmix)


def _adamw_math(w, g, m, v):
    m2 = ADAM_B1 * m + (1.0 - ADAM_B1) * g
    v2 = ADAM_B2 * v + (1.0 - ADAM_B2) * (g * g)
    m_hat = m2 / (1.0 - ADAM_B1 ** ADAM_STEP)
    v_hat = v2 / (1.0 - ADAM_B2 ** ADAM_STEP)
    delta = -ADAM_LR * (m_hat / (jnp.sqrt(v_hat) + ADAM_EPS) + ADAM_WD * w)
    return delta, m2, v2


def _adamw(g_in, w, m, v, name, comm=None):
    r, c = w.shape
    partial_sum = g_in.ndim == 3
    lane_padded = -(-c // LANES) * LANES
    tr = _largest_tile(r, max(16, MIB // (4 * lane_padded) // 16 * 16), 16)

    def body(g_ref, w_ref, m_ref, v_ref, go_ref, d_ref, mo_ref, vo_ref):
        if partial_sum:
            g = g_ref[0].astype(F32)
            for k in range(1, g_in.shape[0]):
                g = g + g_ref[k].astype(F32)
        else:
            g = g_ref[...]
        delta, m2, v2 = _adamw_math(w_ref[...], g, m_ref[...], v_ref[...])
        go_ref[...] = g
        d_ref[...] = delta
        mo_ref[...] = m2
        vo_ref[...] = v2

    blk = pl.BlockSpec((tr, c), lambda i: (i, 0))
    g_spec = pl.BlockSpec((g_in.shape[0], tr, c), lambda i: (0, i, 0)) if partial_sum else blk
    return _call(
        body, grid=(r // tr,), in_specs=[g_spec, blk, blk, blk], out_specs=[blk] * 4,
        out_shape=[SDS((r, c), F32)] * 4, vmem_mib=40, name=name, comm=comm)(g_in, w, m, v)


def _peer(x, y, c, k):
    return (1 - x if k & 4 else x, 1 - y if k & 2 else y, 1 - c if k & 1 else c)


def _exchange(arrs, scatter, name, comm=None):
    n = len(arrs)

    def body(*refs):
        ins, outs = refs[:n], refs[n:2 * n]
        send_sems, recv_sems, local_sems = refs[2 * n:]
        x, y, c = lax.axis_index("x"), lax.axis_index("y"), lax.axis_index("c")
        me = 4 * x + 2 * y + c

        def src(t, dev):
            return ins[t].at[dev] if scatter else ins[t]

        def copy(t, k, arrival):
            px, py, pc = _peer(x, y, c, k)
            dev = 4 * px + 2 * py + pc
            return pltpu.make_async_remote_copy(
                src_ref=src(t, dev), dst_ref=outs[t].at[dev if arrival else me],
                send_sem=send_sems.at[t, k - 1], recv_sem=recv_sems.at[t, k - 1],
                device_id=(px, py, pc), device_id_type=pl.DeviceIdType.MESH)

        local = [pltpu.make_async_copy(src(t, me), outs[t].at[me], local_sems.at[t]) for t in range(n)]
        for cp in local:
            cp.start()
        pairs = [(t, k) for k in range(1, N_DEV) for t in range(n)]
        for t, k in pairs:
            copy(t, k, False).start()
        for cp in local:
            cp.wait()
        for t, k in pairs:
            copy(t, k, True).wait_recv()
        for t, k in pairs:
            copy(t, k, False).wait_send()

    out_shape = [SDS(a.shape if scatter else (N_DEV,) + a.shape, a.dtype) for a in arrs]
    anyspec = pl.BlockSpec(memory_space=pl.ANY)
    return pl.pallas_call(
        body, in_specs=[anyspec] * n, out_specs=[anyspec] * n, out_shape=out_shape,
        scratch_shapes=[pltpu.SemaphoreType.DMA((n, N_DEV - 1)), pltpu.SemaphoreType.DMA((n, N_DEV - 1)),
                        pltpu.SemaphoreType.DMA((n,))],
        name=name)(*arrs)


_SIBLING = 1
_ICI_RELS = (2, 4, 6)


def _mesh_pos():
    return lax.axis_index("x"), lax.axis_index("y"), lax.axis_index("c")


def _sem_pair(sems, t, j, n_rel, scalars):
    if scalars:
        i = 2 * (t * n_rel + j)
        return sems[i], sems[i + 1]
    return sems[0].at[t, j], sems[1].at[t, j]


def _dev(pos):
    return 4 * pos[0] + 2 * pos[1] + pos[2]


def _gather_ici(shards, landing=None):
    n = len(shards)
    rels = (_SIBLING,) + _ICI_RELS

    def remote(ins, outs, sems, arrival):
        x, y, c = _mesh_pos()
        dst = ins[n:] if landing is not None else outs
        cps = []
        for j, k in enumerate(rels):
            peer = _peer(x, y, c, k)
            slot = _dev(peer) if arrival else _dev((x, y, c))
            for t in range(n):
                send_sem, recv_sem = _sem_pair(sems, t, j, len(rels), landing is not None)
                cps.append(pltpu.make_async_remote_copy(
                    src_ref=ins[t], dst_ref=dst[t].at[slot], send_sem=send_sem, recv_sem=recv_sem,
                    device_id=peer, device_id_type=pl.DeviceIdType.MESH))
        return cps

    if landing is not None:
        def start_remote(ins, outs, sems):
            for cp in remote(ins, outs, sems, False):
                cp.start()

        def finish_remote(ins, outs, sems):
            for cp in remote(ins, outs, sems, True):
                cp.wait_recv()
            for cp in remote(ins, outs, sems, False):
                cp.wait_send()

        return _Comm(list(shards) + list(landing), [SDS(a.shape, a.dtype) for a in landing],
                     [pltpu.SemaphoreType.DMA(())] * (2 * n * len(rels)),
                     start_remote, finish_remote, aliases={n + t: t for t in range(n)})

    def local(ins, outs, sems):
        me = _dev(_mesh_pos())
        return [pltpu.make_async_copy(ins[t], outs[t].at[me], sems[2].at[t]) for t in range(n)]

    def start(ins, outs, sems):
        for cp in local(ins, outs, sems) + remote(ins, outs, sems, False):
            cp.start()

    def finish(ins, outs, sems):
        for cp in local(ins, outs, sems):
            cp.wait()
        for cp in remote(ins, outs, sems, True):
            cp.wait_recv()
        for cp in remote(ins, outs, sems, False):
            cp.wait_send()

    return _Comm(shards, [SDS((N_DEV,) + s.shape, s.dtype) for s in shards],
                 [pltpu.SemaphoreType.DMA((n, len(rels))), pltpu.SemaphoreType.DMA((n, len(rels))),
                  pltpu.SemaphoreType.DMA((n,))], start, finish)


def _gather_fwd(partial):
    n = len(partial)

    def copies(ins, outs, sems, arrival):
        x, y, c = _mesh_pos()
        sibling = _peer(x, y, c, _SIBLING)
        cps = []
        for j, k in enumerate(_ICI_RELS):
            slot = _dev(_peer(x, y, c, k | _SIBLING if arrival else k))
            for t in range(n):
                cps.append(pltpu.make_async_remote_copy(
                    src_ref=ins[t].at[slot], dst_ref=outs[t].at[slot], send_sem=sems[0].at[t, j],
                    recv_sem=sems[1].at[t, j], device_id=sibling, device_id_type=pl.DeviceIdType.MESH))
        return cps

    def start(ins, outs, sems):
        for cp in copies(ins, outs, sems, False):
            cp.start()

    def finish(ins, outs, sems):
        for cp in copies(ins, outs, sems, True):
            cp.wait_recv()
        for cp in copies(ins, outs, sems, False):
            cp.wait_send()

    return _Comm(partial, [SDS(a.shape, a.dtype) for a in partial],
                 [pltpu.SemaphoreType.DMA((n, len(_ICI_RELS)))] * 2, start, finish,
                 aliases={t: t for t in range(n)})


def _scatter_sibling(slots):
    n = len(slots)

    def copies(ins, outs, sems):
        x, y, c = _mesh_pos()
        return [pltpu.make_async_remote_copy(
            src_ref=ins[t].at[:, 1 - c], dst_ref=outs[t], send_sem=sems[0].at[t], recv_sem=sems[1].at[t],
            device_id=_peer(x, y, c, _SIBLING), device_id_type=pl.DeviceIdType.MESH) for t in range(n)]

    def start(ins, outs, sems):
        for cp in copies(ins, outs, sems):
            cp.start()

    def finish(ins, outs, sems):
        for cp in copies(ins, outs, sems):
            cp.wait()

    return _Comm(slots, [SDS((s.shape[0],) + s.shape[2:], s.dtype) for s in slots],
                 [pltpu.SemaphoreType.DMA((n,))] * 2, start, finish)


def _scatter_ici(chip_sums, landing=None):
    n = len(chip_sums)

    def remote(ins, outs, sems, arrival):
        x, y, c = _mesh_pos()
        dst = ins[n:] if landing is not None else outs
        cps = []
        for j, k in enumerate(_ICI_RELS):
            peer = _peer(x, y, c, k)
            theirs, mine = 2 * peer[0] + peer[1], 2 * x + y
            for t in range(n):
                send_sem, recv_sem = _sem_pair(sems, t, j, len(_ICI_RELS), landing is not None)
                cps.append(pltpu.make_async_remote_copy(
                    src_ref=ins[t].at[theirs], dst_ref=dst[t].at[theirs if arrival else mine],
                    send_sem=send_sem, recv_sem=recv_sem,
                    device_id=peer, device_id_type=pl.DeviceIdType.MESH))
        return cps

    if landing is not None:
        def start_remote(ins, outs, sems):
            for cp in remote(ins, outs, sems, False):
                cp.start()

        def finish_remote(ins, outs, sems):
            for cp in remote(ins, outs, sems, True):
                cp.wait_recv()
            for cp in remote(ins, outs, sems, False):
                cp.wait_send()

        return _Comm(list(chip_sums) + list(landing), [SDS(a.shape, a.dtype) for a in landing],
                     [pltpu.SemaphoreType.DMA(())] * (2 * n * len(_ICI_RELS)), start_remote, finish_remote,
                     aliases={n + t: t for t in range(n)})

    def local(ins, outs, sems):
        x, y, _ = _mesh_pos()
        return [pltpu.make_async_copy(ins[t].at[2 * x + y], outs[t].at[2 * x + y], sems[2].at[t]) for t in range(n)]

    def start(ins, outs, sems):
        for cp in local(ins, outs, sems) + remote(ins, outs, sems, False):
            cp.start()

    def finish(ins, outs, sems):
        for cp in local(ins, outs, sems):
            cp.wait()
        for cp in remote(ins, outs, sems, True):
            cp.wait_recv()
        for cp in remote(ins, outs, sems, False):
            cp.wait_send()

    return _Comm(chip_sums, [SDS(a.shape, a.dtype) for a in chip_sums],
                 [pltpu.SemaphoreType.DMA((n, len(_ICI_RELS))), pltpu.SemaphoreType.DMA((n, len(_ICI_RELS))),
                  pltpu.SemaphoreType.DMA((n,))], start, finish)


def _chip_sum(slots, from_sibling, core, name):
    nq, _, r, c = slots.shape
    tr = _largest_tile(r, 1024, 16)

    def body(core_ref, a_ref, b_ref, o_ref):
        o_ref[...] = (a_ref[...].astype(F32) + b_ref[...].astype(F32)).astype(BF)

    return pl.pallas_call(
        body,
        grid_spec=pltpu.PrefetchScalarGridSpec(
            num_scalar_prefetch=1, grid=(nq, r // tr),
            in_specs=[pl.BlockSpec((None, None, tr, c), lambda q, i, core_ref: (q, core_ref[0], i, 0)),
                      pl.BlockSpec((None, tr, c), lambda q, i, core_ref: (q, i, 0))],
            out_specs=pl.BlockSpec((None, tr, c), lambda q, i, core_ref: (q, i, 0))),
        out_shape=SDS((nq, r, c), BF), compiler_params=pltpu.CompilerParams(vmem_limit_bytes=40 * MIB),
        name=name)(core, slots, from_sibling)


def _small_reduce(pack_g, meta_g, loss_scale, name, comm=None):
    w = pack_g.shape[2]

    def body(p_ref, m_ref, tot_ref, meta_ref, loss_ref):
        acc = p_ref[0]
        macc = m_ref[0]
        for k in range(1, N_DEV):
            acc = acc + p_ref[k]
            macc = macc + m_ref[k]
        tot = jnp.sum(acc, axis=0, keepdims=True)
        tot_ref[...] = tot
        meta_ref[...] = macc
        loss_ref[...] = jnp.full((1, LANES), loss_scale * jnp.sum(tot[:, w - LANES:w]), F32)

    return pl.pallas_call(
        body, out_shape=[SDS((1, w), F32), SDS(meta_g.shape[1:], F32), SDS((1, LANES), F32)],
        compiler_params=pltpu.CompilerParams(vmem_limit_bytes=32 * MIB), name=name)(pack_g, meta_g)


def _local_step(x, target, sw, plan):
    s_len, d = x.shape
    n_heads = plan.n_heads
    plan.at("start")
    meta = plan.weights("meta")
    n_meta = meta.shape[0]
    l = n_meta + s_len
    lp = -(-l // LANES) * LANES
    tm = _largest_tile(lp, 544, 16)
    tq = _largest_tile(lp, 272, 16)
    te = _largest_tile(lp, 272, 16)
    tmd = _largest_tile(d, 512, LANES)

    zpad = jnp.zeros((lp - l, d), F32)
    h0 = jnp.concatenate([meta, x, zpad], axis=0)
    tpad = jnp.concatenate([jnp.zeros((n_meta, d), F32), target, zpad], axis=0)

    wg1, wu1, wd1 = plan.weights("ffn1")
    fs = wg1.shape[1]
    h1, a1, b1, u1 = _ffn_fwd(h0, sw["ffn1_norm"], wg1, wu1, wd1, tm, "ffn1_fwd", plan.comm("ffn1_fwd"))
    plan.at("after_ffn1_fwd", h1)
    win, pw, wout = plan.weights("mix")
    nz = win.shape[1]
    p_w = sw["pool_scale"].shape[1]
    npb = p_w // LANES
    fblk = nz // LANES - 1
    tnz = _largest_tile(nz, 1408, LANES)
    qw, kw, bfp, ps = sw["q_norm"], sw["k_norm"], sw["b_forget"], sw["pool_scale"]
    z, u2 = _norm_matmul(h1, sw["mix_norm"], win, tm, tnz, "mix_in", plan.comm("mix_in"))
    cum = _fox_prep(z, bfp, fblk, "fox_prep")
    cumt = cum[:, :n_heads].T
    pool_o = _pool_fwd(z, pw, ps, "pool_fwd")
    att_o = _att_fwd(z, cum, cumt, qw, kw, n_heads, npb, tq, "att_fwd", plan.comm("att_fwd"))
    plan.at("after_att_fwd", att_o)
    h2 =_out_proj(h1, pool_o, att_o, wout, tm, "out_proj", plan.comm("out_proj"))
    wg2, wu2, wd2 = plan.weights("ffn2")
    h3, a2, b2, u3 = _ffn_fwd(h2, sw["ffn2_norm"], wg2, wu2, wd2, tm, "ffn2_fwd", plan.comm("ffn2_fwd"))
    dy, dob3, lsq = _loss_head(h3, tpad, n_meta, l, te, "loss_head")

    du3, da2, db2, hid2 = _ffn_bwd_dx(dob3, a2, b2, wg2, wu2, wd2, tm, "ffn2_bwd_dx", plan.comm("ffn2_bwd_dx"))
    dh2, dh2b, dn2 = _rms_bwd(du3, h2, sw["ffn2_norm"], dy, 1.0, te, "ffn2_rms_bwd")
    plan.grad("ffn2_w_gate", _matmul_tn(da2, u3, fs, d, "ffn2_dwg", plan.comm("ffn2_dwg")))
    plan.grad("ffn2_w_up", _matmul_tn(db2, u3, fs, d, "ffn2_dwu", plan.comm("ffn2_dwu")))
    plan.grad("ffn2_w_down", _matmul_tn(hid2, dob3, fs, d, "ffn2_dwd", plan.comm("ffn2_dwd")))
    plan.at("after_ffn2_dwd")

    dmix = _matmul_nt(dh2b, wout, tm, d, BF, "out_proj_bwd", plan.comm("out_proj_bwd"))
    plan.at("after_out_proj_bwd")
    tmp = _largest_tile(p_w, 512, LANES)
    plan.grad("w_out", jnp.concatenate([_matmul_tn(pool_o, dh2b, tmp, d, "dwout_pool"),
                                        _matmul_tn(att_o, dh2b, tmp, d, "dwout_att")], axis=0))
    dzp, dpw, dps = _pool_bwd(z, dmix, pw, ps, "pool_bwd")
    plan.grad("pool_w", dpw)
    plan.at("before_att_bwd")
    dq, dk, dv, dcq, dck, dqw, dkw = _att_bwd(z, cum, cumt, qw, kw, dmix, n_heads, npb, npb, tq, "att_bwd",
                                              plan.comm("att_bwd"))
    dcum = dcq[:, :, 0].T - dck[:, 0, :].T
    dcum = jnp.pad(dcum, ((0, 0), (0, LANES - n_heads)))
    dzf, dbf = _fox_bwd(z, bfp, dcum, fblk, "fox_bwd")
    dz = jnp.concatenate([dzp, dq, dk, dv, dzf], axis=1)
    plan.grad("w_in", _matmul_tn(u2, dz, tmd, tnz, "dwin", plan.comm("dwin")))
    du2 = _matmul_nt(dz, win, tm, tnz, F32, "mix_in_bwd", plan.comm("mix_in_bwd"))
    plan.at("before_ffn1_bwd_dx")
    dh1, dob1, dnm = _rms_bwd(du2, h1, sw["mix_norm"], dh2, 0.5, te, "mix_rms_bwd")

    du1, da1, db1, hid1 = _ffn_bwd_dx(dob1, a1, b1, wg1, wu1, wd1, tm, "ffn1_bwd_dx", plan.comm("ffn1_bwd_dx"))
    dh0, _, dn1 = _rms_bwd(du1, h0, sw["ffn1_norm"], dh1, 1.0, te, "ffn1_rms_bwd")
    plan.grad("ffn1_w_gate", _matmul_tn(da1, u1, fs, d, "ffn1_dwg", plan.comm("ffn1_dwg")))
    plan.grad("ffn1_w_up", _matmul_tn(db1, u1, fs, d, "ffn1_dwu", plan.comm("ffn1_dwu")))
    plan.at("before_ffn1_dwd")
    plan.grad("ffn1_w_down", _matmul_tn(hid1, dob1, fs, d, "ffn1_dwd", plan.comm("ffn1_dwd")))
    plan.at("after_ffn1_dwd")

    small = [dn1, dnm, dn2, dps, dqw, dkw, dbf, lsq]
    return dh0[n_meta:l], dh0[:n_meta], small


_BIG = ("ffn1_w_gate", "ffn1_w_up", "ffn1_w_down", "w_in", "pool_w", "w_out", "ffn2_w_gate", "ffn2_w_up", "ffn2_w_down")
_SMALL = ("ffn1_norm", "mix_norm", "ffn2_norm", "pool_scale", "q_norm", "k_norm", "b_forget")
_ORDER = ("meta_tokens", "ffn1_norm", "ffn1_w_gate", "ffn1_w_up", "ffn1_w_down", "mix_norm", "w_in", "b_forget",
          "q_norm", "k_norm", "pool_w", "pool_scale", "w_out", "ffn2_norm", "ffn2_w_gate", "ffn2_w_up", "ffn2_w_down")


_FFN1 = ("ffn1_w_gate", "ffn1_w_up", "ffn1_w_down")
_FFN2 = ("ffn2_w_gate", "ffn2_w_up", "ffn2_w_down")
_MIX = ("w_in", "pool_w", "w_out")

_RIDES = {
    "out_proj": (("g2", _FFN2),),
    "ffn2_dwu": (("s1", ("ffn2_w_gate",)),),
    "ffn2_dwd": (("s1", ("ffn2_w_up",)),),
    "out_proj_bwd": (("s1", ("ffn2_w_down",)),),
    "mix_in_bwd": (("s1", _MIX),),
    "ffn1_dwu": (("s1", ("ffn1_w_gate",)),),
    "ffn1_dwd": (("s1", ("ffn1_w_up",)),),
    "adamw_ffn2_w_gate": (("s1", ("ffn1_w_down",)),),
}
_G1_FFN1 = _FFN1 + ("meta_tokens",)
_POINTS = {
    "start": (("start", "g1", _G1_FFN1), ("start", "g1", _MIX), ("start", "g1", _FFN2),
              ("wait", "g1", _G1_FFN1), ("alone", "g2", _G1_FFN1)),
    "after_ffn1_fwd": (("wait", "g1", _MIX), ("alone", "g2", _MIX)),
    "after_att_fwd": (("wait", "g1", _FFN2),),
    "after_ffn2_dwd": (("sum", ("ffn2_w_gate",)), ("start", "s2", ("ffn2_w_gate",))),
    "after_out_proj_bwd": (("sum", ("ffn2_w_up",)), ("start", "s2", ("ffn2_w_up",))),
    "before_att_bwd": (("sum", ("ffn2_w_down",)), ("start", "s2", ("ffn2_w_down",))),
    "before_ffn1_bwd_dx": (("sum", _MIX), ("start", "s2", _MIX)),
    "before_ffn1_dwd": (("sum", ("ffn1_w_gate",)), ("start", "s2", ("ffn1_w_gate",))),
    "after_ffn1_dwd": (("sum", ("ffn1_w_up",)), ("start", "s2", ("ffn1_w_up",))),
    "before_adamw_ffn2_w_gate": (("wait", "s2", ("ffn2_w_gate",)),),
    "before_adamw_ffn2_w_up": (("sum", ("ffn1_w_down",)), ("start", "s2", ("ffn1_w_down",)),
                               ("wait", "s2", ("ffn2_w_up",))),
    "before_adamw_ffn2_w_down": (("wait", "s2", ("ffn2_w_down",)),),
    "before_adamw_w_in": (("wait", "s2", _MIX),),
    "before_adamw_ffn1_w_gate": (("wait", "s2", ("ffn1_w_gate",)),),
    "before_adamw_ffn1_w_up": (("wait", "s2", ("ffn1_w_up",)),),
    "before_adamw_ffn1_w_down": (("wait", "s2", ("ffn1_w_down",)),),
}


def _own_slot_filled(block, slot, n_slots):
    zone = lax.empty((n_slots,) + block.shape, block.dtype)
    return lax.dynamic_update_slice(zone, block[None], (slot,) + (0,) * block.ndim)


class _MeshPlan:
    def __init__(self, shards, pos, d, d_in, n_heads):
        self.shard, self.pos = dict(shards), pos
        self.core = pos[2].astype(jnp.int32).reshape(1)
        self.d, self.d_in, self.n_heads = d, d_in, n_heads
        self.partial, self.full, self.slots, self.from_sibling, self.chip_sum, self.received = {}, {}, {}, {}, {}, {}
        self.pending, self.started, self.tokens = [], {}, []

    def _phase(self, kind, names):
        src, dst, make = {"g1": (self.shard, self.partial, _gather_ici), "g2": (self.partial, self.full, _gather_fwd),
                          "s1": (self.slots, self.from_sibling, _scatter_sibling),
                          "s2": (self.chip_sum, self.received, _scatter_ici)}[kind]
        op = make([src[n] for n in names])
        self.pending.append((op, dst, names))
        return op

    def _settle(self):
        for op, dst, names in self.pending:
            dst.update(zip(names, op.results))
        self.pending = []

    def _start(self, kind, names):
        x, y, c = self.pos
        if kind == "g1":
            blocks = [self.shard[n] for n in names]
            op = _gather_ici(blocks, [_own_slot_filled(b, 4 * x + 2 * y + c, N_DEV) for b in blocks])
        else:
            sums = [self.chip_sum[n] for n in names]
            mine = [lax.dynamic_index_in_dim(s, 2 * x + y, 0, keepdims=False) for s in sums]
            op = _scatter_ici(sums, [_own_slot_filled(b, 2 * x + y, N_DEV // 2) for b in mine])
        self.started[(kind, names)], token = _split_start(op, "_".join(("start", kind, names[0])))
        self.tokens.append(token)

    def _wait(self, kind, names, after):
        landed = _split_wait(self.started.pop((kind, names)), after, "_".join(("wait", kind, names[0])))
        (self.partial if kind == "g1" else self.received).update(zip(names, landed))

    def comm(self, kernel_name):
        self._settle()
        ops = [self._phase(kind, names) for kind, names in _RIDES.get(kernel_name, ())]
        if self.tokens:
            ops.append(_Comm(self.tokens, [], [], lambda *a: None, lambda *a: None))
            self.tokens = []
        return _merge_comm(ops)

    def at(self, point, after=None):
        for step in _POINTS.get(point, ()):
            self._settle()
            if step[0] == "alone":
                _comm_alone(self._phase(step[1], step[2]), "_".join((step[1], point)))
            elif step[0] == "start":
                self._start(step[1], step[2])
            elif step[0] == "wait":
                self._wait(step[1], step[2], self.tokens[-1] if after is None else after)
            else:
                for n in step[1]:
                    self.chip_sum[n] = _chip_sum(self.slots[n], self.from_sibling[n], self.core, "chip_sum_" + n)

    def weights(self, group):
        self._settle()
        f, d = self.full, self.d
        if group == "meta":
            g = f["meta_tokens"]
            return g.transpose(1, 0, 2).reshape(g.shape[1], d)
        if group == "ffn1":
            return tuple(f[n] for n in _FFN1)
        if group == "ffn2":
            return tuple(f[n] for n in _FFN2)
        n_main = self.d_in - self.n_heads
        win = f["w_in"].transpose(1, 0, 2).reshape(d, self.d_in)
        win = jnp.concatenate([win[:, :n_main], jnp.pad(win[:, n_main:], ((0, 0), (0, LANES - self.n_heads)))], axis=1)
        pw = f["pool_w"]
        pw = pw.transpose(1, 0, 2, 3).reshape(pw.shape[1], pw.shape[3], pw.shape[3])
        return win, pw, f["w_out"].reshape(-1, d)

    def grad(self, name, g):
        d = self.d
        if name == "w_in":
            g = g[:, :self.d_in].reshape(d, N_DEV, -1).transpose(1, 0, 2)
        elif name == "pool_w":
            ng, gw = g.shape[0], g.shape[2]
            g = g.astype(BF).reshape(ng, N_DEV, -1, gw).transpose(1, 0, 2, 3).reshape(N_DEV, -1, gw)
        elif name == "w_out":
            g = g.reshape(N_DEV, -1, d)
        self.slots[name] = g.reshape((N_DEV // 2, 2) + g.shape[1:])

    def gradient_parts(self, name):
        self._settle()
        return self.received[name]


_TRANSPOSED = ("ffn1_w_gate", "ffn1_w_up", "ffn2_w_gate", "ffn2_w_up")


def _as2d(name, a):
    return a[0].T if name in _TRANSPOSED else a.reshape(-1, a.shape[-1])


def _from2d(name, a2d, shape):
    return a2d.T.reshape(shape) if name in _TRANSPOSED else a2d.reshape(shape)


def kernel(x, meta_tokens, ffn1_norm, ffn1_w_gate, ffn1_w_up, ffn1_w_down, mix_norm, w_in, b_forget, q_norm, k_norm, pool_w, pool_scale, w_out, ffn2_norm, ffn2_w_gate, ffn2_w_up, ffn2_w_down, loss_target, m_meta_tokens, m_ffn1_norm, m_ffn1_w_gate, m_ffn1_w_up, m_ffn1_w_down, m_mix_norm, m_w_in, m_b_forget, m_q_norm, m_k_norm, m_pool_w, m_pool_scale, m_w_out, m_ffn2_norm, m_ffn2_w_gate, m_ffn2_w_up, m_ffn2_w_down, v_meta_tokens, v_ffn1_norm, v_ffn1_w_gate, v_ffn1_w_up, v_ffn1_w_down, v_mix_norm, v_w_in, v_b_forget, v_q_norm, v_k_norm, v_pool_w, v_pool_scale, v_w_out, v_ffn2_norm, v_ffn2_w_gate, v_ffn2_w_up, v_ffn2_w_down):
    w = dict(meta_tokens=meta_tokens, ffn1_norm=ffn1_norm, ffn1_w_gate=ffn1_w_gate, ffn1_w_up=ffn1_w_up,
             ffn1_w_down=ffn1_w_down, mix_norm=mix_norm, w_in=w_in, b_forget=b_forget, q_norm=q_norm, k_norm=k_norm,
             pool_w=pool_w, pool_scale=pool_scale, w_out=w_out, ffn2_norm=ffn2_norm, ffn2_w_gate=ffn2_w_gate,
             ffn2_w_up=ffn2_w_up, ffn2_w_down=ffn2_w_down)
    m = dict(meta_tokens=m_meta_tokens, ffn1_norm=m_ffn1_norm, ffn1_w_gate=m_ffn1_w_gate, ffn1_w_up=m_ffn1_w_up,
             ffn1_w_down=m_ffn1_w_down, mix_norm=m_mix_norm, w_in=m_w_in, b_forget=m_b_forget, q_norm=m_q_norm,
             k_norm=m_k_norm, pool_w=m_pool_w, pool_scale=m_pool_scale, w_out=m_w_out, ffn2_norm=m_ffn2_norm,
             ffn2_w_gate=m_ffn2_w_gate, ffn2_w_up=m_ffn2_w_up, ffn2_w_down=m_ffn2_w_down)
    v = dict(meta_tokens=v_meta_tokens, ffn1_norm=v_ffn1_norm, ffn1_w_gate=v_ffn1_w_gate, ffn1_w_up=v_ffn1_w_up,
             ffn1_w_down=v_ffn1_w_down, mix_norm=v_mix_norm, w_in=v_w_in, b_forget=v_b_forget, q_norm=v_q_norm,
             k_norm=v_k_norm, pool_w=v_pool_w, pool_scale=v_pool_scale, w_out=v_w_out, ffn2_norm=v_ffn2_norm,
             ffn2_w_gate=v_ffn2_w_gate, ffn2_w_up=v_ffn2_w_up, ffn2_w_down=v_ffn2_w_down)

    d = x.shape[-1]
    n_heads = b_forget.shape[-1]
    pos = (lax.axis_index("x"), lax.axis_index("y"), lax.axis_index("c"))
    me = 4 * pos[0] + 2 * pos[1] + pos[2]

    shards = {k: (_as2d(k, w[k]) if k in _TRANSPOSED else w[k][0]).astype(BF) for k in _BIG}
    shards["meta_tokens"] = meta_tokens
    plan = _MeshPlan(shards, pos, d, N_DEV * w_in.shape[-1], n_heads)
    sw = {k: w[k] for k in _SMALL}
    sw["b_forget"] = jnp.pad(b_forget, ((0, 0), (0, LANES - n_heads)))
    dx, dmeta, small = _local_step(x[0], loss_target[0], sw, plan)

    res = {}
    last = dx

    def update_shards(names):
        nonlocal last
        for k in names:
            plan.at("before_adamw_" + k, last)
            res[k] = _adamw(plan.gradient_parts(k), _as2d(k, w[k]), _as2d(k, m[k]), _as2d(k, v[k]), "adamw_" + k,
                            plan.comm("adamw_" + k))
            last = res[k][0]

    update_shards(_FFN2 + _MIX)

    pack = jnp.concatenate(small, axis=1)
    pack_g, meta_g = _exchange([pack, dmeta], False, "gather_small")
    tot, dmeta_tot, loss_row = _small_reduce(pack_g, meta_g, 0.5 / d, "small_reduce")

    mcols = meta_tokens.shape[1]
    g_meta = lax.dynamic_slice_in_dim(dmeta_tot, me * mcols, mcols, axis=1)
    res["meta_tokens"] = _adamw(g_meta, meta_tokens, m_meta_tokens, v_meta_tokens, "adamw_meta_tokens")

    def packed(src):
        return jnp.concatenate([src[k] for k in _SMALL[:-1]] + [jnp.pad(src["b_forget"], ((0, 0), (0, LANES - n_heads)))],
                               axis=1)

    wp = packed(w)
    sm = _adamw(tot[:, :wp.shape[1]], wp, packed(m), packed(v), "adamw_small")
    off = 0
    for k in _SMALL:
        width = w[k].shape[1]
        res[k] = tuple(o[:, off:off + width] for o in sm)
        off += width if k != "b_forget" else LANES

    last = sm[0]
    update_shards(_FFN1)

    outs =[loss_row[0, 0], dx[None]]
    for idx in range(4):
        outs += [_from2d(k, res[k][idx], w[k].shape) for k in _ORDER]
    return tuple(outs)
```

```python
import functools

import jax
import jax.numpy as jnp
from jax import lax
from jax.experimental import pallas as pl
from jax.experimental.pallas import tpu as pltpu

F32 = jnp.float32
BF = jnp.bfloat16
SDS = jax.ShapeDtypeStruct

N_DEV = 8
LANES = 128
SUBLANES = 8
HEAD_DIM = 128
POOL_WINDOWS = (2, 4, 8, 16)
RMS_EPS = 1e-6
NEG_BIG = -1e30
MIB = 1024 * 1024

ADAM_LR = 0.001
ADAM_B1 = 0.9
ADAM_B2 = 0.999
ADAM_EPS = 1e-08
ADAM_WD = 0.01
ADAM_STEP = 10


class _Comm:
    def __init__(self, arrs, out_shape, sems, start, finish, aliases=None):
        self.arrs, self.out_shape, self.sems = list(arrs), list(out_shape), list(sems)
        self.start, self.finish, self.aliases = start, finish, dict(aliases or {})
        self.results = None


def _merge_comm(ops):
    ops = [op for op in ops if op is not None]
    if not ops:
        return None
    na, no, ns = [0], [0], [0]
    for op in ops:
        na.append(na[-1] + len(op.arrs))
        no.append(no[-1] + len(op.out_shape))
        ns.append(ns[-1] + len(op.sems))

    def parts(i, ins, outs, sems):
        return ins[na[i]:na[i + 1]], outs[no[i]:no[i + 1]], sems[ns[i]:ns[i + 1]]

    def start(ins, outs, sems):
        for i, op in enumerate(ops):
            op.start(*parts(i, ins, outs, sems))

    def finish(ins, outs, sems):
        for i, op in enumerate(ops):
            op.finish(*parts(i, ins, outs, sems))

    aliases = {}
    for i, op in enumerate(ops):
        for a, o in op.aliases.items():
            aliases[na[i] + a] = no[i] + o
    merged = _Comm([a for op in ops for a in op.arrs], [s for op in ops for s in op.out_shape],
                   [s for op in ops for s in op.sems], start, finish, aliases)
    merged.children = (ops, no)
    return merged


def _deliver(comm, results):
    comm.results = list(results)
    if hasattr(comm, "children"):
        ops, no = comm.children
        for i, op in enumerate(ops):
            _deliver(op, results[no[i]:no[i + 1]])


def _call(body, *, grid, in_specs, out_specs, out_shape, scratch_shapes=(), vmem_mib, name, comm=None):
    single = not isinstance(out_shape, (list, tuple))
    out_specs = [out_specs] if single else list(out_specs)
    out_shape = [out_shape] if single else list(out_shape)
    in_specs, scratch_shapes = list(in_specs), list(scratch_shapes)
    params = pltpu.CompilerParams(dimension_semantics=("arbitrary",) * len(grid), vmem_limit_bytes=vmem_mib * MIB)
    n_in, n_out, n_scr = len(in_specs), len(out_specs), len(scratch_shapes)

    def run(*args):
        if comm is None:
            res = pl.pallas_call(body, grid=grid, in_specs=in_specs, out_specs=out_specs, out_shape=out_shape,
                                 scratch_shapes=scratch_shapes, compiler_params=params, name=name)(*args)
            return res[0] if single else res
        ci, co = len(comm.arrs), len(comm.out_shape)

        def with_comm(*refs):
            ins, cins = refs[:n_in], refs[n_in:n_in + ci]
            o0 = n_in + ci
            outs, couts = refs[o0:o0 + n_out], refs[o0 + n_out:o0 + n_out + co]
            s0 = o0 + n_out + co
            scr, csems = refs[s0:s0 + n_scr], refs[s0 + n_scr:]
            ids = [pl.program_id(a) for a in range(len(grid))]
            first = functools.reduce(jnp.logical_and, [i == 0 for i in ids])
            last = functools.reduce(jnp.logical_and, [i == g - 1 for i, g in zip(ids, grid)])

            @pl.when(first)
            def _():
                comm.start(cins, couts, csems)

            body(*ins, *outs, *scr)

            @pl.when(last)
            def _():
                comm.finish(cins, couts, csems)

        anyspec = pl.BlockSpec(memory_space=pl.ANY)
        res = pl.pallas_call(
            with_comm, grid=grid, in_specs=in_specs + [anyspec] * ci, out_specs=out_specs + [anyspec] * co,
            out_shape=out_shape + comm.out_shape, scratch_shapes=scratch_shapes + comm.sems,
            input_output_aliases={n_in + a: n_out + o for a, o in comm.aliases.items()},
            compiler_params=params, name=name)(*args, *comm.arrs)
        _deliver(comm, res[n_out:])
        return res[0] if single else res[:n_out]

    return run


def _comm_alone(comm, name):
    def body(*refs):
        ci, co = len(comm.arrs), len(comm.out_shape)
        ins, outs, sems = refs[:ci], refs[ci:ci + co], refs[ci + co:]
        comm.start(ins, outs, sems)
        comm.finish(ins, outs, sems)

    anyspec = pl.BlockSpec(memory_space=pl.ANY)
    res = pl.pallas_call(
        body, in_specs=[anyspec] * len(comm.arrs), out_specs=[anyspec] * len(comm.out_shape),
        out_shape=comm.out_shape, scratch_shapes=comm.sems, input_output_aliases=comm.aliases, name=name)(*comm.arrs)
    _deliver(comm, res)


def _split_start(comm, name):
    na, ns = len(comm.arrs), len(comm.sems)

    def body(*refs):
        comm.start(refs[:na], None, refs[na:na + ns])
        token = refs[-1]
        token[...] = jnp.zeros_like(token)

    hbm = pl.BlockSpec(memory_space=pltpu.HBM)
    res = pl.pallas_call(
        body, name=name,
        out_shape=tuple(comm.sems) + tuple(pltpu.HBM(a.shape, a.dtype) for a in comm.arrs)
        + (SDS((SUBLANES, LANES), F32),),
        in_specs=[hbm] * na,
        out_specs=[pl.BlockSpec(memory_space=pltpu.SEMAPHORE)] * ns + [hbm] * na + [pl.BlockSpec(memory_space=pltpu.VMEM)],
        input_output_aliases={i: ns + i for i in range(na)},
        compiler_params=pltpu.CompilerParams(has_side_effects=pltpu.SideEffectType.DATAFLOW_SIDE_EFFECTING),
    )(*[pltpu.with_memory_space_constraint(a, pltpu.HBM) for a in comm.arrs])
    return (comm, res[:ns], res[ns:ns + na]), res[-1]


def _split_wait(started, after, name):
    comm, sems, thru = started
    na, ns = len(thru), len(sems)

    def body(*refs):
        comm.finish(refs[:na], None, refs[na:na + ns])

    hbm = pl.BlockSpec(memory_space=pltpu.HBM)
    res = pl.pallas_call(
        body, name=name, out_shape=tuple(pltpu.HBM(a.shape, a.dtype) for a in thru),
        in_specs=[hbm] * na + [pl.BlockSpec(memory_space=pltpu.SEMAPHORE)] * ns + [pl.BlockSpec(memory_space=pl.ANY)],
        out_specs=[hbm] * na, input_output_aliases={i: i for i in range(na)},
        compiler_params=pltpu.CompilerParams(has_side_effects=pltpu.SideEffectType.DATAFLOW_SIDE_EFFECTING),
    )(*thru, *sems, after)
    return res[na - len(comm.out_shape):]


def _largest_tile(n, cap, mult):
    if n <= cap:
        return n
    best = None
    for t in range(mult, cap + 1, mult):
        if n % t == 0:
            best = t
    assert best is not None, (n, cap, mult)
    return best


def _dot(a, b):
    return jnp.dot(a, b, preferred_element_type=F32)


def _dot_nt(a, b):
    return lax.dot_general(a, b, (((1,), (1,)), ((), ())), preferred_element_type=F32)


def _dot_tn(a, b):
    return lax.dot_general(a, b, (((0,), (0,)), ((), ())), preferred_element_type=F32)


def _rows8(x):
    t, c = x.shape
    return jnp.sum(x.reshape(t // SUBLANES, SUBLANES, c), axis=0)


def _rstd(x):
    return lax.rsqrt(jnp.mean(x * x, axis=-1, keepdims=True) + RMS_EPS)


def _ffn_fwd(h, g, wg, wu, wd, tm, name, comm=None):
    lp, d = h.shape
    ns, fs, _ = wg.shape

    def body(h_ref, g_ref, wg_ref, wu_ref, wd_ref, out_ref, a_ref, b_ref, u_ref, acc_ref):
        j = pl.program_id(1)

        @pl.when(j == 0)
        def _():
            hh = h_ref[...]
            u_ref[...] = (hh * _rstd(hh) * g_ref[...]).astype(BF)
            acc_ref[...] = jnp.zeros_like(acc_ref)

        u = u_ref[...]
        a = _dot_nt(u, wg_ref[...])
        b = _dot_nt(u, wu_ref[...])
        a_ref[...] = a.astype(BF)
        b_ref[...] = b.astype(BF)
        hid = (a * jax.nn.sigmoid(a) * b).astype(BF)
        acc_ref[...] += _dot(hid, wd_ref[...])

        @pl.when(j == ns - 1)
        def _():
            out_ref[...] = h_ref[...] + 0.5 * acc_ref[...]

    row = pl.BlockSpec((tm, d), lambda i, j: (i, 0))
    act = pl.BlockSpec((None, tm, fs), lambda i, j: (j, i, 0))
    return _call(
        body, grid=(lp // tm, ns),
        in_specs=[row, pl.BlockSpec((1, d), lambda i, j: (0, 0)),
                  pl.BlockSpec((None, fs, d), lambda i, j: (j, 0, 0)),
                  pl.BlockSpec((None, fs, d), lambda i, j: (j, 0, 0)),
                  pl.BlockSpec((None, fs, d), lambda i, j: (j, 0, 0))],
        out_specs=[row, act, act, row],
        out_shape=[SDS((lp, d), F32), SDS((ns, lp, fs), BF), SDS((ns, lp, fs), BF), SDS((lp, d), BF)],
        scratch_shapes=[pltpu.VMEM((tm, d), F32)],
        vmem_mib=56, name=name, comm=comm)(h, g, wg, wu, wd)


def _ffn_bwd_dx(dob, a, b, wg, wu, wd, tm, name, comm=None):
    lp, d = dob.shape
    ns, fs, _ = wg.shape

    def body(do_ref, a_ref, b_ref, wg_ref, wu_ref, wd_ref, du_ref, da_ref, db_ref, hid_ref):
        j = pl.program_id(1)

        @pl.when(j == 0)
        def _():
            du_ref[...] = jnp.zeros_like(du_ref)

        dhid = _dot_nt(do_ref[...], wd_ref[...])
        av = a_ref[...].astype(F32)
        bv = b_ref[...].astype(F32)
        sig = jax.nn.sigmoid(av)
        sil = av * sig
        dbv = (dhid * sil).astype(BF)
        dav = (dhid * bv * (sig * (1.0 + av * (1.0 - sig)))).astype(BF)
        hid_ref[...] = (sil * bv).astype(BF)
        da_ref[...] = dav
        db_ref[...] = dbv
        du_ref[...] += _dot(dav, wg_ref[...]) + _dot(dbv, wu_ref[...])

    row = pl.BlockSpec((tm, d), lambda i, j: (i, 0))
    act = pl.BlockSpec((None, tm, fs), lambda i, j: (j, i, 0))
    return _call(
        body, grid=(lp // tm, ns),
        in_specs=[row, act, act,
                  pl.BlockSpec((None, fs, d), lambda i, j: (j, 0, 0)),
                  pl.BlockSpec((None, fs, d), lambda i, j: (j, 0, 0)),
                  pl.BlockSpec((None, fs, d), lambda i, j: (j, 0, 0))],
        out_specs=[row, act, act, act],
        out_shape=[SDS((lp, d), F32)] + [SDS((ns, lp, fs), BF)] * 3,
        vmem_mib=56, name=name, comm=comm)(dob, a, b, wg, wu, wd)


def _rms_bwd(du, h, g, dres, bscale, tm, name, comm=None):
    lp, d = h.shape

    def body(du_ref, h_ref, g_ref, dres_ref, dh_ref, dhb_ref, dg_ref):
        @pl.when(pl.program_id(0) == 0)
        def _():
            dg_ref[...] = jnp.zeros_like(dg_ref)

        hh = h_ref[...]
        r = _rstd(hh)
        xhat = hh * r
        duv = du_ref[...]
        dg_ref[...] += _rows8(duv * xhat)
        dxh = duv * g_ref[...]
        dh = dres_ref[...] + r * (dxh - xhat * jnp.mean(dxh * xhat, axis=-1, keepdims=True))
        dh_ref[...] = dh
        dhb_ref[...] = (bscale * dh).astype(BF)

    row = pl.BlockSpec((tm, d), lambda i: (i, 0))
    return _call(
        body, grid=(lp // tm,),
        in_specs=[row, row, pl.BlockSpec((1, d), lambda i: (0, 0)), row],
        out_specs=[row, row, pl.BlockSpec((SUBLANES, d), lambda i: (0, 0))],
        out_shape=[SDS((lp, d), F32), SDS((lp, d), BF), SDS((SUBLANES, d), F32)],
        vmem_mib=48, name=name, comm=comm)(du, h, g, dres)


def _matmul_tn(a, b, tm, tn, name, comm=None):
    a_b, b_b = a.ndim == 3, b.ndim == 3
    ns = a.shape[0] if a_b else (b.shape[0] if b_b else 1)
    l, m = a.shape[-2:]
    n = b.shape[-1]

    def body(a_ref, b_ref, o_ref):
        o_ref[...] = _dot_tn(a_ref[...], b_ref[...]).astype(o_ref.dtype)

    a_spec = (pl.BlockSpec((None, l, tm), lambda s, i, j: (s, 0, i)) if a_b
              else pl.BlockSpec((l, tm), lambda s, i, j: (0, i)))
    b_spec = (pl.BlockSpec((None, l, tn), lambda s, i, j: (s, 0, j)) if b_b
              else pl.BlockSpec((l, tn), lambda s, i, j: (0, j)))
    batched = a_b or b_b
    o_spec = (pl.BlockSpec((None, tm, tn), lambda s, i, j: (s, i, j)) if batched
              else pl.BlockSpec((tm, tn), lambda s, i, j: (i, j)))
    o_shape = SDS((ns, m, n), BF) if batched else SDS((m, n), BF)
    return _call(
        body, grid=(ns, m // tm, n // tn), in_specs=[a_spec, b_spec], out_specs=o_spec, out_shape=o_shape,
        vmem_mib=48, name=name, comm=comm)(a, b)


def _matmul_nt(x, w, tm, tk, out_dtype, name, comm=None):
    l, k = x.shape
    n = w.shape[0]
    nk = k // tk

    def body(x_ref, w_ref, o_ref, acc_ref):
        kk = pl.program_id(1)

        @pl.when(kk == 0)
        def _():
            acc_ref[...] = jnp.zeros_like(acc_ref)

        acc_ref[...] += _dot_nt(x_ref[...], w_ref[...])

        @pl.when(kk == nk - 1)
        def _():
            o_ref[...] = acc_ref[...].astype(o_ref.dtype)

    return _call(
        body, grid=(l // tm, nk),
        in_specs=[pl.BlockSpec((tm, tk), lambda i, kk: (i, kk)), pl.BlockSpec((n, tk), lambda i, kk: (0, kk))],
        out_specs=pl.BlockSpec((tm, n), lambda i, kk: (i, 0)),
        out_shape=SDS((l, n), out_dtype),
        scratch_shapes=[pltpu.VMEM((tm, n), F32)],
        vmem_mib=48, name=name, comm=comm)(x, w)


def _norm_matmul(h, g, w, tm, tn, name, comm=None):
    lp, d = h.shape
    n = w.shape[1]

    def body(h_ref, g_ref, w_ref, z_ref, u_ref):
        @pl.when(pl.program_id(1) == 0)
        def _():
            hh = h_ref[...]
            u_ref[...] = (hh * _rstd(hh) * g_ref[...]).astype(BF)

        z_ref[...] = _dot(u_ref[...], w_ref[...])

    row = pl.BlockSpec((tm, d), lambda i, j: (i, 0))
    return _call(
        body, grid=(lp // tm, n // tn),
        in_specs=[row, pl.BlockSpec((1, d), lambda i, j: (0, 0)), pl.BlockSpec((d, tn), lambda i, j: (0, j))],
        out_specs=[pl.BlockSpec((tm, tn), lambda i, j: (i, j)), row],
        out_shape=[SDS((lp, n), F32), SDS((lp, d), BF)],
        vmem_mib=48, name=name, comm=comm)(h, g, w)


def _out_proj(h, pool_o, att_o, w_out, tm, name, comm=None):
    lp, d = h.shape
    p = pool_o.shape[1]
    dm = w_out.shape[0]

    def body(h_ref, p_ref, a_ref, w_ref, o_ref):
        o_ref[...] = h_ref[...] + _dot(p_ref[...], w_ref[0:p, :]) + _dot(a_ref[...], w_ref[p:dm, :])

    row = pl.BlockSpec((tm, d), lambda i: (i, 0))
    return _call(
        body, grid=(lp // tm,),
        in_specs=[row, pl.BlockSpec((tm, p), lambda i: (i, 0)), pl.BlockSpec((tm, dm - p), lambda i: (i, 0)),
                  pl.BlockSpec((dm, d), lambda i: (0, 0))],
        out_specs=row, out_shape=SDS((lp, d), F32),
        vmem_mib=48, name=name, comm=comm)(h, pool_o, att_o, w_out)


def _loss_head(y, tpad, row0, row1, tm, name, comm=None):
    lp, d = y.shape

    def body(y_ref, t_ref, dy_ref, dob_ref, ls_ref):
        i = pl.program_id(0)

        @pl.when(i == 0)
        def _():
            ls_ref[...] = jnp.zeros_like(ls_ref)

        rows = i * tm + lax.broadcasted_iota(jnp.int32, (tm, d), 0)
        err = jnp.where((rows >= row0) & (rows < row1), y_ref[...] - t_ref[...], 0.0)
        dy = err * (1.0 / d)
        dy_ref[...] = dy
        dob_ref[...] = (0.5 * dy).astype(BF)
        sq = _rows8(err * err)
        acc = sq[:, 0:LANES]
        for c in range(1, d // LANES):
            acc = acc + sq[:, c * LANES:(c + 1) * LANES]
        ls_ref[...] += acc

    row = pl.BlockSpec((tm, d), lambda i: (i, 0))
    return _call(
        body, grid=(lp // tm,), in_specs=[row, row],
        out_specs=[row, row, pl.BlockSpec((SUBLANES, LANES), lambda i: (0, 0))],
        out_shape=[SDS((lp, d), F32), SDS((lp, d), BF), SDS((SUBLANES, LANES), F32)],
        vmem_mib=48, name=name, comm=comm)(y, tpad)


def _window_select(levels, gidx):
    out = levels[-1]
    for k in range(len(levels) - 2, -1, -1):
        out = jnp.where(gidx == k, levels[k], out)
    return out


def _pool_window_mean_minus_id(x, gidx):
    rows = lax.broadcasted_iota(jnp.int32, x.shape, 0)
    levels = []
    s = x
    shift = 1
    while shift < POOL_WINDOWS[-1]:
        s = s + jnp.where(rows >= shift, pltpu.roll(s, shift, 0), 0.0)
        shift *= 2
        if shift in POOL_WINDOWS:
            levels.append(s)
    win = _window_select(levels, gidx)
    cnt = jnp.minimum(rows + 1, _window_select(list(POOL_WINDOWS), gidx)).astype(F32)
    return win / cnt - x, cnt


def _pool_window_transpose(dy, cnt, gidx):
    lp = dy.shape[0]
    rows = lax.broadcasted_iota(jnp.int32, dy.shape, 0)
    levels = []
    s = dy / cnt
    shift = 1
    while shift < POOL_WINDOWS[-1]:
        s = s + jnp.where(rows < lp - shift, pltpu.roll(s, lp - shift, 0), 0.0)
        shift *= 2
        if shift in POOL_WINDOWS:
            levels.append(s)
    return _window_select(levels, gidx) - dy


def _pool_fwd(z, pool_w, pool_scale, name, comm=None):
    lp = z.shape[0]
    ng, gw, _ = pool_w.shape

    def body(p_ref, w_ref, s_ref, o_ref):
        pooled, _ = _pool_window_mean_minus_id(p_ref[...], pl.program_id(0))
        o_ref[...] = (_dot(pooled.astype(BF), w_ref[...]) * s_ref[...]).astype(BF)

    return _call(
        body, grid=(ng,),
        in_specs=[pl.BlockSpec((lp, gw), lambda g: (0, g)), pl.BlockSpec((None, gw, gw), lambda g: (g, 0, 0)),
                  pl.BlockSpec((1, gw), lambda g: (0, g))],
        out_specs=pl.BlockSpec((lp, gw), lambda g: (0, g)), out_shape=SDS((lp, ng * gw), BF),
        vmem_mib=48, name=name, comm=comm)(z, pool_w, pool_scale)


def _pool_bwd(z, dmix, pool_w, pool_scale, name, comm=None):
    lp = z.shape[0]
    ng, gw, _ = pool_w.shape

    def body(p_ref, d_ref, w_ref, s_ref, dz_ref, dw_ref, ds_ref):
        g = pl.program_id(0)
        pooled, cnt = _pool_window_mean_minus_id(p_ref[...], g)
        pooled_b = pooled.astype(BF)
        w = w_ref[...]
        mixed = _dot(pooled_b, w)
        dpo = d_ref[...].astype(F32)
        ds_ref[...] = _rows8(dpo * mixed)
        dmixed = (dpo * s_ref[...]).astype(BF)
        dw_ref[...] = _dot_tn(pooled_b, dmixed)
        dpooled = _dot_nt(dmixed, w)
        dz_ref[...] = _pool_window_transpose(dpooled, cnt, g).astype(BF)

    return _call(
        body, grid=(ng,),
        in_specs=[pl.BlockSpec((lp, gw), lambda g: (0, g)), pl.BlockSpec((lp, gw), lambda g: (0, g)),
                  pl.BlockSpec((None, gw, gw), lambda g: (g, 0, 0)), pl.BlockSpec((1, gw), lambda g: (0, g))],
        out_specs=[pl.BlockSpec((lp, gw), lambda g: (0, g)), pl.BlockSpec((None, gw, gw), lambda g: (g, 0, 0)),
                   pl.BlockSpec((SUBLANES, gw), lambda g: (0, g))],
        out_shape=[SDS((lp, ng * gw), BF), SDS((ng, gw, gw), F32), SDS((SUBLANES, ng * gw), F32)],
        vmem_mib=48, name=name, comm=comm)(z, dmix, pool_w, pool_scale)


def _log_sigmoid(x):
    return jnp.minimum(x, 0.0) - jnp.log(1.0 + jnp.exp(-jnp.abs(x)))


def _fox_prep(z, bfp, fblk, name, comm=None):
    lp = z.shape[0]
    nb = lp // LANES

    def body(f_ref, b_ref, cum_ref):
        r = lax.broadcasted_iota(jnp.int32, (LANES, LANES), 0)
        c = lax.broadcasted_iota(jnp.int32, (LANES, LANES), 1)
        tri = (r >= c).astype(F32)
        carry = jnp.zeros((1, LANES), F32)
        for blk in range(nb):
            sl = slice(blk * LANES, (blk + 1) * LANES)
            lf = _log_sigmoid(f_ref[sl, :] + b_ref[...])
            cb = jnp.dot(tri, lf, preferred_element_type=F32, precision=lax.Precision.HIGHEST) + carry
            cum_ref[sl, :] = cb
            carry = cb[LANES - 1:LANES, :]

    return _call(
        body, grid=(1,),
        in_specs=[pl.BlockSpec((lp, LANES), lambda i: (0, fblk)), pl.BlockSpec((1, LANES), lambda i: (0, 0))],
        out_specs=pl.BlockSpec((lp, LANES), lambda i: (0, 0)), out_shape=SDS((lp, LANES), F32),
        vmem_mib=32, name=name, comm=comm)(z, bfp)


def _fox_bwd(z, bfp, dcum, fblk, name, comm=None):
    lp = z.shape[0]
    nb = lp // LANES

    def body(f_ref, b_ref, dc_ref, dz_ref, db_ref):
        r = lax.broadcasted_iota(jnp.int32, (LANES, LANES), 0)
        c = lax.broadcasted_iota(jnp.int32, (LANES, LANES), 1)
        tri = (r <= c).astype(F32)
        carry = jnp.zeros((1, LANES), F32)
        acc = jnp.zeros((SUBLANES, LANES), F32)
        for blk in range(nb - 1, -1, -1):
            sl = slice(blk * LANES, (blk + 1) * LANES)
            dlf = jnp.dot(tri, dc_ref[sl, :], preferred_element_type=F32, precision=lax.Precision.HIGHEST) + carry
            carry = dlf[0:1, :]
            df = dlf * jax.nn.sigmoid(-(f_ref[sl, :] + b_ref[...]))
            dz_ref[sl, :] = df.astype(BF)
            acc = acc + _rows8(df)
        db_ref[...] = acc

    return _call(
        body, grid=(1,),
        in_specs=[pl.BlockSpec((lp, LANES), lambda i: (0, fblk)), pl.BlockSpec((1, LANES), lambda i: (0, 0)),
                  pl.BlockSpec((lp, LANES), lambda i: (0, 0))],
        out_specs=[pl.BlockSpec((lp, LANES), lambda i: (0, 0)), pl.BlockSpec((SUBLANES, LANES), lambda i: (0, 0))],
        out_shape=[SDS((lp, LANES), BF), SDS((SUBLANES, LANES), F32)],
        vmem_mib=32, name=name, comm=comm)(z, bfp, dcum)


def _att_scores(q_ref, cum_ref, cumt_ref, qw_ref, kn_s, h, i, tq, lk):
    scale = 1.0 / (HEAD_DIM ** 0.5)
    q = q_ref[...]
    rq = _rstd(q)
    qhat = q * rq
    qn = (qhat * qw_ref[...]).astype(BF)
    s = _dot_nt(qn, kn_s[0:lk, :]) * scale
    lane = lax.broadcasted_iota(jnp.int32, (tq, LANES), 1)
    cq = jnp.sum(jnp.where(lane == h, cum_ref[...], 0.0), axis=1, keepdims=True)
    ck = cumt_ref[pl.ds(h, 1), 0:lk]
    s = s + (cq - ck)
    qpos = i * tq + lax.broadcasted_iota(jnp.int32, (tq, lk), 0)
    kpos = lax.broadcasted_iota(jnp.int32, (tq, lk), 1)
    s = jnp.where(qpos >= kpos, s, NEG_BIG)
    e = jnp.exp(s - jnp.max(s, axis=1, keepdims=True))
    p = e / jnp.sum(e, axis=1, keepdims=True)
    return p, qn, qhat, rq


def _per_query_tile(i, nq, tq, lp, fn):
    for t in range(nq):
        lk = min(lp, -(-((t + 1) * tq) // LANES) * LANES)
        pl.when(i == t)(functools.partial(fn, lk))


def _att_fwd(z, cum, cumt, qw, kw, n_heads, qblk0, tq, name, comm=None):
    lp = z.shape[0]
    nh = n_heads

    def body(q_ref, k_ref, v_ref, cum_ref, cumt_ref, qw_ref, kw_ref, o_ref, kn_s, vb_s):
        h, i = pl.program_id(0), pl.program_id(1)

        @pl.when(i == 0)
        def _():
            k = k_ref[...]
            kn_s[...] = (k * _rstd(k) * kw_ref[...]).astype(BF)
            vb_s[...] = v_ref[...].astype(BF)

        def tile(lk):
            p, _, _, _ = _att_scores(q_ref, cum_ref, cumt_ref, qw_ref, kn_s, h, i, tq, lk)
            o_ref[...] = _dot(p.astype(BF), vb_s[0:lk, :]).astype(BF)

        _per_query_tile(i, lp // tq, tq, lp, tile)

    vec = pl.BlockSpec((1, HEAD_DIM), lambda h, i: (0, 0))
    return _call(
        body, grid=(nh, lp // tq),
        in_specs=[pl.BlockSpec((tq, HEAD_DIM), lambda h, i: (i, qblk0 + h)),
                  pl.BlockSpec((lp, HEAD_DIM), lambda h, i: (0, qblk0 + nh + h)),
                  pl.BlockSpec((lp, HEAD_DIM), lambda h, i: (0, qblk0 + 2 * nh + h)),
                  pl.BlockSpec((tq, LANES), lambda h, i: (i, 0)),
                  pl.BlockSpec((nh, lp), lambda h, i: (0, 0)), vec, vec],
        out_specs=pl.BlockSpec((tq, HEAD_DIM), lambda h, i: (i, h)),
        out_shape=SDS((lp, nh * HEAD_DIM), BF),
        scratch_shapes=[pltpu.VMEM((lp, HEAD_DIM), BF), pltpu.VMEM((lp, HEAD_DIM), BF)],
        vmem_mib=48, name=name, comm=comm)(z, z, z, cum, cumt, qw, kw)


def _att_bwd(z, cum, cumt, qw, kw, dmix, n_heads, qblk0, oblk0, tq, name, comm=None):
    lp = z.shape[0]
    nh = n_heads
    nq = lp // tq
    scale = 1.0 / (HEAD_DIM ** 0.5)

    def body(q_ref, k_ref, v_ref, cum_ref, cumt_ref, qw_ref, kw_ref, do_ref,
             dq_ref, dk_ref, dv_ref, dcq_ref, dck_ref, dqw_ref, dkw_ref,
             kn_s, vb_s, dkn_s, dv_s, dck_s):
        h, i = pl.program_id(0), pl.program_id(1)

        @pl.when((h == 0) & (i == 0))
        def _():
            dqw_ref[...] = jnp.zeros_like(dqw_ref)
            dkw_ref[...] = jnp.zeros_like(dkw_ref)

        @pl.when(i == 0)
        def _():
            k = k_ref[...]
            kn_s[...] = (k * _rstd(k) * kw_ref[...]).astype(BF)
            vb_s[...] = v_ref[...].astype(BF)
            dkn_s[...] = jnp.zeros_like(dkn_s)
            dv_s[...] = jnp.zeros_like(dv_s)
            dck_s[...] = jnp.zeros_like(dck_s)

        def tile(lk):
            p, qn, qhat, rq = _att_scores(q_ref, cum_ref, cumt_ref, qw_ref, kn_s, h, i, tq, lk)
            dob = do_ref[...]
            dp = _dot_nt(dob, vb_s[0:lk, :])
            ds = p * (dp - jnp.sum(p * dp, axis=1, keepdims=True))
            dsb = ds.astype(BF)
            dv_s[0:lk, :] += _dot_tn(p.astype(BF), dob)
            dkn_s[0:lk, :] += _dot_tn(dsb, qn)
            dcq_ref[...] = jnp.sum(ds, axis=1, keepdims=True)
            dck_s[:, 0:lk] += jnp.sum(ds, axis=0, keepdims=True)
            dqn = _dot(dsb, kn_s[0:lk, :]) * scale
            gq = dqn * qw_ref[...]
            dq_ref[...] = (rq * (gq - qhat * jnp.mean(gq * qhat, axis=-1, keepdims=True))).astype(BF)
            dqw_ref[...] += _rows8(dqn * qhat)

        _per_query_tile(i, nq, tq, lp, tile)

        @pl.when(i == nq - 1)
        def _():
            k = k_ref[...]
            rk = _rstd(k)
            khat = k * rk
            dkn = dkn_s[...] * scale
            gk = dkn * kw_ref[...]
            dk_ref[...] = (rk * (gk - khat * jnp.mean(gk * khat, axis=-1, keepdims=True))).astype(BF)
            dkw_ref[...] += _rows8(dkn * khat)
            dv_ref[...] = dv_s[...].astype(BF)
            dck_ref[...] = dck_s[...]

    vec = pl.BlockSpec((1, HEAD_DIM), lambda h, i: (0, 0))
    part = pl.BlockSpec((SUBLANES, LANES), lambda h, i: (0, 0))
    return _call(
        body, grid=(nh, nq),
        in_specs=[pl.BlockSpec((tq, HEAD_DIM), lambda h, i: (i, qblk0 + h)),
                  pl.BlockSpec((lp, HEAD_DIM), lambda h, i: (0, qblk0 + nh + h)),
                  pl.BlockSpec((lp, HEAD_DIM), lambda h, i: (0, qblk0 + 2 * nh + h)),
                  pl.BlockSpec((tq, LANES), lambda h, i: (i, 0)),
                  pl.BlockSpec((nh, lp), lambda h, i: (0, 0)), vec, vec,
                  pl.BlockSpec((tq, HEAD_DIM), lambda h, i: (i, oblk0 + h))],
        out_specs=[pl.BlockSpec((tq, HEAD_DIM), lambda h, i: (i, h)),
                   pl.BlockSpec((lp, HEAD_DIM), lambda h, i: (0, h)),
                   pl.BlockSpec((lp, HEAD_DIM), lambda h, i: (0, h)),
                   pl.BlockSpec((None, tq, 1), lambda h, i: (h, i, 0)),
                   pl.BlockSpec((None, 1, lp), lambda h, i: (h, 0, 0)),
                   part, part],
        out_shape=[SDS((lp, nh * HEAD_DIM), BF)] * 3
        + [SDS((nh, lp, 1), F32), SDS((nh, 1, lp), F32), SDS((SUBLANES, LANES), F32), SDS((SUBLANES, LANES), F32)],
        scratch_shapes=[pltpu.VMEM((lp, HEAD_DIM), BF), pltpu.VMEM((lp, HEAD_DIM), BF),
                        pltpu.VMEM((lp, HEAD_DIM), F32), pltpu.VMEM((lp, HEAD_DIM), F32),
                        pltpu.VMEM((1, lp), F32)],
        vmem_mib=56, name=name, comm=comm)(z, z, z, cum, cumt, qw, kw, dmix)


def _adamw_math(w, g, m, v):
    m2 = ADAM_B1 * m + (1.0 - ADAM_B1) * g
    v2 = ADAM_B2 * v + (1.0 - ADAM_B2) * (g * g)
    m_hat = m2 / (1.0 - ADAM_B1 ** ADAM_STEP)
    v_hat = v2 / (1.0 - ADAM_B2 ** ADAM_STEP)
    delta = -ADAM_LR * (m_hat / (jnp.sqrt(v_hat) + ADAM_EPS) + ADAM_WD * w)
    return delta, m2, v2


def _adamw(g_in, w, m, v, name, comm=None):
    r, c = w.shape
    partial_sum = g_in.ndim == 3
    lane_padded = -(-c // LANES) * LANES
    tr = _largest_tile(r, max(16, MIB // (4 * lane_padded) // 16 * 16), 16)

    def body(g_ref, w_ref, m_ref, v_ref, go_ref, d_ref, mo_ref, vo_ref):
        if partial_sum:
            g = g_ref[0].astype(F32)
            for k in range(1, g_in.shape[0]):
                g = g + g_ref[k].astype(F32)
        else:
            g = g_ref[...]
        delta, m2, v2 = _adamw_math(w_ref[...], g, m_ref[...], v_ref[...])
        go_ref[...] = g
        d_ref[...] = delta
        mo_ref[...] = m2
        vo_ref[...] = v2

    blk = pl.BlockSpec((tr, c), lambda i: (i, 0))
    g_spec = pl.BlockSpec((g_in.shape[0], tr, c), lambda i: (0, i, 0)) if partial_sum else blk
    return _call(
        body, grid=(r // tr,), in_specs=[g_spec, blk, blk, blk], out_specs=[blk] * 4,
        out_shape=[SDS((r, c), F32)] * 4, vmem_mib=40, name=name, comm=comm)(g_in, w, m, v)


def _peer(x, y, c, k):
    return (1 - x if k & 4 else x, 1 - y if k & 2 else y, 1 - c if k & 1 else c)


def _exchange(arrs, scatter, name, comm=None):
    n = len(arrs)

    def body(*refs):
        ins, outs = refs[:n], refs[n:2 * n]
        send_sems, recv_sems, local_sems = refs[2 * n:]
        x, y, c = lax.axis_index("x"), lax.axis_index("y"), lax.axis_index("c")
        me = 4 * x + 2 * y + c

        def src(t, dev):
            return ins[t].at[dev] if scatter else ins[t]

        def copy(t, k, arrival):
            px, py, pc = _peer(x, y, c, k)
            dev = 4 * px + 2 * py + pc
            return pltpu.make_async_remote_copy(
                src_ref=src(t, dev), dst_ref=outs[t].at[dev if arrival else me],
                send_sem=send_sems.at[t, k - 1], recv_sem=recv_sems.at[t, k - 1],
                device_id=(px, py, pc), device_id_type=pl.DeviceIdType.MESH)

        local = [pltpu.make_async_copy(src(t, me), outs[t].at[me], local_sems.at[t]) for t in range(n)]
        for cp in local:
            cp.start()
        pairs = [(t, k) for k in range(1, N_DEV) for t in range(n)]
        for t, k in pairs:
            copy(t, k, False).start()
        for cp in local:
            cp.wait()
        for t, k in pairs:
            copy(t, k, True).wait_recv()
        for t, k in pairs:
            copy(t, k, False).wait_send()

    out_shape = [SDS(a.shape if scatter else (N_DEV,) + a.shape, a.dtype) for a in arrs]
    anyspec = pl.BlockSpec(memory_space=pl.ANY)
    return pl.pallas_call(
        body, in_specs=[anyspec] * n, out_specs=[anyspec] * n, out_shape=out_shape,
        scratch_shapes=[pltpu.SemaphoreType.DMA((n, N_DEV - 1)), pltpu.SemaphoreType.DMA((n, N_DEV - 1)),
                        pltpu.SemaphoreType.DMA((n,))],
        name=name)(*arrs)


_SIBLING = 1
_ICI_RELS = (2, 4, 6)


def _mesh_pos():
    return lax.axis_index("x"), lax.axis_index("y"), lax.axis_index("c")


def _sem_pair(sems, t, j, n_rel, scalars):
    if scalars:
        i = 2 * (t * n_rel + j)
        return sems[i], sems[i + 1]
    return sems[0].at[t, j], sems[1].at[t, j]


def _dev(pos):
    return 4 * pos[0] + 2 * pos[1] + pos[2]


def _gather_ici(shards, landing=None):
    n = len(shards)
    rels = (_SIBLING,) + _ICI_RELS

    def remote(ins, outs, sems, arrival):
        x, y, c = _mesh_pos()
        dst = ins[n:] if landing is not None else outs
        cps = []
        for j, k in enumerate(rels):
            peer = _peer(x, y, c, k)
            slot = _dev(peer) if arrival else _dev((x, y, c))
            for t in range(n):
                send_sem, recv_sem = _sem_pair(sems, t, j, len(rels), landing is not None)
                cps.append(pltpu.make_async_remote_copy(
                    src_ref=ins[t], dst_ref=dst[t].at[slot], send_sem=send_sem, recv_sem=recv_sem,
                    device_id=peer, device_id_type=pl.DeviceIdType.MESH))
        return cps

    if landing is not None:
        def start_remote(ins, outs, sems):
            for cp in remote(ins, outs, sems, False):
                cp.start()

        def finish_remote(ins, outs, sems):
            for cp in remote(ins, outs, sems, True):
                cp.wait_recv()
            for cp in remote(ins, outs, sems, False):
                cp.wait_send()

        return _Comm(list(shards) + list(landing), [SDS(a.shape, a.dtype) for a in landing],
                     [pltpu.SemaphoreType.DMA(())] * (2 * n * len(rels)),
                     start_remote, finish_remote, aliases={n + t: t for t in range(n)})

    def local(ins, outs, sems):
        me = _dev(_mesh_pos())
        return [pltpu.make_async_copy(ins[t], outs[t].at[me], sems[2].at[t]) for t in range(n)]

    def start(ins, outs, sems):
        for cp in local(ins, outs, sems) + remote(ins, outs, sems, False):
            cp.start()

    def finish(ins, outs, sems):
        for cp in local(ins, outs, sems):
            cp.wait()
        for cp in remote(ins, outs, sems, True):
            cp.wait_recv()
        for cp in remote(ins, outs, sems, False):
            cp.wait_send()

    return _Comm(shards, [SDS((N_DEV,) + s.shape, s.dtype) for s in shards],
                 [pltpu.SemaphoreType.DMA((n, len(rels))), pltpu.SemaphoreType.DMA((n, len(rels))),
                  pltpu.SemaphoreType.DMA((n,))], start, finish)


def _gather_fwd(partial):
    n = len(partial)

    def copies(ins, outs, sems, arrival):
        x, y, c = _mesh_pos()
        sibling = _peer(x, y, c, _SIBLING)
        cps = []
        for j, k in enumerate(_ICI_RELS):
            slot = _dev(_peer(x, y, c, k | _SIBLING if arrival else k))
            for t in range(n):
                cps.append(pltpu.make_async_remote_copy(
                    src_ref=ins[t].at[slot], dst_ref=outs[t].at[slot], send_sem=sems[0].at[t, j],
                    recv_sem=sems[1].at[t, j], device_id=sibling, device_id_type=pl.DeviceIdType.MESH))
        return cps

    def start(ins, outs, sems):
        for cp in copies(ins, outs, sems, False):
            cp.start()

    def finish(ins, outs, sems):
        for cp in copies(ins, outs, sems, True):
            cp.wait_recv()
        for cp in copies(ins, outs, sems, False):
            cp.wait_send()

    return _Comm(partial, [SDS(a.shape, a.dtype) for a in partial],
                 [pltpu.SemaphoreType.DMA((n, len(_ICI_RELS)))] * 2, start, finish,
                 aliases={t: t for t in range(n)})


def _scatter_sibling(slots):
    n = len(slots)

    def copies(ins, outs, sems):
        x, y, c = _mesh_pos()
        return [pltpu.make_async_remote_copy(
            src_ref=ins[t].at[:, 1 - c], dst_ref=outs[t], send_sem=sems[0].at[t], recv_sem=sems[1].at[t],
            device_id=_peer(x, y, c, _SIBLING), device_id_type=pl.DeviceIdType.MESH) for t in range(n)]

    def start(ins, outs, sems):
        for cp in copies(ins, outs, sems):
            cp.start()

    def finish(ins, outs, sems):
        for cp in copies(ins, outs, sems):
            cp.wait()

    return _Comm(slots, [SDS((s.shape[0],) + s.shape[2:], s.dtype) for s in slots],
                 [pltpu.SemaphoreType.DMA((n,))] * 2, start, finish)


def _scatter_ici(chip_sums, landing=None):
    n = len(chip_sums)

    def remote(ins, outs, sems, arrival):
        x, y, c = _mesh_pos()
        dst = ins[n:] if landing is not None else outs
        cps = []
        for j, k in enumerate(_ICI_RELS):
            peer = _peer(x, y, c, k)
            theirs, mine = 2 * peer[0] + peer[1], 2 * x + y
            for t in range(n):
                send_sem, recv_sem = _sem_pair(sems, t, j, len(_ICI_RELS), landing is not None)
                cps.append(pltpu.make_async_remote_copy(
                    src_ref=ins[t].at[theirs], dst_ref=dst[t].at[theirs if arrival else mine],
                    send_sem=send_sem, recv_sem=recv_sem,
                    device_id=peer, device_id_type=pl.DeviceIdType.MESH))
        return cps

    if landing is not None:
        def start_remote(ins, outs, sems):
            for cp in remote(ins, outs, sems, False):
                cp.start()

        def finish_remote(ins, outs, sems):
            for cp in remote(ins, outs, sems, True):
                cp.wait_recv()
            for cp in remote(ins, outs, sems, False):
                cp.wait_send()

        return _Comm(list(chip_sums) + list(landing), [SDS(a.shape, a.dtype) for a in landing],
                     [pltpu.SemaphoreType.DMA(())] * (2 * n * len(_ICI_RELS)), start_remote, finish_remote,
                     aliases={n + t: t for t in range(n)})

    def local(ins, outs, sems):
        x, y, _ = _mesh_pos()
        return [pltpu.make_async_copy(ins[t].at[2 * x + y], outs[t].at[2 * x + y], sems[2].at[t]) for t in range(n)]

    def start(ins, outs, sems):
        for cp in local(ins, outs, sems) + remote(ins, outs, sems, False):
            cp.start()

    def finish(ins, outs, sems):
        for cp in local(ins, outs, sems):
            cp.wait()
        for cp in remote(ins, outs, sems, True):
            cp.wait_recv()
        for cp in remote(ins, outs, sems, False):
            cp.wait_send()

    return _Comm(chip_sums, [SDS(a.shape, a.dtype) for a in chip_sums],
                 [pltpu.SemaphoreType.DMA((n, len(_ICI_RELS))), pltpu.SemaphoreType.DMA((n, len(_ICI_RELS))),
                  pltpu.SemaphoreType.DMA((n,))], start, finish)


def _chip_sum(slots, from_sibling, core, name):
    nq, _, r, c = slots.shape
    tr = _largest_tile(r, 1024, 16)

    def body(core_ref, a_ref, b_ref, o_ref):
        o_ref[...] = (a_ref[...].astype(F32) + b_ref[...].astype(F32)).astype(BF)

    return pl.pallas_call(
        body,
        grid_spec=pltpu.PrefetchScalarGridSpec(
            num_scalar_prefetch=1, grid=(nq, r // tr),
            in_specs=[pl.BlockSpec((None, None, tr, c), lambda q, i, core_ref: (q, core_ref[0], i, 0)),
                      pl.BlockSpec((None, tr, c), lambda q, i, core_ref: (q, i, 0))],
            out_specs=pl.BlockSpec((None, tr, c), lambda q, i, core_ref: (q, i, 0))),
        out_shape=SDS((nq, r, c), BF), compiler_params=pltpu.CompilerParams(vmem_limit_bytes=40 * MIB),
        name=name)(core, slots, from_sibling)


def _small_reduce(pack_g, meta_g, loss_scale, name, comm=None):
    w = pack_g.shape[2]

    def body(p_ref, m_ref, tot_ref, meta_ref, loss_ref):
        acc = p_ref[0]
        macc = m_ref[0]
        for k in range(1, N_DEV):
            acc = acc + p_ref[k]
            macc = macc + m_ref[k]
        tot = jnp.sum(acc, axis=0, keepdims=True)
        tot_ref[...] = tot
        meta_ref[...] = macc
        loss_ref[...] = jnp.full((1, LANES), loss_scale * jnp.sum(tot[:, w - LANES:w]), F32)

    return pl.pallas_call(
        body, out_shape=[SDS((1, w), F32), SDS(meta_g.shape[1:], F32), SDS((1, LANES), F32)],
        compiler_params=pltpu.CompilerParams(vmem_limit_bytes=32 * MIB), name=name)(pack_g, meta_g)


def _local_step(x, target, sw, plan):
    s_len, d = x.shape
    n_heads = plan.n_heads
    plan.at("start")
    meta = plan.weights("meta")
    n_meta = meta.shape[0]
    l = n_meta + s_len
    lp = -(-l // LANES) * LANES
    tm = _largest_tile(lp, 544, 16)
    tq = _largest_tile(lp, 272, 16)
    te = _largest_tile(lp, 272, 16)
    tmd = _largest_tile(d, 512, LANES)

    zpad = jnp.zeros((lp - l, d), F32)
    h0 = jnp.concatenate([meta, x, zpad], axis=0)
    tpad = jnp.concatenate([jnp.zeros((n_meta, d), F32), target, zpad], axis=0)

    wg1, wu1, wd1 = plan.weights("ffn1")
    fs = wg1.shape[1]
    h1, a1, b1, u1 = _ffn_fwd(h0, sw["ffn1_norm"], wg1, wu1, wd1, tm, "ffn1_fwd", plan.comm("ffn1_fwd"))
    plan.at("after_ffn1_fwd", h1)
    win, pw, wout = plan.weights("mix")
    nz = win.shape[1]
    p_w = sw["pool_scale"].shape[1]
    npb = p_w // LANES
    fblk = nz // LANES - 1
    tnz = _largest_tile(nz, 1408, LANES)
    qw, kw, bfp, ps = sw["q_norm"], sw["k_norm"], sw["b_forget"], sw["pool_scale"]
    z, u2 = _norm_matmul(h1, sw["mix_norm"], win, tm, tnz, "mix_in", plan.comm("mix_in"))
    cum = _fox_prep(z, bfp, fblk, "fox_prep")
    cumt = cum[:, :n_heads].T
    pool_o = _pool_fwd(z, pw, ps, "pool_fwd")
    att_o = _att_fwd(z, cum, cumt, qw, kw, n_heads, npb, tq, "att_fwd", plan.comm("att_fwd"))
    plan.at("after_att_fwd", att_o)
    h2 =_out_proj(h1, pool_o, att_o, wout, tm, "out_proj", plan.comm("out_proj"))
    wg2, wu2, wd2 = plan.weights("ffn2")
    h3, a2, b2, u3 = _ffn_fwd(h2, sw["ffn2_norm"], wg2, wu2, wd2, tm, "ffn2_fwd", plan.comm("ffn2_fwd"))
    dy, dob3, lsq = _loss_head(h3, tpad, n_meta, l, te, "loss_head")

    du3, da2, db2, hid2 = _ffn_bwd_dx(dob3, a2, b2, wg2, wu2, wd2, tm, "ffn2_bwd_dx", plan.comm("ffn2_bwd_dx"))
    dh2, dh2b, dn2 = _rms_bwd(du3, h2, sw["ffn2_norm"], dy, 1.0, te, "ffn2_rms_bwd")
    plan.grad("ffn2_w_gate", _matmul_tn(da2, u3, fs, d, "ffn2_dwg", plan.comm("ffn2_dwg")))
    plan.grad("ffn2_w_up", _matmul_tn(db2, u3, fs, d, "ffn2_dwu", plan.comm("ffn2_dwu")))
    plan.grad("ffn2_w_down", _matmul_tn(hid2, dob3, fs, d, "ffn2_dwd", plan.comm("ffn2_dwd")))
    plan.at("after_ffn2_dwd")

    dmix = _matmul_nt(dh2b, wout, tm, d, BF, "out_proj_bwd", plan.comm("out_proj_bwd"))
    plan.at("after_out_proj_bwd")
    tmp = _largest_tile(p_w, 512, LANES)
    plan.grad("w_out", jnp.concatenate([_matmul_tn(pool_o, dh2b, tmp, d, "dwout_pool"),
                                        _matmul_tn(att_o, dh2b, tmp, d, "dwout_att")], axis=0))
    dzp, dpw, dps = _pool_bwd(z, dmix, pw, ps, "pool_bwd")
    plan.grad("pool_w", dpw)
    plan.at("before_att_bwd")
    dq, dk, dv, dcq, dck, dqw, dkw = _att_bwd(z, cum, cumt, qw, kw, dmix, n_heads, npb, npb, tq, "att_bwd",
                                              plan.comm("att_bwd"))
    dcum = dcq[:, :, 0].T - dck[:, 0, :].T
    dcum = jnp.pad(dcum, ((0, 0), (0, LANES - n_heads)))
    dzf, dbf = _fox_bwd(z, bfp, dcum, fblk, "fox_bwd")
    dz = jnp.concatenate([dzp, dq, dk, dv, dzf], axis=1)
    plan.grad("w_in", _matmul_tn(u2, dz, tmd, tnz, "dwin", plan.comm("dwin")))
    du2 = _matmul_nt(dz, win, tm, tnz, F32, "mix_in_bwd", plan.comm("mix_in_bwd"))
    plan.at("before_ffn1_bwd_dx")
    dh1, dob1, dnm = _rms_bwd(du2, h1, sw["mix_norm"], dh2, 0.5, te, "mix_rms_bwd")

    du1, da1, db1, hid1 = _ffn_bwd_dx(dob1, a1, b1, wg1, wu1, wd1, tm, "ffn1_bwd_dx", plan.comm("ffn1_bwd_dx"))
    dh0, _, dn1 = _rms_bwd(du1, h0, sw["ffn1_norm"], dh1, 1.0, te, "ffn1_rms_bwd")
    plan.grad("ffn1_w_gate", _matmul_tn(da1, u1, fs, d, "ffn1_dwg", plan.comm("ffn1_dwg")))
    plan.grad("ffn1_w_up", _matmul_tn(db1, u1, fs, d, "ffn1_dwu", plan.comm("ffn1_dwu")))
    plan.at("before_ffn1_dwd")
    plan.grad("ffn1_w_down", _matmul_tn(hid1, dob1, fs, d, "ffn1_dwd", plan.comm("ffn1_dwd")))
    plan.at("after_ffn1_dwd")

    small = [dn1, dnm, dn2, dps, dqw, dkw, dbf, lsq]
    return dh0[n_meta:l], dh0[:n_meta], small


_BIG = ("ffn1_w_gate", "ffn1_w_up", "ffn1_w_down", "w_in", "pool_w", "w_out", "ffn2_w_gate", "ffn2_w_up", "ffn2_w_down")
_SMALL = ("ffn1_norm", "mix_norm", "ffn2_norm", "pool_scale", "q_norm", "k_norm", "b_forget")
_ORDER = ("meta_tokens", "ffn1_norm", "ffn1_w_gate", "ffn1_w_up", "ffn1_w_down", "mix_norm", "w_in", "b_forget",
          "q_norm", "k_norm", "pool_w", "pool_scale", "w_out", "ffn2_norm", "ffn2_w_gate", "ffn2_w_up", "ffn2_w_down")


_FFN1 = ("ffn1_w_gate", "ffn1_w_up", "ffn1_w_down")
_FFN2 = ("ffn2_w_gate", "ffn2_w_up", "ffn2_w_down")
_MIX = ("w_in", "pool_w", "w_out")

_RIDES = {
    "out_proj": (("g2", _FFN2),),
    "ffn2_dwu": (("s1", ("ffn2_w_gate",)),),
    "ffn2_dwd": (("s1", ("ffn2_w_up",)),),
    "out_proj_bwd": (("s1", ("ffn2_w_down",)),),
    "mix_in_bwd": (("s1", _MIX),),
    "ffn1_dwu": (("s1", ("ffn1_w_gate",)),),
    "ffn1_dwd": (("s1", ("ffn1_w_up",)),),
    "adamw_ffn2_w_gate": (("s1", ("ffn1_w_down",)),),
}
_G1_FFN1 = _FFN1 + ("meta_tokens",)
_POINTS = {
    "start": (("start", "g1", _G1_FFN1), ("wait", "g1", _G1_FFN1), ("start", "g1", _MIX), ("start", "g1", _FFN2),
              ("alone", "g2", _G1_FFN1)),
    "after_ffn1_fwd": (("wait", "g1", _MIX), ("alone", "g2", _MIX)),
    "after_att_fwd": (("wait", "g1", _FFN2),),
    "after_ffn2_dwd": (("sum", ("ffn2_w_gate",)), ("start", "s2", ("ffn2_w_gate",))),
    "after_out_proj_bwd": (("sum", ("ffn2_w_up",)), ("start", "s2", ("ffn2_w_up",))),
    "before_att_bwd": (("sum", ("ffn2_w_down",)), ("start", "s2", ("ffn2_w_down",))),
    "before_ffn1_bwd_dx": (("sum", _MIX), ("start", "s2", _MIX)),
    "before_ffn1_dwd": (("sum", ("ffn1_w_gate",)), ("start", "s2", ("ffn1_w_gate",))),
    "after_ffn1_dwd": (("sum", ("ffn1_w_up",)), ("start", "s2", ("ffn1_w_up",))),
    "before_adamw_ffn2_w_gate": (("wait", "s2", ("ffn2_w_gate",)),),
    "before_adamw_ffn2_w_up": (("sum", ("ffn1_w_down",)), ("start", "s2", ("ffn1_w_down",)),
                               ("wait", "s2", ("ffn2_w_up",))),
    "before_adamw_ffn2_w_down": (("wait", "s2", ("ffn2_w_down",)),),
    "before_adamw_w_in": (("wait", "s2", _MIX),),
    "before_adamw_ffn1_w_gate": (("wait", "s2", ("ffn1_w_gate",)),),
    "before_adamw_ffn1_w_up": (("wait", "s2", ("ffn1_w_up",)),),
    "before_adamw_ffn1_w_down": (("wait", "s2", ("ffn1_w_down",)),),
}


def _own_slot_filled(block, slot, n_slots):
    zone = lax.empty((n_slots,) + block.shape, block.dtype)
    return lax.dynamic_update_slice(zone, block[None], (slot,) + (0,) * block.ndim)


class _MeshPlan:
    def __init__(self, shards, pos, d, d_in, n_heads):
        self.shard, self.pos = dict(shards), pos
        self.core = pos[2].astype(jnp.int32).reshape(1)
        self.d, self.d_in, self.n_heads = d, d_in, n_heads
        self.partial, self.full, self.slots, self.from_sibling, self.chip_sum, self.received = {}, {}, {}, {}, {}, {}
        self.pending, self.started, self.tokens = [], {}, []

    def _phase(self, kind, names):
        src, dst, make = {"g1": (self.shard, self.partial, _gather_ici), "g2": (self.partial, self.full, _gather_fwd),
                          "s1": (self.slots, self.from_sibling, _scatter_sibling),
                          "s2": (self.chip_sum, self.received, _scatter_ici)}[kind]
        op = make([src[n] for n in names])
        self.pending.append((op, dst, names))
        return op

    def _settle(self):
        for op, dst, names in self.pending:
            dst.update(zip(names, op.results))
        self.pending = []

    def _start(self, kind, names):
        x, y, c = self.pos
        if kind == "g1":
            blocks = [self.shard[n] for n in names]
            op = _gather_ici(blocks, [_own_slot_filled(b, 4 * x + 2 * y + c, N_DEV) for b in blocks])
        else:
            sums = [self.chip_sum[n] for n in names]
            mine = [lax.dynamic_index_in_dim(s, 2 * x + y, 0, keepdims=False) for s in sums]
            op = _scatter_ici(sums, [_own_slot_filled(b, 2 * x + y, N_DEV // 2) for b in mine])
        self.started[(kind, names)], token = _split_start(op, "_".join(("start", kind, names[0])))
        self.tokens.append(token)

    def _wait(self, kind, names, after):
        landed = _split_wait(self.started.pop((kind, names)), after, "_".join(("wait", kind, names[0])))
        (self.partial if kind == "g1" else self.received).update(zip(names, landed))

    def comm(self, kernel_name):
        self._settle()
        ops = [self._phase(kind, names) for kind, names in _RIDES.get(kernel_name, ())]
        if self.tokens:
            ops.append(_Comm(self.tokens, [], [], lambda *a: None, lambda *a: None))
            self.tokens = []
        return _merge_comm(ops)

    def at(self, point, after=None):
        for step in _POINTS.get(point, ()):
            self._settle()
            if step[0] == "alone":
                _comm_alone(self._phase(step[1], step[2]), "_".join((step[1], point)))
            elif step[0] == "start":
                self._start(step[1], step[2])
            elif step[0] == "wait":
                self._wait(step[1], step[2], self.tokens[-1] if after is None else after)
            else:
                for n in step[1]:
                    self.chip_sum[n] = _chip_sum(self.slots[n], self.from_sibling[n], self.core, "chip_sum_" + n)

    def weights(self, group):
        self._settle()
        f, d = self.full, self.d
        if group == "meta":
            g = f["meta_tokens"]
            return g.transpose(1, 0, 2).reshape(g.shape[1], d)
        if group == "ffn1":
            return tuple(f[n] for n in _FFN1)
        if group == "ffn2":
            return tuple(f[n] for n in _FFN2)
        n_main = self.d_in - self.n_heads
        win = f["w_in"].transpose(1, 0, 2).reshape(d, self.d_in)
        win = jnp.concatenate([win[:, :n_main], jnp.pad(win[:, n_main:], ((0, 0), (0, LANES - self.n_heads)))], axis=1)
        pw = f["pool_w"]
        pw = pw.transpose(1, 0, 2, 3).reshape(pw.shape[1], pw.shape[3], pw.shape[3])
        return win, pw, f["w_out"].reshape(-1, d)

    def grad(self, name, g):
        d = self.d
        if name == "w_in":
            g = g[:, :self.d_in].reshape(d, N_DEV, -1).transpose(1, 0, 2)
        elif name == "pool_w":
            ng, gw = g.shape[0], g.shape[2]
            g = g.astype(BF).reshape(ng, N_DEV, -1, gw).transpose(1, 0, 2, 3).reshape(N_DEV, -1, gw)
        elif name == "w_out":
            g = g.reshape(N_DEV, -1, d)
        self.slots[name] = g.reshape((N_DEV // 2, 2) + g.shape[1:])

    def gradient_parts(self, name):
        self._settle()
        return self.received[name]


_TRANSPOSED = ("ffn1_w_gate", "ffn1_w_up", "ffn2_w_gate", "ffn2_w_up")


def _as2d(name, a):
    return a[0].T if name in _TRANSPOSED else a.reshape(-1, a.shape[-1])


def _from2d(name, a2d, shape):
    return a2d.T.reshape(shape) if name in _TRANSPOSED else a2d.reshape(shape)


def kernel(x, meta_tokens, ffn1_norm, ffn1_w_gate, ffn1_w_up, ffn1_w_down, mix_norm, w_in, b_forget, q_norm, k_norm, pool_w, pool_scale, w_out, ffn2_norm, ffn2_w_gate, ffn2_w_up, ffn2_w_down, loss_target, m_meta_tokens, m_ffn1_norm, m_ffn1_w_gate, m_ffn1_w_up, m_ffn1_w_down, m_mix_norm, m_w_in, m_b_forget, m_q_norm, m_k_norm, m_pool_w, m_pool_scale, m_w_out, m_ffn2_norm, m_ffn2_w_gate, m_ffn2_w_up, m_ffn2_w_down, v_meta_tokens, v_ffn1_norm, v_ffn1_w_gate, v_ffn1_w_up, v_ffn1_w_down, v_mix_norm, v_w_in, v_b_forget, v_q_norm, v_k_norm, v_pool_w, v_pool_scale, v_w_out, v_ffn2_norm, v_ffn2_w_gate, v_ffn2_w_up, v_ffn2_w_down):
    w = dict(meta_tokens=meta_tokens, ffn1_norm=ffn1_norm, ffn1_w_gate=ffn1_w_gate, ffn1_w_up=ffn1_w_up,
             ffn1_w_down=ffn1_w_down, mix_norm=mix_norm, w_in=w_in, b_forget=b_forget, q_norm=q_norm, k_norm=k_norm,
             pool_w=pool_w, pool_scale=pool_scale, w_out=w_out, ffn2_norm=ffn2_norm, ffn2_w_gate=ffn2_w_gate,
             ffn2_w_up=ffn2_w_up, ffn2_w_down=ffn2_w_down)
    m = dict(meta_tokens=m_meta_tokens, ffn1_norm=m_ffn1_norm, ffn1_w_gate=m_ffn1_w_gate, ffn1_w_up=m_ffn1_w_up,
             ffn1_w_down=m_ffn1_w_down, mix_norm=m_mix_norm, w_in=m_w_in, b_forget=m_b_forget, q_norm=m_q_norm,
             k_norm=m_k_norm, pool_w=m_pool_w, pool_scale=m_pool_scale, w_out=m_w_out, ffn2_norm=m_ffn2_norm,
             ffn2_w_gate=m_ffn2_w_gate, ffn2_w_up=m_ffn2_w_up, ffn2_w_down=m_ffn2_w_down)
    v = dict(meta_tokens=v_meta_tokens, ffn1_norm=v_ffn1_norm, ffn1_w_gate=v_ffn1_w_gate, ffn1_w_up=v_ffn1_w_up,
             ffn1_w_down=v_ffn1_w_down, mix_norm=v_mix_norm, w_in=v_w_in, b_forget=v_b_forget, q_norm=v_q_norm,
             k_norm=v_k_norm, pool_w=v_pool_w, pool_scale=v_pool_scale, w_out=v_w_out, ffn2_norm=v_ffn2_norm,
             ffn2_w_gate=v_ffn2_w_gate, ffn2_w_up=v_ffn2_w_up, ffn2_w_down=v_ffn2_w_down)

    d = x.shape[-1]
    n_heads = b_forget.shape[-1]
    pos = (lax.axis_index("x"), lax.axis_index("y"), lax.axis_index("c"))
    me = 4 * pos[0] + 2 * pos[1] + pos[2]

    shards = {k: (_as2d(k, w[k]) if k in _TRANSPOSED else w[k][0]).astype(BF) for k in _BIG}
    shards["meta_tokens"] = meta_tokens
    plan = _MeshPlan(shards, pos, d, N_DEV * w_in.shape[-1], n_heads)
    sw = {k: w[k] for k in _SMALL}
    sw["b_forget"] = jnp.pad(b_forget, ((0, 0), (0, LANES - n_heads)))
    dx, dmeta, small = _local_step(x[0], loss_target[0], sw, plan)

    res = {}
    last = dx

    def update_shards(names):
        nonlocal last
        for k in names:
            plan.at("before_adamw_" + k, last)
            res[k] = _adamw(plan.gradient_parts(k), _as2d(k, w[k]), _as2d(k, m[k]), _as2d(k, v[k]), "adamw_" + k,
                            plan.comm("adamw_" + k))
            last = res[k][0]

    update_shards(_FFN2 + _MIX)

    pack = jnp.concatenate(small, axis=1)
    pack_g, meta_g = _exchange([pack, dmeta], False, "gather_small")
    tot, dmeta_tot, loss_row = _small_reduce(pack_g, meta_g, 0.5 / d, "small_reduce")

    mcols = meta_tokens.shape[1]
    g_meta = lax.dynamic_slice_in_dim(dmeta_tot, me * mcols, mcols, axis=1)
    res["meta_tokens"] = _adamw(g_meta, meta_tokens, m_meta_tokens, v_meta_tokens, "adamw_meta_tokens")

    def packed(src):
        return jnp.concatenate([src[k] for k in _SMALL[:-1]] + [jnp.pad(src["b_forget"], ((0, 0), (0, LANES - n_heads)))],
                               axis=1)

    wp = packed(w)
    sm = _adamw(tot[:, :wp.shape[1]], wp, packed(m), packed(v), "adamw_small")
    off = 0
    for k in _SMALL:
        width = w[k].shape[1]
        res[k] = tuple(o[:, off:off + width] for o in sm)
        off += width if k != "b_forget" else LANES

    last = sm[0]
    update_shards(_FFN1)

    outs =[loss_row[0, 0], dx[None]]
    for idx in range(4):
        outs += [_from2d(k, res[k][idx], w[k].shape) for k in _ORDER]
    return tuple(outs)
```

```python
import functools

import jax
import jax.numpy as jnp
from jax import lax
from jax.experimental import pallas as pl
from jax.experimental.pallas import tpu as pltpu

F32 = jnp.float32
BF = jnp.bfloat16
SDS = jax.ShapeDtypeStruct

N_DEV = 8
LANES = 128
SUBLANES = 8
HEAD_DIM = 128
POOL_WINDOWS = (2, 4, 8, 16)
RMS_EPS = 1e-6
NEG_BIG = -1e30
MIB = 1024 * 1024

ADAM_LR = 0.001
ADAM_B1 = 0.9
ADAM_B2 = 0.999
ADAM_EPS = 1e-08
ADAM_WD = 0.01
ADAM_STEP = 10


class _Comm:
    def __init__(self, arrs, out_shape, sems, start, finish, aliases=None):
        self.arrs, self.out_shape, self.sems = list(arrs), list(out_shape), list(sems)
        self.start, self.finish, self.aliases = start, finish, dict(aliases or {})
        self.results = None


def _merge_comm(ops):
    ops = [op for op in ops if op is not None]
    if not ops:
        return None
    na, no, ns = [0], [0], [0]
    for op in ops:
        na.append(na[-1] + len(op.arrs))
        no.append(no[-1] + len(op.out_shape))
        ns.append(ns[-1] + len(op.sems))

    def parts(i, ins, outs, sems):
        return ins[na[i]:na[i + 1]], outs[no[i]:no[i + 1]], sems[ns[i]:ns[i + 1]]

    def start(ins, outs, sems):
        for i, op in enumerate(ops):
            op.start(*parts(i, ins, outs, sems))

    def finish(ins, outs, sems):
        for i, op in enumerate(ops):
            op.finish(*parts(i, ins, outs, sems))

    aliases = {}
    for i, op in enumerate(ops):
        for a, o in op.aliases.items():
            aliases[na[i] + a] = no[i] + o
    merged = _Comm([a for op in ops for a in op.arrs], [s for op in ops for s in op.out_shape],
                   [s for op in ops for s in op.sems], start, finish, aliases)
    merged.children = (ops, no)
    return merged


def _deliver(comm, results):
    comm.results = list(results)
    if hasattr(comm, "children"):
        ops, no = comm.children
        for i, op in enumerate(ops):
            _deliver(op, results[no[i]:no[i + 1]])


def _call(body, *, grid, in_specs, out_specs, out_shape, scratch_shapes=(), vmem_mib, name, comm=None):
    single = not isinstance(out_shape, (list, tuple))
    out_specs = [out_specs] if single else list(out_specs)
    out_shape = [out_shape] if single else list(out_shape)
    in_specs, scratch_shapes = list(in_specs), list(scratch_shapes)
    params = pltpu.CompilerParams(dimension_semantics=("arbitrary",) * len(grid), vmem_limit_bytes=vmem_mib * MIB)
    n_in, n_out, n_scr = len(in_specs), len(out_specs), len(scratch_shapes)

    def run(*args):
        if comm is None:
            res = pl.pallas_call(body, grid=grid, in_specs=in_specs, out_specs=out_specs, out_shape=out_shape,
                                 scratch_shapes=scratch_shapes, compiler_params=params, name=name)(*args)
            return res[0] if single else res
        ci, co = len(comm.arrs), len(comm.out_shape)

        def with_comm(*refs):
            ins, cins = refs[:n_in], refs[n_in:n_in + ci]
            o0 = n_in + ci
            outs, couts = refs[o0:o0 + n_out], refs[o0 + n_out:o0 + n_out + co]
            s0 = o0 + n_out + co
            scr, csems = refs[s0:s0 + n_scr], refs[s0 + n_scr:]
            ids = [pl.program_id(a) for a in range(len(grid))]
            first = functools.reduce(jnp.logical_and, [i == 0 for i in ids])
            last = functools.reduce(jnp.logical_and, [i == g - 1 for i, g in zip(ids, grid)])

            @pl.when(first)
            def _():
                comm.start(cins, couts, csems)

            body(*ins, *outs, *scr)

            @pl.when(last)
            def _():
                comm.finish(cins, couts, csems)

        anyspec = pl.BlockSpec(memory_space=pl.ANY)
        res = pl.pallas_call(
            with_comm, grid=grid, in_specs=in_specs + [anyspec] * ci, out_specs=out_specs + [anyspec] * co,
            out_shape=out_shape + comm.out_shape, scratch_shapes=scratch_shapes + comm.sems,
            input_output_aliases={n_in + a: n_out + o for a, o in comm.aliases.items()},
            compiler_params=params, name=name)(*args, *comm.arrs)
        _deliver(comm, res[n_out:])
        return res[0] if single else res[:n_out]

    return run


def _comm_alone(comm, name):
    def body(*refs):
        ci, co = len(comm.arrs), len(comm.out_shape)
        ins, outs, sems = refs[:ci], refs[ci:ci + co], refs[ci + co:]
        comm.start(ins, outs, sems)
        comm.finish(ins, outs, sems)

    anyspec = pl.BlockSpec(memory_space=pl.ANY)
    res = pl.pallas_call(
        body, in_specs=[anyspec] * len(comm.arrs), out_specs=[anyspec] * len(comm.out_shape),
        out_shape=comm.out_shape, scratch_shapes=comm.sems, input_output_aliases=comm.aliases, name=name)(*comm.arrs)
    _deliver(comm, res)


def _split_start(comm, name):
    na, ns = len(comm.arrs), len(comm.sems)

    def body(*refs):
        comm.start(refs[:na], None, refs[na:na + ns])
        token = refs[-1]
        token[...] = jnp.zeros_like(token)

    hbm = pl.BlockSpec(memory_space=pltpu.HBM)
    res = pl.pallas_call(
        body, name=name,
        out_shape=tuple(comm.sems) + tuple(pltpu.HBM(a.shape, a.dtype) for a in comm.arrs)
        + (SDS((SUBLANES, LANES), F32),),
        in_specs=[hbm] * na,
        out_specs=[pl.BlockSpec(memory_space=pltpu.SEMAPHORE)] * ns + [hbm] * na + [pl.BlockSpec(memory_space=pltpu.VMEM)],
        input_output_aliases={i: ns + i for i in range(na)},
        compiler_params=pltpu.CompilerParams(has_side_effects=pltpu.SideEffectType.DATAFLOW_SIDE_EFFECTING),
    )(*[pltpu.with_memory_space_constraint(a, pltpu.HBM) for a in comm.arrs])
    return (comm, res[:ns], res[ns:ns + na]), res[-1]


def _split_wait(started, after, name):
    comm, sems, thru = started
    na, ns = len(thru), len(sems)

    def body(*refs):
        comm.finish(refs[:na], None, refs[na:na + ns])

    hbm = pl.BlockSpec(memory_space=pltpu.HBM)
    res = pl.pallas_call(
        body, name=name, out_shape=tuple(pltpu.HBM(a.shape, a.dtype) for a in thru),
        in_specs=[hbm] * na + [pl.BlockSpec(memory_space=pltpu.SEMAPHORE)] * ns + [pl.BlockSpec(memory_space=pl.ANY)],
        out_specs=[hbm] * na, input_output_aliases={i: i for i in range(na)},
        compiler_params=pltpu.CompilerParams(has_side_effects=pltpu.SideEffectType.DATAFLOW_SIDE_EFFECTING),
    )(*thru, *sems, after)
    return res[na - len(comm.out_shape):]


def _largest_tile(n, cap, mult):
    if n <= cap:
        return n
    best = None
    for t in range(mult, cap + 1, mult):
        if n % t == 0:
            best = t
    assert best is not None, (n, cap, mult)
    return best


def _dot(a, b):
    return jnp.dot(a, b, preferred_element_type=F32)


def _dot_nt(a, b):
    return lax.dot_general(a, b, (((1,), (1,)), ((), ())), preferred_element_type=F32)


def _dot_tn(a, b):
    return lax.dot_general(a, b, (((0,), (0,)), ((), ())), preferred_element_type=F32)


def _rows8(x):
    t, c = x.shape
    return jnp.sum(x.reshape(t // SUBLANES, SUBLANES, c), axis=0)


def _rstd(x):
    return lax.rsqrt(jnp.mean(x * x, axis=-1, keepdims=True) + RMS_EPS)


def _ffn_fwd(h, g, wg, wu, wd, tm, name, comm=None):
    lp, d = h.shape
    ns, fs, _ = wg.shape

    def body(h_ref, g_ref, wg_ref, wu_ref, wd_ref, out_ref, a_ref, b_ref, u_ref, acc_ref):
        j = pl.program_id(1)

        @pl.when(j == 0)
        def _():
            hh = h_ref[...]
            u_ref[...] = (hh * _rstd(hh) * g_ref[...]).astype(BF)
            acc_ref[...] = jnp.zeros_like(acc_ref)

        u = u_ref[...]
        a = _dot_nt(u, wg_ref[...])
        b = _dot_nt(u, wu_ref[...])
        a_ref[...] = a.astype(BF)
        b_ref[...] = b.astype(BF)
        hid = (a * jax.nn.sigmoid(a) * b).astype(BF)
        acc_ref[...] += _dot(hid, wd_ref[...])

        @pl.when(j == ns - 1)
        def _():
            out_ref[...] = h_ref[...] + 0.5 * acc_ref[...]

    row = pl.BlockSpec((tm, d), lambda i, j: (i, 0))
    act = pl.BlockSpec((None, tm, fs), lambda i, j: (j, i, 0))
    return _call(
        body, grid=(lp // tm, ns),
        in_specs=[row, pl.BlockSpec((1, d), lambda i, j: (0, 0)),
                  pl.BlockSpec((None, fs, d), lambda i, j: (j, 0, 0)),
                  pl.BlockSpec((None, fs, d), lambda i, j: (j, 0, 0)),
                  pl.BlockSpec((None, fs, d), lambda i, j: (j, 0, 0))],
        out_specs=[row, act, act, row],
        out_shape=[SDS((lp, d), F32), SDS((ns, lp, fs), BF), SDS((ns, lp, fs), BF), SDS((lp, d), BF)],
        scratch_shapes=[pltpu.VMEM((tm, d), F32)],
        vmem_mib=56, name=name, comm=comm)(h, g, wg, wu, wd)


def _ffn_bwd_dx(dob, a, b, wg, wu, wd, tm, name, comm=None):
    lp, d = dob.shape
    ns, fs, _ = wg.shape

    def body(do_ref, a_ref, b_ref, wg_ref, wu_ref, wd_ref, du_ref, da_ref, db_ref, hid_ref):
        j = pl.program_id(1)

        @pl.when(j == 0)
        def _():
            du_ref[...] = jnp.zeros_like(du_ref)

        dhid = _dot_nt(do_ref[...], wd_ref[...])
        av = a_ref[...].astype(F32)
        bv = b_ref[...].astype(F32)
        sig = jax.nn.sigmoid(av)
        sil = av * sig
        dbv = (dhid * sil).astype(BF)
        dav = (dhid * bv * (sig * (1.0 + av * (1.0 - sig)))).astype(BF)
        hid_ref[...] = (sil * bv).astype(BF)
        da_ref[...] = dav
        db_ref[...] = dbv
        du_ref[...] += _dot(dav, wg_ref[...]) + _dot(dbv, wu_ref[...])

    row = pl.BlockSpec((tm, d), lambda i, j: (i, 0))
    act = pl.BlockSpec((None, tm, fs), lambda i, j: (j, i, 0))
    return _call(
        body, grid=(lp // tm, ns),
        in_specs=[row, act, act,
                  pl.BlockSpec((None, fs, d), lambda i, j: (j, 0, 0)),
                  pl.BlockSpec((None, fs, d), lambda i, j: (j, 0, 0)),
                  pl.BlockSpec((None, fs, d), lambda i, j: (j, 0, 0))],
        out_specs=[row, act, act, act],
        out_shape=[SDS((lp, d), F32)] + [SDS((ns, lp, fs), BF)] * 3,
        vmem_mib=56, name=name, comm=comm)(dob, a, b, wg, wu, wd)


def _rms_bwd(du, h, g, dres, bscale, tm, name, comm=None):
    lp, d = h.shape

    def body(du_ref, h_ref, g_ref, dres_ref, dh_ref, dhb_ref, dg_ref):
        @pl.when(pl.program_id(0) == 0)
        def _():
            dg_ref[...] = jnp.zeros_like(dg_ref)

        hh = h_ref[...]
        r = _rstd(hh)
        xhat = hh * r
        duv = du_ref[...]
        dg_ref[...] += _rows8(duv * xhat)
        dxh = duv * g_ref[...]
        dh = dres_ref[...] + r * (dxh - xhat * jnp.mean(dxh * xhat, axis=-1, keepdims=True))
        dh_ref[...] = dh
        dhb_ref[...] = (bscale * dh).astype(BF)

    row = pl.BlockSpec((tm, d), lambda i: (i, 0))
    return _call(
        body, grid=(lp // tm,),
        in_specs=[row, row, pl.BlockSpec((1, d), lambda i: (0, 0)), row],
        out_specs=[row, row, pl.BlockSpec((SUBLANES, d), lambda i: (0, 0))],
        out_shape=[SDS((lp, d), F32), SDS((lp, d), BF), SDS((SUBLANES, d), F32)],
        vmem_mib=48, name=name, comm=comm)(du, h, g, dres)


def _matmul_tn(a, b, tm, tn, name, comm=None):
    a_b, b_b = a.ndim == 3, b.ndim == 3
    ns = a.shape[0] if a_b else (b.shape[0] if b_b else 1)
    l, m = a.shape[-2:]
    n = b.shape[-1]

    def body(a_ref, b_ref, o_ref):
        o_ref[...] = _dot_tn(a_ref[...], b_ref[...]).astype(o_ref.dtype)

    a_spec = (pl.BlockSpec((None, l, tm), lambda s, i, j: (s, 0, i)) if a_b
              else pl.BlockSpec((l, tm), lambda s, i, j: (0, i)))
    b_spec = (pl.BlockSpec((None, l, tn), lambda s, i, j: (s, 0, j)) if b_b
              else pl.BlockSpec((l, tn), lambda s, i, j: (0, j)))
    batched = a_b or b_b
    o_spec = (pl.BlockSpec((None, tm, tn), lambda s, i, j: (s, i, j)) if batched
              else pl.BlockSpec((tm, tn), lambda s, i, j: (i, j)))
    o_shape = SDS((ns, m, n), BF) if batched else SDS((m, n), BF)
    return _call(
        body, grid=(ns, m // tm, n // tn), in_specs=[a_spec, b_spec], out_specs=o_spec, out_shape=o_shape,
        vmem_mib=48, name=name, comm=comm)(a, b)


def _matmul_nt(x, w, tm, tk, out_dtype, name, comm=None):
    l, k = x.shape
    n = w.shape[0]
    nk = k // tk

    def body(x_ref, w_ref, o_ref, acc_ref):
        kk = pl.program_id(1)

        @pl.when(kk == 0)
        def _():
            acc_ref[...] = jnp.zeros_like(acc_ref)

        acc_ref[...] += _dot_nt(x_ref[...], w_ref[...])

        @pl.when(kk == nk - 1)
        def _():
            o_ref[...] = acc_ref[...].astype(o_ref.dtype)

    return _call(
        body, grid=(l // tm, nk),
        in_specs=[pl.BlockSpec((tm, tk), lambda i, kk: (i, kk)), pl.BlockSpec((n, tk), lambda i, kk: (0, kk))],
        out_specs=pl.BlockSpec((tm, n), lambda i, kk: (i, 0)),
        out_shape=SDS((l, n), out_dtype),
        scratch_shapes=[pltpu.VMEM((tm, n), F32)],
        vmem_mib=48, name=name, comm=comm)(x, w)


def _norm_matmul(h, g, w, tm, tn, name, comm=None):
    lp, d = h.shape
    n = w.shape[1]

    def body(h_ref, g_ref, w_ref, z_ref, u_ref):
        @pl.when(pl.program_id(1) == 0)
        def _():
            hh = h_ref[...]
            u_ref[...] = (hh * _rstd(hh) * g_ref[...]).astype(BF)

        z_ref[...] = _dot(u_ref[...], w_ref[...])

    row = pl.BlockSpec((tm, d), lambda i, j: (i, 0))
    return _call(
        body, grid=(lp // tm, n // tn),
        in_specs=[row, pl.BlockSpec((1, d), lambda i, j: (0, 0)), pl.BlockSpec((d, tn), lambda i, j: (0, j))],
        out_specs=[pl.BlockSpec((tm, tn), lambda i, j: (i, j)), row],
        out_shape=[SDS((lp, n), F32), SDS((lp, d), BF)],
        vmem_mib=48, name=name, comm=comm)(h, g, w)


def _out_proj(h, pool_o, att_o, w_out, tm, name, comm=None):
    lp, d = h.shape
    p = pool_o.shape[1]
    dm = w_out.shape[0]

    def body(h_ref, p_ref, a_ref, w_ref, o_ref):
        o_ref[...] = h_ref[...] + _dot(p_ref[...], w_ref[0:p, :]) + _dot(a_ref[...], w_ref[p:dm, :])

    row = pl.BlockSpec((tm, d), lambda i: (i, 0))
    return _call(
        body, grid=(lp // tm,),
        in_specs=[row, pl.BlockSpec((tm, p), lambda i: (i, 0)), pl.BlockSpec((tm, dm - p), lambda i: (i, 0)),
                  pl.BlockSpec((dm, d), lambda i: (0, 0))],
        out_specs=row, out_shape=SDS((lp, d), F32),
        vmem_mib=48, name=name, comm=comm)(h, pool_o, att_o, w_out)


def _loss_head(y, tpad, row0, row1, tm, name, comm=None):
    lp, d = y.shape

    def body(y_ref, t_ref, dy_ref, dob_ref, ls_ref):
        i = pl.program_id(0)

        @pl.when(i == 0)
        def _():
            ls_ref[...] = jnp.zeros_like(ls_ref)

        rows = i * tm + lax.broadcasted_iota(jnp.int32, (tm, d), 0)
        err = jnp.where((rows >= row0) & (rows < row1), y_ref[...] - t_ref[...], 0.0)
        dy = err * (1.0 / d)
        dy_ref[...] = dy
        dob_ref[...] = (0.5 * dy).astype(BF)
        sq = _rows8(err * err)
        acc = sq[:, 0:LANES]
        for c in range(1, d // LANES):
            acc = acc + sq[:, c * LANES:(c + 1) * LANES]
        ls_ref[...] += acc

    row = pl.BlockSpec((tm, d), lambda i: (i, 0))
    return _call(
        body, grid=(lp // tm,), in_specs=[row, row],
        out_specs=[row, row, pl.BlockSpec((SUBLANES, LANES), lambda i: (0, 0))],
        out_shape=[SDS((lp, d), F32), SDS((lp, d), BF), SDS((SUBLANES, LANES), F32)],
        vmem_mib=48, name=name, comm=comm)(y, tpad)


def _window_select(levels, gidx):
    out = levels[-1]
    for k in range(len(levels) - 2, -1, -1):
        out = jnp.where(gidx == k, levels[k], out)
    return out


def _pool_window_mean_minus_id(x, gidx):
    rows = lax.broadcasted_iota(jnp.int32, x.shape, 0)
    levels = []
    s = x
    shift = 1
    while shift < POOL_WINDOWS[-1]:
        s = s + jnp.where(rows >= shift, pltpu.roll(s, shift, 0), 0.0)
        shift *= 2
        if shift in POOL_WINDOWS:
            levels.append(s)
    win = _window_select(levels, gidx)
    cnt = jnp.minimum(rows + 1, _window_select(list(POOL_WINDOWS), gidx)).astype(F32)
    return win / cnt - x, cnt


def _pool_window_transpose(dy, cnt, gidx):
    lp = dy.shape[0]
    rows = lax.broadcasted_iota(jnp.int32, dy.shape, 0)
    levels = []
    s = dy / cnt
    shift = 1
    while shift < POOL_WINDOWS[-1]:
        s = s + jnp.where(rows < lp - shift, pltpu.roll(s, lp - shift, 0), 0.0)
        shift *= 2
        if shift in POOL_WINDOWS:
            levels.append(s)
    return _window_select(levels, gidx) - dy


def _pool_fwd(z, pool_w, pool_scale, name, comm=None):
    lp = z.shape[0]
    ng, gw, _ = pool_w.shape

    def body(p_ref, w_ref, s_ref, o_ref):
        pooled, _ = _pool_window_mean_minus_id(p_ref[...], pl.program_id(0))
        o_ref[...] = (_dot(pooled.astype(BF), w_ref[...]) * s_ref[...]).astype(BF)

    return _call(
        body, grid=(ng,),
        in_specs=[pl.BlockSpec((lp, gw), lambda g: (0, g)), pl.BlockSpec((None, gw, gw), lambda g: (g, 0, 0)),
                  pl.BlockSpec((1, gw), lambda g: (0, g))],
        out_specs=pl.BlockSpec((lp, gw), lambda g: (0, g)), out_shape=SDS((lp, ng * gw), BF),
        vmem_mib=48, name=name, comm=comm)(z, pool_w, pool_scale)


def _pool_bwd(z, dmix, pool_w, pool_scale, name, comm=None):
    lp = z.shape[0]
    ng, gw, _ = pool_w.shape

    def body(p_ref, d_ref, w_ref, s_ref, dz_ref, dw_ref, ds_ref):
        g = pl.program_id(0)
        pooled, cnt = _pool_window_mean_minus_id(p_ref[...], g)
        pooled_b = pooled.astype(BF)
        w = w_ref[...]
        mixed = _dot(pooled_b, w)
        dpo = d_ref[...].astype(F32)
        ds_ref[...] = _rows8(dpo * mixed)
        dmixed = (dpo * s_ref[...]).astype(BF)
        dw_ref[...] = _dot_tn(pooled_b, dmixed)
        dpooled = _dot_nt(dmixed, w)
        dz_ref[...] = _pool_window_transpose(dpooled, cnt, g).astype(BF)

    return _call(
        body, grid=(ng,),
        in_specs=[pl.BlockSpec((lp, gw), lambda g: (0, g)), pl.BlockSpec((lp, gw), lambda g: (0, g)),
                  pl.BlockSpec((None, gw, gw), lambda g: (g, 0, 0)), pl.BlockSpec((1, gw), lambda g: (0, g))],
        out_specs=[pl.BlockSpec((lp, gw), lambda g: (0, g)), pl.BlockSpec((None, gw, gw), lambda g: (g, 0, 0)),
                   pl.BlockSpec((SUBLANES, gw), lambda g: (0, g))],
        out_shape=[SDS((lp, ng * gw), BF), SDS((ng, gw, gw), F32), SDS((SUBLANES, ng * gw), F32)],
        vmem_mib=48, name=name, comm=comm)(z, dmix, pool_w, pool_scale)


def _log_sigmoid(x):
    return jnp.minimum(x, 0.0) - jnp.log(1.0 + jnp.exp(-jnp.abs(x)))


def _fox_prep(z, bfp, fblk, name, comm=None):
    lp = z.shape[0]
    nb = lp // LANES

    def body(f_ref, b_ref, cum_ref):
        r = lax.broadcasted_iota(jnp.int32, (LANES, LANES), 0)
        c = lax.broadcasted_iota(jnp.int32, (LANES, LANES), 1)
        tri = (r >= c).astype(F32)
        carry = jnp.zeros((1, LANES), F32)
        for blk in range(nb):
            sl = slice(blk * LANES, (blk + 1) * LANES)
            lf = _log_sigmoid(f_ref[sl, :] + b_ref[...])
            cb = jnp.dot(tri, lf, preferred_element_type=F32, precision=lax.Precision.HIGHEST) + carry
            cum_ref[sl, :] = cb
            carry = cb[LANES - 1:LANES, :]

    return _call(
        body, grid=(1,),
        in_specs=[pl.BlockSpec((lp, LANES), lambda i: (0, fblk)), pl.BlockSpec((1, LANES), lambda i: (0, 0))],
        out_specs=pl.BlockSpec((lp, LANES), lambda i: (0, 0)), out_shape=SDS((lp, LANES), F32),
        vmem_mib=32, name=name, comm=comm)(z, bfp)


def _fox_bwd(z, bfp, dcum, fblk, name, comm=None):
    lp = z.shape[0]
    nb = lp // LANES

    def body(f_ref, b_ref, dc_ref, dz_ref, db_ref):
        r = lax.broadcasted_iota(jnp.int32, (LANES, LANES), 0)
        c = lax.broadcasted_iota(jnp.int32, (LANES, LANES), 1)
        tri = (r <= c).astype(F32)
        carry = jnp.zeros((1, LANES), F32)
        acc = jnp.zeros((SUBLANES, LANES), F32)
        for blk in range(nb - 1, -1, -1):
            sl = slice(blk * LANES, (blk + 1) * LANES)
            dlf = jnp.dot(tri, dc_ref[sl, :], preferred_element_type=F32, precision=lax.Precision.HIGHEST) + carry
            carry = dlf[0:1, :]
            df = dlf * jax.nn.sigmoid(-(f_ref[sl, :] + b_ref[...]))
            dz_ref[sl, :] = df.astype(BF)
            acc = acc + _rows8(df)
        db_ref[...] = acc

    return _call(
        body, grid=(1,),
        in_specs=[pl.BlockSpec((lp, LANES), lambda i: (0, fblk)), pl.BlockSpec((1, LANES), lambda i: (0, 0)),
                  pl.BlockSpec((lp, LANES), lambda i: (0, 0))],
        out_specs=[pl.BlockSpec((lp, LANES), lambda i: (0, 0)), pl.BlockSpec((SUBLANES, LANES), lambda i: (0, 0))],
        out_shape=[SDS((lp, LANES), BF), SDS((SUBLANES, LANES), F32)],
        vmem_mib=32, name=name, comm=comm)(z, bfp, dcum)


def _att_scores(q_ref, cum_ref, cumt_ref, qw_ref, kn_s, h, i, tq, lk):
    scale = 1.0 / (HEAD_DIM ** 0.5)
    q = q_ref[...]
    rq = _rstd(q)
    qhat = q * rq
    qn = (qhat * qw_ref[...]).astype(BF)
    s = _dot_nt(qn, kn_s[0:lk, :]) * scale
    lane = lax.broadcasted_iota(jnp.int32, (tq, LANES), 1)
    cq = jnp.sum(jnp.where(lane == h, cum_ref[...], 0.0), axis=1, keepdims=True)
    ck = cumt_ref[pl.ds(h, 1), 0:lk]
    s = s + (cq - ck)
    qpos = i * tq + lax.broadcasted_iota(jnp.int32, (tq, lk), 0)
    kpos = lax.broadcasted_iota(jnp.int32, (tq, lk), 1)
    s = jnp.where(qpos >= kpos, s, NEG_BIG)
    e = jnp.exp(s - jnp.max(s, axis=1, keepdims=True))
    p = e / jnp.sum(e, axis=1, keepdims=True)
    return p, qn, qhat, rq


def _per_query_tile(i, nq, tq, lp, fn):
    for t in range(nq):
        lk = min(lp, -(-((t + 1) * tq) // LANES) * LANES)
        pl.when(i == t)(functools.partial(fn, lk))


def _att_fwd(z, cum, cumt, qw, kw, n_heads, qblk0, tq, name, comm=None):
    lp = z.shape[0]
    nh = n_heads

    def body(q_ref, k_ref, v_ref, cum_ref, cumt_ref, qw_ref, kw_ref, o_ref, kn_s, vb_s):
        h, i = pl.program_id(0), pl.program_id(1)

        @pl.when(i == 0)
        def _():
            k = k_ref[...]
            kn_s[...] = (k * _rstd(k) * kw_ref[...]).astype(BF)
            vb_s[...] = v_ref[...].astype(BF)

        def tile(lk):
            p, _, _, _ = _att_scores(q_ref, cum_ref, cumt_ref, qw_ref, kn_s, h, i, tq, lk)
            o_ref[...] = _dot(p.astype(BF), vb_s[0:lk, :]).astype(BF)

        _per_query_tile(i, lp // tq, tq, lp, tile)

    vec = pl.BlockSpec((1, HEAD_DIM), lambda h, i: (0, 0))
    return _call(
        body, grid=(nh, lp // tq),
        in_specs=[pl.BlockSpec((tq, HEAD_DIM), lambda h, i: (i, qblk0 + h)),
                  pl.BlockSpec((lp, HEAD_DIM), lambda h, i: (0, qblk0 + nh + h)),
                  pl.BlockSpec((lp, HEAD_DIM), lambda h, i: (0, qblk0 + 2 * nh + h)),
                  pl.BlockSpec((tq, LANES), lambda h, i: (i, 0)),
                  pl.BlockSpec((nh, lp), lambda h, i: (0, 0)), vec, vec],
        out_specs=pl.BlockSpec((tq, HEAD_DIM), lambda h, i: (i, h)),
        out_shape=SDS((lp, nh * HEAD_DIM), BF),
        scratch_shapes=[pltpu.VMEM((lp, HEAD_DIM), BF), pltpu.VMEM((lp, HEAD_DIM), BF)],
        vmem_mib=48, name=name, comm=comm)(z, z, z, cum, cumt, qw, kw)


def _att_bwd(z, cum, cumt, qw, kw, dmix, n_heads, qblk0, oblk0, tq, name, comm=None):
    lp = z.shape[0]
    nh = n_heads
    nq = lp // tq
    scale = 1.0 / (HEAD_DIM ** 0.5)

    def body(q_ref, k_ref, v_ref, cum_ref, cumt_ref, qw_ref, kw_ref, do_ref,
             dq_ref, dk_ref, dv_ref, dcq_ref, dck_ref, dqw_ref, dkw_ref,
             kn_s, vb_s, dkn_s, dv_s, dck_s):
        h, i = pl.program_id(0), pl.program_id(1)

        @pl.when((h == 0) & (i == 0))
        def _():
            dqw_ref[...] = jnp.zeros_like(dqw_ref)
            dkw_ref[...] = jnp.zeros_like(dkw_ref)

        @pl.when(i == 0)
        def _():
            k = k_ref[...]
            kn_s[...] = (k * _rstd(k) * kw_ref[...]).astype(BF)
            vb_s[...] = v_ref[...].astype(BF)
            dkn_s[...] = jnp.zeros_like(dkn_s)
            dv_s[...] = jnp.zeros_like(dv_s)
            dck_s[...] = jnp.zeros_like(dck_s)

        def tile(lk):
            p, qn, qhat, rq = _att_scores(q_ref, cum_ref, cumt_ref, qw_ref, kn_s, h, i, tq, lk)
            dob = do_ref[...]
            dp = _dot_nt(dob, vb_s[0:lk, :])
            ds = p * (dp - jnp.sum(p * dp, axis=1, keepdims=True))
            dsb = ds.astype(BF)
            dv_s[0:lk, :] += _dot_tn(p.astype(BF), dob)
            dkn_s[0:lk, :] += _dot_tn(dsb, qn)
            dcq_ref[...] = jnp.sum(ds, axis=1, keepdims=True)
            dck_s[:, 0:lk] += jnp.sum(ds, axis=0, keepdims=True)
            dqn = _dot(dsb, kn_s[0:lk, :]) * scale
            gq = dqn * qw_ref[...]
            dq_ref[...] = (rq * (gq - qhat * jnp.mean(gq * qhat, axis=-1, keepdims=True))).astype(BF)
            dqw_ref[...] += _rows8(dqn * qhat)

        _per_query_tile(i, nq, tq, lp, tile)

        @pl.when(i == nq - 1)
        def _():
            k = k_ref[...]
            rk = _rstd(k)
            khat = k * rk
            dkn = dkn_s[...] * scale
            gk = dkn * kw_ref[...]
            dk_ref[...] = (rk * (gk - khat * jnp.mean(gk * khat, axis=-1, keepdims=True))).astype(BF)
            dkw_ref[...] += _rows8(dkn * khat)
            dv_ref[...] = dv_s[...].astype(BF)
            dck_ref[...] = dck_s[...]

    vec = pl.BlockSpec((1, HEAD_DIM), lambda h, i: (0, 0))
    part = pl.BlockSpec((SUBLANES, LANES), lambda h, i: (0, 0))
    return _call(
        body, grid=(nh, nq),
        in_specs=[pl.BlockSpec((tq, HEAD_DIM), lambda h, i: (i, qblk0 + h)),
                  pl.BlockSpec((lp, HEAD_DIM), lambda h, i: (0, qblk0 + nh + h)),
                  pl.BlockSpec((lp, HEAD_DIM), lambda h, i: (0, qblk0 + 2 * nh + h)),
                  pl.BlockSpec((tq, LANES), lambda h, i: (i, 0)),
                  pl.BlockSpec((nh, lp), lambda h, i: (0, 0)), vec, vec,
                  pl.BlockSpec((tq, HEAD_DIM), lambda h, i: (i, oblk0 + h))],
        out_specs=[pl.BlockSpec((tq, HEAD_DIM), lambda h, i: (i, h)),
                   pl.BlockSpec((lp, HEAD_DIM), lambda h, i: (0, h)),
                   pl.BlockSpec((lp, HEAD_DIM), lambda h, i: (0, h)),
                   pl.BlockSpec((None, tq, 1), lambda h, i: (h, i, 0)),
                   pl.BlockSpec((None, 1, lp), lambda h, i: (h, 0, 0)),
                   part, part],
        out_shape=[SDS((lp, nh * HEAD_DIM), BF)] * 3
        + [SDS((nh, lp, 1), F32), SDS((nh, 1, lp), F32), SDS((SUBLANES, LANES), F32), SDS((SUBLANES, LANES), F32)],
        scratch_shapes=[pltpu.VMEM((lp, HEAD_DIM), BF), pltpu.VMEM((lp, HEAD_DIM), BF),
                        pltpu.VMEM((lp, HEAD_DIM), F32), pltpu.VMEM((lp, HEAD_DIM), F32),
                        pltpu.VMEM((1, lp), F32)],
        vmem_mib=56, name=name, comm=comm)(z, z, z, cum, cumt, qw, kw, dmix)


def _adamw_math(w, g, m, v):
    m2 = ADAM_B1 * m + (1.0 - ADAM_B1) * g
    v2 = ADAM_B2 * v + (1.0 - ADAM_B2) * (g * g)
    m_hat = m2 / (1.0 - ADAM_B1 ** ADAM_STEP)
    v_hat = v2 / (1.0 - ADAM_B2 ** ADAM_STEP)
    delta = -ADAM_LR * (m_hat / (jnp.sqrt(v_hat) + ADAM_EPS) + ADAM_WD * w)
    return delta, m2, v2


def _adamw(g_in, w, m, v, name, comm=None):
    r, c = w.shape
    partial_sum = g_in.ndim == 3
    lane_padded = -(-c // LANES) * LANES
    tr = _largest_tile(r, max(16, MIB // (4 * lane_padded) // 16 * 16), 16)

    def body(g_ref, w_ref, m_ref, v_ref, go_ref, d_ref, mo_ref, vo_ref):
        if partial_sum:
            g = g_ref[0].astype(F32)
            for k in range(1, g_in.shape[0]):
                g = g + g_ref[k].astype(F32)
        else:
            g = g_ref[...]
        delta, m2, v2 = _adamw_math(w_ref[...], g, m_ref[...], v_ref[...])
        go_ref[...] = g
        d_ref[...] = delta
        mo_ref[...] = m2
        vo_ref[...] = v2

    blk = pl.BlockSpec((tr, c), lambda i: (i, 0))
    g_spec = pl.BlockSpec((g_in.shape[0], tr, c), lambda i: (0, i, 0)) if partial_sum else blk
    return _call(
        body, grid=(r // tr,), in_specs=[g_spec, blk, blk, blk], out_specs=[blk] * 4,
        out_shape=[SDS((r, c), F32)] * 4, vmem_mib=40, name=name, comm=comm)(g_in, w, m, v)


def _peer(x, y, c, k):
    return (1 - x if k & 4 else x, 1 - y if k & 2 else y, 1 - c if k & 1 else c)


def _exchange(arrs, scatter, name, comm=None):
    n = len(arrs)

    def body(*refs):
        ins, outs = refs[:n], refs[n:2 * n]
        send_sems, recv_sems, local_sems = refs[2 * n:]
        x, y, c = lax.axis_index("x"), lax.axis_index("y"), lax.axis_index("c")
        me = 4 * x + 2 * y + c

        def src(t, dev):
            return ins[t].at[dev] if scatter else ins[t]

        def copy(t, k, arrival):
            px, py, pc = _peer(x, y, c, k)
            dev = 4 * px + 2 * py + pc
            return pltpu.make_async_remote_copy(
                src_ref=src(t, dev), dst_ref=outs[t].at[dev if arrival else me],
                send_sem=send_sems.at[t, k - 1], recv_sem=recv_sems.at[t, k - 1],
                device_id=(px, py, pc), device_id_type=pl.DeviceIdType.MESH)

        local = [pltpu.make_async_copy(src(t, me), outs[t].at[me], local_sems.at[t]) for t in range(n)]
        for cp in local:
            cp.start()
        pairs = [(t, k) for k in range(1, N_DEV) for t in range(n)]
        for t, k in pairs:
            copy(t, k, False).start()
        for cp in local:
            cp.wait()
        for t, k in pairs:
            copy(t, k, True).wait_recv()
        for t, k in pairs:
            copy(t, k, False).wait_send()

    out_shape = [SDS(a.shape if scatter else (N_DEV,) + a.shape, a.dtype) for a in arrs]
    anyspec = pl.BlockSpec(memory_space=pl.ANY)
    return pl.pallas_call(
        body, in_specs=[anyspec] * n, out_specs=[anyspec] * n, out_shape=out_shape,
        scratch_shapes=[pltpu.SemaphoreType.DMA((n, N_DEV - 1)), pltpu.SemaphoreType.DMA((n, N_DEV - 1)),
                        pltpu.SemaphoreType.DMA((n,))],
        name=name)(*arrs)


_SIBLING = 1
_ICI_RELS = (2, 4, 6)


def _mesh_pos():
    return lax.axis_index("x"), lax.axis_index("y"), lax.axis_index("c")


def _sem_pair(sems, t, j, n_rel, scalars):
    if scalars:
        i = 2 * (t * n_rel + j)
        return sems[i], sems[i + 1]
    return sems[0].at[t, j], sems[1].at[t, j]


def _dev(pos):
    return 4 * pos[0] + 2 * pos[1] + pos[2]


def _gather_ici(shards, landing=None):
    n = len(shards)
    rels = (_SIBLING,) + _ICI_RELS

    def remote(ins, outs, sems, arrival):
        x, y, c = _mesh_pos()
        dst = ins[n:] if landing is not None else outs
        cps = []
        for j, k in enumerate(rels):
            peer = _peer(x, y, c, k)
            slot = _dev(peer) if arrival else _dev((x, y, c))
            for t in range(n):
                send_sem, recv_sem = _sem_pair(sems, t, j, len(rels), landing is not None)
                cps.append(pltpu.make_async_remote_copy(
                    src_ref=ins[t], dst_ref=dst[t].at[slot], send_sem=send_sem, recv_sem=recv_sem,
                    device_id=peer, device_id_type=pl.DeviceIdType.MESH))
        return cps

    if landing is not None:
        def start_remote(ins, outs, sems):
            for cp in remote(ins, outs, sems, False):
                cp.start()

        def finish_remote(ins, outs, sems):
            for cp in remote(ins, outs, sems, True):
                cp.wait_recv()
            for cp in remote(ins, outs, sems, False):
                cp.wait_send()

        return _Comm(list(shards) + list(landing), [SDS(a.shape, a.dtype) for a in landing],
                     [pltpu.SemaphoreType.DMA(())] * (2 * n * len(rels)),
                     start_remote, finish_remote, aliases={n + t: t for t in range(n)})

    def local(ins, outs, sems):
        me = _dev(_mesh_pos())
        return [pltpu.make_async_copy(ins[t], outs[t].at[me], sems[2].at[t]) for t in range(n)]

    def start(ins, outs, sems):
        for cp in local(ins, outs, sems) + remote(ins, outs, sems, False):
            cp.start()

    def finish(ins, outs, sems):
        for cp in local(ins, outs, sems):
            cp.wait()
        for cp in remote(ins, outs, sems, True):
            cp.wait_recv()
        for cp in remote(ins, outs, sems, False):
            cp.wait_send()

    return _Comm(shards, [SDS((N_DEV,) + s.shape, s.dtype) for s in shards],
                 [pltpu.SemaphoreType.DMA((n, len(rels))), pltpu.SemaphoreType.DMA((n, len(rels))),
                  pltpu.SemaphoreType.DMA((n,))], start, finish)


def _gather_fwd(partial):
    n = len(partial)

    def copies(ins, outs, sems, arrival):
        x, y, c = _mesh_pos()
        sibling = _peer(x, y, c, _SIBLING)
        cps = []
        for j, k in enumerate(_ICI_RELS):
            slot = _dev(_peer(x, y, c, k | _SIBLING if arrival else k))
            for t in range(n):
                cps.append(pltpu.make_async_remote_copy(
                    src_ref=ins[t].at[slot], dst_ref=outs[t].at[slot], send_sem=sems[0].at[t, j],
                    recv_sem=sems[1].at[t, j], device_id=sibling, device_id_type=pl.DeviceIdType.MESH))
        return cps

    def start(ins, outs, sems):
        for cp in copies(ins, outs, sems, False):
            cp.start()

    def finish(ins, outs, sems):
        for cp in copies(ins, outs, sems, True):
            cp.wait_recv()
        for cp in copies(ins, outs, sems, False):
            cp.wait_send()

    return _Comm(partial, [SDS(a.shape, a.dtype) for a in partial],
                 [pltpu.SemaphoreType.DMA((n, len(_ICI_RELS)))] * 2, start, finish,
                 aliases={t: t for t in range(n)})


def _scatter_sibling(slots):
    n = len(slots)

    def copies(ins, outs, sems):
        x, y, c = _mesh_pos()
        return [pltpu.make_async_remote_copy(
            src_ref=ins[t].at[:, 1 - c], dst_ref=outs[t], send_sem=sems[0].at[t], recv_sem=sems[1].at[t],
            device_id=_peer(x, y, c, _SIBLING), device_id_type=pl.DeviceIdType.MESH) for t in range(n)]

    def start(ins, outs, sems):
        for cp in copies(ins, outs, sems):
            cp.start()

    def finish(ins, outs, sems):
        for cp in copies(ins, outs, sems):
            cp.wait()

    return _Comm(slots, [SDS((s.shape[0],) + s.shape[2:], s.dtype) for s in slots],
                 [pltpu.SemaphoreType.DMA((n,))] * 2, start, finish)


def _scatter_ici(chip_sums, landing=None):
    n = len(chip_sums)

    def remote(ins, outs, sems, arrival):
        x, y, c = _mesh_pos()
        dst = ins[n:] if landing is not None else outs
        cps = []
        for j, k in enumerate(_ICI_RELS):
            peer = _peer(x, y, c, k)
            theirs, mine = 2 * peer[0] + peer[1], 2 * x + y
            for t in range(n):
                send_sem, recv_sem = _sem_pair(sems, t, j, len(_ICI_RELS), landing is not None)
                cps.append(pltpu.make_async_remote_copy(
                    src_ref=ins[t].at[theirs], dst_ref=dst[t].at[theirs if arrival else mine],
                    send_sem=send_sem, recv_sem=recv_sem,
                    device_id=peer, device_id_type=pl.DeviceIdType.MESH))
        return cps

    if landing is not None:
        def start_remote(ins, outs, sems):
            for cp in remote(ins, outs, sems, False):
                cp.start()

        def finish_remote(ins, outs, sems):
            for cp in remote(ins, outs, sems, True):
                cp.wait_recv()
            for cp in remote(ins, outs, sems, False):
                cp.wait_send()

        return _Comm(list(chip_sums) + list(landing), [SDS(a.shape, a.dtype) for a in landing],
                     [pltpu.SemaphoreType.DMA(())] * (2 * n * len(_ICI_RELS)), start_remote, finish_remote,
                     aliases={n + t: t for t in range(n)})

    def local(ins, outs, sems):
        x, y, _ = _mesh_pos()
        return [pltpu.make_async_copy(ins[t].at[2 * x + y], outs[t].at[2 * x + y], sems[2].at[t]) for t in range(n)]

    def start(ins, outs, sems):
        for cp in local(ins, outs, sems) + remote(ins, outs, sems, False):
            cp.start()

    def finish(ins, outs, sems):
        for cp in local(ins, outs, sems):
            cp.wait()
        for cp in remote(ins, outs, sems, True):
            cp.wait_recv()
        for cp in remote(ins, outs, sems, False):
            cp.wait_send()

    return _Comm(chip_sums, [SDS(a.shape, a.dtype) for a in chip_sums],
                 [pltpu.SemaphoreType.DMA((n, len(_ICI_RELS))), pltpu.SemaphoreType.DMA((n, len(_ICI_RELS))),
                  pltpu.SemaphoreType.DMA((n,))], start, finish)


def _chip_sum(slots, from_sibling, core, name):
    nq, _, r, c = slots.shape
    tr = _largest_tile(r, 1024, 16)

    def body(core_ref, a_ref, b_ref, o_ref):
        o_ref[...] = (a_ref[...].astype(F32) + b_ref[...].astype(F32)).astype(BF)

    return pl.pallas_call(
        body,
        grid_spec=pltpu.PrefetchScalarGridSpec(
            num_scalar_prefetch=1, grid=(nq, r // tr),
            in_specs=[pl.BlockSpec((None, None, tr, c), lambda q, i, core_ref: (q, core_ref[0], i, 0)),
                      pl.BlockSpec((None, tr, c), lambda q, i, core_ref: (q, i, 0))],
            out_specs=pl.BlockSpec((None, tr, c), lambda q, i, core_ref: (q, i, 0))),
        out_shape=SDS((nq, r, c), BF), compiler_params=pltpu.CompilerParams(vmem_limit_bytes=40 * MIB),
        name=name)(core, slots, from_sibling)


def _small_reduce(pack_g, meta_g, loss_scale, name, comm=None):
    w = pack_g.shape[2]

    def body(p_ref, m_ref, tot_ref, meta_ref, loss_ref):
        acc = p_ref[0]
        macc = m_ref[0]
        for k in range(1, N_DEV):
            acc = acc + p_ref[k]
            macc = macc + m_ref[k]
        tot = jnp.sum(acc, axis=0, keepdims=True)
        tot_ref[...] = tot
        meta_ref[...] = macc
        loss_ref[...] = jnp.full((1, LANES), loss_scale * jnp.sum(tot[:, w - LANES:w]), F32)

    return pl.pallas_call(
        body, out_shape=[SDS((1, w), F32), SDS(meta_g.shape[1:], F32), SDS((1, LANES), F32)],
        compiler_params=pltpu.CompilerParams(vmem_limit_bytes=32 * MIB), name=name)(pack_g, meta_g)


def _local_step(x, target, sw, plan):
    s_len, d = x.shape
    n_heads = plan.n_heads
    plan.at("start")
    meta = plan.weights("meta")
    n_meta = meta.shape[0]
    l = n_meta + s_len
    lp = -(-l // LANES) * LANES
    tm = _largest_tile(lp, 544, 16)
    tq = _largest_tile(lp, 272, 16)
    te = _largest_tile(lp, 272, 16)
    tmd = _largest_tile(d, 512, LANES)

    zpad = jnp.zeros((lp - l, d), F32)
    h0 = jnp.concatenate([meta, x, zpad], axis=0)
    tpad = jnp.concatenate([jnp.zeros((n_meta, d), F32), target, zpad], axis=0)

    wg1, wu1, wd1 = plan.weights("ffn1")
    fs = wg1.shape[1]
    h1, a1, b1, u1 = _ffn_fwd(h0, sw["ffn1_norm"], wg1, wu1, wd1, tm, "ffn1_fwd", plan.comm("ffn1_fwd"))
    plan.at("after_ffn1_fwd", h1)
    win, pw, wout = plan.weights("mix")
    nz = win.shape[1]
    p_w = sw["pool_scale"].shape[1]
    npb = p_w // LANES
    fblk = nz // LANES - 1
    tnz = _largest_tile(nz, 1408, LANES)
    qw, kw, bfp, ps = sw["q_norm"], sw["k_norm"], sw["b_forget"], sw["pool_scale"]
    z, u2 = _norm_matmul(h1, sw["mix_norm"], win, tm, tnz, "mix_in", plan.comm("mix_in"))
    cum = _fox_prep(z, bfp, fblk, "fox_prep")
    cumt = cum[:, :n_heads].T
    pool_o = _pool_fwd(z, pw, ps, "pool_fwd")
    att_o = _att_fwd(z, cum, cumt, qw, kw, n_heads, npb, tq, "att_fwd", plan.comm("att_fwd"))
    plan.at("after_att_fwd", att_o)
    h2 =_out_proj(h1, pool_o, att_o, wout, tm, "out_proj", plan.comm("out_proj"))
    wg2, wu2, wd2 = plan.weights("ffn2")
    h3, a2, b2, u3 = _ffn_fwd(h2, sw["ffn2_norm"], wg2, wu2, wd2, tm, "ffn2_fwd", plan.comm("ffn2_fwd"))
    dy, dob3, lsq = _loss_head(h3, tpad, n_meta, l, te, "loss_head")

    du3, da2, db2, hid2 = _ffn_bwd_dx(dob3, a2, b2, wg2, wu2, wd2, tm, "ffn2_bwd_dx", plan.comm("ffn2_bwd_dx"))
    dh2, dh2b, dn2 = _rms_bwd(du3, h2, sw["ffn2_norm"], dy, 1.0, te, "ffn2_rms_bwd")
    plan.grad("ffn2_w_gate", _matmul_tn(da2, u3, fs, d, "ffn2_dwg", plan.comm("ffn2_dwg")))
    plan.grad("ffn2_w_up", _matmul_tn(db2, u3, fs, d, "ffn2_dwu", plan.comm("ffn2_dwu")))
    plan.grad("ffn2_w_down", _matmul_tn(hid2, dob3, fs, d, "ffn2_dwd", plan.comm("ffn2_dwd")))
    plan.at("after_ffn2_dwd")

    dmix = _matmul_nt(dh2b, wout, tm, d, BF, "out_proj_bwd", plan.comm("out_proj_bwd"))
    plan.at("after_out_proj_bwd")
    tmp = _largest_tile(p_w, 512, LANES)
    plan.grad("w_out", jnp.concatenate([_matmul_tn(pool_o, dh2b, tmp, d, "dwout_pool"),
                                        _matmul_tn(att_o, dh2b, tmp, d, "dwout_att")], axis=0))
    dzp, dpw, dps = _pool_bwd(z, dmix, pw, ps, "pool_bwd")
    plan.grad("pool_w", dpw)
    plan.at("before_att_bwd")
    dq, dk, dv, dcq, dck, dqw, dkw = _att_bwd(z, cum, cumt, qw, kw, dmix, n_heads, npb, npb, tq, "att_bwd",
                                              plan.comm("att_bwd"))
    dcum = dcq[:, :, 0].T - dck[:, 0, :].T
    dcum = jnp.pad(dcum, ((0, 0), (0, LANES - n_heads)))
    dzf, dbf = _fox_bwd(z, bfp, dcum, fblk, "fox_bwd")
    dz = jnp.concatenate([dzp, dq, dk, dv, dzf], axis=1)
    plan.grad("w_in", _matmul_tn(u2, dz, tmd, tnz, "dwin", plan.comm("dwin")))
    du2 = _matmul_nt(dz, win, tm, tnz, F32, "mix_in_bwd", plan.comm("mix_in_bwd"))
    plan.at("before_ffn1_bwd_dx")
    dh1, dob1, dnm = _rms_bwd(du2, h1, sw["mix_norm"], dh2, 0.5, te, "mix_rms_bwd")

    du1, da1, db1, hid1 = _ffn_bwd_dx(dob1, a1, b1, wg1, wu1, wd1, tm, "ffn1_bwd_dx", plan.comm("ffn1_bwd_dx"))
    dh0, _, dn1 = _rms_bwd(du1, h0, sw["ffn1_norm"], dh1, 1.0, te, "ffn1_rms_bwd")
    plan.grad("ffn1_w_gate", _matmul_tn(da1, u1, fs, d, "ffn1_dwg", plan.comm("ffn1_dwg")))
    plan.grad("ffn1_w_up", _matmul_tn(db1, u1, fs, d, "ffn1_dwu", plan.comm("ffn1_dwu")))
    plan.at("before_ffn1_dwd")
    plan.grad("ffn1_w_down", _matmul_tn(hid1, dob1, fs, d, "ffn1_dwd", plan.comm("ffn1_dwd")))
    plan.at("after_ffn1_dwd")

    small = [dn1, dnm, dn2, dps, dqw, dkw, dbf, lsq]
    return dh0[n_meta:l], dh0[:n_meta], small


_BIG = ("ffn1_w_gate", "ffn1_w_up", "ffn1_w_down", "w_in", "pool_w", "w_out", "ffn2_w_gate", "ffn2_w_up", "ffn2_w_down")
_SMALL = ("ffn1_norm", "mix_norm", "ffn2_norm", "pool_scale", "q_norm", "k_norm", "b_forget")
_ORDER = ("meta_tokens", "ffn1_norm", "ffn1_w_gate", "ffn1_w_up", "ffn1_w_down", "mix_norm", "w_in", "b_forget",
          "q_norm", "k_norm", "pool_w", "pool_scale", "w_out", "ffn2_norm", "ffn2_w_gate", "ffn2_w_up", "ffn2_w_down")


_FFN1 = ("ffn1_w_gate", "ffn1_w_up", "ffn1_w_down")
_FFN2 = ("ffn2_w_gate", "ffn2_w_up", "ffn2_w_down")
_MIX = ("w_in", "pool_w", "w_out")

_RIDES = {
    "out_proj": (("g2", _FFN2),),
    "ffn2_dwu": (("s1", ("ffn2_w_gate",)),),
    "ffn2_dwd": (("s1", ("ffn2_w_up",)),),
    "out_proj_bwd": (("s1", ("ffn2_w_down",)),),
    "mix_in_bwd": (("s1", _MIX),),
    "ffn1_dwu": (("s1", ("ffn1_w_gate",)),),
    "ffn1_dwd": (("s1", ("ffn1_w_up",)),),
    "adamw_ffn2_w_gate": (("s1", ("ffn1_w_down",)),),
}
_G1_FFN1 = _FFN1 + ("meta_tokens",)
_POINTS = {
    "start": (("start", "g1", _G1_FFN1), ("wait", "g1", _G1_FFN1), ("start", "g1", _MIX),
              ("start", "g1", ("ffn2_w_down",)), ("alone", "g2", _G1_FFN1)),
    "after_ffn1_fwd": (("wait", "g1", _MIX), ("start", "g1", ("ffn2_w_gate", "ffn2_w_up")), ("alone", "g2", _MIX)),
    "after_att_fwd": (("wait", "g1", ("ffn2_w_down",)), ("wait", "g1", ("ffn2_w_gate", "ffn2_w_up"))),
    "after_ffn2_dwd": (("sum", ("ffn2_w_gate",)), ("start", "s2", ("ffn2_w_gate",))),
    "after_out_proj_bwd": (("sum", ("ffn2_w_up",)), ("start", "s2", ("ffn2_w_up",))),
    "before_att_bwd": (("sum", ("ffn2_w_down",)), ("start", "s2", ("ffn2_w_down",))),
    "before_ffn1_bwd_dx": (("sum", _MIX), ("start", "s2", _MIX)),
    "before_ffn1_dwd": (("sum", ("ffn1_w_gate",)), ("start", "s2", ("ffn1_w_gate",))),
    "after_ffn1_dwd": (("sum", ("ffn1_w_up",)), ("start", "s2", ("ffn1_w_up",))),
    "before_adamw_ffn2_w_gate": (("wait", "s2", ("ffn2_w_gate",)),),
    "before_adamw_ffn2_w_up": (("sum", ("ffn1_w_down",)), ("start", "s2", ("ffn1_w_down",)),
                               ("wait", "s2", ("ffn2_w_up",))),
    "before_adamw_ffn2_w_down": (("wait", "s2", ("ffn2_w_down",)),),
    "before_adamw_w_in": (("wait", "s2", _MIX),),
    "before_adamw_ffn1_w_gate": (("wait", "s2", ("ffn1_w_gate",)),),
    "before_adamw_ffn1_w_up": (("wait", "s2", ("ffn1_w_up",)),),
    "before_adamw_ffn1_w_down": (("wait", "s2", ("ffn1_w_down",)),),
}


def _own_slot_filled(block, slot, n_slots):
    zone = lax.empty((n_slots,) + block.shape, block.dtype)
    return lax.dynamic_update_slice(zone, block[None], (slot,) + (0,) * block.ndim)


class _MeshPlan:
    def __init__(self, shards, pos, d, d_in, n_heads):
        self.shard, self.pos = dict(shards), pos
        self.core = pos[2].astype(jnp.int32).reshape(1)
        self.d, self.d_in, self.n_heads = d, d_in, n_heads
        self.partial, self.full, self.slots, self.from_sibling, self.chip_sum, self.received = {}, {}, {}, {}, {}, {}
        self.pending, self.started, self.tokens = [], {}, []

    def _phase(self, kind, names):
        src, dst, make = {"g1": (self.shard, self.partial, _gather_ici), "g2": (self.partial, self.full, _gather_fwd),
                          "s1": (self.slots, self.from_sibling, _scatter_sibling),
                          "s2": (self.chip_sum, self.received, _scatter_ici)}[kind]
        op = make([src[n] for n in names])
        self.pending.append((op, dst, names))
        return op

    def _settle(self):
        for op, dst, names in self.pending:
            dst.update(zip(names, op.results))
        self.pending = []

    def _start(self, kind, names):
        x, y, c = self.pos
        if kind == "g1":
            blocks = [self.shard[n] for n in names]
            op = _gather_ici(blocks, [_own_slot_filled(b, 4 * x + 2 * y + c, N_DEV) for b in blocks])
        else:
            sums = [self.chip_sum[n] for n in names]
            mine = [lax.dynamic_index_in_dim(s, 2 * x + y, 0, keepdims=False) for s in sums]
            op = _scatter_ici(sums, [_own_slot_filled(b, 2 * x + y, N_DEV // 2) for b in mine])
        self.started[(kind, names)], token = _split_start(op, "_".join(("start", kind, names[0])))
        self.tokens.append(token)

    def _wait(self, kind, names, after):
        landed = _split_wait(self.started.pop((kind, names)), after, "_".join(("wait", kind, names[0])))
        (self.partial if kind == "g1" else self.received).update(zip(names, landed))

    def comm(self, kernel_name):
        self._settle()
        ops = [self._phase(kind, names) for kind, names in _RIDES.get(kernel_name, ())]
        if self.tokens:
            ops.append(_Comm(self.tokens, [], [], lambda *a: None, lambda *a: None))
            self.tokens = []
        return _merge_comm(ops)

    def at(self, point, after=None):
        for step in _POINTS.get(point, ()):
            self._settle()
            if step[0] == "alone":
                _comm_alone(self._phase(step[1], step[2]), "_".join((step[1], point)))
            elif step[0] == "start":
                self._start(step[1], step[2])
            elif step[0] == "wait":
                self._wait(step[1], step[2], self.tokens[-1] if after is None else after)
            else:
                for n in step[1]:
                    self.chip_sum[n] = _chip_sum(self.slots[n], self.from_sibling[n], self.core, "chip_sum_" + n)

    def weights(self, group):
        self._settle()
        f, d = self.full, self.d
        if group == "meta":
            g = f["meta_tokens"]
            return g.transpose(1, 0, 2).reshape(g.shape[1], d)
        if group == "ffn1":
            return tuple(f[n] for n in _FFN1)
        if group == "ffn2":
            return tuple(f[n] for n in _FFN2)
        n_main = self.d_in - self.n_heads
        win = f["w_in"].transpose(1, 0, 2).reshape(d, self.d_in)
        win = jnp.concatenate([win[:, :n_main], jnp.pad(win[:, n_main:], ((0, 0), (0, LANES - self.n_heads)))], axis=1)
        pw = f["pool_w"]
        pw = pw.transpose(1, 0, 2, 3).reshape(pw.shape[1], pw.shape[3], pw.shape[3])
        return win, pw, f["w_out"].reshape(-1, d)

    def grad(self, name, g):
        d = self.d
        if name == "w_in":
            g = g[:, :self.d_in].reshape(d, N_DEV, -1).transpose(1, 0, 2)
        elif name == "pool_w":
            ng, gw = g.shape[0], g.shape[2]
            g = g.astype(BF).reshape(ng, N_DEV, -1, gw).transpose(1, 0, 2, 3).reshape(N_DEV, -1, gw)
        elif name == "w_out":
            g = g.reshape(N_DEV, -1, d)
        self.slots[name] = g.reshape((N_DEV // 2, 2) + g.shape[1:])

    def gradient_parts(self, name):
        self._settle()
        return self.received[name]


_TRANSPOSED = ("ffn1_w_gate", "ffn1_w_up", "ffn2_w_gate", "ffn2_w_up")


def _as2d(name, a):
    return a[0].T if name in _TRANSPOSED else a.reshape(-1, a.shape[-1])


def _from2d(name, a2d, shape):
    return a2d.T.reshape(shape) if name in _TRANSPOSED else a2d.reshape(shape)


def kernel(x, meta_tokens, ffn1_norm, ffn1_w_gate, ffn1_w_up, ffn1_w_down, mix_norm, w_in, b_forget, q_norm, k_norm, pool_w, pool_scale, w_out, ffn2_norm, ffn2_w_gate, ffn2_w_up, ffn2_w_down, loss_target, m_meta_tokens, m_ffn1_norm, m_ffn1_w_gate, m_ffn1_w_up, m_ffn1_w_down, m_mix_norm, m_w_in, m_b_forget, m_q_norm, m_k_norm, m_pool_w, m_pool_scale, m_w_out, m_ffn2_norm, m_ffn2_w_gate, m_ffn2_w_up, m_ffn2_w_down, v_meta_tokens, v_ffn1_norm, v_ffn1_w_gate, v_ffn1_w_up, v_ffn1_w_down, v_mix_norm, v_w_in, v_b_forget, v_q_norm, v_k_norm, v_pool_w, v_pool_scale, v_w_out, v_ffn2_norm, v_ffn2_w_gate, v_ffn2_w_up, v_ffn2_w_down):
    w = dict(meta_tokens=meta_tokens, ffn1_norm=ffn1_norm, ffn1_w_gate=ffn1_w_gate, ffn1_w_up=ffn1_w_up,
             ffn1_w_down=ffn1_w_down, mix_norm=mix_norm, w_in=w_in, b_forget=b_forget, q_norm=q_norm, k_norm=k_norm,
             pool_w=pool_w, pool_scale=pool_scale, w_out=w_out, ffn2_norm=ffn2_norm, ffn2_w_gate=ffn2_w_gate,
             ffn2_w_up=ffn2_w_up, ffn2_w_down=ffn2_w_down)
    m = dict(meta_tokens=m_meta_tokens, ffn1_norm=m_ffn1_norm, ffn1_w_gate=m_ffn1_w_gate, ffn1_w_up=m_ffn1_w_up,
             ffn1_w_down=m_ffn1_w_down, mix_norm=m_mix_norm, w_in=m_w_in, b_forget=m_b_forget, q_norm=m_q_norm,
             k_norm=m_k_norm, pool_w=m_pool_w, pool_scale=m_pool_scale, w_out=m_w_out, ffn2_norm=m_ffn2_norm,
             ffn2_w_gate=m_ffn2_w_gate, ffn2_w_up=m_ffn2_w_up, ffn2_w_down=m_ffn2_w_down)
    v = dict(meta_tokens=v_meta_tokens, ffn1_norm=v_ffn1_norm, ffn1_w_gate=v_ffn1_w_gate, ffn1_w_up=v_ffn1_w_up,
             ffn1_w_down=v_ffn1_w_down, mix_norm=v_mix_norm, w_in=v_w_in, b_forget=v_b_forget, q_norm=v_q_norm,
             k_norm=v_k_norm, pool_w=v_pool_w, pool_scale=v_pool_scale, w_out=v_w_out, ffn2_norm=v_ffn2_norm,
             ffn2_w_gate=v_ffn2_w_gate, ffn2_w_up=v_ffn2_w_up, ffn2_w_down=v_ffn2_w_down)

    d = x.shape[-1]
    n_heads = b_forget.shape[-1]
    pos = (lax.axis_index("x"), lax.axis_index("y"), lax.axis_index("c"))
    me = 4 * pos[0] + 2 * pos[1] + pos[2]

    shards = {k: (_as2d(k, w[k]) if k in _TRANSPOSED else w[k][0]).astype(BF) for k in _BIG}
    shards["meta_tokens"] = meta_tokens
    plan = _MeshPlan(shards, pos, d, N_DEV * w_in.shape[-1], n_heads)
    sw = {k: w[k] for k in _SMALL}
    sw["b_forget"] = jnp.pad(b_forget, ((0, 0), (0, LANES - n_heads)))
    dx, dmeta, small = _local_step(x[0], loss_target[0], sw, plan)

    res = {}
    last = dx

    def update_shards(names):
        nonlocal last
        for k in names:
            plan.at("before_adamw_" + k, last)
            res[k] = _adamw(plan.gradient_parts(k), _as2d(k, w[k]), _as2d(k, m[k]), _as2d(k, v[k]), "adamw_" + k,
                            plan.comm("adamw_" + k))
            last = res[k][0]

    update_shards(_FFN2 + _MIX)

    pack = jnp.concatenate(small, axis=1)
    pack_g, meta_g = _exchange([pack, dmeta], False, "gather_small")
    tot, dmeta_tot, loss_row = _small_reduce(pack_g, meta_g, 0.5 / d, "small_reduce")

    mcols = meta_tokens.shape[1]
    g_meta = lax.dynamic_slice_in_dim(dmeta_tot, me * mcols, mcols, axis=1)
    res["meta_tokens"] = _adamw(g_meta, meta_tokens, m_meta_tokens, v_meta_tokens, "adamw_meta_tokens")

    def packed(src):
        return jnp.concatenate([src[k] for k in _SMALL[:-1]] + [jnp.pad(src["b_forget"], ((0, 0), (0, LANES - n_heads)))],
                               axis=1)

    wp = packed(w)
    sm = _adamw(tot[:, :wp.shape[1]], wp, packed(m), packed(v), "adamw_small")
    off = 0
    for k in _SMALL:
        width = w[k].shape[1]
        res[k] = tuple(o[:, off:off + width] for o in sm)
        off += width if k != "b_forget" else LANES

    last = sm[0]
    update_shards(_FFN1)

    outs =[loss_row[0, 0], dx[None]]
    for idx in range(4):
        outs += [_from2d(k, res[k][idx], w[k].shape) for k in _ORDER]
    return tuple(outs)
```

```python
import functools

import jax
import jax.numpy as jnp
from jax import lax
from jax.experimental import pallas as pl
from jax.experimental.pallas import tpu as pltpu

F32 = jnp.float32
BF = jnp.bfloat16
SDS = jax.ShapeDtypeStruct

N_DEV = 8
LANES = 128
SUBLANES = 8
HEAD_DIM = 128
POOL_WINDOWS = (2, 4, 8, 16)
RMS_EPS = 1e-6
NEG_BIG = -1e30
MIB = 1024 * 1024

ADAM_LR = 0.001
ADAM_B1 = 0.9
ADAM_B2 = 0.999
ADAM_EPS = 1e-08
ADAM_WD = 0.01
ADAM_STEP = 10


class _Comm:
    def __init__(self, arrs, out_shape, sems, start, finish, aliases=None):
        self.arrs, self.out_shape, self.sems = list(arrs), list(out_shape), list(sems)
        self.start, self.finish, self.aliases = start, finish, dict(aliases or {})
        self.results = None


def _merge_comm(ops):
    ops = [op for op in ops if op is not None]
    if not ops:
        return None
    na, no, ns = [0], [0], [0]
    for op in ops:
        na.append(na[-1] + len(op.arrs))
        no.append(no[-1] + len(op.out_shape))
        ns.append(ns[-1] + len(op.sems))

    def parts(i, ins, outs, sems):
        return ins[na[i]:na[i + 1]], outs[no[i]:no[i + 1]], sems[ns[i]:ns[i + 1]]

    def start(ins, outs, sems):
        for i, op in enumerate(ops):
            op.start(*parts(i, ins, outs, sems))

    def finish(ins, outs, sems):
        for i, op in enumerate(ops):
            op.finish(*parts(i, ins, outs, sems))

    aliases = {}
    for i, op in enumerate(ops):
        for a, o in op.aliases.items():
            aliases[na[i] + a] = no[i] + o
    merged = _Comm([a for op in ops for a in op.arrs], [s for op in ops for s in op.out_shape],
                   [s for op in ops for s in op.sems], start, finish, aliases)
    merged.children = (ops, no)
    return merged


def _deliver(comm, results):
    comm.results = list(results)
    if hasattr(comm, "children"):
        ops, no = comm.children
        for i, op in enumerate(ops):
            _deliver(op, results[no[i]:no[i + 1]])


def _call(body, *, grid, in_specs, out_specs, out_shape, scratch_shapes=(), vmem_mib, name, comm=None):
    single = not isinstance(out_shape, (list, tuple))
    out_specs = [out_specs] if single else list(out_specs)
    out_shape = [out_shape] if single else list(out_shape)
    in_specs, scratch_shapes = list(in_specs), list(scratch_shapes)
    params = pltpu.CompilerParams(dimension_semantics=("arbitrary",) * len(grid), vmem_limit_bytes=vmem_mib * MIB)
    n_in, n_out, n_scr = len(in_specs), len(out_specs), len(scratch_shapes)

    def run(*args):
        if comm is None:
            res = pl.pallas_call(body, grid=grid, in_specs=in_specs, out_specs=out_specs, out_shape=out_shape,
                                 scratch_shapes=scratch_shapes, compiler_params=params, name=name)(*args)
            return res[0] if single else res
        ci, co = len(comm.arrs), len(comm.out_shape)

        def with_comm(*refs):
            ins, cins = refs[:n_in], refs[n_in:n_in + ci]
            o0 = n_in + ci
            outs, couts = refs[o0:o0 + n_out], refs[o0 + n_out:o0 + n_out + co]
            s0 = o0 + n_out + co
            scr, csems = refs[s0:s0 + n_scr], refs[s0 + n_scr:]
            ids = [pl.program_id(a) for a in range(len(grid))]
            first = functools.reduce(jnp.logical_and, [i == 0 for i in ids])
            last = functools.reduce(jnp.logical_and, [i == g - 1 for i, g in zip(ids, grid)])

            @pl.when(first)
            def _():
                comm.start(cins, couts, csems)

            body(*ins, *outs, *scr)

            @pl.when(last)
            def _():
                comm.finish(cins, couts, csems)

        anyspec = pl.BlockSpec(memory_space=pl.ANY)
        res = pl.pallas_call(
            with_comm, grid=grid, in_specs=in_specs + [anyspec] * ci, out_specs=out_specs + [anyspec] * co,
            out_shape=out_shape + comm.out_shape, scratch_shapes=scratch_shapes + comm.sems,
            input_output_aliases={n_in + a: n_out + o for a, o in comm.aliases.items()},
            compiler_params=params, name=name)(*args, *comm.arrs)
        _deliver(comm, res[n_out:])
        return res[0] if single else res[:n_out]

    return run


def _comm_alone(comm, name):
    def body(*refs):
        ci, co = len(comm.arrs), len(comm.out_shape)
        ins, outs, sems = refs[:ci], refs[ci:ci + co], refs[ci + co:]
        comm.start(ins, outs, sems)
        comm.finish(ins, outs, sems)

    anyspec = pl.BlockSpec(memory_space=pl.ANY)
    res = pl.pallas_call(
        body, in_specs=[anyspec] * len(comm.arrs), out_specs=[anyspec] * len(comm.out_shape),
        out_shape=comm.out_shape, scratch_shapes=comm.sems, input_output_aliases=comm.aliases, name=name)(*comm.arrs)
    _deliver(comm, res)


def _split_start(comm, name):
    na, ns = len(comm.arrs), len(comm.sems)

    def body(*refs):
        comm.start(refs[:na], None, refs[na:na + ns])
        token = refs[-1]
        token[...] = jnp.zeros_like(token)

    hbm = pl.BlockSpec(memory_space=pltpu.HBM)
    res = pl.pallas_call(
        body, name=name,
        out_shape=tuple(comm.sems) + tuple(pltpu.HBM(a.shape, a.dtype) for a in comm.arrs)
        + (SDS((SUBLANES, LANES), F32),),
        in_specs=[hbm] * na,
        out_specs=[pl.BlockSpec(memory_space=pltpu.SEMAPHORE)] * ns + [hbm] * na + [pl.BlockSpec(memory_space=pltpu.VMEM)],
        input_output_aliases={i: ns + i for i in range(na)},
        compiler_params=pltpu.CompilerParams(has_side_effects=pltpu.SideEffectType.DATAFLOW_SIDE_EFFECTING),
    )(*[pltpu.with_memory_space_constraint(a, pltpu.HBM) for a in comm.arrs])
    return (comm, res[:ns], res[ns:ns + na]), res[-1]


def _split_wait(started, afters, name):
    comm, sems, thru = started
    na, ns = len(thru), len(sems)
    afters = list(afters)

    def body(*refs):
        comm.finish(refs[:na], None, refs[na:na + ns])

    hbm = pl.BlockSpec(memory_space=pltpu.HBM)
    res = pl.pallas_call(
        body, name=name, out_shape=tuple(pltpu.HBM(a.shape, a.dtype) for a in thru),
        in_specs=[hbm] * na + [pl.BlockSpec(memory_space=pltpu.SEMAPHORE)] * ns
        + [pl.BlockSpec(memory_space=pl.ANY)] * len(afters),
        out_specs=[hbm] * na, input_output_aliases={i: i for i in range(na)},
        compiler_params=pltpu.CompilerParams(has_side_effects=pltpu.SideEffectType.DATAFLOW_SIDE_EFFECTING),
    )(*thru, *sems, *afters)
    return res[na - len(comm.out_shape):]


def _largest_tile(n, cap, mult):
    if n <= cap:
        return n
    best = None
    for t in range(mult, cap + 1, mult):
        if n % t == 0:
            best = t
    assert best is not None, (n, cap, mult)
    return best


def _dot(a, b):
    return jnp.dot(a, b, preferred_element_type=F32)


def _dot_nt(a, b):
    return lax.dot_general(a, b, (((1,), (1,)), ((), ())), preferred_element_type=F32)


def _dot_tn(a, b):
    return lax.dot_general(a, b, (((0,), (0,)), ((), ())), preferred_element_type=F32)


def _rows8(x):
    t, c = x.shape
    return jnp.sum(x.reshape(t // SUBLANES, SUBLANES, c), axis=0)


def _rstd(x):
    return lax.rsqrt(jnp.mean(x * x, axis=-1, keepdims=True) + RMS_EPS)


def _ffn_fwd(h, g, wg, wu, wd, tm, name, comm=None):
    lp, d = h.shape
    ns, fs, _ = wg.shape

    def body(h_ref, g_ref, wg_ref, wu_ref, wd_ref, out_ref, a_ref, b_ref, u_ref, acc_ref):
        j = pl.program_id(1)

        @pl.when(j == 0)
        def _():
            hh = h_ref[...]
            u_ref[...] = (hh * _rstd(hh) * g_ref[...]).astype(BF)
            acc_ref[...] = jnp.zeros_like(acc_ref)

        u = u_ref[...]
        a = _dot_nt(u, wg_ref[...])
        b = _dot_nt(u, wu_ref[...])
        a_ref[...] = a.astype(BF)
        b_ref[...] = b.astype(BF)
        hid = (a * jax.nn.sigmoid(a) * b).astype(BF)
        acc_ref[...] += _dot(hid, wd_ref[...])

        @pl.when(j == ns - 1)
        def _():
            out_ref[...] = h_ref[...] + 0.5 * acc_ref[...]

    row = pl.BlockSpec((tm, d), lambda i, j: (i, 0))
    act = pl.BlockSpec((None, tm, fs), lambda i, j: (j, i, 0))
    return _call(
        body, grid=(lp // tm, ns),
        in_specs=[row, pl.BlockSpec((1, d), lambda i, j: (0, 0)),
                  pl.BlockSpec((None, fs, d), lambda i, j: (j, 0, 0)),
                  pl.BlockSpec((None, fs, d), lambda i, j: (j, 0, 0)),
                  pl.BlockSpec((None, fs, d), lambda i, j: (j, 0, 0))],
        out_specs=[row, act, act, row],
        out_shape=[SDS((lp, d), F32), SDS((ns, lp, fs), BF), SDS((ns, lp, fs), BF), SDS((lp, d), BF)],
        scratch_shapes=[pltpu.VMEM((tm, d), F32)],
        vmem_mib=56, name=name, comm=comm)(h, g, wg, wu, wd)


def _ffn_bwd_dx(dob, a, b, wg, wu, wd, tm, name, comm=None):
    lp, d = dob.shape
    ns, fs, _ = wg.shape

    def body(do_ref, a_ref, b_ref, wg_ref, wu_ref, wd_ref, du_ref, da_ref, db_ref, hid_ref):
        j = pl.program_id(1)

        @pl.when(j == 0)
        def _():
            du_ref[...] = jnp.zeros_like(du_ref)

        dhid = _dot_nt(do_ref[...], wd_ref[...])
        av = a_ref[...].astype(F32)
        bv = b_ref[...].astype(F32)
        sig = jax.nn.sigmoid(av)
        sil = av * sig
        dbv = (dhid * sil).astype(BF)
        dav = (dhid * bv * (sig * (1.0 + av * (1.0 - sig)))).astype(BF)
        hid_ref[...] = (sil * bv).astype(BF)
        da_ref[...] = dav
        db_ref[...] = dbv
        du_ref[...] += _dot(dav, wg_ref[...]) + _dot(dbv, wu_ref[...])

    row = pl.BlockSpec((tm, d), lambda i, j: (i, 0))
    act = pl.BlockSpec((None, tm, fs), lambda i, j: (j, i, 0))
    return _call(
        body, grid=(lp // tm, ns),
        in_specs=[row, act, act,
                  pl.BlockSpec((None, fs, d), lambda i, j: (j, 0, 0)),
                  pl.BlockSpec((None, fs, d), lambda i, j: (j, 0, 0)),
                  pl.BlockSpec((None, fs, d), lambda i, j: (j, 0, 0))],
        out_specs=[row, act, act, act],
        out_shape=[SDS((lp, d), F32)] + [SDS((ns, lp, fs), BF)] * 3,
        vmem_mib=56, name=name, comm=comm)(dob, a, b, wg, wu, wd)


def _rms_bwd(du, h, g, dres, bscale, tm, name, comm=None):
    lp, d = h.shape

    def body(du_ref, h_ref, g_ref, dres_ref, dh_ref, dhb_ref, dg_ref):
        @pl.when(pl.program_id(0) == 0)
        def _():
            dg_ref[...] = jnp.zeros_like(dg_ref)

        hh = h_ref[...]
        r = _rstd(hh)
        xhat = hh * r
        duv = du_ref[...]
        dg_ref[...] += _rows8(duv * xhat)
        dxh = duv * g_ref[...]
        dh = dres_ref[...] + r * (dxh - xhat * jnp.mean(dxh * xhat, axis=-1, keepdims=True))
        dh_ref[...] = dh
        dhb_ref[...] = (bscale * dh).astype(BF)

    row = pl.BlockSpec((tm, d), lambda i: (i, 0))
    return _call(
        body, grid=(lp // tm,),
        in_specs=[row, row, pl.BlockSpec((1, d), lambda i: (0, 0)), row],
        out_specs=[row, row, pl.BlockSpec((SUBLANES, d), lambda i: (0, 0))],
        out_shape=[SDS((lp, d), F32), SDS((lp, d), BF), SDS((SUBLANES, d), F32)],
        vmem_mib=48, name=name, comm=comm)(du, h, g, dres)


def _matmul_tn(a, b, tm, tn, name, comm=None):
    a_b, b_b = a.ndim == 3, b.ndim == 3
    ns = a.shape[0] if a_b else (b.shape[0] if b_b else 1)
    l, m = a.shape[-2:]
    n = b.shape[-1]

    def body(a_ref, b_ref, o_ref):
        o_ref[...] = _dot_tn(a_ref[...], b_ref[...]).astype(o_ref.dtype)

    a_spec = (pl.BlockSpec((None, l, tm), lambda s, i, j: (s, 0, i)) if a_b
              else pl.BlockSpec((l, tm), lambda s, i, j: (0, i)))
    b_spec = (pl.BlockSpec((None, l, tn), lambda s, i, j: (s, 0, j)) if b_b
              else pl.BlockSpec((l, tn), lambda s, i, j: (0, j)))
    batched = a_b or b_b
    o_spec = (pl.BlockSpec((None, tm, tn), lambda s, i, j: (s, i, j)) if batched
              else pl.BlockSpec((tm, tn), lambda s, i, j: (i, j)))
    o_shape = SDS((ns, m, n), BF) if batched else SDS((m, n), BF)
    return _call(
        body, grid=(ns, m // tm, n // tn), in_specs=[a_spec, b_spec], out_specs=o_spec, out_shape=o_shape,
        vmem_mib=48, name=name, comm=comm)(a, b)


def _matmul_nt(x, w, tm, tk, out_dtype, name, comm=None):
    l, k = x.shape
    n = w.shape[0]
    nk = k // tk

    def body(x_ref, w_ref, o_ref, acc_ref):
        kk = pl.program_id(1)

        @pl.when(kk == 0)
        def _():
            acc_ref[...] = jnp.zeros_like(acc_ref)

        acc_ref[...] += _dot_nt(x_ref[...], w_ref[...])

        @pl.when(kk == nk - 1)
        def _():
            o_ref[...] = acc_ref[...].astype(o_ref.dtype)

    return _call(
        body, grid=(l // tm, nk),
        in_specs=[pl.BlockSpec((tm, tk), lambda i, kk: (i, kk)), pl.BlockSpec((n, tk), lambda i, kk: (0, kk))],
        out_specs=pl.BlockSpec((tm, n), lambda i, kk: (i, 0)),
        out_shape=SDS((l, n), out_dtype),
        scratch_shapes=[pltpu.VMEM((tm, n), F32)],
        vmem_mib=48, name=name, comm=comm)(x, w)


def _norm_matmul(h, g, w, tm, tn, name, comm=None):
    lp, d = h.shape
    n = w.shape[1]

    def body(h_ref, g_ref, w_ref, z_ref, u_ref):
        @pl.when(pl.program_id(1) == 0)
        def _():
            hh = h_ref[...]
            u_ref[...] = (hh * _rstd(hh) * g_ref[...]).astype(BF)

        z_ref[...] = _dot(u_ref[...], w_ref[...])

    row = pl.BlockSpec((tm, d), lambda i, j: (i, 0))
    return _call(
        body, grid=(lp // tm, n // tn),
        in_specs=[row, pl.BlockSpec((1, d), lambda i, j: (0, 0)), pl.BlockSpec((d, tn), lambda i, j: (0, j))],
        out_specs=[pl.BlockSpec((tm, tn), lambda i, j: (i, j)), row],
        out_shape=[SDS((lp, n), F32), SDS((lp, d), BF)],
        vmem_mib=48, name=name, comm=comm)(h, g, w)


def _out_proj(h, pool_o, att_o, w_out, tm, name, comm=None):
    lp, d = h.shape
    p = pool_o.shape[1]
    dm = w_out.shape[0]

    def body(h_ref, p_ref, a_ref, w_ref, o_ref):
        o_ref[...] = h_ref[...] + _dot(p_ref[...], w_ref[0:p, :]) + _dot(a_ref[...], w_ref[p:dm, :])

    row = pl.BlockSpec((tm, d), lambda i: (i, 0))
    return _call(
        body, grid=(lp // tm,),
        in_specs=[row, pl.BlockSpec((tm, p), lambda i: (i, 0)), pl.BlockSpec((tm, dm - p), lambda i: (i, 0)),
                  pl.BlockSpec((dm, d), lambda i: (0, 0))],
        out_specs=row, out_shape=SDS((lp, d), F32),
        vmem_mib=48, name=name, comm=comm)(h, pool_o, att_o, w_out)


def _loss_head(y, tpad, row0, row1, tm, name, comm=None):
    lp, d = y.shape

    def body(y_ref, t_ref, dy_ref, dob_ref, ls_ref):
        i = pl.program_id(0)

        @pl.when(i == 0)
        def _():
            ls_ref[...] = jnp.zeros_like(ls_ref)

        rows = i * tm + lax.broadcasted_iota(jnp.int32, (tm, d), 0)
        err = jnp.where((rows >= row0) & (rows < row1), y_ref[...] - t_ref[...], 0.0)
        dy = err * (1.0 / d)
        dy_ref[...] = dy
        dob_ref[...] = (0.5 * dy).astype(BF)
        sq = _rows8(err * err)
        acc = sq[:, 0:LANES]
        for c in range(1, d // LANES):
            acc = acc + sq[:, c * LANES:(c + 1) * LANES]
        ls_ref[...] += acc

    row = pl.BlockSpec((tm, d), lambda i: (i, 0))
    return _call(
        body, grid=(lp // tm,), in_specs=[row, row],
        out_specs=[row, row, pl.BlockSpec((SUBLANES, LANES), lambda i: (0, 0))],
        out_shape=[SDS((lp, d), F32), SDS((lp, d), BF), SDS((SUBLANES, LANES), F32)],
        vmem_mib=48, name=name, comm=comm)(y, tpad)


def _window_select(levels, gidx):
    out = levels[-1]
    for k in range(len(levels) - 2, -1, -1):
        out = jnp.where(gidx == k, levels[k], out)
    return out


def _pool_window_mean_minus_id(x, gidx):
    rows = lax.broadcasted_iota(jnp.int32, x.shape, 0)
    levels = []
    s = x
    shift = 1
    while shift < POOL_WINDOWS[-1]:
        s = s + jnp.where(rows >= shift, pltpu.roll(s, shift, 0), 0.0)
        shift *= 2
        if shift in POOL_WINDOWS:
            levels.append(s)
    win = _window_select(levels, gidx)
    cnt = jnp.minimum(rows + 1, _window_select(list(POOL_WINDOWS), gidx)).astype(F32)
    return win / cnt - x, cnt


def _pool_window_transpose(dy, cnt, gidx):
    lp = dy.shape[0]
    rows = lax.broadcasted_iota(jnp.int32, dy.shape, 0)
    levels = []
    s = dy / cnt
    shift = 1
    while shift < POOL_WINDOWS[-1]:
        s = s + jnp.where(rows < lp - shift, pltpu.roll(s, lp - shift, 0), 0.0)
        shift *= 2
        if shift in POOL_WINDOWS:
            levels.append(s)
    return _window_select(levels, gidx) - dy


def _pool_fwd(z, pool_w, pool_scale, name, comm=None):
    lp = z.shape[0]
    ng, gw, _ = pool_w.shape

    def body(p_ref, w_ref, s_ref, o_ref):
        pooled, _ = _pool_window_mean_minus_id(p_ref[...], pl.program_id(0))
        o_ref[...] = (_dot(pooled.astype(BF), w_ref[...]) * s_ref[...]).astype(BF)

    return _call(
        body, grid=(ng,),
        in_specs=[pl.BlockSpec((lp, gw), lambda g: (0, g)), pl.BlockSpec((None, gw, gw), lambda g: (g, 0, 0)),
                  pl.BlockSpec((1, gw), lambda g: (0, g))],
        out_specs=pl.BlockSpec((lp, gw), lambda g: (0, g)), out_shape=SDS((lp, ng * gw), BF),
        vmem_mib=48, name=name, comm=comm)(z, pool_w, pool_scale)


def _pool_bwd(z, dmix, pool_w, pool_scale, name, comm=None):
    lp = z.shape[0]
    ng, gw, _ = pool_w.shape

    def body(p_ref, d_ref, w_ref, s_ref, dz_ref, dw_ref, ds_ref):
        g = pl.program_id(0)
        pooled, cnt = _pool_window_mean_minus_id(p_ref[...], g)
        pooled_b = pooled.astype(BF)
        w = w_ref[...]
        mixed = _dot(pooled_b, w)
        dpo = d_ref[...].astype(F32)
        ds_ref[...] = _rows8(dpo * mixed)
        dmixed = (dpo * s_ref[...]).astype(BF)
        dw_ref[...] = _dot_tn(pooled_b, dmixed)
        dpooled = _dot_nt(dmixed, w)
        dz_ref[...] = _pool_window_transpose(dpooled, cnt, g).astype(BF)

    return _call(
        body, grid=(ng,),
        in_specs=[pl.BlockSpec((lp, gw), lambda g: (0, g)), pl.BlockSpec((lp, gw), lambda g: (0, g)),
                  pl.BlockSpec((None, gw, gw), lambda g: (g, 0, 0)), pl.BlockSpec((1, gw), lambda g: (0, g))],
        out_specs=[pl.BlockSpec((lp, gw), lambda g: (0, g)), pl.BlockSpec((None, gw, gw), lambda g: (g, 0, 0)),
                   pl.BlockSpec((SUBLANES, gw), lambda g: (0, g))],
        out_shape=[SDS((lp, ng * gw), BF), SDS((ng, gw, gw), F32), SDS((SUBLANES, ng * gw), F32)],
        vmem_mib=48, name=name, comm=comm)(z, dmix, pool_w, pool_scale)


def _log_sigmoid(x):
    return jnp.minimum(x, 0.0) - jnp.log(1.0 + jnp.exp(-jnp.abs(x)))


def _fox_prep(z, bfp, fblk, name, comm=None):
    lp = z.shape[0]
    nb = lp // LANES

    def body(f_ref, b_ref, cum_ref):
        r = lax.broadcasted_iota(jnp.int32, (LANES, LANES), 0)
        c = lax.broadcasted_iota(jnp.int32, (LANES, LANES), 1)
        tri = (r >= c).astype(F32)
        carry = jnp.zeros((1, LANES), F32)
        for blk in range(nb):
            sl = slice(blk * LANES, (blk + 1) * LANES)
            lf = _log_sigmoid(f_ref[sl, :] + b_ref[...])
            cb = jnp.dot(tri, lf, preferred_element_type=F32, precision=lax.Precision.HIGHEST) + carry
            cum_ref[sl, :] = cb
            carry = cb[LANES - 1:LANES, :]

    return _call(
        body, grid=(1,),
        in_specs=[pl.BlockSpec((lp, LANES), lambda i: (0, fblk)), pl.BlockSpec((1, LANES), lambda i: (0, 0))],
        out_specs=pl.BlockSpec((lp, LANES), lambda i: (0, 0)), out_shape=SDS((lp, LANES), F32),
        vmem_mib=32, name=name, comm=comm)(z, bfp)


def _fox_bwd(z, bfp, dcum, fblk, name, comm=None):
    lp = z.shape[0]
    nb = lp // LANES

    def body(f_ref, b_ref, dc_ref, dz_ref, db_ref):
        r = lax.broadcasted_iota(jnp.int32, (LANES, LANES), 0)
        c = lax.broadcasted_iota(jnp.int32, (LANES, LANES), 1)
        tri = (r <= c).astype(F32)
        carry = jnp.zeros((1, LANES), F32)
        acc = jnp.zeros((SUBLANES, LANES), F32)
        for blk in range(nb - 1, -1, -1):
            sl = slice(blk * LANES, (blk + 1) * LANES)
            dlf = jnp.dot(tri, dc_ref[sl, :], preferred_element_type=F32, precision=lax.Precision.HIGHEST) + carry
            carry = dlf[0:1, :]
            df = dlf * jax.nn.sigmoid(-(f_ref[sl, :] + b_ref[...]))
            dz_ref[sl, :] = df.astype(BF)
            acc = acc + _rows8(df)
        db_ref[...] = acc

    return _call(
        body, grid=(1,),
        in_specs=[pl.BlockSpec((lp, LANES), lambda i: (0, fblk)), pl.BlockSpec((1, LANES), lambda i: (0, 0)),
                  pl.BlockSpec((lp, LANES), lambda i: (0, 0))],
        out_specs=[pl.BlockSpec((lp, LANES), lambda i: (0, 0)), pl.BlockSpec((SUBLANES, LANES), lambda i: (0, 0))],
        out_shape=[SDS((lp, LANES), BF), SDS((SUBLANES, LANES), F32)],
        vmem_mib=32, name=name, comm=comm)(z, bfp, dcum)


def _att_scores(q_ref, cum_ref, cumt_ref, qw_ref, kn_s, h, i, tq, lk):
    scale = 1.0 / (HEAD_DIM ** 0.5)
    q = q_ref[...]
    rq = _rstd(q)
    qhat = q * rq
    qn = (qhat * qw_ref[...]).astype(BF)
    s = _dot_nt(qn, kn_s[0:lk, :]) * scale
    lane = lax.broadcasted_iota(jnp.int32, (tq, LANES), 1)
    cq = jnp.sum(jnp.where(lane == h, cum_ref[...], 0.0), axis=1, keepdims=True)
    ck = cumt_ref[pl.ds(h, 1), 0:lk]
    s = s + (cq - ck)
    qpos = i * tq + lax.broadcasted_iota(jnp.int32, (tq, lk), 0)
    kpos = lax.broadcasted_iota(jnp.int32, (tq, lk), 1)
    s = jnp.where(qpos >= kpos, s, NEG_BIG)
    e = jnp.exp(s - jnp.max(s, axis=1, keepdims=True))
    p = e / jnp.sum(e, axis=1, keepdims=True)
    return p, qn, qhat, rq


def _per_query_tile(i, nq, tq, lp, fn):
    for t in range(nq):
        lk = min(lp, -(-((t + 1) * tq) // LANES) * LANES)
        pl.when(i == t)(functools.partial(fn, lk))


def _att_fwd(z, cum, cumt, qw, kw, n_heads, qblk0, tq, name, comm=None):
    lp = z.shape[0]
    nh = n_heads

    def body(q_ref, k_ref, v_ref, cum_ref, cumt_ref, qw_ref, kw_ref, o_ref, kn_s, vb_s):
        h, i = pl.program_id(0), pl.program_id(1)

        @pl.when(i == 0)
        def _():
            k = k_ref[...]
            kn_s[...] = (k * _rstd(k) * kw_ref[...]).astype(BF)
            vb_s[...] = v_ref[...].astype(BF)

        def tile(lk):
            p, _, _, _ = _att_scores(q_ref, cum_ref, cumt_ref, qw_ref, kn_s, h, i, tq, lk)
            o_ref[...] = _dot(p.astype(BF), vb_s[0:lk, :]).astype(BF)

        _per_query_tile(i, lp // tq, tq, lp, tile)

    vec = pl.BlockSpec((1, HEAD_DIM), lambda h, i: (0, 0))
    return _call(
        body, grid=(nh, lp // tq),
        in_specs=[pl.BlockSpec((tq, HEAD_DIM), lambda h, i: (i, qblk0 + h)),
                  pl.BlockSpec((lp, HEAD_DIM), lambda h, i: (0, qblk0 + nh + h)),
                  pl.BlockSpec((lp, HEAD_DIM), lambda h, i: (0, qblk0 + 2 * nh + h)),
                  pl.BlockSpec((tq, LANES), lambda h, i: (i, 0)),
                  pl.BlockSpec((nh, lp), lambda h, i: (0, 0)), vec, vec],
        out_specs=pl.BlockSpec((tq, HEAD_DIM), lambda h, i: (i, h)),
        out_shape=SDS((lp, nh * HEAD_DIM), BF),
        scratch_shapes=[pltpu.VMEM((lp, HEAD_DIM), BF), pltpu.VMEM((lp, HEAD_DIM), BF)],
        vmem_mib=48, name=name, comm=comm)(z, z, z, cum, cumt, qw, kw)


def _att_bwd(z, cum, cumt, qw, kw, dmix, n_heads, qblk0, oblk0, tq, name, comm=None):
    lp = z.shape[0]
    nh = n_heads
    nq = lp // tq
    scale = 1.0 / (HEAD_DIM ** 0.5)

    def body(q_ref, k_ref, v_ref, cum_ref, cumt_ref, qw_ref, kw_ref, do_ref,
             dq_ref, dk_ref, dv_ref, dcq_ref, dck_ref, dqw_ref, dkw_ref,
             kn_s, vb_s, dkn_s, dv_s, dck_s):
        h, i = pl.program_id(0), pl.program_id(1)

        @pl.when((h == 0) & (i == 0))
        def _():
            dqw_ref[...] = jnp.zeros_like(dqw_ref)
            dkw_ref[...] = jnp.zeros_like(dkw_ref)

        @pl.when(i == 0)
        def _():
            k = k_ref[...]
            kn_s[...] = (k * _rstd(k) * kw_ref[...]).astype(BF)
            vb_s[...] = v_ref[...].astype(BF)
            dkn_s[...] = jnp.zeros_like(dkn_s)
            dv_s[...] = jnp.zeros_like(dv_s)
            dck_s[...] = jnp.zeros_like(dck_s)

        def tile(lk):
            p, qn, qhat, rq = _att_scores(q_ref, cum_ref, cumt_ref, qw_ref, kn_s, h, i, tq, lk)
            dob = do_ref[...]
            dp = _dot_nt(dob, vb_s[0:lk, :])
            ds = p * (dp - jnp.sum(p * dp, axis=1, keepdims=True))
            dsb = ds.astype(BF)
            dv_s[0:lk, :] += _dot_tn(p.astype(BF), dob)
            dkn_s[0:lk, :] += _dot_tn(dsb, qn)
            dcq_ref[...] = jnp.sum(ds, axis=1, keepdims=True)
            dck_s[:, 0:lk] += jnp.sum(ds, axis=0, keepdims=True)
            dqn = _dot(dsb, kn_s[0:lk, :]) * scale
            gq = dqn * qw_ref[...]
            dq_ref[...] = (rq * (gq - qhat * jnp.mean(gq * qhat, axis=-1, keepdims=True))).astype(BF)
            dqw_ref[...] += _rows8(dqn * qhat)

        _per_query_tile(i, nq, tq, lp, tile)

        @pl.when(i == nq - 1)
        def _():
            k = k_ref[...]
            rk = _rstd(k)
            khat = k * rk
            dkn = dkn_s[...] * scale
            gk = dkn * kw_ref[...]
            dk_ref[...] = (rk * (gk - khat * jnp.mean(gk * khat, axis=-1, keepdims=True))).astype(BF)
            dkw_ref[...] += _rows8(dkn * khat)
            dv_ref[...] = dv_s[...].astype(BF)
            dck_ref[...] = dck_s[...]

    vec = pl.BlockSpec((1, HEAD_DIM), lambda h, i: (0, 0))
    part = pl.BlockSpec((SUBLANES, LANES), lambda h, i: (0, 0))
    return _call(
        body, grid=(nh, nq),
        in_specs=[pl.BlockSpec((tq, HEAD_DIM), lambda h, i: (i, qblk0 + h)),
                  pl.BlockSpec((lp, HEAD_DIM), lambda h, i: (0, qblk0 + nh + h)),
                  pl.BlockSpec((lp, HEAD_DIM), lambda h, i: (0, qblk0 + 2 * nh + h)),
                  pl.BlockSpec((tq, LANES), lambda h, i: (i, 0)),
                  pl.BlockSpec((nh, lp), lambda h, i: (0, 0)), vec, vec,
                  pl.BlockSpec((tq, HEAD_DIM), lambda h, i: (i, oblk0 + h))],
        out_specs=[pl.BlockSpec((tq, HEAD_DIM), lambda h, i: (i, h)),
                   pl.BlockSpec((lp, HEAD_DIM), lambda h, i: (0, h)),
                   pl.BlockSpec((lp, HEAD_DIM), lambda h, i: (0, h)),
                   pl.BlockSpec((None, tq, 1), lambda h, i: (h, i, 0)),
                   pl.BlockSpec((None, 1, lp), lambda h, i: (h, 0, 0)),
                   part, part],
        out_shape=[SDS((lp, nh * HEAD_DIM), BF)] * 3
        + [SDS((nh, lp, 1), F32), SDS((nh, 1, lp), F32), SDS((SUBLANES, LANES), F32), SDS((SUBLANES, LANES), F32)],
        scratch_shapes=[pltpu.VMEM((lp, HEAD_DIM), BF), pltpu.VMEM((lp, HEAD_DIM), BF),
                        pltpu.VMEM((lp, HEAD_DIM), F32), pltpu.VMEM((lp, HEAD_DIM), F32),
                        pltpu.VMEM((1, lp), F32)],
        vmem_mib=56, name=name, comm=comm)(z, z, z, cum, cumt, qw, kw, dmix)


def _adamw_math(w, g, m, v):
    m2 = ADAM_B1 * m + (1.0 - ADAM_B1) * g
    v2 = ADAM_B2 * v + (1.0 - ADAM_B2) * (g * g)
    m_hat = m2 / (1.0 - ADAM_B1 ** ADAM_STEP)
    v_hat = v2 / (1.0 - ADAM_B2 ** ADAM_STEP)
    delta = -ADAM_LR * (m_hat / (jnp.sqrt(v_hat) + ADAM_EPS) + ADAM_WD * w)
    return delta, m2, v2


def _adamw(g_in, w, m, v, name, comm=None):
    r, c = w.shape
    partial_sum = g_in.ndim == 3
    lane_padded = -(-c // LANES) * LANES
    tr = _largest_tile(r, max(16, MIB // (4 * lane_padded) // 16 * 16), 16)

    def body(g_ref, w_ref, m_ref, v_ref, go_ref, d_ref, mo_ref, vo_ref):
        if partial_sum:
            g = g_ref[0].astype(F32)
            for k in range(1, g_in.shape[0]):
                g = g + g_ref[k].astype(F32)
        else:
            g = g_ref[...]
        delta, m2, v2 = _adamw_math(w_ref[...], g, m_ref[...], v_ref[...])
        go_ref[...] = g
        d_ref[...] = delta
        mo_ref[...] = m2
        vo_ref[...] = v2

    blk = pl.BlockSpec((tr, c), lambda i: (i, 0))
    g_spec = pl.BlockSpec((g_in.shape[0], tr, c), lambda i: (0, i, 0)) if partial_sum else blk
    return _call(
        body, grid=(r // tr,), in_specs=[g_spec, blk, blk, blk], out_specs=[blk] * 4,
        out_shape=[SDS((r, c), F32)] * 4, vmem_mib=40, name=name, comm=comm)(g_in, w, m, v)


def _peer(x, y, c, k):
    return (1 - x if k & 4 else x, 1 - y if k & 2 else y, 1 - c if k & 1 else c)


def _exchange(arrs, scatter, name, comm=None):
    n = len(arrs)

    def body(*refs):
        ins, outs = refs[:n], refs[n:2 * n]
        send_sems, recv_sems, local_sems = refs[2 * n:]
        x, y, c = lax.axis_index("x"), lax.axis_index("y"), lax.axis_index("c")
        me = 4 * x + 2 * y + c

        def src(t, dev):
            return ins[t].at[dev] if scatter else ins[t]

        def copy(t, k, arrival):
            px, py, pc = _peer(x, y, c, k)
            dev = 4 * px + 2 * py + pc
            return pltpu.make_async_remote_copy(
                src_ref=src(t, dev), dst_ref=outs[t].at[dev if arrival else me],
                send_sem=send_sems.at[t, k - 1], recv_sem=recv_sems.at[t, k - 1],
                device_id=(px, py, pc), device_id_type=pl.DeviceIdType.MESH)

        local = [pltpu.make_async_copy(src(t, me), outs[t].at[me], local_sems.at[t]) for t in range(n)]
        for cp in local:
            cp.start()
        pairs = [(t, k) for k in range(1, N_DEV) for t in range(n)]
        for t, k in pairs:
            copy(t, k, False).start()
        for cp in local:
            cp.wait()
        for t, k in pairs:
            copy(t, k, True).wait_recv()
        for t, k in pairs:
            copy(t, k, False).wait_send()

    out_shape = [SDS(a.shape if scatter else (N_DEV,) + a.shape, a.dtype) for a in arrs]
    anyspec = pl.BlockSpec(memory_space=pl.ANY)
    return pl.pallas_call(
        body, in_specs=[anyspec] * n, out_specs=[anyspec] * n, out_shape=out_shape,
        scratch_shapes=[pltpu.SemaphoreType.DMA((n, N_DEV - 1)), pltpu.SemaphoreType.DMA((n, N_DEV - 1)),
                        pltpu.SemaphoreType.DMA((n,))],
        name=name)(*arrs)


_SIBLING = 1
_ICI_RELS = (2, 4, 6)


def _mesh_pos():
    return lax.axis_index("x"), lax.axis_index("y"), lax.axis_index("c")


def _sem_pair(sems, t, j, n_rel, scalars):
    if scalars:
        i = 2 * (t * n_rel + j)
        return sems[i], sems[i + 1]
    return sems[0].at[t, j], sems[1].at[t, j]


def _dev(pos):
    return 4 * pos[0] + 2 * pos[1] + pos[2]


def _gather_ici(shards, landing=None):
    n = len(shards)
    rels = (_SIBLING,) + _ICI_RELS

    def remote(ins, outs, sems, arrival):
        x, y, c = _mesh_pos()
        dst = ins[n:] if landing is not None else outs
        cps = []
        for j, k in enumerate(rels):
            peer = _peer(x, y, c, k)
            slot = _dev(peer) if arrival else _dev((x, y, c))
            for t in range(n):
                send_sem, recv_sem = _sem_pair(sems, t, j, len(rels), landing is not None)
                cps.append(pltpu.make_async_remote_copy(
                    src_ref=ins[t], dst_ref=dst[t].at[slot], send_sem=send_sem, recv_sem=recv_sem,
                    device_id=peer, device_id_type=pl.DeviceIdType.MESH))
        return cps

    if landing is not None:
        def start_remote(ins, outs, sems):
            for cp in remote(ins, outs, sems, False):
                cp.start()

        def finish_remote(ins, outs, sems):
            for cp in remote(ins, outs, sems, True):
                cp.wait_recv()
            for cp in remote(ins, outs, sems, False):
                cp.wait_send()

        return _Comm(list(shards) + list(landing), [SDS(a.shape, a.dtype) for a in landing],
                     [pltpu.SemaphoreType.DMA(())] * (2 * n * len(rels)),
                     start_remote, finish_remote, aliases={n + t: t for t in range(n)})

    def local(ins, outs, sems):
        me = _dev(_mesh_pos())
        return [pltpu.make_async_copy(ins[t], outs[t].at[me], sems[2].at[t]) for t in range(n)]

    def start(ins, outs, sems):
        for cp in local(ins, outs, sems) + remote(ins, outs, sems, False):
            cp.start()

    def finish(ins, outs, sems):
        for cp in local(ins, outs, sems):
            cp.wait()
        for cp in remote(ins, outs, sems, True):
            cp.wait_recv()
        for cp in remote(ins, outs, sems, False):
            cp.wait_send()

    return _Comm(shards, [SDS((N_DEV,) + s.shape, s.dtype) for s in shards],
                 [pltpu.SemaphoreType.DMA((n, len(rels))), pltpu.SemaphoreType.DMA((n, len(rels))),
                  pltpu.SemaphoreType.DMA((n,))], start, finish)


def _gather_fwd(partial):
    n = len(partial)

    def copies(ins, outs, sems, arrival):
        x, y, c = _mesh_pos()
        sibling = _peer(x, y, c, _SIBLING)
        cps = []
        for j, k in enumerate(_ICI_RELS):
            slot = _dev(_peer(x, y, c, k | _SIBLING if arrival else k))
            for t in range(n):
                cps.append(pltpu.make_async_remote_copy(
                    src_ref=ins[t].at[slot], dst_ref=outs[t].at[slot], send_sem=sems[0].at[t, j],
                    recv_sem=sems[1].at[t, j], device_id=sibling, device_id_type=pl.DeviceIdType.MESH))
        return cps

    def start(ins, outs, sems):
        for cp in copies(ins, outs, sems, False):
            cp.start()

    def finish(ins, outs, sems):
        for cp in copies(ins, outs, sems, True):
            cp.wait_recv()
        for cp in copies(ins, outs, sems, False):
            cp.wait_send()

    return _Comm(partial, [SDS(a.shape, a.dtype) for a in partial],
                 [pltpu.SemaphoreType.DMA((n, len(_ICI_RELS)))] * 2, start, finish,
                 aliases={t: t for t in range(n)})


def _scatter_sibling(slots):
    n = len(slots)

    def copies(ins, outs, sems):
        x, y, c = _mesh_pos()
        return [pltpu.make_async_remote_copy(
            src_ref=ins[t].at[:, 1 - c], dst_ref=outs[t], send_sem=sems[0].at[t], recv_sem=sems[1].at[t],
            device_id=_peer(x, y, c, _SIBLING), device_id_type=pl.DeviceIdType.MESH) for t in range(n)]

    def start(ins, outs, sems):
        for cp in copies(ins, outs, sems):
            cp.start()

    def finish(ins, outs, sems):
        for cp in copies(ins, outs, sems):
            cp.wait()

    return _Comm(slots, [SDS((s.shape[0],) + s.shape[2:], s.dtype) for s in slots],
                 [pltpu.SemaphoreType.DMA((n,))] * 2, start, finish)


def _scatter_ici(chip_sums, landing=None):
    n = len(chip_sums)

    def remote(ins, outs, sems, arrival):
        x, y, c = _mesh_pos()
        dst = ins[n:] if landing is not None else outs
        cps = []
        for j, k in enumerate(_ICI_RELS):
            peer = _peer(x, y, c, k)
            theirs, mine = 2 * peer[0] + peer[1], 2 * x + y
            for t in range(n):
                send_sem, recv_sem = _sem_pair(sems, t, j, len(_ICI_RELS), landing is not None)
                cps.append(pltpu.make_async_remote_copy(
                    src_ref=ins[t].at[theirs], dst_ref=dst[t].at[theirs if arrival else mine],
                    send_sem=send_sem, recv_sem=recv_sem,
                    device_id=peer, device_id_type=pl.DeviceIdType.MESH))
        return cps

    if landing is not None:
        def start_remote(ins, outs, sems):
            for cp in remote(ins, outs, sems, False):
                cp.start()

        def finish_remote(ins, outs, sems):
            for cp in remote(ins, outs, sems, True):
                cp.wait_recv()
            for cp in remote(ins, outs, sems, False):
                cp.wait_send()

        return _Comm(list(chip_sums) + list(landing), [SDS(a.shape, a.dtype) for a in landing],
                     [pltpu.SemaphoreType.DMA(())] * (2 * n * len(_ICI_RELS)), start_remote, finish_remote,
                     aliases={n + t: t for t in range(n)})

    def local(ins, outs, sems):
        x, y, _ = _mesh_pos()
        return [pltpu.make_async_copy(ins[t].at[2 * x + y], outs[t].at[2 * x + y], sems[2].at[t]) for t in range(n)]

    def start(ins, outs, sems):
        for cp in local(ins, outs, sems) + remote(ins, outs, sems, False):
            cp.start()

    def finish(ins, outs, sems):
        for cp in local(ins, outs, sems):
            cp.wait()
        for cp in remote(ins, outs, sems, True):
            cp.wait_recv()
        for cp in remote(ins, outs, sems, False):
            cp.wait_send()

    return _Comm(chip_sums, [SDS(a.shape, a.dtype) for a in chip_sums],
                 [pltpu.SemaphoreType.DMA((n, len(_ICI_RELS))), pltpu.SemaphoreType.DMA((n, len(_ICI_RELS))),
                  pltpu.SemaphoreType.DMA((n,))], start, finish)


def _chip_sum(slots, from_sibling, core, name):
    nq, _, r, c = slots.shape
    tr = _largest_tile(r, 1024, 16)

    def body(core_ref, a_ref, b_ref, o_ref):
        o_ref[...] = (a_ref[...].astype(F32) + b_ref[...].astype(F32)).astype(BF)

    return pl.pallas_call(
        body,
        grid_spec=pltpu.PrefetchScalarGridSpec(
            num_scalar_prefetch=1, grid=(nq, r // tr),
            in_specs=[pl.BlockSpec((None, None, tr, c), lambda q, i, core_ref: (q, core_ref[0], i, 0)),
                      pl.BlockSpec((None, tr, c), lambda q, i, core_ref: (q, i, 0))],
            out_specs=pl.BlockSpec((None, tr, c), lambda q, i, core_ref: (q, i, 0))),
        out_shape=SDS((nq, r, c), BF), compiler_params=pltpu.CompilerParams(vmem_limit_bytes=40 * MIB),
        name=name)(core, slots, from_sibling)


def _small_reduce(pack_g, meta_g, loss_scale, name, comm=None):
    w = pack_g.shape[2]

    def body(p_ref, m_ref, tot_ref, meta_ref, loss_ref):
        acc = p_ref[0]
        macc = m_ref[0]
        for k in range(1, N_DEV):
            acc = acc + p_ref[k]
            macc = macc + m_ref[k]
        tot = jnp.sum(acc, axis=0, keepdims=True)
        tot_ref[...] = tot
        meta_ref[...] = macc
        loss_ref[...] = jnp.full((1, LANES), loss_scale * jnp.sum(tot[:, w - LANES:w]), F32)

    return pl.pallas_call(
        body, out_shape=[SDS((1, w), F32), SDS(meta_g.shape[1:], F32), SDS((1, LANES), F32)],
        compiler_params=pltpu.CompilerParams(vmem_limit_bytes=32 * MIB), name=name)(pack_g, meta_g)


def _local_step(x, target, sw, plan):
    s_len, d = x.shape
    n_heads, n_meta = plan.n_heads, plan.n_meta
    l = n_meta + s_len
    lp = -(-l // LANES) * LANES
    tm = _largest_tile(lp, 544, 16)
    tq = _largest_tile(lp, 272, 16)
    te = _largest_tile(lp, 272, 16)
    tmd = _largest_tile(d, 512, LANES)

    plan.at("start")
    x, target = plan.gate((x, target))
    zmeta, zpad = jnp.zeros((n_meta, d), F32), jnp.zeros((lp - l, d), F32)
    h0 = jnp.concatenate([zmeta, x, zpad], axis=0)
    tpad = jnp.concatenate([zmeta, target, zpad], axis=0)
    plan.at("landed", (h0, tpad))
    h0 = lax.dynamic_update_slice(h0, plan.weights("meta"), (0, 0))

    wg1, wu1, wd1 = plan.weights("ffn1")
    fs = wg1.shape[1]
    h1, a1, b1, u1 = _ffn_fwd(h0, sw["ffn1_norm"], wg1, wu1, wd1, tm, "ffn1_fwd", plan.comm("ffn1_fwd"))
    plan.at("after_ffn1_fwd", (h1,))
    win, pw, wout = plan.weights("mix")
    nz = win.shape[1]
    p_w = sw["pool_scale"].shape[1]
    npb = p_w // LANES
    fblk = nz // LANES - 1
    tnz = _largest_tile(nz, 1408, LANES)
    qw, kw, bfp, ps = sw["q_norm"], sw["k_norm"], sw["b_forget"], sw["pool_scale"]
    z, u2 = _norm_matmul(h1, sw["mix_norm"], win, tm, tnz, "mix_in", plan.comm("mix_in"))
    cum = _fox_prep(z, bfp, fblk, "fox_prep")
    cumt = cum[:, :n_heads].T
    pool_o = _pool_fwd(z, pw, ps, "pool_fwd")
    att_o = _att_fwd(z, cum, cumt, qw, kw, n_heads, npb, tq, "att_fwd", plan.comm("att_fwd"))
    plan.at("after_att_fwd", (att_o,))
    h2 =_out_proj(h1, pool_o, att_o, wout, tm, "out_proj", plan.comm("out_proj"))
    wg2, wu2, wd2 = plan.weights("ffn2")
    h3, a2, b2, u3 = _ffn_fwd(h2, sw["ffn2_norm"], wg2, wu2, wd2, tm, "ffn2_fwd", plan.comm("ffn2_fwd"))
    dy, dob3, lsq = _loss_head(h3, tpad, n_meta, l, te, "loss_head")

    du3, da2, db2, hid2 = _ffn_bwd_dx(dob3, a2, b2, wg2, wu2, wd2, tm, "ffn2_bwd_dx", plan.comm("ffn2_bwd_dx"))
    dh2, dh2b, dn2 = _rms_bwd(du3, h2, sw["ffn2_norm"], dy, 1.0, te, "ffn2_rms_bwd")
    plan.grad("ffn2_w_gate", _matmul_tn(da2, u3, fs, d, "ffn2_dwg", plan.comm("ffn2_dwg")))
    plan.grad("ffn2_w_up", _matmul_tn(db2, u3, fs, d, "ffn2_dwu", plan.comm("ffn2_dwu")))
    plan.grad("ffn2_w_down", _matmul_tn(hid2, dob3, fs, d, "ffn2_dwd", plan.comm("ffn2_dwd")))
    plan.at("after_ffn2_dwd")

    dmix = _matmul_nt(dh2b, wout, tm, d, BF, "out_proj_bwd", plan.comm("out_proj_bwd"))
    plan.at("after_out_proj_bwd")
    tmp = _largest_tile(p_w, 512, LANES)
    plan.grad("w_out", jnp.concatenate([_matmul_tn(pool_o, dh2b, tmp, d, "dwout_pool"),
                                        _matmul_tn(att_o, dh2b, tmp, d, "dwout_att")], axis=0))
    dzp, dpw, dps = _pool_bwd(z, dmix, pw, ps, "pool_bwd")
    plan.grad("pool_w", dpw)
    plan.at("before_att_bwd")
    dq, dk, dv, dcq, dck, dqw, dkw = _att_bwd(z, cum, cumt, qw, kw, dmix, n_heads, npb, npb, tq, "att_bwd",
                                              plan.comm("att_bwd"))
    dcum = dcq[:, :, 0].T - dck[:, 0, :].T
    dcum = jnp.pad(dcum, ((0, 0), (0, LANES - n_heads)))
    dzf, dbf = _fox_bwd(z, bfp, dcum, fblk, "fox_bwd")
    dz = jnp.concatenate([dzp, dq, dk, dv, dzf], axis=1)
    plan.grad("w_in", _matmul_tn(u2, dz, tmd, tnz, "dwin", plan.comm("dwin")))
    du2 = _matmul_nt(dz, win, tm, tnz, F32, "mix_in_bwd", plan.comm("mix_in_bwd"))
    plan.at("before_ffn1_bwd_dx")
    dh1, dob1, dnm = _rms_bwd(du2, h1, sw["mix_norm"], dh2, 0.5, te, "mix_rms_bwd")

    du1, da1, db1, hid1 = _ffn_bwd_dx(dob1, a1, b1, wg1, wu1, wd1, tm, "ffn1_bwd_dx", plan.comm("ffn1_bwd_dx"))
    plan.grad("ffn1_w_gate", _matmul_tn(da1, u1, fs, d, "ffn1_dwg", plan.comm("ffn1_dwg")))
    plan.grad("ffn1_w_up", _matmul_tn(db1, u1, fs, d, "ffn1_dwu", plan.comm("ffn1_dwu")))
    plan.at("before_ffn1_dwd")
    plan.grad("ffn1_w_down", _matmul_tn(hid1, dob1, fs, d, "ffn1_dwd", plan.comm("ffn1_dwd")))
    plan.at("after_ffn1_dwd")
    dh0, _, dn1 = _rms_bwd(du1, h0, sw["ffn1_norm"], dh1, 1.0, te, "ffn1_rms_bwd", plan.comm("ffn1_rms_bwd"))
    plan.at("after_ffn1_rms_bwd")

    small = [dn1, dnm, dn2, dps, dqw, dkw, dbf, lsq]
    return dh0[n_meta:l], dh0[:n_meta], small


_BIG = ("ffn1_w_gate", "ffn1_w_up", "ffn1_w_down", "w_in", "pool_w", "w_out", "ffn2_w_gate", "ffn2_w_up", "ffn2_w_down")
_SMALL = ("ffn1_norm", "mix_norm", "ffn2_norm", "pool_scale", "q_norm", "k_norm", "b_forget")
_ORDER = ("meta_tokens", "ffn1_norm", "ffn1_w_gate", "ffn1_w_up", "ffn1_w_down", "mix_norm", "w_in", "b_forget",
          "q_norm", "k_norm", "pool_w", "pool_scale", "w_out", "ffn2_norm", "ffn2_w_gate", "ffn2_w_up", "ffn2_w_down")


_FFN1 = ("ffn1_w_gate", "ffn1_w_up", "ffn1_w_down")
_FFN2 = ("ffn2_w_gate", "ffn2_w_up", "ffn2_w_down")
_MIX = ("w_in", "pool_w", "w_out")

_RIDES = {
    "out_proj": (("g2", _FFN2),),
    "ffn2_dwu": (("s1", ("ffn2_w_gate",)),),
    "ffn2_dwd": (("s1", ("ffn2_w_up",)),),
    "out_proj_bwd": (("s1", ("ffn2_w_down",)),),
    "mix_in_bwd": (("s1", _MIX),),
    "ffn1_dwu": (("s1", ("ffn1_w_gate",)),),
    "ffn1_dwd": (("s1", ("ffn1_w_up",)),),
    "ffn1_rms_bwd": (("s1", ("ffn1_w_down",)),),
}
_G1_FFN1 = _FFN1 + ("meta_tokens",)
_POINTS = {
    "start": (("start", "g1", _G1_FFN1), ("gate", _MIX + _FFN2), ("prepare", "g1", _MIX),
              ("prepare", "g1", ("ffn2_w_down",)), ("prepare", "g1", ("ffn2_w_gate", "ffn2_w_up"))),
    "landed": (("wait", "g1", _G1_FFN1), ("start", "g1", _MIX), ("start", "g1", ("ffn2_w_down",)),
               ("alone", "g2", _G1_FFN1)),
    "after_ffn1_fwd": (("wait", "g1", _MIX), ("start", "g1", ("ffn2_w_gate", "ffn2_w_up")), ("alone", "g2", _MIX)),
    "after_att_fwd": (("wait", "g1", ("ffn2_w_down",)), ("wait", "g1", ("ffn2_w_gate", "ffn2_w_up"))),
    "after_ffn2_dwd": (("sum", ("ffn2_w_gate",)), ("start", "s2", ("ffn2_w_gate",))),
    "after_out_proj_bwd": (("sum", ("ffn2_w_up",)), ("start", "s2", ("ffn2_w_up",))),
    "before_att_bwd": (("sum", ("ffn2_w_down",)), ("start", "s2", ("ffn2_w_down",))),
    "before_ffn1_bwd_dx": (("sum", _MIX), ("start", "s2", _MIX)),
    "before_ffn1_dwd": (("sum", ("ffn1_w_gate",)), ("start", "s2", ("ffn1_w_gate",))),
    "after_ffn1_dwd": (("sum", ("ffn1_w_up",)), ("start", "s2", ("ffn1_w_up",))),
    "after_ffn1_rms_bwd": (("sum", ("ffn1_w_down",)), ("start", "s2", ("ffn1_w_down",))),
    "before_adamw_ffn2_w_gate": (("wait", "s2", ("ffn2_w_gate",)),),
    "before_adamw_ffn2_w_up": (("wait", "s2", ("ffn2_w_up",)),),
    "before_adamw_ffn2_w_down": (("wait", "s2", ("ffn2_w_down",)),),
    "before_adamw_w_in": (("wait", "s2", _MIX),),
    "before_adamw_ffn1_w_gate": (("wait", "s2", ("ffn1_w_gate",)),),
    "before_adamw_ffn1_w_up": (("wait", "s2", ("ffn1_w_up",)),),
    "before_adamw_ffn1_w_down": (("wait", "s2", ("ffn1_w_down",)),),
}


def _own_slot_filled(block, slot, n_slots):
    zone = lax.empty((n_slots,) + block.shape, block.dtype)
    return lax.dynamic_update_slice(zone, block[None], (slot,) + (0,) * block.ndim)


class _MeshPlan:
    def __init__(self, raw, pos, d, d_in, n_heads):
        self.raw, self.pos = dict(raw), pos
        self.core = pos[2].astype(jnp.int32).reshape(1)
        self.d, self.d_in, self.n_heads, self.n_meta = d, d_in, n_heads, raw["meta_tokens"].shape[0]
        self.partial, self.full, self.slots, self.from_sibling, self.chip_sum, self.received = {}, {}, {}, {}, {}, {}
        self.pending, self.prepared, self.started, self.tokens = [], {}, {}, []

    def gate(self, arrays):
        gated = lax.optimization_barrier((self.tokens[-1], tuple(arrays)))
        self.tokens[-1] = gated[0]
        return gated[1]

    def _phase(self, kind, names):
        src, dst, make = {"g2": (self.partial, self.full, _gather_fwd),
                          "s1": (self.slots, self.from_sibling, _scatter_sibling),
                          "s2": (self.chip_sum, self.received, _scatter_ici)}[kind]
        op = make([src[n] for n in names])
        self.pending.append((op, dst, names))
        return op

    def _settle(self):
        for op, dst, names in self.pending:
            dst.update(zip(names, op.results))
        self.pending = []

    def _prepare(self, kind, names):
        x, y, c = self.pos
        if kind == "g1":
            blocks = [self.raw[n] if n == "meta_tokens" else self.raw[n].astype(BF) for n in names]
            op = _gather_ici(blocks, [_own_slot_filled(b, 4 * x + 2 * y + c, N_DEV) for b in blocks])
        else:
            sums = [self.chip_sum[n] for n in names]
            mine = [lax.dynamic_index_in_dim(s, 2 * x + y, 0, keepdims=False) for s in sums]
            op = _scatter_ici(sums, [_own_slot_filled(b, 2 * x + y, N_DEV // 2) for b in mine])
        self.prepared[(kind, names)] = op

    def _start(self, kind, names):
        if (kind, names) not in self.prepared:
            self._prepare(kind, names)
        op = self.prepared.pop((kind, names))
        self.started[(kind, names)], token = _split_start(op, "_".join(("start", kind, names[0])))
        self.tokens.append(token)

    def _wait(self, kind, names, afters):
        afters = list(afters) + [a for op in self.prepared.values() for a in op.arrs[len(op.arrs) // 2:]]
        landed = _split_wait(self.started.pop((kind, names)), afters, "_".join(("wait", kind, names[0])))
        (self.partial if kind == "g1" else self.received).update(zip(names, landed))

    def comm(self, kernel_name):
        self._settle()
        ops = [self._phase(kind, names) for kind, names in _RIDES.get(kernel_name, ())]
        if self.tokens:
            ops.append(_Comm(self.tokens, [], [], lambda *a: None, lambda *a: None))
            self.tokens = []
        return _merge_comm(ops)

    def at(self, point, after=()):
        for step in _POINTS.get(point, ()):
            self._settle()
            if step[0] == "alone":
                _comm_alone(self._phase(step[1], step[2]), "_".join((step[1], point)))
            elif step[0] == "start":
                self._start(step[1], step[2])
            elif step[0] == "prepare":
                self._prepare(step[1], step[2])
            elif step[0] == "gate":
                self.raw.update(zip(step[1], self.gate([self.raw[n] for n in step[1]])))
            elif step[0] == "wait":
                self._wait(step[1], step[2], tuple(after) + tuple(self.tokens[-1:]))
            else:
                for n in step[1]:
                    self.chip_sum[n] = _chip_sum(self.slots[n], self.from_sibling[n], self.core, "chip_sum_" + n)

    def weights(self, group):
        self._settle()
        f, d = self.full, self.d
        if group == "meta":
            g = f["meta_tokens"]
            return g.transpose(1, 0, 2).reshape(g.shape[1], d)
        if group == "ffn1":
            return tuple(f[n] for n in _FFN1)
        if group == "ffn2":
            return tuple(f[n] for n in _FFN2)
        n_main = self.d_in - self.n_heads
        win = f["w_in"].transpose(1, 0, 2).reshape(d, self.d_in)
        win = jnp.concatenate([win[:, :n_main], jnp.pad(win[:, n_main:], ((0, 0), (0, LANES - self.n_heads)))], axis=1)
        pw = f["pool_w"]
        gw = pw.shape[2]
        pw = pw.reshape(N_DEV, -1, gw // N_DEV, gw).transpose(1, 0, 2, 3).reshape(-1, gw, gw)
        return win, pw, f["w_out"].reshape(-1, d)

    def grad(self, name, g):
        d = self.d
        if name == "w_in":
            g = g[:, :self.d_in].reshape(d, N_DEV, -1).transpose(1, 0, 2)
        elif name == "pool_w":
            ng, gw = g.shape[0], g.shape[2]
            g = g.astype(BF).reshape(ng, N_DEV, -1, gw).transpose(1, 0, 2, 3).reshape(N_DEV, -1, gw)
        elif name == "w_out":
            g = g.reshape(N_DEV, -1, d)
        self.slots[name] = g.reshape((N_DEV // 2, 2) + g.shape[1:])

    def gradient_parts(self, name):
        self._settle()
        return self.received[name]


_TRANSPOSED = ("ffn1_w_gate", "ffn1_w_up", "ffn2_w_gate", "ffn2_w_up")


def _as2d(name, a):
    return a[0].T if name in _TRANSPOSED else a.reshape(-1, a.shape[-1])


def _from2d(name, a2d, shape):
    return a2d.T.reshape(shape) if name in _TRANSPOSED else a2d.reshape(shape)


def kernel(x, meta_tokens, ffn1_norm, ffn1_w_gate, ffn1_w_up, ffn1_w_down, mix_norm, w_in, b_forget, q_norm, k_norm, pool_w, pool_scale, w_out, ffn2_norm, ffn2_w_gate, ffn2_w_up, ffn2_w_down, loss_target, m_meta_tokens, m_ffn1_norm, m_ffn1_w_gate, m_ffn1_w_up, m_ffn1_w_down, m_mix_norm, m_w_in, m_b_forget, m_q_norm, m_k_norm, m_pool_w, m_pool_scale, m_w_out, m_ffn2_norm, m_ffn2_w_gate, m_ffn2_w_up, m_ffn2_w_down, v_meta_tokens, v_ffn1_norm, v_ffn1_w_gate, v_ffn1_w_up, v_ffn1_w_down, v_mix_norm, v_w_in, v_b_forget, v_q_norm, v_k_norm, v_pool_w, v_pool_scale, v_w_out, v_ffn2_norm, v_ffn2_w_gate, v_ffn2_w_up, v_ffn2_w_down):
    w = dict(meta_tokens=meta_tokens, ffn1_norm=ffn1_norm, ffn1_w_gate=ffn1_w_gate, ffn1_w_up=ffn1_w_up,
             ffn1_w_down=ffn1_w_down, mix_norm=mix_norm, w_in=w_in, b_forget=b_forget, q_norm=q_norm, k_norm=k_norm,
             pool_w=pool_w, pool_scale=pool_scale, w_out=w_out, ffn2_norm=ffn2_norm, ffn2_w_gate=ffn2_w_gate,
             ffn2_w_up=ffn2_w_up, ffn2_w_down=ffn2_w_down)
    m = dict(meta_tokens=m_meta_tokens, ffn1_norm=m_ffn1_norm, ffn1_w_gate=m_ffn1_w_gate, ffn1_w_up=m_ffn1_w_up,
             ffn1_w_down=m_ffn1_w_down, mix_norm=m_mix_norm, w_in=m_w_in, b_forget=m_b_forget, q_norm=m_q_norm,
             k_norm=m_k_norm, pool_w=m_pool_w, pool_scale=m_pool_scale, w_out=m_w_out, ffn2_norm=m_ffn2_norm,
             ffn2_w_gate=m_ffn2_w_gate, ffn2_w_up=m_ffn2_w_up, ffn2_w_down=m_ffn2_w_down)
    v = dict(meta_tokens=v_meta_tokens, ffn1_norm=v_ffn1_norm, ffn1_w_gate=v_ffn1_w_gate, ffn1_w_up=v_ffn1_w_up,
             ffn1_w_down=v_ffn1_w_down, mix_norm=v_mix_norm, w_in=v_w_in, b_forget=v_b_forget, q_norm=v_q_norm,
             k_norm=v_k_norm, pool_w=v_pool_w, pool_scale=v_pool_scale, w_out=v_w_out, ffn2_norm=v_ffn2_norm,
             ffn2_w_gate=v_ffn2_w_gate, ffn2_w_up=v_ffn2_w_up, ffn2_w_down=v_ffn2_w_down)

    d = x.shape[-1]
    n_heads = b_forget.shape[-1]
    pos = (lax.axis_index("x"), lax.axis_index("y"), lax.axis_index("c"))
    me = 4 * pos[0] + 2 * pos[1] + pos[2]

    raw = {k: _as2d(k, w[k]) for k in _BIG}
    raw["meta_tokens"] = meta_tokens
    plan = _MeshPlan(raw, pos, d, N_DEV * w_in.shape[-1], n_heads)
    sw = {k: w[k] for k in _SMALL}
    sw["b_forget"] = jnp.pad(b_forget, ((0, 0), (0, LANES - n_heads)))
    dx, dmeta, small = _local_step(x[0], loss_target[0], sw, plan)

    res = {}
    last = dx

    def update_shards(names):
        nonlocal last
        for k in names:
            plan.at("before_adamw_" + k, (last,))
            res[k] = _adamw(plan.gradient_parts(k), _as2d(k, w[k]), _as2d(k, m[k]), _as2d(k, v[k]), "adamw_" + k,
                            plan.comm("adamw_" + k))
            last = res[k][0]

    update_shards(_FFN2 + _MIX)

    pack = jnp.concatenate(small, axis=1)
    pack_g, meta_g = _exchange([pack, dmeta], False, "gather_small")
    tot, dmeta_tot, loss_row = _small_reduce(pack_g, meta_g, 0.5 / d, "small_reduce")

    mcols = meta_tokens.shape[1]
    g_meta = lax.dynamic_slice_in_dim(dmeta_tot, me * mcols, mcols, axis=1)
    res["meta_tokens"] = _adamw(g_meta, meta_tokens, m_meta_tokens, v_meta_tokens, "adamw_meta_tokens")

    def packed(src):
        return jnp.concatenate([src[k] for k in _SMALL[:-1]] + [jnp.pad(src["b_forget"], ((0, 0), (0, LANES - n_heads)))],
                               axis=1)

    wp = packed(w)
    sm = _adamw(tot[:, :wp.shape[1]], wp, packed(m), packed(v), "adamw_small")
    off = 0
    for k in _SMALL:
        width = w[k].shape[1]
        res[k] = tuple(o[:, off:off + width] for o in sm)
        off += width if k != "b_forget" else LANES

    last = sm[0]
    update_shards(_FFN1)

    outs =[loss_row[0, 0], dx[None]]
    for idx in range(4):
        outs += [_from2d(k, res[k][idx], w[k].shape) for k in _ORDER]
    return tuple(outs)
```

```python
import functools

import jax
import jax.numpy as jnp
from jax import lax
from jax.experimental import pallas as pl
from jax.experimental.pallas import tpu as pltpu

F32 = jnp.float32
BF = jnp.bfloat16
SDS = jax.ShapeDtypeStruct

N_DEV = 8
LANES = 128
SUBLANES = 8
HEAD_DIM = 128
POOL_WINDOWS = (2, 4, 8, 16)
RMS_EPS = 1e-6
NEG_BIG = -1e30
MIB = 1024 * 1024

ADAM_LR = 0.001
ADAM_B1 = 0.9
ADAM_B2 = 0.999
ADAM_EPS = 1e-08
ADAM_WD = 0.01
ADAM_STEP = 10


class _Comm:
    def __init__(self, arrs, out_shape, sems, start, finish, aliases=None):
        self.arrs, self.out_shape, self.sems = list(arrs), list(out_shape), list(sems)
        self.start, self.finish, self.aliases = start, finish, dict(aliases or {})
        self.results = None


def _merge_comm(ops):
    ops = [op for op in ops if op is not None]
    if not ops:
        return None
    na, no, ns = [0], [0], [0]
    for op in ops:
        na.append(na[-1] + len(op.arrs))
        no.append(no[-1] + len(op.out_shape))
        ns.append(ns[-1] + len(op.sems))

    def parts(i, ins, outs, sems):
        return ins[na[i]:na[i + 1]], outs[no[i]:no[i + 1]], sems[ns[i]:ns[i + 1]]

    def start(ins, outs, sems):
        for i, op in enumerate(ops):
            op.start(*parts(i, ins, outs, sems))

    def finish(ins, outs, sems):
        for i, op in enumerate(ops):
            op.finish(*parts(i, ins, outs, sems))

    aliases = {}
    for i, op in enumerate(ops):
        for a, o in op.aliases.items():
            aliases[na[i] + a] = no[i] + o
    merged = _Comm([a for op in ops for a in op.arrs], [s for op in ops for s in op.out_shape],
                   [s for op in ops for s in op.sems], start, finish, aliases)
    merged.children = (ops, no)
    return merged


def _deliver(comm, results):
    comm.results = list(results)
    if hasattr(comm, "children"):
        ops, no = comm.children
        for i, op in enumerate(ops):
            _deliver(op, results[no[i]:no[i + 1]])


def _call(body, *, grid, in_specs, out_specs, out_shape, scratch_shapes=(), vmem_mib, name, comm=None):
    single = not isinstance(out_shape, (list, tuple))
    out_specs = [out_specs] if single else list(out_specs)
    out_shape = [out_shape] if single else list(out_shape)
    in_specs, scratch_shapes = list(in_specs), list(scratch_shapes)
    params = pltpu.CompilerParams(dimension_semantics=("arbitrary",) * len(grid), vmem_limit_bytes=vmem_mib * MIB)
    n_in, n_out, n_scr = len(in_specs), len(out_specs), len(scratch_shapes)

    def run(*args):
        if comm is None:
            res = pl.pallas_call(body, grid=grid, in_specs=in_specs, out_specs=out_specs, out_shape=out_shape,
                                 scratch_shapes=scratch_shapes, compiler_params=params, name=name)(*args)
            return res[0] if single else res
        ci, co = len(comm.arrs), len(comm.out_shape)

        def with_comm(*refs):
            ins, cins = refs[:n_in], refs[n_in:n_in + ci]
            o0 = n_in + ci
            outs, couts = refs[o0:o0 + n_out], refs[o0 + n_out:o0 + n_out + co]
            s0 = o0 + n_out + co
            scr, csems = refs[s0:s0 + n_scr], refs[s0 + n_scr:]
            ids = [pl.program_id(a) for a in range(len(grid))]
            first = functools.reduce(jnp.logical_and, [i == 0 for i in ids])
            last = functools.reduce(jnp.logical_and, [i == g - 1 for i, g in zip(ids, grid)])

            @pl.when(first)
            def _():
                comm.start(cins, couts, csems)

            body(*ins, *outs, *scr)

            @pl.when(last)
            def _():
                comm.finish(cins, couts, csems)

        anyspec = pl.BlockSpec(memory_space=pl.ANY)
        res = pl.pallas_call(
            with_comm, grid=grid, in_specs=in_specs + [anyspec] * ci, out_specs=out_specs + [anyspec] * co,
            out_shape=out_shape + comm.out_shape, scratch_shapes=scratch_shapes + comm.sems,
            input_output_aliases={n_in + a: n_out + o for a, o in comm.aliases.items()},
            compiler_params=params, name=name)(*args, *comm.arrs)
        _deliver(comm, res[n_out:])
        return res[0] if single else res[:n_out]

    return run


def _comm_alone(comm, name):
    def body(*refs):
        ci, co = len(comm.arrs), len(comm.out_shape)
        ins, outs, sems = refs[:ci], refs[ci:ci + co], refs[ci + co:]
        comm.start(ins, outs, sems)
        comm.finish(ins, outs, sems)

    anyspec = pl.BlockSpec(memory_space=pl.ANY)
    res = pl.pallas_call(
        body, in_specs=[anyspec] * len(comm.arrs), out_specs=[anyspec] * len(comm.out_shape),
        out_shape=comm.out_shape, scratch_shapes=comm.sems, input_output_aliases=comm.aliases, name=name)(*comm.arrs)
    _deliver(comm, res)


def _split_start(comm, name):
    na, ns = len(comm.arrs), len(comm.sems)

    def body(*refs):
        comm.start(refs[:na], None, refs[na:na + ns])
        token = refs[-1]
        token[...] = jnp.zeros_like(token)

    hbm = pl.BlockSpec(memory_space=pltpu.HBM)
    res = pl.pallas_call(
        body, name=name,
        out_shape=tuple(comm.sems) + tuple(pltpu.HBM(a.shape, a.dtype) for a in comm.arrs)
        + (SDS((SUBLANES, LANES), F32),),
        in_specs=[hbm] * na,
        out_specs=[pl.BlockSpec(memory_space=pltpu.SEMAPHORE)] * ns + [hbm] * na + [pl.BlockSpec(memory_space=pltpu.VMEM)],
        input_output_aliases={i: ns + i for i in range(na)},
        compiler_params=pltpu.CompilerParams(has_side_effects=pltpu.SideEffectType.DATAFLOW_SIDE_EFFECTING),
    )(*[pltpu.with_memory_space_constraint(a, pltpu.HBM) for a in comm.arrs])
    return (comm, res[:ns], res[ns:ns + na]), res[-1]


def _split_wait(started, afters, name):
    comm, sems, thru = started
    na, ns = len(thru), len(sems)
    afters = list(afters)

    def body(*refs):
        comm.finish(refs[:na], None, refs[na:na + ns])

    hbm = pl.BlockSpec(memory_space=pltpu.HBM)
    res = pl.pallas_call(
        body, name=name, out_shape=tuple(pltpu.HBM(a.shape, a.dtype) for a in thru),
        in_specs=[hbm] * na + [pl.BlockSpec(memory_space=pltpu.SEMAPHORE)] * ns
        + [pl.BlockSpec(memory_space=pl.ANY)] * len(afters),
        out_specs=[hbm] * na, input_output_aliases={i: i for i in range(na)},
        compiler_params=pltpu.CompilerParams(has_side_effects=pltpu.SideEffectType.DATAFLOW_SIDE_EFFECTING),
    )(*thru, *sems, *afters)
    return res[na - len(comm.out_shape):]


def _largest_tile(n, cap, mult):
    if n <= cap:
        return n
    best = None
    for t in range(mult, cap + 1, mult):
        if n % t == 0:
            best = t
    assert best is not None, (n, cap, mult)
    return best


def _dot(a, b):
    return jnp.dot(a, b, preferred_element_type=F32)


def _dot_nt(a, b):
    return lax.dot_general(a, b, (((1,), (1,)), ((), ())), preferred_element_type=F32)


def _dot_tn(a, b):
    return lax.dot_general(a, b, (((0,), (0,)), ((), ())), preferred_element_type=F32)


def _rows8(x):
    t, c = x.shape
    return jnp.sum(x.reshape(t // SUBLANES, SUBLANES, c), axis=0)


def _rstd(x):
    return lax.rsqrt(jnp.mean(x * x, axis=-1, keepdims=True) + RMS_EPS)


def _ffn_fwd(h, g, wg, wu, wd, tm, name, comm=None):
    lp, d = h.shape
    ns, fs, _ = wg.shape

    def body(h_ref, g_ref, wg_ref, wu_ref, wd_ref, out_ref, a_ref, b_ref, u_ref, acc_ref):
        j = pl.program_id(1)

        @pl.when(j == 0)
        def _():
            hh = h_ref[...]
            u_ref[...] = (hh * _rstd(hh) * g_ref[...]).astype(BF)
            acc_ref[...] = jnp.zeros_like(acc_ref)

        u = u_ref[...]
        a = _dot_nt(u, wg_ref[...])
        b = _dot_nt(u, wu_ref[...])
        a_ref[...] = a.astype(BF)
        b_ref[...] = b.astype(BF)
        hid = (a * jax.nn.sigmoid(a) * b).astype(BF)
        acc_ref[...] += _dot(hid, wd_ref[...])

        @pl.when(j == ns - 1)
        def _():
            out_ref[...] = h_ref[...] + 0.5 * acc_ref[...]

    row = pl.BlockSpec((tm, d), lambda i, j: (i, 0))
    act = pl.BlockSpec((None, tm, fs), lambda i, j: (j, i, 0))
    return _call(
        body, grid=(lp // tm, ns),
        in_specs=[row, pl.BlockSpec((1, d), lambda i, j: (0, 0)),
                  pl.BlockSpec((None, fs, d), lambda i, j: (j, 0, 0)),
                  pl.BlockSpec((None, fs, d), lambda i, j: (j, 0, 0)),
                  pl.BlockSpec((None, fs, d), lambda i, j: (j, 0, 0))],
        out_specs=[row, act, act, row],
        out_shape=[SDS((lp, d), F32), SDS((ns, lp, fs), BF), SDS((ns, lp, fs), BF), SDS((lp, d), BF)],
        scratch_shapes=[pltpu.VMEM((tm, d), F32)],
        vmem_mib=56, name=name, comm=comm)(h, g, wg, wu, wd)


def _ffn_bwd_dx(dob, a, b, wg, wu, wd, tm, name, comm=None):
    lp, d = dob.shape
    ns, fs, _ = wg.shape

    def body(do_ref, a_ref, b_ref, wg_ref, wu_ref, wd_ref, du_ref, da_ref, db_ref, hid_ref):
        j = pl.program_id(1)

        @pl.when(j == 0)
        def _():
            du_ref[...] = jnp.zeros_like(du_ref)

        dhid = _dot_nt(do_ref[...], wd_ref[...])
        av = a_ref[...].astype(F32)
        bv = b_ref[...].astype(F32)
        sig = jax.nn.sigmoid(av)
        sil = av * sig
        dbv = (dhid * sil).astype(BF)
        dav = (dhid * bv * (sig * (1.0 + av * (1.0 - sig)))).astype(BF)
        hid_ref[...] = (sil * bv).astype(BF)
        da_ref[...] = dav
        db_ref[...] = dbv
        du_ref[...] += _dot(dav, wg_ref[...]) + _dot(dbv, wu_ref[...])

    row = pl.BlockSpec((tm, d), lambda i, j: (i, 0))
    act = pl.BlockSpec((None, tm, fs), lambda i, j: (j, i, 0))
    return _call(
        body, grid=(lp // tm, ns),
        in_specs=[row, act, act,
                  pl.BlockSpec((None, fs, d), lambda i, j: (j, 0, 0)),
                  pl.BlockSpec((None, fs, d), lambda i, j: (j, 0, 0)),
                  pl.BlockSpec((None, fs, d), lambda i, j: (j, 0, 0))],
        out_specs=[row, act, act, act],
        out_shape=[SDS((lp, d), F32)] + [SDS((ns, lp, fs), BF)] * 3,
        vmem_mib=56, name=name, comm=comm)(dob, a, b, wg, wu, wd)


def _rms_bwd(du, h, g, dres, bscale, tm, name, comm=None):
    lp, d = h.shape

    def body(du_ref, h_ref, g_ref, dres_ref, dh_ref, dhb_ref, dg_ref):
        @pl.when(pl.program_id(0) == 0)
        def _():
            dg_ref[...] = jnp.zeros_like(dg_ref)

        hh = h_ref[...]
        r = _rstd(hh)
        xhat = hh * r
        duv = du_ref[...]
        dg_ref[...] += _rows8(duv * xhat)
        dxh = duv * g_ref[...]
        dh = dres_ref[...] + r * (dxh - xhat * jnp.mean(dxh * xhat, axis=-1, keepdims=True))
        dh_ref[...] = dh
        dhb_ref[...] = (bscale * dh).astype(BF)

    row = pl.BlockSpec((tm, d), lambda i: (i, 0))
    return _call(
        body, grid=(lp // tm,),
        in_specs=[row, row, pl.BlockSpec((1, d), lambda i: (0, 0)), row],
        out_specs=[row, row, pl.BlockSpec((SUBLANES, d), lambda i: (0, 0))],
        out_shape=[SDS((lp, d), F32), SDS((lp, d), BF), SDS((SUBLANES, d), F32)],
        vmem_mib=48, name=name, comm=comm)(du, h, g, dres)


def _matmul_tn(a, b, tm, tn, name, comm=None):
    a_b, b_b = a.ndim == 3, b.ndim == 3
    ns = a.shape[0] if a_b else (b.shape[0] if b_b else 1)
    l, m = a.shape[-2:]
    n = b.shape[-1]

    def body(a_ref, b_ref, o_ref):
        o_ref[...] = _dot_tn(a_ref[...], b_ref[...]).astype(o_ref.dtype)

    a_spec = (pl.BlockSpec((None, l, tm), lambda s, i, j: (s, 0, i)) if a_b
              else pl.BlockSpec((l, tm), lambda s, i, j: (0, i)))
    b_spec = (pl.BlockSpec((None, l, tn), lambda s, i, j: (s, 0, j)) if b_b
              else pl.BlockSpec((l, tn), lambda s, i, j: (0, j)))
    batched = a_b or b_b
    o_spec = (pl.BlockSpec((None, tm, tn), lambda s, i, j: (s, i, j)) if batched
              else pl.BlockSpec((tm, tn), lambda s, i, j: (i, j)))
    o_shape = SDS((ns, m, n), BF) if batched else SDS((m, n), BF)
    return _call(
        body, grid=(ns, m // tm, n // tn), in_specs=[a_spec, b_spec], out_specs=o_spec, out_shape=o_shape,
        vmem_mib=48, name=name, comm=comm)(a, b)


def _matmul_nt(x, w, tm, tk, out_dtype, name, comm=None):
    l, k = x.shape
    n = w.shape[0]
    nk = k // tk

    def body(x_ref, w_ref, o_ref, acc_ref):
        kk = pl.program_id(1)

        @pl.when(kk == 0)
        def _():
            acc_ref[...] = jnp.zeros_like(acc_ref)

        acc_ref[...] += _dot_nt(x_ref[...], w_ref[...])

        @pl.when(kk == nk - 1)
        def _():
            o_ref[...] = acc_ref[...].astype(o_ref.dtype)

    return _call(
        body, grid=(l // tm, nk),
        in_specs=[pl.BlockSpec((tm, tk), lambda i, kk: (i, kk)), pl.BlockSpec((n, tk), lambda i, kk: (0, kk))],
        out_specs=pl.BlockSpec((tm, n), lambda i, kk: (i, 0)),
        out_shape=SDS((l, n), out_dtype),
        scratch_shapes=[pltpu.VMEM((tm, n), F32)],
        vmem_mib=48, name=name, comm=comm)(x, w)


def _norm_matmul(h, g, w, tm, tn, name, comm=None):
    lp, d = h.shape
    n = w.shape[1]

    def body(h_ref, g_ref, w_ref, z_ref, u_ref):
        @pl.when(pl.program_id(1) == 0)
        def _():
            hh = h_ref[...]
            u_ref[...] = (hh * _rstd(hh) * g_ref[...]).astype(BF)

        z_ref[...] = _dot(u_ref[...], w_ref[...])

    row = pl.BlockSpec((tm, d), lambda i, j: (i, 0))
    return _call(
        body, grid=(lp // tm, n // tn),
        in_specs=[row, pl.BlockSpec((1, d), lambda i, j: (0, 0)), pl.BlockSpec((d, tn), lambda i, j: (0, j))],
        out_specs=[pl.BlockSpec((tm, tn), lambda i, j: (i, j)), row],
        out_shape=[SDS((lp, n), F32), SDS((lp, d), BF)],
        vmem_mib=48, name=name, comm=comm)(h, g, w)


def _out_proj(h, pool_o, att_o, w_out, tm, name, comm=None):
    lp, d = h.shape
    p = pool_o.shape[1]
    dm = w_out.shape[0]

    def body(h_ref, p_ref, a_ref, w_ref, o_ref):
        o_ref[...] = h_ref[...] + _dot(p_ref[...], w_ref[0:p, :]) + _dot(a_ref[...], w_ref[p:dm, :])

    row = pl.BlockSpec((tm, d), lambda i: (i, 0))
    return _call(
        body, grid=(lp // tm,),
        in_specs=[row, pl.BlockSpec((tm, p), lambda i: (i, 0)), pl.BlockSpec((tm, dm - p), lambda i: (i, 0)),
                  pl.BlockSpec((dm, d), lambda i: (0, 0))],
        out_specs=row, out_shape=SDS((lp, d), F32),
        vmem_mib=48, name=name, comm=comm)(h, pool_o, att_o, w_out)


def _loss_head(y, tpad, row0, row1, tm, name, comm=None):
    lp, d = y.shape

    def body(y_ref, t_ref, dy_ref, dob_ref, ls_ref):
        i = pl.program_id(0)

        @pl.when(i == 0)
        def _():
            ls_ref[...] = jnp.zeros_like(ls_ref)

        rows = i * tm + lax.broadcasted_iota(jnp.int32, (tm, d), 0)
        err = jnp.where((rows >= row0) & (rows < row1), y_ref[...] - t_ref[...], 0.0)
        dy = err * (1.0 / d)
        dy_ref[...] = dy
        dob_ref[...] = (0.5 * dy).astype(BF)
        sq = _rows8(err * err)
        acc = sq[:, 0:LANES]
        for c in range(1, d // LANES):
            acc = acc + sq[:, c * LANES:(c + 1) * LANES]
        ls_ref[...] += acc

    row = pl.BlockSpec((tm, d), lambda i: (i, 0))
    return _call(
        body, grid=(lp // tm,), in_specs=[row, row],
        out_specs=[row, row, pl.BlockSpec((SUBLANES, LANES), lambda i: (0, 0))],
        out_shape=[SDS((lp, d), F32), SDS((lp, d), BF), SDS((SUBLANES, LANES), F32)],
        vmem_mib=48, name=name, comm=comm)(y, tpad)


def _window_select(levels, gidx):
    out = levels[-1]
    for k in range(len(levels) - 2, -1, -1):
        out = jnp.where(gidx == k, levels[k], out)
    return out


def _pool_window_mean_minus_id(x, gidx):
    rows = lax.broadcasted_iota(jnp.int32, x.shape, 0)
    levels = []
    s = x
    shift = 1
    while shift < POOL_WINDOWS[-1]:
        s = s + jnp.where(rows >= shift, pltpu.roll(s, shift, 0), 0.0)
        shift *= 2
        if shift in POOL_WINDOWS:
            levels.append(s)
    win = _window_select(levels, gidx)
    cnt = jnp.minimum(rows + 1, _window_select(list(POOL_WINDOWS), gidx)).astype(F32)
    return win / cnt - x, cnt


def _pool_window_transpose(dy, cnt, gidx):
    lp = dy.shape[0]
    rows = lax.broadcasted_iota(jnp.int32, dy.shape, 0)
    levels = []
    s = dy / cnt
    shift = 1
    while shift < POOL_WINDOWS[-1]:
        s = s + jnp.where(rows < lp - shift, pltpu.roll(s, lp - shift, 0), 0.0)
        shift *= 2
        if shift in POOL_WINDOWS:
            levels.append(s)
    return _window_select(levels, gidx) - dy


def _pool_fwd(z, pool_w, pool_scale, name, comm=None):
    lp = z.shape[0]
    ng, gw, _ = pool_w.shape

    def body(p_ref, w_ref, s_ref, o_ref):
        pooled, _ = _pool_window_mean_minus_id(p_ref[...], pl.program_id(0))
        o_ref[...] = (_dot(pooled.astype(BF), w_ref[...]) * s_ref[...]).astype(BF)

    return _call(
        body, grid=(ng,),
        in_specs=[pl.BlockSpec((lp, gw), lambda g: (0, g)), pl.BlockSpec((None, gw, gw), lambda g: (g, 0, 0)),
                  pl.BlockSpec((1, gw), lambda g: (0, g))],
        out_specs=pl.BlockSpec((lp, gw), lambda g: (0, g)), out_shape=SDS((lp, ng * gw), BF),
        vmem_mib=48, name=name, comm=comm)(z, pool_w, pool_scale)


def _pool_bwd(z, dmix, pool_w, pool_scale, name, comm=None):
    lp = z.shape[0]
    ng, gw, _ = pool_w.shape

    def body(p_ref, d_ref, w_ref, s_ref, dz_ref, dw_ref, ds_ref):
        g = pl.program_id(0)
        pooled, cnt = _pool_window_mean_minus_id(p_ref[...], g)
        pooled_b = pooled.astype(BF)
        w = w_ref[...]
        mixed = _dot(pooled_b, w)
        dpo = d_ref[...].astype(F32)
        ds_ref[...] = _rows8(dpo * mixed)
        dmixed = (dpo * s_ref[...]).astype(BF)
        dw_ref[...] = _dot_tn(pooled_b, dmixed)
        dpooled = _dot_nt(dmixed, w)
        dz_ref[...] = _pool_window_transpose(dpooled, cnt, g).astype(BF)

    return _call(
        body, grid=(ng,),
        in_specs=[pl.BlockSpec((lp, gw), lambda g: (0, g)), pl.BlockSpec((lp, gw), lambda g: (0, g)),
                  pl.BlockSpec((None, gw, gw), lambda g: (g, 0, 0)), pl.BlockSpec((1, gw), lambda g: (0, g))],
        out_specs=[pl.BlockSpec((lp, gw), lambda g: (0, g)), pl.BlockSpec((None, gw, gw), lambda g: (g, 0, 0)),
                   pl.BlockSpec((SUBLANES, gw), lambda g: (0, g))],
        out_shape=[SDS((lp, ng * gw), BF), SDS((ng, gw, gw), F32), SDS((SUBLANES, ng * gw), F32)],
        vmem_mib=48, name=name, comm=comm)(z, dmix, pool_w, pool_scale)


def _log_sigmoid(x):
    return jnp.minimum(x, 0.0) - jnp.log(1.0 + jnp.exp(-jnp.abs(x)))


def _fox_prep(z, bfp, fblk, name, comm=None):
    lp = z.shape[0]
    nb = lp // LANES

    def body(f_ref, b_ref, cum_ref):
        r = lax.broadcasted_iota(jnp.int32, (LANES, LANES), 0)
        c = lax.broadcasted_iota(jnp.int32, (LANES, LANES), 1)
        tri = (r >= c).astype(F32)
        carry = jnp.zeros((1, LANES), F32)
        for blk in range(nb):
            sl = slice(blk * LANES, (blk + 1) * LANES)
            lf = _log_sigmoid(f_ref[sl, :] + b_ref[...])
            cb = jnp.dot(tri, lf, preferred_element_type=F32, precision=lax.Precision.HIGHEST) + carry
            cum_ref[sl, :] = cb
            carry = cb[LANES - 1:LANES, :]

    return _call(
        body, grid=(1,),
        in_specs=[pl.BlockSpec((lp, LANES), lambda i: (0, fblk)), pl.BlockSpec((1, LANES), lambda i: (0, 0))],
        out_specs=pl.BlockSpec((lp, LANES), lambda i: (0, 0)), out_shape=SDS((lp, LANES), F32),
        vmem_mib=32, name=name, comm=comm)(z, bfp)


def _fox_bwd(z, bfp, dcum, fblk, name, comm=None):
    lp = z.shape[0]
    nb = lp // LANES

    def body(f_ref, b_ref, dc_ref, dz_ref, db_ref):
        r = lax.broadcasted_iota(jnp.int32, (LANES, LANES), 0)
        c = lax.broadcasted_iota(jnp.int32, (LANES, LANES), 1)
        tri = (r <= c).astype(F32)
        carry = jnp.zeros((1, LANES), F32)
        acc = jnp.zeros((SUBLANES, LANES), F32)
        for blk in range(nb - 1, -1, -1):
            sl = slice(blk * LANES, (blk + 1) * LANES)
            dlf = jnp.dot(tri, dc_ref[sl, :], preferred_element_type=F32, precision=lax.Precision.HIGHEST) + carry
            carry = dlf[0:1, :]
            df = dlf * jax.nn.sigmoid(-(f_ref[sl, :] + b_ref[...]))
            dz_ref[sl, :] = df.astype(BF)
            acc = acc + _rows8(df)
        db_ref[...] = acc

    return _call(
        body, grid=(1,),
        in_specs=[pl.BlockSpec((lp, LANES), lambda i: (0, fblk)), pl.BlockSpec((1, LANES), lambda i: (0, 0)),
                  pl.BlockSpec((lp, LANES), lambda i: (0, 0))],
        out_specs=[pl.BlockSpec((lp, LANES), lambda i: (0, 0)), pl.BlockSpec((SUBLANES, LANES), lambda i: (0, 0))],
        out_shape=[SDS((lp, LANES), BF), SDS((SUBLANES, LANES), F32)],
        vmem_mib=32, name=name, comm=comm)(z, bfp, dcum)


def _att_scores(q_ref, cum_ref, cumt_ref, qw_ref, kn_s, h, i, tq, lk):
    scale = 1.0 / (HEAD_DIM ** 0.5)
    q = q_ref[...]
    rq = _rstd(q)
    qhat = q * rq
    qn = (qhat * qw_ref[...]).astype(BF)
    s = _dot_nt(qn, kn_s[0:lk, :]) * scale
    lane = lax.broadcasted_iota(jnp.int32, (tq, LANES), 1)
    cq = jnp.sum(jnp.where(lane == h, cum_ref[...], 0.0), axis=1, keepdims=True)
    ck = cumt_ref[pl.ds(h, 1), 0:lk]
    s = s + (cq - ck)
    qpos = i * tq + lax.broadcasted_iota(jnp.int32, (tq, lk), 0)
    kpos = lax.broadcasted_iota(jnp.int32, (tq, lk), 1)
    s = jnp.where(qpos >= kpos, s, NEG_BIG)
    e = jnp.exp(s - jnp.max(s, axis=1, keepdims=True))
    p = e / jnp.sum(e, axis=1, keepdims=True)
    return p, qn, qhat, rq


def _per_query_tile(i, nq, tq, lp, fn):
    for t in range(nq):
        lk = min(lp, -(-((t + 1) * tq) // LANES) * LANES)
        pl.when(i == t)(functools.partial(fn, lk))


def _att_fwd(z, cum, cumt, qw, kw, n_heads, qblk0, tq, name, comm=None):
    lp = z.shape[0]
    nh = n_heads

    def body(q_ref, k_ref, v_ref, cum_ref, cumt_ref, qw_ref, kw_ref, o_ref, kn_s, vb_s):
        h, i = pl.program_id(0), pl.program_id(1)

        @pl.when(i == 0)
        def _():
            k = k_ref[...]
            kn_s[...] = (k * _rstd(k) * kw_ref[...]).astype(BF)
            vb_s[...] = v_ref[...].astype(BF)

        def tile(lk):
            p, _, _, _ = _att_scores(q_ref, cum_ref, cumt_ref, qw_ref, kn_s, h, i, tq, lk)
            o_ref[...] = _dot(p.astype(BF), vb_s[0:lk, :]).astype(BF)

        _per_query_tile(i, lp // tq, tq, lp, tile)

    vec = pl.BlockSpec((1, HEAD_DIM), lambda h, i: (0, 0))
    return _call(
        body, grid=(nh, lp // tq),
        in_specs=[pl.BlockSpec((tq, HEAD_DIM), lambda h, i: (i, qblk0 + h)),
                  pl.BlockSpec((lp, HEAD_DIM), lambda h, i: (0, qblk0 + nh + h)),
                  pl.BlockSpec((lp, HEAD_DIM), lambda h, i: (0, qblk0 + 2 * nh + h)),
                  pl.BlockSpec((tq, LANES), lambda h, i: (i, 0)),
                  pl.BlockSpec((nh, lp), lambda h, i: (0, 0)), vec, vec],
        out_specs=pl.BlockSpec((tq, HEAD_DIM), lambda h, i: (i, h)),
        out_shape=SDS((lp, nh * HEAD_DIM), BF),
        scratch_shapes=[pltpu.VMEM((lp, HEAD_DIM), BF), pltpu.VMEM((lp, HEAD_DIM), BF)],
        vmem_mib=48, name=name, comm=comm)(z, z, z, cum, cumt, qw, kw)


def _att_bwd(z, cum, cumt, qw, kw, dmix, n_heads, qblk0, oblk0, tq, name, comm=None):
    lp = z.shape[0]
    nh = n_heads
    nq = lp // tq
    scale = 1.0 / (HEAD_DIM ** 0.5)

    def body(q_ref, k_ref, v_ref, cum_ref, cumt_ref, qw_ref, kw_ref, do_ref,
             dq_ref, dk_ref, dv_ref, dcq_ref, dck_ref, dqw_ref, dkw_ref,
             kn_s, vb_s, dkn_s, dv_s, dck_s):
        h, i = pl.program_id(0), pl.program_id(1)

        @pl.when((h == 0) & (i == 0))
        def _():
            dqw_ref[...] = jnp.zeros_like(dqw_ref)
            dkw_ref[...] = jnp.zeros_like(dkw_ref)

        @pl.when(i == 0)
        def _():
            k = k_ref[...]
            kn_s[...] = (k * _rstd(k) * kw_ref[...]).astype(BF)
            vb_s[...] = v_ref[...].astype(BF)
            dkn_s[...] = jnp.zeros_like(dkn_s)
            dv_s[...] = jnp.zeros_like(dv_s)
            dck_s[...] = jnp.zeros_like(dck_s)

        def tile(lk):
            p, qn, qhat, rq = _att_scores(q_ref, cum_ref, cumt_ref, qw_ref, kn_s, h, i, tq, lk)
            dob = do_ref[...]
            dp = _dot_nt(dob, vb_s[0:lk, :])
            ds = p * (dp - jnp.sum(p * dp, axis=1, keepdims=True))
            dsb = ds.astype(BF)
            dv_s[0:lk, :] += _dot_tn(p.astype(BF), dob)
            dkn_s[0:lk, :] += _dot_tn(dsb, qn)
            dcq_ref[...] = jnp.sum(ds, axis=1, keepdims=True)
            dck_s[:, 0:lk] += jnp.sum(ds, axis=0, keepdims=True)
            dqn = _dot(dsb, kn_s[0:lk, :]) * scale
            gq = dqn * qw_ref[...]
            dq_ref[...] = (rq * (gq - qhat * jnp.mean(gq * qhat, axis=-1, keepdims=True))).astype(BF)
            dqw_ref[...] += _rows8(dqn * qhat)

        _per_query_tile(i, nq, tq, lp, tile)

        @pl.when(i == nq - 1)
        def _():
            k = k_ref[...]
            rk = _rstd(k)
            khat = k * rk
            dkn = dkn_s[...] * scale
            gk = dkn * kw_ref[...]
            dk_ref[...] = (rk * (gk - khat * jnp.mean(gk * khat, axis=-1, keepdims=True))).astype(BF)
            dkw_ref[...] += _rows8(dkn * khat)
            dv_ref[...] = dv_s[...].astype(BF)
            dck_ref[...] = dck_s[...]

    vec = pl.BlockSpec((1, HEAD_DIM), lambda h, i: (0, 0))
    part = pl.BlockSpec((SUBLANES, LANES), lambda h, i: (0, 0))
    return _call(
        body, grid=(nh, nq),
        in_specs=[pl.BlockSpec((tq, HEAD_DIM), lambda h, i: (i, qblk0 + h)),
                  pl.BlockSpec((lp, HEAD_DIM), lambda h, i: (0, qblk0 + nh + h)),
                  pl.BlockSpec((lp, HEAD_DIM), lambda h, i: (0, qblk0 + 2 * nh + h)),
                  pl.BlockSpec((tq, LANES), lambda h, i: (i, 0)),
                  pl.BlockSpec((nh, lp), lambda h, i: (0, 0)), vec, vec,
                  pl.BlockSpec((tq, HEAD_DIM), lambda h, i: (i, oblk0 + h))],
        out_specs=[pl.BlockSpec((tq, HEAD_DIM), lambda h, i: (i, h)),
                   pl.BlockSpec((lp, HEAD_DIM), lambda h, i: (0, h)),
                   pl.BlockSpec((lp, HEAD_DIM), lambda h, i: (0, h)),
                   pl.BlockSpec((None, tq, 1), lambda h, i: (h, i, 0)),
                   pl.BlockSpec((None, 1, lp), lambda h, i: (h, 0, 0)),
                   part, part],
        out_shape=[SDS((lp, nh * HEAD_DIM), BF)] * 3
        + [SDS((nh, lp, 1), F32), SDS((nh, 1, lp), F32), SDS((SUBLANES, LANES), F32), SDS((SUBLANES, LANES), F32)],
        scratch_shapes=[pltpu.VMEM((lp, HEAD_DIM), BF), pltpu.VMEM((lp, HEAD_DIM), BF),
                        pltpu.VMEM((lp, HEAD_DIM), F32), pltpu.VMEM((lp, HEAD_DIM), F32),
                        pltpu.VMEM((1, lp), F32)],
        vmem_mib=56, name=name, comm=comm)(z, z, z, cum, cumt, qw, kw, dmix)


def _adamw_math(w, g, m, v):
    m2 = ADAM_B1 * m + (1.0 - ADAM_B1) * g
    v2 = ADAM_B2 * v + (1.0 - ADAM_B2) * (g * g)
    m_hat = m2 / (1.0 - ADAM_B1 ** ADAM_STEP)
    v_hat = v2 / (1.0 - ADAM_B2 ** ADAM_STEP)
    delta = -ADAM_LR * (m_hat / (jnp.sqrt(v_hat) + ADAM_EPS) + ADAM_WD * w)
    return delta, m2, v2


def _adamw(g_in, w, m, v, name, comm=None):
    r, c = w.shape
    partial_sum = g_in.ndim == 3
    lane_padded = -(-c // LANES) * LANES
    tr = _largest_tile(r, max(16, MIB // (4 * lane_padded) // 16 * 16), 16)

    def body(g_ref, w_ref, m_ref, v_ref, go_ref, d_ref, mo_ref, vo_ref):
        if partial_sum:
            g = g_ref[0].astype(F32)
            for k in range(1, g_in.shape[0]):
                g = g + g_ref[k].astype(F32)
        else:
            g = g_ref[...]
        delta, m2, v2 = _adamw_math(w_ref[...], g, m_ref[...], v_ref[...])
        go_ref[...] = g
        d_ref[...] = delta
        mo_ref[...] = m2
        vo_ref[...] = v2

    blk = pl.BlockSpec((tr, c), lambda i: (i, 0))
    g_spec = pl.BlockSpec((g_in.shape[0], tr, c), lambda i: (0, i, 0)) if partial_sum else blk
    return _call(
        body, grid=(r // tr,), in_specs=[g_spec, blk, blk, blk], out_specs=[blk] * 4,
        out_shape=[SDS((r, c), F32)] * 4, vmem_mib=40, name=name, comm=comm)(g_in, w, m, v)


def _peer(x, y, c, k):
    return (1 - x if k & 4 else x, 1 - y if k & 2 else y, 1 - c if k & 1 else c)


_SIBLING = 1
_ICI_RELS = (2, 4, 6)


def _mesh_pos():
    return lax.axis_index("x"), lax.axis_index("y"), lax.axis_index("c")


def _sem_pair(sems, t, j, n_rel, scalars):
    if scalars:
        i = 2 * (t * n_rel + j)
        return sems[i], sems[i + 1]
    return sems[0].at[t, j], sems[1].at[t, j]


def _dev(pos):
    return 4 * pos[0] + 2 * pos[1] + pos[2]


def _gather_ici(shards, landing=None, rels=(_SIBLING,) + _ICI_RELS):
    n = len(shards)

    def remote(ins, outs, sems, arrival):
        x, y, c = _mesh_pos()
        dst = ins[n:] if landing is not None else outs
        cps = []
        for j, k in enumerate(rels):
            peer = _peer(x, y, c, k)
            slot = _dev(peer) if arrival else _dev((x, y, c))
            for t in range(n):
                send_sem, recv_sem = _sem_pair(sems, t, j, len(rels), landing is not None)
                cps.append(pltpu.make_async_remote_copy(
                    src_ref=ins[t], dst_ref=dst[t].at[slot], send_sem=send_sem, recv_sem=recv_sem,
                    device_id=peer, device_id_type=pl.DeviceIdType.MESH))
        return cps

    if landing is not None:
        def start_remote(ins, outs, sems):
            for cp in remote(ins, outs, sems, False):
                cp.start()

        def finish_remote(ins, outs, sems):
            for cp in remote(ins, outs, sems, True):
                cp.wait_recv()
            for cp in remote(ins, outs, sems, False):
                cp.wait_send()

        return _Comm(list(shards) + list(landing), [SDS(a.shape, a.dtype) for a in landing],
                     [pltpu.SemaphoreType.DMA(())] * (2 * n * len(rels)),
                     start_remote, finish_remote, aliases={n + t: t for t in range(n)})

    def local(ins, outs, sems):
        me = _dev(_mesh_pos())
        return [pltpu.make_async_copy(ins[t], outs[t].at[me], sems[2].at[t]) for t in range(n)]

    def start(ins, outs, sems):
        for cp in local(ins, outs, sems) + remote(ins, outs, sems, False):
            cp.start()

    def finish(ins, outs, sems):
        for cp in local(ins, outs, sems):
            cp.wait()
        for cp in remote(ins, outs, sems, True):
            cp.wait_recv()
        for cp in remote(ins, outs, sems, False):
            cp.wait_send()

    return _Comm(shards, [SDS((N_DEV,) + s.shape, s.dtype) for s in shards],
                 [pltpu.SemaphoreType.DMA((n, len(rels))), pltpu.SemaphoreType.DMA((n, len(rels))),
                  pltpu.SemaphoreType.DMA((n,))], start, finish)


def _gather_fwd(partial):
    n = len(partial)

    def copies(ins, outs, sems, arrival):
        x, y, c = _mesh_pos()
        sibling = _peer(x, y, c, _SIBLING)
        cps = []
        for j, k in enumerate(_ICI_RELS):
            slot = _dev(_peer(x, y, c, k | _SIBLING if arrival else k))
            for t in range(n):
                cps.append(pltpu.make_async_remote_copy(
                    src_ref=ins[t].at[slot], dst_ref=outs[t].at[slot], send_sem=sems[0].at[t, j],
                    recv_sem=sems[1].at[t, j], device_id=sibling, device_id_type=pl.DeviceIdType.MESH))
        return cps

    def start(ins, outs, sems):
        for cp in copies(ins, outs, sems, False):
            cp.start()

    def finish(ins, outs, sems):
        for cp in copies(ins, outs, sems, True):
            cp.wait_recv()
        for cp in copies(ins, outs, sems, False):
            cp.wait_send()

    return _Comm(partial, [SDS(a.shape, a.dtype) for a in partial],
                 [pltpu.SemaphoreType.DMA((n, len(_ICI_RELS)))] * 2, start, finish,
                 aliases={t: t for t in range(n)})


def _scatter_sibling(slots):
    n = len(slots)

    def copies(ins, outs, sems):
        x, y, c = _mesh_pos()
        return [pltpu.make_async_remote_copy(
            src_ref=ins[t].at[:, 1 - c], dst_ref=outs[t], send_sem=sems[0].at[t], recv_sem=sems[1].at[t],
            device_id=_peer(x, y, c, _SIBLING), device_id_type=pl.DeviceIdType.MESH) for t in range(n)]

    def start(ins, outs, sems):
        for cp in copies(ins, outs, sems):
            cp.start()

    def finish(ins, outs, sems):
        for cp in copies(ins, outs, sems):
            cp.wait()

    return _Comm(slots, [SDS((s.shape[0],) + s.shape[2:], s.dtype) for s in slots],
                 [pltpu.SemaphoreType.DMA((n,))] * 2, start, finish)


def _scatter_ici(chip_sums, landing=None):
    n = len(chip_sums)

    def remote(ins, outs, sems, arrival):
        x, y, c = _mesh_pos()
        dst = ins[n:] if landing is not None else outs
        cps = []
        for j, k in enumerate(_ICI_RELS):
            peer = _peer(x, y, c, k)
            theirs, mine = 2 * peer[0] + peer[1], 2 * x + y
            for t in range(n):
                send_sem, recv_sem = _sem_pair(sems, t, j, len(_ICI_RELS), landing is not None)
                cps.append(pltpu.make_async_remote_copy(
                    src_ref=ins[t].at[theirs], dst_ref=dst[t].at[theirs if arrival else mine],
                    send_sem=send_sem, recv_sem=recv_sem,
                    device_id=peer, device_id_type=pl.DeviceIdType.MESH))
        return cps

    if landing is not None:
        def start_remote(ins, outs, sems):
            for cp in remote(ins, outs, sems, False):
                cp.start()

        def finish_remote(ins, outs, sems):
            for cp in remote(ins, outs, sems, True):
                cp.wait_recv()
            for cp in remote(ins, outs, sems, False):
                cp.wait_send()

        return _Comm(list(chip_sums) + list(landing), [SDS(a.shape, a.dtype) for a in landing],
                     [pltpu.SemaphoreType.DMA(())] * (2 * n * len(_ICI_RELS)), start_remote, finish_remote,
                     aliases={n + t: t for t in range(n)})

    def local(ins, outs, sems):
        x, y, _ = _mesh_pos()
        return [pltpu.make_async_copy(ins[t].at[2 * x + y], outs[t].at[2 * x + y], sems[2].at[t]) for t in range(n)]

    def start(ins, outs, sems):
        for cp in local(ins, outs, sems) + remote(ins, outs, sems, False):
            cp.start()

    def finish(ins, outs, sems):
        for cp in local(ins, outs, sems):
            cp.wait()
        for cp in remote(ins, outs, sems, True):
            cp.wait_recv()
        for cp in remote(ins, outs, sems, False):
            cp.wait_send()

    return _Comm(chip_sums, [SDS(a.shape, a.dtype) for a in chip_sums],
                 [pltpu.SemaphoreType.DMA((n, len(_ICI_RELS))), pltpu.SemaphoreType.DMA((n, len(_ICI_RELS))),
                  pltpu.SemaphoreType.DMA((n,))], start, finish)


def _chip_sum(slots, from_sibling, core, name):
    nq, _, r, c = slots.shape
    tr = _largest_tile(r, 1024, 16)

    def body(core_ref, a_ref, b_ref, o_ref):
        o_ref[...] = (a_ref[...].astype(F32) + b_ref[...].astype(F32)).astype(BF)

    return pl.pallas_call(
        body,
        grid_spec=pltpu.PrefetchScalarGridSpec(
            num_scalar_prefetch=1, grid=(nq, r // tr),
            in_specs=[pl.BlockSpec((None, None, tr, c), lambda q, i, core_ref: (q, core_ref[0], i, 0)),
                      pl.BlockSpec((None, tr, c), lambda q, i, core_ref: (q, i, 0))],
            out_specs=pl.BlockSpec((None, tr, c), lambda q, i, core_ref: (q, i, 0))),
        out_shape=SDS((nq, r, c), BF), compiler_params=pltpu.CompilerParams(vmem_limit_bytes=40 * MIB),
        name=name)(core, slots, from_sibling)


def _small_reduce(pack_g, meta_g, loss_scale, name, comm=None):
    w = pack_g.shape[2]

    def body(p_ref, m_ref, tot_ref, meta_ref, loss_ref):
        acc = p_ref[0]
        macc = m_ref[0]
        for k in range(1, N_DEV):
            acc = acc + p_ref[k]
            macc = macc + m_ref[k]
        tot = jnp.sum(acc, axis=0, keepdims=True)
        tot_ref[...] = tot
        meta_ref[...] = macc
        loss_ref[...] = jnp.full((1, LANES), loss_scale * jnp.sum(tot[:, w - LANES:w]), F32)

    return pl.pallas_call(
        body, out_shape=[SDS((1, w), F32), SDS(meta_g.shape[1:], F32), SDS((1, LANES), F32)],
        compiler_params=pltpu.CompilerParams(vmem_limit_bytes=32 * MIB), name=name)(pack_g, meta_g)


def _local_step(x, target, sw, plan):
    s_len, d = x.shape
    n_heads, n_meta = plan.n_heads, plan.n_meta
    l = n_meta + s_len
    lp = -(-l // LANES) * LANES
    tm = _largest_tile(lp, 544, 16)
    tq = _largest_tile(lp, 272, 16)
    te = _largest_tile(lp, 272, 16)
    tmd = _largest_tile(d, 512, LANES)

    plan.at("start")
    x, target = plan.gate((x, target))
    zmeta, zpad = jnp.zeros((n_meta, d), F32), jnp.zeros((lp - l, d), F32)
    h0 = jnp.concatenate([zmeta, x, zpad], axis=0)
    tpad = jnp.concatenate([zmeta, target, zpad], axis=0)
    plan.at("landed", (h0, tpad))
    h0 = lax.dynamic_update_slice(h0, plan.weights("meta"), (0, 0))

    wg1, wu1, wd1 = plan.weights("ffn1")
    fs = wg1.shape[1]
    h1, a1, b1, u1 = _ffn_fwd(h0, sw["ffn1_norm"], wg1, wu1, wd1, tm, "ffn1_fwd", plan.comm("ffn1_fwd"))
    plan.at("after_ffn1_fwd", (h1,))
    win, pw, wout = plan.weights("mix")
    nz = win.shape[1]
    p_w = sw["pool_scale"].shape[1]
    npb = p_w // LANES
    fblk = nz // LANES - 1
    tnz = _largest_tile(nz, 1408, LANES)
    qw, kw, bfp, ps = sw["q_norm"], sw["k_norm"], sw["b_forget"], sw["pool_scale"]
    z, u2 = _norm_matmul(h1, sw["mix_norm"], win, tm, tnz, "mix_in", plan.comm("mix_in"))
    cum = _fox_prep(z, bfp, fblk, "fox_prep")
    cumt = cum[:, :n_heads].T
    pool_o = _pool_fwd(z, pw, ps, "pool_fwd")
    att_o = _att_fwd(z, cum, cumt, qw, kw, n_heads, npb, tq, "att_fwd", plan.comm("att_fwd"))
    plan.at("after_att_fwd", (att_o,))
    h2 =_out_proj(h1, pool_o, att_o, wout, tm, "out_proj", plan.comm("out_proj"))
    wg2, wu2, wd2 = plan.weights("ffn2")
    h3, a2, b2, u3 = _ffn_fwd(h2, sw["ffn2_norm"], wg2, wu2, wd2, tm, "ffn2_fwd", plan.comm("ffn2_fwd"))
    dy, dob3, lsq = _loss_head(h3, tpad, n_meta, l, te, "loss_head")

    du3, da2, db2, hid2 = _ffn_bwd_dx(dob3, a2, b2, wg2, wu2, wd2, tm, "ffn2_bwd_dx", plan.comm("ffn2_bwd_dx"))
    dh2, dh2b, dn2 = _rms_bwd(du3, h2, sw["ffn2_norm"], dy, 1.0, te, "ffn2_rms_bwd")
    plan.grad("ffn2_w_gate", _matmul_tn(da2, u3, fs, d, "ffn2_dwg", plan.comm("ffn2_dwg")))
    plan.grad("ffn2_w_up", _matmul_tn(db2, u3, fs, d, "ffn2_dwu", plan.comm("ffn2_dwu")))
    plan.grad("ffn2_w_down", _matmul_tn(hid2, dob3, fs, d, "ffn2_dwd", plan.comm("ffn2_dwd")))
    plan.at("after_ffn2_dwd")

    dmix = _matmul_nt(dh2b, wout, tm, d, BF, "out_proj_bwd", plan.comm("out_proj_bwd"))
    plan.at("after_out_proj_bwd")
    tmp = _largest_tile(p_w, 512, LANES)
    plan.grad("w_out", jnp.concatenate([_matmul_tn(pool_o, dh2b, tmp, d, "dwout_pool"),
                                        _matmul_tn(att_o, dh2b, tmp, d, "dwout_att")], axis=0))
    dzp, dpw, dps = _pool_bwd(z, dmix, pw, ps, "pool_bwd")
    plan.grad("pool_w", dpw)
    plan.at("before_att_bwd")
    dq, dk, dv, dcq, dck, dqw, dkw = _att_bwd(z, cum, cumt, qw, kw, dmix, n_heads, npb, npb, tq, "att_bwd",
                                              plan.comm("att_bwd"))
    dcum = dcq[:, :, 0].T - dck[:, 0, :].T
    dcum = jnp.pad(dcum, ((0, 0), (0, LANES - n_heads)))
    dzf, dbf = _fox_bwd(z, bfp, dcum, fblk, "fox_bwd")
    dz = jnp.concatenate([dzp, dq, dk, dv, dzf], axis=1)
    plan.grad("w_in", _matmul_tn(u2, dz, tmd, tnz, "dwin", plan.comm("dwin")))
    du2 = _matmul_nt(dz, win, tm, tnz, F32, "mix_in_bwd", plan.comm("mix_in_bwd"))
    plan.at("before_ffn1_bwd_dx")
    dh1, dob1, dnm = _rms_bwd(du2, h1, sw["mix_norm"], dh2, 0.5, te, "mix_rms_bwd")

    du1, da1, db1, hid1 = _ffn_bwd_dx(dob1, a1, b1, wg1, wu1, wd1, tm, "ffn1_bwd_dx", plan.comm("ffn1_bwd_dx"))
    plan.grad("ffn1_w_gate", _matmul_tn(da1, u1, fs, d, "ffn1_dwg", plan.comm("ffn1_dwg")))
    plan.grad("ffn1_w_up", _matmul_tn(db1, u1, fs, d, "ffn1_dwu", plan.comm("ffn1_dwu")))
    plan.at("before_ffn1_dwd")
    plan.grad("ffn1_w_down", _matmul_tn(hid1, dob1, fs, d, "ffn1_dwd", plan.comm("ffn1_dwd")))
    plan.at("after_ffn1_dwd")
    dh0, _, dn1 = _rms_bwd(du1, h0, sw["ffn1_norm"], dh1, 1.0, te, "ffn1_rms_bwd", plan.comm("ffn1_rms_bwd"))
    plan.at("after_ffn1_rms_bwd")

    small = [dn1, dnm, dn2, dps, dqw, dkw, dbf, lsq]
    return dh0[n_meta:l], dh0[:n_meta], small


_BIG = ("ffn1_w_gate", "ffn1_w_up", "ffn1_w_down", "w_in", "pool_w", "w_out", "ffn2_w_gate", "ffn2_w_up", "ffn2_w_down")
_SMALL = ("ffn1_norm", "mix_norm", "ffn2_norm", "pool_scale", "q_norm", "k_norm", "b_forget")
_ORDER = ("meta_tokens", "ffn1_norm", "ffn1_w_gate", "ffn1_w_up", "ffn1_w_down", "mix_norm", "w_in", "b_forget",
          "q_norm", "k_norm", "pool_w", "pool_scale", "w_out", "ffn2_norm", "ffn2_w_gate", "ffn2_w_up", "ffn2_w_down")


_FFN1 = ("ffn1_w_gate", "ffn1_w_up", "ffn1_w_down")
_FFN2 = ("ffn2_w_gate", "ffn2_w_up", "ffn2_w_down")
_MIX = ("w_in", "pool_w", "w_out")

_RIDES = {
    "out_proj": (("g2", _FFN2),),
    "ffn2_dwu": (("s1", ("ffn2_w_gate",)),),
    "ffn2_dwd": (("s1", ("ffn2_w_up",)),),
    "out_proj_bwd": (("s1", ("ffn2_w_down",)),),
    "mix_in_bwd": (("s1", _MIX),),
    "ffn1_dwu": (("s1", ("ffn1_w_gate",)),),
    "ffn1_dwd": (("s1", ("ffn1_w_up",)),),
    "ffn1_rms_bwd": (("s1", ("ffn1_w_down",)),),
}
_G1_FFN1 = _FFN1 + ("meta_tokens",)
_POINTS = {
    "start": (("start", "g1", _G1_FFN1), ("gate", _MIX + _FFN2), ("prepare", "g1", _MIX),
              ("prepare", "g1", ("ffn2_w_down",)), ("prepare", "g1", ("ffn2_w_gate", "ffn2_w_up"))),
    "landed": (("wait", "g1", _G1_FFN1), ("start", "g1", _MIX), ("start", "g1", ("ffn2_w_down",)),
               ("alone", "g2", _G1_FFN1)),
    "after_ffn1_fwd": (("wait", "g1", _MIX), ("start", "g1", ("ffn2_w_gate", "ffn2_w_up")), ("alone", "g2", _MIX)),
    "after_att_fwd": (("wait", "g1", ("ffn2_w_down",)), ("wait", "g1", ("ffn2_w_gate", "ffn2_w_up"))),
    "after_ffn2_dwd": (("sum", ("ffn2_w_gate",)), ("start", "s2", ("ffn2_w_gate",))),
    "after_out_proj_bwd": (("sum", ("ffn2_w_up",)), ("start", "s2", ("ffn2_w_up",))),
    "before_att_bwd": (("sum", ("ffn2_w_down",)), ("start", "s2", ("ffn2_w_down",))),
    "before_ffn1_bwd_dx": (("sum", _MIX), ("start", "s2", _MIX)),
    "before_ffn1_dwd": (("sum", ("ffn1_w_gate",)), ("start", "s2", ("ffn1_w_gate",))),
    "after_ffn1_dwd": (("sum", ("ffn1_w_up",)), ("start", "s2", ("ffn1_w_up",))),
    "after_ffn1_rms_bwd": (("sum", ("ffn1_w_down",)), ("start", "s2", ("ffn1_w_down",))),
    "before_adamw_ffn2_w_gate": (("wait", "s2", ("ffn2_w_gate",)),),
    "before_adamw_ffn2_w_up": (("wait", "s2", ("ffn2_w_up",)),),
    "before_adamw_ffn2_w_down": (("wait", "s2", ("ffn2_w_down",)),),
    "before_adamw_w_in": (("wait", "s2", _MIX),),
    "before_adamw_ffn1_w_gate": (("wait", "s2", ("ffn1_w_gate",)),),
    "before_adamw_ffn1_w_up": (("wait", "s2", ("ffn1_w_up",)),),
    "before_adamw_ffn1_w_down": (("wait", "s2", ("ffn1_w_down",)),),
}


def _own_slot_filled(block, slot, n_slots):
    zone = lax.empty((n_slots,) + block.shape, block.dtype)
    return lax.dynamic_update_slice(zone, block[None], (slot,) + (0,) * block.ndim)


class _MeshPlan:
    def __init__(self, raw, pos, d, d_in, n_heads):
        self.raw, self.pos = dict(raw), pos
        self.core = pos[2].astype(jnp.int32).reshape(1)
        self.d, self.d_in, self.n_heads, self.n_meta = d, d_in, n_heads, raw["meta_tokens"].shape[0]
        self.partial, self.full, self.slots, self.from_sibling, self.chip_sum, self.received = {}, {}, {}, {}, {}, {}
        self.pending, self.prepared, self.started, self.tokens = [], {}, {}, []

    def gate(self, arrays):
        gated = lax.optimization_barrier((self.tokens[-1], tuple(arrays)))
        self.tokens[-1] = gated[0]
        return gated[1]

    def _phase(self, kind, names):
        src, dst, make = {"g2": (self.partial, self.full, _gather_fwd),
                          "s1": (self.slots, self.from_sibling, _scatter_sibling),
                          "s2": (self.chip_sum, self.received, _scatter_ici)}[kind]
        op = make([src[n] for n in names])
        self.pending.append((op, dst, names))
        return op

    def _settle(self):
        for op, dst, names in self.pending:
            dst.update(zip(names, op.results))
        self.pending = []

    def _prepare(self, kind, names):
        x, y, c = self.pos
        if kind == "g1":
            blocks = [self.raw[n] if n == "meta_tokens" else self.raw[n].astype(BF) for n in names]
            op = _gather_ici(blocks, [_own_slot_filled(b, 4 * x + 2 * y + c, N_DEV) for b in blocks])
        else:
            sums = [self.chip_sum[n] for n in names]
            mine = [lax.dynamic_index_in_dim(s, 2 * x + y, 0, keepdims=False) for s in sums]
            op = _scatter_ici(sums, [_own_slot_filled(b, 2 * x + y, N_DEV // 2) for b in mine])
        self.prepared[(kind, names)] = op

    def _start(self, kind, names):
        if (kind, names) not in self.prepared:
            self._prepare(kind, names)
        self._launch((kind, names), self.prepared.pop((kind, names)), "_".join(("start", kind, names[0])))

    def _launch(self, key, op, name):
        if self.tokens:
            op.arrs = list(self.gate(op.arrs))
        self.started[key], token = _split_start(op, name)
        self.tokens.append(token)

    def start_small_gather(self, arrays):
        x, y, c = self.pos
        zones = [_own_slot_filled(a, 4 * x + 2 * y + c, N_DEV) for a in arrays]
        self._launch("small", _gather_ici(list(arrays), zones, rels=tuple(range(1, N_DEV))), "start_gather_small")

    def wait_small_gather(self, afters):
        return _split_wait(self.started.pop("small"), afters, "wait_gather_small")

    def _wait(self, kind, names, afters):
        afters = list(afters) + [a for op in self.prepared.values() for a in op.arrs[len(op.arrs) // 2:]]
        landed = _split_wait(self.started.pop((kind, names)), afters, "_".join(("wait", kind, names[0])))
        (self.partial if kind == "g1" else self.received).update(zip(names, landed))

    def comm(self, kernel_name):
        self._settle()
        ops = [self._phase(kind, names) for kind, names in _RIDES.get(kernel_name, ())]
        if self.tokens:
            ops.append(_Comm(self.tokens, [], [], lambda *a: None, lambda *a: None))
            self.tokens = []
        return _merge_comm(ops)

    def at(self, point, after=()):
        for step in _POINTS.get(point, ()):
            self._settle()
            if step[0] == "alone":
                _comm_alone(self._phase(step[1], step[2]), "_".join((step[1], point)))
            elif step[0] == "start":
                self._start(step[1], step[2])
            elif step[0] == "prepare":
                self._prepare(step[1], step[2])
            elif step[0] == "gate":
                self.raw.update(zip(step[1], self.gate([self.raw[n] for n in step[1]])))
            elif step[0] == "wait":
                self._wait(step[1], step[2], tuple(after) + tuple(self.tokens[-1:]))
            else:
                for n in step[1]:
                    self.chip_sum[n] = _chip_sum(self.slots[n], self.from_sibling[n], self.core, "chip_sum_" + n)

    def weights(self, group):
        self._settle()
        f, d = self.full, self.d
        if group == "meta":
            g = f["meta_tokens"]
            return g.transpose(1, 0, 2).reshape(g.shape[1], d)
        if group == "ffn1":
            return tuple(f[n] for n in _FFN1)
        if group == "ffn2":
            return tuple(f[n] for n in _FFN2)
        n_main = self.d_in - self.n_heads
        win = f["w_in"].transpose(1, 0, 2).reshape(d, self.d_in)
        win = jnp.concatenate([win[:, :n_main], jnp.pad(win[:, n_main:], ((0, 0), (0, LANES - self.n_heads)))], axis=1)
        pw = f["pool_w"]
        gw = pw.shape[2]
        pw = pw.reshape(N_DEV, -1, gw // N_DEV, gw).transpose(1, 0, 2, 3).reshape(-1, gw, gw)
        return win, pw, f["w_out"].reshape(-1, d)

    def grad(self, name, g):
        d = self.d
        if name == "w_in":
            g = g[:, :self.d_in].reshape(d, N_DEV, -1).transpose(1, 0, 2)
        elif name == "pool_w":
            ng, gw = g.shape[0], g.shape[2]
            g = g.astype(BF).reshape(ng, N_DEV, -1, gw).transpose(1, 0, 2, 3).reshape(N_DEV, -1, gw)
        elif name == "w_out":
            g = g.reshape(N_DEV, -1, d)
        self.slots[name] = g.reshape((N_DEV // 2, 2) + g.shape[1:])

    def gradient_parts(self, name):
        self._settle()
        return self.received[name]


_TRANSPOSED = ("ffn1_w_gate", "ffn1_w_up", "ffn2_w_gate", "ffn2_w_up")


def _as2d(name, a):
    return a[0].T if name in _TRANSPOSED else a.reshape(-1, a.shape[-1])


def _from2d(name, a2d, shape):
    return a2d.T.reshape(shape) if name in _TRANSPOSED else a2d.reshape(shape)


def kernel(x, meta_tokens, ffn1_norm, ffn1_w_gate, ffn1_w_up, ffn1_w_down, mix_norm, w_in, b_forget, q_norm, k_norm, pool_w, pool_scale, w_out, ffn2_norm, ffn2_w_gate, ffn2_w_up, ffn2_w_down, loss_target, m_meta_tokens, m_ffn1_norm, m_ffn1_w_gate, m_ffn1_w_up, m_ffn1_w_down, m_mix_norm, m_w_in, m_b_forget, m_q_norm, m_k_norm, m_pool_w, m_pool_scale, m_w_out, m_ffn2_norm, m_ffn2_w_gate, m_ffn2_w_up, m_ffn2_w_down, v_meta_tokens, v_ffn1_norm, v_ffn1_w_gate, v_ffn1_w_up, v_ffn1_w_down, v_mix_norm, v_w_in, v_b_forget, v_q_norm, v_k_norm, v_pool_w, v_pool_scale, v_w_out, v_ffn2_norm, v_ffn2_w_gate, v_ffn2_w_up, v_ffn2_w_down):
    w = dict(meta_tokens=meta_tokens, ffn1_norm=ffn1_norm, ffn1_w_gate=ffn1_w_gate, ffn1_w_up=ffn1_w_up,
             ffn1_w_down=ffn1_w_down, mix_norm=mix_norm, w_in=w_in, b_forget=b_forget, q_norm=q_norm, k_norm=k_norm,
             pool_w=pool_w, pool_scale=pool_scale, w_out=w_out, ffn2_norm=ffn2_norm, ffn2_w_gate=ffn2_w_gate,
             ffn2_w_up=ffn2_w_up, ffn2_w_down=ffn2_w_down)
    m = dict(meta_tokens=m_meta_tokens, ffn1_norm=m_ffn1_norm, ffn1_w_gate=m_ffn1_w_gate, ffn1_w_up=m_ffn1_w_up,
             ffn1_w_down=m_ffn1_w_down, mix_norm=m_mix_norm, w_in=m_w_in, b_forget=m_b_forget, q_norm=m_q_norm,
             k_norm=m_k_norm, pool_w=m_pool_w, pool_scale=m_pool_scale, w_out=m_w_out, ffn2_norm=m_ffn2_norm,
             ffn2_w_gate=m_ffn2_w_gate, ffn2_w_up=m_ffn2_w_up, ffn2_w_down=m_ffn2_w_down)
    v = dict(meta_tokens=v_meta_tokens, ffn1_norm=v_ffn1_norm, ffn1_w_gate=v_ffn1_w_gate, ffn1_w_up=v_ffn1_w_up,
             ffn1_w_down=v_ffn1_w_down, mix_norm=v_mix_norm, w_in=v_w_in, b_forget=v_b_forget, q_norm=v_q_norm,
             k_norm=v_k_norm, pool_w=v_pool_w, pool_scale=v_pool_scale, w_out=v_w_out, ffn2_norm=v_ffn2_norm,
             ffn2_w_gate=v_ffn2_w_gate, ffn2_w_up=v_ffn2_w_up, ffn2_w_down=v_ffn2_w_down)

    d = x.shape[-1]
    n_heads = b_forget.shape[-1]
    pos = (lax.axis_index("x"), lax.axis_index("y"), lax.axis_index("c"))
    me = 4 * pos[0] + 2 * pos[1] + pos[2]

    raw = {k: _as2d(k, w[k]) for k in _BIG}
    raw["meta_tokens"] = meta_tokens
    plan = _MeshPlan(raw, pos, d, N_DEV * w_in.shape[-1], n_heads)
    sw = {k: w[k] for k in _SMALL}
    sw["b_forget"] = jnp.pad(b_forget, ((0, 0), (0, LANES - n_heads)))
    dx, dmeta, small = _local_step(x[0], loss_target[0], sw, plan)

    res = {}
    last = dx

    plan.start_small_gather([jnp.concatenate(small, axis=1), dmeta])

    def update_shards(names):
        nonlocal last
        for k in names:
            plan.at("before_adamw_" + k, (last,))
            res[k] = _adamw(plan.gradient_parts(k), _as2d(k, w[k]), _as2d(k, m[k]), _as2d(k, v[k]), "adamw_" + k,
                            plan.comm("adamw_" + k))
            last = res[k][0]

    update_shards(_FFN2 + _MIX)

    pack_g, meta_g = plan.wait_small_gather((last,))
    tot, dmeta_tot, loss_row = _small_reduce(pack_g, meta_g, 0.5 / d, "small_reduce")

    mcols = meta_tokens.shape[1]
    g_meta = lax.dynamic_slice_in_dim(dmeta_tot, me * mcols, mcols, axis=1)
    res["meta_tokens"] = _adamw(g_meta, meta_tokens, m_meta_tokens, v_meta_tokens, "adamw_meta_tokens")

    def packed(src):
        return jnp.concatenate([src[k] for k in _SMALL[:-1]] + [jnp.pad(src["b_forget"], ((0, 0), (0, LANES - n_heads)))],
                               axis=1)

    wp = packed(w)
    sm = _adamw(tot[:, :wp.shape[1]], wp, packed(m), packed(v), "adamw_small")
    off = 0
    for k in _SMALL:
        width = w[k].shape[1]
        res[k] = tuple(o[:, off:off + width] for o in sm)
        off += width if k != "b_forget" else LANES

    last = sm[0]
    update_shards(_FFN1)

    outs =[loss_row[0, 0], dx[None]]
    for idx in range(4):
        outs += [_from2d(k, res[k][idx], w[k].shape) for k in _ORDER]
    return tuple(outs)
```

```python
import functools

import jax
import jax.numpy as jnp
from jax import lax
from jax.experimental import pallas as pl
from jax.experimental.pallas import tpu as pltpu

F32 = jnp.float32
BF = jnp.bfloat16
SDS = jax.ShapeDtypeStruct

N_DEV = 8
LANES = 128
SUBLANES = 8
HEAD_DIM = 128
POOL_WINDOWS = (2, 4, 8, 16)
RMS_EPS = 1e-6
NEG_BIG = -1e30
MIB = 1024 * 1024

ADAM_LR = 0.001
ADAM_B1 = 0.9
ADAM_B2 = 0.999
ADAM_EPS = 1e-08
ADAM_WD = 0.01
ADAM_STEP = 10


class _Comm:
    def __init__(self, arrs, out_shape, sems, start, finish, aliases=None):
        self.arrs, self.out_shape, self.sems = list(arrs), list(out_shape), list(sems)
        self.start, self.finish, self.aliases = start, finish, dict(aliases or {})
        self.results = None


def _merge_comm(ops):
    ops = [op for op in ops if op is not None]
    if not ops:
        return None
    na, no, ns = [0], [0], [0]
    for op in ops:
        na.append(na[-1] + len(op.arrs))
        no.append(no[-1] + len(op.out_shape))
        ns.append(ns[-1] + len(op.sems))

    def parts(i, ins, outs, sems):
        return ins[na[i]:na[i + 1]], outs[no[i]:no[i + 1]], sems[ns[i]:ns[i + 1]]

    def start(ins, outs, sems):
        for i, op in enumerate(ops):
            op.start(*parts(i, ins, outs, sems))

    def finish(ins, outs, sems):
        for i, op in enumerate(ops):
            op.finish(*parts(i, ins, outs, sems))

    aliases = {}
    for i, op in enumerate(ops):
        for a, o in op.aliases.items():
            aliases[na[i] + a] = no[i] + o
    merged = _Comm([a for op in ops for a in op.arrs], [s for op in ops for s in op.out_shape],
                   [s for op in ops for s in op.sems], start, finish, aliases)
    merged.children = (ops, no)
    return merged


def _deliver(comm, results):
    comm.results = list(results)
    if hasattr(comm, "children"):
        ops, no = comm.children
        for i, op in enumerate(ops):
            _deliver(op, results[no[i]:no[i + 1]])


def _call(body, *, grid, in_specs, out_specs, out_shape, scratch_shapes=(), vmem_mib, name, comm=None):
    single = not isinstance(out_shape, (list, tuple))
    out_specs = [out_specs] if single else list(out_specs)
    out_shape = [out_shape] if single else list(out_shape)
    in_specs, scratch_shapes = list(in_specs), list(scratch_shapes)
    params = pltpu.CompilerParams(dimension_semantics=("arbitrary",) * len(grid), vmem_limit_bytes=vmem_mib * MIB)
    n_in, n_out, n_scr = len(in_specs), len(out_specs), len(scratch_shapes)

    def run(*args):
        if comm is None:
            res = pl.pallas_call(body, grid=grid, in_specs=in_specs, out_specs=out_specs, out_shape=out_shape,
                                 scratch_shapes=scratch_shapes, compiler_params=params, name=name)(*args)
            return res[0] if single else res
        ci, co = len(comm.arrs), len(comm.out_shape)

        def with_comm(*refs):
            ins, cins = refs[:n_in], refs[n_in:n_in + ci]
            o0 = n_in + ci
            outs, couts = refs[o0:o0 + n_out], refs[o0 + n_out:o0 + n_out + co]
            s0 = o0 + n_out + co
            scr, csems = refs[s0:s0 + n_scr], refs[s0 + n_scr:]
            ids = [pl.program_id(a) for a in range(len(grid))]
            first = functools.reduce(jnp.logical_and, [i == 0 for i in ids])
            last = functools.reduce(jnp.logical_and, [i == g - 1 for i, g in zip(ids, grid)])

            @pl.when(first)
            def _():
                comm.start(cins, couts, csems)

            body(*ins, *outs, *scr)

            @pl.when(last)
            def _():
                comm.finish(cins, couts, csems)

        anyspec = pl.BlockSpec(memory_space=pl.ANY)
        res = pl.pallas_call(
            with_comm, grid=grid, in_specs=in_specs + [anyspec] * ci, out_specs=out_specs + [anyspec] * co,
            out_shape=out_shape + comm.out_shape, scratch_shapes=scratch_shapes + comm.sems,
            input_output_aliases={n_in + a: n_out + o for a, o in comm.aliases.items()},
            compiler_params=params, name=name)(*args, *comm.arrs)
        _deliver(comm, res[n_out:])
        return res[0] if single else res[:n_out]

    return run


def _comm_alone(comm, name):
    def body(*refs):
        ci, co = len(comm.arrs), len(comm.out_shape)
        ins, outs, sems = refs[:ci], refs[ci:ci + co], refs[ci + co:]
        comm.start(ins, outs, sems)
        comm.finish(ins, outs, sems)

    anyspec = pl.BlockSpec(memory_space=pl.ANY)
    res = pl.pallas_call(
        body, in_specs=[anyspec] * len(comm.arrs), out_specs=[anyspec] * len(comm.out_shape),
        out_shape=comm.out_shape, scratch_shapes=comm.sems, input_output_aliases=comm.aliases, name=name)(*comm.arrs)
    _deliver(comm, res)


def _split_start(comm, name):
    na, ns = len(comm.arrs), len(comm.sems)

    def body(*refs):
        comm.start(refs[:na], None, refs[na:na + ns])
        token = refs[-1]
        token[...] = jnp.zeros_like(token)

    hbm = pl.BlockSpec(memory_space=pltpu.HBM)
    res = pl.pallas_call(
        body, name=name,
        out_shape=tuple(comm.sems) + tuple(pltpu.HBM(a.shape, a.dtype) for a in comm.arrs)
        + (SDS((SUBLANES, LANES), F32),),
        in_specs=[hbm] * na,
        out_specs=[pl.BlockSpec(memory_space=pltpu.SEMAPHORE)] * ns + [hbm] * na + [pl.BlockSpec(memory_space=pltpu.VMEM)],
        input_output_aliases={i: ns + i for i in range(na)},
        compiler_params=pltpu.CompilerParams(has_side_effects=pltpu.SideEffectType.DATAFLOW_SIDE_EFFECTING),
    )(*[pltpu.with_memory_space_constraint(a, pltpu.HBM) for a in comm.arrs])
    return (comm, res[:ns], res[ns:ns + na]), res[-1]


def _split_wait(started, afters, name):
    comm, sems, thru = started
    na, ns = len(thru), len(sems)
    afters = list(afters)

    def body(*refs):
        comm.finish(refs[:na], None, refs[na:na + ns])

    hbm = pl.BlockSpec(memory_space=pltpu.HBM)
    res = pl.pallas_call(
        body, name=name, out_shape=tuple(pltpu.HBM(a.shape, a.dtype) for a in thru),
        in_specs=[hbm] * na + [pl.BlockSpec(memory_space=pltpu.SEMAPHORE)] * ns
        + [pl.BlockSpec(memory_space=pl.ANY)] * len(afters),
        out_specs=[hbm] * na, input_output_aliases={i: i for i in range(na)},
        compiler_params=pltpu.CompilerParams(has_side_effects=pltpu.SideEffectType.DATAFLOW_SIDE_EFFECTING),
    )(*thru, *sems, *afters)
    return res[na - len(comm.out_shape):]


def _largest_tile(n, cap, mult):
    if n <= cap:
        return n
    best = None
    for t in range(mult, cap + 1, mult):
        if n % t == 0:
            best = t
    assert best is not None, (n, cap, mult)
    return best


def _dot(a, b):
    return jnp.dot(a, b, preferred_element_type=F32)


def _dot_nt(a, b):
    return lax.dot_general(a, b, (((1,), (1,)), ((), ())), preferred_element_type=F32)


def _dot_tn(a, b):
    return lax.dot_general(a, b, (((0,), (0,)), ((), ())), preferred_element_type=F32)


def _rows8(x):
    t, c = x.shape
    return jnp.sum(x.reshape(t // SUBLANES, SUBLANES, c), axis=0)


def _rstd(x):
    return lax.rsqrt(jnp.mean(x * x, axis=-1, keepdims=True) + RMS_EPS)


def _ffn_fwd(h, g, wg, wu, wd, tm, name, comm=None):
    lp, d = h.shape
    ns, fs, _ = wg.shape

    def body(h_ref, g_ref, wg_ref, wu_ref, wd_ref, out_ref, a_ref, b_ref, u_ref, acc_ref):
        j = pl.program_id(1)

        @pl.when(j == 0)
        def _():
            hh = h_ref[...]
            u_ref[...] = (hh * _rstd(hh) * g_ref[...]).astype(BF)
            acc_ref[...] = jnp.zeros_like(acc_ref)

        u = u_ref[...]
        a = _dot_nt(u, wg_ref[...])
        b = _dot_nt(u, wu_ref[...])
        a_ref[...] = a.astype(BF)
        b_ref[...] = b.astype(BF)
        hid = (a * jax.nn.sigmoid(a) * b).astype(BF)
        acc_ref[...] += _dot(hid, wd_ref[...])

        @pl.when(j == ns - 1)
        def _():
            out_ref[...] = h_ref[...] + 0.5 * acc_ref[...]

    row = pl.BlockSpec((tm, d), lambda i, j: (i, 0))
    act = pl.BlockSpec((None, tm, fs), lambda i, j: (j, i, 0))
    return _call(
        body, grid=(lp // tm, ns),
        in_specs=[row, pl.BlockSpec((1, d), lambda i, j: (0, 0)),
                  pl.BlockSpec((None, fs, d), lambda i, j: (j, 0, 0)),
                  pl.BlockSpec((None, fs, d), lambda i, j: (j, 0, 0)),
                  pl.BlockSpec((None, fs, d), lambda i, j: (j, 0, 0))],
        out_specs=[row, act, act, row],
        out_shape=[SDS((lp, d), F32), SDS((ns, lp, fs), BF), SDS((ns, lp, fs), BF), SDS((lp, d), BF)],
        scratch_shapes=[pltpu.VMEM((tm, d), F32)],
        vmem_mib=56, name=name, comm=comm)(h, g, wg, wu, wd)


def _ffn_bwd_dx(dob, a, b, wg, wu, wd, tm, name, comm=None):
    lp, d = dob.shape
    ns, fs, _ = wg.shape

    def body(do_ref, a_ref, b_ref, wg_ref, wu_ref, wd_ref, du_ref, da_ref, db_ref, hid_ref):
        j = pl.program_id(1)

        @pl.when(j == 0)
        def _():
            du_ref[...] = jnp.zeros_like(du_ref)

        dhid = _dot_nt(do_ref[...], wd_ref[...])
        av = a_ref[...].astype(F32)
        bv = b_ref[...].astype(F32)
        sig = jax.nn.sigmoid(av)
        sil = av * sig
        dbv = (dhid * sil).astype(BF)
        dav = (dhid * bv * (sig * (1.0 + av * (1.0 - sig)))).astype(BF)
        hid_ref[...] = (sil * bv).astype(BF)
        da_ref[...] = dav
        db_ref[...] = dbv
        du_ref[...] += _dot(dav, wg_ref[...]) + _dot(dbv, wu_ref[...])

    row = pl.BlockSpec((tm, d), lambda i, j: (i, 0))
    act = pl.BlockSpec((None, tm, fs), lambda i, j: (j, i, 0))
    return _call(
        body, grid=(lp // tm, ns),
        in_specs=[row, act, act,
                  pl.BlockSpec((None, fs, d), lambda i, j: (j, 0, 0)),
                  pl.BlockSpec((None, fs, d), lambda i, j: (j, 0, 0)),
                  pl.BlockSpec((None, fs, d), lambda i, j: (j, 0, 0))],
        out_specs=[row, act, act, act],
        out_shape=[SDS((lp, d), F32)] + [SDS((ns, lp, fs), BF)] * 3,
        vmem_mib=56, name=name, comm=comm)(dob, a, b, wg, wu, wd)


def _rms_bwd(du, h, g, dres, bscale, tm, name, comm=None):
    lp, d = h.shape

    def body(du_ref, h_ref, g_ref, dres_ref, dh_ref, dhb_ref, dg_ref):
        @pl.when(pl.program_id(0) == 0)
        def _():
            dg_ref[...] = jnp.zeros_like(dg_ref)

        hh = h_ref[...]
        r = _rstd(hh)
        xhat = hh * r
        duv = du_ref[...]
        dg_ref[...] += _rows8(duv * xhat)
        dxh = duv * g_ref[...]
        dh = dres_ref[...] + r * (dxh - xhat * jnp.mean(dxh * xhat, axis=-1, keepdims=True))
        dh_ref[...] = dh
        dhb_ref[...] = (bscale * dh).astype(BF)

    row = pl.BlockSpec((tm, d), lambda i: (i, 0))
    return _call(
        body, grid=(lp // tm,),
        in_specs=[row, row, pl.BlockSpec((1, d), lambda i: (0, 0)), row],
        out_specs=[row, row, pl.BlockSpec((SUBLANES, d), lambda i: (0, 0))],
        out_shape=[SDS((lp, d), F32), SDS((lp, d), BF), SDS((SUBLANES, d), F32)],
        vmem_mib=48, name=name, comm=comm)(du, h, g, dres)


def _matmul_tn(a, b, tm, tn, name, comm=None):
    a_b, b_b = a.ndim == 3, b.ndim == 3
    ns = a.shape[0] if a_b else (b.shape[0] if b_b else 1)
    l, m = a.shape[-2:]
    n = b.shape[-1]

    def body(a_ref, b_ref, o_ref):
        o_ref[...] = _dot_tn(a_ref[...], b_ref[...]).astype(o_ref.dtype)

    a_spec = (pl.BlockSpec((None, l, tm), lambda s, i, j: (s, 0, i)) if a_b
              else pl.BlockSpec((l, tm), lambda s, i, j: (0, i)))
    b_spec = (pl.BlockSpec((None, l, tn), lambda s, i, j: (s, 0, j)) if b_b
              else pl.BlockSpec((l, tn), lambda s, i, j: (0, j)))
    batched = a_b or b_b
    o_spec = (pl.BlockSpec((None, tm, tn), lambda s, i, j: (s, i, j)) if batched
              else pl.BlockSpec((tm, tn), lambda s, i, j: (i, j)))
    o_shape = SDS((ns, m, n), BF) if batched else SDS((m, n), BF)
    return _call(
        body, grid=(ns, m // tm, n // tn), in_specs=[a_spec, b_spec], out_specs=o_spec, out_shape=o_shape,
        vmem_mib=48, name=name, comm=comm)(a, b)


def _matmul_nt(x, w, tm, tk, out_dtype, name, comm=None):
    l, k = x.shape
    n = w.shape[0]
    nk = k // tk

    def body(x_ref, w_ref, o_ref, acc_ref):
        kk = pl.program_id(1)

        @pl.when(kk == 0)
        def _():
            acc_ref[...] = jnp.zeros_like(acc_ref)

        acc_ref[...] += _dot_nt(x_ref[...], w_ref[...])

        @pl.when(kk == nk - 1)
        def _():
            o_ref[...] = acc_ref[...].astype(o_ref.dtype)

    return _call(
        body, grid=(l // tm, nk),
        in_specs=[pl.BlockSpec((tm, tk), lambda i, kk: (i, kk)), pl.BlockSpec((n, tk), lambda i, kk: (0, kk))],
        out_specs=pl.BlockSpec((tm, n), lambda i, kk: (i, 0)),
        out_shape=SDS((l, n), out_dtype),
        scratch_shapes=[pltpu.VMEM((tm, n), F32)],
        vmem_mib=48, name=name, comm=comm)(x, w)


def _norm_matmul(h, g, w, tm, tn, name, comm=None):
    lp, d = h.shape
    n = w.shape[1]

    def body(h_ref, g_ref, w_ref, z_ref, u_ref):
        @pl.when(pl.program_id(1) == 0)
        def _():
            hh = h_ref[...]
            u_ref[...] = (hh * _rstd(hh) * g_ref[...]).astype(BF)

        z_ref[...] = _dot(u_ref[...], w_ref[...])

    row = pl.BlockSpec((tm, d), lambda i, j: (i, 0))
    return _call(
        body, grid=(lp // tm, n // tn),
        in_specs=[row, pl.BlockSpec((1, d), lambda i, j: (0, 0)), pl.BlockSpec((d, tn), lambda i, j: (0, j))],
        out_specs=[pl.BlockSpec((tm, tn), lambda i, j: (i, j)), row],
        out_shape=[SDS((lp, n), F32), SDS((lp, d), BF)],
        vmem_mib=48, name=name, comm=comm)(h, g, w)


def _out_proj(h, pool_o, att_o, w_out, tm, name, comm=None):
    lp, d = h.shape
    p = pool_o.shape[1]
    dm = w_out.shape[0]

    def body(h_ref, p_ref, a_ref, w_ref, o_ref):
        o_ref[...] = h_ref[...] + _dot(p_ref[...], w_ref[0:p, :]) + _dot(a_ref[...], w_ref[p:dm, :])

    row = pl.BlockSpec((tm, d), lambda i: (i, 0))
    return _call(
        body, grid=(lp // tm,),
        in_specs=[row, pl.BlockSpec((tm, p), lambda i: (i, 0)), pl.BlockSpec((tm, dm - p), lambda i: (i, 0)),
                  pl.BlockSpec((dm, d), lambda i: (0, 0))],
        out_specs=row, out_shape=SDS((lp, d), F32),
        vmem_mib=48, name=name, comm=comm)(h, pool_o, att_o, w_out)


def _loss_head(y, tpad, row0, row1, tm, name, comm=None):
    lp, d = y.shape

    def body(y_ref, t_ref, dy_ref, dob_ref, ls_ref):
        i = pl.program_id(0)

        @pl.when(i == 0)
        def _():
            ls_ref[...] = jnp.zeros_like(ls_ref)

        rows = i * tm + lax.broadcasted_iota(jnp.int32, (tm, d), 0)
        err = jnp.where((rows >= row0) & (rows < row1), y_ref[...] - t_ref[...], 0.0)
        dy = err * (1.0 / d)
        dy_ref[...] = dy
        dob_ref[...] = (0.5 * dy).astype(BF)
        sq = _rows8(err * err)
        acc = sq[:, 0:LANES]
        for c in range(1, d // LANES):
            acc = acc + sq[:, c * LANES:(c + 1) * LANES]
        ls_ref[...] += acc

    row = pl.BlockSpec((tm, d), lambda i: (i, 0))
    return _call(
        body, grid=(lp // tm,), in_specs=[row, row],
        out_specs=[row, row, pl.BlockSpec((SUBLANES, LANES), lambda i: (0, 0))],
        out_shape=[SDS((lp, d), F32), SDS((lp, d), BF), SDS((SUBLANES, LANES), F32)],
        vmem_mib=48, name=name, comm=comm)(y, tpad)


def _window_select(levels, gidx):
    out = levels[-1]
    for k in range(len(levels) - 2, -1, -1):
        out = jnp.where(gidx == k, levels[k], out)
    return out


def _pool_window_mean_minus_id(x, gidx):
    rows = lax.broadcasted_iota(jnp.int32, x.shape, 0)
    levels = []
    s = x
    shift = 1
    while shift < POOL_WINDOWS[-1]:
        s = s + jnp.where(rows >= shift, pltpu.roll(s, shift, 0), 0.0)
        shift *= 2
        if shift in POOL_WINDOWS:
            levels.append(s)
    win = _window_select(levels, gidx)
    cnt = jnp.minimum(rows + 1, _window_select(list(POOL_WINDOWS), gidx)).astype(F32)
    return win / cnt - x, cnt


def _pool_window_transpose(dy, cnt, gidx):
    lp = dy.shape[0]
    rows = lax.broadcasted_iota(jnp.int32, dy.shape, 0)
    levels = []
    s = dy / cnt
    shift = 1
    while shift < POOL_WINDOWS[-1]:
        s = s + jnp.where(rows < lp - shift, pltpu.roll(s, lp - shift, 0), 0.0)
        shift *= 2
        if shift in POOL_WINDOWS:
            levels.append(s)
    return _window_select(levels, gidx) - dy


def _pool_fwd(z, pool_w, pool_scale, name, comm=None):
    lp = z.shape[0]
    ng, gw, _ = pool_w.shape

    def body(p_ref, w_ref, s_ref, o_ref):
        pooled, _ = _pool_window_mean_minus_id(p_ref[...], pl.program_id(0))
        o_ref[...] = (_dot(pooled.astype(BF), w_ref[...]) * s_ref[...]).astype(BF)

    return _call(
        body, grid=(ng,),
        in_specs=[pl.BlockSpec((lp, gw), lambda g: (0, g)), pl.BlockSpec((None, gw, gw), lambda g: (g, 0, 0)),
                  pl.BlockSpec((1, gw), lambda g: (0, g))],
        out_specs=pl.BlockSpec((lp, gw), lambda g: (0, g)), out_shape=SDS((lp, ng * gw), BF),
        vmem_mib=48, name=name, comm=comm)(z, pool_w, pool_scale)


def _pool_bwd(z, dmix, pool_w, pool_scale, name, comm=None):
    lp = z.shape[0]
    ng, gw, _ = pool_w.shape

    def body(p_ref, d_ref, w_ref, s_ref, dz_ref, dw_ref, ds_ref):
        g = pl.program_id(0)
        pooled, cnt = _pool_window_mean_minus_id(p_ref[...], g)
        pooled_b = pooled.astype(BF)
        w = w_ref[...]
        mixed = _dot(pooled_b, w)
        dpo = d_ref[...].astype(F32)
        ds_ref[...] = _rows8(dpo * mixed)
        dmixed = (dpo * s_ref[...]).astype(BF)
        dw_ref[...] = _dot_tn(pooled_b, dmixed)
        dpooled = _dot_nt(dmixed, w)
        dz_ref[...] = _pool_window_transpose(dpooled, cnt, g).astype(BF)

    return _call(
        body, grid=(ng,),
        in_specs=[pl.BlockSpec((lp, gw), lambda g: (0, g)), pl.BlockSpec((lp, gw), lambda g: (0, g)),
                  pl.BlockSpec((None, gw, gw), lambda g: (g, 0, 0)), pl.BlockSpec((1, gw), lambda g: (0, g))],
        out_specs=[pl.BlockSpec((lp, gw), lambda g: (0, g)), pl.BlockSpec((None, gw, gw), lambda g: (g, 0, 0)),
                   pl.BlockSpec((SUBLANES, gw), lambda g: (0, g))],
        out_shape=[SDS((lp, ng * gw), BF), SDS((ng, gw, gw), F32), SDS((SUBLANES, ng * gw), F32)],
        vmem_mib=48, name=name, comm=comm)(z, dmix, pool_w, pool_scale)


def _log_sigmoid(x):
    return jnp.minimum(x, 0.0) - jnp.log(1.0 + jnp.exp(-jnp.abs(x)))


def _fox_prep(z, bfp, fblk, name, comm=None):
    lp = z.shape[0]
    nb = lp // LANES

    def body(f_ref, b_ref, cum_ref):
        r = lax.broadcasted_iota(jnp.int32, (LANES, LANES), 0)
        c = lax.broadcasted_iota(jnp.int32, (LANES, LANES), 1)
        tri = (r >= c).astype(F32)
        carry = jnp.zeros((1, LANES), F32)
        for blk in range(nb):
            sl = slice(blk * LANES, (blk + 1) * LANES)
            lf = _log_sigmoid(f_ref[sl, :] + b_ref[...])
            cb = jnp.dot(tri, lf, preferred_element_type=F32, precision=lax.Precision.HIGHEST) + carry
            cum_ref[sl, :] = cb
            carry = cb[LANES - 1:LANES, :]

    return _call(
        body, grid=(1,),
        in_specs=[pl.BlockSpec((lp, LANES), lambda i: (0, fblk)), pl.BlockSpec((1, LANES), lambda i: (0, 0))],
        out_specs=pl.BlockSpec((lp, LANES), lambda i: (0, 0)), out_shape=SDS((lp, LANES), F32),
        vmem_mib=32, name=name, comm=comm)(z, bfp)


def _fox_bwd(z, bfp, dcum, fblk, name, comm=None):
    lp = z.shape[0]
    nb = lp // LANES

    def body(f_ref, b_ref, dc_ref, dz_ref, db_ref):
        r = lax.broadcasted_iota(jnp.int32, (LANES, LANES), 0)
        c = lax.broadcasted_iota(jnp.int32, (LANES, LANES), 1)
        tri = (r <= c).astype(F32)
        carry = jnp.zeros((1, LANES), F32)
        acc = jnp.zeros((SUBLANES, LANES), F32)
        for blk in range(nb - 1, -1, -1):
            sl = slice(blk * LANES, (blk + 1) * LANES)
            dlf = jnp.dot(tri, dc_ref[sl, :], preferred_element_type=F32, precision=lax.Precision.HIGHEST) + carry
            carry = dlf[0:1, :]
            df = dlf * jax.nn.sigmoid(-(f_ref[sl, :] + b_ref[...]))
            dz_ref[sl, :] = df.astype(BF)
            acc = acc + _rows8(df)
        db_ref[...] = acc

    return _call(
        body, grid=(1,),
        in_specs=[pl.BlockSpec((lp, LANES), lambda i: (0, fblk)), pl.BlockSpec((1, LANES), lambda i: (0, 0)),
                  pl.BlockSpec((lp, LANES), lambda i: (0, 0))],
        out_specs=[pl.BlockSpec((lp, LANES), lambda i: (0, 0)), pl.BlockSpec((SUBLANES, LANES), lambda i: (0, 0))],
        out_shape=[SDS((lp, LANES), BF), SDS((SUBLANES, LANES), F32)],
        vmem_mib=32, name=name, comm=comm)(z, bfp, dcum)


def _att_scores(q_ref, cum_ref, cumt_ref, qw_ref, kn_s, h, i, tq, lk):
    scale = 1.0 / (HEAD_DIM ** 0.5)
    q = q_ref[...]
    rq = _rstd(q)
    qhat = q * rq
    qn = (qhat * qw_ref[...]).astype(BF)
    s = _dot_nt(qn, kn_s[0:lk, :]) * scale
    lane = lax.broadcasted_iota(jnp.int32, (tq, LANES), 1)
    cq = jnp.sum(jnp.where(lane == h, cum_ref[...], 0.0), axis=1, keepdims=True)
    ck = cumt_ref[pl.ds(h, 1), 0:lk]
    s = s + (cq - ck)
    qpos = i * tq + lax.broadcasted_iota(jnp.int32, (tq, lk), 0)
    kpos = lax.broadcasted_iota(jnp.int32, (tq, lk), 1)
    s = jnp.where(qpos >= kpos, s, NEG_BIG)
    e = jnp.exp(s - jnp.max(s, axis=1, keepdims=True))
    p = e / jnp.sum(e, axis=1, keepdims=True)
    return p, qn, qhat, rq


def _per_query_tile(i, nq, tq, lp, fn):
    for t in range(nq):
        lk = min(lp, -(-((t + 1) * tq) // LANES) * LANES)
        pl.when(i == t)(functools.partial(fn, lk))


def _att_fwd(z, cum, cumt, qw, kw, n_heads, qblk0, tq, name, comm=None):
    lp = z.shape[0]
    nh = n_heads

    def body(q_ref, k_ref, v_ref, cum_ref, cumt_ref, qw_ref, kw_ref, o_ref, kn_s, vb_s):
        h, i = pl.program_id(0), pl.program_id(1)

        @pl.when(i == 0)
        def _():
            k = k_ref[...]
            kn_s[...] = (k * _rstd(k) * kw_ref[...]).astype(BF)
            vb_s[...] = v_ref[...].astype(BF)

        def tile(lk):
            p, _, _, _ = _att_scores(q_ref, cum_ref, cumt_ref, qw_ref, kn_s, h, i, tq, lk)
            o_ref[...] = _dot(p.astype(BF), vb_s[0:lk, :]).astype(BF)

        _per_query_tile(i, lp // tq, tq, lp, tile)

    vec = pl.BlockSpec((1, HEAD_DIM), lambda h, i: (0, 0))
    return _call(
        body, grid=(nh, lp // tq),
        in_specs=[pl.BlockSpec((tq, HEAD_DIM), lambda h, i: (i, qblk0 + h)),
                  pl.BlockSpec((lp, HEAD_DIM), lambda h, i: (0, qblk0 + nh + h)),
                  pl.BlockSpec((lp, HEAD_DIM), lambda h, i: (0, qblk0 + 2 * nh + h)),
                  pl.BlockSpec((tq, LANES), lambda h, i: (i, 0)),
                  pl.BlockSpec((nh, lp), lambda h, i: (0, 0)), vec, vec],
        out_specs=pl.BlockSpec((tq, HEAD_DIM), lambda h, i: (i, h)),
        out_shape=SDS((lp, nh * HEAD_DIM), BF),
        scratch_shapes=[pltpu.VMEM((lp, HEAD_DIM), BF), pltpu.VMEM((lp, HEAD_DIM), BF)],
        vmem_mib=48, name=name, comm=comm)(z, z, z, cum, cumt, qw, kw)


def _att_bwd(z, cum, cumt, qw, kw, dmix, n_heads, qblk0, oblk0, tq, name, comm=None):
    lp = z.shape[0]
    nh = n_heads
    nq = lp // tq
    scale = 1.0 / (HEAD_DIM ** 0.5)

    def body(q_ref, k_ref, v_ref, cum_ref, cumt_ref, qw_ref, kw_ref, do_ref,
             dq_ref, dk_ref, dv_ref, dcq_ref, dck_ref, dqw_ref, dkw_ref,
             kn_s, vb_s, dkn_s, dv_s, dck_s):
        h, i = pl.program_id(0), pl.program_id(1)

        @pl.when((h == 0) & (i == 0))
        def _():
            dqw_ref[...] = jnp.zeros_like(dqw_ref)
            dkw_ref[...] = jnp.zeros_like(dkw_ref)

        @pl.when(i == 0)
        def _():
            k = k_ref[...]
            kn_s[...] = (k * _rstd(k) * kw_ref[...]).astype(BF)
            vb_s[...] = v_ref[...].astype(BF)
            dkn_s[...] = jnp.zeros_like(dkn_s)
            dv_s[...] = jnp.zeros_like(dv_s)
            dck_s[...] = jnp.zeros_like(dck_s)

        def tile(lk):
            p, qn, qhat, rq = _att_scores(q_ref, cum_ref, cumt_ref, qw_ref, kn_s, h, i, tq, lk)
            dob = do_ref[...]
            dp = _dot_nt(dob, vb_s[0:lk, :])
            ds = p * (dp - jnp.sum(p * dp, axis=1, keepdims=True))
            dsb = ds.astype(BF)
            dv_s[0:lk, :] += _dot_tn(p.astype(BF), dob)
            dkn_s[0:lk, :] += _dot_tn(dsb, qn)
            dcq_ref[...] = jnp.sum(ds, axis=1, keepdims=True)
            dck_s[:, 0:lk] += jnp.sum(ds, axis=0, keepdims=True)
            dqn = _dot(dsb, kn_s[0:lk, :]) * scale
            gq = dqn * qw_ref[...]
            dq_ref[...] = (rq * (gq - qhat * jnp.mean(gq * qhat, axis=-1, keepdims=True))).astype(BF)
            dqw_ref[...] += _rows8(dqn * qhat)

        _per_query_tile(i, nq, tq, lp, tile)

        @pl.when(i == nq - 1)
        def _():
            k = k_ref[...]
            rk = _rstd(k)
            khat = k * rk
            dkn = dkn_s[...] * scale
            gk = dkn * kw_ref[...]
            dk_ref[...] = (rk * (gk - khat * jnp.mean(gk * khat, axis=-1, keepdims=True))).astype(BF)
            dkw_ref[...] += _rows8(dkn * khat)
            dv_ref[...] = dv_s[...].astype(BF)
            dck_ref[...] = dck_s[...]

    vec = pl.BlockSpec((1, HEAD_DIM), lambda h, i: (0, 0))
    part = pl.BlockSpec((SUBLANES, LANES), lambda h, i: (0, 0))
    return _call(
        body, grid=(nh, nq),
        in_specs=[pl.BlockSpec((tq, HEAD_DIM), lambda h, i: (i, qblk0 + h)),
                  pl.BlockSpec((lp, HEAD_DIM), lambda h, i: (0, qblk0 + nh + h)),
                  pl.BlockSpec((lp, HEAD_DIM), lambda h, i: (0, qblk0 + 2 * nh + h)),
                  pl.BlockSpec((tq, LANES), lambda h, i: (i, 0)),
                  pl.BlockSpec((nh, lp), lambda h, i: (0, 0)), vec, vec,
                  pl.BlockSpec((tq, HEAD_DIM), lambda h, i: (i, oblk0 + h))],
        out_specs=[pl.BlockSpec((tq, HEAD_DIM), lambda h, i: (i, h)),
                   pl.BlockSpec((lp, HEAD_DIM), lambda h, i: (0, h)),
                   pl.BlockSpec((lp, HEAD_DIM), lambda h, i: (0, h)),
                   pl.BlockSpec((None, tq, 1), lambda h, i: (h, i, 0)),
                   pl.BlockSpec((None, 1, lp), lambda h, i: (h, 0, 0)),
                   part, part],
        out_shape=[SDS((lp, nh * HEAD_DIM), BF)] * 3
        + [SDS((nh, lp, 1), F32), SDS((nh, 1, lp), F32), SDS((SUBLANES, LANES), F32), SDS((SUBLANES, LANES), F32)],
        scratch_shapes=[pltpu.VMEM((lp, HEAD_DIM), BF), pltpu.VMEM((lp, HEAD_DIM), BF),
                        pltpu.VMEM((lp, HEAD_DIM), F32), pltpu.VMEM((lp, HEAD_DIM), F32),
                        pltpu.VMEM((1, lp), F32)],
        vmem_mib=56, name=name, comm=comm)(z, z, z, cum, cumt, qw, kw, dmix)


def _adamw_math(w, g, m, v):
    m2 = ADAM_B1 * m + (1.0 - ADAM_B1) * g
    v2 = ADAM_B2 * v + (1.0 - ADAM_B2) * (g * g)
    m_hat = m2 / (1.0 - ADAM_B1 ** ADAM_STEP)
    v_hat = v2 / (1.0 - ADAM_B2 ** ADAM_STEP)
    delta = -ADAM_LR * (m_hat / (jnp.sqrt(v_hat) + ADAM_EPS) + ADAM_WD * w)
    return delta, m2, v2


def _adamw(g_in, w, m, v, name, comm=None):
    r, c = w.shape
    partial_sum = g_in.ndim == 3
    lane_padded = -(-c // LANES) * LANES
    tr = _largest_tile(r, max(16, MIB // (4 * lane_padded) // 16 * 16), 16)

    def body(g_ref, w_ref, m_ref, v_ref, go_ref, d_ref, mo_ref, vo_ref):
        if partial_sum:
            g = g_ref[0].astype(F32)
            for k in range(1, g_in.shape[0]):
                g = g + g_ref[k].astype(F32)
        else:
            g = g_ref[...]
        delta, m2, v2 = _adamw_math(w_ref[...], g, m_ref[...], v_ref[...])
        go_ref[...] = g
        d_ref[...] = delta
        mo_ref[...] = m2
        vo_ref[...] = v2

    blk = pl.BlockSpec((tr, c), lambda i: (i, 0))
    g_spec = pl.BlockSpec((g_in.shape[0], tr, c), lambda i: (0, i, 0)) if partial_sum else blk
    return _call(
        body, grid=(r // tr,), in_specs=[g_spec, blk, blk, blk], out_specs=[blk] * 4,
        out_shape=[SDS((r, c), F32)] * 4, vmem_mib=40, name=name, comm=comm)(g_in, w, m, v)


def _peer(x, y, c, k):
    return (1 - x if k & 4 else x, 1 - y if k & 2 else y, 1 - c if k & 1 else c)


_SIBLING = 1
_ICI_RELS = (2, 4, 6)


def _mesh_pos():
    return lax.axis_index("x"), lax.axis_index("y"), lax.axis_index("c")


def _sem_pair(sems, t, j, n_rel, scalars):
    if scalars:
        i = 2 * (t * n_rel + j)
        return sems[i], sems[i + 1]
    return sems[0].at[t, j], sems[1].at[t, j]


def _dev(pos):
    return 4 * pos[0] + 2 * pos[1] + pos[2]


def _gather_ici(shards, landing=None, rels=(_SIBLING,) + _ICI_RELS):
    n = len(shards)

    def remote(ins, outs, sems, arrival):
        x, y, c = _mesh_pos()
        dst = ins[n:] if landing is not None else outs
        cps = []
        for j, k in enumerate(rels):
            peer = _peer(x, y, c, k)
            slot = _dev(peer) if arrival else _dev((x, y, c))
            for t in range(n):
                send_sem, recv_sem = _sem_pair(sems, t, j, len(rels), landing is not None)
                cps.append(pltpu.make_async_remote_copy(
                    src_ref=ins[t], dst_ref=dst[t].at[slot], send_sem=send_sem, recv_sem=recv_sem,
                    device_id=peer, device_id_type=pl.DeviceIdType.MESH))
        return cps

    if landing is not None:
        def start_remote(ins, outs, sems):
            for cp in remote(ins, outs, sems, False):
                cp.start()

        def finish_remote(ins, outs, sems):
            for cp in remote(ins, outs, sems, True):
                cp.wait_recv()
            for cp in remote(ins, outs, sems, False):
                cp.wait_send()

        return _Comm(list(shards) + list(landing), [SDS(a.shape, a.dtype) for a in landing],
                     [pltpu.SemaphoreType.DMA(())] * (2 * n * len(rels)),
                     start_remote, finish_remote, aliases={n + t: t for t in range(n)})

    def local(ins, outs, sems):
        me = _dev(_mesh_pos())
        return [pltpu.make_async_copy(ins[t], outs[t].at[me], sems[2].at[t]) for t in range(n)]

    def start(ins, outs, sems):
        for cp in local(ins, outs, sems) + remote(ins, outs, sems, False):
            cp.start()

    def finish(ins, outs, sems):
        for cp in local(ins, outs, sems):
            cp.wait()
        for cp in remote(ins, outs, sems, True):
            cp.wait_recv()
        for cp in remote(ins, outs, sems, False):
            cp.wait_send()

    return _Comm(shards, [SDS((N_DEV,) + s.shape, s.dtype) for s in shards],
                 [pltpu.SemaphoreType.DMA((n, len(rels))), pltpu.SemaphoreType.DMA((n, len(rels))),
                  pltpu.SemaphoreType.DMA((n,))], start, finish)


def _gather_fwd(partial):
    n = len(partial)

    def copies(ins, outs, sems, arrival):
        x, y, c = _mesh_pos()
        sibling = _peer(x, y, c, _SIBLING)
        cps = []
        for j, k in enumerate(_ICI_RELS):
            slot = _dev(_peer(x, y, c, k | _SIBLING if arrival else k))
            for t in range(n):
                cps.append(pltpu.make_async_remote_copy(
                    src_ref=ins[t].at[slot], dst_ref=outs[t].at[slot], send_sem=sems[0].at[t, j],
                    recv_sem=sems[1].at[t, j], device_id=sibling, device_id_type=pl.DeviceIdType.MESH))
        return cps

    def start(ins, outs, sems):
        for cp in copies(ins, outs, sems, False):
            cp.start()

    def finish(ins, outs, sems):
        for cp in copies(ins, outs, sems, True):
            cp.wait_recv()
        for cp in copies(ins, outs, sems, False):
            cp.wait_send()

    return _Comm(partial, [SDS(a.shape, a.dtype) for a in partial],
                 [pltpu.SemaphoreType.DMA((n, len(_ICI_RELS)))] * 2, start, finish,
                 aliases={t: t for t in range(n)})


def _scatter_sibling(slots):
    n = len(slots)

    def copies(ins, outs, sems):
        x, y, c = _mesh_pos()
        return [pltpu.make_async_remote_copy(
            src_ref=ins[t].at[:, 1 - c], dst_ref=outs[t], send_sem=sems[0].at[t], recv_sem=sems[1].at[t],
            device_id=_peer(x, y, c, _SIBLING), device_id_type=pl.DeviceIdType.MESH) for t in range(n)]

    def start(ins, outs, sems):
        for cp in copies(ins, outs, sems):
            cp.start()

    def finish(ins, outs, sems):
        for cp in copies(ins, outs, sems):
            cp.wait()

    return _Comm(slots, [SDS((s.shape[0],) + s.shape[2:], s.dtype) for s in slots],
                 [pltpu.SemaphoreType.DMA((n,))] * 2, start, finish)


def _scatter_ici(chip_sums, landing=None):
    n = len(chip_sums)

    def remote(ins, outs, sems, arrival):
        x, y, c = _mesh_pos()
        dst = ins[n:] if landing is not None else outs
        cps = []
        for j, k in enumerate(_ICI_RELS):
            peer = _peer(x, y, c, k)
            theirs, mine = 2 * peer[0] + peer[1], 2 * x + y
            for t in range(n):
                send_sem, recv_sem = _sem_pair(sems, t, j, len(_ICI_RELS), landing is not None)
                cps.append(pltpu.make_async_remote_copy(
                    src_ref=ins[t].at[theirs], dst_ref=dst[t].at[theirs if arrival else mine],
                    send_sem=send_sem, recv_sem=recv_sem,
                    device_id=peer, device_id_type=pl.DeviceIdType.MESH))
        return cps

    if landing is not None:
        def start_remote(ins, outs, sems):
            for cp in remote(ins, outs, sems, False):
                cp.start()

        def finish_remote(ins, outs, sems):
            for cp in remote(ins, outs, sems, True):
                cp.wait_recv()
            for cp in remote(ins, outs, sems, False):
                cp.wait_send()

        return _Comm(list(chip_sums) + list(landing), [SDS(a.shape, a.dtype) for a in landing],
                     [pltpu.SemaphoreType.DMA(())] * (2 * n * len(_ICI_RELS)), start_remote, finish_remote,
                     aliases={n + t: t for t in range(n)})

    def local(ins, outs, sems):
        x, y, _ = _mesh_pos()
        return [pltpu.make_async_copy(ins[t].at[2 * x + y], outs[t].at[2 * x + y], sems[2].at[t]) for t in range(n)]

    def start(ins, outs, sems):
        for cp in local(ins, outs, sems) + remote(ins, outs, sems, False):
            cp.start()

    def finish(ins, outs, sems):
        for cp in local(ins, outs, sems):
            cp.wait()
        for cp in remote(ins, outs, sems, True):
            cp.wait_recv()
        for cp in remote(ins, outs, sems, False):
            cp.wait_send()

    return _Comm(chip_sums, [SDS(a.shape, a.dtype) for a in chip_sums],
                 [pltpu.SemaphoreType.DMA((n, len(_ICI_RELS))), pltpu.SemaphoreType.DMA((n, len(_ICI_RELS))),
                  pltpu.SemaphoreType.DMA((n,))], start, finish)


def _chip_sum(slots, from_sibling, core, name):
    nq, _, r, c = slots.shape
    tr = _largest_tile(r, 1024, 16)

    def body(core_ref, a_ref, b_ref, o_ref):
        o_ref[...] = (a_ref[...].astype(F32) + b_ref[...].astype(F32)).astype(BF)

    return pl.pallas_call(
        body,
        grid_spec=pltpu.PrefetchScalarGridSpec(
            num_scalar_prefetch=1, grid=(nq, r // tr),
            in_specs=[pl.BlockSpec((None, None, tr, c), lambda q, i, core_ref: (q, core_ref[0], i, 0)),
                      pl.BlockSpec((None, tr, c), lambda q, i, core_ref: (q, i, 0))],
            out_specs=pl.BlockSpec((None, tr, c), lambda q, i, core_ref: (q, i, 0))),
        out_shape=SDS((nq, r, c), BF), compiler_params=pltpu.CompilerParams(vmem_limit_bytes=40 * MIB),
        name=name)(core, slots, from_sibling)


def _small_reduce(pack_g, meta_g, loss_scale, name, comm=None):
    w = pack_g.shape[2]

    def body(p_ref, m_ref, tot_ref, meta_ref, loss_ref):
        acc = p_ref[0]
        macc = m_ref[0]
        for k in range(1, N_DEV):
            acc = acc + p_ref[k]
            macc = macc + m_ref[k]
        tot = jnp.sum(acc, axis=0, keepdims=True)
        tot_ref[...] = tot
        meta_ref[...] = macc
        loss_ref[...] = jnp.full((1, LANES), loss_scale * jnp.sum(tot[:, w - LANES:w]), F32)

    return pl.pallas_call(
        body, out_shape=[SDS((1, w), F32), SDS(meta_g.shape[1:], F32), SDS((1, LANES), F32)],
        compiler_params=pltpu.CompilerParams(vmem_limit_bytes=32 * MIB), name=name)(pack_g, meta_g)


def _local_step(x, target, sw, plan):
    s_len, d = x.shape
    n_heads, n_meta = plan.n_heads, plan.n_meta
    l = n_meta + s_len
    lp = -(-l // LANES) * LANES
    tm = _largest_tile(lp, 544, 16)
    tq = _largest_tile(lp, 272, 16)
    te = _largest_tile(lp, 272, 16)
    tmd = _largest_tile(d, 512, LANES)

    plan.at("start")
    x, target = plan.gate((x, target))
    zmeta, zpad = jnp.zeros((n_meta, d), F32), jnp.zeros((lp - l, d), F32)
    h0 = jnp.concatenate([zmeta, x, zpad], axis=0)
    tpad = jnp.concatenate([zmeta, target, zpad], axis=0)
    plan.at("landed", (h0, tpad))
    h0 = lax.dynamic_update_slice(h0, plan.weights("meta"), (0, 0))

    wg1, wu1, wd1 = plan.weights("ffn1")
    fs = wg1.shape[1]
    h1, a1, b1, u1 = _ffn_fwd(h0, sw["ffn1_norm"], wg1, wu1, wd1, tm, "ffn1_fwd", plan.comm("ffn1_fwd"))
    plan.at("after_ffn1_fwd", (h1,))
    win, pw, wout = plan.weights("mix")
    nz = win.shape[1]
    p_w = sw["pool_scale"].shape[1]
    npb = p_w // LANES
    fblk = nz // LANES - 1
    tnz = _largest_tile(nz, 1408, LANES)
    qw, kw, bfp, ps = sw["q_norm"], sw["k_norm"], sw["b_forget"], sw["pool_scale"]
    z, u2 = _norm_matmul(h1, sw["mix_norm"], win, tm, tnz, "mix_in", plan.comm("mix_in"))
    cum = _fox_prep(z, bfp, fblk, "fox_prep")
    cumt = cum[:, :n_heads].T
    pool_o = _pool_fwd(z, pw, ps, "pool_fwd")
    att_o = _att_fwd(z, cum, cumt, qw, kw, n_heads, npb, tq, "att_fwd", plan.comm("att_fwd"))
    plan.at("after_att_fwd", (att_o,))
    h2 =_out_proj(h1, pool_o, att_o, wout, tm, "out_proj", plan.comm("out_proj"))
    wg2, wu2, wd2 = plan.weights("ffn2")
    h3, a2, b2, u3 = _ffn_fwd(h2, sw["ffn2_norm"], wg2, wu2, wd2, tm, "ffn2_fwd", plan.comm("ffn2_fwd"))
    dy, dob3, lsq = _loss_head(h3, tpad, n_meta, l, te, "loss_head")

    du3, da2, db2, hid2 = _ffn_bwd_dx(dob3, a2, b2, wg2, wu2, wd2, tm, "ffn2_bwd_dx", plan.comm("ffn2_bwd_dx"))
    dh2, dh2b, dn2 = _rms_bwd(du3, h2, sw["ffn2_norm"], dy, 1.0, te, "ffn2_rms_bwd")
    plan.grad("ffn2_w_gate", _matmul_tn(da2, u3, fs, d, "ffn2_dwg", plan.comm("ffn2_dwg")))
    plan.grad("ffn2_w_up", _matmul_tn(db2, u3, fs, d, "ffn2_dwu", plan.comm("ffn2_dwu")))
    plan.grad("ffn2_w_down", _matmul_tn(hid2, dob3, fs, d, "ffn2_dwd", plan.comm("ffn2_dwd")))
    plan.at("after_ffn2_dwd")

    dmix = _matmul_nt(dh2b, wout, tm, d, BF, "out_proj_bwd", plan.comm("out_proj_bwd"))
    plan.at("after_out_proj_bwd")
    tmp = _largest_tile(p_w, 512, LANES)
    plan.grad("w_out", jnp.concatenate([_matmul_tn(pool_o, dh2b, tmp, d, "dwout_pool"),
                                        _matmul_tn(att_o, dh2b, tmp, d, "dwout_att")], axis=0))
    dzp, dpw, dps = _pool_bwd(z, dmix, pw, ps, "pool_bwd")
    plan.grad("pool_w", dpw)
    plan.at("before_att_bwd")
    dq, dk, dv, dcq, dck, dqw, dkw = _att_bwd(z, cum, cumt, qw, kw, dmix, n_heads, npb, npb, tq, "att_bwd",
                                              plan.comm("att_bwd"))
    dcum = dcq[:, :, 0].T - dck[:, 0, :].T
    dcum = jnp.pad(dcum, ((0, 0), (0, LANES - n_heads)))
    dzf, dbf = _fox_bwd(z, bfp, dcum, fblk, "fox_bwd")
    dz = jnp.concatenate([dzp, dq, dk, dv, dzf], axis=1)
    plan.grad("w_in", _matmul_tn(u2, dz, tmd, tnz, "dwin", plan.comm("dwin")))
    du2 = _matmul_nt(dz, win, tm, tnz, F32, "mix_in_bwd", plan.comm("mix_in_bwd"))
    plan.at("before_ffn1_bwd_dx")
    dh1, dob1, dnm = _rms_bwd(du2, h1, sw["mix_norm"], dh2, 0.5, te, "mix_rms_bwd")

    du1, da1, db1, hid1 = _ffn_bwd_dx(dob1, a1, b1, wg1, wu1, wd1, tm, "ffn1_bwd_dx", plan.comm("ffn1_bwd_dx"))
    plan.grad("ffn1_w_gate", _matmul_tn(da1, u1, fs, d, "ffn1_dwg", plan.comm("ffn1_dwg")))
    plan.grad("ffn1_w_up", _matmul_tn(db1, u1, fs, d, "ffn1_dwu", plan.comm("ffn1_dwu")))
    plan.at("before_ffn1_dwd")
    plan.grad("ffn1_w_down", _matmul_tn(hid1, dob1, fs, d, "ffn1_dwd", plan.comm("ffn1_dwd")))
    plan.at("after_ffn1_dwd")
    dh0, _, dn1 = _rms_bwd(du1, h0, sw["ffn1_norm"], dh1, 1.0, te, "ffn1_rms_bwd", plan.comm("ffn1_rms_bwd"))

    small = [dn1, dnm, dn2, dps, dqw, dkw, dbf, lsq]
    return dh0[n_meta:l], dh0[:n_meta], small


_BIG = ("ffn1_w_gate", "ffn1_w_up", "ffn1_w_down", "w_in", "pool_w", "w_out", "ffn2_w_gate", "ffn2_w_up", "ffn2_w_down")
_SMALL = ("ffn1_norm", "mix_norm", "ffn2_norm", "pool_scale", "q_norm", "k_norm", "b_forget")
_ORDER = ("meta_tokens", "ffn1_norm", "ffn1_w_gate", "ffn1_w_up", "ffn1_w_down", "mix_norm", "w_in", "b_forget",
          "q_norm", "k_norm", "pool_w", "pool_scale", "w_out", "ffn2_norm", "ffn2_w_gate", "ffn2_w_up", "ffn2_w_down")


_FFN1 = ("ffn1_w_gate", "ffn1_w_up", "ffn1_w_down")
_FFN2 = ("ffn2_w_gate", "ffn2_w_up", "ffn2_w_down")
_MIX = ("w_in", "pool_w", "w_out")

_RIDES = {
    "out_proj": (("g2", _FFN2),),
    "ffn2_dwu": (("s1", ("ffn2_w_gate",)),),
    "ffn2_dwd": (("s1", ("ffn2_w_up",)),),
    "out_proj_bwd": (("s1", ("ffn2_w_down",)),),
    "mix_in_bwd": (("s1", _MIX),),
    "ffn1_dwu": (("s1", ("ffn1_w_gate",)),),
    "ffn1_dwd": (("s1", ("ffn1_w_up",)),),
    "ffn1_rms_bwd": (("s1", ("ffn1_w_down",)),),
}
_G1_FFN1 = _FFN1 + ("meta_tokens",)
_POINTS = {
    "start": (("start", "g1", _G1_FFN1), ("gate", _MIX + _FFN2), ("prepare", "g1", _MIX),
              ("prepare", "g1", ("ffn2_w_down",)), ("prepare", "g1", ("ffn2_w_gate", "ffn2_w_up"))),
    "landed": (("wait", "g1", _G1_FFN1), ("start", "g1", _MIX), ("start", "g1", ("ffn2_w_down",)),
               ("alone", "g2", _G1_FFN1)),
    "after_ffn1_fwd": (("wait", "g1", _MIX), ("start", "g1", ("ffn2_w_gate", "ffn2_w_up")), ("alone", "g2", _MIX)),
    "after_att_fwd": (("wait", "g1", ("ffn2_w_down",)), ("wait", "g1", ("ffn2_w_gate", "ffn2_w_up"))),
    "after_ffn2_dwd": (("sum", ("ffn2_w_gate",)), ("start", "s2", ("ffn2_w_gate",))),
    "after_out_proj_bwd": (("sum", ("ffn2_w_up",)), ("start", "s2", ("ffn2_w_up",))),
    "before_att_bwd": (("sum", ("ffn2_w_down",)), ("start", "s2", ("ffn2_w_down",))),
    "before_ffn1_bwd_dx": (("sum", _MIX), ("start", "s2", _MIX)),
    "before_ffn1_dwd": (("sum", ("ffn1_w_gate",)), ("start", "s2", ("ffn1_w_gate",))),
    "after_ffn1_dwd": (("sum", ("ffn1_w_up",)), ("start", "s2", ("ffn1_w_up",))),
    "after_ffn1_rms_bwd": (("sum", ("ffn1_w_down",)), ("start", "s2", ("ffn1_w_down",))),
    "before_adamw_ffn2_w_gate": (("wait", "s2", ("ffn2_w_gate",)),),
    "before_adamw_ffn2_w_up": (("wait", "s2", ("ffn2_w_up",)),),
    "before_adamw_ffn2_w_down": (("wait", "s2", ("ffn2_w_down",)),),
    "before_adamw_w_in": (("wait", "s2", _MIX),),
    "before_adamw_ffn1_w_gate": (("wait", "s2", ("ffn1_w_gate",)),),
    "before_adamw_ffn1_w_up": (("wait", "s2", ("ffn1_w_up",)),),
    "before_adamw_ffn1_w_down": (("wait", "s2", ("ffn1_w_down",)),),
}


def _own_slot_filled(block, slot, n_slots):
    zone = lax.empty((n_slots,) + block.shape, block.dtype)
    return lax.dynamic_update_slice(zone, block[None], (slot,) + (0,) * block.ndim)


class _MeshPlan:
    def __init__(self, raw, pos, d, d_in, n_heads):
        self.raw, self.pos = dict(raw), pos
        self.core = pos[2].astype(jnp.int32).reshape(1)
        self.d, self.d_in, self.n_heads, self.n_meta = d, d_in, n_heads, raw["meta_tokens"].shape[0]
        self.partial, self.full, self.slots, self.from_sibling, self.chip_sum, self.received = {}, {}, {}, {}, {}, {}
        self.pending, self.prepared, self.started, self.tokens = [], {}, {}, []

    def gate(self, arrays):
        gated = lax.optimization_barrier((self.tokens[-1], tuple(arrays)))
        self.tokens[-1] = gated[0]
        return gated[1]

    def _phase(self, kind, names):
        src, dst, make = {"g2": (self.partial, self.full, _gather_fwd),
                          "s1": (self.slots, self.from_sibling, _scatter_sibling),
                          "s2": (self.chip_sum, self.received, _scatter_ici)}[kind]
        op = make([src[n] for n in names])
        self.pending.append((op, dst, names))
        return op

    def _settle(self):
        for op, dst, names in self.pending:
            dst.update(zip(names, op.results))
        self.pending = []

    def _prepare(self, kind, names):
        x, y, c = self.pos
        if kind == "g1":
            blocks = [self.raw[n] if n == "meta_tokens" else self.raw[n].astype(BF) for n in names]
            op = _gather_ici(blocks, [_own_slot_filled(b, 4 * x + 2 * y + c, N_DEV) for b in blocks])
        else:
            sums = [self.chip_sum[n] for n in names]
            mine = [lax.dynamic_index_in_dim(s, 2 * x + y, 0, keepdims=False) for s in sums]
            op = _scatter_ici(sums, [_own_slot_filled(b, 2 * x + y, N_DEV // 2) for b in mine])
        self.prepared[(kind, names)] = op

    def _start(self, kind, names):
        if (kind, names) not in self.prepared:
            self._prepare(kind, names)
        self._launch((kind, names), self.prepared.pop((kind, names)), "_".join(("start", kind, names[0])))

    def _launch(self, key, op, name):
        if self.tokens:
            op.arrs = list(self.gate(op.arrs))
        self.started[key], token = _split_start(op, name)
        self.tokens.append(token)

    def start_small_gather(self, arrays):
        x, y, c = self.pos
        zones = [_own_slot_filled(a, 4 * x + 2 * y + c, N_DEV) for a in arrays]
        self._launch("small", _gather_ici(list(arrays), zones, rels=tuple(range(1, N_DEV))), "start_gather_small")

    def wait_small_gather(self, afters):
        return _split_wait(self.started.pop("small"), afters, "wait_gather_small")

    def _wait(self, kind, names, afters):
        afters = list(afters) + [a for op in self.prepared.values() for a in op.arrs[len(op.arrs) // 2:]]
        landed = _split_wait(self.started.pop((kind, names)), afters, "_".join(("wait", kind, names[0])))
        (self.partial if kind == "g1" else self.received).update(zip(names, landed))

    def comm(self, kernel_name):
        self._settle()
        ops = [self._phase(kind, names) for kind, names in _RIDES.get(kernel_name, ())]
        if self.tokens:
            ops.append(_Comm(self.tokens, [], [], lambda *a: None, lambda *a: None))
            self.tokens = []
        return _merge_comm(ops)

    def at(self, point, after=()):
        for step in _POINTS.get(point, ()):
            self._settle()
            if step[0] == "alone":
                _comm_alone(self._phase(step[1], step[2]), "_".join((step[1], point)))
            elif step[0] == "start":
                self._start(step[1], step[2])
            elif step[0] == "prepare":
                self._prepare(step[1], step[2])
            elif step[0] == "gate":
                self.raw.update(zip(step[1], self.gate([self.raw[n] for n in step[1]])))
            elif step[0] == "wait":
                self._wait(step[1], step[2], tuple(after) + tuple(self.tokens[-1:]))
            else:
                for n in step[1]:
                    self.chip_sum[n] = _chip_sum(self.slots[n], self.from_sibling[n], self.core, "chip_sum_" + n)

    def weights(self, group):
        self._settle()
        f, d = self.full, self.d
        if group == "meta":
            g = f["meta_tokens"]
            return g.transpose(1, 0, 2).reshape(g.shape[1], d)
        if group == "ffn1":
            return tuple(f[n] for n in _FFN1)
        if group == "ffn2":
            return tuple(f[n] for n in _FFN2)
        n_main = self.d_in - self.n_heads
        win = f["w_in"].transpose(1, 0, 2).reshape(d, self.d_in)
        win = jnp.concatenate([win[:, :n_main], jnp.pad(win[:, n_main:], ((0, 0), (0, LANES - self.n_heads)))], axis=1)
        pw = f["pool_w"]
        gw = pw.shape[2]
        pw = pw.reshape(N_DEV, -1, gw // N_DEV, gw).transpose(1, 0, 2, 3).reshape(-1, gw, gw)
        return win, pw, f["w_out"].reshape(-1, d)

    def grad(self, name, g):
        d = self.d
        if name == "w_in":
            g = g[:, :self.d_in].reshape(d, N_DEV, -1).transpose(1, 0, 2)
        elif name == "pool_w":
            ng, gw = g.shape[0], g.shape[2]
            g = g.astype(BF).reshape(ng, N_DEV, -1, gw).transpose(1, 0, 2, 3).reshape(N_DEV, -1, gw)
        elif name == "w_out":
            g = g.reshape(N_DEV, -1, d)
        self.slots[name] = g.reshape((N_DEV // 2, 2) + g.shape[1:])

    def gradient_parts(self, name):
        self._settle()
        return self.received[name]


_TRANSPOSED = ("ffn1_w_gate", "ffn1_w_up", "ffn2_w_gate", "ffn2_w_up")


def _as2d(name, a):
    return a[0].T if name in _TRANSPOSED else a.reshape(-1, a.shape[-1])


def _from2d(name, a2d, shape):
    return a2d.T.reshape(shape) if name in _TRANSPOSED else a2d.reshape(shape)


def kernel(x, meta_tokens, ffn1_norm, ffn1_w_gate, ffn1_w_up, ffn1_w_down, mix_norm, w_in, b_forget, q_norm, k_norm, pool_w, pool_scale, w_out, ffn2_norm, ffn2_w_gate, ffn2_w_up, ffn2_w_down, loss_target, m_meta_tokens, m_ffn1_norm, m_ffn1_w_gate, m_ffn1_w_up, m_ffn1_w_down, m_mix_norm, m_w_in, m_b_forget, m_q_norm, m_k_norm, m_pool_w, m_pool_scale, m_w_out, m_ffn2_norm, m_ffn2_w_gate, m_ffn2_w_up, m_ffn2_w_down, v_meta_tokens, v_ffn1_norm, v_ffn1_w_gate, v_ffn1_w_up, v_ffn1_w_down, v_mix_norm, v_w_in, v_b_forget, v_q_norm, v_k_norm, v_pool_w, v_pool_scale, v_w_out, v_ffn2_norm, v_ffn2_w_gate, v_ffn2_w_up, v_ffn2_w_down):
    w = dict(meta_tokens=meta_tokens, ffn1_norm=ffn1_norm, ffn1_w_gate=ffn1_w_gate, ffn1_w_up=ffn1_w_up,
             ffn1_w_down=ffn1_w_down, mix_norm=mix_norm, w_in=w_in, b_forget=b_forget, q_norm=q_norm, k_norm=k_norm,
             pool_w=pool_w, pool_scale=pool_scale, w_out=w_out, ffn2_norm=ffn2_norm, ffn2_w_gate=ffn2_w_gate,
             ffn2_w_up=ffn2_w_up, ffn2_w_down=ffn2_w_down)
    m = dict(meta_tokens=m_meta_tokens, ffn1_norm=m_ffn1_norm, ffn1_w_gate=m_ffn1_w_gate, ffn1_w_up=m_ffn1_w_up,
             ffn1_w_down=m_ffn1_w_down, mix_norm=m_mix_norm, w_in=m_w_in, b_forget=m_b_forget, q_norm=m_q_norm,
             k_norm=m_k_norm, pool_w=m_pool_w, pool_scale=m_pool_scale, w_out=m_w_out, ffn2_norm=m_ffn2_norm,
             ffn2_w_gate=m_ffn2_w_gate, ffn2_w_up=m_ffn2_w_up, ffn2_w_down=m_ffn2_w_down)
    v = dict(meta_tokens=v_meta_tokens, ffn1_norm=v_ffn1_norm, ffn1_w_gate=v_ffn1_w_gate, ffn1_w_up=v_ffn1_w_up,
             ffn1_w_down=v_ffn1_w_down, mix_norm=v_mix_norm, w_in=v_w_in, b_forget=v_b_forget, q_norm=v_q_norm,
             k_norm=v_k_norm, pool_w=v_pool_w, pool_scale=v_pool_scale, w_out=v_w_out, ffn2_norm=v_ffn2_norm,
             ffn2_w_gate=v_ffn2_w_gate, ffn2_w_up=v_ffn2_w_up, ffn2_w_down=v_ffn2_w_down)

    d = x.shape[-1]
    n_heads = b_forget.shape[-1]
    pos = (lax.axis_index("x"), lax.axis_index("y"), lax.axis_index("c"))
    me = 4 * pos[0] + 2 * pos[1] + pos[2]

    raw = {k: _as2d(k, w[k]) for k in _BIG}
    raw["meta_tokens"] = meta_tokens
    plan = _MeshPlan(raw, pos, d, N_DEV * w_in.shape[-1], n_heads)
    sw = {k: w[k] for k in _SMALL}
    sw["b_forget"] = jnp.pad(b_forget, ((0, 0), (0, LANES - n_heads)))
    dx, dmeta, small = _local_step(x[0], loss_target[0], sw, plan)

    res = {}
    last = dx

    plan.start_small_gather([jnp.concatenate(small, axis=1), dmeta])
    plan.at("after_ffn1_rms_bwd")

    def update_shards(names):
        nonlocal last
        for k in names:
            plan.at("before_adamw_" + k, (last,))
            res[k] = _adamw(plan.gradient_parts(k), _as2d(k, w[k]), _as2d(k, m[k]), _as2d(k, v[k]), "adamw_" + k,
                            plan.comm("adamw_" + k))
            last = res[k][0]

    update_shards(_FFN2 + _MIX + ("ffn1_w_gate", "ffn1_w_up"))

    pack_g, meta_g = plan.wait_small_gather((last,))
    tot, dmeta_tot, loss_row = _small_reduce(pack_g, meta_g, 0.5 / d, "small_reduce")

    mcols = meta_tokens.shape[1]
    g_meta = lax.dynamic_slice_in_dim(dmeta_tot, me * mcols, mcols, axis=1)
    res["meta_tokens"] = _adamw(g_meta, meta_tokens, m_meta_tokens, v_meta_tokens, "adamw_meta_tokens")

    def packed(src):
        return jnp.concatenate([src[k] for k in _SMALL[:-1]] + [jnp.pad(src["b_forget"], ((0, 0), (0, LANES - n_heads)))],
                               axis=1)

    wp = packed(w)
    sm = _adamw(tot[:, :wp.shape[1]], wp, packed(m), packed(v), "adamw_small")
    off = 0
    for k in _SMALL:
        width = w[k].shape[1]
        res[k] = tuple(o[:, off:off + width] for o in sm)
        off += width if k != "b_forget" else LANES

    last = sm[0]
    update_shards(("ffn1_w_down",))

    outs =[loss_row[0, 0], dx[None]]
    for idx in range(4):
        outs += [_from2d(k, res[k][idx], w[k].shape) for k in _ORDER]
    return tuple(outs)
```

```python
import functools

import jax
import jax.numpy as jnp
from jax import lax
from jax.experimental import pallas as pl
from jax.experimental.pallas import tpu as pltpu

F32 = jnp.float32
BF = jnp.bfloat16
SDS = jax.ShapeDtypeStruct

N_DEV = 8
LANES = 128
SUBLANES = 8
HEAD_DIM = 128
POOL_WINDOWS = (2, 4, 8, 16)
RMS_EPS = 1e-6
NEG_BIG = -1e30
MIB = 1024 * 1024

ADAM_LR = 0.001
ADAM_B1 = 0.9
ADAM_B2 = 0.999
ADAM_EPS = 1e-08
ADAM_WD = 0.01
ADAM_STEP = 10


class _Comm:
    def __init__(self, arrs, out_shape, sems, start, finish, aliases=None):
        self.arrs, self.out_shape, self.sems = list(arrs), list(out_shape), list(sems)
        self.start, self.finish, self.aliases = start, finish, dict(aliases or {})
        self.results = None


def _merge_comm(ops):
    ops = [op for op in ops if op is not None]
    if not ops:
        return None
    na, no, ns = [0], [0], [0]
    for op in ops:
        na.append(na[-1] + len(op.arrs))
        no.append(no[-1] + len(op.out_shape))
        ns.append(ns[-1] + len(op.sems))

    def parts(i, ins, outs, sems):
        return ins[na[i]:na[i + 1]], outs[no[i]:no[i + 1]], sems[ns[i]:ns[i + 1]]

    def start(ins, outs, sems):
        for i, op in enumerate(ops):
            op.start(*parts(i, ins, outs, sems))

    def finish(ins, outs, sems):
        for i, op in enumerate(ops):
            op.finish(*parts(i, ins, outs, sems))

    aliases = {}
    for i, op in enumerate(ops):
        for a, o in op.aliases.items():
            aliases[na[i] + a] = no[i] + o
    merged = _Comm([a for op in ops for a in op.arrs], [s for op in ops for s in op.out_shape],
                   [s for op in ops for s in op.sems], start, finish, aliases)
    merged.children = (ops, no)
    return merged


def _deliver(comm, results):
    comm.results = list(results)
    if hasattr(comm, "children"):
        ops, no = comm.children
        for i, op in enumerate(ops):
            _deliver(op, results[no[i]:no[i + 1]])


def _call(body, *, grid, in_specs, out_specs, out_shape, scratch_shapes=(), vmem_mib, name, comm=None):
    single = not isinstance(out_shape, (list, tuple))
    out_specs = [out_specs] if single else list(out_specs)
    out_shape = [out_shape] if single else list(out_shape)
    in_specs, scratch_shapes = list(in_specs), list(scratch_shapes)
    params = pltpu.CompilerParams(dimension_semantics=("arbitrary",) * len(grid), vmem_limit_bytes=vmem_mib * MIB)
    n_in, n_out, n_scr = len(in_specs), len(out_specs), len(scratch_shapes)

    def run(*args):
        if comm is None:
            res = pl.pallas_call(body, grid=grid, in_specs=in_specs, out_specs=out_specs, out_shape=out_shape,
                                 scratch_shapes=scratch_shapes, compiler_params=params, name=name)(*args)
            return res[0] if single else res
        ci, co = len(comm.arrs), len(comm.out_shape)

        def with_comm(*refs):
            ins, cins = refs[:n_in], refs[n_in:n_in + ci]
            o0 = n_in + ci
            outs, couts = refs[o0:o0 + n_out], refs[o0 + n_out:o0 + n_out + co]
            s0 = o0 + n_out + co
            scr, csems = refs[s0:s0 + n_scr], refs[s0 + n_scr:]
            ids = [pl.program_id(a) for a in range(len(grid))]
            first = functools.reduce(jnp.logical_and, [i == 0 for i in ids])
            last = functools.reduce(jnp.logical_and, [i == g - 1 for i, g in zip(ids, grid)])

            @pl.when(first)
            def _():
                comm.start(cins, couts, csems)

            body(*ins, *outs, *scr)

            @pl.when(last)
            def _():
                comm.finish(cins, couts, csems)

        anyspec = pl.BlockSpec(memory_space=pl.ANY)
        res = pl.pallas_call(
            with_comm, grid=grid, in_specs=in_specs + [anyspec] * ci, out_specs=out_specs + [anyspec] * co,
            out_shape=out_shape + comm.out_shape, scratch_shapes=scratch_shapes + comm.sems,
            input_output_aliases={n_in + a: n_out + o for a, o in comm.aliases.items()},
            compiler_params=params, name=name)(*args, *comm.arrs)
        _deliver(comm, res[n_out:])
        return res[0] if single else res[:n_out]

    return run


def _comm_alone(comm, name):
    def body(*refs):
        ci, co = len(comm.arrs), len(comm.out_shape)
        ins, outs, sems = refs[:ci], refs[ci:ci + co], refs[ci + co:]
        comm.start(ins, outs, sems)
        comm.finish(ins, outs, sems)

    anyspec = pl.BlockSpec(memory_space=pl.ANY)
    res = pl.pallas_call(
        body, in_specs=[anyspec] * len(comm.arrs), out_specs=[anyspec] * len(comm.out_shape),
        out_shape=comm.out_shape, scratch_shapes=comm.sems, input_output_aliases=comm.aliases, name=name)(*comm.arrs)
    _deliver(comm, res)


def _split_start(comm, name):
    na, ns = len(comm.arrs), len(comm.sems)

    def body(*refs):
        comm.start(refs[:na], None, refs[na:na + ns])
        token = refs[-1]
        token[...] = jnp.zeros_like(token)

    hbm = pl.BlockSpec(memory_space=pltpu.HBM)
    res = pl.pallas_call(
        body, name=name,
        out_shape=tuple(comm.sems) + tuple(pltpu.HBM(a.shape, a.dtype) for a in comm.arrs)
        + (SDS((SUBLANES, LANES), F32),),
        in_specs=[hbm] * na,
        out_specs=[pl.BlockSpec(memory_space=pltpu.SEMAPHORE)] * ns + [hbm] * na + [pl.BlockSpec(memory_space=pltpu.VMEM)],
        input_output_aliases={i: ns + i for i in range(na)},
        compiler_params=pltpu.CompilerParams(has_side_effects=pltpu.SideEffectType.DATAFLOW_SIDE_EFFECTING),
    )(*[pltpu.with_memory_space_constraint(a, pltpu.HBM) for a in comm.arrs])
    return (comm, res[:ns], res[ns:ns + na]), res[-1]


def _split_wait(started, afters, name):
    comm, sems, thru = started
    na, ns = len(thru), len(sems)
    afters = list(afters)

    def body(*refs):
        comm.finish(refs[:na], None, refs[na:na + ns])

    hbm = pl.BlockSpec(memory_space=pltpu.HBM)
    res = pl.pallas_call(
        body, name=name, out_shape=tuple(pltpu.HBM(a.shape, a.dtype) for a in thru),
        in_specs=[hbm] * na + [pl.BlockSpec(memory_space=pltpu.SEMAPHORE)] * ns
        + [pl.BlockSpec(memory_space=pl.ANY)] * len(afters),
        out_specs=[hbm] * na, input_output_aliases={i: i for i in range(na)},
        compiler_params=pltpu.CompilerParams(has_side_effects=pltpu.SideEffectType.DATAFLOW_SIDE_EFFECTING),
    )(*thru, *sems, *afters)
    return res[na - len(comm.out_shape):]


def _largest_tile(n, cap, mult):
    if n <= cap:
        return n
    best = None
    for t in range(mult, cap + 1, mult):
        if n % t == 0:
            best = t
    assert best is not None, (n, cap, mult)
    return best


def _dot(a, b):
    return jnp.dot(a, b, preferred_element_type=F32)


def _dot_nt(a, b):
    return lax.dot_general(a, b, (((1,), (1,)), ((), ())), preferred_element_type=F32)


def _dot_tn(a, b):
    return lax.dot_general(a, b, (((0,), (0,)), ((), ())), preferred_element_type=F32)


def _rows8(x):
    t, c = x.shape
    return jnp.sum(x.reshape(t // SUBLANES, SUBLANES, c), axis=0)


def _rstd(x):
    return lax.rsqrt(jnp.mean(x * x, axis=-1, keepdims=True) + RMS_EPS)


def _ffn_fwd(h, g, wg, wu, wd, tm, name, comm=None):
    lp, d = h.shape
    ns, fs, _ = wg.shape

    def body(h_ref, g_ref, wg_ref, wu_ref, wd_ref, out_ref, a_ref, b_ref, u_ref, acc_ref):
        j = pl.program_id(1)

        @pl.when(j == 0)
        def _():
            hh = h_ref[...]
            u_ref[...] = (hh * _rstd(hh) * g_ref[...]).astype(BF)
            acc_ref[...] = jnp.zeros_like(acc_ref)

        u = u_ref[...]
        a = _dot_nt(u, wg_ref[...])
        b = _dot_nt(u, wu_ref[...])
        a_ref[...] = a.astype(BF)
        b_ref[...] = b.astype(BF)
        hid = (a * jax.nn.sigmoid(a) * b).astype(BF)
        acc_ref[...] += _dot(hid, wd_ref[...])

        @pl.when(j == ns - 1)
        def _():
            out_ref[...] = h_ref[...] + 0.5 * acc_ref[...]

    row = pl.BlockSpec((tm, d), lambda i, j: (i, 0))
    act = pl.BlockSpec((None, tm, fs), lambda i, j: (j, i, 0))
    return _call(
        body, grid=(lp // tm, ns),
        in_specs=[row, pl.BlockSpec((1, d), lambda i, j: (0, 0)),
                  pl.BlockSpec((None, fs, d), lambda i, j: (j, 0, 0)),
                  pl.BlockSpec((None, fs, d), lambda i, j: (j, 0, 0)),
                  pl.BlockSpec((None, fs, d), lambda i, j: (j, 0, 0))],
        out_specs=[row, act, act, row],
        out_shape=[SDS((lp, d), F32), SDS((ns, lp, fs), BF), SDS((ns, lp, fs), BF), SDS((lp, d), BF)],
        scratch_shapes=[pltpu.VMEM((tm, d), F32)],
        vmem_mib=56, name=name, comm=comm)(h, g, wg, wu, wd)


def _ffn_bwd_dx(dob, a, b, wg, wu, wd, tm, name, comm=None):
    lp, d = dob.shape
    ns, fs, _ = wg.shape

    def body(do_ref, a_ref, b_ref, wg_ref, wu_ref, wd_ref, du_ref, da_ref, db_ref, hid_ref):
        j = pl.program_id(1)

        @pl.when(j == 0)
        def _():
            du_ref[...] = jnp.zeros_like(du_ref)

        dhid = _dot_nt(do_ref[...], wd_ref[...])
        av = a_ref[...].astype(F32)
        bv = b_ref[...].astype(F32)
        sig = jax.nn.sigmoid(av)
        sil = av * sig
        dbv = (dhid * sil).astype(BF)
        dav = (dhid * bv * (sig * (1.0 + av * (1.0 - sig)))).astype(BF)
        hid_ref[...] = (sil * bv).astype(BF)
        da_ref[...] = dav
        db_ref[...] = dbv
        du_ref[...] += _dot(dav, wg_ref[...]) + _dot(dbv, wu_ref[...])

    row = pl.BlockSpec((tm, d), lambda i, j: (i, 0))
    act = pl.BlockSpec((None, tm, fs), lambda i, j: (j, i, 0))
    return _call(
        body, grid=(lp // tm, ns),
        in_specs=[row, act, act,
                  pl.BlockSpec((None, fs, d), lambda i, j: (j, 0, 0)),
                  pl.BlockSpec((None, fs, d), lambda i, j: (j, 0, 0)),
                  pl.BlockSpec((None, fs, d), lambda i, j: (j, 0, 0))],
        out_specs=[row, act, act, act],
        out_shape=[SDS((lp, d), F32)] + [SDS((ns, lp, fs), BF)] * 3,
        vmem_mib=56, name=name, comm=comm)(dob, a, b, wg, wu, wd)


def _rms_bwd(du, h, g, dres, bscale, tm, name, comm=None):
    lp, d = h.shape

    def body(du_ref, h_ref, g_ref, dres_ref, dh_ref, dhb_ref, dg_ref):
        @pl.when(pl.program_id(0) == 0)
        def _():
            dg_ref[...] = jnp.zeros_like(dg_ref)

        hh = h_ref[...]
        r = _rstd(hh)
        xhat = hh * r
        duv = du_ref[...]
        dg_ref[...] += _rows8(duv * xhat)
        dxh = duv * g_ref[...]
        dh = dres_ref[...] + r * (dxh - xhat * jnp.mean(dxh * xhat, axis=-1, keepdims=True))
        dh_ref[...] = dh
        dhb_ref[...] = (bscale * dh).astype(BF)

    row = pl.BlockSpec((tm, d), lambda i: (i, 0))
    return _call(
        body, grid=(lp // tm,),
        in_specs=[row, row, pl.BlockSpec((1, d), lambda i: (0, 0)), row],
        out_specs=[row, row, pl.BlockSpec((SUBLANES, d), lambda i: (0, 0))],
        out_shape=[SDS((lp, d), F32), SDS((lp, d), BF), SDS((SUBLANES, d), F32)],
        vmem_mib=48, name=name, comm=comm)(du, h, g, dres)


def _matmul_tn(a, b, tm, tn, name, comm=None):
    a_b, b_b = a.ndim == 3, b.ndim == 3
    ns = a.shape[0] if a_b else (b.shape[0] if b_b else 1)
    l, m = a.shape[-2:]
    n = b.shape[-1]

    def body(a_ref, b_ref, o_ref):
        o_ref[...] = _dot_tn(a_ref[...], b_ref[...]).astype(o_ref.dtype)

    a_spec = (pl.BlockSpec((None, l, tm), lambda s, i, j: (s, 0, i)) if a_b
              else pl.BlockSpec((l, tm), lambda s, i, j: (0, i)))
    b_spec = (pl.BlockSpec((None, l, tn), lambda s, i, j: (s, 0, j)) if b_b
              else pl.BlockSpec((l, tn), lambda s, i, j: (0, j)))
    batched = a_b or b_b
    o_spec = (pl.BlockSpec((None, tm, tn), lambda s, i, j: (s, i, j)) if batched
              else pl.BlockSpec((tm, tn), lambda s, i, j: (i, j)))
    o_shape = SDS((ns, m, n), BF) if batched else SDS((m, n), BF)
    return _call(
        body, grid=(ns, m // tm, n // tn), in_specs=[a_spec, b_spec], out_specs=o_spec, out_shape=o_shape,
        vmem_mib=48, name=name, comm=comm)(a, b)


def _matmul_nt(x, w, tm, tk, out_dtype, name, comm=None):
    l, k = x.shape
    n = w.shape[0]
    nk = k // tk

    def body(x_ref, w_ref, o_ref, acc_ref):
        kk = pl.program_id(1)

        @pl.when(kk == 0)
        def _():
            acc_ref[...] = jnp.zeros_like(acc_ref)

        acc_ref[...] += _dot_nt(x_ref[...], w_ref[...])

        @pl.when(kk == nk - 1)
        def _():
            o_ref[...] = acc_ref[...].astype(o_ref.dtype)

    return _call(
        body, grid=(l // tm, nk),
        in_specs=[pl.BlockSpec((tm, tk), lambda i, kk: (i, kk)), pl.BlockSpec((n, tk), lambda i, kk: (0, kk))],
        out_specs=pl.BlockSpec((tm, n), lambda i, kk: (i, 0)),
        out_shape=SDS((l, n), out_dtype),
        scratch_shapes=[pltpu.VMEM((tm, n), F32)],
        vmem_mib=48, name=name, comm=comm)(x, w)


def _norm_matmul(h, g, w, tm, tn, name, comm=None):
    lp, d = h.shape
    n = w.shape[1]

    def body(h_ref, g_ref, w_ref, z_ref, u_ref):
        @pl.when(pl.program_id(1) == 0)
        def _():
            hh = h_ref[...]
            u_ref[...] = (hh * _rstd(hh) * g_ref[...]).astype(BF)

        z_ref[...] = _dot(u_ref[...], w_ref[...])

    row = pl.BlockSpec((tm, d), lambda i, j: (i, 0))
    return _call(
        body, grid=(lp // tm, n // tn),
        in_specs=[row, pl.BlockSpec((1, d), lambda i, j: (0, 0)), pl.BlockSpec((d, tn), lambda i, j: (0, j))],
        out_specs=[pl.BlockSpec((tm, tn), lambda i, j: (i, j)), row],
        out_shape=[SDS((lp, n), F32), SDS((lp, d), BF)],
        vmem_mib=48, name=name, comm=comm)(h, g, w)


def _out_proj(h, pool_o, att_o, w_out, tm, name, comm=None):
    lp, d = h.shape
    p = pool_o.shape[1]
    dm = w_out.shape[0]

    def body(h_ref, p_ref, a_ref, w_ref, o_ref):
        o_ref[...] = h_ref[...] + _dot(p_ref[...], w_ref[0:p, :]) + _dot(a_ref[...], w_ref[p:dm, :])

    row = pl.BlockSpec((tm, d), lambda i: (i, 0))
    return _call(
        body, grid=(lp // tm,),
        in_specs=[row, pl.BlockSpec((tm, p), lambda i: (i, 0)), pl.BlockSpec((tm, dm - p), lambda i: (i, 0)),
                  pl.BlockSpec((dm, d), lambda i: (0, 0))],
        out_specs=row, out_shape=SDS((lp, d), F32),
        vmem_mib=48, name=name, comm=comm)(h, pool_o, att_o, w_out)


def _loss_head(y, tpad, row0, row1, tm, name, comm=None):
    lp, d = y.shape

    def body(y_ref, t_ref, dy_ref, dob_ref, ls_ref):
        i = pl.program_id(0)

        @pl.when(i == 0)
        def _():
            ls_ref[...] = jnp.zeros_like(ls_ref)

        rows = i * tm + lax.broadcasted_iota(jnp.int32, (tm, d), 0)
        err = jnp.where((rows >= row0) & (rows < row1), y_ref[...] - t_ref[...], 0.0)
        dy = err * (1.0 / d)
        dy_ref[...] = dy
        dob_ref[...] = (0.5 * dy).astype(BF)
        sq = _rows8(err * err)
        acc = sq[:, 0:LANES]
        for c in range(1, d // LANES):
            acc = acc + sq[:, c * LANES:(c + 1) * LANES]
        ls_ref[...] += acc

    row = pl.BlockSpec((tm, d), lambda i: (i, 0))
    return _call(
        body, grid=(lp // tm,), in_specs=[row, row],
        out_specs=[row, row, pl.BlockSpec((SUBLANES, LANES), lambda i: (0, 0))],
        out_shape=[SDS((lp, d), F32), SDS((lp, d), BF), SDS((SUBLANES, LANES), F32)],
        vmem_mib=48, name=name, comm=comm)(y, tpad)


def _window_select(levels, gidx):
    out = levels[-1]
    for k in range(len(levels) - 2, -1, -1):
        out = jnp.where(gidx == k, levels[k], out)
    return out


def _pool_window_mean_minus_id(x, gidx):
    rows = lax.broadcasted_iota(jnp.int32, x.shape, 0)
    levels = []
    s = x
    shift = 1
    while shift < POOL_WINDOWS[-1]:
        s = s + jnp.where(rows >= shift, pltpu.roll(s, shift, 0), 0.0)
        shift *= 2
        if shift in POOL_WINDOWS:
            levels.append(s)
    win = _window_select(levels, gidx)
    cnt = jnp.minimum(rows + 1, _window_select(list(POOL_WINDOWS), gidx)).astype(F32)
    return win / cnt - x, cnt


def _pool_window_transpose(dy, cnt, gidx):
    lp = dy.shape[0]
    rows = lax.broadcasted_iota(jnp.int32, dy.shape, 0)
    levels = []
    s = dy / cnt
    shift = 1
    while shift < POOL_WINDOWS[-1]:
        s = s + jnp.where(rows < lp - shift, pltpu.roll(s, lp - shift, 0), 0.0)
        shift *= 2
        if shift in POOL_WINDOWS:
            levels.append(s)
    return _window_select(levels, gidx) - dy


def _pool_fwd(z, pool_w, pool_scale, name, comm=None):
    lp = z.shape[0]
    ng, gw, _ = pool_w.shape

    def body(p_ref, w_ref, s_ref, o_ref):
        pooled, _ = _pool_window_mean_minus_id(p_ref[...], pl.program_id(0))
        o_ref[...] = (_dot(pooled.astype(BF), w_ref[...]) * s_ref[...]).astype(BF)

    return _call(
        body, grid=(ng,),
        in_specs=[pl.BlockSpec((lp, gw), lambda g: (0, g)), pl.BlockSpec((None, gw, gw), lambda g: (g, 0, 0)),
                  pl.BlockSpec((1, gw), lambda g: (0, g))],
        out_specs=pl.BlockSpec((lp, gw), lambda g: (0, g)), out_shape=SDS((lp, ng * gw), BF),
        vmem_mib=48, name=name, comm=comm)(z, pool_w, pool_scale)


def _pool_bwd(z, dmix, pool_w, pool_scale, name, comm=None):
    lp = z.shape[0]
    ng, gw, _ = pool_w.shape

    def body(p_ref, d_ref, w_ref, s_ref, dz_ref, dw_ref, ds_ref):
        g = pl.program_id(0)
        pooled, cnt = _pool_window_mean_minus_id(p_ref[...], g)
        pooled_b = pooled.astype(BF)
        w = w_ref[...]
        mixed = _dot(pooled_b, w)
        dpo = d_ref[...].astype(F32)
        ds_ref[...] = _rows8(dpo * mixed)
        dmixed = (dpo * s_ref[...]).astype(BF)
        dw_ref[...] = _dot_tn(pooled_b, dmixed)
        dpooled = _dot_nt(dmixed, w)
        dz_ref[...] = _pool_window_transpose(dpooled, cnt, g).astype(BF)

    return _call(
        body, grid=(ng,),
        in_specs=[pl.BlockSpec((lp, gw), lambda g: (0, g)), pl.BlockSpec((lp, gw), lambda g: (0, g)),
                  pl.BlockSpec((None, gw, gw), lambda g: (g, 0, 0)), pl.BlockSpec((1, gw), lambda g: (0, g))],
        out_specs=[pl.BlockSpec((lp, gw), lambda g: (0, g)), pl.BlockSpec((None, gw, gw), lambda g: (g, 0, 0)),
                   pl.BlockSpec((SUBLANES, gw), lambda g: (0, g))],
        out_shape=[SDS((lp, ng * gw), BF), SDS((ng, gw, gw), F32), SDS((SUBLANES, ng * gw), F32)],
        vmem_mib=48, name=name, comm=comm)(z, dmix, pool_w, pool_scale)


def _log_sigmoid(x):
    return jnp.minimum(x, 0.0) - jnp.log(1.0 + jnp.exp(-jnp.abs(x)))


def _fox_prep(z, bfp, fblk, name, comm=None):
    lp = z.shape[0]
    nb = lp // LANES

    def body(f_ref, b_ref, cum_ref):
        r = lax.broadcasted_iota(jnp.int32, (LANES, LANES), 0)
        c = lax.broadcasted_iota(jnp.int32, (LANES, LANES), 1)
        tri = (r >= c).astype(F32)
        carry = jnp.zeros((1, LANES), F32)
        for blk in range(nb):
            sl = slice(blk * LANES, (blk + 1) * LANES)
            lf = _log_sigmoid(f_ref[sl, :] + b_ref[...])
            cb = jnp.dot(tri, lf, preferred_element_type=F32, precision=lax.Precision.HIGHEST) + carry
            cum_ref[sl, :] = cb
            carry = cb[LANES - 1:LANES, :]

    return _call(
        body, grid=(1,),
        in_specs=[pl.BlockSpec((lp, LANES), lambda i: (0, fblk)), pl.BlockSpec((1, LANES), lambda i: (0, 0))],
        out_specs=pl.BlockSpec((lp, LANES), lambda i: (0, 0)), out_shape=SDS((lp, LANES), F32),
        vmem_mib=32, name=name, comm=comm)(z, bfp)


def _fox_bwd(z, bfp, dcum, fblk, name, comm=None):
    lp = z.shape[0]
    nb = lp // LANES

    def body(f_ref, b_ref, dc_ref, dz_ref, db_ref):
        r = lax.broadcasted_iota(jnp.int32, (LANES, LANES), 0)
        c = lax.broadcasted_iota(jnp.int32, (LANES, LANES), 1)
        tri = (r <= c).astype(F32)
        carry = jnp.zeros((1, LANES), F32)
        acc = jnp.zeros((SUBLANES, LANES), F32)
        for blk in range(nb - 1, -1, -1):
            sl = slice(blk * LANES, (blk + 1) * LANES)
            dlf = jnp.dot(tri, dc_ref[sl, :], preferred_element_type=F32, precision=lax.Precision.HIGHEST) + carry
            carry = dlf[0:1, :]
            df = dlf * jax.nn.sigmoid(-(f_ref[sl, :] + b_ref[...]))
            dz_ref[sl, :] = df.astype(BF)
            acc = acc + _rows8(df)
        db_ref[...] = acc

    return _call(
        body, grid=(1,),
        in_specs=[pl.BlockSpec((lp, LANES), lambda i: (0, fblk)), pl.BlockSpec((1, LANES), lambda i: (0, 0)),
                  pl.BlockSpec((lp, LANES), lambda i: (0, 0))],
        out_specs=[pl.BlockSpec((lp, LANES), lambda i: (0, 0)), pl.BlockSpec((SUBLANES, LANES), lambda i: (0, 0))],
        out_shape=[SDS((lp, LANES), BF), SDS((SUBLANES, LANES), F32)],
        vmem_mib=32, name=name, comm=comm)(z, bfp, dcum)


def _att_scores(q_ref, cum_ref, cumt_ref, qw_ref, kn_s, h, i, tq, lk):
    scale = 1.0 / (HEAD_DIM ** 0.5)
    q = q_ref[...]
    rq = _rstd(q)
    qhat = q * rq
    qn = (qhat * qw_ref[...]).astype(BF)
    s = _dot_nt(qn, kn_s[0:lk, :]) * scale
    lane = lax.broadcasted_iota(jnp.int32, (tq, LANES), 1)
    cq = jnp.sum(jnp.where(lane == h, cum_ref[...], 0.0), axis=1, keepdims=True)
    ck = cumt_ref[pl.ds(h, 1), 0:lk]
    s = s + (cq - ck)
    qpos = i * tq + lax.broadcasted_iota(jnp.int32, (tq, lk), 0)
    kpos = lax.broadcasted_iota(jnp.int32, (tq, lk), 1)
    s = jnp.where(qpos >= kpos, s, NEG_BIG)
    e = jnp.exp(s - jnp.max(s, axis=1, keepdims=True))
    p = e * (1.0 / jnp.sum(e, axis=1, keepdims=True))
    return p, qn, qhat, rq


def _per_query_tile(i, nq, tq, lp, fn):
    for t in range(nq):
        lk = min(lp, -(-((t + 1) * tq) // LANES) * LANES)
        pl.when(i == t)(functools.partial(fn, lk))


def _att_fwd(z, cum, cumt, qw, kw, n_heads, qblk0, tq, name, comm=None):
    lp = z.shape[0]
    nh = n_heads

    def body(q_ref, k_ref, v_ref, cum_ref, cumt_ref, qw_ref, kw_ref, o_ref, kn_s, vb_s):
        h, i = pl.program_id(0), pl.program_id(1)

        @pl.when(i == 0)
        def _():
            k = k_ref[...]
            kn_s[...] = (k * _rstd(k) * kw_ref[...]).astype(BF)
            vb_s[...] = v_ref[...].astype(BF)

        def tile(lk):
            p, _, _, _ = _att_scores(q_ref, cum_ref, cumt_ref, qw_ref, kn_s, h, i, tq, lk)
            o_ref[...] = _dot(p.astype(BF), vb_s[0:lk, :]).astype(BF)

        _per_query_tile(i, lp // tq, tq, lp, tile)

    vec = pl.BlockSpec((1, HEAD_DIM), lambda h, i: (0, 0))
    return _call(
        body, grid=(nh, lp // tq),
        in_specs=[pl.BlockSpec((tq, HEAD_DIM), lambda h, i: (i, qblk0 + h)),
                  pl.BlockSpec((lp, HEAD_DIM), lambda h, i: (0, qblk0 + nh + h)),
                  pl.BlockSpec((lp, HEAD_DIM), lambda h, i: (0, qblk0 + 2 * nh + h)),
                  pl.BlockSpec((tq, LANES), lambda h, i: (i, 0)),
                  pl.BlockSpec((nh, lp), lambda h, i: (0, 0)), vec, vec],
        out_specs=pl.BlockSpec((tq, HEAD_DIM), lambda h, i: (i, h)),
        out_shape=SDS((lp, nh * HEAD_DIM), BF),
        scratch_shapes=[pltpu.VMEM((lp, HEAD_DIM), BF), pltpu.VMEM((lp, HEAD_DIM), BF)],
        vmem_mib=48, name=name, comm=comm)(z, z, z, cum, cumt, qw, kw)


def _att_bwd(z, cum, cumt, qw, kw, dmix, n_heads, qblk0, oblk0, tq, name, comm=None):
    lp = z.shape[0]
    nh = n_heads
    nq = lp // tq
    scale = 1.0 / (HEAD_DIM ** 0.5)

    def body(q_ref, k_ref, v_ref, cum_ref, cumt_ref, qw_ref, kw_ref, do_ref,
             dq_ref, dk_ref, dv_ref, dck_ref, dqw_ref, dkw_ref,
             kn_s, vb_s, dkn_s, dv_s, dck_s):
        h, i = pl.program_id(0), pl.program_id(1)

        @pl.when((h == 0) & (i == 0))
        def _():
            dqw_ref[...] = jnp.zeros_like(dqw_ref)
            dkw_ref[...] = jnp.zeros_like(dkw_ref)

        @pl.when(i == 0)
        def _():
            k = k_ref[...]
            kn_s[...] = (k * _rstd(k) * kw_ref[...]).astype(BF)
            vb_s[...] = v_ref[...].astype(BF)
            dkn_s[...] = jnp.zeros_like(dkn_s)
            dv_s[...] = jnp.zeros_like(dv_s)
            dck_s[...] = jnp.zeros_like(dck_s)

        def tile(lk):
            p, qn, qhat, rq = _att_scores(q_ref, cum_ref, cumt_ref, qw_ref, kn_s, h, i, tq, lk)
            dob = do_ref[...]
            dp = _dot_nt(dob, vb_s[0:lk, :])
            ds = p * (dp - jnp.sum(p * dp, axis=1, keepdims=True))
            dsb = ds.astype(BF)
            dv_s[0:lk, :] += _dot_tn(p.astype(BF), dob)
            dkn_s[0:lk, :] += _dot_tn(dsb, qn)
            dck_s[:, 0:lk] += jnp.sum(ds, axis=0, keepdims=True)
            dqn = _dot(dsb, kn_s[0:lk, :]) * scale
            gq = dqn * qw_ref[...]
            dq_ref[...] = (rq * (gq - qhat * jnp.mean(gq * qhat, axis=-1, keepdims=True))).astype(BF)
            dqw_ref[...] += _rows8(dqn * qhat)

        _per_query_tile(i, nq, tq, lp, tile)

        @pl.when(i == nq - 1)
        def _():
            k = k_ref[...]
            rk = _rstd(k)
            khat = k * rk
            dkn = dkn_s[...] * scale
            gk = dkn * kw_ref[...]
            dk_ref[...] = (rk * (gk - khat * jnp.mean(gk * khat, axis=-1, keepdims=True))).astype(BF)
            dkw_ref[...] += _rows8(dkn * khat)
            dv_ref[...] = dv_s[...].astype(BF)
            dck_ref[...] = dck_s[...]

    vec = pl.BlockSpec((1, HEAD_DIM), lambda h, i: (0, 0))
    part = pl.BlockSpec((SUBLANES, LANES), lambda h, i: (0, 0))
    return _call(
        body, grid=(nh, nq),
        in_specs=[pl.BlockSpec((tq, HEAD_DIM), lambda h, i: (i, qblk0 + h)),
                  pl.BlockSpec((lp, HEAD_DIM), lambda h, i: (0, qblk0 + nh + h)),
                  pl.BlockSpec((lp, HEAD_DIM), lambda h, i: (0, qblk0 + 2 * nh + h)),
                  pl.BlockSpec((tq, LANES), lambda h, i: (i, 0)),
                  pl.BlockSpec((nh, lp), lambda h, i: (0, 0)), vec, vec,
                  pl.BlockSpec((tq, HEAD_DIM), lambda h, i: (i, oblk0 + h))],
        out_specs=[pl.BlockSpec((tq, HEAD_DIM), lambda h, i: (i, h)),
                   pl.BlockSpec((lp, HEAD_DIM), lambda h, i: (0, h)),
                   pl.BlockSpec((lp, HEAD_DIM), lambda h, i: (0, h)),
                   pl.BlockSpec((None, 1, lp), lambda h, i: (h, 0, 0)),
                   part, part],
        out_shape=[SDS((lp, nh * HEAD_DIM), BF)] * 3
        + [SDS((nh, 1, lp), F32), SDS((SUBLANES, LANES), F32), SDS((SUBLANES, LANES), F32)],
        scratch_shapes=[pltpu.VMEM((lp, HEAD_DIM), BF), pltpu.VMEM((lp, HEAD_DIM), BF),
                        pltpu.VMEM((lp, HEAD_DIM), F32), pltpu.VMEM((lp, HEAD_DIM), F32),
                        pltpu.VMEM((1, lp), F32)],
        vmem_mib=56, name=name, comm=comm)(z, z, z, cum, cumt, qw, kw, dmix)


def _adamw_math(w, g, m, v):
    m2 = ADAM_B1 * m + (1.0 - ADAM_B1) * g
    v2 = ADAM_B2 * v + (1.0 - ADAM_B2) * (g * g)
    m_hat = m2 / (1.0 - ADAM_B1 ** ADAM_STEP)
    v_hat = v2 / (1.0 - ADAM_B2 ** ADAM_STEP)
    delta = -ADAM_LR * (m_hat / (jnp.sqrt(v_hat) + ADAM_EPS) + ADAM_WD * w)
    return delta, m2, v2


def _adamw(g_in, w, m, v, name, comm=None):
    r, c = w.shape
    partial_sum = g_in.ndim == 3
    lane_padded = -(-c // LANES) * LANES
    tr = _largest_tile(r, max(16, MIB // (4 * lane_padded) // 16 * 16), 16)

    def body(g_ref, w_ref, m_ref, v_ref, go_ref, d_ref, mo_ref, vo_ref):
        if partial_sum:
            g = g_ref[0].astype(F32)
            for k in range(1, g_in.shape[0]):
                g = g + g_ref[k].astype(F32)
        else:
            g = g_ref[...]
        delta, m2, v2 = _adamw_math(w_ref[...], g, m_ref[...], v_ref[...])
        go_ref[...] = g
        d_ref[...] = delta
        mo_ref[...] = m2
        vo_ref[...] = v2

    blk = pl.BlockSpec((tr, c), lambda i: (i, 0))
    g_spec = pl.BlockSpec((g_in.shape[0], tr, c), lambda i: (0, i, 0)) if partial_sum else blk
    return _call(
        body, grid=(r // tr,), in_specs=[g_spec, blk, blk, blk], out_specs=[blk] * 4,
        out_shape=[SDS((r, c), F32)] * 4, vmem_mib=40, name=name, comm=comm)(g_in, w, m, v)


def _peer(x, y, c, k):
    return (1 - x if k & 4 else x, 1 - y if k & 2 else y, 1 - c if k & 1 else c)


_SIBLING = 1
_ICI_RELS = (2, 4, 6)


def _mesh_pos():
    return lax.axis_index("x"), lax.axis_index("y"), lax.axis_index("c")


def _sem_pair(sems, t, j, n_rel, scalars):
    if scalars:
        i = 2 * (t * n_rel + j)
        return sems[i], sems[i + 1]
    return sems[0].at[t, j], sems[1].at[t, j]


def _dev(pos):
    return 4 * pos[0] + 2 * pos[1] + pos[2]


def _gather_ici(shards, landing=None, rels=(_SIBLING,) + _ICI_RELS):
    n = len(shards)

    def remote(ins, outs, sems, arrival):
        x, y, c = _mesh_pos()
        dst = ins[n:] if landing is not None else outs
        cps = []
        for j, k in enumerate(rels):
            peer = _peer(x, y, c, k)
            slot = _dev(peer) if arrival else _dev((x, y, c))
            for t in range(n):
                send_sem, recv_sem = _sem_pair(sems, t, j, len(rels), landing is not None)
                cps.append(pltpu.make_async_remote_copy(
                    src_ref=ins[t], dst_ref=dst[t].at[slot], send_sem=send_sem, recv_sem=recv_sem,
                    device_id=peer, device_id_type=pl.DeviceIdType.MESH))
        return cps

    if landing is not None:
        def start_remote(ins, outs, sems):
            for cp in remote(ins, outs, sems, False):
                cp.start()

        def finish_remote(ins, outs, sems):
            for cp in remote(ins, outs, sems, True):
                cp.wait_recv()
            for cp in remote(ins, outs, sems, False):
                cp.wait_send()

        return _Comm(list(shards) + list(landing), [SDS(a.shape, a.dtype) for a in landing],
                     [pltpu.SemaphoreType.DMA(())] * (2 * n * len(rels)),
                     start_remote, finish_remote, aliases={n + t: t for t in range(n)})

    def local(ins, outs, sems):
        me = _dev(_mesh_pos())
        return [pltpu.make_async_copy(ins[t], outs[t].at[me], sems[2].at[t]) for t in range(n)]

    def start(ins, outs, sems):
        for cp in local(ins, outs, sems) + remote(ins, outs, sems, False):
            cp.start()

    def finish(ins, outs, sems):
        for cp in local(ins, outs, sems):
            cp.wait()
        for cp in remote(ins, outs, sems, True):
            cp.wait_recv()
        for cp in remote(ins, outs, sems, False):
            cp.wait_send()

    return _Comm(shards, [SDS((N_DEV,) + s.shape, s.dtype) for s in shards],
                 [pltpu.SemaphoreType.DMA((n, len(rels))), pltpu.SemaphoreType.DMA((n, len(rels))),
                  pltpu.SemaphoreType.DMA((n,))], start, finish)


def _gather_diagonal(zones):
    n = len(zones)

    def copies(ins, outs, sems, arrival):
        x, y, c = _mesh_pos()
        y_nb, x_nb, diag = _peer(x, y, c, 2), _peer(x, y, c, 4), _peer(x, y, c, 6)
        cps = []
        for j, (to, origin) in enumerate(((y_nb, x_nb), (x_nb, y_nb))):
            slot = _dev(diag) if arrival else _dev(origin)
            for t in range(n):
                half = ins[t].shape[1] // 2
                rows = ins[t].at[slot, pl.ds(j * half, half)]
                send_sem, recv_sem = _sem_pair(sems, t, j, 2, True)
                cps.append(pltpu.make_async_remote_copy(
                    src_ref=rows, dst_ref=rows, send_sem=send_sem, recv_sem=recv_sem,
                    device_id=to, device_id_type=pl.DeviceIdType.MESH))
        return cps

    def start(ins, outs, sems):
        for cp in copies(ins, outs, sems, False):
            cp.start()

    def finish(ins, outs, sems):
        for cp in copies(ins, outs, sems, True):
            cp.wait_recv()
        for cp in copies(ins, outs, sems, False):
            cp.wait_send()

    return _Comm(list(zones), [SDS(a.shape, a.dtype) for a in zones], [pltpu.SemaphoreType.DMA(())] * (4 * n),
                 start, finish, aliases={t: t for t in range(n)})


def _gather_fwd(partial):
    n = len(partial)

    def copies(ins, outs, sems, arrival):
        x, y, c = _mesh_pos()
        sibling = _peer(x, y, c, _SIBLING)
        cps = []
        for j, k in enumerate(_ICI_RELS):
            slot = _dev(_peer(x, y, c, k | _SIBLING if arrival else k))
            for t in range(n):
                cps.append(pltpu.make_async_remote_copy(
                    src_ref=ins[t].at[slot], dst_ref=outs[t].at[slot], send_sem=sems[0].at[t, j],
                    recv_sem=sems[1].at[t, j], device_id=sibling, device_id_type=pl.DeviceIdType.MESH))
        return cps

    def start(ins, outs, sems):
        for cp in copies(ins, outs, sems, False):
            cp.start()

    def finish(ins, outs, sems):
        for cp in copies(ins, outs, sems, True):
            cp.wait_recv()
        for cp in copies(ins, outs, sems, False):
            cp.wait_send()

    return _Comm(partial, [SDS(a.shape, a.dtype) for a in partial],
                 [pltpu.SemaphoreType.DMA((n, len(_ICI_RELS)))] * 2, start, finish,
                 aliases={t: t for t in range(n)})


def _scatter_sibling(slots):
    n = len(slots)

    def copies(ins, outs, sems):
        x, y, c = _mesh_pos()
        return [pltpu.make_async_remote_copy(
            src_ref=ins[t].at[:, 1 - c], dst_ref=outs[t], send_sem=sems[0].at[t], recv_sem=sems[1].at[t],
            device_id=_peer(x, y, c, _SIBLING), device_id_type=pl.DeviceIdType.MESH) for t in range(n)]

    def start(ins, outs, sems):
        for cp in copies(ins, outs, sems):
            cp.start()

    def finish(ins, outs, sems):
        for cp in copies(ins, outs, sems):
            cp.wait()

    return _Comm(slots, [SDS((s.shape[0],) + s.shape[2:], s.dtype) for s in slots],
                 [pltpu.SemaphoreType.DMA((n,))] * 2, start, finish)


def _scatter_ici(chip_sums, landing=None):
    n = len(chip_sums)

    def remote(ins, outs, sems, arrival):
        x, y, c = _mesh_pos()
        dst = ins[n:] if landing is not None else outs
        cps = []
        for j, k in enumerate(_ICI_RELS):
            peer = _peer(x, y, c, k)
            theirs, mine = 2 * peer[0] + peer[1], 2 * x + y
            for t in range(n):
                send_sem, recv_sem = _sem_pair(sems, t, j, len(_ICI_RELS), landing is not None)
                cps.append(pltpu.make_async_remote_copy(
                    src_ref=ins[t].at[theirs], dst_ref=dst[t].at[theirs if arrival else mine],
                    send_sem=send_sem, recv_sem=recv_sem,
                    device_id=peer, device_id_type=pl.DeviceIdType.MESH))
        return cps

    if landing is not None:
        def start_remote(ins, outs, sems):
            for cp in remote(ins, outs, sems, False):
                cp.start()

        def finish_remote(ins, outs, sems):
            for cp in remote(ins, outs, sems, True):
                cp.wait_recv()
            for cp in remote(ins, outs, sems, False):
                cp.wait_send()

        return _Comm(list(chip_sums) + list(landing), [SDS(a.shape, a.dtype) for a in landing],
                     [pltpu.SemaphoreType.DMA(())] * (2 * n * len(_ICI_RELS)), start_remote, finish_remote,
                     aliases={n + t: t for t in range(n)})

    def local(ins, outs, sems):
        x, y, _ = _mesh_pos()
        return [pltpu.make_async_copy(ins[t].at[2 * x + y], outs[t].at[2 * x + y], sems[2].at[t]) for t in range(n)]

    def start(ins, outs, sems):
        for cp in local(ins, outs, sems) + remote(ins, outs, sems, False):
            cp.start()

    def finish(ins, outs, sems):
        for cp in local(ins, outs, sems):
            cp.wait()
        for cp in remote(ins, outs, sems, True):
            cp.wait_recv()
        for cp in remote(ins, outs, sems, False):
            cp.wait_send()

    return _Comm(chip_sums, [SDS(a.shape, a.dtype) for a in chip_sums],
                 [pltpu.SemaphoreType.DMA((n, len(_ICI_RELS))), pltpu.SemaphoreType.DMA((n, len(_ICI_RELS))),
                  pltpu.SemaphoreType.DMA((n,))], start, finish)


def _chip_sum(slots, from_sibling, core, name):
    nq, _, r, c = slots.shape
    tr = _largest_tile(r, 1024, 16)

    def body(core_ref, a_ref, b_ref, o_ref):
        o_ref[...] = (a_ref[...].astype(F32) + b_ref[...].astype(F32)).astype(BF)

    return pl.pallas_call(
        body,
        grid_spec=pltpu.PrefetchScalarGridSpec(
            num_scalar_prefetch=1, grid=(nq, r // tr),
            in_specs=[pl.BlockSpec((None, None, tr, c), lambda q, i, core_ref: (q, core_ref[0], i, 0)),
                      pl.BlockSpec((None, tr, c), lambda q, i, core_ref: (q, i, 0))],
            out_specs=pl.BlockSpec((None, tr, c), lambda q, i, core_ref: (q, i, 0))),
        out_shape=SDS((nq, r, c), BF), compiler_params=pltpu.CompilerParams(vmem_limit_bytes=40 * MIB),
        name=name)(core, slots, from_sibling)


def _small_reduce(pack_g, meta_g, loss_scale, name, comm=None):
    w = pack_g.shape[2]

    def body(p_ref, m_ref, tot_ref, meta_ref, loss_ref):
        acc = p_ref[0]
        macc = m_ref[0]
        for k in range(1, N_DEV):
            acc = acc + p_ref[k]
            macc = macc + m_ref[k]
        tot = jnp.sum(acc, axis=0, keepdims=True)
        tot_ref[...] = tot
        meta_ref[...] = macc
        loss_ref[...] = jnp.full((1, LANES), loss_scale * jnp.sum(tot[:, w - LANES:w]), F32)

    return pl.pallas_call(
        body, out_shape=[SDS((1, w), F32), SDS(meta_g.shape[1:], F32), SDS((1, LANES), F32)],
        compiler_params=pltpu.CompilerParams(vmem_limit_bytes=32 * MIB), name=name)(pack_g, meta_g)


def _local_step(x, target, sw, plan):
    s_len, d = x.shape
    n_heads, n_meta = plan.n_heads, plan.n_meta
    l = n_meta + s_len
    lp = -(-l // LANES) * LANES
    tm = _largest_tile(lp, 544, 16)
    tq = _largest_tile(lp, 272, 16)
    te = _largest_tile(lp, 272, 16)
    tmd = _largest_tile(d, 512, LANES)

    plan.at("start")
    x, target = plan.gate((x, target))
    zmeta, zpad = jnp.zeros((n_meta, d), F32), jnp.zeros((lp - l, d), F32)
    h0 = jnp.concatenate([zmeta, x, zpad], axis=0)
    tpad = jnp.concatenate([zmeta, target, zpad], axis=0)
    plan.at("landed", (h0, tpad))
    h0 = lax.dynamic_update_slice(h0, plan.weights("meta"), (0, 0))

    wg1, wu1, wd1 = plan.weights("ffn1")
    fs = wg1.shape[1]
    h1, a1, b1, u1 = _ffn_fwd(h0, sw["ffn1_norm"], wg1, wu1, wd1, tm, "ffn1_fwd", plan.comm("ffn1_fwd"))
    plan.at("after_ffn1_fwd", (h1,))
    win, pw, wout = plan.weights("mix")
    nz = win.shape[1]
    p_w = sw["pool_scale"].shape[1]
    npb = p_w // LANES
    fblk = nz // LANES - 1
    tnz = _largest_tile(nz, 1408, LANES)
    qw, kw, bfp, ps = sw["q_norm"], sw["k_norm"], sw["b_forget"], sw["pool_scale"]
    z, u2 = _norm_matmul(h1, sw["mix_norm"], win, tm, tnz, "mix_in", plan.comm("mix_in"))
    cum = _fox_prep(z, bfp, fblk, "fox_prep")
    cumt = cum[:, :n_heads].T
    pool_o = _pool_fwd(z, pw, ps, "pool_fwd")
    att_o = _att_fwd(z, cum, cumt, qw, kw, n_heads, npb, tq, "att_fwd", plan.comm("att_fwd"))
    plan.at("after_att_fwd", (att_o,))
    h2 =_out_proj(h1, pool_o, att_o, wout, tm, "out_proj", plan.comm("out_proj"))
    wg2, wu2, wd2 = plan.weights("ffn2")
    h3, a2, b2, u3 = _ffn_fwd(h2, sw["ffn2_norm"], wg2, wu2, wd2, tm, "ffn2_fwd", plan.comm("ffn2_fwd"))
    dy, dob3, lsq = _loss_head(h3, tpad, n_meta, l, te, "loss_head")

    du3, da2, db2, hid2 = _ffn_bwd_dx(dob3, a2, b2, wg2, wu2, wd2, tm, "ffn2_bwd_dx", plan.comm("ffn2_bwd_dx"))
    dh2, dh2b, dn2 = _rms_bwd(du3, h2, sw["ffn2_norm"], dy, 1.0, te, "ffn2_rms_bwd")
    plan.grad("ffn2_w_gate", _matmul_tn(da2, u3, fs, d, "ffn2_dwg", plan.comm("ffn2_dwg")))
    plan.grad("ffn2_w_up", _matmul_tn(db2, u3, fs, d, "ffn2_dwu", plan.comm("ffn2_dwu")))
    plan.grad("ffn2_w_down", _matmul_tn(hid2, dob3, fs, d, "ffn2_dwd", plan.comm("ffn2_dwd")))
    plan.at("after_ffn2_dwd")

    dmix = _matmul_nt(dh2b, wout, tm, d, BF, "out_proj_bwd", plan.comm("out_proj_bwd"))
    plan.at("after_out_proj_bwd")
    tmp = _largest_tile(p_w, 512, LANES)
    plan.grad("w_out", jnp.concatenate([_matmul_tn(pool_o, dh2b, tmp, d, "dwout_pool"),
                                        _matmul_tn(att_o, dh2b, tmp, d, "dwout_att")], axis=0))
    dzp, dpw, dps = _pool_bwd(z, dmix, pw, ps, "pool_bwd")
    plan.grad("pool_w", dpw)
    plan.at("before_att_bwd")
    dq, dk, dv, dck, dqw, dkw = _att_bwd(z, cum, cumt, qw, kw, dmix, n_heads, npb, npb, tq, "att_bwd",
                                              plan.comm("att_bwd"))
    dcum = -dck[:, 0, :].T
    dcum = jnp.pad(dcum, ((0, 0), (0, LANES - n_heads)))
    dzf, dbf = _fox_bwd(z, bfp, dcum, fblk, "fox_bwd")
    dz = jnp.concatenate([dzp, dq, dk, dv, dzf], axis=1)
    plan.grad("w_in", _matmul_tn(u2, dz, tmd, tnz, "dwin", plan.comm("dwin")))
    du2 = _matmul_nt(dz, win, tm, tnz, F32, "mix_in_bwd", plan.comm("mix_in_bwd"))
    plan.at("before_ffn1_bwd_dx")
    dh1, dob1, dnm = _rms_bwd(du2, h1, sw["mix_norm"], dh2, 0.5, te, "mix_rms_bwd")

    du1, da1, db1, hid1 = _ffn_bwd_dx(dob1, a1, b1, wg1, wu1, wd1, tm, "ffn1_bwd_dx", plan.comm("ffn1_bwd_dx"))
    plan.grad("ffn1_w_gate", _matmul_tn(da1, u1, fs, d, "ffn1_dwg", plan.comm("ffn1_dwg")))
    plan.grad("ffn1_w_up", _matmul_tn(db1, u1, fs, d, "ffn1_dwu", plan.comm("ffn1_dwu")))
    plan.at("before_ffn1_dwd")
    plan.grad("ffn1_w_down", _matmul_tn(hid1, dob1, fs, d, "ffn1_dwd", plan.comm("ffn1_dwd")))
    plan.at("after_ffn1_dwd")
    dh0, _, dn1 = _rms_bwd(du1, h0, sw["ffn1_norm"], dh1, 1.0, te, "ffn1_rms_bwd", plan.comm("ffn1_rms_bwd"))

    small = [dn1, dnm, dn2, dps, dqw, dkw, dbf, lsq]
    return dh0[n_meta:l], dh0[:n_meta], small


_BIG = ("ffn1_w_gate", "ffn1_w_up", "ffn1_w_down", "w_in", "pool_w", "w_out", "ffn2_w_gate", "ffn2_w_up", "ffn2_w_down")
_SMALL = ("ffn1_norm", "mix_norm", "ffn2_norm", "pool_scale", "q_norm", "k_norm", "b_forget")
_ORDER = ("meta_tokens", "ffn1_norm", "ffn1_w_gate", "ffn1_w_up", "ffn1_w_down", "mix_norm", "w_in", "b_forget",
          "q_norm", "k_norm", "pool_w", "pool_scale", "w_out", "ffn2_norm", "ffn2_w_gate", "ffn2_w_up", "ffn2_w_down")


_FFN1 = ("ffn1_w_gate", "ffn1_w_up", "ffn1_w_down")
_FFN2 = ("ffn2_w_gate", "ffn2_w_up", "ffn2_w_down")
_MIX = ("w_in", "pool_w", "w_out")

_RIDES = {
    "out_proj": (("g2", _FFN2),),
    "ffn2_dwu": (("s1", ("ffn2_w_gate",)),),
    "ffn2_dwd": (("s1", ("ffn2_w_up",)),),
    "out_proj_bwd": (("s1", ("ffn2_w_down",)),),
    "mix_in_bwd": (("s1", _MIX),),
    "ffn1_dwu": (("s1", ("ffn1_w_gate",)),),
    "ffn1_dwd": (("s1", ("ffn1_w_up",)),),
    "ffn1_rms_bwd": (("s1", ("ffn1_w_down",)),),
}
_G1_FFN1 = _FFN1 + ("meta_tokens",)
_POINTS = {
    "start": (("start", "g1a", _G1_FFN1), ("gate", _MIX + _FFN2), ("prepare", "g1", _MIX),
              ("prepare", "g1", ("ffn2_w_down",)), ("prepare", "g1", ("ffn2_w_gate", "ffn2_w_up"))),
    "landed": (("wait", "g1a", _G1_FFN1), ("start", "g1b", _G1_FFN1), ("start", "g1", _MIX),
               ("start", "g1", ("ffn2_w_down",)), ("wait", "g1b", _G1_FFN1), ("alone", "g2", _G1_FFN1)),
    "after_ffn1_fwd": (("wait", "g1", _MIX), ("start", "g1", ("ffn2_w_gate", "ffn2_w_up")), ("alone", "g2", _MIX)),
    "after_att_fwd": (("wait", "g1", ("ffn2_w_down",)), ("wait", "g1", ("ffn2_w_gate", "ffn2_w_up"))),
    "after_ffn2_dwd": (("sum", ("ffn2_w_gate",)), ("start", "s2", ("ffn2_w_gate",))),
    "after_out_proj_bwd": (("sum", ("ffn2_w_up",)), ("start", "s2", ("ffn2_w_up",))),
    "before_att_bwd": (("sum", ("ffn2_w_down",)), ("start", "s2", ("ffn2_w_down",))),
    "before_ffn1_bwd_dx": (("sum", _MIX), ("start", "s2", _MIX)),
    "before_ffn1_dwd": (("sum", ("ffn1_w_gate",)), ("start", "s2", ("ffn1_w_gate",))),
    "after_ffn1_dwd": (("sum", ("ffn1_w_up",)), ("start", "s2", ("ffn1_w_up",))),
    "after_ffn1_rms_bwd": (("sum", ("ffn1_w_down",)), ("start", "s2", ("ffn1_w_down",))),
    "before_adamw_ffn2_w_gate": (("wait", "s2", ("ffn2_w_gate",)),),
    "before_adamw_ffn2_w_up": (("wait", "s2", ("ffn2_w_up",)),),
    "before_adamw_ffn2_w_down": (("wait", "s2", ("ffn2_w_down",)),),
    "before_adamw_w_in": (("wait", "s2", _MIX),),
    "before_adamw_ffn1_w_gate": (("wait", "s2", ("ffn1_w_gate",)),),
    "before_adamw_ffn1_w_up": (("wait", "s2", ("ffn1_w_up",)),),
    "before_adamw_ffn1_w_down": (("wait", "s2", ("ffn1_w_down",)),),
}


def _own_slot_filled(block, slot, n_slots):
    zone = lax.empty((n_slots,) + block.shape, block.dtype)
    return lax.dynamic_update_slice(zone, block[None], (slot,) + (0,) * block.ndim)


class _MeshPlan:
    def __init__(self, raw, pos, d, d_in, n_heads):
        self.raw, self.pos = dict(raw), pos
        self.core = pos[2].astype(jnp.int32).reshape(1)
        self.d, self.d_in, self.n_heads, self.n_meta = d, d_in, n_heads, raw["meta_tokens"].shape[0]
        self.partial, self.full, self.slots, self.from_sibling, self.chip_sum, self.received = {}, {}, {}, {}, {}, {}
        self.partial_a, self.pending, self.prepared, self.started, self.tokens = {}, [], {}, {}, []

    def gate(self, arrays):
        gated = lax.optimization_barrier((self.tokens[-1], tuple(arrays)))
        self.tokens[-1] = gated[0]
        return gated[1]

    def _phase(self, kind, names):
        src, dst, make = {"g2": (self.partial, self.full, _gather_fwd),
                          "s1": (self.slots, self.from_sibling, _scatter_sibling),
                          "s2": (self.chip_sum, self.received, _scatter_ici)}[kind]
        op = make([src[n] for n in names])
        self.pending.append((op, dst, names))
        return op

    def _settle(self):
        for op, dst, names in self.pending:
            dst.update(zip(names, op.results))
        self.pending = []

    def _prepare(self, kind, names):
        x, y, c = self.pos
        if kind in ("g1", "g1a"):
            blocks = [self.raw[n] if n == "meta_tokens" else self.raw[n].astype(BF) for n in names]
            rels = (_SIBLING,) + (_ICI_RELS if kind == "g1" else _ICI_RELS[:2])
            op = _gather_ici(blocks, [_own_slot_filled(b, 4 * x + 2 * y + c, N_DEV) for b in blocks], rels)
        elif kind == "g1b":
            op = _gather_diagonal([self.partial_a[n] for n in names])
        else:
            sums = [self.chip_sum[n] for n in names]
            mine = [lax.dynamic_index_in_dim(s, 2 * x + y, 0, keepdims=False) for s in sums]
            op = _scatter_ici(sums, [_own_slot_filled(b, 2 * x + y, N_DEV // 2) for b in mine])
        self.prepared[(kind, names)] = op

    def _start(self, kind, names):
        if (kind, names) not in self.prepared:
            self._prepare(kind, names)
        self._launch((kind, names), self.prepared.pop((kind, names)), "_".join(("start", kind, names[0])))

    def _launch(self, key, op, name):
        if self.tokens:
            op.arrs = list(self.gate(op.arrs))
        self.started[key], token = _split_start(op, name)
        self.tokens.append(token)

    def start_small_gather(self, arrays):
        x, y, c = self.pos
        zones = [_own_slot_filled(a, 4 * x + 2 * y + c, N_DEV) for a in arrays]
        self._launch("small", _gather_ici(list(arrays), zones, rels=tuple(range(1, N_DEV))), "start_gather_small")

    def wait_small_gather(self, afters):
        return _split_wait(self.started.pop("small"), afters, "wait_gather_small")

    def _wait(self, kind, names, afters):
        afters = list(afters) + [a for op in self.prepared.values() for a in op.arrs[len(op.arrs) // 2:]]
        landed = _split_wait(self.started.pop((kind, names)), afters, "_".join(("wait", kind, names[0])))
        {"g1": self.partial, "g1a": self.partial_a, "g1b": self.partial, "s2": self.received}[kind].update(
            zip(names, landed))

    def comm(self, kernel_name):
        self._settle()
        ops = [self._phase(kind, names) for kind, names in _RIDES.get(kernel_name, ())]
        if self.tokens:
            ops.append(_Comm(self.tokens, [], [], lambda *a: None, lambda *a: None))
            self.tokens = []
        return _merge_comm(ops)

    def at(self, point, after=()):
        for step in _POINTS.get(point, ()):
            self._settle()
            if step[0] == "alone":
                _comm_alone(self._phase(step[1], step[2]), "_".join((step[1], point)))
            elif step[0] == "start":
                self._start(step[1], step[2])
            elif step[0] == "prepare":
                self._prepare(step[1], step[2])
            elif step[0] == "gate":
                self.raw.update(zip(step[1], self.gate([self.raw[n] for n in step[1]])))
            elif step[0] == "wait":
                self._wait(step[1], step[2], tuple(after) + tuple(self.tokens[-1:]))
            else:
                for n in step[1]:
                    self.chip_sum[n] = _chip_sum(self.slots[n], self.from_sibling[n], self.core, "chip_sum_" + n)

    def weights(self, group):
        self._settle()
        f, d = self.full, self.d
        if group == "meta":
            g = f["meta_tokens"]
            return g.transpose(1, 0, 2).reshape(g.shape[1], d)
        if group == "ffn1":
            return tuple(f[n] for n in _FFN1)
        if group == "ffn2":
            return tuple(f[n] for n in _FFN2)
        n_main = self.d_in - self.n_heads
        win = f["w_in"].transpose(1, 0, 2).reshape(d, self.d_in)
        win = jnp.concatenate([win[:, :n_main], jnp.pad(win[:, n_main:], ((0, 0), (0, LANES - self.n_heads)))], axis=1)
        pw = f["pool_w"]
        gw = pw.shape[2]
        pw = pw.reshape(N_DEV, -1, gw // N_DEV, gw).transpose(1, 0, 2, 3).reshape(-1, gw, gw)
        return win, pw, f["w_out"].reshape(-1, d)

    def grad(self, name, g):
        d = self.d
        if name == "w_in":
            g = g[:, :self.d_in].reshape(d, N_DEV, -1).transpose(1, 0, 2)
        elif name == "pool_w":
            ng, gw = g.shape[0], g.shape[2]
            g = g.astype(BF).reshape(ng, N_DEV, -1, gw).transpose(1, 0, 2, 3).reshape(N_DEV, -1, gw)
        elif name == "w_out":
            g = g.reshape(N_DEV, -1, d)
        self.slots[name] = g.reshape((N_DEV // 2, 2) + g.shape[1:])

    def gradient_parts(self, name):
        self._settle()
        return self.received[name]


_TRANSPOSED = ("ffn1_w_gate", "ffn1_w_up", "ffn2_w_gate", "ffn2_w_up")


def _as2d(name, a):
    return a[0].T if name in _TRANSPOSED else a.reshape(-1, a.shape[-1])


def _from2d(name, a2d, shape):
    return a2d.T.reshape(shape) if name in _TRANSPOSED else a2d.reshape(shape)


def kernel(x, meta_tokens, ffn1_norm, ffn1_w_gate, ffn1_w_up, ffn1_w_down, mix_norm, w_in, b_forget, q_norm, k_norm, pool_w, pool_scale, w_out, ffn2_norm, ffn2_w_gate, ffn2_w_up, ffn2_w_down, loss_target, m_meta_tokens, m_ffn1_norm, m_ffn1_w_gate, m_ffn1_w_up, m_ffn1_w_down, m_mix_norm, m_w_in, m_b_forget, m_q_norm, m_k_norm, m_pool_w, m_pool_scale, m_w_out, m_ffn2_norm, m_ffn2_w_gate, m_ffn2_w_up, m_ffn2_w_down, v_meta_tokens, v_ffn1_norm, v_ffn1_w_gate, v_ffn1_w_up, v_ffn1_w_down, v_mix_norm, v_w_in, v_b_forget, v_q_norm, v_k_norm, v_pool_w, v_pool_scale, v_w_out, v_ffn2_norm, v_ffn2_w_gate, v_ffn2_w_up, v_ffn2_w_down):
    w = dict(meta_tokens=meta_tokens, ffn1_norm=ffn1_norm, ffn1_w_gate=ffn1_w_gate, ffn1_w_up=ffn1_w_up,
             ffn1_w_down=ffn1_w_down, mix_norm=mix_norm, w_in=w_in, b_forget=b_forget, q_norm=q_norm, k_norm=k_norm,
             pool_w=pool_w, pool_scale=pool_scale, w_out=w_out, ffn2_norm=ffn2_norm, ffn2_w_gate=ffn2_w_gate,
             ffn2_w_up=ffn2_w_up, ffn2_w_down=ffn2_w_down)
    m = dict(meta_tokens=m_meta_tokens, ffn1_norm=m_ffn1_norm, ffn1_w_gate=m_ffn1_w_gate, ffn1_w_up=m_ffn1_w_up,
             ffn1_w_down=m_ffn1_w_down, mix_norm=m_mix_norm, w_in=m_w_in, b_forget=m_b_forget, q_norm=m_q_norm,
             k_norm=m_k_norm, pool_w=m_pool_w, pool_scale=m_pool_scale, w_out=m_w_out, ffn2_norm=m_ffn2_norm,
             ffn2_w_gate=m_ffn2_w_gate, ffn2_w_up=m_ffn2_w_up, ffn2_w_down=m_ffn2_w_down)
    v = dict(meta_tokens=v_meta_tokens, ffn1_norm=v_ffn1_norm, ffn1_w_gate=v_ffn1_w_gate, ffn1_w_up=v_ffn1_w_up,
             ffn1_w_down=v_ffn1_w_down, mix_norm=v_mix_norm, w_in=v_w_in, b_forget=v_b_forget, q_norm=v_q_norm,
             k_norm=v_k_norm, pool_w=v_pool_w, pool_scale=v_pool_scale, w_out=v_w_out, ffn2_norm=v_ffn2_norm,
             ffn2_w_gate=v_ffn2_w_gate, ffn2_w_up=v_ffn2_w_up, ffn2_w_down=v_ffn2_w_down)

    d = x.shape[-1]
    n_heads = b_forget.shape[-1]
    pos = (lax.axis_index("x"), lax.axis_index("y"), lax.axis_index("c"))
    me = 4 * pos[0] + 2 * pos[1] + pos[2]

    raw = {k: _as2d(k, w[k]) for k in _BIG}
    raw["meta_tokens"] = meta_tokens
    plan = _MeshPlan(raw, pos, d, N_DEV * w_in.shape[-1], n_heads)
    sw = {k: w[k] for k in _SMALL}
    sw["b_forget"] = jnp.pad(b_forget, ((0, 0), (0, LANES - n_heads)))
    dx, dmeta, small = _local_step(x[0], loss_target[0], sw, plan)

    res = {}
    last = dx

    plan.start_small_gather([jnp.concatenate(small, axis=1), dmeta])
    plan.at("after_ffn1_rms_bwd")

    def update_shards(names):
        nonlocal last
        for k in names:
            plan.at("before_adamw_" + k, (last,))
            res[k] = _adamw(plan.gradient_parts(k), _as2d(k, w[k]), _as2d(k, m[k]), _as2d(k, v[k]), "adamw_" + k,
                            plan.comm("adamw_" + k))
            last = res[k][0]

    update_shards(_FFN2 + _MIX + ("ffn1_w_gate", "ffn1_w_up"))

    pack_g, meta_g = plan.wait_small_gather((last,))
    tot, dmeta_tot, loss_row = _small_reduce(pack_g, meta_g, 0.5 / d, "small_reduce")

    mcols = meta_tokens.shape[1]
    g_meta = lax.dynamic_slice_in_dim(dmeta_tot, me * mcols, mcols, axis=1)
    res["meta_tokens"] = _adamw(g_meta, meta_tokens, m_meta_tokens, v_meta_tokens, "adamw_meta_tokens")

    def packed(src):
        return jnp.concatenate([src[k] for k in _SMALL[:-1]] + [jnp.pad(src["b_forget"], ((0, 0), (0, LANES - n_heads)))],
                               axis=1)

    wp = packed(w)
    sm = _adamw(tot[:, :wp.shape[1]], wp, packed(m), packed(v), "adamw_small")
    off = 0
    for k in _SMALL:
        width = w[k].shape[1]
        res[k] = tuple(o[:, off:off + width] for o in sm)
        off += width if k != "b_forget" else LANES

    last = sm[0]
    update_shards(("ffn1_w_down",))

    outs =[loss_row[0, 0], dx[None]]
    for idx in range(4):
        outs += [_from2d(k, res[k][idx], w[k].shape) for k in _ORDER]
    return tuple(outs)
```

```python
import functools

import jax
import jax.numpy as jnp
from jax import lax
from jax.experimental import pallas as pl
from jax.experimental.pallas import tpu as pltpu

F32 = jnp.float32
BF = jnp.bfloat16
SDS = jax.ShapeDtypeStruct

N_DEV = 8
LANES = 128
SUBLANES = 8
HEAD_DIM = 128
POOL_WINDOWS = (2, 4, 8, 16)
RMS_EPS = 1e-6
NEG_BIG = -1e30
MIB = 1024 * 1024

ADAM_LR = 0.001
ADAM_B1 = 0.9
ADAM_B2 = 0.999
ADAM_EPS = 1e-08
ADAM_WD = 0.01
ADAM_STEP = 10


class _Comm:
    def __init__(self, arrs, out_shape, sems, start, finish, aliases=None):
        self.arrs, self.out_shape, self.sems = list(arrs), list(out_shape), list(sems)
        self.start, self.finish, self.aliases = start, finish, dict(aliases or {})
        self.results = None


def _merge_comm(ops):
    ops = [op for op in ops if op is not None]
    if not ops:
        return None
    na, no, ns = [0], [0], [0]
    for op in ops:
        na.append(na[-1] + len(op.arrs))
        no.append(no[-1] + len(op.out_shape))
        ns.append(ns[-1] + len(op.sems))

    def parts(i, ins, outs, sems):
        return ins[na[i]:na[i + 1]], outs[no[i]:no[i + 1]], sems[ns[i]:ns[i + 1]]

    def start(ins, outs, sems):
        for i, op in enumerate(ops):
            op.start(*parts(i, ins, outs, sems))

    def finish(ins, outs, sems):
        for i, op in enumerate(ops):
            op.finish(*parts(i, ins, outs, sems))

    aliases = {}
    for i, op in enumerate(ops):
        for a, o in op.aliases.items():
            aliases[na[i] + a] = no[i] + o
    merged = _Comm([a for op in ops for a in op.arrs], [s for op in ops for s in op.out_shape],
                   [s for op in ops for s in op.sems], start, finish, aliases)
    merged.children = (ops, no)
    return merged


def _deliver(comm, results):
    comm.results = list(results)
    if hasattr(comm, "children"):
        ops, no = comm.children
        for i, op in enumerate(ops):
            _deliver(op, results[no[i]:no[i + 1]])


def _call(body, *, grid, in_specs, out_specs, out_shape, scratch_shapes=(), vmem_mib, name, comm=None):
    single = not isinstance(out_shape, (list, tuple))
    out_specs = [out_specs] if single else list(out_specs)
    out_shape = [out_shape] if single else list(out_shape)
    in_specs, scratch_shapes = list(in_specs), list(scratch_shapes)
    params = pltpu.CompilerParams(dimension_semantics=("arbitrary",) * len(grid), vmem_limit_bytes=vmem_mib * MIB)
    n_in, n_out, n_scr = len(in_specs), len(out_specs), len(scratch_shapes)

    def run(*args):
        if comm is None:
            res = pl.pallas_call(body, grid=grid, in_specs=in_specs, out_specs=out_specs, out_shape=out_shape,
                                 scratch_shapes=scratch_shapes, compiler_params=params, name=name)(*args)
            return res[0] if single else res
        ci, co = len(comm.arrs), len(comm.out_shape)

        def with_comm(*refs):
            ins, cins = refs[:n_in], refs[n_in:n_in + ci]
            o0 = n_in + ci
            outs, couts = refs[o0:o0 + n_out], refs[o0 + n_out:o0 + n_out + co]
            s0 = o0 + n_out + co
            scr, csems = refs[s0:s0 + n_scr], refs[s0 + n_scr:]
            ids = [pl.program_id(a) for a in range(len(grid))]
            first = functools.reduce(jnp.logical_and, [i == 0 for i in ids])
            last = functools.reduce(jnp.logical_and, [i == g - 1 for i, g in zip(ids, grid)])

            @pl.when(first)
            def _():
                comm.start(cins, couts, csems)

            body(*ins, *outs, *scr)

            @pl.when(last)
            def _():
                comm.finish(cins, couts, csems)

        anyspec = pl.BlockSpec(memory_space=pl.ANY)
        res = pl.pallas_call(
            with_comm, grid=grid, in_specs=in_specs + [anyspec] * ci, out_specs=out_specs + [anyspec] * co,
            out_shape=out_shape + comm.out_shape, scratch_shapes=scratch_shapes + comm.sems,
            input_output_aliases={n_in + a: n_out + o for a, o in comm.aliases.items()},
            compiler_params=params, name=name)(*args, *comm.arrs)
        _deliver(comm, res[n_out:])
        return res[0] if single else res[:n_out]

    return run


def _comm_alone(comm, name):
    def body(*refs):
        ci, co = len(comm.arrs), len(comm.out_shape)
        ins, outs, sems = refs[:ci], refs[ci:ci + co], refs[ci + co:]
        comm.start(ins, outs, sems)
        comm.finish(ins, outs, sems)

    anyspec = pl.BlockSpec(memory_space=pl.ANY)
    res = pl.pallas_call(
        body, in_specs=[anyspec] * len(comm.arrs), out_specs=[anyspec] * len(comm.out_shape),
        out_shape=comm.out_shape, scratch_shapes=comm.sems, input_output_aliases=comm.aliases, name=name)(*comm.arrs)
    _deliver(comm, res)


def _split_start(comm, name):
    na, ns = len(comm.arrs), len(comm.sems)

    def body(*refs):
        comm.start(refs[:na], None, refs[na:na + ns])
        token = refs[-1]
        token[...] = jnp.zeros_like(token)

    hbm = pl.BlockSpec(memory_space=pltpu.HBM)
    res = pl.pallas_call(
        body, name=name,
        out_shape=tuple(comm.sems) + tuple(pltpu.HBM(a.shape, a.dtype) for a in comm.arrs)
        + (SDS((SUBLANES, LANES), F32),),
        in_specs=[hbm] * na,
        out_specs=[pl.BlockSpec(memory_space=pltpu.SEMAPHORE)] * ns + [hbm] * na + [pl.BlockSpec(memory_space=pltpu.VMEM)],
        input_output_aliases={i: ns + i for i in range(na)},
        compiler_params=pltpu.CompilerParams(has_side_effects=pltpu.SideEffectType.DATAFLOW_SIDE_EFFECTING),
    )(*[pltpu.with_memory_space_constraint(a, pltpu.HBM) for a in comm.arrs])
    return (comm, res[:ns], res[ns:ns + na]), res[-1]


def _split_wait(started, afters, name):
    comm, sems, thru = started
    na, ns = len(thru), len(sems)
    afters = list(afters)

    def body(*refs):
        comm.finish(refs[:na], None, refs[na:na + ns])

    hbm = pl.BlockSpec(memory_space=pltpu.HBM)
    res = pl.pallas_call(
        body, name=name, out_shape=tuple(pltpu.HBM(a.shape, a.dtype) for a in thru),
        in_specs=[hbm] * na + [pl.BlockSpec(memory_space=pltpu.SEMAPHORE)] * ns
        + [pl.BlockSpec(memory_space=pl.ANY)] * len(afters),
        out_specs=[hbm] * na, input_output_aliases={i: i for i in range(na)},
        compiler_params=pltpu.CompilerParams(has_side_effects=pltpu.SideEffectType.DATAFLOW_SIDE_EFFECTING),
    )(*thru, *sems, *afters)
    return res[na - len(comm.out_shape):]


def _largest_tile(n, cap, mult):
    if n <= cap:
        return n
    best = None
    for t in range(mult, cap + 1, mult):
        if n % t == 0:
            best = t
    assert best is not None, (n, cap, mult)
    return best


def _dot(a, b):
    return jnp.dot(a, b, preferred_element_type=F32)


def _dot_nt(a, b):
    return lax.dot_general(a, b, (((1,), (1,)), ((), ())), preferred_element_type=F32)


def _dot_tn(a, b):
    return lax.dot_general(a, b, (((0,), (0,)), ((), ())), preferred_element_type=F32)


def _rows8(x):
    t, c = x.shape
    return jnp.sum(x.reshape(t // SUBLANES, SUBLANES, c), axis=0)


def _rstd(x):
    return lax.rsqrt(jnp.mean(x * x, axis=-1, keepdims=True) + RMS_EPS)


def _ffn_fwd(h, g, wg, wu, wd, tm, name, comm=None):
    lp, d = h.shape
    ns, fs, _ = wg.shape

    def body(h_ref, g_ref, wg_ref, wu_ref, wd_ref, out_ref, a_ref, b_ref, u_ref, acc_ref):
        j = pl.program_id(1)

        @pl.when(j == 0)
        def _():
            hh = h_ref[...]
            u_ref[...] = (hh * _rstd(hh) * g_ref[...]).astype(BF)
            acc_ref[...] = jnp.zeros_like(acc_ref)

        u = u_ref[...]
        a = _dot_nt(u, wg_ref[...])
        b = _dot_nt(u, wu_ref[...])
        a_ref[...] = a.astype(BF)
        b_ref[...] = b.astype(BF)
        hid = (a * jax.nn.sigmoid(a) * b).astype(BF)
        acc_ref[...] += _dot(hid, wd_ref[...])

        @pl.when(j == ns - 1)
        def _():
            out_ref[...] = h_ref[...] + 0.5 * acc_ref[...]

    row = pl.BlockSpec((tm, d), lambda i, j: (i, 0))
    act = pl.BlockSpec((None, tm, fs), lambda i, j: (j, i, 0))
    return _call(
        body, grid=(lp // tm, ns),
        in_specs=[row, pl.BlockSpec((1, d), lambda i, j: (0, 0)),
                  pl.BlockSpec((None, fs, d), lambda i, j: (j, 0, 0)),
                  pl.BlockSpec((None, fs, d), lambda i, j: (j, 0, 0)),
                  pl.BlockSpec((None, fs, d), lambda i, j: (j, 0, 0))],
        out_specs=[row, act, act, row],
        out_shape=[SDS((lp, d), F32), SDS((ns, lp, fs), BF), SDS((ns, lp, fs), BF), SDS((lp, d), BF)],
        scratch_shapes=[pltpu.VMEM((tm, d), F32)],
        vmem_mib=56, name=name, comm=comm)(h, g, wg, wu, wd)


def _ffn_fwd_part(h, g, wg, wu, wd, order, carry, tm, name, deps=()):
    lp, d = h.shape
    fs = wg.shape[1]
    k = order.shape[0]
    first = carry is None
    n_in = 5 if first else 8

    def body(order_ref, *refs):
        outs = refs[n_in + len(deps):]
        if first:
            h_ref, g_ref, wg_ref, wu_ref, wd_ref = refs[:n_in]
            out_ref, a_ref, b_ref, u_ref, acc_ref = outs
        else:
            h_ref, acc_in_ref, u_ref, _, _, wg_ref, wu_ref, wd_ref = refs[:n_in]
            out_ref, a_ref, b_ref, acc_ref = outs
        j = pl.program_id(1)

        @pl.when(j == 0)
        def _():
            if first:
                hh = h_ref[...]
                u_ref[...] = (hh * _rstd(hh) * g_ref[...]).astype(BF)
                acc_ref[...] = jnp.zeros_like(acc_ref)
            else:
                acc_ref[...] = acc_in_ref[...]

        u = u_ref[...]
        a = _dot_nt(u, wg_ref[...])
        b = _dot_nt(u, wu_ref[...])
        a_ref[...] = a.astype(BF)
        b_ref[...] = b.astype(BF)
        hid = (a * jax.nn.sigmoid(a) * b).astype(BF)
        acc_ref[...] += _dot(hid, wd_ref[...])

        @pl.when(j == k - 1)
        def _():
            out_ref[...] = acc_ref[...] if first else h_ref[...] + 0.5 * acc_ref[...]

    row = pl.BlockSpec((tm, d), lambda i, j, o: (i, 0))
    act = pl.BlockSpec((None, tm, fs), lambda i, j, o: (o[j], i, 0))
    wsp = pl.BlockSpec((None, fs, d), lambda i, j, o: (o[j], 0, 0))
    anyspec = pl.BlockSpec(memory_space=pl.ANY)
    acts = [SDS((wg.shape[0], lp, fs), BF)] * 2
    if first:
        in_specs = [row, pl.BlockSpec((1, d), lambda i, j, o: (0, 0)), wsp, wsp, wsp]
        out_specs, out_shape = [row, act, act, row], [SDS((lp, d), F32)] + acts + [SDS((lp, d), BF)]
        args, aliases = (h, g, wg, wu, wd), {}
    else:
        acc, a_prev, b_prev, u_prev = carry
        in_specs = [row, row, row, anyspec, anyspec, wsp, wsp, wsp]
        out_specs, out_shape = [row, act, act], [SDS((lp, d), F32)] + acts
        args, aliases = (h, acc, u_prev, a_prev, b_prev, wg, wu, wd), {4: 1, 5: 2}
    return pl.pallas_call(
        body,
        grid_spec=pltpu.PrefetchScalarGridSpec(
            num_scalar_prefetch=1, grid=(lp // tm, k), in_specs=in_specs + [anyspec] * len(deps),
            out_specs=out_specs, scratch_shapes=[pltpu.VMEM((tm, d), F32)]),
        out_shape=out_shape, input_output_aliases=aliases,
        compiler_params=pltpu.CompilerParams(dimension_semantics=("arbitrary",) * 2, vmem_limit_bytes=60 * MIB),
        name=name)(order, *args, *deps)


def _ffn_bwd_dx(dob, a, b, wg, wu, wd, tm, name, comm=None):
    lp, d = dob.shape
    ns, fs, _ = wg.shape

    def body(do_ref, a_ref, b_ref, wg_ref, wu_ref, wd_ref, du_ref, da_ref, db_ref, hid_ref):
        j = pl.program_id(1)

        @pl.when(j == 0)
        def _():
            du_ref[...] = jnp.zeros_like(du_ref)

        dhid = _dot_nt(do_ref[...], wd_ref[...])
        av = a_ref[...].astype(F32)
        bv = b_ref[...].astype(F32)
        sig = jax.nn.sigmoid(av)
        sil = av * sig
        dbv = (dhid * sil).astype(BF)
        dav = (dhid * bv * (sig * (1.0 + av * (1.0 - sig)))).astype(BF)
        hid_ref[...] = (sil * bv).astype(BF)
        da_ref[...] = dav
        db_ref[...] = dbv
        du_ref[...] += _dot(dav, wg_ref[...]) + _dot(dbv, wu_ref[...])

    row = pl.BlockSpec((tm, d), lambda i, j: (i, 0))
    act = pl.BlockSpec((None, tm, fs), lambda i, j: (j, i, 0))
    return _call(
        body, grid=(lp // tm, ns),
        in_specs=[row, act, act,
                  pl.BlockSpec((None, fs, d), lambda i, j: (j, 0, 0)),
                  pl.BlockSpec((None, fs, d), lambda i, j: (j, 0, 0)),
                  pl.BlockSpec((None, fs, d), lambda i, j: (j, 0, 0))],
        out_specs=[row, act, act, act],
        out_shape=[SDS((lp, d), F32)] + [SDS((ns, lp, fs), BF)] * 3,
        vmem_mib=56, name=name, comm=comm)(dob, a, b, wg, wu, wd)


def _rms_bwd(du, h, g, dres, bscale, tm, name, comm=None):
    lp, d = h.shape

    def body(du_ref, h_ref, g_ref, dres_ref, dh_ref, dhb_ref, dg_ref):
        @pl.when(pl.program_id(0) == 0)
        def _():
            dg_ref[...] = jnp.zeros_like(dg_ref)

        hh = h_ref[...]
        r = _rstd(hh)
        xhat = hh * r
        duv = du_ref[...]
        dg_ref[...] += _rows8(duv * xhat)
        dxh = duv * g_ref[...]
        dh = dres_ref[...] + r * (dxh - xhat * jnp.mean(dxh * xhat, axis=-1, keepdims=True))
        dh_ref[...] = dh
        dhb_ref[...] = (bscale * dh).astype(BF)

    row = pl.BlockSpec((tm, d), lambda i: (i, 0))
    return _call(
        body, grid=(lp // tm,),
        in_specs=[row, row, pl.BlockSpec((1, d), lambda i: (0, 0)), row],
        out_specs=[row, row, pl.BlockSpec((SUBLANES, d), lambda i: (0, 0))],
        out_shape=[SDS((lp, d), F32), SDS((lp, d), BF), SDS((SUBLANES, d), F32)],
        vmem_mib=48, name=name, comm=comm)(du, h, g, dres)


def _matmul_tn(a, b, tm, tn, name, comm=None):
    a_b, b_b = a.ndim == 3, b.ndim == 3
    ns = a.shape[0] if a_b else (b.shape[0] if b_b else 1)
    l, m = a.shape[-2:]
    n = b.shape[-1]

    def body(a_ref, b_ref, o_ref):
        o_ref[...] = _dot_tn(a_ref[...], b_ref[...]).astype(o_ref.dtype)

    a_spec = (pl.BlockSpec((None, l, tm), lambda s, i, j: (s, 0, i)) if a_b
              else pl.BlockSpec((l, tm), lambda s, i, j: (0, i)))
    b_spec = (pl.BlockSpec((None, l, tn), lambda s, i, j: (s, 0, j)) if b_b
              else pl.BlockSpec((l, tn), lambda s, i, j: (0, j)))
    batched = a_b or b_b
    o_spec = (pl.BlockSpec((None, tm, tn), lambda s, i, j: (s, i, j)) if batched
              else pl.BlockSpec((tm, tn), lambda s, i, j: (i, j)))
    o_shape = SDS((ns, m, n), BF) if batched else SDS((m, n), BF)
    return _call(
        body, grid=(ns, m // tm, n // tn), in_specs=[a_spec, b_spec], out_specs=o_spec, out_shape=o_shape,
        vmem_mib=48, name=name, comm=comm)(a, b)


def _matmul_nt(x, w, tm, tk, out_dtype, name, comm=None):
    l, k = x.shape
    n = w.shape[0]
    nk = k // tk

    def body(x_ref, w_ref, o_ref, acc_ref):
        kk = pl.program_id(1)

        @pl.when(kk == 0)
        def _():
            acc_ref[...] = jnp.zeros_like(acc_ref)

        acc_ref[...] += _dot_nt(x_ref[...], w_ref[...])

        @pl.when(kk == nk - 1)
        def _():
            o_ref[...] = acc_ref[...].astype(o_ref.dtype)

    return _call(
        body, grid=(l // tm, nk),
        in_specs=[pl.BlockSpec((tm, tk), lambda i, kk: (i, kk)), pl.BlockSpec((n, tk), lambda i, kk: (0, kk))],
        out_specs=pl.BlockSpec((tm, n), lambda i, kk: (i, 0)),
        out_shape=SDS((l, n), out_dtype),
        scratch_shapes=[pltpu.VMEM((tm, n), F32)],
        vmem_mib=48, name=name, comm=comm)(x, w)


def _norm_matmul(h, g, w, tm, tn, name, comm=None):
    lp, d = h.shape
    n = w.shape[1]

    def body(h_ref, g_ref, w_ref, z_ref, u_ref):
        @pl.when(pl.program_id(1) == 0)
        def _():
            hh = h_ref[...]
            u_ref[...] = (hh * _rstd(hh) * g_ref[...]).astype(BF)

        z_ref[...] = _dot(u_ref[...], w_ref[...])

    row = pl.BlockSpec((tm, d), lambda i, j: (i, 0))
    return _call(
        body, grid=(lp // tm, n // tn),
        in_specs=[row, pl.BlockSpec((1, d), lambda i, j: (0, 0)), pl.BlockSpec((d, tn), lambda i, j: (0, j))],
        out_specs=[pl.BlockSpec((tm, tn), lambda i, j: (i, j)), row],
        out_shape=[SDS((lp, n), F32), SDS((lp, d), BF)],
        vmem_mib=48, name=name, comm=comm)(h, g, w)


def _out_proj(h, pool_o, att_o, w_out, tm, name, comm=None):
    lp, d = h.shape
    p = pool_o.shape[1]
    dm = w_out.shape[0]

    def body(h_ref, p_ref, a_ref, w_ref, o_ref):
        o_ref[...] = h_ref[...] + _dot(p_ref[...], w_ref[0:p, :]) + _dot(a_ref[...], w_ref[p:dm, :])

    row = pl.BlockSpec((tm, d), lambda i: (i, 0))
    return _call(
        body, grid=(lp // tm,),
        in_specs=[row, pl.BlockSpec((tm, p), lambda i: (i, 0)), pl.BlockSpec((tm, dm - p), lambda i: (i, 0)),
                  pl.BlockSpec((dm, d), lambda i: (0, 0))],
        out_specs=row, out_shape=SDS((lp, d), F32),
        vmem_mib=48, name=name, comm=comm)(h, pool_o, att_o, w_out)


def _loss_head(y, tpad, row0, row1, tm, name, comm=None):
    lp, d = y.shape

    def body(y_ref, t_ref, dy_ref, dob_ref, ls_ref):
        i = pl.program_id(0)

        @pl.when(i == 0)
        def _():
            ls_ref[...] = jnp.zeros_like(ls_ref)

        rows = i * tm + lax.broadcasted_iota(jnp.int32, (tm, d), 0)
        err = jnp.where((rows >= row0) & (rows < row1), y_ref[...] - t_ref[...], 0.0)
        dy = err * (1.0 / d)
        dy_ref[...] = dy
        dob_ref[...] = (0.5 * dy).astype(BF)
        sq = _rows8(err * err)
        acc = sq[:, 0:LANES]
        for c in range(1, d // LANES):
            acc = acc + sq[:, c * LANES:(c + 1) * LANES]
        ls_ref[...] += acc

    row = pl.BlockSpec((tm, d), lambda i: (i, 0))
    return _call(
        body, grid=(lp // tm,), in_specs=[row, row],
        out_specs=[row, row, pl.BlockSpec((SUBLANES, LANES), lambda i: (0, 0))],
        out_shape=[SDS((lp, d), F32), SDS((lp, d), BF), SDS((SUBLANES, LANES), F32)],
        vmem_mib=48, name=name, comm=comm)(y, tpad)


def _window_select(levels, gidx):
    out = levels[-1]
    for k in range(len(levels) - 2, -1, -1):
        out = jnp.where(gidx == k, levels[k], out)
    return out


def _pool_window_mean_minus_id(x, gidx):
    rows = lax.broadcasted_iota(jnp.int32, x.shape, 0)
    levels = []
    s = x
    shift = 1
    while shift < POOL_WINDOWS[-1]:
        s = s + jnp.where(rows >= shift, pltpu.roll(s, shift, 0), 0.0)
        shift *= 2
        if shift in POOL_WINDOWS:
            levels.append(s)
    win = _window_select(levels, gidx)
    cnt = jnp.minimum(rows + 1, _window_select(list(POOL_WINDOWS), gidx)).astype(F32)
    return win / cnt - x, cnt


def _pool_window_transpose(dy, cnt, gidx):
    lp = dy.shape[0]
    rows = lax.broadcasted_iota(jnp.int32, dy.shape, 0)
    levels = []
    s = dy / cnt
    shift = 1
    while shift < POOL_WINDOWS[-1]:
        s = s + jnp.where(rows < lp - shift, pltpu.roll(s, lp - shift, 0), 0.0)
        shift *= 2
        if shift in POOL_WINDOWS:
            levels.append(s)
    return _window_select(levels, gidx) - dy


def _pool_fwd(z, pool_w, pool_scale, name, comm=None):
    lp = z.shape[0]
    ng, gw, _ = pool_w.shape

    def body(p_ref, w_ref, s_ref, o_ref):
        pooled, _ = _pool_window_mean_minus_id(p_ref[...], pl.program_id(0))
        o_ref[...] = (_dot(pooled.astype(BF), w_ref[...]) * s_ref[...]).astype(BF)

    return _call(
        body, grid=(ng,),
        in_specs=[pl.BlockSpec((lp, gw), lambda g: (0, g)), pl.BlockSpec((None, gw, gw), lambda g: (g, 0, 0)),
                  pl.BlockSpec((1, gw), lambda g: (0, g))],
        out_specs=pl.BlockSpec((lp, gw), lambda g: (0, g)), out_shape=SDS((lp, ng * gw), BF),
        vmem_mib=48, name=name, comm=comm)(z, pool_w, pool_scale)


def _pool_bwd(z, dmix, pool_w, pool_scale, name, comm=None):
    lp = z.shape[0]
    ng, gw, _ = pool_w.shape

    def body(p_ref, d_ref, w_ref, s_ref, dz_ref, dw_ref, ds_ref):
        g = pl.program_id(0)
        pooled, cnt = _pool_window_mean_minus_id(p_ref[...], g)
        pooled_b = pooled.astype(BF)
        w = w_ref[...]
        mixed = _dot(pooled_b, w)
        dpo = d_ref[...].astype(F32)
        ds_ref[...] = _rows8(dpo * mixed)
        dmixed = (dpo * s_ref[...]).astype(BF)
        dw_ref[...] = _dot_tn(pooled_b, dmixed)
        dpooled = _dot_nt(dmixed, w)
        dz_ref[...] = _pool_window_transpose(dpooled, cnt, g).astype(BF)

    return _call(
        body, grid=(ng,),
        in_specs=[pl.BlockSpec((lp, gw), lambda g: (0, g)), pl.BlockSpec((lp, gw), lambda g: (0, g)),
                  pl.BlockSpec((None, gw, gw), lambda g: (g, 0, 0)), pl.BlockSpec((1, gw), lambda g: (0, g))],
        out_specs=[pl.BlockSpec((lp, gw), lambda g: (0, g)), pl.BlockSpec((None, gw, gw), lambda g: (g, 0, 0)),
                   pl.BlockSpec((SUBLANES, gw), lambda g: (0, g))],
        out_shape=[SDS((lp, ng * gw), BF), SDS((ng, gw, gw), F32), SDS((SUBLANES, ng * gw), F32)],
        vmem_mib=48, name=name, comm=comm)(z, dmix, pool_w, pool_scale)


def _log_sigmoid(x):
    return jnp.minimum(x, 0.0) - jnp.log(1.0 + jnp.exp(-jnp.abs(x)))


def _fox_prep(z, bfp, fblk, name, comm=None):
    lp = z.shape[0]
    nb = lp // LANES

    def body(f_ref, b_ref, cum_ref):
        r = lax.broadcasted_iota(jnp.int32, (LANES, LANES), 0)
        c = lax.broadcasted_iota(jnp.int32, (LANES, LANES), 1)
        tri = (r >= c).astype(F32)
        carry = jnp.zeros((1, LANES), F32)
        for blk in range(nb):
            sl = slice(blk * LANES, (blk + 1) * LANES)
            lf = _log_sigmoid(f_ref[sl, :] + b_ref[...])
            cb = jnp.dot(tri, lf, preferred_element_type=F32, precision=lax.Precision.HIGHEST) + carry
            cum_ref[sl, :] = cb
            carry = cb[LANES - 1:LANES, :]

    return _call(
        body, grid=(1,),
        in_specs=[pl.BlockSpec((lp, LANES), lambda i: (0, fblk)), pl.BlockSpec((1, LANES), lambda i: (0, 0))],
        out_specs=pl.BlockSpec((lp, LANES), lambda i: (0, 0)), out_shape=SDS((lp, LANES), F32),
        vmem_mib=32, name=name, comm=comm)(z, bfp)


def _fox_bwd(z, bfp, dcum, fblk, name, comm=None):
    lp = z.shape[0]
    nb = lp // LANES

    def body(f_ref, b_ref, dc_ref, dz_ref, db_ref):
        r = lax.broadcasted_iota(jnp.int32, (LANES, LANES), 0)
        c = lax.broadcasted_iota(jnp.int32, (LANES, LANES), 1)
        tri = (r <= c).astype(F32)
        carry = jnp.zeros((1, LANES), F32)
        acc = jnp.zeros((SUBLANES, LANES), F32)
        for blk in range(nb - 1, -1, -1):
            sl = slice(blk * LANES, (blk + 1) * LANES)
            dlf = jnp.dot(tri, dc_ref[sl, :], preferred_element_type=F32, precision=lax.Precision.HIGHEST) + carry
            carry = dlf[0:1, :]
            df = dlf * jax.nn.sigmoid(-(f_ref[sl, :] + b_ref[...]))
            dz_ref[sl, :] = df.astype(BF)
            acc = acc + _rows8(df)
        db_ref[...] = acc

    return _call(
        body, grid=(1,),
        in_specs=[pl.BlockSpec((lp, LANES), lambda i: (0, fblk)), pl.BlockSpec((1, LANES), lambda i: (0, 0)),
                  pl.BlockSpec((lp, LANES), lambda i: (0, 0))],
        out_specs=[pl.BlockSpec((lp, LANES), lambda i: (0, 0)), pl.BlockSpec((SUBLANES, LANES), lambda i: (0, 0))],
        out_shape=[SDS((lp, LANES), BF), SDS((SUBLANES, LANES), F32)],
        vmem_mib=32, name=name, comm=comm)(z, bfp, dcum)


def _att_scores(q_ref, cum_ref, cumt_ref, qw_ref, kn_s, h, i, tq, lk):
    scale = 1.0 / (HEAD_DIM ** 0.5)
    q = q_ref[...]
    rq = _rstd(q)
    qhat = q * rq
    qn = (qhat * qw_ref[...]).astype(BF)
    s = _dot_nt(qn, kn_s[0:lk, :]) * scale
    lane = lax.broadcasted_iota(jnp.int32, (tq, LANES), 1)
    cq = jnp.sum(jnp.where(lane == h, cum_ref[...], 0.0), axis=1, keepdims=True)
    ck = cumt_ref[pl.ds(h, 1), 0:lk]
    s = s + (cq - ck)
    qpos = i * tq + lax.broadcasted_iota(jnp.int32, (tq, lk), 0)
    kpos = lax.broadcasted_iota(jnp.int32, (tq, lk), 1)
    s = jnp.where(qpos >= kpos, s, NEG_BIG)
    e = jnp.exp(s - jnp.max(s, axis=1, keepdims=True))
    p = e * (1.0 / jnp.sum(e, axis=1, keepdims=True))
    return p, qn, qhat, rq


def _per_query_tile(i, nq, tq, lp, fn):
    for t in range(nq):
        lk = min(lp, -(-((t + 1) * tq) // LANES) * LANES)
        pl.when(i == t)(functools.partial(fn, lk))


def _att_fwd(z, cum, cumt, qw, kw, n_heads, qblk0, tq, name, comm=None):
    lp = z.shape[0]
    nh = n_heads

    def body(q_ref, k_ref, v_ref, cum_ref, cumt_ref, qw_ref, kw_ref, o_ref, kn_s, vb_s):
        h, i = pl.program_id(0), pl.program_id(1)

        @pl.when(i == 0)
        def _():
            k = k_ref[...]
            kn_s[...] = (k * _rstd(k) * kw_ref[...]).astype(BF)
            vb_s[...] = v_ref[...].astype(BF)

        def tile(lk):
            p, _, _, _ = _att_scores(q_ref, cum_ref, cumt_ref, qw_ref, kn_s, h, i, tq, lk)
            o_ref[...] = _dot(p.astype(BF), vb_s[0:lk, :]).astype(BF)

        _per_query_tile(i, lp // tq, tq, lp, tile)

    vec = pl.BlockSpec((1, HEAD_DIM), lambda h, i: (0, 0))
    return _call(
        body, grid=(nh, lp // tq),
        in_specs=[pl.BlockSpec((tq, HEAD_DIM), lambda h, i: (i, qblk0 + h)),
                  pl.BlockSpec((lp, HEAD_DIM), lambda h, i: (0, qblk0 + nh + h)),
                  pl.BlockSpec((lp, HEAD_DIM), lambda h, i: (0, qblk0 + 2 * nh + h)),
                  pl.BlockSpec((tq, LANES), lambda h, i: (i, 0)),
                  pl.BlockSpec((nh, lp), lambda h, i: (0, 0)), vec, vec],
        out_specs=pl.BlockSpec((tq, HEAD_DIM), lambda h, i: (i, h)),
        out_shape=SDS((lp, nh * HEAD_DIM), BF),
        scratch_shapes=[pltpu.VMEM((lp, HEAD_DIM), BF), pltpu.VMEM((lp, HEAD_DIM), BF)],
        vmem_mib=48, name=name, comm=comm)(z, z, z, cum, cumt, qw, kw)


def _att_bwd(z, cum, cumt, qw, kw, dmix, n_heads, qblk0, oblk0, tq, name, comm=None):
    lp = z.shape[0]
    nh = n_heads
    nq = lp // tq
    scale = 1.0 / (HEAD_DIM ** 0.5)

    def body(q_ref, k_ref, v_ref, cum_ref, cumt_ref, qw_ref, kw_ref, do_ref,
             dq_ref, dk_ref, dv_ref, dck_ref, dqw_ref, dkw_ref,
             kn_s, vb_s, dkn_s, dv_s, dck_s):
        h, i = pl.program_id(0), pl.program_id(1)

        @pl.when((h == 0) & (i == 0))
        def _():
            dqw_ref[...] = jnp.zeros_like(dqw_ref)
            dkw_ref[...] = jnp.zeros_like(dkw_ref)

        @pl.when(i == 0)
        def _():
            k = k_ref[...]
            kn_s[...] = (k * _rstd(k) * kw_ref[...]).astype(BF)
            vb_s[...] = v_ref[...].astype(BF)
            dkn_s[...] = jnp.zeros_like(dkn_s)
            dv_s[...] = jnp.zeros_like(dv_s)
            dck_s[...] = jnp.zeros_like(dck_s)

        def tile(lk):
            p, qn, qhat, rq = _att_scores(q_ref, cum_ref, cumt_ref, qw_ref, kn_s, h, i, tq, lk)
            dob = do_ref[...]
            dp = _dot_nt(dob, vb_s[0:lk, :])
            ds = p * (dp - jnp.sum(p * dp, axis=1, keepdims=True))
            dsb = ds.astype(BF)
            dv_s[0:lk, :] += _dot_tn(p.astype(BF), dob)
            dkn_s[0:lk, :] += _dot_tn(dsb, qn)
            dck_s[:, 0:lk] += jnp.sum(ds, axis=0, keepdims=True)
            dqn = _dot(dsb, kn_s[0:lk, :]) * scale
            gq = dqn * qw_ref[...]
            dq_ref[...] = (rq * (gq - qhat * jnp.mean(gq * qhat, axis=-1, keepdims=True))).astype(BF)
            dqw_ref[...] += _rows8(dqn * qhat)

        _per_query_tile(i, nq, tq, lp, tile)

        @pl.when(i == nq - 1)
        def _():
            k = k_ref[...]
            rk = _rstd(k)
            khat = k * rk
            dkn = dkn_s[...] * scale
            gk = dkn * kw_ref[...]
            dk_ref[...] = (rk * (gk - khat * jnp.mean(gk * khat, axis=-1, keepdims=True))).astype(BF)
            dkw_ref[...] += _rows8(dkn * khat)
            dv_ref[...] = dv_s[...].astype(BF)
            dck_ref[...] = dck_s[...]

    vec = pl.BlockSpec((1, HEAD_DIM), lambda h, i: (0, 0))
    part = pl.BlockSpec((SUBLANES, LANES), lambda h, i: (0, 0))
    return _call(
        body, grid=(nh, nq),
        in_specs=[pl.BlockSpec((tq, HEAD_DIM), lambda h, i: (i, qblk0 + h)),
                  pl.BlockSpec((lp, HEAD_DIM), lambda h, i: (0, qblk0 + nh + h)),
                  pl.BlockSpec((lp, HEAD_DIM), lambda h, i: (0, qblk0 + 2 * nh + h)),
                  pl.BlockSpec((tq, LANES), lambda h, i: (i, 0)),
                  pl.BlockSpec((nh, lp), lambda h, i: (0, 0)), vec, vec,
                  pl.BlockSpec((tq, HEAD_DIM), lambda h, i: (i, oblk0 + h))],
        out_specs=[pl.BlockSpec((tq, HEAD_DIM), lambda h, i: (i, h)),
                   pl.BlockSpec((lp, HEAD_DIM), lambda h, i: (0, h)),
                   pl.BlockSpec((lp, HEAD_DIM), lambda h, i: (0, h)),
                   pl.BlockSpec((None, 1, lp), lambda h, i: (h, 0, 0)),
                   part, part],
        out_shape=[SDS((lp, nh * HEAD_DIM), BF)] * 3
        + [SDS((nh, 1, lp), F32), SDS((SUBLANES, LANES), F32), SDS((SUBLANES, LANES), F32)],
        scratch_shapes=[pltpu.VMEM((lp, HEAD_DIM), BF), pltpu.VMEM((lp, HEAD_DIM), BF),
                        pltpu.VMEM((lp, HEAD_DIM), F32), pltpu.VMEM((lp, HEAD_DIM), F32),
                        pltpu.VMEM((1, lp), F32)],
        vmem_mib=56, name=name, comm=comm)(z, z, z, cum, cumt, qw, kw, dmix)


def _adamw_math(w, g, m, v):
    m2 = ADAM_B1 * m + (1.0 - ADAM_B1) * g
    v2 = ADAM_B2 * v + (1.0 - ADAM_B2) * (g * g)
    m_hat = m2 / (1.0 - ADAM_B1 ** ADAM_STEP)
    v_hat = v2 / (1.0 - ADAM_B2 ** ADAM_STEP)
    delta = -ADAM_LR * (m_hat / (jnp.sqrt(v_hat) + ADAM_EPS) + ADAM_WD * w)
    return delta, m2, v2


def _adamw(g_in, w, m, v, name, comm=None):
    r, c = w.shape
    partial_sum = g_in.ndim == 3
    lane_padded = -(-c // LANES) * LANES
    tr = _largest_tile(r, max(16, MIB // (4 * lane_padded) // 16 * 16), 16)

    def body(g_ref, w_ref, m_ref, v_ref, go_ref, d_ref, mo_ref, vo_ref):
        if partial_sum:
            g = g_ref[0].astype(F32)
            for k in range(1, g_in.shape[0]):
                g = g + g_ref[k].astype(F32)
        else:
            g = g_ref[...]
        delta, m2, v2 = _adamw_math(w_ref[...], g, m_ref[...], v_ref[...])
        go_ref[...] = g
        d_ref[...] = delta
        mo_ref[...] = m2
        vo_ref[...] = v2

    blk = pl.BlockSpec((tr, c), lambda i: (i, 0))
    g_spec = pl.BlockSpec((g_in.shape[0], tr, c), lambda i: (0, i, 0)) if partial_sum else blk
    return _call(
        body, grid=(r // tr,), in_specs=[g_spec, blk, blk, blk], out_specs=[blk] * 4,
        out_shape=[SDS((r, c), F32)] * 4, vmem_mib=40, name=name, comm=comm)(g_in, w, m, v)


def _peer(x, y, c, k):
    return (1 - x if k & 4 else x, 1 - y if k & 2 else y, 1 - c if k & 1 else c)


_SIBLING = 1
_ICI_RELS = (2, 4, 6)


def _mesh_pos():
    return lax.axis_index("x"), lax.axis_index("y"), lax.axis_index("c")


def _sem_pair(sems, t, j, n_rel, scalars):
    if scalars:
        i = 2 * (t * n_rel + j)
        return sems[i], sems[i + 1]
    return sems[0].at[t, j], sems[1].at[t, j]


def _dev(pos):
    return 4 * pos[0] + 2 * pos[1] + pos[2]


def _gather_ici(shards, landing=None, rels=(_SIBLING,) + _ICI_RELS):
    n = len(shards)

    def remote(ins, outs, sems, arrival):
        x, y, c = _mesh_pos()
        dst = ins[n:] if landing is not None else outs
        cps = []
        for j, k in enumerate(rels):
            peer = _peer(x, y, c, k)
            slot = _dev(peer) if arrival else _dev((x, y, c))
            for t in range(n):
                send_sem, recv_sem = _sem_pair(sems, t, j, len(rels), landing is not None)
                cps.append(pltpu.make_async_remote_copy(
                    src_ref=ins[t], dst_ref=dst[t].at[slot], send_sem=send_sem, recv_sem=recv_sem,
                    device_id=peer, device_id_type=pl.DeviceIdType.MESH))
        return cps

    if landing is not None:
        def start_remote(ins, outs, sems):
            for cp in remote(ins, outs, sems, False):
                cp.start()

        def finish_remote(ins, outs, sems):
            for cp in remote(ins, outs, sems, True):
                cp.wait_recv()
            for cp in remote(ins, outs, sems, False):
                cp.wait_send()

        return _Comm(list(shards) + list(landing), [SDS(a.shape, a.dtype) for a in landing],
                     [pltpu.SemaphoreType.DMA(())] * (2 * n * len(rels)),
                     start_remote, finish_remote, aliases={n + t: t for t in range(n)})

    def local(ins, outs, sems):
        me = _dev(_mesh_pos())
        return [pltpu.make_async_copy(ins[t], outs[t].at[me], sems[2].at[t]) for t in range(n)]

    def start(ins, outs, sems):
        for cp in local(ins, outs, sems) + remote(ins, outs, sems, False):
            cp.start()

    def finish(ins, outs, sems):
        for cp in local(ins, outs, sems):
            cp.wait()
        for cp in remote(ins, outs, sems, True):
            cp.wait_recv()
        for cp in remote(ins, outs, sems, False):
            cp.wait_send()

    return _Comm(shards, [SDS((N_DEV,) + s.shape, s.dtype) for s in shards],
                 [pltpu.SemaphoreType.DMA((n, len(rels))), pltpu.SemaphoreType.DMA((n, len(rels))),
                  pltpu.SemaphoreType.DMA((n,))], start, finish)


def _gather_diagonal(zones):
    n = len(zones)

    def copies(ins, outs, sems, arrival):
        x, y, c = _mesh_pos()
        y_nb, x_nb, diag = _peer(x, y, c, 2), _peer(x, y, c, 4), _peer(x, y, c, 6)
        cps = []
        for j, (to, origin) in enumerate(((y_nb, x_nb), (x_nb, y_nb))):
            slot = _dev(diag) if arrival else _dev(origin)
            for t in range(n):
                half = ins[t].shape[1] // 2
                rows = ins[t].at[slot, pl.ds(j * half, half)]
                send_sem, recv_sem = _sem_pair(sems, t, j, 2, True)
                cps.append(pltpu.make_async_remote_copy(
                    src_ref=rows, dst_ref=rows, send_sem=send_sem, recv_sem=recv_sem,
                    device_id=to, device_id_type=pl.DeviceIdType.MESH))
        return cps

    def start(ins, outs, sems):
        for cp in copies(ins, outs, sems, False):
            cp.start()

    def finish(ins, outs, sems):
        for cp in copies(ins, outs, sems, True):
            cp.wait_recv()
        for cp in copies(ins, outs, sems, False):
            cp.wait_send()

    return _Comm(list(zones), [SDS(a.shape, a.dtype) for a in zones], [pltpu.SemaphoreType.DMA(())] * (4 * n),
                 start, finish, aliases={t: t for t in range(n)})


def _gather_fwd(partial):
    n = len(partial)

    def copies(ins, outs, sems, arrival):
        x, y, c = _mesh_pos()
        sibling = _peer(x, y, c, _SIBLING)
        cps = []
        for j, k in enumerate(_ICI_RELS):
            slot = _dev(_peer(x, y, c, k | _SIBLING if arrival else k))
            for t in range(n):
                cps.append(pltpu.make_async_remote_copy(
                    src_ref=ins[t].at[slot], dst_ref=outs[t].at[slot], send_sem=sems[0].at[t, j],
                    recv_sem=sems[1].at[t, j], device_id=sibling, device_id_type=pl.DeviceIdType.MESH))
        return cps

    def start(ins, outs, sems):
        for cp in copies(ins, outs, sems, False):
            cp.start()

    def finish(ins, outs, sems):
        for cp in copies(ins, outs, sems, True):
            cp.wait_recv()
        for cp in copies(ins, outs, sems, False):
            cp.wait_send()

    return _Comm(partial, [SDS(a.shape, a.dtype) for a in partial],
                 [pltpu.SemaphoreType.DMA((n, len(_ICI_RELS)))] * 2, start, finish,
                 aliases={t: t for t in range(n)})


def _scatter_sibling(slots):
    n = len(slots)

    def copies(ins, outs, sems):
        x, y, c = _mesh_pos()
        return [pltpu.make_async_remote_copy(
            src_ref=ins[t].at[:, 1 - c], dst_ref=outs[t], send_sem=sems[0].at[t], recv_sem=sems[1].at[t],
            device_id=_peer(x, y, c, _SIBLING), device_id_type=pl.DeviceIdType.MESH) for t in range(n)]

    def start(ins, outs, sems):
        for cp in copies(ins, outs, sems):
            cp.start()

    def finish(ins, outs, sems):
        for cp in copies(ins, outs, sems):
            cp.wait()

    return _Comm(slots, [SDS((s.shape[0],) + s.shape[2:], s.dtype) for s in slots],
                 [pltpu.SemaphoreType.DMA((n,))] * 2, start, finish)


def _scatter_ici(chip_sums, landing=None):
    n = len(chip_sums)

    def remote(ins, outs, sems, arrival):
        x, y, c = _mesh_pos()
        dst = ins[n:] if landing is not None else outs
        cps = []
        for j, k in enumerate(_ICI_RELS):
            peer = _peer(x, y, c, k)
            theirs, mine = 2 * peer[0] + peer[1], 2 * x + y
            for t in range(n):
                send_sem, recv_sem = _sem_pair(sems, t, j, len(_ICI_RELS), landing is not None)
                cps.append(pltpu.make_async_remote_copy(
                    src_ref=ins[t].at[theirs], dst_ref=dst[t].at[theirs if arrival else mine],
                    send_sem=send_sem, recv_sem=recv_sem,
                    device_id=peer, device_id_type=pl.DeviceIdType.MESH))
        return cps

    if landing is not None:
        def start_remote(ins, outs, sems):
            for cp in remote(ins, outs, sems, False):
                cp.start()

        def finish_remote(ins, outs, sems):
            for cp in remote(ins, outs, sems, True):
                cp.wait_recv()
            for cp in remote(ins, outs, sems, False):
                cp.wait_send()

        return _Comm(list(chip_sums) + list(landing), [SDS(a.shape, a.dtype) for a in landing],
                     [pltpu.SemaphoreType.DMA(())] * (2 * n * len(_ICI_RELS)), start_remote, finish_remote,
                     aliases={n + t: t for t in range(n)})

    def local(ins, outs, sems):
        x, y, _ = _mesh_pos()
        return [pltpu.make_async_copy(ins[t].at[2 * x + y], outs[t].at[2 * x + y], sems[2].at[t]) for t in range(n)]

    def start(ins, outs, sems):
        for cp in local(ins, outs, sems) + remote(ins, outs, sems, False):
            cp.start()

    def finish(ins, outs, sems):
        for cp in local(ins, outs, sems):
            cp.wait()
        for cp in remote(ins, outs, sems, True):
            cp.wait_recv()
        for cp in remote(ins, outs, sems, False):
            cp.wait_send()

    return _Comm(chip_sums, [SDS(a.shape, a.dtype) for a in chip_sums],
                 [pltpu.SemaphoreType.DMA((n, len(_ICI_RELS))), pltpu.SemaphoreType.DMA((n, len(_ICI_RELS))),
                  pltpu.SemaphoreType.DMA((n,))], start, finish)


def _chip_sum(slots, from_sibling, core, name):
    nq, _, r, c = slots.shape
    tr = _largest_tile(r, 1024, 16)

    def body(core_ref, a_ref, b_ref, o_ref):
        o_ref[...] = (a_ref[...].astype(F32) + b_ref[...].astype(F32)).astype(BF)

    return pl.pallas_call(
        body,
        grid_spec=pltpu.PrefetchScalarGridSpec(
            num_scalar_prefetch=1, grid=(nq, r // tr),
            in_specs=[pl.BlockSpec((None, None, tr, c), lambda q, i, core_ref: (q, core_ref[0], i, 0)),
                      pl.BlockSpec((None, tr, c), lambda q, i, core_ref: (q, i, 0))],
            out_specs=pl.BlockSpec((None, tr, c), lambda q, i, core_ref: (q, i, 0))),
        out_shape=SDS((nq, r, c), BF), compiler_params=pltpu.CompilerParams(vmem_limit_bytes=40 * MIB),
        name=name)(core, slots, from_sibling)


def _small_reduce(pack_g, meta_g, loss_scale, name, comm=None):
    w = pack_g.shape[2]

    def body(p_ref, m_ref, tot_ref, meta_ref, loss_ref):
        acc = p_ref[0]
        macc = m_ref[0]
        for k in range(1, N_DEV):
            acc = acc + p_ref[k]
            macc = macc + m_ref[k]
        tot = jnp.sum(acc, axis=0, keepdims=True)
        tot_ref[...] = tot
        meta_ref[...] = macc
        loss_ref[...] = jnp.full((1, LANES), loss_scale * jnp.sum(tot[:, w - LANES:w]), F32)

    return pl.pallas_call(
        body, out_shape=[SDS((1, w), F32), SDS(meta_g.shape[1:], F32), SDS((1, LANES), F32)],
        compiler_params=pltpu.CompilerParams(vmem_limit_bytes=32 * MIB), name=name)(pack_g, meta_g)


def _local_step(x, target, sw, plan):
    s_len, d = x.shape
    n_heads, n_meta = plan.n_heads, plan.n_meta
    l = n_meta + s_len
    lp = -(-l // LANES) * LANES
    tm = _largest_tile(lp, 544, 16)
    tq = _largest_tile(lp, 272, 16)
    te = _largest_tile(lp, 272, 16)
    tmd = _largest_tile(d, 512, LANES)

    plan.at("start")
    x, target = plan.gate((x, target))
    zmeta, zpad = jnp.zeros((n_meta, d), F32), jnp.zeros((lp - l, d), F32)
    h0 = jnp.concatenate([zmeta, x, zpad], axis=0)
    tpad = jnp.concatenate([zmeta, target, zpad], axis=0)
    plan.at("landed", (h0, tpad))
    h0 = lax.dynamic_update_slice(h0, plan.weights("meta"), (0, 0))

    split = plan.ffn1_split()
    if split is None:
        wg1, wu1, wd1 = plan.weights("ffn1")
        h1, a1, b1, u1 = _ffn_fwd(h0, sw["ffn1_norm"], wg1, wu1, wd1, tm, "ffn1_fwd", plan.comm("ffn1_fwd"))
    else:
        carry = _ffn_fwd_part(h0, sw["ffn1_norm"], *plan.weights("ffn1_landing"), split[0], None, tm, "ffn1_fwd_a",
                              plan.order_tokens())
        plan.at("ffn1_mid", (carry[0],))
        wg1, wu1, wd1 = plan.weights("ffn1")
        h1, a1, b1 = _ffn_fwd_part(h0, sw["ffn1_norm"], wg1, wu1, wd1, split[1], carry, tm, "ffn1_fwd_b",
                                   plan.order_tokens())
        u1 = carry[3]
    fs = wg1.shape[1]
    plan.at("after_ffn1_fwd", (h1,))
    win, pw, wout = plan.weights("mix")
    nz = win.shape[1]
    p_w = sw["pool_scale"].shape[1]
    npb = p_w // LANES
    fblk = nz // LANES - 1
    tnz = _largest_tile(nz, 1408, LANES)
    qw, kw, bfp, ps = sw["q_norm"], sw["k_norm"], sw["b_forget"], sw["pool_scale"]
    z, u2 = _norm_matmul(h1, sw["mix_norm"], win, tm, tnz, "mix_in", plan.comm("mix_in"))
    cum = _fox_prep(z, bfp, fblk, "fox_prep")
    cumt = cum[:, :n_heads].T
    pool_o = _pool_fwd(z, pw, ps, "pool_fwd")
    att_o = _att_fwd(z, cum, cumt, qw, kw, n_heads, npb, tq, "att_fwd", plan.comm("att_fwd"))
    plan.at("after_att_fwd", (att_o,))
    h2 =_out_proj(h1, pool_o, att_o, wout, tm, "out_proj", plan.comm("out_proj"))
    wg2, wu2, wd2 = plan.weights("ffn2")
    h3, a2, b2, u3 = _ffn_fwd(h2, sw["ffn2_norm"], wg2, wu2, wd2, tm, "ffn2_fwd", plan.comm("ffn2_fwd"))
    dy, dob3, lsq = _loss_head(h3, tpad, n_meta, l, te, "loss_head")

    du3, da2, db2, hid2 = _ffn_bwd_dx(dob3, a2, b2, wg2, wu2, wd2, tm, "ffn2_bwd_dx", plan.comm("ffn2_bwd_dx"))
    dh2, dh2b, dn2 = _rms_bwd(du3, h2, sw["ffn2_norm"], dy, 1.0, te, "ffn2_rms_bwd")
    plan.grad("ffn2_w_gate", _matmul_tn(da2, u3, fs, d, "ffn2_dwg", plan.comm("ffn2_dwg")))
    plan.grad("ffn2_w_up", _matmul_tn(db2, u3, fs, d, "ffn2_dwu", plan.comm("ffn2_dwu")))
    plan.grad("ffn2_w_down", _matmul_tn(hid2, dob3, fs, d, "ffn2_dwd", plan.comm("ffn2_dwd")))
    plan.at("after_ffn2_dwd")

    dmix = _matmul_nt(dh2b, wout, tm, d, BF, "out_proj_bwd", plan.comm("out_proj_bwd"))
    plan.at("after_out_proj_bwd")
    tmp = _largest_tile(p_w, 512, LANES)
    plan.grad("w_out", jnp.concatenate([_matmul_tn(pool_o, dh2b, tmp, d, "dwout_pool"),
                                        _matmul_tn(att_o, dh2b, tmp, d, "dwout_att")], axis=0))
    dzp, dpw, dps = _pool_bwd(z, dmix, pw, ps, "pool_bwd")
    plan.grad("pool_w", dpw)
    plan.at("before_att_bwd")
    dq, dk, dv, dck, dqw, dkw = _att_bwd(z, cum, cumt, qw, kw, dmix, n_heads, npb, npb, tq, "att_bwd",
                                              plan.comm("att_bwd"))
    dcum = -dck[:, 0, :].T
    dcum = jnp.pad(dcum, ((0, 0), (0, LANES - n_heads)))
    dzf, dbf = _fox_bwd(z, bfp, dcum, fblk, "fox_bwd")
    dz = jnp.concatenate([dzp, dq, dk, dv, dzf], axis=1)
    plan.grad("w_in", _matmul_tn(u2, dz, tmd, tnz, "dwin", plan.comm("dwin")))
    du2 = _matmul_nt(dz, win, tm, tnz, F32, "mix_in_bwd", plan.comm("mix_in_bwd"))
    plan.at("before_ffn1_bwd_dx")
    dh1, dob1, dnm = _rms_bwd(du2, h1, sw["mix_norm"], dh2, 0.5, te, "mix_rms_bwd")

    du1, da1, db1, hid1 = _ffn_bwd_dx(dob1, a1, b1, wg1, wu1, wd1, tm, "ffn1_bwd_dx", plan.comm("ffn1_bwd_dx"))
    plan.grad("ffn1_w_gate", _matmul_tn(da1, u1, fs, d, "ffn1_dwg", plan.comm("ffn1_dwg")))
    plan.grad("ffn1_w_up", _matmul_tn(db1, u1, fs, d, "ffn1_dwu", plan.comm("ffn1_dwu")))
    plan.at("before_ffn1_dwd")
    plan.grad("ffn1_w_down", _matmul_tn(hid1, dob1, fs, d, "ffn1_dwd", plan.comm("ffn1_dwd")))
    plan.at("after_ffn1_dwd")
    dh0, _, dn1 = _rms_bwd(du1, h0, sw["ffn1_norm"], dh1, 1.0, te, "ffn1_rms_bwd", plan.comm("ffn1_rms_bwd"))

    small = [dn1, dnm, dn2, dps, dqw, dkw, dbf, lsq]
    return dh0[n_meta:l], dh0[:n_meta], small


_BIG = ("ffn1_w_gate", "ffn1_w_up", "ffn1_w_down", "w_in", "pool_w", "w_out", "ffn2_w_gate", "ffn2_w_up", "ffn2_w_down")
_SMALL = ("ffn1_norm", "mix_norm", "ffn2_norm", "pool_scale", "q_norm", "k_norm", "b_forget")
_ORDER = ("meta_tokens", "ffn1_norm", "ffn1_w_gate", "ffn1_w_up", "ffn1_w_down", "mix_norm", "w_in", "b_forget",
          "q_norm", "k_norm", "pool_w", "pool_scale", "w_out", "ffn2_norm", "ffn2_w_gate", "ffn2_w_up", "ffn2_w_down")


_FFN1 = ("ffn1_w_gate", "ffn1_w_up", "ffn1_w_down")
_FFN2 = ("ffn2_w_gate", "ffn2_w_up", "ffn2_w_down")
_MIX = ("w_in", "pool_w", "w_out")

_RIDES = {
    "out_proj": (("g2", _FFN2),),
    "ffn2_dwu": (("s1", ("ffn2_w_gate",)),),
    "ffn2_dwd": (("s1", ("ffn2_w_up",)),),
    "out_proj_bwd": (("s1", ("ffn2_w_down",)),),
    "mix_in_bwd": (("s1", _MIX),),
    "ffn1_dwu": (("s1", ("ffn1_w_gate",)),),
    "ffn1_dwd": (("s1", ("ffn1_w_up",)),),
    "ffn1_rms_bwd": (("s1", ("ffn1_w_down",)),),
}
_META = ("meta_tokens",)
_POINTS = {
    "start": (("start", "gm", _META), ("start", "g1a", _FFN1), ("gate", _MIX + _FFN2), ("prepare", "g1", _MIX),
              ("prepare", "g1", ("ffn2_w_down",)), ("prepare", "g1", ("ffn2_w_gate", "ffn2_w_up"))),
    "landed": (("wait", "gm", _META), ("wait", "g1a", _FFN1), ("start", "g1b", _FFN1), ("start", "g1", _MIX),
               ("start", "g1", ("ffn2_w_down",))),
    "ffn1_mid": (("wait", "g1b", _FFN1), ("alone", "g2", _FFN1)),
    "after_ffn1_fwd": (("wait", "g1", _MIX), ("start", "g1", ("ffn2_w_gate", "ffn2_w_up")), ("alone", "g2", _MIX)),
    "after_att_fwd": (("wait", "g1", ("ffn2_w_down",)), ("wait", "g1", ("ffn2_w_gate", "ffn2_w_up"))),
    "after_ffn2_dwd": (("sum", ("ffn2_w_gate",)), ("start", "s2", ("ffn2_w_gate",))),
    "after_out_proj_bwd": (("sum", ("ffn2_w_up",)), ("start", "s2", ("ffn2_w_up",))),
    "before_att_bwd": (("sum", ("ffn2_w_down",)), ("start", "s2", ("ffn2_w_down",))),
    "before_ffn1_bwd_dx": (("sum", _MIX), ("start", "s2", _MIX)),
    "before_ffn1_dwd": (("sum", ("ffn1_w_gate",)), ("start", "s2", ("ffn1_w_gate",))),
    "after_ffn1_dwd": (("sum", ("ffn1_w_up",)), ("start", "s2", ("ffn1_w_up",))),
    "after_ffn1_rms_bwd": (("sum", ("ffn1_w_down",)), ("start", "s2", ("ffn1_w_down",))),
    "before_adamw_ffn2_w_gate": (("wait", "s2", ("ffn2_w_gate",)),),
    "before_adamw_ffn2_w_up": (("wait", "s2", ("ffn2_w_up",)),),
    "before_adamw_ffn2_w_down": (("wait", "s2", ("ffn2_w_down",)),),
    "before_adamw_w_in": (("wait", "s2", _MIX),),
    "before_adamw_ffn1_w_gate": (("wait", "s2", ("ffn1_w_gate",)),),
    "before_adamw_ffn1_w_up": (("wait", "s2", ("ffn1_w_up",)),),
    "before_adamw_ffn1_w_down": (("wait", "s2", ("ffn1_w_down",)),),
}


def _own_slot_filled(block, slot, n_slots):
    zone = lax.empty((n_slots,) + block.shape, block.dtype)
    return lax.dynamic_update_slice(zone, block[None], (slot,) + (0,) * block.ndim)


class _MeshPlan:
    def __init__(self, raw, pos, d, d_in, n_heads):
        self.raw, self.pos = dict(raw), pos
        self.core = pos[2].astype(jnp.int32).reshape(1)
        self.d, self.d_in, self.n_heads, self.n_meta = d, d_in, n_heads, raw["meta_tokens"].shape[0]
        self.partial, self.full, self.slots, self.from_sibling, self.chip_sum, self.received = {}, {}, {}, {}, {}, {}
        self.partial_a, self.pending, self.prepared, self.started, self.tokens = {}, [], {}, {}, []

    def gate(self, arrays):
        gated = lax.optimization_barrier((self.tokens[-1], tuple(arrays)))
        self.tokens[-1] = gated[0]
        return gated[1]

    def _phase(self, kind, names):
        src, dst, make = {"g2": (self.partial, self.full, _gather_fwd),
                          "s1": (self.slots, self.from_sibling, _scatter_sibling),
                          "s2": (self.chip_sum, self.received, _scatter_ici)}[kind]
        op = make([src[n] for n in names])
        self.pending.append((op, dst, names))
        return op

    def _settle(self):
        for op, dst, names in self.pending:
            dst.update(zip(names, op.results))
        self.pending = []

    def _prepare(self, kind, names):
        x, y, c = self.pos
        if kind in ("g1", "g1a", "gm"):
            blocks = [self.raw[n] if kind == "gm" else self.raw[n].astype(BF) for n in names]
            rels = {"g1": (_SIBLING,) + _ICI_RELS, "g1a": (_SIBLING,) + _ICI_RELS[:2], "gm": tuple(range(1, N_DEV))}[kind]
            op = _gather_ici(blocks, [_own_slot_filled(b, 4 * x + 2 * y + c, N_DEV) for b in blocks], rels)
        elif kind == "g1b":
            op = _gather_diagonal([self.partial_a[n] for n in names])
        else:
            sums = [self.chip_sum[n] for n in names]
            mine = [lax.dynamic_index_in_dim(s, 2 * x + y, 0, keepdims=False) for s in sums]
            op = _scatter_ici(sums, [_own_slot_filled(b, 2 * x + y, N_DEV // 2) for b in mine])
        self.prepared[(kind, names)] = op

    def _start(self, kind, names):
        if (kind, names) not in self.prepared:
            self._prepare(kind, names)
        self._launch((kind, names), self.prepared.pop((kind, names)), "_".join(("start", kind, names[0])))

    def _launch(self, key, op, name):
        if self.tokens:
            op.arrs = list(self.gate(op.arrs))
        self.started[key], token = _split_start(op, name)
        self.tokens.append(token)

    def start_small_gather(self, arrays):
        x, y, c = self.pos
        zones = [_own_slot_filled(a, 4 * x + 2 * y + c, N_DEV) for a in arrays]
        self._launch("small", _gather_ici(list(arrays), zones, rels=tuple(range(1, N_DEV))), "start_gather_small")

    def wait_small_gather(self, afters):
        return _split_wait(self.started.pop("small"), afters, "wait_gather_small")

    def _wait(self, kind, names, afters):
        afters = list(afters) + [a for op in self.prepared.values() for a in op.arrs[len(op.arrs) // 2:]]
        landed = _split_wait(self.started.pop((kind, names)), afters, "_".join(("wait", kind, names[0])))
        {"g1": self.partial, "g1a": self.partial_a, "g1b": self.partial, "gm": self.partial,
         "s2": self.received}[kind].update(zip(names, landed))

    def ffn1_split(self):
        x, y, c = self.pos
        first = [(x, y, c), _peer(x, y, c, 1), _peer(x, y, c, 4), _peer(x, y, c, 2)]
        last = [_peer(x, y, c, 6), _peer(x, y, c, 5), _peer(x, y, c, 3), _peer(x, y, c, 7)]
        return tuple(jnp.stack([_dev(p) for p in part]).astype(jnp.int32) for part in (first, last))

    def order_tokens(self):
        tokens, self.tokens = self.tokens, []
        return tokens

    def comm(self, kernel_name):
        self._settle()
        ops = [self._phase(kind, names) for kind, names in _RIDES.get(kernel_name, ())]
        if self.tokens:
            ops.append(_Comm(self.tokens, [], [], lambda *a: None, lambda *a: None))
            self.tokens = []
        return _merge_comm(ops)

    def at(self, point, after=()):
        for step in _POINTS.get(point, ()):
            self._settle()
            if step[0] == "alone":
                _comm_alone(self._phase(step[1], step[2]), "_".join((step[1], point)))
            elif step[0] == "start":
                self._start(step[1], step[2])
            elif step[0] == "prepare":
                self._prepare(step[1], step[2])
            elif step[0] == "gate":
                self.raw.update(zip(step[1], self.gate([self.raw[n] for n in step[1]])))
            elif step[0] == "wait":
                self._wait(step[1], step[2], tuple(after) + tuple(self.tokens[-1:]))
            else:
                for n in step[1]:
                    self.chip_sum[n] = _chip_sum(self.slots[n], self.from_sibling[n], self.core, "chip_sum_" + n)

    def weights(self, group):
        self._settle()
        f, d = self.full, self.d
        if group == "meta":
            g = self.partial["meta_tokens"]
            return g.transpose(1, 0, 2).reshape(g.shape[1], d)
        if group == "ffn1_landing":
            return tuple(self.started[("g1b", _FFN1)][2])
        if group == "ffn1":
            return tuple(f[n] for n in _FFN1)
        if group == "ffn2":
            return tuple(f[n] for n in _FFN2)
        n_main = self.d_in - self.n_heads
        win = f["w_in"].transpose(1, 0, 2).reshape(d, self.d_in)
        win = jnp.concatenate([win[:, :n_main], jnp.pad(win[:, n_main:], ((0, 0), (0, LANES - self.n_heads)))], axis=1)
        pw = f["pool_w"]
        gw = pw.shape[2]
        pw = pw.reshape(N_DEV, -1, gw // N_DEV, gw).transpose(1, 0, 2, 3).reshape(-1, gw, gw)
        return win, pw, f["w_out"].reshape(-1, d)

    def grad(self, name, g):
        d = self.d
        if name == "w_in":
            g = g[:, :self.d_in].reshape(d, N_DEV, -1).transpose(1, 0, 2)
        elif name == "pool_w":
            ng, gw = g.shape[0], g.shape[2]
            g = g.astype(BF).reshape(ng, N_DEV, -1, gw).transpose(1, 0, 2, 3).reshape(N_DEV, -1, gw)
        elif name == "w_out":
            g = g.reshape(N_DEV, -1, d)
        self.slots[name] = g.reshape((N_DEV // 2, 2) + g.shape[1:])

    def gradient_parts(self, name):
        self._settle()
        return self.received[name]


_TRANSPOSED = ("ffn1_w_gate", "ffn1_w_up", "ffn2_w_gate", "ffn2_w_up")


def _as2d(name, a):
    return a[0].T if name in _TRANSPOSED else a.reshape(-1, a.shape[-1])


def _from2d(name, a2d, shape):
    return a2d.T.reshape(shape) if name in _TRANSPOSED else a2d.reshape(shape)


def kernel(x, meta_tokens, ffn1_norm, ffn1_w_gate, ffn1_w_up, ffn1_w_down, mix_norm, w_in, b_forget, q_norm, k_norm, pool_w, pool_scale, w_out, ffn2_norm, ffn2_w_gate, ffn2_w_up, ffn2_w_down, loss_target, m_meta_tokens, m_ffn1_norm, m_ffn1_w_gate, m_ffn1_w_up, m_ffn1_w_down, m_mix_norm, m_w_in, m_b_forget, m_q_norm, m_k_norm, m_pool_w, m_pool_scale, m_w_out, m_ffn2_norm, m_ffn2_w_gate, m_ffn2_w_up, m_ffn2_w_down, v_meta_tokens, v_ffn1_norm, v_ffn1_w_gate, v_ffn1_w_up, v_ffn1_w_down, v_mix_norm, v_w_in, v_b_forget, v_q_norm, v_k_norm, v_pool_w, v_pool_scale, v_w_out, v_ffn2_norm, v_ffn2_w_gate, v_ffn2_w_up, v_ffn2_w_down):
    w = dict(meta_tokens=meta_tokens, ffn1_norm=ffn1_norm, ffn1_w_gate=ffn1_w_gate, ffn1_w_up=ffn1_w_up,
             ffn1_w_down=ffn1_w_down, mix_norm=mix_norm, w_in=w_in, b_forget=b_forget, q_norm=q_norm, k_norm=k_norm,
             pool_w=pool_w, pool_scale=pool_scale, w_out=w_out, ffn2_norm=ffn2_norm, ffn2_w_gate=ffn2_w_gate,
             ffn2_w_up=ffn2_w_up, ffn2_w_down=ffn2_w_down)
    m = dict(meta_tokens=m_meta_tokens, ffn1_norm=m_ffn1_norm, ffn1_w_gate=m_ffn1_w_gate, ffn1_w_up=m_ffn1_w_up,
             ffn1_w_down=m_ffn1_w_down, mix_norm=m_mix_norm, w_in=m_w_in, b_forget=m_b_forget, q_norm=m_q_norm,
             k_norm=m_k_norm, pool_w=m_pool_w, pool_scale=m_pool_scale, w_out=m_w_out, ffn2_norm=m_ffn2_norm,
             ffn2_w_gate=m_ffn2_w_gate, ffn2_w_up=m_ffn2_w_up, ffn2_w_down=m_ffn2_w_down)
    v = dict(meta_tokens=v_meta_tokens, ffn1_norm=v_ffn1_norm, ffn1_w_gate=v_ffn1_w_gate, ffn1_w_up=v_ffn1_w_up,
             ffn1_w_down=v_ffn1_w_down, mix_norm=v_mix_norm, w_in=v_w_in, b_forget=v_b_forget, q_norm=v_q_norm,
             k_norm=v_k_norm, pool_w=v_pool_w, pool_scale=v_pool_scale, w_out=v_w_out, ffn2_norm=v_ffn2_norm,
             ffn2_w_gate=v_ffn2_w_gate, ffn2_w_up=v_ffn2_w_up, ffn2_w_down=v_ffn2_w_down)

    d = x.shape[-1]
    n_heads = b_forget.shape[-1]
    pos = (lax.axis_index("x"), lax.axis_index("y"), lax.axis_index("c"))
    me = 4 * pos[0] + 2 * pos[1] + pos[2]

    raw = {k: _as2d(k, w[k]) for k in _BIG}
    raw["meta_tokens"] = meta_tokens
    plan = _MeshPlan(raw, pos, d, N_DEV * w_in.shape[-1], n_heads)
    sw = {k: w[k] for k in _SMALL}
    sw["b_forget"] = jnp.pad(b_forget, ((0, 0), (0, LANES - n_heads)))
    dx, dmeta, small = _local_step(x[0], loss_target[0], sw, plan)

    res = {}
    last = dx

    plan.start_small_gather([jnp.concatenate(small, axis=1), dmeta])
    plan.at("after_ffn1_rms_bwd")

    def update_shards(names):
        nonlocal last
        for k in names:
            plan.at("before_adamw_" + k, (last,))
            res[k] = _adamw(plan.gradient_parts(k), _as2d(k, w[k]), _as2d(k, m[k]), _as2d(k, v[k]), "adamw_" + k,
                            plan.comm("adamw_" + k))
            last = res[k][0]

    update_shards(_FFN2 + _MIX + ("ffn1_w_gate", "ffn1_w_up"))

    pack_g, meta_g = plan.wait_small_gather((last,))
    tot, dmeta_tot, loss_row = _small_reduce(pack_g, meta_g, 0.5 / d, "small_reduce")

    mcols = meta_tokens.shape[1]
    g_meta = lax.dynamic_slice_in_dim(dmeta_tot, me * mcols, mcols, axis=1)
    res["meta_tokens"] = _adamw(g_meta, meta_tokens, m_meta_tokens, v_meta_tokens, "adamw_meta_tokens")

    def packed(src):
        return jnp.concatenate([src[k] for k in _SMALL[:-1]] + [jnp.pad(src["b_forget"], ((0, 0), (0, LANES - n_heads)))],
                               axis=1)

    wp = packed(w)
    sm = _adamw(tot[:, :wp.shape[1]], wp, packed(m), packed(v), "adamw_small")
    off = 0
    for k in _SMALL:
        width = w[k].shape[1]
        res[k] = tuple(o[:, off:off + width] for o in sm)
        off += width if k != "b_forget" else LANES

    last = sm[0]
    update_shards(("ffn1_w_down",))

    outs =[loss_row[0, 0], dx[None]]
    for idx in range(4):
        outs += [_from2d(k, res[k][idx], w[k].shape) for k in _ORDER]
    return tuple(outs)
```

```python
import functools

import jax
import jax.numpy as jnp
from jax import lax
from jax.experimental import pallas as pl
from jax.experimental.pallas import tpu as pltpu

F32 = jnp.float32
BF = jnp.bfloat16
SDS = jax.ShapeDtypeStruct

N_DEV = 8
LANES = 128
SUBLANES = 8
HEAD_DIM = 128
POOL_WINDOWS = (2, 4, 8, 16)
RMS_EPS = 1e-6
NEG_BIG = -1e30
MIB = 1024 * 1024

ADAM_LR = 0.001
ADAM_B1 = 0.9
ADAM_B2 = 0.999
ADAM_EPS = 1e-08
ADAM_WD = 0.01
ADAM_STEP = 10


class _Comm:
    def __init__(self, arrs, out_shape, sems, start, finish, aliases=None):
        self.arrs, self.out_shape, self.sems = list(arrs), list(out_shape), list(sems)
        self.start, self.finish, self.aliases = start, finish, dict(aliases or {})
        self.results = None


def _merge_comm(ops):
    ops = [op for op in ops if op is not None]
    if not ops:
        return None
    na, no, ns = [0], [0], [0]
    for op in ops:
        na.append(na[-1] + len(op.arrs))
        no.append(no[-1] + len(op.out_shape))
        ns.append(ns[-1] + len(op.sems))

    def parts(i, ins, outs, sems):
        return ins[na[i]:na[i + 1]], outs[no[i]:no[i + 1]], sems[ns[i]:ns[i + 1]]

    def start(ins, outs, sems):
        for i, op in enumerate(ops):
            op.start(*parts(i, ins, outs, sems))

    def finish(ins, outs, sems):
        for i, op in enumerate(ops):
            op.finish(*parts(i, ins, outs, sems))

    aliases = {}
    for i, op in enumerate(ops):
        for a, o in op.aliases.items():
            aliases[na[i] + a] = no[i] + o
    merged = _Comm([a for op in ops for a in op.arrs], [s for op in ops for s in op.out_shape],
                   [s for op in ops for s in op.sems], start, finish, aliases)
    merged.children = (ops, no)
    return merged


def _deliver(comm, results):
    comm.results = list(results)
    if hasattr(comm, "children"):
        ops, no = comm.children
        for i, op in enumerate(ops):
            _deliver(op, results[no[i]:no[i + 1]])


def _call(body, *, grid, in_specs, out_specs, out_shape, scratch_shapes=(), vmem_mib, name, comm=None):
    single = not isinstance(out_shape, (list, tuple))
    out_specs = [out_specs] if single else list(out_specs)
    out_shape = [out_shape] if single else list(out_shape)
    in_specs, scratch_shapes = list(in_specs), list(scratch_shapes)
    params = pltpu.CompilerParams(dimension_semantics=("arbitrary",) * len(grid), vmem_limit_bytes=vmem_mib * MIB)
    n_in, n_out, n_scr = len(in_specs), len(out_specs), len(scratch_shapes)

    def run(*args):
        if comm is None:
            res = pl.pallas_call(body, grid=grid, in_specs=in_specs, out_specs=out_specs, out_shape=out_shape,
                                 scratch_shapes=scratch_shapes, compiler_params=params, name=name)(*args)
            return res[0] if single else res
        ci, co = len(comm.arrs), len(comm.out_shape)

        def with_comm(*refs):
            ins, cins = refs[:n_in], refs[n_in:n_in + ci]
            o0 = n_in + ci
            outs, couts = refs[o0:o0 + n_out], refs[o0 + n_out:o0 + n_out + co]
            s0 = o0 + n_out + co
            scr, csems = refs[s0:s0 + n_scr], refs[s0 + n_scr:]
            ids = [pl.program_id(a) for a in range(len(grid))]
            first = functools.reduce(jnp.logical_and, [i == 0 for i in ids])
            last = functools.reduce(jnp.logical_and, [i == g - 1 for i, g in zip(ids, grid)])

            @pl.when(first)
            def _():
                comm.start(cins, couts, csems)

            body(*ins, *outs, *scr)

            @pl.when(last)
            def _():
                comm.finish(cins, couts, csems)

        anyspec = pl.BlockSpec(memory_space=pl.ANY)
        res = pl.pallas_call(
            with_comm, grid=grid, in_specs=in_specs + [anyspec] * ci, out_specs=out_specs + [anyspec] * co,
            out_shape=out_shape + comm.out_shape, scratch_shapes=scratch_shapes + comm.sems,
            input_output_aliases={n_in + a: n_out + o for a, o in comm.aliases.items()},
            compiler_params=params, name=name)(*args, *comm.arrs)
        _deliver(comm, res[n_out:])
        return res[0] if single else res[:n_out]

    return run


def _comm_alone(comm, name):
    def body(*refs):
        ci, co = len(comm.arrs), len(comm.out_shape)
        ins, outs, sems = refs[:ci], refs[ci:ci + co], refs[ci + co:]
        comm.start(ins, outs, sems)
        comm.finish(ins, outs, sems)

    anyspec = pl.BlockSpec(memory_space=pl.ANY)
    res = pl.pallas_call(
        body, in_specs=[anyspec] * len(comm.arrs), out_specs=[anyspec] * len(comm.out_shape),
        out_shape=comm.out_shape, scratch_shapes=comm.sems, input_output_aliases=comm.aliases, name=name)(*comm.arrs)
    _deliver(comm, res)


def _split_start(comm, name):
    na, ns = len(comm.arrs), len(comm.sems)

    def body(*refs):
        comm.start(refs[:na], None, refs[na:na + ns])
        token = refs[-1]
        token[...] = jnp.zeros_like(token)

    hbm = pl.BlockSpec(memory_space=pltpu.HBM)
    res = pl.pallas_call(
        body, name=name,
        out_shape=tuple(comm.sems) + tuple(pltpu.HBM(a.shape, a.dtype) for a in comm.arrs)
        + (SDS((SUBLANES, LANES), F32),),
        in_specs=[hbm] * na,
        out_specs=[pl.BlockSpec(memory_space=pltpu.SEMAPHORE)] * ns + [hbm] * na + [pl.BlockSpec(memory_space=pltpu.VMEM)],
        input_output_aliases={i: ns + i for i in range(na)},
        compiler_params=pltpu.CompilerParams(has_side_effects=pltpu.SideEffectType.DATAFLOW_SIDE_EFFECTING),
    )(*[pltpu.with_memory_space_constraint(a, pltpu.HBM) for a in comm.arrs])
    return (comm, res[:ns], res[ns:ns + na]), res[-1]


def _split_wait(started, afters, name):
    comm, sems, thru = started
    na, ns = len(thru), len(sems)
    afters = list(afters)

    def body(*refs):
        comm.finish(refs[:na], None, refs[na:na + ns])

    hbm = pl.BlockSpec(memory_space=pltpu.HBM)
    res = pl.pallas_call(
        body, name=name, out_shape=tuple(pltpu.HBM(a.shape, a.dtype) for a in thru),
        in_specs=[hbm] * na + [pl.BlockSpec(memory_space=pltpu.SEMAPHORE)] * ns
        + [pl.BlockSpec(memory_space=pl.ANY)] * len(afters),
        out_specs=[hbm] * na, input_output_aliases={i: i for i in range(na)},
        compiler_params=pltpu.CompilerParams(has_side_effects=pltpu.SideEffectType.DATAFLOW_SIDE_EFFECTING),
    )(*thru, *sems, *afters)
    return res[na - len(comm.out_shape):]


def _largest_tile(n, cap, mult):
    if n <= cap:
        return n
    best = None
    for t in range(mult, cap + 1, mult):
        if n % t == 0:
            best = t
    assert best is not None, (n, cap, mult)
    return best


def _dot(a, b):
    return jnp.dot(a, b, preferred_element_type=F32)


def _dot_nt(a, b):
    return lax.dot_general(a, b, (((1,), (1,)), ((), ())), preferred_element_type=F32)


def _dot_tn(a, b):
    return lax.dot_general(a, b, (((0,), (0,)), ((), ())), preferred_element_type=F32)


def _rows8(x):
    t, c = x.shape
    return jnp.sum(x.reshape(t // SUBLANES, SUBLANES, c), axis=0)


def _rstd(x):
    return lax.rsqrt(jnp.mean(x * x, axis=-1, keepdims=True) + RMS_EPS)


def _ffn_fwd(h, g, wg, wu, wd, tm, name, comm=None):
    lp, d = h.shape
    ns, fs, _ = wg.shape

    def body(h_ref, g_ref, wg_ref, wu_ref, wd_ref, out_ref, a_ref, b_ref, u_ref, acc_ref):
        j = pl.program_id(1)

        @pl.when(j == 0)
        def _():
            hh = h_ref[...]
            u_ref[...] = (hh * _rstd(hh) * g_ref[...]).astype(BF)
            acc_ref[...] = jnp.zeros_like(acc_ref)

        u = u_ref[...]
        a = _dot_nt(u, wg_ref[...])
        b = _dot_nt(u, wu_ref[...])
        a_ref[...] = a.astype(BF)
        b_ref[...] = b.astype(BF)
        hid = (a * jax.nn.sigmoid(a) * b).astype(BF)
        acc_ref[...] += _dot(hid, wd_ref[...])

        @pl.when(j == ns - 1)
        def _():
            out_ref[...] = h_ref[...] + 0.5 * acc_ref[...]

    row = pl.BlockSpec((tm, d), lambda i, j: (i, 0))
    act = pl.BlockSpec((None, tm, fs), lambda i, j: (j, i, 0))
    return _call(
        body, grid=(lp // tm, ns),
        in_specs=[row, pl.BlockSpec((1, d), lambda i, j: (0, 0)),
                  pl.BlockSpec((None, fs, d), lambda i, j: (j, 0, 0)),
                  pl.BlockSpec((None, fs, d), lambda i, j: (j, 0, 0)),
                  pl.BlockSpec((None, fs, d), lambda i, j: (j, 0, 0))],
        out_specs=[row, act, act, row],
        out_shape=[SDS((lp, d), F32), SDS((ns, lp, fs), BF), SDS((ns, lp, fs), BF), SDS((lp, d), BF)],
        scratch_shapes=[pltpu.VMEM((tm, d), F32)],
        vmem_mib=56, name=name, comm=comm)(h, g, wg, wu, wd)


def _ffn_fwd_part(h, g, wg, wu, wd, order, carry, tm, name, deps=()):
    lp, d = h.shape
    fs = wg.shape[1]
    k = order.shape[0]
    first = carry is None
    n_in = 5 if first else 8

    def body(order_ref, *refs):
        outs = refs[n_in + len(deps):]
        if first:
            h_ref, g_ref, wg_ref, wu_ref, wd_ref = refs[:n_in]
            out_ref, a_ref, b_ref, u_ref, acc_ref = outs
        else:
            h_ref, acc_in_ref, u_ref, _, _, wg_ref, wu_ref, wd_ref = refs[:n_in]
            out_ref, a_ref, b_ref, acc_ref = outs
        j = pl.program_id(1)

        @pl.when(j == 0)
        def _():
            if first:
                hh = h_ref[...]
                u_ref[...] = (hh * _rstd(hh) * g_ref[...]).astype(BF)
                acc_ref[...] = jnp.zeros_like(acc_ref)
            else:
                acc_ref[...] = acc_in_ref[...]

        u = u_ref[...]
        a = _dot_nt(u, wg_ref[...])
        b = _dot_nt(u, wu_ref[...])
        a_ref[...] = a.astype(BF)
        b_ref[...] = b.astype(BF)
        hid = (a * jax.nn.sigmoid(a) * b).astype(BF)
        acc_ref[...] += _dot(hid, wd_ref[...])

        @pl.when(j == k - 1)
        def _():
            out_ref[...] = acc_ref[...] if first else h_ref[...] + 0.5 * acc_ref[...]

    row = pl.BlockSpec((tm, d), lambda i, j, o: (i, 0))
    act = pl.BlockSpec((None, tm, fs), lambda i, j, o: (o[j], i, 0))
    wsp = pl.BlockSpec((None, fs, d), lambda i, j, o: (o[j], 0, 0))
    anyspec = pl.BlockSpec(memory_space=pl.ANY)
    acts = [SDS((wg.shape[0], lp, fs), BF)] * 2
    if first:
        in_specs = [row, pl.BlockSpec((1, d), lambda i, j, o: (0, 0)), wsp, wsp, wsp]
        out_specs, out_shape = [row, act, act, row], [SDS((lp, d), F32)] + acts + [SDS((lp, d), BF)]
        args, aliases = (h, g, wg, wu, wd), {}
    else:
        acc, a_prev, b_prev, u_prev = carry
        in_specs = [row, row, row, anyspec, anyspec, wsp, wsp, wsp]
        out_specs, out_shape = [row, act, act], [SDS((lp, d), F32)] + acts
        args, aliases = (h, acc, u_prev, a_prev, b_prev, wg, wu, wd), {4: 1, 5: 2}
    return pl.pallas_call(
        body,
        grid_spec=pltpu.PrefetchScalarGridSpec(
            num_scalar_prefetch=1, grid=(lp // tm, k), in_specs=in_specs + [anyspec] * len(deps),
            out_specs=out_specs, scratch_shapes=[pltpu.VMEM((tm, d), F32)]),
        out_shape=out_shape, input_output_aliases=aliases,
        compiler_params=pltpu.CompilerParams(dimension_semantics=("arbitrary",) * 2, vmem_limit_bytes=60 * MIB),
        name=name)(order, *args, *deps)


def _ffn_bwd_dx(dob, a, b, wg, wu, wd, tm, name, comm=None):
    lp, d = dob.shape
    ns, fs, _ = wg.shape

    def body(do_ref, a_ref, b_ref, wg_ref, wu_ref, wd_ref, du_ref, da_ref, db_ref, hid_ref):
        j = pl.program_id(1)

        @pl.when(j == 0)
        def _():
            du_ref[...] = jnp.zeros_like(du_ref)

        dhid = _dot_nt(do_ref[...], wd_ref[...])
        av = a_ref[...].astype(F32)
        bv = b_ref[...].astype(F32)
        sig = jax.nn.sigmoid(av)
        sil = av * sig
        dbv = (dhid * sil).astype(BF)
        dav = (dhid * bv * (sig * (1.0 + av * (1.0 - sig)))).astype(BF)
        hid_ref[...] = (sil * bv).astype(BF)
        da_ref[...] = dav
        db_ref[...] = dbv
        du_ref[...] += _dot(dav, wg_ref[...]) + _dot(dbv, wu_ref[...])

    row = pl.BlockSpec((tm, d), lambda i, j: (i, 0))
    act = pl.BlockSpec((None, tm, fs), lambda i, j: (j, i, 0))
    return _call(
        body, grid=(lp // tm, ns),
        in_specs=[row, act, act,
                  pl.BlockSpec((None, fs, d), lambda i, j: (j, 0, 0)),
                  pl.BlockSpec((None, fs, d), lambda i, j: (j, 0, 0)),
                  pl.BlockSpec((None, fs, d), lambda i, j: (j, 0, 0))],
        out_specs=[row, act, act, act],
        out_shape=[SDS((lp, d), F32)] + [SDS((ns, lp, fs), BF)] * 3,
        vmem_mib=56, name=name, comm=comm)(dob, a, b, wg, wu, wd)


def _rms_bwd(du, h, g, dres, bscale, tm, name, comm=None):
    lp, d = h.shape

    def body(du_ref, h_ref, g_ref, dres_ref, dh_ref, dhb_ref, dg_ref):
        @pl.when(pl.program_id(0) == 0)
        def _():
            dg_ref[...] = jnp.zeros_like(dg_ref)

        hh = h_ref[...]
        r = _rstd(hh)
        xhat = hh * r
        duv = du_ref[...]
        dg_ref[...] += _rows8(duv * xhat)
        dxh = duv * g_ref[...]
        dh = dres_ref[...] + r * (dxh - xhat * jnp.mean(dxh * xhat, axis=-1, keepdims=True))
        dh_ref[...] = dh
        dhb_ref[...] = (bscale * dh).astype(BF)

    row = pl.BlockSpec((tm, d), lambda i: (i, 0))
    return _call(
        body, grid=(lp // tm,),
        in_specs=[row, row, pl.BlockSpec((1, d), lambda i: (0, 0)), row],
        out_specs=[row, row, pl.BlockSpec((SUBLANES, d), lambda i: (0, 0))],
        out_shape=[SDS((lp, d), F32), SDS((lp, d), BF), SDS((SUBLANES, d), F32)],
        vmem_mib=48, name=name, comm=comm)(du, h, g, dres)


def _matmul_tn(a, b, tm, tn, name, comm=None):
    a_b, b_b = a.ndim == 3, b.ndim == 3
    ns = a.shape[0] if a_b else (b.shape[0] if b_b else 1)
    l, m = a.shape[-2:]
    n = b.shape[-1]

    def body(a_ref, b_ref, o_ref):
        o_ref[...] = _dot_tn(a_ref[...], b_ref[...]).astype(o_ref.dtype)

    a_spec = (pl.BlockSpec((None, l, tm), lambda s, i, j: (s, 0, i)) if a_b
              else pl.BlockSpec((l, tm), lambda s, i, j: (0, i)))
    b_spec = (pl.BlockSpec((None, l, tn), lambda s, i, j: (s, 0, j)) if b_b
              else pl.BlockSpec((l, tn), lambda s, i, j: (0, j)))
    batched = a_b or b_b
    o_spec = (pl.BlockSpec((None, tm, tn), lambda s, i, j: (s, i, j)) if batched
              else pl.BlockSpec((tm, tn), lambda s, i, j: (i, j)))
    o_shape = SDS((ns, m, n), BF) if batched else SDS((m, n), BF)
    return _call(
        body, grid=(ns, m // tm, n // tn), in_specs=[a_spec, b_spec], out_specs=o_spec, out_shape=o_shape,
        vmem_mib=48, name=name, comm=comm)(a, b)


def _matmul_nt(x, w, tm, tk, out_dtype, name, comm=None):
    l, k = x.shape
    n = w.shape[0]
    nk = k // tk

    def body(x_ref, w_ref, o_ref, acc_ref):
        kk = pl.program_id(1)

        @pl.when(kk == 0)
        def _():
            acc_ref[...] = jnp.zeros_like(acc_ref)

        acc_ref[...] += _dot_nt(x_ref[...], w_ref[...])

        @pl.when(kk == nk - 1)
        def _():
            o_ref[...] = acc_ref[...].astype(o_ref.dtype)

    return _call(
        body, grid=(l // tm, nk),
        in_specs=[pl.BlockSpec((tm, tk), lambda i, kk: (i, kk)), pl.BlockSpec((n, tk), lambda i, kk: (0, kk))],
        out_specs=pl.BlockSpec((tm, n), lambda i, kk: (i, 0)),
        out_shape=SDS((l, n), out_dtype),
        scratch_shapes=[pltpu.VMEM((tm, n), F32)],
        vmem_mib=48, name=name, comm=comm)(x, w)


def _norm_matmul(h, g, w, tm, tn, name, comm=None):
    lp, d = h.shape
    n = w.shape[1]

    def body(h_ref, g_ref, w_ref, z_ref, u_ref):
        @pl.when(pl.program_id(1) == 0)
        def _():
            hh = h_ref[...]
            u_ref[...] = (hh * _rstd(hh) * g_ref[...]).astype(BF)

        z_ref[...] = _dot(u_ref[...], w_ref[...])

    row = pl.BlockSpec((tm, d), lambda i, j: (i, 0))
    return _call(
        body, grid=(lp // tm, n // tn),
        in_specs=[row, pl.BlockSpec((1, d), lambda i, j: (0, 0)), pl.BlockSpec((d, tn), lambda i, j: (0, j))],
        out_specs=[pl.BlockSpec((tm, tn), lambda i, j: (i, j)), row],
        out_shape=[SDS((lp, n), F32), SDS((lp, d), BF)],
        vmem_mib=48, name=name, comm=comm)(h, g, w)


def _out_proj(h, pool_o, att_o, w_out, tm, name, comm=None):
    lp, d = h.shape
    p = pool_o.shape[1]
    dm = w_out.shape[0]

    def body(h_ref, p_ref, a_ref, w_ref, o_ref):
        o_ref[...] = h_ref[...] + _dot(p_ref[...], w_ref[0:p, :]) + _dot(a_ref[...], w_ref[p:dm, :])

    row = pl.BlockSpec((tm, d), lambda i: (i, 0))
    return _call(
        body, grid=(lp // tm,),
        in_specs=[row, pl.BlockSpec((tm, p), lambda i: (i, 0)), pl.BlockSpec((tm, dm - p), lambda i: (i, 0)),
                  pl.BlockSpec((dm, d), lambda i: (0, 0))],
        out_specs=row, out_shape=SDS((lp, d), F32),
        vmem_mib=48, name=name, comm=comm)(h, pool_o, att_o, w_out)


def _loss_head(y, tpad, row0, row1, tm, name, comm=None):
    lp, d = y.shape

    def body(y_ref, t_ref, dy_ref, dob_ref, ls_ref):
        i = pl.program_id(0)

        @pl.when(i == 0)
        def _():
            ls_ref[...] = jnp.zeros_like(ls_ref)

        rows = i * tm + lax.broadcasted_iota(jnp.int32, (tm, d), 0)
        err = jnp.where((rows >= row0) & (rows < row1), y_ref[...] - t_ref[...], 0.0)
        dy = err * (1.0 / d)
        dy_ref[...] = dy
        dob_ref[...] = (0.5 * dy).astype(BF)
        sq = _rows8(err * err)
        acc = sq[:, 0:LANES]
        for c in range(1, d // LANES):
            acc = acc + sq[:, c * LANES:(c + 1) * LANES]
        ls_ref[...] += acc

    row = pl.BlockSpec((tm, d), lambda i: (i, 0))
    return _call(
        body, grid=(lp // tm,), in_specs=[row, row],
        out_specs=[row, row, pl.BlockSpec((SUBLANES, LANES), lambda i: (0, 0))],
        out_shape=[SDS((lp, d), F32), SDS((lp, d), BF), SDS((SUBLANES, LANES), F32)],
        vmem_mib=48, name=name, comm=comm)(y, tpad)


def _window_select(levels, gidx):
    out = levels[-1]
    for k in range(len(levels) - 2, -1, -1):
        out = jnp.where(gidx == k, levels[k], out)
    return out


def _pool_window_mean_minus_id(x, gidx):
    rows = lax.broadcasted_iota(jnp.int32, x.shape, 0)
    levels = []
    s = x
    shift = 1
    while shift < POOL_WINDOWS[-1]:
        s = s + jnp.where(rows >= shift, pltpu.roll(s, shift, 0), 0.0)
        shift *= 2
        if shift in POOL_WINDOWS:
            levels.append(s)
    win = _window_select(levels, gidx)
    cnt = jnp.minimum(rows + 1, _window_select(list(POOL_WINDOWS), gidx)).astype(F32)
    return win / cnt - x, cnt


def _pool_window_transpose(dy, cnt, gidx):
    lp = dy.shape[0]
    rows = lax.broadcasted_iota(jnp.int32, dy.shape, 0)
    levels = []
    s = dy / cnt
    shift = 1
    while shift < POOL_WINDOWS[-1]:
        s = s + jnp.where(rows < lp - shift, pltpu.roll(s, lp - shift, 0), 0.0)
        shift *= 2
        if shift in POOL_WINDOWS:
            levels.append(s)
    return _window_select(levels, gidx) - dy


def _pool_fwd(z, pool_w, pool_scale, name, comm=None):
    lp = z.shape[0]
    ng, gw, _ = pool_w.shape

    def body(p_ref, w_ref, s_ref, o_ref):
        pooled, _ = _pool_window_mean_minus_id(p_ref[...], pl.program_id(0))
        o_ref[...] = (_dot(pooled.astype(BF), w_ref[...]) * s_ref[...]).astype(BF)

    return _call(
        body, grid=(ng,),
        in_specs=[pl.BlockSpec((lp, gw), lambda g: (0, g)), pl.BlockSpec((None, gw, gw), lambda g: (g, 0, 0)),
                  pl.BlockSpec((1, gw), lambda g: (0, g))],
        out_specs=pl.BlockSpec((lp, gw), lambda g: (0, g)), out_shape=SDS((lp, ng * gw), BF),
        vmem_mib=48, name=name, comm=comm)(z, pool_w, pool_scale)


def _pool_bwd(z, dmix, pool_w, pool_scale, name, comm=None):
    lp = z.shape[0]
    ng, gw, _ = pool_w.shape

    def body(p_ref, d_ref, w_ref, s_ref, dz_ref, dw_ref, ds_ref):
        g = pl.program_id(0)
        pooled, cnt = _pool_window_mean_minus_id(p_ref[...], g)
        pooled_b = pooled.astype(BF)
        w = w_ref[...]
        mixed = _dot(pooled_b, w)
        dpo = d_ref[...].astype(F32)
        ds_ref[...] = _rows8(dpo * mixed)
        dmixed = (dpo * s_ref[...]).astype(BF)
        dw_ref[...] = _dot_tn(pooled_b, dmixed)
        dpooled = _dot_nt(dmixed, w)
        dz_ref[...] = _pool_window_transpose(dpooled, cnt, g).astype(BF)

    return _call(
        body, grid=(ng,),
        in_specs=[pl.BlockSpec((lp, gw), lambda g: (0, g)), pl.BlockSpec((lp, gw), lambda g: (0, g)),
                  pl.BlockSpec((None, gw, gw), lambda g: (g, 0, 0)), pl.BlockSpec((1, gw), lambda g: (0, g))],
        out_specs=[pl.BlockSpec((lp, gw), lambda g: (0, g)), pl.BlockSpec((None, gw, gw), lambda g: (g, 0, 0)),
                   pl.BlockSpec((SUBLANES, gw), lambda g: (0, g))],
        out_shape=[SDS((lp, ng * gw), BF), SDS((ng, gw, gw), F32), SDS((SUBLANES, ng * gw), F32)],
        vmem_mib=48, name=name, comm=comm)(z, dmix, pool_w, pool_scale)


def _log_sigmoid(x):
    return jnp.minimum(x, 0.0) - jnp.log(1.0 + jnp.exp(-jnp.abs(x)))


def _fox_prep(z, bfp, fblk, name, comm=None):
    lp = z.shape[0]
    nb = lp // LANES

    def body(f_ref, b_ref, cum_ref):
        r = lax.broadcasted_iota(jnp.int32, (LANES, LANES), 0)
        c = lax.broadcasted_iota(jnp.int32, (LANES, LANES), 1)
        tri = (r >= c).astype(F32)
        carry = jnp.zeros((1, LANES), F32)
        for blk in range(nb):
            sl = slice(blk * LANES, (blk + 1) * LANES)
            lf = _log_sigmoid(f_ref[sl, :] + b_ref[...])
            cb = jnp.dot(tri, lf, preferred_element_type=F32, precision=lax.Precision.HIGHEST) + carry
            cum_ref[sl, :] = cb
            carry = cb[LANES - 1:LANES, :]

    return _call(
        body, grid=(1,),
        in_specs=[pl.BlockSpec((lp, LANES), lambda i: (0, fblk)), pl.BlockSpec((1, LANES), lambda i: (0, 0))],
        out_specs=pl.BlockSpec((lp, LANES), lambda i: (0, 0)), out_shape=SDS((lp, LANES), F32),
        vmem_mib=32, name=name, comm=comm)(z, bfp)


def _fox_bwd(z, bfp, dcum, fblk, name, comm=None):
    lp = z.shape[0]
    nb = lp // LANES

    def body(f_ref, b_ref, dc_ref, dz_ref, db_ref):
        r = lax.broadcasted_iota(jnp.int32, (LANES, LANES), 0)
        c = lax.broadcasted_iota(jnp.int32, (LANES, LANES), 1)
        tri = (r <= c).astype(F32)
        carry = jnp.zeros((1, LANES), F32)
        acc = jnp.zeros((SUBLANES, LANES), F32)
        for blk in range(nb - 1, -1, -1):
            sl = slice(blk * LANES, (blk + 1) * LANES)
            dlf = jnp.dot(tri, dc_ref[sl, :], preferred_element_type=F32, precision=lax.Precision.HIGHEST) + carry
            carry = dlf[0:1, :]
            df = dlf * jax.nn.sigmoid(-(f_ref[sl, :] + b_ref[...]))
            dz_ref[sl, :] = df.astype(BF)
            acc = acc + _rows8(df)
        db_ref[...] = acc

    return _call(
        body, grid=(1,),
        in_specs=[pl.BlockSpec((lp, LANES), lambda i: (0, fblk)), pl.BlockSpec((1, LANES), lambda i: (0, 0)),
                  pl.BlockSpec((lp, LANES), lambda i: (0, 0))],
        out_specs=[pl.BlockSpec((lp, LANES), lambda i: (0, 0)), pl.BlockSpec((SUBLANES, LANES), lambda i: (0, 0))],
        out_shape=[SDS((lp, LANES), BF), SDS((SUBLANES, LANES), F32)],
        vmem_mib=32, name=name, comm=comm)(z, bfp, dcum)


def _att_scores(q_ref, cum_ref, cumt_ref, qw_ref, kn_s, h, i, tq, lk):
    scale = 1.0 / (HEAD_DIM ** 0.5)
    q = q_ref[...]
    rq = _rstd(q)
    qhat = q * rq
    qn = (qhat * qw_ref[...]).astype(BF)
    s = _dot_nt(qn, kn_s[0:lk, :]) * scale
    lane = lax.broadcasted_iota(jnp.int32, (tq, LANES), 1)
    cq = jnp.sum(jnp.where(lane == h, cum_ref[...], 0.0), axis=1, keepdims=True)
    ck = cumt_ref[pl.ds(h, 1), 0:lk]
    s = s + (cq - ck)
    qpos = i * tq + lax.broadcasted_iota(jnp.int32, (tq, lk), 0)
    kpos = lax.broadcasted_iota(jnp.int32, (tq, lk), 1)
    s = jnp.where(qpos >= kpos, s, NEG_BIG)
    e = jnp.exp(s - jnp.max(s, axis=1, keepdims=True))
    p = e * (1.0 / jnp.sum(e, axis=1, keepdims=True))
    return p, qn, qhat, rq


def _per_query_tile(i, nq, tq, lp, fn):
    for t in range(nq):
        lk = min(lp, -(-((t + 1) * tq) // LANES) * LANES)
        pl.when(i == t)(functools.partial(fn, lk))


def _att_fwd(z, cum, cumt, qw, kw, n_heads, qblk0, tq, name, comm=None):
    lp = z.shape[0]
    nh = n_heads

    def body(q_ref, k_ref, v_ref, cum_ref, cumt_ref, qw_ref, kw_ref, o_ref, kn_s, vb_s):
        h, i = pl.program_id(0), pl.program_id(1)

        @pl.when(i == 0)
        def _():
            k = k_ref[...]
            kn_s[...] = (k * _rstd(k) * kw_ref[...]).astype(BF)
            vb_s[...] = v_ref[...].astype(BF)

        def tile(lk):
            p, _, _, _ = _att_scores(q_ref, cum_ref, cumt_ref, qw_ref, kn_s, h, i, tq, lk)
            o_ref[...] = _dot(p.astype(BF), vb_s[0:lk, :]).astype(BF)

        _per_query_tile(i, lp // tq, tq, lp, tile)

    vec = pl.BlockSpec((1, HEAD_DIM), lambda h, i: (0, 0))
    return _call(
        body, grid=(nh, lp // tq),
        in_specs=[pl.BlockSpec((tq, HEAD_DIM), lambda h, i: (i, qblk0 + h)),
                  pl.BlockSpec((lp, HEAD_DIM), lambda h, i: (0, qblk0 + nh + h)),
                  pl.BlockSpec((lp, HEAD_DIM), lambda h, i: (0, qblk0 + 2 * nh + h)),
                  pl.BlockSpec((tq, LANES), lambda h, i: (i, 0)),
                  pl.BlockSpec((nh, lp), lambda h, i: (0, 0)), vec, vec],
        out_specs=pl.BlockSpec((tq, HEAD_DIM), lambda h, i: (i, h)),
        out_shape=SDS((lp, nh * HEAD_DIM), BF),
        scratch_shapes=[pltpu.VMEM((lp, HEAD_DIM), BF), pltpu.VMEM((lp, HEAD_DIM), BF)],
        vmem_mib=48, name=name, comm=comm)(z, z, z, cum, cumt, qw, kw)


def _att_bwd(z, cum, cumt, qw, kw, dmix, n_heads, qblk0, oblk0, tq, name, comm=None):
    lp = z.shape[0]
    nh = n_heads
    nq = lp // tq
    scale = 1.0 / (HEAD_DIM ** 0.5)

    def body(q_ref, k_ref, v_ref, cum_ref, cumt_ref, qw_ref, kw_ref, do_ref,
             dq_ref, dk_ref, dv_ref, dck_ref, dqw_ref, dkw_ref,
             kn_s, vb_s, dkn_s, dv_s, dck_s):
        h, i = pl.program_id(0), pl.program_id(1)

        @pl.when((h == 0) & (i == 0))
        def _():
            dqw_ref[...] = jnp.zeros_like(dqw_ref)
            dkw_ref[...] = jnp.zeros_like(dkw_ref)

        @pl.when(i == 0)
        def _():
            k = k_ref[...]
            kn_s[...] = (k * _rstd(k) * kw_ref[...]).astype(BF)
            vb_s[...] = v_ref[...].astype(BF)
            dkn_s[...] = jnp.zeros_like(dkn_s)
            dv_s[...] = jnp.zeros_like(dv_s)
            dck_s[...] = jnp.zeros_like(dck_s)

        def tile(lk):
            p, qn, qhat, rq = _att_scores(q_ref, cum_ref, cumt_ref, qw_ref, kn_s, h, i, tq, lk)
            dob = do_ref[...]
            dp = _dot_nt(dob, vb_s[0:lk, :])
            ds = p * (dp - jnp.sum(p * dp, axis=1, keepdims=True))
            dsb = ds.astype(BF)
            dv_s[0:lk, :] += _dot_tn(p.astype(BF), dob)
            dkn_s[0:lk, :] += _dot_tn(dsb, qn)
            dck_s[:, 0:lk] += jnp.sum(ds, axis=0, keepdims=True)
            dqn = _dot(dsb, kn_s[0:lk, :]) * scale
            gq = dqn * qw_ref[...]
            dq_ref[...] = (rq * (gq - qhat * jnp.mean(gq * qhat, axis=-1, keepdims=True))).astype(BF)
            dqw_ref[...] += _rows8(dqn * qhat)

        _per_query_tile(i, nq, tq, lp, tile)

        @pl.when(i == nq - 1)
        def _():
            k = k_ref[...]
            rk = _rstd(k)
            khat = k * rk
            dkn = dkn_s[...] * scale
            gk = dkn * kw_ref[...]
            dk_ref[...] = (rk * (gk - khat * jnp.mean(gk * khat, axis=-1, keepdims=True))).astype(BF)
            dkw_ref[...] += _rows8(dkn * khat)
            dv_ref[...] = dv_s[...].astype(BF)
            dck_ref[...] = dck_s[...]

    vec = pl.BlockSpec((1, HEAD_DIM), lambda h, i: (0, 0))
    part = pl.BlockSpec((SUBLANES, LANES), lambda h, i: (0, 0))
    return _call(
        body, grid=(nh, nq),
        in_specs=[pl.BlockSpec((tq, HEAD_DIM), lambda h, i: (i, qblk0 + h)),
                  pl.BlockSpec((lp, HEAD_DIM), lambda h, i: (0, qblk0 + nh + h)),
                  pl.BlockSpec((lp, HEAD_DIM), lambda h, i: (0, qblk0 + 2 * nh + h)),
                  pl.BlockSpec((tq, LANES), lambda h, i: (i, 0)),
                  pl.BlockSpec((nh, lp), lambda h, i: (0, 0)), vec, vec,
                  pl.BlockSpec((tq, HEAD_DIM), lambda h, i: (i, oblk0 + h))],
        out_specs=[pl.BlockSpec((tq, HEAD_DIM), lambda h, i: (i, h)),
                   pl.BlockSpec((lp, HEAD_DIM), lambda h, i: (0, h)),
                   pl.BlockSpec((lp, HEAD_DIM), lambda h, i: (0, h)),
                   pl.BlockSpec((None, 1, lp), lambda h, i: (h, 0, 0)),
                   part, part],
        out_shape=[SDS((lp, nh * HEAD_DIM), BF)] * 3
        + [SDS((nh, 1, lp), F32), SDS((SUBLANES, LANES), F32), SDS((SUBLANES, LANES), F32)],
        scratch_shapes=[pltpu.VMEM((lp, HEAD_DIM), BF), pltpu.VMEM((lp, HEAD_DIM), BF),
                        pltpu.VMEM((lp, HEAD_DIM), F32), pltpu.VMEM((lp, HEAD_DIM), F32),
                        pltpu.VMEM((1, lp), F32)],
        vmem_mib=56, name=name, comm=comm)(z, z, z, cum, cumt, qw, kw, dmix)


def _adamw_math(w, g, m, v):
    m2 = ADAM_B1 * m + (1.0 - ADAM_B1) * g
    v2 = ADAM_B2 * v + (1.0 - ADAM_B2) * (g * g)
    m_hat = m2 / (1.0 - ADAM_B1 ** ADAM_STEP)
    v_hat = v2 / (1.0 - ADAM_B2 ** ADAM_STEP)
    delta = -ADAM_LR * (m_hat / (jnp.sqrt(v_hat) + ADAM_EPS) + ADAM_WD * w)
    return delta, m2, v2


def _adamw(g_in, w, m, v, name, comm=None):
    r, c = w.shape
    partial_sum = g_in.ndim == 3
    lane_padded = -(-c // LANES) * LANES
    tr = _largest_tile(r, max(16, MIB // (4 * lane_padded) // 16 * 16), 16)

    def body(g_ref, w_ref, m_ref, v_ref, go_ref, d_ref, mo_ref, vo_ref):
        if partial_sum:
            g = g_ref[0].astype(F32)
            for k in range(1, g_in.shape[0]):
                g = g + g_ref[k].astype(F32)
        else:
            g = g_ref[...]
        delta, m2, v2 = _adamw_math(w_ref[...], g, m_ref[...], v_ref[...])
        go_ref[...] = g
        d_ref[...] = delta
        mo_ref[...] = m2
        vo_ref[...] = v2

    blk = pl.BlockSpec((tr, c), lambda i: (i, 0))
    g_spec = pl.BlockSpec((g_in.shape[0], tr, c), lambda i: (0, i, 0)) if partial_sum else blk
    return _call(
        body, grid=(r // tr,), in_specs=[g_spec, blk, blk, blk], out_specs=[blk] * 4,
        out_shape=[SDS((r, c), F32)] * 4, vmem_mib=40, name=name, comm=comm)(g_in, w, m, v)


def _peer(x, y, c, k):
    return (1 - x if k & 4 else x, 1 - y if k & 2 else y, 1 - c if k & 1 else c)


_SIBLING = 1
_ICI_RELS = (2, 4, 6)


def _mesh_pos():
    return lax.axis_index("x"), lax.axis_index("y"), lax.axis_index("c")


def _sem_pair(sems, t, j, n_rel, scalars):
    if scalars:
        i = 2 * (t * n_rel + j)
        return sems[i], sems[i + 1]
    return sems[0].at[t, j], sems[1].at[t, j]


def _dev(pos):
    return 4 * pos[0] + 2 * pos[1] + pos[2]


def _gather_ici(shards, landing=None, rels=(_SIBLING,) + _ICI_RELS):
    n = len(shards)

    def remote(ins, outs, sems, arrival):
        x, y, c = _mesh_pos()
        dst = ins[n:] if landing is not None else outs
        cps = []
        for j, k in enumerate(rels):
            peer = _peer(x, y, c, k)
            slot = _dev(peer) if arrival else _dev((x, y, c))
            for t in range(n):
                send_sem, recv_sem = _sem_pair(sems, t, j, len(rels), landing is not None)
                cps.append(pltpu.make_async_remote_copy(
                    src_ref=ins[t], dst_ref=dst[t].at[slot], send_sem=send_sem, recv_sem=recv_sem,
                    device_id=peer, device_id_type=pl.DeviceIdType.MESH))
        return cps

    if landing is not None:
        def start_remote(ins, outs, sems):
            for cp in remote(ins, outs, sems, False):
                cp.start()

        def finish_remote(ins, outs, sems):
            for cp in remote(ins, outs, sems, True):
                cp.wait_recv()
            for cp in remote(ins, outs, sems, False):
                cp.wait_send()

        return _Comm(list(shards) + list(landing), [SDS(a.shape, a.dtype) for a in landing],
                     [pltpu.SemaphoreType.DMA(())] * (2 * n * len(rels)),
                     start_remote, finish_remote, aliases={n + t: t for t in range(n)})

    def local(ins, outs, sems):
        me = _dev(_mesh_pos())
        return [pltpu.make_async_copy(ins[t], outs[t].at[me], sems[2].at[t]) for t in range(n)]

    def start(ins, outs, sems):
        for cp in local(ins, outs, sems) + remote(ins, outs, sems, False):
            cp.start()

    def finish(ins, outs, sems):
        for cp in local(ins, outs, sems):
            cp.wait()
        for cp in remote(ins, outs, sems, True):
            cp.wait_recv()
        for cp in remote(ins, outs, sems, False):
            cp.wait_send()

    return _Comm(shards, [SDS((N_DEV,) + s.shape, s.dtype) for s in shards],
                 [pltpu.SemaphoreType.DMA((n, len(rels))), pltpu.SemaphoreType.DMA((n, len(rels))),
                  pltpu.SemaphoreType.DMA((n,))], start, finish)


def _gather_diagonal(zones):
    n = len(zones)

    def copies(ins, outs, sems, arrival):
        x, y, c = _mesh_pos()
        y_nb, x_nb, diag = _peer(x, y, c, 2), _peer(x, y, c, 4), _peer(x, y, c, 6)
        cps = []
        for j, (to, origin) in enumerate(((y_nb, x_nb), (x_nb, y_nb))):
            slot = _dev(diag) if arrival else _dev(origin)
            for t in range(n):
                half = ins[t].shape[1] // 2
                rows = ins[t].at[slot, pl.ds(j * half, half)]
                send_sem, recv_sem = _sem_pair(sems, t, j, 2, True)
                cps.append(pltpu.make_async_remote_copy(
                    src_ref=rows, dst_ref=rows, send_sem=send_sem, recv_sem=recv_sem,
                    device_id=to, device_id_type=pl.DeviceIdType.MESH))
        return cps

    def start(ins, outs, sems):
        for cp in copies(ins, outs, sems, False):
            cp.start()

    def finish(ins, outs, sems):
        for cp in copies(ins, outs, sems, True):
            cp.wait_recv()
        for cp in copies(ins, outs, sems, False):
            cp.wait_send()

    return _Comm(list(zones), [SDS(a.shape, a.dtype) for a in zones], [pltpu.SemaphoreType.DMA(())] * (4 * n),
                 start, finish, aliases={t: t for t in range(n)})


def _gather_fwd(partial):
    n = len(partial)

    def copies(ins, outs, sems, arrival):
        x, y, c = _mesh_pos()
        sibling = _peer(x, y, c, _SIBLING)
        cps = []
        for j, k in enumerate(_ICI_RELS):
            slot = _dev(_peer(x, y, c, k | _SIBLING if arrival else k))
            for t in range(n):
                cps.append(pltpu.make_async_remote_copy(
                    src_ref=ins[t].at[slot], dst_ref=outs[t].at[slot], send_sem=sems[0].at[t, j],
                    recv_sem=sems[1].at[t, j], device_id=sibling, device_id_type=pl.DeviceIdType.MESH))
        return cps

    def start(ins, outs, sems):
        for cp in copies(ins, outs, sems, False):
            cp.start()

    def finish(ins, outs, sems):
        for cp in copies(ins, outs, sems, True):
            cp.wait_recv()
        for cp in copies(ins, outs, sems, False):
            cp.wait_send()

    return _Comm(partial, [SDS(a.shape, a.dtype) for a in partial],
                 [pltpu.SemaphoreType.DMA((n, len(_ICI_RELS)))] * 2, start, finish,
                 aliases={t: t for t in range(n)})


def _scatter_sibling(slots):
    n = len(slots)

    def copies(ins, outs, sems):
        x, y, c = _mesh_pos()
        return [pltpu.make_async_remote_copy(
            src_ref=ins[t].at[:, 1 - c], dst_ref=outs[t], send_sem=sems[0].at[t], recv_sem=sems[1].at[t],
            device_id=_peer(x, y, c, _SIBLING), device_id_type=pl.DeviceIdType.MESH) for t in range(n)]

    def start(ins, outs, sems):
        for cp in copies(ins, outs, sems):
            cp.start()

    def finish(ins, outs, sems):
        for cp in copies(ins, outs, sems):
            cp.wait()

    return _Comm(slots, [SDS((s.shape[0],) + s.shape[2:], s.dtype) for s in slots],
                 [pltpu.SemaphoreType.DMA((n,))] * 2, start, finish)


def _scatter_ici(chip_sums, landing=None):
    n = len(chip_sums)

    def remote(ins, outs, sems, arrival):
        x, y, c = _mesh_pos()
        dst = ins[n:] if landing is not None else outs
        cps = []
        for j, k in enumerate(_ICI_RELS):
            peer = _peer(x, y, c, k)
            theirs, mine = 2 * peer[0] + peer[1], 2 * x + y
            for t in range(n):
                send_sem, recv_sem = _sem_pair(sems, t, j, len(_ICI_RELS), landing is not None)
                cps.append(pltpu.make_async_remote_copy(
                    src_ref=ins[t].at[theirs], dst_ref=dst[t].at[theirs if arrival else mine],
                    send_sem=send_sem, recv_sem=recv_sem,
                    device_id=peer, device_id_type=pl.DeviceIdType.MESH))
        return cps

    if landing is not None:
        def start_remote(ins, outs, sems):
            for cp in remote(ins, outs, sems, False):
                cp.start()

        def finish_remote(ins, outs, sems):
            for cp in remote(ins, outs, sems, True):
                cp.wait_recv()
            for cp in remote(ins, outs, sems, False):
                cp.wait_send()

        return _Comm(list(chip_sums) + list(landing), [SDS(a.shape, a.dtype) for a in landing],
                     [pltpu.SemaphoreType.DMA(())] * (2 * n * len(_ICI_RELS)), start_remote, finish_remote,
                     aliases={n + t: t for t in range(n)})

    def local(ins, outs, sems):
        x, y, _ = _mesh_pos()
        return [pltpu.make_async_copy(ins[t].at[2 * x + y], outs[t].at[2 * x + y], sems[2].at[t]) for t in range(n)]

    def start(ins, outs, sems):
        for cp in local(ins, outs, sems) + remote(ins, outs, sems, False):
            cp.start()

    def finish(ins, outs, sems):
        for cp in local(ins, outs, sems):
            cp.wait()
        for cp in remote(ins, outs, sems, True):
            cp.wait_recv()
        for cp in remote(ins, outs, sems, False):
            cp.wait_send()

    return _Comm(chip_sums, [SDS(a.shape, a.dtype) for a in chip_sums],
                 [pltpu.SemaphoreType.DMA((n, len(_ICI_RELS))), pltpu.SemaphoreType.DMA((n, len(_ICI_RELS))),
                  pltpu.SemaphoreType.DMA((n,))], start, finish)


def _chip_sum(slots, from_sibling, core, name):
    nq, _, r, c = slots.shape
    tr = _largest_tile(r, 1024, 16)

    def body(core_ref, a_ref, b_ref, o_ref):
        o_ref[...] = (a_ref[...].astype(F32) + b_ref[...].astype(F32)).astype(BF)

    return pl.pallas_call(
        body,
        grid_spec=pltpu.PrefetchScalarGridSpec(
            num_scalar_prefetch=1, grid=(nq, r // tr),
            in_specs=[pl.BlockSpec((None, None, tr, c), lambda q, i, core_ref: (q, core_ref[0], i, 0)),
                      pl.BlockSpec((None, tr, c), lambda q, i, core_ref: (q, i, 0))],
            out_specs=pl.BlockSpec((None, tr, c), lambda q, i, core_ref: (q, i, 0))),
        out_shape=SDS((nq, r, c), BF), compiler_params=pltpu.CompilerParams(vmem_limit_bytes=40 * MIB),
        name=name)(core, slots, from_sibling)


def _small_reduce(pack_g, meta_g, loss_scale, name, comm=None):
    w = pack_g.shape[2]

    def body(p_ref, m_ref, tot_ref, meta_ref, loss_ref):
        acc = p_ref[0]
        macc = m_ref[0]
        for k in range(1, N_DEV):
            acc = acc + p_ref[k]
            macc = macc + m_ref[k]
        tot = jnp.sum(acc, axis=0, keepdims=True)
        tot_ref[...] = tot
        meta_ref[...] = macc
        loss_ref[...] = jnp.full((1, LANES), loss_scale * jnp.sum(tot[:, w - LANES:w]), F32)

    return pl.pallas_call(
        body, out_shape=[SDS((1, w), F32), SDS(meta_g.shape[1:], F32), SDS((1, LANES), F32)],
        compiler_params=pltpu.CompilerParams(vmem_limit_bytes=32 * MIB), name=name)(pack_g, meta_g)


def _local_step(x, target, sw, plan):
    s_len, d = x.shape
    n_heads, n_meta = plan.n_heads, plan.n_meta
    l = n_meta + s_len
    lp = -(-l // LANES) * LANES
    tm = _largest_tile(lp, 544, 16)
    tq = _largest_tile(lp, 272, 16)
    te = _largest_tile(lp, 272, 16)
    tmd = _largest_tile(d, 512, LANES)

    plan.at("start")
    x, target = plan.gate((x, target))
    zmeta, zpad = jnp.zeros((n_meta, d), F32), jnp.zeros((lp - l, d), F32)
    h0 = jnp.concatenate([zmeta, x, zpad], axis=0)
    tpad = jnp.concatenate([zmeta, target, zpad], axis=0)
    plan.at("landed", (h0, tpad))
    h0 = lax.dynamic_update_slice(h0, plan.weights("meta"), (0, 0))

    split = plan.ffn1_split()
    if split is None:
        wg1, wu1, wd1 = plan.weights("ffn1")
        h1, a1, b1, u1 = _ffn_fwd(h0, sw["ffn1_norm"], wg1, wu1, wd1, tm, "ffn1_fwd", plan.comm("ffn1_fwd"))
    else:
        carry = _ffn_fwd_part(h0, sw["ffn1_norm"], *plan.weights("ffn1_landing"), split[0], None, tm, "ffn1_fwd_a",
                              plan.order_tokens())
        plan.at("ffn1_mid", (carry[0],))
        wg1, wu1, wd1 = plan.weights("ffn1")
        h1, a1, b1 = _ffn_fwd_part(h0, sw["ffn1_norm"], wg1, wu1, wd1, split[1], carry, tm, "ffn1_fwd_b",
                                   plan.order_tokens())
        u1 = carry[3]
    fs = wg1.shape[1]
    plan.at("after_ffn1_fwd", (h1,))
    win, pw, wout = plan.weights("mix")
    nz = win.shape[1]
    p_w = sw["pool_scale"].shape[1]
    npb = p_w // LANES
    fblk = nz // LANES - 1
    tnz = _largest_tile(nz, 1408, LANES)
    qw, kw, bfp, ps = sw["q_norm"], sw["k_norm"], sw["b_forget"], sw["pool_scale"]
    z, u2 = _norm_matmul(h1, sw["mix_norm"], win, tm, tnz, "mix_in", plan.comm("mix_in"))
    plan.at("after_mix_in", (u2,))
    cum = _fox_prep(z, bfp, fblk, "fox_prep")
    cumt = cum[:, :n_heads].T
    pool_o = _pool_fwd(z, pw, ps, "pool_fwd")
    att_o = _att_fwd(z, cum, cumt, qw, kw, n_heads, npb, tq, "att_fwd", plan.comm("att_fwd"))
    plan.at("after_att_fwd", (att_o,))
    h2 =_out_proj(h1, pool_o, att_o, wout, tm, "out_proj", plan.comm("out_proj"))
    wg2, wu2, wd2 = plan.weights("ffn2")
    h3, a2, b2, u3 = _ffn_fwd(h2, sw["ffn2_norm"], wg2, wu2, wd2, tm, "ffn2_fwd", plan.comm("ffn2_fwd"))
    dy, dob3, lsq = _loss_head(h3, tpad, n_meta, l, te, "loss_head")

    du3, da2, db2, hid2 = _ffn_bwd_dx(dob3, a2, b2, wg2, wu2, wd2, tm, "ffn2_bwd_dx", plan.comm("ffn2_bwd_dx"))
    dh2, dh2b, dn2 = _rms_bwd(du3, h2, sw["ffn2_norm"], dy, 1.0, te, "ffn2_rms_bwd")
    plan.grad("ffn2_w_gate", _matmul_tn(da2, u3, fs, d, "ffn2_dwg", plan.comm("ffn2_dwg")))
    plan.grad("ffn2_w_up", _matmul_tn(db2, u3, fs, d, "ffn2_dwu", plan.comm("ffn2_dwu")))
    plan.grad("ffn2_w_down", _matmul_tn(hid2, dob3, fs, d, "ffn2_dwd", plan.comm("ffn2_dwd")))
    plan.at("after_ffn2_dwd")

    dmix = _matmul_nt(dh2b, wout, tm, d, BF, "out_proj_bwd", plan.comm("out_proj_bwd"))
    plan.at("after_out_proj_bwd")
    tmp = _largest_tile(p_w, 512, LANES)
    plan.grad("w_out", jnp.concatenate([_matmul_tn(pool_o, dh2b, tmp, d, "dwout_pool"),
                                        _matmul_tn(att_o, dh2b, tmp, d, "dwout_att")], axis=0))
    dzp, dpw, dps = _pool_bwd(z, dmix, pw, ps, "pool_bwd")
    plan.grad("pool_w", dpw)
    plan.at("before_att_bwd")
    dq, dk, dv, dck, dqw, dkw = _att_bwd(z, cum, cumt, qw, kw, dmix, n_heads, npb, npb, tq, "att_bwd",
                                              plan.comm("att_bwd"))
    dcum = -dck[:, 0, :].T
    dcum = jnp.pad(dcum, ((0, 0), (0, LANES - n_heads)))
    dzf, dbf = _fox_bwd(z, bfp, dcum, fblk, "fox_bwd")
    dz = jnp.concatenate([dzp, dq, dk, dv, dzf], axis=1)
    plan.grad("w_in", _matmul_tn(u2, dz, tmd, tnz, "dwin", plan.comm("dwin")))
    du2 = _matmul_nt(dz, win, tm, tnz, F32, "mix_in_bwd", plan.comm("mix_in_bwd"))
    plan.at("before_ffn1_bwd_dx")
    dh1, dob1, dnm = _rms_bwd(du2, h1, sw["mix_norm"], dh2, 0.5, te, "mix_rms_bwd")

    du1, da1, db1, hid1 = _ffn_bwd_dx(dob1, a1, b1, wg1, wu1, wd1, tm, "ffn1_bwd_dx", plan.comm("ffn1_bwd_dx"))
    plan.grad("ffn1_w_gate", _matmul_tn(da1, u1, fs, d, "ffn1_dwg", plan.comm("ffn1_dwg")))
    plan.grad("ffn1_w_up", _matmul_tn(db1, u1, fs, d, "ffn1_dwu", plan.comm("ffn1_dwu")))
    plan.at("before_ffn1_dwd")
    plan.grad("ffn1_w_down", _matmul_tn(hid1, dob1, fs, d, "ffn1_dwd", plan.comm("ffn1_dwd")))
    plan.at("after_ffn1_dwd")
    dh0, _, dn1 = _rms_bwd(du1, h0, sw["ffn1_norm"], dh1, 1.0, te, "ffn1_rms_bwd", plan.comm("ffn1_rms_bwd"))

    small = [dn1, dnm, dn2, dps, dqw, dkw, dbf, lsq]
    return dh0[n_meta:l], dh0[:n_meta], small


_BIG = ("ffn1_w_gate", "ffn1_w_up", "ffn1_w_down", "w_in", "pool_w", "w_out", "ffn2_w_gate", "ffn2_w_up", "ffn2_w_down")
_SMALL = ("ffn1_norm", "mix_norm", "ffn2_norm", "pool_scale", "q_norm", "k_norm", "b_forget")
_ORDER = ("meta_tokens", "ffn1_norm", "ffn1_w_gate", "ffn1_w_up", "ffn1_w_down", "mix_norm", "w_in", "b_forget",
          "q_norm", "k_norm", "pool_w", "pool_scale", "w_out", "ffn2_norm", "ffn2_w_gate", "ffn2_w_up", "ffn2_w_down")


_FFN1 = ("ffn1_w_gate", "ffn1_w_up", "ffn1_w_down")
_FFN2 = ("ffn2_w_gate", "ffn2_w_up", "ffn2_w_down")
_MIX = ("w_in", "pool_w", "w_out")

_RIDES = {
    "out_proj": (("g2", _FFN2),),
    "ffn2_dwu": (("s1", ("ffn2_w_gate",)),),
    "ffn2_dwd": (("s1", ("ffn2_w_up",)),),
    "out_proj_bwd": (("s1", ("ffn2_w_down",)),),
    "mix_in_bwd": (("s1", _MIX),),
    "ffn1_dwu": (("s1", ("ffn1_w_gate",)),),
    "ffn1_dwd": (("s1", ("ffn1_w_up",)),),
    "ffn1_rms_bwd": (("s1", ("ffn1_w_down",)),),
}
_META = ("meta_tokens",)
_POINTS = {
    "start": (("start", "gm", _META), ("start", "g1a", _FFN1), ("gate", _MIX + _FFN2), ("prepare", "g1", _MIX),
              ("prepare", "g1a", _FFN2)),
    "landed": (("wait", "gm", _META), ("wait", "g1a", _FFN1), ("start", "g1b", _FFN1), ("start", "g1", _MIX),
               ("start", "g1a", _FFN2)),
    "ffn1_mid": (("wait", "g1b", _FFN1), ("alone", "g2", _FFN1)),
    "after_ffn1_fwd": (("wait", "g1", _MIX), ("alone", "g2", _MIX)),
    "after_mix_in": (("wait", "g1a", _FFN2), ("start", "g1b", _FFN2)),
    "after_att_fwd": (("wait", "g1b", _FFN2),),
    "after_ffn2_dwd": (("sum", ("ffn2_w_gate",)), ("start", "s2", ("ffn2_w_gate",))),
    "after_out_proj_bwd": (("sum", ("ffn2_w_up",)), ("start", "s2", ("ffn2_w_up",))),
    "before_att_bwd": (("sum", ("ffn2_w_down",)), ("start", "s2", ("ffn2_w_down",))),
    "before_ffn1_bwd_dx": (("sum", _MIX), ("start", "s2", _MIX)),
    "before_ffn1_dwd": (("sum", ("ffn1_w_gate",)), ("start", "s2", ("ffn1_w_gate",))),
    "after_ffn1_dwd": (("sum", ("ffn1_w_up",)), ("start", "s2", ("ffn1_w_up",))),
    "after_ffn1_rms_bwd": (("sum", ("ffn1_w_down",)), ("start", "s2", ("ffn1_w_down",))),
    "before_adamw_ffn2_w_gate": (("wait", "s2", ("ffn2_w_gate",)),),
    "before_adamw_ffn2_w_up": (("wait", "s2", ("ffn2_w_up",)),),
    "before_adamw_ffn2_w_down": (("wait", "s2", ("ffn2_w_down",)),),
    "before_adamw_w_in": (("wait", "s2", _MIX),),
    "before_adamw_ffn1_w_gate": (("wait", "s2", ("ffn1_w_gate",)),),
    "before_adamw_ffn1_w_up": (("wait", "s2", ("ffn1_w_up",)),),
    "before_adamw_ffn1_w_down": (("wait", "s2", ("ffn1_w_down",)),),
}


def _own_slot_filled(block, slot, n_slots):
    zone = lax.empty((n_slots,) + block.shape, block.dtype)
    return lax.dynamic_update_slice(zone, block[None], (slot,) + (0,) * block.ndim)


class _MeshPlan:
    def __init__(self, raw, pos, d, d_in, n_heads):
        self.raw, self.pos = dict(raw), pos
        self.core = pos[2].astype(jnp.int32).reshape(1)
        self.d, self.d_in, self.n_heads, self.n_meta = d, d_in, n_heads, raw["meta_tokens"].shape[0]
        self.partial, self.full, self.slots, self.from_sibling, self.chip_sum, self.received = {}, {}, {}, {}, {}, {}
        self.partial_a, self.pending, self.prepared, self.started, self.tokens = {}, [], {}, {}, []

    def gate(self, arrays):
        gated = lax.optimization_barrier((self.tokens[-1], tuple(arrays)))
        self.tokens[-1] = gated[0]
        return gated[1]

    def _phase(self, kind, names):
        src, dst, make = {"g2": (self.partial, self.full, _gather_fwd),
                          "s1": (self.slots, self.from_sibling, _scatter_sibling),
                          "s2": (self.chip_sum, self.received, _scatter_ici)}[kind]
        op = make([src[n] for n in names])
        self.pending.append((op, dst, names))
        return op

    def _settle(self):
        for op, dst, names in self.pending:
            dst.update(zip(names, op.results))
        self.pending = []

    def _prepare(self, kind, names):
        x, y, c = self.pos
        if kind in ("g1", "g1a", "gm"):
            blocks = [self.raw[n] if kind == "gm" else self.raw[n].astype(BF) for n in names]
            rels = {"g1": (_SIBLING,) + _ICI_RELS, "g1a": (_SIBLING,) + _ICI_RELS[:2], "gm": tuple(range(1, N_DEV))}[kind]
            op = _gather_ici(blocks, [_own_slot_filled(b, 4 * x + 2 * y + c, N_DEV) for b in blocks], rels)
        elif kind == "g1b":
            op = _gather_diagonal([self.partial_a[n] for n in names])
        else:
            sums = [self.chip_sum[n] for n in names]
            mine = [lax.dynamic_index_in_dim(s, 2 * x + y, 0, keepdims=False) for s in sums]
            op = _scatter_ici(sums, [_own_slot_filled(b, 2 * x + y, N_DEV // 2) for b in mine])
        self.prepared[(kind, names)] = op

    def _start(self, kind, names):
        if (kind, names) not in self.prepared:
            self._prepare(kind, names)
        self._launch((kind, names), self.prepared.pop((kind, names)), "_".join(("start", kind, names[0])))

    def _launch(self, key, op, name):
        if self.tokens:
            op.arrs = list(self.gate(op.arrs))
        self.started[key], token = _split_start(op, name)
        self.tokens.append(token)

    def start_small_gather(self, arrays):
        x, y, c = self.pos
        zones = [_own_slot_filled(a, 4 * x + 2 * y + c, N_DEV) for a in arrays]
        self._launch("small", _gather_ici(list(arrays), zones, rels=tuple(range(1, N_DEV))), "start_gather_small")

    def wait_small_gather(self, afters):
        return _split_wait(self.started.pop("small"), afters, "wait_gather_small")

    def _wait(self, kind, names, afters):
        afters = list(afters) + [a for op in self.prepared.values() for a in op.arrs[len(op.arrs) // 2:]]
        landed = _split_wait(self.started.pop((kind, names)), afters, "_".join(("wait", kind, names[0])))
        {"g1": self.partial, "g1a": self.partial_a, "g1b": self.partial, "gm": self.partial,
         "s2": self.received}[kind].update(zip(names, landed))

    def ffn1_split(self):
        x, y, c = self.pos
        first = [(x, y, c), _peer(x, y, c, 1), _peer(x, y, c, 4), _peer(x, y, c, 2)]
        last = [_peer(x, y, c, 6), _peer(x, y, c, 5), _peer(x, y, c, 3), _peer(x, y, c, 7)]
        return tuple(jnp.stack([_dev(p) for p in part]).astype(jnp.int32) for part in (first, last))

    def order_tokens(self):
        tokens, self.tokens = self.tokens, []
        return tokens

    def comm(self, kernel_name):
        self._settle()
        ops = [self._phase(kind, names) for kind, names in _RIDES.get(kernel_name, ())]
        if self.tokens:
            ops.append(_Comm(self.tokens, [], [], lambda *a: None, lambda *a: None))
            self.tokens = []
        return _merge_comm(ops)

    def at(self, point, after=()):
        for step in _POINTS.get(point, ()):
            self._settle()
            if step[0] == "alone":
                _comm_alone(self._phase(step[1], step[2]), "_".join((step[1], point)))
            elif step[0] == "start":
                self._start(step[1], step[2])
            elif step[0] == "prepare":
                self._prepare(step[1], step[2])
            elif step[0] == "gate":
                self.raw.update(zip(step[1], self.gate([self.raw[n] for n in step[1]])))
            elif step[0] == "wait":
                self._wait(step[1], step[2], tuple(after) + tuple(self.tokens[-1:]))
            else:
                for n in step[1]:
                    self.chip_sum[n] = _chip_sum(self.slots[n], self.from_sibling[n], self.core, "chip_sum_" + n)

    def weights(self, group):
        self._settle()
        f, d = self.full, self.d
        if group == "meta":
            g = self.partial["meta_tokens"]
            return g.transpose(1, 0, 2).reshape(g.shape[1], d)
        if group == "ffn1_landing":
            return tuple(self.started[("g1b", _FFN1)][2])
        if group == "ffn1":
            return tuple(f[n] for n in _FFN1)
        if group == "ffn2":
            return tuple(f[n] for n in _FFN2)
        n_main = self.d_in - self.n_heads
        win = f["w_in"].transpose(1, 0, 2).reshape(d, self.d_in)
        win = jnp.concatenate([win[:, :n_main], jnp.pad(win[:, n_main:], ((0, 0), (0, LANES - self.n_heads)))], axis=1)
        pw = f["pool_w"]
        gw = pw.shape[2]
        pw = pw.reshape(N_DEV, -1, gw // N_DEV, gw).transpose(1, 0, 2, 3).reshape(-1, gw, gw)
        return win, pw, f["w_out"].reshape(-1, d)

    def grad(self, name, g):
        d = self.d
        if name == "w_in":
            g = g[:, :self.d_in].reshape(d, N_DEV, -1).transpose(1, 0, 2)
        elif name == "pool_w":
            ng, gw = g.shape[0], g.shape[2]
            g = g.astype(BF).reshape(ng, N_DEV, -1, gw).transpose(1, 0, 2, 3).reshape(N_DEV, -1, gw)
        elif name == "w_out":
            g = g.reshape(N_DEV, -1, d)
        self.slots[name] = g.reshape((N_DEV // 2, 2) + g.shape[1:])

    def gradient_parts(self, name):
        self._settle()
        return self.received[name]


_TRANSPOSED = ("ffn1_w_gate", "ffn1_w_up", "ffn2_w_gate", "ffn2_w_up")


def _as2d(name, a):
    return a[0].T if name in _TRANSPOSED else a.reshape(-1, a.shape[-1])


def _from2d(name, a2d, shape):
    return a2d.T.reshape(shape) if name in _TRANSPOSED else a2d.reshape(shape)


def kernel(x, meta_tokens, ffn1_norm, ffn1_w_gate, ffn1_w_up, ffn1_w_down, mix_norm, w_in, b_forget, q_norm, k_norm, pool_w, pool_scale, w_out, ffn2_norm, ffn2_w_gate, ffn2_w_up, ffn2_w_down, loss_target, m_meta_tokens, m_ffn1_norm, m_ffn1_w_gate, m_ffn1_w_up, m_ffn1_w_down, m_mix_norm, m_w_in, m_b_forget, m_q_norm, m_k_norm, m_pool_w, m_pool_scale, m_w_out, m_ffn2_norm, m_ffn2_w_gate, m_ffn2_w_up, m_ffn2_w_down, v_meta_tokens, v_ffn1_norm, v_ffn1_w_gate, v_ffn1_w_up, v_ffn1_w_down, v_mix_norm, v_w_in, v_b_forget, v_q_norm, v_k_norm, v_pool_w, v_pool_scale, v_w_out, v_ffn2_norm, v_ffn2_w_gate, v_ffn2_w_up, v_ffn2_w_down):
    w = dict(meta_tokens=meta_tokens, ffn1_norm=ffn1_norm, ffn1_w_gate=ffn1_w_gate, ffn1_w_up=ffn1_w_up,
             ffn1_w_down=ffn1_w_down, mix_norm=mix_norm, w_in=w_in, b_forget=b_forget, q_norm=q_norm, k_norm=k_norm,
             pool_w=pool_w, pool_scale=pool_scale, w_out=w_out, ffn2_norm=ffn2_norm, ffn2_w_gate=ffn2_w_gate,
             ffn2_w_up=ffn2_w_up, ffn2_w_down=ffn2_w_down)
    m = dict(meta_tokens=m_meta_tokens, ffn1_norm=m_ffn1_norm, ffn1_w_gate=m_ffn1_w_gate, ffn1_w_up=m_ffn1_w_up,
             ffn1_w_down=m_ffn1_w_down, mix_norm=m_mix_norm, w_in=m_w_in, b_forget=m_b_forget, q_norm=m_q_norm,
             k_norm=m_k_norm, pool_w=m_pool_w, pool_scale=m_pool_scale, w_out=m_w_out, ffn2_norm=m_ffn2_norm,
             ffn2_w_gate=m_ffn2_w_gate, ffn2_w_up=m_ffn2_w_up, ffn2_w_down=m_ffn2_w_down)
    v = dict(meta_tokens=v_meta_tokens, ffn1_norm=v_ffn1_norm, ffn1_w_gate=v_ffn1_w_gate, ffn1_w_up=v_ffn1_w_up,
             ffn1_w_down=v_ffn1_w_down, mix_norm=v_mix_norm, w_in=v_w_in, b_forget=v_b_forget, q_norm=v_q_norm,
             k_norm=v_k_norm, pool_w=v_pool_w, pool_scale=v_pool_scale, w_out=v_w_out, ffn2_norm=v_ffn2_norm,
             ffn2_w_gate=v_ffn2_w_gate, ffn2_w_up=v_ffn2_w_up, ffn2_w_down=v_ffn2_w_down)

    d = x.shape[-1]
    n_heads = b_forget.shape[-1]
    pos = (lax.axis_index("x"), lax.axis_index("y"), lax.axis_index("c"))
    me = 4 * pos[0] + 2 * pos[1] + pos[2]

    raw = {k: _as2d(k, w[k]) for k in _BIG}
    raw["meta_tokens"] = meta_tokens
    plan = _MeshPlan(raw, pos, d, N_DEV * w_in.shape[-1], n_heads)
    sw = {k: w[k] for k in _SMALL}
    sw["b_forget"] = jnp.pad(b_forget, ((0, 0), (0, LANES - n_heads)))
    dx, dmeta, small = _local_step(x[0], loss_target[0], sw, plan)

    res = {}
    last = dx

    plan.start_small_gather([jnp.concatenate(small, axis=1), dmeta])
    plan.at("after_ffn1_rms_bwd")

    def update_shards(names):
        nonlocal last
        for k in names:
            plan.at("before_adamw_" + k, (last,))
            res[k] = _adamw(plan.gradient_parts(k), _as2d(k, w[k]), _as2d(k, m[k]), _as2d(k, v[k]), "adamw_" + k,
                            plan.comm("adamw_" + k))
            last = res[k][0]

    update_shards(_FFN2 + _MIX + ("ffn1_w_gate", "ffn1_w_up"))

    pack_g, meta_g = plan.wait_small_gather((last,))
    tot, dmeta_tot, loss_row = _small_reduce(pack_g, meta_g, 0.5 / d, "small_reduce")

    mcols = meta_tokens.shape[1]
    g_meta = lax.dynamic_slice_in_dim(dmeta_tot, me * mcols, mcols, axis=1)
    res["meta_tokens"] = _adamw(g_meta, meta_tokens, m_meta_tokens, v_meta_tokens, "adamw_meta_tokens")

    def packed(src):
        return jnp.concatenate([src[k] for k in _SMALL[:-1]] + [jnp.pad(src["b_forget"], ((0, 0), (0, LANES - n_heads)))],
                               axis=1)

    wp = packed(w)
    sm = _adamw(tot[:, :wp.shape[1]], wp, packed(m), packed(v), "adamw_small")
    off = 0
    for k in _SMALL:
        width = w[k].shape[1]
        res[k] = tuple(o[:, off:off + width] for o in sm)
        off += width if k != "b_forget" else LANES

    last = sm[0]
    update_shards(("ffn1_w_down",))

    outs =[loss_row[0, 0], dx[None]]
    for idx in range(4):
        outs += [_from2d(k, res[k][idx], w[k].shape) for k in _ORDER]
    return tuple(outs)
```

```python
import functools

import jax
import jax.numpy as jnp
from jax import lax
from jax.experimental import pallas as pl
from jax.experimental.pallas import tpu as pltpu

F32 = jnp.float32
BF = jnp.bfloat16
SDS = jax.ShapeDtypeStruct

N_DEV = 8
LANES = 128
SUBLANES = 8
HEAD_DIM = 128
POOL_WINDOWS = (2, 4, 8, 16)
RMS_EPS = 1e-6
NEG_BIG = -1e30
MIB = 1024 * 1024

ADAM_LR = 0.001
ADAM_B1 = 0.9
ADAM_B2 = 0.999
ADAM_EPS = 1e-08
ADAM_WD = 0.01
ADAM_STEP = 10


class _Comm:
    def __init__(self, arrs, out_shape, sems, start, finish, aliases=None):
        self.arrs, self.out_shape, self.sems = list(arrs), list(out_shape), list(sems)
        self.start, self.finish, self.aliases = start, finish, dict(aliases or {})
        self.results = None


def _merge_comm(ops):
    ops = [op for op in ops if op is not None]
    if not ops:
        return None
    na, no, ns = [0], [0], [0]
    for op in ops:
        na.append(na[-1] + len(op.arrs))
        no.append(no[-1] + len(op.out_shape))
        ns.append(ns[-1] + len(op.sems))

    def parts(i, ins, outs, sems):
        return ins[na[i]:na[i + 1]], outs[no[i]:no[i + 1]], sems[ns[i]:ns[i + 1]]

    def start(ins, outs, sems):
        for i, op in enumerate(ops):
            op.start(*parts(i, ins, outs, sems))

    def finish(ins, outs, sems):
        for i, op in enumerate(ops):
            op.finish(*parts(i, ins, outs, sems))

    aliases = {}
    for i, op in enumerate(ops):
        for a, o in op.aliases.items():
            aliases[na[i] + a] = no[i] + o
    merged = _Comm([a for op in ops for a in op.arrs], [s for op in ops for s in op.out_shape],
                   [s for op in ops for s in op.sems], start, finish, aliases)
    merged.children = (ops, no)
    return merged


def _deliver(comm, results):
    comm.results = list(results)
    if hasattr(comm, "children"):
        ops, no = comm.children
        for i, op in enumerate(ops):
            _deliver(op, results[no[i]:no[i + 1]])


def _call(body, *, grid, in_specs, out_specs, out_shape, scratch_shapes=(), vmem_mib, name, comm=None):
    single = not isinstance(out_shape, (list, tuple))
    out_specs = [out_specs] if single else list(out_specs)
    out_shape = [out_shape] if single else list(out_shape)
    in_specs, scratch_shapes = list(in_specs), list(scratch_shapes)
    params = pltpu.CompilerParams(dimension_semantics=("arbitrary",) * len(grid), vmem_limit_bytes=vmem_mib * MIB)
    n_in, n_out, n_scr = len(in_specs), len(out_specs), len(scratch_shapes)

    def run(*args):
        if comm is None:
            res = pl.pallas_call(body, grid=grid, in_specs=in_specs, out_specs=out_specs, out_shape=out_shape,
                                 scratch_shapes=scratch_shapes, compiler_params=params, name=name)(*args)
            return res[0] if single else res
        ci, co = len(comm.arrs), len(comm.out_shape)

        def with_comm(*refs):
            ins, cins = refs[:n_in], refs[n_in:n_in + ci]
            o0 = n_in + ci
            outs, couts = refs[o0:o0 + n_out], refs[o0 + n_out:o0 + n_out + co]
            s0 = o0 + n_out + co
            scr, csems = refs[s0:s0 + n_scr], refs[s0 + n_scr:]
            ids = [pl.program_id(a) for a in range(len(grid))]
            first = functools.reduce(jnp.logical_and, [i == 0 for i in ids])
            last = functools.reduce(jnp.logical_and, [i == g - 1 for i, g in zip(ids, grid)])

            @pl.when(first)
            def _():
                comm.start(cins, couts, csems)

            body(*ins, *outs, *scr)

            @pl.when(last)
            def _():
                comm.finish(cins, couts, csems)

        anyspec = pl.BlockSpec(memory_space=pl.ANY)
        res = pl.pallas_call(
            with_comm, grid=grid, in_specs=in_specs + [anyspec] * ci, out_specs=out_specs + [anyspec] * co,
            out_shape=out_shape + comm.out_shape, scratch_shapes=scratch_shapes + comm.sems,
            input_output_aliases={n_in + a: n_out + o for a, o in comm.aliases.items()},
            compiler_params=params, name=name)(*args, *comm.arrs)
        _deliver(comm, res[n_out:])
        return res[0] if single else res[:n_out]

    return run


def _comm_alone(comm, name):
    def body(*refs):
        ci, co = len(comm.arrs), len(comm.out_shape)
        ins, outs, sems = refs[:ci], refs[ci:ci + co], refs[ci + co:]
        comm.start(ins, outs, sems)
        comm.finish(ins, outs, sems)

    anyspec = pl.BlockSpec(memory_space=pl.ANY)
    res = pl.pallas_call(
        body, in_specs=[anyspec] * len(comm.arrs), out_specs=[anyspec] * len(comm.out_shape),
        out_shape=comm.out_shape, scratch_shapes=comm.sems, input_output_aliases=comm.aliases, name=name)(*comm.arrs)
    _deliver(comm, res)


def _split_start(comm, name):
    na, ns = len(comm.arrs), len(comm.sems)

    def body(*refs):
        comm.start(refs[:na], None, refs[na:na + ns])
        token = refs[-1]
        token[...] = jnp.zeros_like(token)

    hbm = pl.BlockSpec(memory_space=pltpu.HBM)
    res = pl.pallas_call(
        body, name=name,
        out_shape=tuple(comm.sems) + tuple(pltpu.HBM(a.shape, a.dtype) for a in comm.arrs)
        + (SDS((SUBLANES, LANES), F32),),
        in_specs=[hbm] * na,
        out_specs=[pl.BlockSpec(memory_space=pltpu.SEMAPHORE)] * ns + [hbm] * na + [pl.BlockSpec(memory_space=pltpu.VMEM)],
        input_output_aliases={i: ns + i for i in range(na)},
        compiler_params=pltpu.CompilerParams(has_side_effects=pltpu.SideEffectType.DATAFLOW_SIDE_EFFECTING),
    )(*[pltpu.with_memory_space_constraint(a, pltpu.HBM) for a in comm.arrs])
    return (comm, res[:ns], res[ns:ns + na]), res[-1]


def _split_wait(started, afters, name):
    comm, sems, thru = started
    na, ns = len(thru), len(sems)
    afters = list(afters)

    def body(*refs):
        comm.finish(refs[:na], None, refs[na:na + ns])

    hbm = pl.BlockSpec(memory_space=pltpu.HBM)
    res = pl.pallas_call(
        body, name=name, out_shape=tuple(pltpu.HBM(a.shape, a.dtype) for a in thru),
        in_specs=[hbm] * na + [pl.BlockSpec(memory_space=pltpu.SEMAPHORE)] * ns
        + [pl.BlockSpec(memory_space=pl.ANY)] * len(afters),
        out_specs=[hbm] * na, input_output_aliases={i: i for i in range(na)},
        compiler_params=pltpu.CompilerParams(has_side_effects=pltpu.SideEffectType.DATAFLOW_SIDE_EFFECTING),
    )(*thru, *sems, *afters)
    return res[na - len(comm.out_shape):]


def _largest_tile(n, cap, mult):
    if n <= cap:
        return n
    best = None
    for t in range(mult, cap + 1, mult):
        if n % t == 0:
            best = t
    assert best is not None, (n, cap, mult)
    return best


def _dot(a, b):
    return jnp.dot(a, b, preferred_element_type=F32)


def _dot_nt(a, b):
    return lax.dot_general(a, b, (((1,), (1,)), ((), ())), preferred_element_type=F32)


def _dot_tn(a, b):
    return lax.dot_general(a, b, (((0,), (0,)), ((), ())), preferred_element_type=F32)


def _rows8(x):
    t, c = x.shape
    return jnp.sum(x.reshape(t // SUBLANES, SUBLANES, c), axis=0)


def _rstd(x):
    return lax.rsqrt(jnp.mean(x * x, axis=-1, keepdims=True) + RMS_EPS)


def _ffn_fwd(h, g, wg, wu, wd, tm, name, comm=None):
    lp, d = h.shape
    ns, fs, _ = wg.shape

    def body(h_ref, g_ref, wg_ref, wu_ref, wd_ref, out_ref, a_ref, b_ref, u_ref, acc_ref):
        j = pl.program_id(1)

        @pl.when(j == 0)
        def _():
            hh = h_ref[...]
            u_ref[...] = (hh * _rstd(hh) * g_ref[...]).astype(BF)
            acc_ref[...] = jnp.zeros_like(acc_ref)

        u = u_ref[...]
        a = _dot_nt(u, wg_ref[...])
        b = _dot_nt(u, wu_ref[...])
        a_ref[...] = a.astype(BF)
        b_ref[...] = b.astype(BF)
        hid = (a * jax.nn.sigmoid(a) * b).astype(BF)
        acc_ref[...] += _dot(hid, wd_ref[...])

        @pl.when(j == ns - 1)
        def _():
            out_ref[...] = h_ref[...] + 0.5 * acc_ref[...]

    row = pl.BlockSpec((tm, d), lambda i, j: (i, 0))
    act = pl.BlockSpec((None, tm, fs), lambda i, j: (j, i, 0))
    return _call(
        body, grid=(lp // tm, ns),
        in_specs=[row, pl.BlockSpec((1, d), lambda i, j: (0, 0)),
                  pl.BlockSpec((None, fs, d), lambda i, j: (j, 0, 0)),
                  pl.BlockSpec((None, fs, d), lambda i, j: (j, 0, 0)),
                  pl.BlockSpec((None, fs, d), lambda i, j: (j, 0, 0))],
        out_specs=[row, act, act, row],
        out_shape=[SDS((lp, d), F32), SDS((ns, lp, fs), BF), SDS((ns, lp, fs), BF), SDS((lp, d), BF)],
        scratch_shapes=[pltpu.VMEM((tm, d), F32)],
        vmem_mib=56, name=name, comm=comm)(h, g, wg, wu, wd)


def _ffn_fwd_part(h, g, wg, wu, wd, order, carry, tm, name, deps=()):
    lp, d = h.shape
    fs = wg.shape[1]
    k = order.shape[0]
    first = carry is None
    n_in = 5 if first else 8

    def body(order_ref, *refs):
        outs = refs[n_in + len(deps):]
        if first:
            h_ref, g_ref, wg_ref, wu_ref, wd_ref = refs[:n_in]
            out_ref, a_ref, b_ref, u_ref, acc_ref = outs
        else:
            h_ref, acc_in_ref, u_ref, _, _, wg_ref, wu_ref, wd_ref = refs[:n_in]
            out_ref, a_ref, b_ref, acc_ref = outs
        j = pl.program_id(1)

        @pl.when(j == 0)
        def _():
            if first:
                hh = h_ref[...]
                u_ref[...] = (hh * _rstd(hh) * g_ref[...]).astype(BF)
                acc_ref[...] = jnp.zeros_like(acc_ref)
            else:
                acc_ref[...] = acc_in_ref[...]

        u = u_ref[...]
        a = _dot_nt(u, wg_ref[...])
        b = _dot_nt(u, wu_ref[...])
        a_ref[...] = a.astype(BF)
        b_ref[...] = b.astype(BF)
        hid = (a * jax.nn.sigmoid(a) * b).astype(BF)
        acc_ref[...] += _dot(hid, wd_ref[...])

        @pl.when(j == k - 1)
        def _():
            out_ref[...] = acc_ref[...] if first else h_ref[...] + 0.5 * acc_ref[...]

    row = pl.BlockSpec((tm, d), lambda i, j, o: (i, 0))
    act = pl.BlockSpec((None, tm, fs), lambda i, j, o: (o[j], i, 0))
    wsp = pl.BlockSpec((None, fs, d), lambda i, j, o: (o[j], 0, 0))
    anyspec = pl.BlockSpec(memory_space=pl.ANY)
    acts = [SDS((wg.shape[0], lp, fs), BF)] * 2
    if first:
        in_specs = [row, pl.BlockSpec((1, d), lambda i, j, o: (0, 0)), wsp, wsp, wsp]
        out_specs, out_shape = [row, act, act, row], [SDS((lp, d), F32)] + acts + [SDS((lp, d), BF)]
        args, aliases = (h, g, wg, wu, wd), {}
    else:
        acc, a_prev, b_prev, u_prev = carry
        in_specs = [row, row, row, anyspec, anyspec, wsp, wsp, wsp]
        out_specs, out_shape = [row, act, act], [SDS((lp, d), F32)] + acts
        args, aliases = (h, acc, u_prev, a_prev, b_prev, wg, wu, wd), {4: 1, 5: 2}
    return pl.pallas_call(
        body,
        grid_spec=pltpu.PrefetchScalarGridSpec(
            num_scalar_prefetch=1, grid=(lp // tm, k), in_specs=in_specs + [anyspec] * len(deps),
            out_specs=out_specs, scratch_shapes=[pltpu.VMEM((tm, d), F32)]),
        out_shape=out_shape, input_output_aliases=aliases,
        compiler_params=pltpu.CompilerParams(dimension_semantics=("arbitrary",) * 2, vmem_limit_bytes=60 * MIB),
        name=name)(order, *args, *deps)


def _ffn_bwd_dx(dob, a, b, wg, wu, wd, tm, name, comm=None):
    lp, d = dob.shape
    ns, fs, _ = wg.shape

    def body(do_ref, a_ref, b_ref, wg_ref, wu_ref, wd_ref, du_ref, da_ref, db_ref, hid_ref):
        j = pl.program_id(1)

        @pl.when(j == 0)
        def _():
            du_ref[...] = jnp.zeros_like(du_ref)

        dhid = _dot_nt(do_ref[...], wd_ref[...])
        av = a_ref[...].astype(F32)
        bv = b_ref[...].astype(F32)
        sig = jax.nn.sigmoid(av)
        sil = av * sig
        dbv = (dhid * sil).astype(BF)
        dav = (dhid * bv * (sig * (1.0 + av * (1.0 - sig)))).astype(BF)
        hid_ref[...] = (sil * bv).astype(BF)
        da_ref[...] = dav
        db_ref[...] = dbv
        du_ref[...] += _dot(dav, wg_ref[...]) + _dot(dbv, wu_ref[...])

    row = pl.BlockSpec((tm, d), lambda i, j: (i, 0))
    act = pl.BlockSpec((None, tm, fs), lambda i, j: (j, i, 0))
    return _call(
        body, grid=(lp // tm, ns),
        in_specs=[row, act, act,
                  pl.BlockSpec((None, fs, d), lambda i, j: (j, 0, 0)),
                  pl.BlockSpec((None, fs, d), lambda i, j: (j, 0, 0)),
                  pl.BlockSpec((None, fs, d), lambda i, j: (j, 0, 0))],
        out_specs=[row, act, act, act],
        out_shape=[SDS((lp, d), F32)] + [SDS((ns, lp, fs), BF)] * 3,
        vmem_mib=56, name=name, comm=comm)(dob, a, b, wg, wu, wd)


def _rms_bwd(du, h, g, dres, bscale, tm, name, comm=None):
    lp, d = h.shape

    def body(du_ref, h_ref, g_ref, dres_ref, dh_ref, dhb_ref, dg_ref):
        @pl.when(pl.program_id(0) == 0)
        def _():
            dg_ref[...] = jnp.zeros_like(dg_ref)

        hh = h_ref[...]
        r = _rstd(hh)
        xhat = hh * r
        duv = du_ref[...]
        dg_ref[...] += _rows8(duv * xhat)
        dxh = duv * g_ref[...]
        dh = dres_ref[...] + r * (dxh - xhat * jnp.mean(dxh * xhat, axis=-1, keepdims=True))
        dh_ref[...] = dh
        dhb_ref[...] = (bscale * dh).astype(BF)

    row = pl.BlockSpec((tm, d), lambda i: (i, 0))
    return _call(
        body, grid=(lp // tm,),
        in_specs=[row, row, pl.BlockSpec((1, d), lambda i: (0, 0)), row],
        out_specs=[row, row, pl.BlockSpec((SUBLANES, d), lambda i: (0, 0))],
        out_shape=[SDS((lp, d), F32), SDS((lp, d), BF), SDS((SUBLANES, d), F32)],
        vmem_mib=48, name=name, comm=comm)(du, h, g, dres)


def _matmul_tn(a, b, tm, tn, name, comm=None):
    a_b, b_b = a.ndim == 3, b.ndim == 3
    ns = a.shape[0] if a_b else (b.shape[0] if b_b else 1)
    l, m = a.shape[-2:]
    n = b.shape[-1]

    def body(a_ref, b_ref, o_ref):
        o_ref[...] = _dot_tn(a_ref[...], b_ref[...]).astype(o_ref.dtype)

    a_spec = (pl.BlockSpec((None, l, tm), lambda s, i, j: (s, 0, i)) if a_b
              else pl.BlockSpec((l, tm), lambda s, i, j: (0, i)))
    b_spec = (pl.BlockSpec((None, l, tn), lambda s, i, j: (s, 0, j)) if b_b
              else pl.BlockSpec((l, tn), lambda s, i, j: (0, j)))
    batched = a_b or b_b
    o_spec = (pl.BlockSpec((None, tm, tn), lambda s, i, j: (s, i, j)) if batched
              else pl.BlockSpec((tm, tn), lambda s, i, j: (i, j)))
    o_shape = SDS((ns, m, n), BF) if batched else SDS((m, n), BF)
    return _call(
        body, grid=(ns, m // tm, n // tn), in_specs=[a_spec, b_spec], out_specs=o_spec, out_shape=o_shape,
        vmem_mib=48, name=name, comm=comm)(a, b)


def _matmul_nt(x, w, tm, tk, out_dtype, name, comm=None):
    l, k = x.shape
    n = w.shape[0]
    nk = k // tk

    def body(x_ref, w_ref, o_ref, acc_ref):
        kk = pl.program_id(1)

        @pl.when(kk == 0)
        def _():
            acc_ref[...] = jnp.zeros_like(acc_ref)

        acc_ref[...] += _dot_nt(x_ref[...], w_ref[...])

        @pl.when(kk == nk - 1)
        def _():
            o_ref[...] = acc_ref[...].astype(o_ref.dtype)

    return _call(
        body, grid=(l // tm, nk),
        in_specs=[pl.BlockSpec((tm, tk), lambda i, kk: (i, kk)), pl.BlockSpec((n, tk), lambda i, kk: (0, kk))],
        out_specs=pl.BlockSpec((tm, n), lambda i, kk: (i, 0)),
        out_shape=SDS((l, n), out_dtype),
        scratch_shapes=[pltpu.VMEM((tm, n), F32)],
        vmem_mib=48, name=name, comm=comm)(x, w)


def _norm_matmul(h, g, w, tm, tn, name, comm=None):
    lp, d = h.shape
    n = w.shape[1]

    def body(h_ref, g_ref, w_ref, z_ref, u_ref):
        @pl.when(pl.program_id(1) == 0)
        def _():
            hh = h_ref[...]
            u_ref[...] = (hh * _rstd(hh) * g_ref[...]).astype(BF)

        z_ref[...] = _dot(u_ref[...], w_ref[...])

    row = pl.BlockSpec((tm, d), lambda i, j: (i, 0))
    return _call(
        body, grid=(lp // tm, n // tn),
        in_specs=[row, pl.BlockSpec((1, d), lambda i, j: (0, 0)), pl.BlockSpec((d, tn), lambda i, j: (0, j))],
        out_specs=[pl.BlockSpec((tm, tn), lambda i, j: (i, j)), row],
        out_shape=[SDS((lp, n), F32), SDS((lp, d), BF)],
        vmem_mib=48, name=name, comm=comm)(h, g, w)


def _out_proj(h, pool_o, att_o, w_out, tm, name, comm=None):
    lp, d = h.shape
    p = pool_o.shape[1]
    dm = w_out.shape[0]

    def body(h_ref, p_ref, a_ref, w_ref, o_ref):
        o_ref[...] = h_ref[...] + _dot(p_ref[...], w_ref[0:p, :]) + _dot(a_ref[...], w_ref[p:dm, :])

    row = pl.BlockSpec((tm, d), lambda i: (i, 0))
    return _call(
        body, grid=(lp // tm,),
        in_specs=[row, pl.BlockSpec((tm, p), lambda i: (i, 0)), pl.BlockSpec((tm, dm - p), lambda i: (i, 0)),
                  pl.BlockSpec((dm, d), lambda i: (0, 0))],
        out_specs=row, out_shape=SDS((lp, d), F32),
        vmem_mib=48, name=name, comm=comm)(h, pool_o, att_o, w_out)


def _loss_head(y, tpad, row0, row1, tm, name, comm=None):
    lp, d = y.shape

    def body(y_ref, t_ref, dy_ref, dob_ref, ls_ref):
        i = pl.program_id(0)

        @pl.when(i == 0)
        def _():
            ls_ref[...] = jnp.zeros_like(ls_ref)

        rows = i * tm + lax.broadcasted_iota(jnp.int32, (tm, d), 0)
        err = jnp.where((rows >= row0) & (rows < row1), y_ref[...] - t_ref[...], 0.0)
        dy = err * (1.0 / d)
        dy_ref[...] = dy
        dob_ref[...] = (0.5 * dy).astype(BF)
        sq = _rows8(err * err)
        acc = sq[:, 0:LANES]
        for c in range(1, d // LANES):
            acc = acc + sq[:, c * LANES:(c + 1) * LANES]
        ls_ref[...] += acc

    row = pl.BlockSpec((tm, d), lambda i: (i, 0))
    return _call(
        body, grid=(lp // tm,), in_specs=[row, row],
        out_specs=[row, row, pl.BlockSpec((SUBLANES, LANES), lambda i: (0, 0))],
        out_shape=[SDS((lp, d), F32), SDS((lp, d), BF), SDS((SUBLANES, LANES), F32)],
        vmem_mib=48, name=name, comm=comm)(y, tpad)


def _window_select(levels, gidx):
    out = levels[-1]
    for k in range(len(levels) - 2, -1, -1):
        out = jnp.where(gidx == k, levels[k], out)
    return out


def _pool_window_mean_minus_id(x, gidx):
    rows = lax.broadcasted_iota(jnp.int32, x.shape, 0)
    levels = []
    s = x
    shift = 1
    while shift < POOL_WINDOWS[-1]:
        s = s + jnp.where(rows >= shift, pltpu.roll(s, shift, 0), 0.0)
        shift *= 2
        if shift in POOL_WINDOWS:
            levels.append(s)
    win = _window_select(levels, gidx)
    cnt = jnp.minimum(rows + 1, _window_select(list(POOL_WINDOWS), gidx)).astype(F32)
    return win / cnt - x, cnt


def _pool_window_transpose(dy, cnt, gidx):
    lp = dy.shape[0]
    rows = lax.broadcasted_iota(jnp.int32, dy.shape, 0)
    levels = []
    s = dy / cnt
    shift = 1
    while shift < POOL_WINDOWS[-1]:
        s = s + jnp.where(rows < lp - shift, pltpu.roll(s, lp - shift, 0), 0.0)
        shift *= 2
        if shift in POOL_WINDOWS:
            levels.append(s)
    return _window_select(levels, gidx) - dy


def _pool_fwd(z, pool_w, pool_scale, name, comm=None):
    lp = z.shape[0]
    ng, gw, _ = pool_w.shape

    def body(p_ref, w_ref, s_ref, o_ref):
        pooled, _ = _pool_window_mean_minus_id(p_ref[...], pl.program_id(0))
        o_ref[...] = (_dot(pooled.astype(BF), w_ref[...]) * s_ref[...]).astype(BF)

    return _call(
        body, grid=(ng,),
        in_specs=[pl.BlockSpec((lp, gw), lambda g: (0, g)), pl.BlockSpec((None, gw, gw), lambda g: (g, 0, 0)),
                  pl.BlockSpec((1, gw), lambda g: (0, g))],
        out_specs=pl.BlockSpec((lp, gw), lambda g: (0, g)), out_shape=SDS((lp, ng * gw), BF),
        vmem_mib=48, name=name, comm=comm)(z, pool_w, pool_scale)


def _pool_bwd(z, dmix, pool_w, pool_scale, name, comm=None):
    lp = z.shape[0]
    ng, gw, _ = pool_w.shape

    def body(p_ref, d_ref, w_ref, s_ref, dz_ref, dw_ref, ds_ref):
        g = pl.program_id(0)
        pooled, cnt = _pool_window_mean_minus_id(p_ref[...], g)
        pooled_b = pooled.astype(BF)
        w = w_ref[...]
        mixed = _dot(pooled_b, w)
        dpo = d_ref[...].astype(F32)
        ds_ref[...] = _rows8(dpo * mixed)
        dmixed = (dpo * s_ref[...]).astype(BF)
        dw_ref[...] = _dot_tn(pooled_b, dmixed)
        dpooled = _dot_nt(dmixed, w)
        dz_ref[...] = _pool_window_transpose(dpooled, cnt, g).astype(BF)

    return _call(
        body, grid=(ng,),
        in_specs=[pl.BlockSpec((lp, gw), lambda g: (0, g)), pl.BlockSpec((lp, gw), lambda g: (0, g)),
                  pl.BlockSpec((None, gw, gw), lambda g: (g, 0, 0)), pl.BlockSpec((1, gw), lambda g: (0, g))],
        out_specs=[pl.BlockSpec((lp, gw), lambda g: (0, g)), pl.BlockSpec((None, gw, gw), lambda g: (g, 0, 0)),
                   pl.BlockSpec((SUBLANES, gw), lambda g: (0, g))],
        out_shape=[SDS((lp, ng * gw), BF), SDS((ng, gw, gw), F32), SDS((SUBLANES, ng * gw), F32)],
        vmem_mib=48, name=name, comm=comm)(z, dmix, pool_w, pool_scale)


def _log_sigmoid(x):
    return jnp.minimum(x, 0.0) - jnp.log(1.0 + jnp.exp(-jnp.abs(x)))


def _fox_prep(z, bfp, fblk, name, comm=None):
    lp = z.shape[0]
    nb = lp // LANES

    def body(f_ref, b_ref, cum_ref):
        r = lax.broadcasted_iota(jnp.int32, (LANES, LANES), 0)
        c = lax.broadcasted_iota(jnp.int32, (LANES, LANES), 1)
        tri = (r >= c).astype(F32)
        carry = jnp.zeros((1, LANES), F32)
        for blk in range(nb):
            sl = slice(blk * LANES, (blk + 1) * LANES)
            lf = _log_sigmoid(f_ref[sl, :] + b_ref[...])
            cb = jnp.dot(tri, lf, preferred_element_type=F32, precision=lax.Precision.HIGHEST) + carry
            cum_ref[sl, :] = cb
            carry = cb[LANES - 1:LANES, :]

    return _call(
        body, grid=(1,),
        in_specs=[pl.BlockSpec((lp, LANES), lambda i: (0, fblk)), pl.BlockSpec((1, LANES), lambda i: (0, 0))],
        out_specs=pl.BlockSpec((lp, LANES), lambda i: (0, 0)), out_shape=SDS((lp, LANES), F32),
        vmem_mib=32, name=name, comm=comm)(z, bfp)


def _fox_bwd(z, bfp, dcum, fblk, name, comm=None):
    lp = z.shape[0]
    nb = lp // LANES

    def body(f_ref, b_ref, dc_ref, dz_ref, db_ref):
        r = lax.broadcasted_iota(jnp.int32, (LANES, LANES), 0)
        c = lax.broadcasted_iota(jnp.int32, (LANES, LANES), 1)
        tri = (r <= c).astype(F32)
        carry = jnp.zeros((1, LANES), F32)
        acc = jnp.zeros((SUBLANES, LANES), F32)
        for blk in range(nb - 1, -1, -1):
            sl = slice(blk * LANES, (blk + 1) * LANES)
            dlf = jnp.dot(tri, dc_ref[sl, :], preferred_element_type=F32, precision=lax.Precision.HIGHEST) + carry
            carry = dlf[0:1, :]
            df = dlf * jax.nn.sigmoid(-(f_ref[sl, :] + b_ref[...]))
            dz_ref[sl, :] = df.astype(BF)
            acc = acc + _rows8(df)
        db_ref[...] = acc

    return _call(
        body, grid=(1,),
        in_specs=[pl.BlockSpec((lp, LANES), lambda i: (0, fblk)), pl.BlockSpec((1, LANES), lambda i: (0, 0)),
                  pl.BlockSpec((lp, LANES), lambda i: (0, 0))],
        out_specs=[pl.BlockSpec((lp, LANES), lambda i: (0, 0)), pl.BlockSpec((SUBLANES, LANES), lambda i: (0, 0))],
        out_shape=[SDS((lp, LANES), BF), SDS((SUBLANES, LANES), F32)],
        vmem_mib=32, name=name, comm=comm)(z, bfp, dcum)


def _att_scores(q_ref, cum_ref, cumt_ref, qw_ref, kn_s, h, i, tq, lk):
    scale = 1.0 / (HEAD_DIM ** 0.5)
    q = q_ref[...]
    rq = _rstd(q)
    qhat = q * rq
    qn = (qhat * qw_ref[...]).astype(BF)
    s = _dot_nt(qn, kn_s[0:lk, :]) * scale
    lane = lax.broadcasted_iota(jnp.int32, (tq, LANES), 1)
    cq = jnp.sum(jnp.where(lane == h, cum_ref[...], 0.0), axis=1, keepdims=True)
    ck = cumt_ref[pl.ds(h, 1), 0:lk]
    s = s + (cq - ck)
    qpos = i * tq + lax.broadcasted_iota(jnp.int32, (tq, lk), 0)
    kpos = lax.broadcasted_iota(jnp.int32, (tq, lk), 1)
    s = jnp.where(qpos >= kpos, s, NEG_BIG)
    e = jnp.exp(s - jnp.max(s, axis=1, keepdims=True))
    p = e * (1.0 / jnp.sum(e, axis=1, keepdims=True))
    return p, qn, qhat, rq


def _per_query_tile(i, nq, tq, lp, fn):
    for t in range(nq):
        lk = min(lp, -(-((t + 1) * tq) // LANES) * LANES)
        pl.when(i == t)(functools.partial(fn, lk))


def _att_fwd(z, cum, cumt, qw, kw, n_heads, qblk0, tq, name, comm=None):
    lp = z.shape[0]
    nh = n_heads

    def body(q_ref, k_ref, v_ref, cum_ref, cumt_ref, qw_ref, kw_ref, o_ref, kn_s, vb_s):
        h, i = pl.program_id(0), pl.program_id(1)

        @pl.when(i == 0)
        def _():
            k = k_ref[...]
            kn_s[...] = (k * _rstd(k) * kw_ref[...]).astype(BF)
            vb_s[...] = v_ref[...].astype(BF)

        def tile(lk):
            p, _, _, _ = _att_scores(q_ref, cum_ref, cumt_ref, qw_ref, kn_s, h, i, tq, lk)
            o_ref[...] = _dot(p.astype(BF), vb_s[0:lk, :]).astype(BF)

        _per_query_tile(i, lp // tq, tq, lp, tile)

    vec = pl.BlockSpec((1, HEAD_DIM), lambda h, i: (0, 0))
    return _call(
        body, grid=(nh, lp // tq),
        in_specs=[pl.BlockSpec((tq, HEAD_DIM), lambda h, i: (i, qblk0 + h)),
                  pl.BlockSpec((lp, HEAD_DIM), lambda h, i: (0, qblk0 + nh + h)),
                  pl.BlockSpec((lp, HEAD_DIM), lambda h, i: (0, qblk0 + 2 * nh + h)),
                  pl.BlockSpec((tq, LANES), lambda h, i: (i, 0)),
                  pl.BlockSpec((nh, lp), lambda h, i: (0, 0)), vec, vec],
        out_specs=pl.BlockSpec((tq, HEAD_DIM), lambda h, i: (i, h)),
        out_shape=SDS((lp, nh * HEAD_DIM), BF),
        scratch_shapes=[pltpu.VMEM((lp, HEAD_DIM), BF), pltpu.VMEM((lp, HEAD_DIM), BF)],
        vmem_mib=48, name=name, comm=comm)(z, z, z, cum, cumt, qw, kw)


def _att_bwd(z, cum, cumt, qw, kw, dmix, n_heads, qblk0, oblk0, tq, name, comm=None):
    lp = z.shape[0]
    nh = n_heads
    nq = lp // tq
    scale = 1.0 / (HEAD_DIM ** 0.5)

    def body(q_ref, k_ref, v_ref, cum_ref, cumt_ref, qw_ref, kw_ref, do_ref,
             dq_ref, dk_ref, dv_ref, dck_ref, dqw_ref, dkw_ref,
             kn_s, vb_s, dkn_s, dv_s, dck_s):
        h, i = pl.program_id(0), pl.program_id(1)

        @pl.when((h == 0) & (i == 0))
        def _():
            dqw_ref[...] = jnp.zeros_like(dqw_ref)
            dkw_ref[...] = jnp.zeros_like(dkw_ref)

        @pl.when(i == 0)
        def _():
            k = k_ref[...]
            kn_s[...] = (k * _rstd(k) * kw_ref[...]).astype(BF)
            vb_s[...] = v_ref[...].astype(BF)
            dkn_s[...] = jnp.zeros_like(dkn_s)
            dv_s[...] = jnp.zeros_like(dv_s)
            dck_s[...] = jnp.zeros_like(dck_s)

        def tile(lk):
            p, qn, qhat, rq = _att_scores(q_ref, cum_ref, cumt_ref, qw_ref, kn_s, h, i, tq, lk)
            dob = do_ref[...]
            dp = _dot_nt(dob, vb_s[0:lk, :])
            ds = p * (dp - jnp.sum(p * dp, axis=1, keepdims=True))
            dsb = ds.astype(BF)
            dv_s[0:lk, :] += _dot_tn(p.astype(BF), dob)
            dkn_s[0:lk, :] += _dot_tn(dsb, qn)
            dck_s[:, 0:lk] += jnp.sum(ds, axis=0, keepdims=True)
            dqn = _dot(dsb, kn_s[0:lk, :]) * scale
            gq = dqn * qw_ref[...]
            dq_ref[...] = (rq * (gq - qhat * jnp.mean(gq * qhat, axis=-1, keepdims=True))).astype(BF)
            dqw_ref[...] += _rows8(dqn * qhat)

        _per_query_tile(i, nq, tq, lp, tile)

        @pl.when(i == nq - 1)
        def _():
            k = k_ref[...]
            rk = _rstd(k)
            khat = k * rk
            dkn = dkn_s[...] * scale
            gk = dkn * kw_ref[...]
            dk_ref[...] = (rk * (gk - khat * jnp.mean(gk * khat, axis=-1, keepdims=True))).astype(BF)
            dkw_ref[...] += _rows8(dkn * khat)
            dv_ref[...] = dv_s[...].astype(BF)
            dck_ref[...] = dck_s[...]

    vec = pl.BlockSpec((1, HEAD_DIM), lambda h, i: (0, 0))
    part = pl.BlockSpec((SUBLANES, LANES), lambda h, i: (0, 0))
    return _call(
        body, grid=(nh, nq),
        in_specs=[pl.BlockSpec((tq, HEAD_DIM), lambda h, i: (i, qblk0 + h)),
                  pl.BlockSpec((lp, HEAD_DIM), lambda h, i: (0, qblk0 + nh + h)),
                  pl.BlockSpec((lp, HEAD_DIM), lambda h, i: (0, qblk0 + 2 * nh + h)),
                  pl.BlockSpec((tq, LANES), lambda h, i: (i, 0)),
                  pl.BlockSpec((nh, lp), lambda h, i: (0, 0)), vec, vec,
                  pl.BlockSpec((tq, HEAD_DIM), lambda h, i: (i, oblk0 + h))],
        out_specs=[pl.BlockSpec((tq, HEAD_DIM), lambda h, i: (i, h)),
                   pl.BlockSpec((lp, HEAD_DIM), lambda h, i: (0, h)),
                   pl.BlockSpec((lp, HEAD_DIM), lambda h, i: (0, h)),
                   pl.BlockSpec((None, 1, lp), lambda h, i: (h, 0, 0)),
                   part, part],
        out_shape=[SDS((lp, nh * HEAD_DIM), BF)] * 3
        + [SDS((nh, 1, lp), F32), SDS((SUBLANES, LANES), F32), SDS((SUBLANES, LANES), F32)],
        scratch_shapes=[pltpu.VMEM((lp, HEAD_DIM), BF), pltpu.VMEM((lp, HEAD_DIM), BF),
                        pltpu.VMEM((lp, HEAD_DIM), F32), pltpu.VMEM((lp, HEAD_DIM), F32),
                        pltpu.VMEM((1, lp), F32)],
        vmem_mib=56, name=name, comm=comm)(z, z, z, cum, cumt, qw, kw, dmix)


def _adamw_math(w, g, m, v):
    m2 = ADAM_B1 * m + (1.0 - ADAM_B1) * g
    v2 = ADAM_B2 * v + (1.0 - ADAM_B2) * (g * g)
    m_hat = m2 / (1.0 - ADAM_B1 ** ADAM_STEP)
    v_hat = v2 / (1.0 - ADAM_B2 ** ADAM_STEP)
    delta = -ADAM_LR * (m_hat / (jnp.sqrt(v_hat) + ADAM_EPS) + ADAM_WD * w)
    return delta, m2, v2


def _adamw(g_in, w, m, v, name, comm=None):
    r, c = w.shape
    partial_sum = g_in.ndim == 3
    lane_padded = -(-c // LANES) * LANES
    tr = _largest_tile(r, max(16, MIB // (4 * lane_padded) // 16 * 16), 16)

    def body(g_ref, w_ref, m_ref, v_ref, go_ref, d_ref, mo_ref, vo_ref):
        if partial_sum:
            g = g_ref[0].astype(F32)
            for k in range(1, g_in.shape[0]):
                g = g + g_ref[k].astype(F32)
        else:
            g = g_ref[...]
        delta, m2, v2 = _adamw_math(w_ref[...], g, m_ref[...], v_ref[...])
        go_ref[...] = g
        d_ref[...] = delta
        mo_ref[...] = m2
        vo_ref[...] = v2

    blk = pl.BlockSpec((tr, c), lambda i: (i, 0))
    g_spec = pl.BlockSpec((g_in.shape[0], tr, c), lambda i: (0, i, 0)) if partial_sum else blk
    return _call(
        body, grid=(r // tr,), in_specs=[g_spec, blk, blk, blk], out_specs=[blk] * 4,
        out_shape=[SDS((r, c), F32)] * 4, vmem_mib=40, name=name, comm=comm)(g_in, w, m, v)


def _peer(x, y, c, k):
    return (1 - x if k & 4 else x, 1 - y if k & 2 else y, 1 - c if k & 1 else c)


_SIBLING = 1
_ICI_RELS = (2, 4, 6)


def _mesh_pos():
    return lax.axis_index("x"), lax.axis_index("y"), lax.axis_index("c")


def _sem_pair(sems, t, j, n_rel, scalars):
    if scalars:
        i = 2 * (t * n_rel + j)
        return sems[i], sems[i + 1]
    return sems[0].at[t, j], sems[1].at[t, j]


def _dev(pos):
    return 4 * pos[0] + 2 * pos[1] + pos[2]


def _gather_ici(shards, landing=None, rels=(_SIBLING,) + _ICI_RELS):
    n = len(shards)

    def remote(ins, outs, sems, arrival):
        x, y, c = _mesh_pos()
        dst = ins[n:] if landing is not None else outs
        cps = []
        for j, k in enumerate(rels):
            peer = _peer(x, y, c, k)
            slot = _dev(peer) if arrival else _dev((x, y, c))
            for t in range(n):
                send_sem, recv_sem = _sem_pair(sems, t, j, len(rels), landing is not None)
                cps.append(pltpu.make_async_remote_copy(
                    src_ref=ins[t], dst_ref=dst[t].at[slot], send_sem=send_sem, recv_sem=recv_sem,
                    device_id=peer, device_id_type=pl.DeviceIdType.MESH))
        return cps

    if landing is not None:
        def start_remote(ins, outs, sems):
            for cp in remote(ins, outs, sems, False):
                cp.start()

        def finish_remote(ins, outs, sems):
            for cp in remote(ins, outs, sems, True):
                cp.wait_recv()
            for cp in remote(ins, outs, sems, False):
                cp.wait_send()

        return _Comm(list(shards) + list(landing), [SDS(a.shape, a.dtype) for a in landing],
                     [pltpu.SemaphoreType.DMA(())] * (2 * n * len(rels)),
                     start_remote, finish_remote, aliases={n + t: t for t in range(n)})

    def local(ins, outs, sems):
        me = _dev(_mesh_pos())
        return [pltpu.make_async_copy(ins[t], outs[t].at[me], sems[2].at[t]) for t in range(n)]

    def start(ins, outs, sems):
        for cp in local(ins, outs, sems) + remote(ins, outs, sems, False):
            cp.start()

    def finish(ins, outs, sems):
        for cp in local(ins, outs, sems):
            cp.wait()
        for cp in remote(ins, outs, sems, True):
            cp.wait_recv()
        for cp in remote(ins, outs, sems, False):
            cp.wait_send()

    return _Comm(shards, [SDS((N_DEV,) + s.shape, s.dtype) for s in shards],
                 [pltpu.SemaphoreType.DMA((n, len(rels))), pltpu.SemaphoreType.DMA((n, len(rels))),
                  pltpu.SemaphoreType.DMA((n,))], start, finish)


def _gather_diagonal(zones):
    n = len(zones)

    def copies(ins, outs, sems, arrival):
        x, y, c = _mesh_pos()
        y_nb, x_nb, diag = _peer(x, y, c, 2), _peer(x, y, c, 4), _peer(x, y, c, 6)
        cps = []
        for j, (to, origin) in enumerate(((y_nb, x_nb), (x_nb, y_nb))):
            slot = _dev(diag) if arrival else _dev(origin)
            for t in range(n):
                half = ins[t].shape[1] // 2
                rows = ins[t].at[slot, pl.ds(j * half, half)]
                send_sem, recv_sem = _sem_pair(sems, t, j, 2, True)
                cps.append(pltpu.make_async_remote_copy(
                    src_ref=rows, dst_ref=rows, send_sem=send_sem, recv_sem=recv_sem,
                    device_id=to, device_id_type=pl.DeviceIdType.MESH))
        return cps

    def start(ins, outs, sems):
        for cp in copies(ins, outs, sems, False):
            cp.start()

    def finish(ins, outs, sems):
        for cp in copies(ins, outs, sems, True):
            cp.wait_recv()
        for cp in copies(ins, outs, sems, False):
            cp.wait_send()

    return _Comm(list(zones), [SDS(a.shape, a.dtype) for a in zones], [pltpu.SemaphoreType.DMA(())] * (4 * n),
                 start, finish, aliases={t: t for t in range(n)})


def _gather_fwd(partial):
    n = len(partial)

    def copies(ins, outs, sems, arrival):
        x, y, c = _mesh_pos()
        sibling = _peer(x, y, c, _SIBLING)
        cps = []
        for j, k in enumerate(_ICI_RELS):
            slot = _dev(_peer(x, y, c, k | _SIBLING if arrival else k))
            for t in range(n):
                cps.append(pltpu.make_async_remote_copy(
                    src_ref=ins[t].at[slot], dst_ref=outs[t].at[slot], send_sem=sems[0].at[t, j],
                    recv_sem=sems[1].at[t, j], device_id=sibling, device_id_type=pl.DeviceIdType.MESH))
        return cps

    def start(ins, outs, sems):
        for cp in copies(ins, outs, sems, False):
            cp.start()

    def finish(ins, outs, sems):
        for cp in copies(ins, outs, sems, True):
            cp.wait_recv()
        for cp in copies(ins, outs, sems, False):
            cp.wait_send()

    return _Comm(partial, [SDS(a.shape, a.dtype) for a in partial],
                 [pltpu.SemaphoreType.DMA((n, len(_ICI_RELS)))] * 2, start, finish,
                 aliases={t: t for t in range(n)})


def _scatter_sibling(slots):
    n = len(slots)

    def copies(ins, outs, sems):
        x, y, c = _mesh_pos()
        return [pltpu.make_async_remote_copy(
            src_ref=ins[t].at[:, 1 - c], dst_ref=outs[t], send_sem=sems[0].at[t], recv_sem=sems[1].at[t],
            device_id=_peer(x, y, c, _SIBLING), device_id_type=pl.DeviceIdType.MESH) for t in range(n)]

    def start(ins, outs, sems):
        for cp in copies(ins, outs, sems):
            cp.start()

    def finish(ins, outs, sems):
        for cp in copies(ins, outs, sems):
            cp.wait()

    return _Comm(slots, [SDS((s.shape[0],) + s.shape[2:], s.dtype) for s in slots],
                 [pltpu.SemaphoreType.DMA((n,))] * 2, start, finish)


def _scatter_ici(chip_sums, landing=None):
    n = len(chip_sums)

    def remote(ins, outs, sems, arrival):
        x, y, c = _mesh_pos()
        dst = ins[n:] if landing is not None else outs
        cps = []
        for j, k in enumerate(_ICI_RELS):
            peer = _peer(x, y, c, k)
            theirs, mine = 2 * peer[0] + peer[1], 2 * x + y
            for t in range(n):
                send_sem, recv_sem = _sem_pair(sems, t, j, len(_ICI_RELS), landing is not None)
                cps.append(pltpu.make_async_remote_copy(
                    src_ref=ins[t].at[theirs], dst_ref=dst[t].at[theirs if arrival else mine],
                    send_sem=send_sem, recv_sem=recv_sem,
                    device_id=peer, device_id_type=pl.DeviceIdType.MESH))
        return cps

    if landing is not None:
        def start_remote(ins, outs, sems):
            for cp in remote(ins, outs, sems, False):
                cp.start()

        def finish_remote(ins, outs, sems):
            for cp in remote(ins, outs, sems, True):
                cp.wait_recv()
            for cp in remote(ins, outs, sems, False):
                cp.wait_send()

        return _Comm(list(chip_sums) + list(landing), [SDS(a.shape, a.dtype) for a in landing],
                     [pltpu.SemaphoreType.DMA(())] * (2 * n * len(_ICI_RELS)), start_remote, finish_remote,
                     aliases={n + t: t for t in range(n)})

    def local(ins, outs, sems):
        x, y, _ = _mesh_pos()
        return [pltpu.make_async_copy(ins[t].at[2 * x + y], outs[t].at[2 * x + y], sems[2].at[t]) for t in range(n)]

    def start(ins, outs, sems):
        for cp in local(ins, outs, sems) + remote(ins, outs, sems, False):
            cp.start()

    def finish(ins, outs, sems):
        for cp in local(ins, outs, sems):
            cp.wait()
        for cp in remote(ins, outs, sems, True):
            cp.wait_recv()
        for cp in remote(ins, outs, sems, False):
            cp.wait_send()

    return _Comm(chip_sums, [SDS(a.shape, a.dtype) for a in chip_sums],
                 [pltpu.SemaphoreType.DMA((n, len(_ICI_RELS))), pltpu.SemaphoreType.DMA((n, len(_ICI_RELS))),
                  pltpu.SemaphoreType.DMA((n,))], start, finish)


def _chip_sum(slots, from_sibling, core, name):
    nq, _, r, c = slots.shape
    tr = _largest_tile(r, 1024, 16)

    def body(core_ref, a_ref, b_ref, o_ref):
        o_ref[...] = (a_ref[...].astype(F32) + b_ref[...].astype(F32)).astype(BF)

    return pl.pallas_call(
        body,
        grid_spec=pltpu.PrefetchScalarGridSpec(
            num_scalar_prefetch=1, grid=(nq, r // tr),
            in_specs=[pl.BlockSpec((None, None, tr, c), lambda q, i, core_ref: (q, core_ref[0], i, 0)),
                      pl.BlockSpec((None, tr, c), lambda q, i, core_ref: (q, i, 0))],
            out_specs=pl.BlockSpec((None, tr, c), lambda q, i, core_ref: (q, i, 0))),
        out_shape=SDS((nq, r, c), BF), compiler_params=pltpu.CompilerParams(vmem_limit_bytes=40 * MIB),
        name=name)(core, slots, from_sibling)


def _small_reduce(pack_g, meta_g, loss_scale, name, comm=None):
    w = pack_g.shape[2]

    def body(p_ref, m_ref, tot_ref, meta_ref, loss_ref):
        acc = p_ref[0]
        macc = m_ref[0]
        for k in range(1, N_DEV):
            acc = acc + p_ref[k]
            macc = macc + m_ref[k]
        tot = jnp.sum(acc, axis=0, keepdims=True)
        tot_ref[...] = tot
        meta_ref[...] = macc
        loss_ref[...] = jnp.full((1, LANES), loss_scale * jnp.sum(tot[:, w - LANES:w]), F32)

    return pl.pallas_call(
        body, out_shape=[SDS((1, w), F32), SDS(meta_g.shape[1:], F32), SDS((1, LANES), F32)],
        compiler_params=pltpu.CompilerParams(vmem_limit_bytes=32 * MIB), name=name)(pack_g, meta_g)


def _local_step(x, target, sw, plan):
    s_len, d = x.shape
    n_heads, n_meta = plan.n_heads, plan.n_meta
    l = n_meta + s_len
    lp = -(-l // LANES) * LANES
    tm = _largest_tile(lp, 544, 16)
    tq = _largest_tile(lp, 272, 16)
    te = _largest_tile(lp, 272, 16)
    tmd = _largest_tile(d, 512, LANES)

    plan.at("start")
    x, target = plan.gate((x, target))
    zmeta, zpad = jnp.zeros((n_meta, d), F32), jnp.zeros((lp - l, d), F32)
    h0 = jnp.concatenate([zmeta, x, zpad], axis=0)
    tpad = jnp.concatenate([zmeta, target, zpad], axis=0)
    plan.at("landed", (h0, tpad))
    h0 = lax.dynamic_update_slice(h0, plan.weights("meta"), (0, 0))

    split = plan.ffn1_split()
    if split is None:
        wg1, wu1, wd1 = plan.weights("ffn1")
        h1, a1, b1, u1 = _ffn_fwd(h0, sw["ffn1_norm"], wg1, wu1, wd1, tm, "ffn1_fwd", plan.comm("ffn1_fwd"))
    else:
        carry = _ffn_fwd_part(h0, sw["ffn1_norm"], *plan.weights("ffn1_landing"), split[0], None, tm, "ffn1_fwd_a",
                              plan.order_tokens())
        plan.at("ffn1_mid", (carry[0],))
        wg1, wu1, wd1 = plan.weights("ffn1")
        h1, a1, b1 = _ffn_fwd_part(h0, sw["ffn1_norm"], wg1, wu1, wd1, split[1], carry, tm, "ffn1_fwd_b",
                                   plan.order_tokens())
        u1 = carry[3]
    fs = wg1.shape[1]
    plan.at("after_ffn1_fwd", (h1,))
    win, pw, wout = plan.weights("mix")
    nz = win.shape[1]
    p_w = sw["pool_scale"].shape[1]
    npb = p_w // LANES
    fblk = nz // LANES - 1
    tnz = _largest_tile(nz, 1408, LANES)
    qw, kw, bfp, ps = sw["q_norm"], sw["k_norm"], sw["b_forget"], sw["pool_scale"]
    z, u2 = _norm_matmul(h1, sw["mix_norm"], win, tm, tnz, "mix_in", plan.comm("mix_in"))
    plan.at("after_mix_in", (u2,))
    cum = _fox_prep(z, bfp, fblk, "fox_prep")
    cumt = cum[:, :n_heads].T
    pool_o = _pool_fwd(z, pw, ps, "pool_fwd")
    att_o = _att_fwd(z, cum, cumt, qw, kw, n_heads, npb, tq, "att_fwd", plan.comm("att_fwd"))
    plan.at("after_att_fwd", (att_o,))
    h2 =_out_proj(h1, pool_o, att_o, wout, tm, "out_proj", plan.comm("out_proj"))
    wg2, wu2, wd2 = plan.weights("ffn2")
    h3, a2, b2, u3 = _ffn_fwd(h2, sw["ffn2_norm"], wg2, wu2, wd2, tm, "ffn2_fwd", plan.comm("ffn2_fwd"))
    dy, dob3, lsq = _loss_head(h3, tpad, n_meta, l, te, "loss_head")

    du3, da2, db2, hid2 = _ffn_bwd_dx(dob3, a2, b2, wg2, wu2, wd2, tm, "ffn2_bwd_dx", plan.comm("ffn2_bwd_dx"))
    dh2, dh2b, dn2 = _rms_bwd(du3, h2, sw["ffn2_norm"], dy, 1.0, te, "ffn2_rms_bwd")
    plan.grad("ffn2_w_gate", _matmul_tn(da2, u3, fs, d, "ffn2_dwg", plan.comm("ffn2_dwg")))
    plan.grad("ffn2_w_up", _matmul_tn(db2, u3, fs, d, "ffn2_dwu", plan.comm("ffn2_dwu")))
    plan.grad("ffn2_w_down", _matmul_tn(hid2, dob3, fs, d, "ffn2_dwd", plan.comm("ffn2_dwd")))
    plan.at("after_ffn2_dwd")

    dmix = _matmul_nt(dh2b, wout, tm, d, BF, "out_proj_bwd", plan.comm("out_proj_bwd"))
    plan.at("after_out_proj_bwd")
    tmp = _largest_tile(p_w, 512, LANES)
    plan.grad("w_out", jnp.concatenate([_matmul_tn(pool_o, dh2b, tmp, d, "dwout_pool"),
                                        _matmul_tn(att_o, dh2b, tmp, d, "dwout_att")], axis=0))
    dzp, dpw, dps = _pool_bwd(z, dmix, pw, ps, "pool_bwd")
    plan.grad("pool_w", dpw)
    plan.at("before_att_bwd")
    dq, dk, dv, dck, dqw, dkw = _att_bwd(z, cum, cumt, qw, kw, dmix, n_heads, npb, npb, tq, "att_bwd",
                                              plan.comm("att_bwd"))
    dcum = -dck[:, 0, :].T
    dcum = jnp.pad(dcum, ((0, 0), (0, LANES - n_heads)))
    dzf, dbf = _fox_bwd(z, bfp, dcum, fblk, "fox_bwd")
    dz = jnp.concatenate([dzp, dq, dk, dv, dzf], axis=1)
    plan.grad("w_in", _matmul_tn(u2, dz, tmd, tnz, "dwin", plan.comm("dwin")))
    du2 = _matmul_nt(dz, win, tm, tnz, F32, "mix_in_bwd", plan.comm("mix_in_bwd"))
    plan.at("before_ffn1_bwd_dx")
    dh1, dob1, dnm = _rms_bwd(du2, h1, sw["mix_norm"], dh2, 0.5, te, "mix_rms_bwd")

    du1, da1, db1, hid1 = _ffn_bwd_dx(dob1, a1, b1, wg1, wu1, wd1, tm, "ffn1_bwd_dx", plan.comm("ffn1_bwd_dx"))
    plan.grad("ffn1_w_gate", _matmul_tn(da1, u1, fs, d, "ffn1_dwg", plan.comm("ffn1_dwg")))
    plan.grad("ffn1_w_up", _matmul_tn(db1, u1, fs, d, "ffn1_dwu", plan.comm("ffn1_dwu")))
    plan.at("before_ffn1_dwd")
    plan.grad("ffn1_w_down", _matmul_tn(hid1, dob1, fs, d, "ffn1_dwd", plan.comm("ffn1_dwd")))
    plan.at("after_ffn1_dwd")
    dh0, _, dn1 = _rms_bwd(du1, h0, sw["ffn1_norm"], dh1, 1.0, te, "ffn1_rms_bwd", plan.comm("ffn1_rms_bwd"))

    small = [dn1, dnm, dn2, dps, dqw, dkw, dbf, lsq]
    return dh0[n_meta:l], dh0[:n_meta], small


_BIG = ("ffn1_w_gate", "ffn1_w_up", "ffn1_w_down", "w_in", "pool_w", "w_out", "ffn2_w_gate", "ffn2_w_up", "ffn2_w_down")
_SMALL = ("ffn1_norm", "mix_norm", "ffn2_norm", "pool_scale", "q_norm", "k_norm", "b_forget")
_ORDER = ("meta_tokens", "ffn1_norm", "ffn1_w_gate", "ffn1_w_up", "ffn1_w_down", "mix_norm", "w_in", "b_forget",
          "q_norm", "k_norm", "pool_w", "pool_scale", "w_out", "ffn2_norm", "ffn2_w_gate", "ffn2_w_up", "ffn2_w_down")


_FFN1 = ("ffn1_w_gate", "ffn1_w_up", "ffn1_w_down")
_FFN2 = ("ffn2_w_gate", "ffn2_w_up", "ffn2_w_down")
_MIX = ("w_in", "pool_w", "w_out")

_RIDES = {
    "out_proj": (("g2", _FFN2),),
    "ffn2_dwu": (("s1", ("ffn2_w_gate",)),),
    "ffn2_dwd": (("s1", ("ffn2_w_up",)),),
    "out_proj_bwd": (("s1", ("ffn2_w_down",)),),
    "mix_in_bwd": (("s1", _MIX),),
    "ffn1_dwu": (("s1", ("ffn1_w_gate",)),),
    "ffn1_dwd": (("s1", ("ffn1_w_up",)),),
    "ffn1_rms_bwd": (("s1", ("ffn1_w_down",)),),
}
_META = ("meta_tokens",)
_POINTS = {
    "start": (("start", "gm", _META), ("start", "g1a", _FFN1), ("gate", _MIX + _FFN2), ("prepare", "g1", _MIX),
              ("prepare", "g1a", _FFN2)),
    "landed": (("wait", "gm", _META), ("wait", "g1a", _FFN1), ("start", "g1b", _FFN1), ("start", "g1", _MIX),
               ("start", "g1a", _FFN2)),
    "ffn1_mid": (("wait", "g1b", _FFN1), ("alone", "g2", _FFN1)),
    "after_ffn1_fwd": (("wait", "g1", _MIX), ("alone", "g2", _MIX)),
    "after_mix_in": (("wait", "g1a", _FFN2), ("start", "g1b", _FFN2)),
    "after_att_fwd": (("wait", "g1b", _FFN2),),
    "after_ffn2_dwd": (("sum", ("ffn2_w_gate",)), ("start", "s2", ("ffn2_w_gate",))),
    "after_out_proj_bwd": (("sum", ("ffn2_w_up",)), ("start", "s2", ("ffn2_w_up",))),
    "before_att_bwd": (("sum", ("ffn2_w_down",)), ("start", "s2", ("ffn2_w_down",))),
    "before_ffn1_bwd_dx": (("sum", _MIX), ("start", "s2", _MIX)),
    "before_ffn1_dwd": (("sum", ("ffn1_w_gate",)), ("start", "s2", ("ffn1_w_gate",))),
    "after_ffn1_dwd": (("sum", ("ffn1_w_up",)), ("start", "s2", ("ffn1_w_up",))),
    "after_ffn1_rms_bwd": (("sum", ("ffn1_w_down",)), ("start", "s2", ("ffn1_w_down",))),
    "before_adamw_ffn2_w_gate": (("wait", "s2", ("ffn2_w_gate",)),),
    "before_adamw_ffn2_w_up": (("wait", "s2", ("ffn2_w_up",)),),
    "before_adamw_ffn2_w_down": (("wait", "s2", ("ffn2_w_down",)),),
    "before_adamw_w_in": (("wait", "s2", _MIX),),
    "before_adamw_ffn1_w_gate": (("wait", "s2", ("ffn1_w_gate",)),),
    "before_adamw_ffn1_w_up": (("wait", "s2", ("ffn1_w_up",)),),
    "before_adamw_ffn1_w_down": (("wait", "s2", ("ffn1_w_down",)),),
}


def _own_slot_filled(block, slot, n_slots):
    zone = lax.empty((n_slots,) + block.shape, block.dtype)
    return lax.dynamic_update_slice(zone, block[None], (slot,) + (0,) * block.ndim)


class _MeshPlan:
    def __init__(self, raw, pos, d, d_in, n_heads):
        self.raw, self.pos = dict(raw), pos
        self.core = pos[2].astype(jnp.int32).reshape(1)
        self.d, self.d_in, self.n_heads, self.n_meta = d, d_in, n_heads, raw["meta_tokens"].shape[0]
        self.partial, self.full, self.slots, self.from_sibling, self.chip_sum, self.received = {}, {}, {}, {}, {}, {}
        self.partial_a, self.pending, self.prepared, self.started, self.tokens = {}, [], {}, {}, []

    def gate(self, arrays):
        gated = lax.optimization_barrier((self.tokens[-1], tuple(arrays)))
        self.tokens[-1] = gated[0]
        return gated[1]

    def _phase(self, kind, names):
        src, dst, make = {"g2": (self.partial, self.full, _gather_fwd),
                          "s1": (self.slots, self.from_sibling, _scatter_sibling),
                          "s2": (self.chip_sum, self.received, _scatter_ici)}[kind]
        op = make([src[n] for n in names])
        self.pending.append((op, dst, names))
        return op

    def _settle(self):
        for op, dst, names in self.pending:
            dst.update(zip(names, op.results))
        self.pending = []

    def _prepare(self, kind, names):
        x, y, c = self.pos
        if kind in ("g1", "g1a", "gm"):
            blocks = [self.raw[n] if kind == "gm" else self.raw[n].astype(BF) for n in names]
            rels = {"g1": (_SIBLING,) + _ICI_RELS, "g1a": (_SIBLING,) + _ICI_RELS[:2], "gm": tuple(range(1, N_DEV))}[kind]
            op = _gather_ici(blocks, [_own_slot_filled(b, 4 * x + 2 * y + c, N_DEV) for b in blocks], rels)
        elif kind == "g1b":
            op = _gather_diagonal([self.partial_a[n] for n in names])
        else:
            sums = [self.chip_sum[n] for n in names]
            mine = [lax.dynamic_index_in_dim(s, 2 * x + y, 0, keepdims=False) for s in sums]
            op = _scatter_ici(sums, [_own_slot_filled(b, 2 * x + y, N_DEV // 2) for b in mine])
        self.prepared[(kind, names)] = op

    def _start(self, kind, names):
        if (kind, names) not in self.prepared:
            self._prepare(kind, names)
        self._launch((kind, names), self.prepared.pop((kind, names)), "_".join(("start", kind, names[0])))

    def _launch(self, key, op, name):
        if self.tokens:
            op.arrs = list(self.gate(op.arrs))
        self.started[key], token = _split_start(op, name)
        self.tokens.append(token)

    def start_small_gather(self, arrays):
        x, y, c = self.pos
        zones = [_own_slot_filled(a, 4 * x + 2 * y + c, N_DEV) for a in arrays]
        self._launch("small", _gather_ici(list(arrays), zones, rels=tuple(range(1, N_DEV))), "start_gather_small")

    def wait_small_gather(self, afters):
        return _split_wait(self.started.pop("small"), afters, "wait_gather_small")

    def _wait(self, kind, names, afters):
        afters = list(afters) + [a for op in self.prepared.values() for a in op.arrs[len(op.arrs) // 2:]]
        landed = _split_wait(self.started.pop((kind, names)), afters, "_".join(("wait", kind, names[0])))
        {"g1": self.partial, "g1a": self.partial_a, "g1b": self.partial, "gm": self.partial,
         "s2": self.received}[kind].update(zip(names, landed))

    def ffn1_split(self):
        x, y, c = self.pos
        first = [(x, y, c), _peer(x, y, c, 1), _peer(x, y, c, 4), _peer(x, y, c, 2)]
        last = [_peer(x, y, c, 6), _peer(x, y, c, 5), _peer(x, y, c, 3), _peer(x, y, c, 7)]
        return tuple(jnp.stack([_dev(p) for p in part]).astype(jnp.int32) for part in (first, last))

    def order_tokens(self):
        tokens, self.tokens = self.tokens, []
        return tokens

    def comm(self, kernel_name):
        self._settle()
        ops = [self._phase(kind, names) for kind, names in _RIDES.get(kernel_name, ())]
        if self.tokens:
            ops.append(_Comm(self.tokens, [], [], lambda *a: None, lambda *a: None))
            self.tokens = []
        return _merge_comm(ops)

    def at(self, point, after=()):
        for step in _POINTS.get(point, ()):
            self._settle()
            if step[0] == "alone":
                _comm_alone(self._phase(step[1], step[2]), "_".join((step[1], point)))
            elif step[0] == "start":
                self._start(step[1], step[2])
            elif step[0] == "prepare":
                self._prepare(step[1], step[2])
            elif step[0] == "gate":
                self.raw.update(zip(step[1], self.gate([self.raw[n] for n in step[1]])))
            elif step[0] == "wait":
                self._wait(step[1], step[2], tuple(after) + tuple(self.tokens[-1:]))
            else:
                for n in step[1]:
                    self.chip_sum[n] = _chip_sum(self.slots[n], self.from_sibling[n], self.core, "chip_sum_" + n)

    def weights(self, group):
        self._settle()
        f, d = self.full, self.d
        if group == "meta":
            g = self.partial["meta_tokens"]
            return g.transpose(1, 0, 2).reshape(g.shape[1], d)
        if group == "ffn1_landing":
            return tuple(self.started[("g1b", _FFN1)][2])
        if group == "ffn1":
            return tuple(f[n] for n in _FFN1)
        if group == "ffn2":
            return tuple(f[n] for n in _FFN2)
        n_main = self.d_in - self.n_heads
        win = f["w_in"].transpose(1, 0, 2).reshape(d, self.d_in)
        win = jnp.concatenate([win[:, :n_main], jnp.pad(win[:, n_main:], ((0, 0), (0, LANES - self.n_heads)))], axis=1)
        pw = f["pool_w"]
        gw = pw.shape[2]
        pw = pw.reshape(N_DEV, -1, gw // N_DEV, gw).transpose(1, 0, 2, 3).reshape(-1, gw, gw)
        return win, pw, f["w_out"].reshape(-1, d)

    def grad(self, name, g):
        d = self.d
        if name == "w_in":
            g = g[:, :self.d_in].reshape(d, N_DEV, -1).transpose(1, 0, 2)
        elif name == "pool_w":
            ng, gw = g.shape[0], g.shape[2]
            g = g.astype(BF).reshape(ng, N_DEV, -1, gw).transpose(1, 0, 2, 3).reshape(N_DEV, -1, gw)
        elif name == "w_out":
            g = g.reshape(N_DEV, -1, d)
        self.slots[name] = g.reshape((N_DEV // 2, 2) + g.shape[1:])

    def gradient_parts(self, name):
        self._settle()
        return self.received[name]


_TRANSPOSED = ("ffn1_w_gate", "ffn1_w_up", "ffn2_w_gate", "ffn2_w_up")


def _as2d(name, a):
    return a[0].T if name in _TRANSPOSED else a.reshape(-1, a.shape[-1])


def _from2d(name, a2d, shape):
    return a2d.T.reshape(shape) if name in _TRANSPOSED else a2d.reshape(shape)


def kernel(x, meta_tokens, ffn1_norm, ffn1_w_gate, ffn1_w_up, ffn1_w_down, mix_norm, w_in, b_forget, q_norm, k_norm, pool_w, pool_scale, w_out, ffn2_norm, ffn2_w_gate, ffn2_w_up, ffn2_w_down, loss_target, m_meta_tokens, m_ffn1_norm, m_ffn1_w_gate, m_ffn1_w_up, m_ffn1_w_down, m_mix_norm, m_w_in, m_b_forget, m_q_norm, m_k_norm, m_pool_w, m_pool_scale, m_w_out, m_ffn2_norm, m_ffn2_w_gate, m_ffn2_w_up, m_ffn2_w_down, v_meta_tokens, v_ffn1_norm, v_ffn1_w_gate, v_ffn1_w_up, v_ffn1_w_down, v_mix_norm, v_w_in, v_b_forget, v_q_norm, v_k_norm, v_pool_w, v_pool_scale, v_w_out, v_ffn2_norm, v_ffn2_w_gate, v_ffn2_w_up, v_ffn2_w_down):
    w = dict(meta_tokens=meta_tokens, ffn1_norm=ffn1_norm, ffn1_w_gate=ffn1_w_gate, ffn1_w_up=ffn1_w_up,
             ffn1_w_down=ffn1_w_down, mix_norm=mix_norm, w_in=w_in, b_forget=b_forget, q_norm=q_norm, k_norm=k_norm,
             pool_w=pool_w, pool_scale=pool_scale, w_out=w_out, ffn2_norm=ffn2_norm, ffn2_w_gate=ffn2_w_gate,
             ffn2_w_up=ffn2_w_up, ffn2_w_down=ffn2_w_down)
    m = dict(meta_tokens=m_meta_tokens, ffn1_norm=m_ffn1_norm, ffn1_w_gate=m_ffn1_w_gate, ffn1_w_up=m_ffn1_w_up,
             ffn1_w_down=m_ffn1_w_down, mix_norm=m_mix_norm, w_in=m_w_in, b_forget=m_b_forget, q_norm=m_q_norm,
             k_norm=m_k_norm, pool_w=m_pool_w, pool_scale=m_pool_scale, w_out=m_w_out, ffn2_norm=m_ffn2_norm,
             ffn2_w_gate=m_ffn2_w_gate, ffn2_w_up=m_ffn2_w_up, ffn2_w_down=m_ffn2_w_down)
    v = dict(meta_tokens=v_meta_tokens, ffn1_norm=v_ffn1_norm, ffn1_w_gate=v_ffn1_w_gate, ffn1_w_up=v_ffn1_w_up,
             ffn1_w_down=v_ffn1_w_down, mix_norm=v_mix_norm, w_in=v_w_in, b_forget=v_b_forget, q_norm=v_q_norm,
             k_norm=v_k_norm, pool_w=v_pool_w, pool_scale=v_pool_scale, w_out=v_w_out, ffn2_norm=v_ffn2_norm,
             ffn2_w_gate=v_ffn2_w_gate, ffn2_w_up=v_ffn2_w_up, ffn2_w_down=v_ffn2_w_down)

    d = x.shape[-1]
    n_heads = b_forget.shape[-1]
    pos = (lax.axis_index("x"), lax.axis_index("y"), lax.axis_index("c"))
    me = 4 * pos[0] + 2 * pos[1] + pos[2]

    raw = {k: _as2d(k, w[k]) for k in _BIG}
    raw["meta_tokens"] = meta_tokens
    plan = _MeshPlan(raw, pos, d, N_DEV * w_in.shape[-1], n_heads)
    sw = {k: w[k] for k in _SMALL}
    sw["b_forget"] = jnp.pad(b_forget, ((0, 0), (0, LANES - n_heads)))
    dx, dmeta, small = _local_step(x[0], loss_target[0], sw, plan)

    res = {}
    last = dx

    plan.start_small_gather([jnp.concatenate(small, axis=1), dmeta])
    plan.at("after_ffn1_rms_bwd")

    def update_shards(names):
        nonlocal last
        for k in names:
            plan.at("before_adamw_" + k, (last,))
            state = [pltpu.with_memory_space_constraint(_as2d(k, t[k]), pltpu.HBM) for t in (w, m, v)]
            res[k] = _adamw(plan.gradient_parts(k), *state, "adamw_" + k, plan.comm("adamw_" + k))
            last = res[k][0]

    update_shards(_FFN2 + _MIX + ("ffn1_w_gate", "ffn1_w_up"))

    pack_g, meta_g = plan.wait_small_gather((last,))
    tot, dmeta_tot, loss_row = _small_reduce(pack_g, meta_g, 0.5 / d, "small_reduce")

    mcols = meta_tokens.shape[1]
    g_meta = lax.dynamic_slice_in_dim(dmeta_tot, me * mcols, mcols, axis=1)
    res["meta_tokens"] = _adamw(g_meta, meta_tokens, m_meta_tokens, v_meta_tokens, "adamw_meta_tokens")

    def packed(src):
        return jnp.concatenate([src[k] for k in _SMALL[:-1]] + [jnp.pad(src["b_forget"], ((0, 0), (0, LANES - n_heads)))],
                               axis=1)

    wp = packed(w)
    sm = _adamw(tot[:, :wp.shape[1]], wp, packed(m), packed(v), "adamw_small")
    off = 0
    for k in _SMALL:
        width = w[k].shape[1]
        res[k] = tuple(o[:, off:off + width] for o in sm)
        off += width if k != "b_forget" else LANES

    last = sm[0]
    update_shards(("ffn1_w_down",))

    outs =[loss_row[0, 0], dx[None]]
    for idx in range(4):
        outs += [_from2d(k, res[k][idx], w[k].shape) for k in _ORDER]
    return tuple(outs)
```

```python
import functools

import jax
import jax.numpy as jnp
from jax import lax
from jax.experimental import pallas as pl
from jax.experimental.pallas import tpu as pltpu

F32 = jnp.float32
BF = jnp.bfloat16
SDS = jax.ShapeDtypeStruct

N_DEV = 8
LANES = 128
SUBLANES = 8
HEAD_DIM = 128
POOL_WINDOWS = (2, 4, 8, 16)
RMS_EPS = 1e-6
NEG_BIG = -1e30
MIB = 1024 * 1024

ADAM_LR = 0.001
ADAM_B1 = 0.9
ADAM_B2 = 0.999
ADAM_EPS = 1e-08
ADAM_WD = 0.01
ADAM_STEP = 10


class _Comm:
    def __init__(self, arrs, out_shape, sems, start, finish, aliases=None):
        self.arrs, self.out_shape, self.sems = list(arrs), list(out_shape), list(sems)
        self.start, self.finish, self.aliases = start, finish, dict(aliases or {})
        self.results = None


def _merge_comm(ops):
    ops = [op for op in ops if op is not None]
    if not ops:
        return None
    na, no, ns = [0], [0], [0]
    for op in ops:
        na.append(na[-1] + len(op.arrs))
        no.append(no[-1] + len(op.out_shape))
        ns.append(ns[-1] + len(op.sems))

    def parts(i, ins, outs, sems):
        return ins[na[i]:na[i + 1]], outs[no[i]:no[i + 1]], sems[ns[i]:ns[i + 1]]

    def start(ins, outs, sems):
        for i, op in enumerate(ops):
            op.start(*parts(i, ins, outs, sems))

    def finish(ins, outs, sems):
        for i, op in enumerate(ops):
            op.finish(*parts(i, ins, outs, sems))

    aliases = {}
    for i, op in enumerate(ops):
        for a, o in op.aliases.items():
            aliases[na[i] + a] = no[i] + o
    merged = _Comm([a for op in ops for a in op.arrs], [s for op in ops for s in op.out_shape],
                   [s for op in ops for s in op.sems], start, finish, aliases)
    merged.children = (ops, no)
    return merged


def _deliver(comm, results):
    comm.results = list(results)
    if hasattr(comm, "children"):
        ops, no = comm.children
        for i, op in enumerate(ops):
            _deliver(op, results[no[i]:no[i + 1]])


def _call(body, *, grid, in_specs, out_specs, out_shape, scratch_shapes=(), vmem_mib, name, comm=None):
    single = not isinstance(out_shape, (list, tuple))
    out_specs = [out_specs] if single else list(out_specs)
    out_shape = [out_shape] if single else list(out_shape)
    in_specs, scratch_shapes = list(in_specs), list(scratch_shapes)
    params = pltpu.CompilerParams(dimension_semantics=("arbitrary",) * len(grid), vmem_limit_bytes=vmem_mib * MIB)
    n_in, n_out, n_scr = len(in_specs), len(out_specs), len(scratch_shapes)

    def run(*args):
        if comm is None:
            res = pl.pallas_call(body, grid=grid, in_specs=in_specs, out_specs=out_specs, out_shape=out_shape,
                                 scratch_shapes=scratch_shapes, compiler_params=params, name=name)(*args)
            return res[0] if single else res
        ci, co = len(comm.arrs), len(comm.out_shape)

        def with_comm(*refs):
            ins, cins = refs[:n_in], refs[n_in:n_in + ci]
            o0 = n_in + ci
            outs, couts = refs[o0:o0 + n_out], refs[o0 + n_out:o0 + n_out + co]
            s0 = o0 + n_out + co
            scr, csems = refs[s0:s0 + n_scr], refs[s0 + n_scr:]
            ids = [pl.program_id(a) for a in range(len(grid))]
            first = functools.reduce(jnp.logical_and, [i == 0 for i in ids])
            last = functools.reduce(jnp.logical_and, [i == g - 1 for i, g in zip(ids, grid)])

            @pl.when(first)
            def _():
                comm.start(cins, couts, csems)

            body(*ins, *outs, *scr)

            @pl.when(last)
            def _():
                comm.finish(cins, couts, csems)

        anyspec = pl.BlockSpec(memory_space=pl.ANY)
        res = pl.pallas_call(
            with_comm, grid=grid, in_specs=in_specs + [anyspec] * ci, out_specs=out_specs + [anyspec] * co,
            out_shape=out_shape + comm.out_shape, scratch_shapes=scratch_shapes + comm.sems,
            input_output_aliases={n_in + a: n_out + o for a, o in comm.aliases.items()},
            compiler_params=params, name=name)(*args, *comm.arrs)
        _deliver(comm, res[n_out:])
        return res[0] if single else res[:n_out]

    return run


def _comm_alone(comm, name):
    def body(*refs):
        ci, co = len(comm.arrs), len(comm.out_shape)
        ins, outs, sems = refs[:ci], refs[ci:ci + co], refs[ci + co:]
        comm.start(ins, outs, sems)
        comm.finish(ins, outs, sems)

    anyspec = pl.BlockSpec(memory_space=pl.ANY)
    res = pl.pallas_call(
        body, in_specs=[anyspec] * len(comm.arrs), out_specs=[anyspec] * len(comm.out_shape),
        out_shape=comm.out_shape, scratch_shapes=comm.sems, input_output_aliases=comm.aliases, name=name)(*comm.arrs)
    _deliver(comm, res)


def _split_start(comm, name):
    na, ns = len(comm.arrs), len(comm.sems)

    def body(*refs):
        comm.start(refs[:na], None, refs[na:na + ns])
        token = refs[-1]
        token[...] = jnp.zeros_like(token)

    hbm = pl.BlockSpec(memory_space=pltpu.HBM)
    res = pl.pallas_call(
        body, name=name,
        out_shape=tuple(comm.sems) + tuple(pltpu.HBM(a.shape, a.dtype) for a in comm.arrs)
        + (SDS((SUBLANES, LANES), F32),),
        in_specs=[hbm] * na,
        out_specs=[pl.BlockSpec(memory_space=pltpu.SEMAPHORE)] * ns + [hbm] * na + [pl.BlockSpec(memory_space=pltpu.VMEM)],
        input_output_aliases={i: ns + i for i in range(na)},
        compiler_params=pltpu.CompilerParams(has_side_effects=pltpu.SideEffectType.DATAFLOW_SIDE_EFFECTING),
    )(*[pltpu.with_memory_space_constraint(a, pltpu.HBM) for a in comm.arrs])
    return (comm, res[:ns], res[ns:ns + na]), res[-1]


def _split_wait(started, afters, name):
    comm, sems, thru = started
    na, ns = len(thru), len(sems)
    afters = list(afters)

    def body(*refs):
        comm.finish(refs[:na], None, refs[na:na + ns])

    hbm = pl.BlockSpec(memory_space=pltpu.HBM)
    res = pl.pallas_call(
        body, name=name, out_shape=tuple(pltpu.HBM(a.shape, a.dtype) for a in thru),
        in_specs=[hbm] * na + [pl.BlockSpec(memory_space=pltpu.SEMAPHORE)] * ns
        + [pl.BlockSpec(memory_space=pl.ANY)] * len(afters),
        out_specs=[hbm] * na, input_output_aliases={i: i for i in range(na)},
        compiler_params=pltpu.CompilerParams(has_side_effects=pltpu.SideEffectType.DATAFLOW_SIDE_EFFECTING),
    )(*thru, *sems, *afters)
    return res[na - len(comm.out_shape):]


def _largest_tile(n, cap, mult):
    if n <= cap:
        return n
    best = None
    for t in range(mult, cap + 1, mult):
        if n % t == 0:
            best = t
    assert best is not None, (n, cap, mult)
    return best


def _dot(a, b):
    return jnp.dot(a, b, preferred_element_type=F32)


def _dot_nt(a, b):
    return lax.dot_general(a, b, (((1,), (1,)), ((), ())), preferred_element_type=F32)


def _dot_tn(a, b):
    return lax.dot_general(a, b, (((0,), (0,)), ((), ())), preferred_element_type=F32)


def _rows8(x):
    t, c = x.shape
    return jnp.sum(x.reshape(t // SUBLANES, SUBLANES, c), axis=0)


def _rstd(x):
    return lax.rsqrt(jnp.mean(x * x, axis=-1, keepdims=True) + RMS_EPS)


def _ffn_fwd(h, g, wg, wu, wd, tm, name, comm=None):
    lp, d = h.shape
    ns, fs, _ = wg.shape

    def body(h_ref, g_ref, wg_ref, wu_ref, wd_ref, out_ref, a_ref, b_ref, u_ref, acc_ref):
        j = pl.program_id(1)

        @pl.when(j == 0)
        def _():
            hh = h_ref[...]
            u_ref[...] = (hh * _rstd(hh) * g_ref[...]).astype(BF)
            acc_ref[...] = jnp.zeros_like(acc_ref)

        u = u_ref[...]
        a = _dot_nt(u, wg_ref[...])
        b = _dot_nt(u, wu_ref[...])
        a_ref[...] = a.astype(BF)
        b_ref[...] = b.astype(BF)
        hid = (a * jax.nn.sigmoid(a) * b).astype(BF)
        acc_ref[...] += _dot(hid, wd_ref[...])

        @pl.when(j == ns - 1)
        def _():
            out_ref[...] = h_ref[...] + 0.5 * acc_ref[...]

    row = pl.BlockSpec((tm, d), lambda i, j: (i, 0))
    act = pl.BlockSpec((None, tm, fs), lambda i, j: (j, i, 0))
    return _call(
        body, grid=(lp // tm, ns),
        in_specs=[row, pl.BlockSpec((1, d), lambda i, j: (0, 0)),
                  pl.BlockSpec((None, fs, d), lambda i, j: (j, 0, 0)),
                  pl.BlockSpec((None, fs, d), lambda i, j: (j, 0, 0)),
                  pl.BlockSpec((None, fs, d), lambda i, j: (j, 0, 0))],
        out_specs=[row, act, act, row],
        out_shape=[SDS((lp, d), F32), SDS((ns, lp, fs), BF), SDS((ns, lp, fs), BF), SDS((lp, d), BF)],
        scratch_shapes=[pltpu.VMEM((tm, d), F32)],
        vmem_mib=56, name=name, comm=comm)(h, g, wg, wu, wd)


def _ffn_fwd_part(h, g, wg, wu, wd, order, carry, tm, name, deps=()):
    lp, d = h.shape
    fs = wg.shape[1]
    k = order.shape[0]
    first = carry is None
    n_in = 5 if first else 8

    def body(order_ref, *refs):
        outs = refs[n_in + len(deps):]
        if first:
            h_ref, g_ref, wg_ref, wu_ref, wd_ref = refs[:n_in]
            out_ref, a_ref, b_ref, u_ref, acc_ref = outs
        else:
            h_ref, acc_in_ref, u_ref, _, _, wg_ref, wu_ref, wd_ref = refs[:n_in]
            out_ref, a_ref, b_ref, acc_ref = outs
        j = pl.program_id(1)

        @pl.when(j == 0)
        def _():
            if first:
                hh = h_ref[...]
                u_ref[...] = (hh * _rstd(hh) * g_ref[...]).astype(BF)
                acc_ref[...] = jnp.zeros_like(acc_ref)
            else:
                acc_ref[...] = acc_in_ref[...]

        u = u_ref[...]
        a = _dot_nt(u, wg_ref[...])
        b = _dot_nt(u, wu_ref[...])
        a_ref[...] = a.astype(BF)
        b_ref[...] = b.astype(BF)
        hid = (a * jax.nn.sigmoid(a) * b).astype(BF)
        acc_ref[...] += _dot(hid, wd_ref[...])

        @pl.when(j == k - 1)
        def _():
            out_ref[...] = acc_ref[...] if first else h_ref[...] + 0.5 * acc_ref[...]

    row = pl.BlockSpec((tm, d), lambda i, j, o: (i, 0))
    act = pl.BlockSpec((None, tm, fs), lambda i, j, o: (o[j], i, 0))
    wsp = pl.BlockSpec((None, fs, d), lambda i, j, o: (o[j], 0, 0))
    anyspec = pl.BlockSpec(memory_space=pl.ANY)
    acts = [SDS((wg.shape[0], lp, fs), BF)] * 2
    if first:
        in_specs = [row, pl.BlockSpec((1, d), lambda i, j, o: (0, 0)), wsp, wsp, wsp]
        out_specs, out_shape = [row, act, act, row], [SDS((lp, d), F32)] + acts + [SDS((lp, d), BF)]
        args, aliases = (h, g, wg, wu, wd), {}
    else:
        acc, a_prev, b_prev, u_prev = carry
        in_specs = [row, row, row, anyspec, anyspec, wsp, wsp, wsp]
        out_specs, out_shape = [row, act, act], [SDS((lp, d), F32)] + acts
        args, aliases = (h, acc, u_prev, a_prev, b_prev, wg, wu, wd), {4: 1, 5: 2}
    return pl.pallas_call(
        body,
        grid_spec=pltpu.PrefetchScalarGridSpec(
            num_scalar_prefetch=1, grid=(lp // tm, k), in_specs=in_specs + [anyspec] * len(deps),
            out_specs=out_specs, scratch_shapes=[pltpu.VMEM((tm, d), F32)]),
        out_shape=out_shape, input_output_aliases=aliases,
        compiler_params=pltpu.CompilerParams(dimension_semantics=("arbitrary",) * 2, vmem_limit_bytes=60 * MIB),
        name=name)(order, *args, *deps)


def _ffn_bwd_dx(dob, a, b, wg, wu, wd, tm, name, comm=None):
    lp, d = dob.shape
    ns, fs, _ = wg.shape
    row = pl.BlockSpec((tm, d), lambda i, j: (i, 0))
    act = pl.BlockSpec((None, tm, fs), lambda i, j: (j, i, 0))
    wsp = pl.BlockSpec((None, fs, d), lambda i, j: (j, 0, 0))

    def act_body(do_ref, a_ref, b_ref, wd_ref, da_ref, db_ref, hid_ref):
        dhid = _dot_nt(do_ref[...], wd_ref[...])
        av = a_ref[...].astype(F32)
        bv = b_ref[...].astype(F32)
        sig = jax.nn.sigmoid(av)
        sil = av * sig
        hid_ref[...] = (sil * bv).astype(BF)
        da_ref[...] = (dhid * bv * (sig * (1.0 + av * (1.0 - sig)))).astype(BF)
        db_ref[...] = (dhid * sil).astype(BF)

    da, db, hid = _call(
        act_body, grid=(lp // tm, ns), in_specs=[row, act, act, wsp], out_specs=[act, act, act],
        out_shape=[SDS((ns, lp, fs), BF)] * 3, vmem_mib=40, name=name + "_act", comm=comm)(dob, a, b, wd)

    def du_body(da_ref, db_ref, wg_ref, wu_ref, du_ref):
        @pl.when(pl.program_id(1) == 0)
        def _():
            du_ref[...] = jnp.zeros_like(du_ref)

        du_ref[...] += _dot(da_ref[...], wg_ref[...]) + _dot(db_ref[...], wu_ref[...])

    du = _call(
        du_body, grid=(lp // tm, ns), in_specs=[act, act, wsp, wsp], out_specs=row,
        out_shape=SDS((lp, d), F32), vmem_mib=48, name=name + "_du")(da, db, wg, wu)
    return du, da, db, hid


def _rms_bwd(du, h, g, dres, bscale, tm, name, comm=None):
    lp, d = h.shape

    def body(du_ref, h_ref, g_ref, dres_ref, dh_ref, dhb_ref, dg_ref):
        @pl.when(pl.program_id(0) == 0)
        def _():
            dg_ref[...] = jnp.zeros_like(dg_ref)

        hh = h_ref[...]
        r = _rstd(hh)
        xhat = hh * r
        duv = du_ref[...]
        dg_ref[...] += _rows8(duv * xhat)
        dxh = duv * g_ref[...]
        dh = dres_ref[...] + r * (dxh - xhat * jnp.mean(dxh * xhat, axis=-1, keepdims=True))
        dh_ref[...] = dh
        dhb_ref[...] = (bscale * dh).astype(BF)

    row = pl.BlockSpec((tm, d), lambda i: (i, 0))
    return _call(
        body, grid=(lp // tm,),
        in_specs=[row, row, pl.BlockSpec((1, d), lambda i: (0, 0)), row],
        out_specs=[row, row, pl.BlockSpec((SUBLANES, d), lambda i: (0, 0))],
        out_shape=[SDS((lp, d), F32), SDS((lp, d), BF), SDS((SUBLANES, d), F32)],
        vmem_mib=48, name=name, comm=comm)(du, h, g, dres)


def _matmul_tn(a, b, tm, tn, name, comm=None):
    a_b, b_b = a.ndim == 3, b.ndim == 3
    ns = a.shape[0] if a_b else (b.shape[0] if b_b else 1)
    l, m = a.shape[-2:]
    n = b.shape[-1]

    def body(a_ref, b_ref, o_ref):
        o_ref[...] = _dot_tn(a_ref[...], b_ref[...]).astype(o_ref.dtype)

    a_spec = (pl.BlockSpec((None, l, tm), lambda s, i, j: (s, 0, i)) if a_b
              else pl.BlockSpec((l, tm), lambda s, i, j: (0, i)))
    b_spec = (pl.BlockSpec((None, l, tn), lambda s, i, j: (s, 0, j)) if b_b
              else pl.BlockSpec((l, tn), lambda s, i, j: (0, j)))
    batched = a_b or b_b
    o_spec = (pl.BlockSpec((None, tm, tn), lambda s, i, j: (s, i, j)) if batched
              else pl.BlockSpec((tm, tn), lambda s, i, j: (i, j)))
    o_shape = SDS((ns, m, n), BF) if batched else SDS((m, n), BF)
    return _call(
        body, grid=(ns, m // tm, n // tn), in_specs=[a_spec, b_spec], out_specs=o_spec, out_shape=o_shape,
        vmem_mib=48, name=name, comm=comm)(a, b)


def _matmul_nt(x, w, tm, tk, out_dtype, name, comm=None):
    l, k = x.shape
    n = w.shape[0]
    nk = k // tk

    def body(x_ref, w_ref, o_ref, acc_ref):
        kk = pl.program_id(1)

        @pl.when(kk == 0)
        def _():
            acc_ref[...] = jnp.zeros_like(acc_ref)

        acc_ref[...] += _dot_nt(x_ref[...], w_ref[...])

        @pl.when(kk == nk - 1)
        def _():
            o_ref[...] = acc_ref[...].astype(o_ref.dtype)

    return _call(
        body, grid=(l // tm, nk),
        in_specs=[pl.BlockSpec((tm, tk), lambda i, kk: (i, kk)), pl.BlockSpec((n, tk), lambda i, kk: (0, kk))],
        out_specs=pl.BlockSpec((tm, n), lambda i, kk: (i, 0)),
        out_shape=SDS((l, n), out_dtype),
        scratch_shapes=[pltpu.VMEM((tm, n), F32)],
        vmem_mib=48, name=name, comm=comm)(x, w)


def _norm_matmul(h, g, w, tm, tn, name, comm=None):
    lp, d = h.shape
    n = w.shape[1]

    def body(h_ref, g_ref, w_ref, z_ref, u_ref):
        @pl.when(pl.program_id(1) == 0)
        def _():
            hh = h_ref[...]
            u_ref[...] = (hh * _rstd(hh) * g_ref[...]).astype(BF)

        z_ref[...] = _dot(u_ref[...], w_ref[...])

    row = pl.BlockSpec((tm, d), lambda i, j: (i, 0))
    return _call(
        body, grid=(lp // tm, n // tn),
        in_specs=[row, pl.BlockSpec((1, d), lambda i, j: (0, 0)), pl.BlockSpec((d, tn), lambda i, j: (0, j))],
        out_specs=[pl.BlockSpec((tm, tn), lambda i, j: (i, j)), row],
        out_shape=[SDS((lp, n), F32), SDS((lp, d), BF)],
        vmem_mib=48, name=name, comm=comm)(h, g, w)


def _out_proj(h, pool_o, att_o, w_out, tm, name, comm=None):
    lp, d = h.shape
    p = pool_o.shape[1]
    dm = w_out.shape[0]

    def body(h_ref, p_ref, a_ref, w_ref, o_ref):
        o_ref[...] = h_ref[...] + _dot(p_ref[...], w_ref[0:p, :]) + _dot(a_ref[...], w_ref[p:dm, :])

    row = pl.BlockSpec((tm, d), lambda i: (i, 0))
    return _call(
        body, grid=(lp // tm,),
        in_specs=[row, pl.BlockSpec((tm, p), lambda i: (i, 0)), pl.BlockSpec((tm, dm - p), lambda i: (i, 0)),
                  pl.BlockSpec((dm, d), lambda i: (0, 0))],
        out_specs=row, out_shape=SDS((lp, d), F32),
        vmem_mib=48, name=name, comm=comm)(h, pool_o, att_o, w_out)


def _loss_head(y, tpad, row0, row1, tm, name, comm=None):
    lp, d = y.shape

    def body(y_ref, t_ref, dy_ref, dob_ref, ls_ref):
        i = pl.program_id(0)

        @pl.when(i == 0)
        def _():
            ls_ref[...] = jnp.zeros_like(ls_ref)

        rows = i * tm + lax.broadcasted_iota(jnp.int32, (tm, d), 0)
        err = jnp.where((rows >= row0) & (rows < row1), y_ref[...] - t_ref[...], 0.0)
        dy = err * (1.0 / d)
        dy_ref[...] = dy
        dob_ref[...] = (0.5 * dy).astype(BF)
        sq = _rows8(err * err)
        acc = sq[:, 0:LANES]
        for c in range(1, d // LANES):
            acc = acc + sq[:, c * LANES:(c + 1) * LANES]
        ls_ref[...] += acc

    row = pl.BlockSpec((tm, d), lambda i: (i, 0))
    return _call(
        body, grid=(lp // tm,), in_specs=[row, row],
        out_specs=[row, row, pl.BlockSpec((SUBLANES, LANES), lambda i: (0, 0))],
        out_shape=[SDS((lp, d), F32), SDS((lp, d), BF), SDS((SUBLANES, LANES), F32)],
        vmem_mib=48, name=name, comm=comm)(y, tpad)


def _window_select(levels, gidx):
    out = levels[-1]
    for k in range(len(levels) - 2, -1, -1):
        out = jnp.where(gidx == k, levels[k], out)
    return out


def _pool_window_mean_minus_id(x, gidx):
    rows = lax.broadcasted_iota(jnp.int32, x.shape, 0)
    levels = []
    s = x
    shift = 1
    while shift < POOL_WINDOWS[-1]:
        s = s + jnp.where(rows >= shift, pltpu.roll(s, shift, 0), 0.0)
        shift *= 2
        if shift in POOL_WINDOWS:
            levels.append(s)
    win = _window_select(levels, gidx)
    cnt = jnp.minimum(rows + 1, _window_select(list(POOL_WINDOWS), gidx)).astype(F32)
    return win / cnt - x, cnt


def _pool_window_transpose(dy, cnt, gidx):
    lp = dy.shape[0]
    rows = lax.broadcasted_iota(jnp.int32, dy.shape, 0)
    levels = []
    s = dy / cnt
    shift = 1
    while shift < POOL_WINDOWS[-1]:
        s = s + jnp.where(rows < lp - shift, pltpu.roll(s, lp - shift, 0), 0.0)
        shift *= 2
        if shift in POOL_WINDOWS:
            levels.append(s)
    return _window_select(levels, gidx) - dy


def _pool_fwd(z, pool_w, pool_scale, name, comm=None):
    lp = z.shape[0]
    ng, gw, _ = pool_w.shape

    def body(p_ref, w_ref, s_ref, o_ref):
        pooled, _ = _pool_window_mean_minus_id(p_ref[...], pl.program_id(0))
        o_ref[...] = (_dot(pooled.astype(BF), w_ref[...]) * s_ref[...]).astype(BF)

    return _call(
        body, grid=(ng,),
        in_specs=[pl.BlockSpec((lp, gw), lambda g: (0, g)), pl.BlockSpec((None, gw, gw), lambda g: (g, 0, 0)),
                  pl.BlockSpec((1, gw), lambda g: (0, g))],
        out_specs=pl.BlockSpec((lp, gw), lambda g: (0, g)), out_shape=SDS((lp, ng * gw), BF),
        vmem_mib=48, name=name, comm=comm)(z, pool_w, pool_scale)


def _pool_bwd(z, dmix, pool_w, pool_scale, name, comm=None):
    lp = z.shape[0]
    ng, gw, _ = pool_w.shape

    def body(p_ref, d_ref, w_ref, s_ref, dz_ref, dw_ref, ds_ref):
        g = pl.program_id(0)
        pooled, cnt = _pool_window_mean_minus_id(p_ref[...], g)
        pooled_b = pooled.astype(BF)
        w = w_ref[...]
        mixed = _dot(pooled_b, w)
        dpo = d_ref[...].astype(F32)
        ds_ref[...] = _rows8(dpo * mixed)
        dmixed = (dpo * s_ref[...]).astype(BF)
        dw_ref[...] = _dot_tn(pooled_b, dmixed)
        dpooled = _dot_nt(dmixed, w)
        dz_ref[...] = _pool_window_transpose(dpooled, cnt, g).astype(BF)

    return _call(
        body, grid=(ng,),
        in_specs=[pl.BlockSpec((lp, gw), lambda g: (0, g)), pl.BlockSpec((lp, gw), lambda g: (0, g)),
                  pl.BlockSpec((None, gw, gw), lambda g: (g, 0, 0)), pl.BlockSpec((1, gw), lambda g: (0, g))],
        out_specs=[pl.BlockSpec((lp, gw), lambda g: (0, g)), pl.BlockSpec((None, gw, gw), lambda g: (g, 0, 0)),
                   pl.BlockSpec((SUBLANES, gw), lambda g: (0, g))],
        out_shape=[SDS((lp, ng * gw), BF), SDS((ng, gw, gw), F32), SDS((SUBLANES, ng * gw), F32)],
        vmem_mib=48, name=name, comm=comm)(z, dmix, pool_w, pool_scale)


def _log_sigmoid(x):
    return jnp.minimum(x, 0.0) - jnp.log(1.0 + jnp.exp(-jnp.abs(x)))


def _fox_prep(z, bfp, fblk, name, comm=None):
    lp = z.shape[0]
    nb = lp // LANES

    def body(f_ref, b_ref, cum_ref):
        r = lax.broadcasted_iota(jnp.int32, (LANES, LANES), 0)
        c = lax.broadcasted_iota(jnp.int32, (LANES, LANES), 1)
        tri = (r >= c).astype(F32)
        carry = jnp.zeros((1, LANES), F32)
        for blk in range(nb):
            sl = slice(blk * LANES, (blk + 1) * LANES)
            lf = _log_sigmoid(f_ref[sl, :] + b_ref[...])
            cb = jnp.dot(tri, lf, preferred_element_type=F32, precision=lax.Precision.HIGHEST) + carry
            cum_ref[sl, :] = cb
            carry = cb[LANES - 1:LANES, :]

    return _call(
        body, grid=(1,),
        in_specs=[pl.BlockSpec((lp, LANES), lambda i: (0, fblk)), pl.BlockSpec((1, LANES), lambda i: (0, 0))],
        out_specs=pl.BlockSpec((lp, LANES), lambda i: (0, 0)), out_shape=SDS((lp, LANES), F32),
        vmem_mib=32, name=name, comm=comm)(z, bfp)


def _fox_bwd(z, bfp, dcum, fblk, name, comm=None):
    lp = z.shape[0]
    nb = lp // LANES

    def body(f_ref, b_ref, dc_ref, dz_ref, db_ref):
        r = lax.broadcasted_iota(jnp.int32, (LANES, LANES), 0)
        c = lax.broadcasted_iota(jnp.int32, (LANES, LANES), 1)
        tri = (r <= c).astype(F32)
        carry = jnp.zeros((1, LANES), F32)
        acc = jnp.zeros((SUBLANES, LANES), F32)
        for blk in range(nb - 1, -1, -1):
            sl = slice(blk * LANES, (blk + 1) * LANES)
            dlf = jnp.dot(tri, dc_ref[sl, :], preferred_element_type=F32, precision=lax.Precision.HIGHEST) + carry
            carry = dlf[0:1, :]
            df = dlf * jax.nn.sigmoid(-(f_ref[sl, :] + b_ref[...]))
            dz_ref[sl, :] = df.astype(BF)
            acc = acc + _rows8(df)
        db_ref[...] = acc

    return _call(
        body, grid=(1,),
        in_specs=[pl.BlockSpec((lp, LANES), lambda i: (0, fblk)), pl.BlockSpec((1, LANES), lambda i: (0, 0)),
                  pl.BlockSpec((lp, LANES), lambda i: (0, 0))],
        out_specs=[pl.BlockSpec((lp, LANES), lambda i: (0, 0)), pl.BlockSpec((SUBLANES, LANES), lambda i: (0, 0))],
        out_shape=[SDS((lp, LANES), BF), SDS((SUBLANES, LANES), F32)],
        vmem_mib=32, name=name, comm=comm)(z, bfp, dcum)


def _att_scores(q_ref, cum_ref, cumt_ref, qw_ref, kn_s, h, i, tq, lk):
    scale = 1.0 / (HEAD_DIM ** 0.5)
    q = q_ref[...]
    rq = _rstd(q)
    qhat = q * rq
    qn = (qhat * qw_ref[...]).astype(BF)
    s = _dot_nt(qn, kn_s[0:lk, :]) * scale
    lane = lax.broadcasted_iota(jnp.int32, (tq, LANES), 1)
    cq = jnp.sum(jnp.where(lane == h, cum_ref[...], 0.0), axis=1, keepdims=True)
    ck = cumt_ref[pl.ds(h, 1), 0:lk]
    s = s + (cq - ck)
    qpos = i * tq + lax.broadcasted_iota(jnp.int32, (tq, lk), 0)
    kpos = lax.broadcasted_iota(jnp.int32, (tq, lk), 1)
    s = jnp.where(qpos >= kpos, s, NEG_BIG)
    e = jnp.exp(s - jnp.max(s, axis=1, keepdims=True))
    p = e * (1.0 / jnp.sum(e, axis=1, keepdims=True))
    return p, qn, qhat, rq


def _per_query_tile(i, nq, tq, lp, fn):
    for t in range(nq):
        lk = min(lp, -(-((t + 1) * tq) // LANES) * LANES)
        pl.when(i == t)(functools.partial(fn, lk))


def _att_fwd(z, cum, cumt, qw, kw, n_heads, qblk0, tq, name, comm=None):
    lp = z.shape[0]
    nh = n_heads

    def body(q_ref, k_ref, v_ref, cum_ref, cumt_ref, qw_ref, kw_ref, o_ref, kn_s, vb_s):
        h, i = pl.program_id(0), pl.program_id(1)

        @pl.when(i == 0)
        def _():
            k = k_ref[...]
            kn_s[...] = (k * _rstd(k) * kw_ref[...]).astype(BF)
            vb_s[...] = v_ref[...].astype(BF)

        def tile(lk):
            p, _, _, _ = _att_scores(q_ref, cum_ref, cumt_ref, qw_ref, kn_s, h, i, tq, lk)
            o_ref[...] = _dot(p.astype(BF), vb_s[0:lk, :]).astype(BF)

        _per_query_tile(i, lp // tq, tq, lp, tile)

    vec = pl.BlockSpec((1, HEAD_DIM), lambda h, i: (0, 0))
    return _call(
        body, grid=(nh, lp // tq),
        in_specs=[pl.BlockSpec((tq, HEAD_DIM), lambda h, i: (i, qblk0 + h)),
                  pl.BlockSpec((lp, HEAD_DIM), lambda h, i: (0, qblk0 + nh + h)),
                  pl.BlockSpec((lp, HEAD_DIM), lambda h, i: (0, qblk0 + 2 * nh + h)),
                  pl.BlockSpec((tq, LANES), lambda h, i: (i, 0)),
                  pl.BlockSpec((nh, lp), lambda h, i: (0, 0)), vec, vec],
        out_specs=pl.BlockSpec((tq, HEAD_DIM), lambda h, i: (i, h)),
        out_shape=SDS((lp, nh * HEAD_DIM), BF),
        scratch_shapes=[pltpu.VMEM((lp, HEAD_DIM), BF), pltpu.VMEM((lp, HEAD_DIM), BF)],
        vmem_mib=48, name=name, comm=comm)(z, z, z, cum, cumt, qw, kw)


def _att_bwd(z, cum, cumt, qw, kw, dmix, n_heads, qblk0, oblk0, tq, name, comm=None):
    lp = z.shape[0]
    nh = n_heads
    nq = lp // tq
    scale = 1.0 / (HEAD_DIM ** 0.5)

    def body(q_ref, k_ref, v_ref, cum_ref, cumt_ref, qw_ref, kw_ref, do_ref,
             dq_ref, dk_ref, dv_ref, dck_ref, dqw_ref, dkw_ref,
             kn_s, vb_s, dkn_s, dv_s, dck_s):
        h, i = pl.program_id(0), pl.program_id(1)

        @pl.when((h == 0) & (i == 0))
        def _():
            dqw_ref[...] = jnp.zeros_like(dqw_ref)
            dkw_ref[...] = jnp.zeros_like(dkw_ref)

        @pl.when(i == 0)
        def _():
            k = k_ref[...]
            kn_s[...] = (k * _rstd(k) * kw_ref[...]).astype(BF)
            vb_s[...] = v_ref[...].astype(BF)
            dkn_s[...] = jnp.zeros_like(dkn_s)
            dv_s[...] = jnp.zeros_like(dv_s)
            dck_s[...] = jnp.zeros_like(dck_s)

        def tile(lk):
            p, qn, qhat, rq = _att_scores(q_ref, cum_ref, cumt_ref, qw_ref, kn_s, h, i, tq, lk)
            dob = do_ref[...]
            dp = _dot_nt(dob, vb_s[0:lk, :])
            ds = p * (dp - jnp.sum(p * dp, axis=1, keepdims=True))
            dsb = ds.astype(BF)
            dv_s[0:lk, :] += _dot_tn(p.astype(BF), dob)
            dkn_s[0:lk, :] += _dot_tn(dsb, qn)
            dck_s[:, 0:lk] += jnp.sum(ds, axis=0, keepdims=True)
            dqn = _dot(dsb, kn_s[0:lk, :]) * scale
            gq = dqn * qw_ref[...]
            dq_ref[...] = (rq * (gq - qhat * jnp.mean(gq * qhat, axis=-1, keepdims=True))).astype(BF)
            dqw_ref[...] += _rows8(dqn * qhat)

        _per_query_tile(i, nq, tq, lp, tile)

        @pl.when(i == nq - 1)
        def _():
            k = k_ref[...]
            rk = _rstd(k)
            khat = k * rk
            dkn = dkn_s[...] * scale
            gk = dkn * kw_ref[...]
            dk_ref[...] = (rk * (gk - khat * jnp.mean(gk * khat, axis=-1, keepdims=True))).astype(BF)
            dkw_ref[...] += _rows8(dkn * khat)
            dv_ref[...] = dv_s[...].astype(BF)
            dck_ref[...] = dck_s[...]

    vec = pl.BlockSpec((1, HEAD_DIM), lambda h, i: (0, 0))
    part = pl.BlockSpec((SUBLANES, LANES), lambda h, i: (0, 0))
    return _call(
        body, grid=(nh, nq),
        in_specs=[pl.BlockSpec((tq, HEAD_DIM), lambda h, i: (i, qblk0 + h)),
                  pl.BlockSpec((lp, HEAD_DIM), lambda h, i: (0, qblk0 + nh + h)),
                  pl.BlockSpec((lp, HEAD_DIM), lambda h, i: (0, qblk0 + 2 * nh + h)),
                  pl.BlockSpec((tq, LANES), lambda h, i: (i, 0)),
                  pl.BlockSpec((nh, lp), lambda h, i: (0, 0)), vec, vec,
                  pl.BlockSpec((tq, HEAD_DIM), lambda h, i: (i, oblk0 + h))],
        out_specs=[pl.BlockSpec((tq, HEAD_DIM), lambda h, i: (i, h)),
                   pl.BlockSpec((lp, HEAD_DIM), lambda h, i: (0, h)),
                   pl.BlockSpec((lp, HEAD_DIM), lambda h, i: (0, h)),
                   pl.BlockSpec((None, 1, lp), lambda h, i: (h, 0, 0)),
                   part, part],
        out_shape=[SDS((lp, nh * HEAD_DIM), BF)] * 3
        + [SDS((nh, 1, lp), F32), SDS((SUBLANES, LANES), F32), SDS((SUBLANES, LANES), F32)],
        scratch_shapes=[pltpu.VMEM((lp, HEAD_DIM), BF), pltpu.VMEM((lp, HEAD_DIM), BF),
                        pltpu.VMEM((lp, HEAD_DIM), F32), pltpu.VMEM((lp, HEAD_DIM), F32),
                        pltpu.VMEM((1, lp), F32)],
        vmem_mib=56, name=name, comm=comm)(z, z, z, cum, cumt, qw, kw, dmix)


def _adamw_math(w, g, m, v):
    m2 = ADAM_B1 * m + (1.0 - ADAM_B1) * g
    v2 = ADAM_B2 * v + (1.0 - ADAM_B2) * (g * g)
    m_hat = m2 / (1.0 - ADAM_B1 ** ADAM_STEP)
    v_hat = v2 / (1.0 - ADAM_B2 ** ADAM_STEP)
    delta = -ADAM_LR * (m_hat / (jnp.sqrt(v_hat) + ADAM_EPS) + ADAM_WD * w)
    return delta, m2, v2


def _adamw(g_in, w, m, v, name, comm=None):
    r, c = w.shape
    partial_sum = g_in.ndim == 3
    lane_padded = -(-c // LANES) * LANES
    tr = _largest_tile(r, max(16, MIB // (4 * lane_padded) // 16 * 16), 16)

    def body(g_ref, w_ref, m_ref, v_ref, go_ref, d_ref, mo_ref, vo_ref):
        if partial_sum:
            g = g_ref[0].astype(F32)
            for k in range(1, g_in.shape[0]):
                g = g + g_ref[k].astype(F32)
        else:
            g = g_ref[...]
        delta, m2, v2 = _adamw_math(w_ref[...], g, m_ref[...], v_ref[...])
        go_ref[...] = g
        d_ref[...] = delta
        mo_ref[...] = m2
        vo_ref[...] = v2

    blk = pl.BlockSpec((tr, c), lambda i: (i, 0))
    g_spec = pl.BlockSpec((g_in.shape[0], tr, c), lambda i: (0, i, 0)) if partial_sum else blk
    return _call(
        body, grid=(r // tr,), in_specs=[g_spec, blk, blk, blk], out_specs=[blk] * 4,
        out_shape=[SDS((r, c), F32)] * 4, vmem_mib=40, name=name, comm=comm)(g_in, w, m, v)


def _peer(x, y, c, k):
    return (1 - x if k & 4 else x, 1 - y if k & 2 else y, 1 - c if k & 1 else c)


_SIBLING = 1
_ICI_RELS = (2, 4, 6)


def _mesh_pos():
    return lax.axis_index("x"), lax.axis_index("y"), lax.axis_index("c")


def _sem_pair(sems, t, j, n_rel, scalars):
    if scalars:
        i = 2 * (t * n_rel + j)
        return sems[i], sems[i + 1]
    return sems[0].at[t, j], sems[1].at[t, j]


def _dev(pos):
    return 4 * pos[0] + 2 * pos[1] + pos[2]


def _gather_ici(shards, landing=None, rels=(_SIBLING,) + _ICI_RELS):
    n = len(shards)

    def remote(ins, outs, sems, arrival):
        x, y, c = _mesh_pos()
        dst = ins[n:] if landing is not None else outs
        cps = []
        for j, k in enumerate(rels):
            peer = _peer(x, y, c, k)
            slot = _dev(peer) if arrival else _dev((x, y, c))
            for t in range(n):
                send_sem, recv_sem = _sem_pair(sems, t, j, len(rels), landing is not None)
                cps.append(pltpu.make_async_remote_copy(
                    src_ref=ins[t], dst_ref=dst[t].at[slot], send_sem=send_sem, recv_sem=recv_sem,
                    device_id=peer, device_id_type=pl.DeviceIdType.MESH))
        return cps

    if landing is not None:
        def start_remote(ins, outs, sems):
            for cp in remote(ins, outs, sems, False):
                cp.start()

        def finish_remote(ins, outs, sems):
            for cp in remote(ins, outs, sems, True):
                cp.wait_recv()
            for cp in remote(ins, outs, sems, False):
                cp.wait_send()

        return _Comm(list(shards) + list(landing), [SDS(a.shape, a.dtype) for a in landing],
                     [pltpu.SemaphoreType.DMA(())] * (2 * n * len(rels)),
                     start_remote, finish_remote, aliases={n + t: t for t in range(n)})

    def local(ins, outs, sems):
        me = _dev(_mesh_pos())
        return [pltpu.make_async_copy(ins[t], outs[t].at[me], sems[2].at[t]) for t in range(n)]

    def start(ins, outs, sems):
        for cp in local(ins, outs, sems) + remote(ins, outs, sems, False):
            cp.start()

    def finish(ins, outs, sems):
        for cp in local(ins, outs, sems):
            cp.wait()
        for cp in remote(ins, outs, sems, True):
            cp.wait_recv()
        for cp in remote(ins, outs, sems, False):
            cp.wait_send()

    return _Comm(shards, [SDS((N_DEV,) + s.shape, s.dtype) for s in shards],
                 [pltpu.SemaphoreType.DMA((n, len(rels))), pltpu.SemaphoreType.DMA((n, len(rels))),
                  pltpu.SemaphoreType.DMA((n,))], start, finish)


def _gather_diagonal(zones):
    n = len(zones)

    def copies(ins, outs, sems, arrival):
        x, y, c = _mesh_pos()
        y_nb, x_nb, diag = _peer(x, y, c, 2), _peer(x, y, c, 4), _peer(x, y, c, 6)
        cps = []
        for j, (to, origin) in enumerate(((y_nb, x_nb), (x_nb, y_nb))):
            slot = _dev(diag) if arrival else _dev(origin)
            for t in range(n):
                half = ins[t].shape[1] // 2
                rows = ins[t].at[slot, pl.ds(j * half, half)]
                send_sem, recv_sem = _sem_pair(sems, t, j, 2, True)
                cps.append(pltpu.make_async_remote_copy(
                    src_ref=rows, dst_ref=rows, send_sem=send_sem, recv_sem=recv_sem,
                    device_id=to, device_id_type=pl.DeviceIdType.MESH))
        return cps

    def start(ins, outs, sems):
        for cp in copies(ins, outs, sems, False):
            cp.start()

    def finish(ins, outs, sems):
        for cp in copies(ins, outs, sems, True):
            cp.wait_recv()
        for cp in copies(ins, outs, sems, False):
            cp.wait_send()

    return _Comm(list(zones), [SDS(a.shape, a.dtype) for a in zones], [pltpu.SemaphoreType.DMA(())] * (4 * n),
                 start, finish, aliases={t: t for t in range(n)})


def _gather_fwd(partial):
    n = len(partial)

    def copies(ins, outs, sems, arrival):
        x, y, c = _mesh_pos()
        sibling = _peer(x, y, c, _SIBLING)
        cps = []
        for j, k in enumerate(_ICI_RELS):
            slot = _dev(_peer(x, y, c, k | _SIBLING if arrival else k))
            for t in range(n):
                cps.append(pltpu.make_async_remote_copy(
                    src_ref=ins[t].at[slot], dst_ref=outs[t].at[slot], send_sem=sems[0].at[t, j],
                    recv_sem=sems[1].at[t, j], device_id=sibling, device_id_type=pl.DeviceIdType.MESH))
        return cps

    def start(ins, outs, sems):
        for cp in copies(ins, outs, sems, False):
            cp.start()

    def finish(ins, outs, sems):
        for cp in copies(ins, outs, sems, True):
            cp.wait_recv()
        for cp in copies(ins, outs, sems, False):
            cp.wait_send()

    return _Comm(partial, [SDS(a.shape, a.dtype) for a in partial],
                 [pltpu.SemaphoreType.DMA((n, len(_ICI_RELS)))] * 2, start, finish,
                 aliases={t: t for t in range(n)})


def _scatter_sibling(slots):
    n = len(slots)

    def copies(ins, outs, sems):
        x, y, c = _mesh_pos()
        return [pltpu.make_async_remote_copy(
            src_ref=ins[t].at[:, 1 - c], dst_ref=outs[t], send_sem=sems[0].at[t], recv_sem=sems[1].at[t],
            device_id=_peer(x, y, c, _SIBLING), device_id_type=pl.DeviceIdType.MESH) for t in range(n)]

    def start(ins, outs, sems):
        for cp in copies(ins, outs, sems):
            cp.start()

    def finish(ins, outs, sems):
        for cp in copies(ins, outs, sems):
            cp.wait()

    return _Comm(slots, [SDS((s.shape[0],) + s.shape[2:], s.dtype) for s in slots],
                 [pltpu.SemaphoreType.DMA((n,))] * 2, start, finish)


def _scatter_ici(chip_sums, landing=None):
    n = len(chip_sums)

    def remote(ins, outs, sems, arrival):
        x, y, c = _mesh_pos()
        dst = ins[n:] if landing is not None else outs
        cps = []
        for j, k in enumerate(_ICI_RELS):
            peer = _peer(x, y, c, k)
            theirs, mine = 2 * peer[0] + peer[1], 2 * x + y
            for t in range(n):
                send_sem, recv_sem = _sem_pair(sems, t, j, len(_ICI_RELS), landing is not None)
                cps.append(pltpu.make_async_remote_copy(
                    src_ref=ins[t].at[theirs], dst_ref=dst[t].at[theirs if arrival else mine],
                    send_sem=send_sem, recv_sem=recv_sem,
                    device_id=peer, device_id_type=pl.DeviceIdType.MESH))
        return cps

    if landing is not None:
        def start_remote(ins, outs, sems):
            for cp in remote(ins, outs, sems, False):
                cp.start()

        def finish_remote(ins, outs, sems):
            for cp in remote(ins, outs, sems, True):
                cp.wait_recv()
            for cp in remote(ins, outs, sems, False):
                cp.wait_send()

        return _Comm(list(chip_sums) + list(landing), [SDS(a.shape, a.dtype) for a in landing],
                     [pltpu.SemaphoreType.DMA(())] * (2 * n * len(_ICI_RELS)), start_remote, finish_remote,
                     aliases={n + t: t for t in range(n)})

    def local(ins, outs, sems):
        x, y, _ = _mesh_pos()
        return [pltpu.make_async_copy(ins[t].at[2 * x + y], outs[t].at[2 * x + y], sems[2].at[t]) for t in range(n)]

    def start(ins, outs, sems):
        for cp in local(ins, outs, sems) + remote(ins, outs, sems, False):
            cp.start()

    def finish(ins, outs, sems):
        for cp in local(ins, outs, sems):
            cp.wait()
        for cp in remote(ins, outs, sems, True):
            cp.wait_recv()
        for cp in remote(ins, outs, sems, False):
            cp.wait_send()

    return _Comm(chip_sums, [SDS(a.shape, a.dtype) for a in chip_sums],
                 [pltpu.SemaphoreType.DMA((n, len(_ICI_RELS))), pltpu.SemaphoreType.DMA((n, len(_ICI_RELS))),
                  pltpu.SemaphoreType.DMA((n,))], start, finish)


def _chip_sum(slots, from_sibling, core, name):
    nq, _, r, c = slots.shape
    tr = _largest_tile(r, 1024, 16)

    def body(core_ref, a_ref, b_ref, o_ref):
        o_ref[...] = (a_ref[...].astype(F32) + b_ref[...].astype(F32)).astype(BF)

    return pl.pallas_call(
        body,
        grid_spec=pltpu.PrefetchScalarGridSpec(
            num_scalar_prefetch=1, grid=(nq, r // tr),
            in_specs=[pl.BlockSpec((None, None, tr, c), lambda q, i, core_ref: (q, core_ref[0], i, 0)),
                      pl.BlockSpec((None, tr, c), lambda q, i, core_ref: (q, i, 0))],
            out_specs=pl.BlockSpec((None, tr, c), lambda q, i, core_ref: (q, i, 0))),
        out_shape=SDS((nq, r, c), BF), compiler_params=pltpu.CompilerParams(vmem_limit_bytes=40 * MIB),
        name=name)(core, slots, from_sibling)


def _small_reduce(pack_g, meta_g, loss_scale, name, comm=None):
    w = pack_g.shape[2]

    def body(p_ref, m_ref, tot_ref, meta_ref, loss_ref):
        acc = p_ref[0]
        macc = m_ref[0]
        for k in range(1, N_DEV):
            acc = acc + p_ref[k]
            macc = macc + m_ref[k]
        tot = jnp.sum(acc, axis=0, keepdims=True)
        tot_ref[...] = tot
        meta_ref[...] = macc
        loss_ref[...] = jnp.full((1, LANES), loss_scale * jnp.sum(tot[:, w - LANES:w]), F32)

    return pl.pallas_call(
        body, out_shape=[SDS((1, w), F32), SDS(meta_g.shape[1:], F32), SDS((1, LANES), F32)],
        compiler_params=pltpu.CompilerParams(vmem_limit_bytes=32 * MIB), name=name)(pack_g, meta_g)


def _local_step(x, target, sw, plan):
    s_len, d = x.shape
    n_heads, n_meta = plan.n_heads, plan.n_meta
    l = n_meta + s_len
    lp = -(-l // LANES) * LANES
    tm = _largest_tile(lp, 544, 16)
    tq = _largest_tile(lp, 272, 16)
    te = _largest_tile(lp, 272, 16)
    tmd = _largest_tile(d, 512, LANES)

    plan.at("start")
    x, target = plan.gate((x, target))
    zmeta, zpad = jnp.zeros((n_meta, d), F32), jnp.zeros((lp - l, d), F32)
    h0 = jnp.concatenate([zmeta, x, zpad], axis=0)
    tpad = jnp.concatenate([zmeta, target, zpad], axis=0)
    plan.at("landed", (h0, tpad))
    h0 = lax.dynamic_update_slice(h0, plan.weights("meta"), (0, 0))

    split = plan.ffn1_split()
    if split is None:
        wg1, wu1, wd1 = plan.weights("ffn1")
        h1, a1, b1, u1 = _ffn_fwd(h0, sw["ffn1_norm"], wg1, wu1, wd1, tm, "ffn1_fwd", plan.comm("ffn1_fwd"))
    else:
        carry = _ffn_fwd_part(h0, sw["ffn1_norm"], *plan.weights("ffn1_landing"), split[0], None, tm, "ffn1_fwd_a",
                              plan.order_tokens())
        plan.at("ffn1_mid", (carry[0],))
        wg1, wu1, wd1 = plan.weights("ffn1")
        h1, a1, b1 = _ffn_fwd_part(h0, sw["ffn1_norm"], wg1, wu1, wd1, split[1], carry, tm, "ffn1_fwd_b",
                                   plan.order_tokens())
        u1 = carry[3]
    fs = wg1.shape[1]
    plan.at("after_ffn1_fwd", (h1,))
    win, pw, wout = plan.weights("mix")
    nz = win.shape[1]
    p_w = sw["pool_scale"].shape[1]
    npb = p_w // LANES
    fblk = nz // LANES - 1
    tnz = _largest_tile(nz, 1408, LANES)
    qw, kw, bfp, ps = sw["q_norm"], sw["k_norm"], sw["b_forget"], sw["pool_scale"]
    z, u2 = _norm_matmul(h1, sw["mix_norm"], win, tm, tnz, "mix_in", plan.comm("mix_in"))
    plan.at("after_mix_in", (u2,))
    cum = _fox_prep(z, bfp, fblk, "fox_prep")
    cumt = cum[:, :n_heads].T
    pool_o = _pool_fwd(z, pw, ps, "pool_fwd")
    att_o = _att_fwd(z, cum, cumt, qw, kw, n_heads, npb, tq, "att_fwd", plan.comm("att_fwd"))
    plan.at("after_att_fwd", (att_o,))
    h2 =_out_proj(h1, pool_o, att_o, wout, tm, "out_proj", plan.comm("out_proj"))
    wg2, wu2, wd2 = plan.weights("ffn2")
    h3, a2, b2, u3 = _ffn_fwd(h2, sw["ffn2_norm"], wg2, wu2, wd2, tm, "ffn2_fwd", plan.comm("ffn2_fwd"))
    dy, dob3, lsq = _loss_head(h3, tpad, n_meta, l, te, "loss_head")

    du3, da2, db2, hid2 = _ffn_bwd_dx(dob3, a2, b2, wg2, wu2, wd2, tm, "ffn2_bwd_dx", plan.comm("ffn2_bwd_dx"))
    dh2, dh2b, dn2 = _rms_bwd(du3, h2, sw["ffn2_norm"], dy, 1.0, te, "ffn2_rms_bwd")
    plan.grad("ffn2_w_gate", _matmul_tn(da2, u3, fs, d, "ffn2_dwg", plan.comm("ffn2_dwg")))
    plan.grad("ffn2_w_up", _matmul_tn(db2, u3, fs, d, "ffn2_dwu", plan.comm("ffn2_dwu")))
    plan.grad("ffn2_w_down", _matmul_tn(hid2, dob3, fs, d, "ffn2_dwd", plan.comm("ffn2_dwd")))
    plan.at("after_ffn2_dwd")

    dmix = _matmul_nt(dh2b, wout, tm, d, BF, "out_proj_bwd", plan.comm("out_proj_bwd"))
    plan.at("after_out_proj_bwd")
    tmp = _largest_tile(p_w, 512, LANES)
    plan.grad("w_out", jnp.concatenate([_matmul_tn(pool_o, dh2b, tmp, d, "dwout_pool"),
                                        _matmul_tn(att_o, dh2b, tmp, d, "dwout_att")], axis=0))
    dzp, dpw, dps = _pool_bwd(z, dmix, pw, ps, "pool_bwd")
    plan.grad("pool_w", dpw)
    plan.at("before_att_bwd")
    dq, dk, dv, dck, dqw, dkw = _att_bwd(z, cum, cumt, qw, kw, dmix, n_heads, npb, npb, tq, "att_bwd",
                                              plan.comm("att_bwd"))
    dcum = -dck[:, 0, :].T
    dcum = jnp.pad(dcum, ((0, 0), (0, LANES - n_heads)))
    dzf, dbf = _fox_bwd(z, bfp, dcum, fblk, "fox_bwd")
    dz = jnp.concatenate([dzp, dq, dk, dv, dzf], axis=1)
    plan.grad("w_in", _matmul_tn(u2, dz, tmd, tnz, "dwin", plan.comm("dwin")))
    du2 = _matmul_nt(dz, win, tm, tnz, F32, "mix_in_bwd", plan.comm("mix_in_bwd"))
    plan.at("before_ffn1_bwd_dx")
    dh1, dob1, dnm = _rms_bwd(du2, h1, sw["mix_norm"], dh2, 0.5, te, "mix_rms_bwd")

    du1, da1, db1, hid1 = _ffn_bwd_dx(dob1, a1, b1, wg1, wu1, wd1, tm, "ffn1_bwd_dx", plan.comm("ffn1_bwd_dx"))
    plan.grad("ffn1_w_gate", _matmul_tn(da1, u1, fs, d, "ffn1_dwg", plan.comm("ffn1_dwg")))
    plan.grad("ffn1_w_up", _matmul_tn(db1, u1, fs, d, "ffn1_dwu", plan.comm("ffn1_dwu")))
    plan.at("before_ffn1_dwd")
    plan.grad("ffn1_w_down", _matmul_tn(hid1, dob1, fs, d, "ffn1_dwd", plan.comm("ffn1_dwd")))
    plan.at("after_ffn1_dwd")
    dh0, _, dn1 = _rms_bwd(du1, h0, sw["ffn1_norm"], dh1, 1.0, te, "ffn1_rms_bwd", plan.comm("ffn1_rms_bwd"))

    small = [dn1, dnm, dn2, dps, dqw, dkw, dbf, lsq]
    return dh0[n_meta:l], dh0[:n_meta], small


_BIG = ("ffn1_w_gate", "ffn1_w_up", "ffn1_w_down", "w_in", "pool_w", "w_out", "ffn2_w_gate", "ffn2_w_up", "ffn2_w_down")
_SMALL = ("ffn1_norm", "mix_norm", "ffn2_norm", "pool_scale", "q_norm", "k_norm", "b_forget")
_ORDER = ("meta_tokens", "ffn1_norm", "ffn1_w_gate", "ffn1_w_up", "ffn1_w_down", "mix_norm", "w_in", "b_forget",
          "q_norm", "k_norm", "pool_w", "pool_scale", "w_out", "ffn2_norm", "ffn2_w_gate", "ffn2_w_up", "ffn2_w_down")


_FFN1 = ("ffn1_w_gate", "ffn1_w_up", "ffn1_w_down")
_FFN2 = ("ffn2_w_gate", "ffn2_w_up", "ffn2_w_down")
_MIX = ("w_in", "pool_w", "w_out")

_RIDES = {
    "out_proj": (("g2", _FFN2),),
    "ffn2_dwu": (("s1", ("ffn2_w_gate",)),),
    "ffn2_dwd": (("s1", ("ffn2_w_up",)),),
    "out_proj_bwd": (("s1", ("ffn2_w_down",)),),
    "mix_in_bwd": (("s1", _MIX),),
    "ffn1_dwu": (("s1", ("ffn1_w_gate",)),),
    "ffn1_dwd": (("s1", ("ffn1_w_up",)),),
    "ffn1_rms_bwd": (("s1", ("ffn1_w_down",)),),
}
_META = ("meta_tokens",)
_POINTS = {
    "start": (("start", "gm", _META), ("start", "g1a", _FFN1), ("gate", _MIX + _FFN2), ("prepare", "g1", _MIX),
              ("prepare", "g1a", _FFN2)),
    "landed": (("wait", "gm", _META), ("wait", "g1a", _FFN1), ("start", "g1b", _FFN1), ("start", "g1", _MIX),
               ("start", "g1a", _FFN2)),
    "ffn1_mid": (("wait", "g1b", _FFN1), ("alone", "g2", _FFN1)),
    "after_ffn1_fwd": (("wait", "g1", _MIX), ("alone", "g2", _MIX)),
    "after_mix_in": (("wait", "g1a", _FFN2), ("start", "g1b", _FFN2)),
    "after_att_fwd": (("wait", "g1b", _FFN2),),
    "after_ffn2_dwd": (("sum", ("ffn2_w_gate",)), ("start", "s2", ("ffn2_w_gate",))),
    "after_out_proj_bwd": (("sum", ("ffn2_w_up",)), ("start", "s2", ("ffn2_w_up",))),
    "before_att_bwd": (("sum", ("ffn2_w_down",)), ("start", "s2", ("ffn2_w_down",))),
    "before_ffn1_bwd_dx": (("sum", _MIX), ("start", "s2", _MIX)),
    "before_ffn1_dwd": (("sum", ("ffn1_w_gate",)), ("start", "s2", ("ffn1_w_gate",))),
    "after_ffn1_dwd": (("sum", ("ffn1_w_up",)), ("start", "s2", ("ffn1_w_up",))),
    "after_ffn1_rms_bwd": (("sum", ("ffn1_w_down",)), ("start", "s2", ("ffn1_w_down",))),
    "before_adamw_ffn2_w_gate": (("wait", "s2", ("ffn2_w_gate",)),),
    "before_adamw_ffn2_w_up": (("wait", "s2", ("ffn2_w_up",)),),
    "before_adamw_ffn2_w_down": (("wait", "s2", ("ffn2_w_down",)),),
    "before_adamw_w_in": (("wait", "s2", _MIX),),
    "before_adamw_ffn1_w_gate": (("wait", "s2", ("ffn1_w_gate",)),),
    "before_adamw_ffn1_w_up": (("wait", "s2", ("ffn1_w_up",)),),
    "before_adamw_ffn1_w_down": (("wait", "s2", ("ffn1_w_down",)),),
}


def _own_slot_filled(block, slot, n_slots):
    zone = lax.empty((n_slots,) + block.shape, block.dtype)
    return lax.dynamic_update_slice(zone, block[None], (slot,) + (0,) * block.ndim)


class _MeshPlan:
    def __init__(self, raw, pos, d, d_in, n_heads):
        self.raw, self.pos = dict(raw), pos
        self.core = pos[2].astype(jnp.int32).reshape(1)
        self.d, self.d_in, self.n_heads, self.n_meta = d, d_in, n_heads, raw["meta_tokens"].shape[0]
        self.partial, self.full, self.slots, self.from_sibling, self.chip_sum, self.received = {}, {}, {}, {}, {}, {}
        self.partial_a, self.pending, self.prepared, self.started, self.tokens = {}, [], {}, {}, []

    def gate(self, arrays):
        gated = lax.optimization_barrier((self.tokens[-1], tuple(arrays)))
        self.tokens[-1] = gated[0]
        return gated[1]

    def _phase(self, kind, names):
        src, dst, make = {"g2": (self.partial, self.full, _gather_fwd),
                          "s1": (self.slots, self.from_sibling, _scatter_sibling),
                          "s2": (self.chip_sum, self.received, _scatter_ici)}[kind]
        op = make([src[n] for n in names])
        self.pending.append((op, dst, names))
        return op

    def _settle(self):
        for op, dst, names in self.pending:
            dst.update(zip(names, op.results))
        self.pending = []

    def _prepare(self, kind, names):
        x, y, c = self.pos
        if kind in ("g1", "g1a", "gm"):
            blocks = [self.raw[n] if kind == "gm" else self.raw[n].astype(BF) for n in names]
            rels = {"g1": (_SIBLING,) + _ICI_RELS, "g1a": (_SIBLING,) + _ICI_RELS[:2], "gm": tuple(range(1, N_DEV))}[kind]
            op = _gather_ici(blocks, [_own_slot_filled(b, 4 * x + 2 * y + c, N_DEV) for b in blocks], rels)
        elif kind == "g1b":
            op = _gather_diagonal([self.partial_a[n] for n in names])
        else:
            sums = [self.chip_sum[n] for n in names]
            mine = [lax.dynamic_index_in_dim(s, 2 * x + y, 0, keepdims=False) for s in sums]
            op = _scatter_ici(sums, [_own_slot_filled(b, 2 * x + y, N_DEV // 2) for b in mine])
        self.prepared[(kind, names)] = op

    def _start(self, kind, names):
        if (kind, names) not in self.prepared:
            self._prepare(kind, names)
        self._launch((kind, names), self.prepared.pop((kind, names)), "_".join(("start", kind, names[0])))

    def _launch(self, key, op, name):
        if self.tokens:
            op.arrs = list(self.gate(op.arrs))
        self.started[key], token = _split_start(op, name)
        self.tokens.append(token)

    def start_small_gather(self, arrays):
        x, y, c = self.pos
        zones = [_own_slot_filled(a, 4 * x + 2 * y + c, N_DEV) for a in arrays]
        self._launch("small", _gather_ici(list(arrays), zones, rels=tuple(range(1, N_DEV))), "start_gather_small")

    def wait_small_gather(self, afters):
        return _split_wait(self.started.pop("small"), afters, "wait_gather_small")

    def _wait(self, kind, names, afters):
        afters = list(afters) + [a for op in self.prepared.values() for a in op.arrs[len(op.arrs) // 2:]]
        landed = _split_wait(self.started.pop((kind, names)), afters, "_".join(("wait", kind, names[0])))
        {"g1": self.partial, "g1a": self.partial_a, "g1b": self.partial, "gm": self.partial,
         "s2": self.received}[kind].update(zip(names, landed))

    def ffn1_split(self):
        x, y, c = self.pos
        first = [(x, y, c), _peer(x, y, c, 1), _peer(x, y, c, 4), _peer(x, y, c, 2)]
        last = [_peer(x, y, c, 6), _peer(x, y, c, 5), _peer(x, y, c, 3), _peer(x, y, c, 7)]
        return tuple(jnp.stack([_dev(p) for p in part]).astype(jnp.int32) for part in (first, last))

    def order_tokens(self):
        tokens, self.tokens = self.tokens, []
        return tokens

    def comm(self, kernel_name):
        self._settle()
        ops = [self._phase(kind, names) for kind, names in _RIDES.get(kernel_name, ())]
        if self.tokens:
            ops.append(_Comm(self.tokens, [], [], lambda *a: None, lambda *a: None))
            self.tokens = []
        return _merge_comm(ops)

    def at(self, point, after=()):
        for step in _POINTS.get(point, ()):
            self._settle()
            if step[0] == "alone":
                _comm_alone(self._phase(step[1], step[2]), "_".join((step[1], point)))
            elif step[0] == "start":
                self._start(step[1], step[2])
            elif step[0] == "prepare":
                self._prepare(step[1], step[2])
            elif step[0] == "gate":
                self.raw.update(zip(step[1], self.gate([self.raw[n] for n in step[1]])))
            elif step[0] == "wait":
                self._wait(step[1], step[2], tuple(after) + tuple(self.tokens[-1:]))
            else:
                for n in step[1]:
                    self.chip_sum[n] = _chip_sum(self.slots[n], self.from_sibling[n], self.core, "chip_sum_" + n)

    def weights(self, group):
        self._settle()
        f, d = self.full, self.d
        if group == "meta":
            g = self.partial["meta_tokens"]
            return g.transpose(1, 0, 2).reshape(g.shape[1], d)
        if group == "ffn1_landing":
            return tuple(self.started[("g1b", _FFN1)][2])
        if group == "ffn1":
            return tuple(f[n] for n in _FFN1)
        if group == "ffn2":
            return tuple(f[n] for n in _FFN2)
        n_main = self.d_in - self.n_heads
        win = f["w_in"].transpose(1, 0, 2).reshape(d, self.d_in)
        win = jnp.concatenate([win[:, :n_main], jnp.pad(win[:, n_main:], ((0, 0), (0, LANES - self.n_heads)))], axis=1)
        pw = f["pool_w"]
        gw = pw.shape[2]
        pw = pw.reshape(N_DEV, -1, gw // N_DEV, gw).transpose(1, 0, 2, 3).reshape(-1, gw, gw)
        return win, pw, f["w_out"].reshape(-1, d)

    def grad(self, name, g):
        d = self.d
        if name == "w_in":
            g = g[:, :self.d_in].reshape(d, N_DEV, -1).transpose(1, 0, 2)
        elif name == "pool_w":
            ng, gw = g.shape[0], g.shape[2]
            g = g.astype(BF).reshape(ng, N_DEV, -1, gw).transpose(1, 0, 2, 3).reshape(N_DEV, -1, gw)
        elif name == "w_out":
            g = g.reshape(N_DEV, -1, d)
        self.slots[name] = g.reshape((N_DEV // 2, 2) + g.shape[1:])

    def gradient_parts(self, name):
        self._settle()
        return self.received[name]


_TRANSPOSED = ("ffn1_w_gate", "ffn1_w_up", "ffn2_w_gate", "ffn2_w_up")


def _as2d(name, a):
    return a[0].T if name in _TRANSPOSED else a.reshape(-1, a.shape[-1])


def _from2d(name, a2d, shape):
    return a2d.T.reshape(shape) if name in _TRANSPOSED else a2d.reshape(shape)


def kernel(x, meta_tokens, ffn1_norm, ffn1_w_gate, ffn1_w_up, ffn1_w_down, mix_norm, w_in, b_forget, q_norm, k_norm, pool_w, pool_scale, w_out, ffn2_norm, ffn2_w_gate, ffn2_w_up, ffn2_w_down, loss_target, m_meta_tokens, m_ffn1_norm, m_ffn1_w_gate, m_ffn1_w_up, m_ffn1_w_down, m_mix_norm, m_w_in, m_b_forget, m_q_norm, m_k_norm, m_pool_w, m_pool_scale, m_w_out, m_ffn2_norm, m_ffn2_w_gate, m_ffn2_w_up, m_ffn2_w_down, v_meta_tokens, v_ffn1_norm, v_ffn1_w_gate, v_ffn1_w_up, v_ffn1_w_down, v_mix_norm, v_w_in, v_b_forget, v_q_norm, v_k_norm, v_pool_w, v_pool_scale, v_w_out, v_ffn2_norm, v_ffn2_w_gate, v_ffn2_w_up, v_ffn2_w_down):
    w = dict(meta_tokens=meta_tokens, ffn1_norm=ffn1_norm, ffn1_w_gate=ffn1_w_gate, ffn1_w_up=ffn1_w_up,
             ffn1_w_down=ffn1_w_down, mix_norm=mix_norm, w_in=w_in, b_forget=b_forget, q_norm=q_norm, k_norm=k_norm,
             pool_w=pool_w, pool_scale=pool_scale, w_out=w_out, ffn2_norm=ffn2_norm, ffn2_w_gate=ffn2_w_gate,
             ffn2_w_up=ffn2_w_up, ffn2_w_down=ffn2_w_down)
    m = dict(meta_tokens=m_meta_tokens, ffn1_norm=m_ffn1_norm, ffn1_w_gate=m_ffn1_w_gate, ffn1_w_up=m_ffn1_w_up,
             ffn1_w_down=m_ffn1_w_down, mix_norm=m_mix_norm, w_in=m_w_in, b_forget=m_b_forget, q_norm=m_q_norm,
             k_norm=m_k_norm, pool_w=m_pool_w, pool_scale=m_pool_scale, w_out=m_w_out, ffn2_norm=m_ffn2_norm,
             ffn2_w_gate=m_ffn2_w_gate, ffn2_w_up=m_ffn2_w_up, ffn2_w_down=m_ffn2_w_down)
    v = dict(meta_tokens=v_meta_tokens, ffn1_norm=v_ffn1_norm, ffn1_w_gate=v_ffn1_w_gate, ffn1_w_up=v_ffn1_w_up,
             ffn1_w_down=v_ffn1_w_down, mix_norm=v_mix_norm, w_in=v_w_in, b_forget=v_b_forget, q_norm=v_q_norm,
             k_norm=v_k_norm, pool_w=v_pool_w, pool_scale=v_pool_scale, w_out=v_w_out, ffn2_norm=v_ffn2_norm,
             ffn2_w_gate=v_ffn2_w_gate, ffn2_w_up=v_ffn2_w_up, ffn2_w_down=v_ffn2_w_down)

    d = x.shape[-1]
    n_heads = b_forget.shape[-1]
    pos = (lax.axis_index("x"), lax.axis_index("y"), lax.axis_index("c"))
    me = 4 * pos[0] + 2 * pos[1] + pos[2]

    raw = {k: _as2d(k, w[k]) for k in _BIG}
    raw["meta_tokens"] = meta_tokens
    plan = _MeshPlan(raw, pos, d, N_DEV * w_in.shape[-1], n_heads)
    sw = {k: w[k] for k in _SMALL}
    sw["b_forget"] = jnp.pad(b_forget, ((0, 0), (0, LANES - n_heads)))
    dx, dmeta, small = _local_step(x[0], loss_target[0], sw, plan)

    res = {}
    last = dx

    plan.start_small_gather([jnp.concatenate(small, axis=1), dmeta])
    plan.at("after_ffn1_rms_bwd")

    def update_shards(names):
        nonlocal last
        for k in names:
            plan.at("before_adamw_" + k, (last,))
            state = [pltpu.with_memory_space_constraint(_as2d(k, t[k]), pltpu.HBM) for t in (w, m, v)]
            res[k] = _adamw(plan.gradient_parts(k), *state, "adamw_" + k, plan.comm("adamw_" + k))
            last = res[k][0]

    update_shards(_FFN2 + _MIX + ("ffn1_w_gate", "ffn1_w_up"))

    pack_g, meta_g = plan.wait_small_gather((last,))
    tot, dmeta_tot, loss_row = _small_reduce(pack_g, meta_g, 0.5 / d, "small_reduce")

    mcols = meta_tokens.shape[1]
    g_meta = lax.dynamic_slice_in_dim(dmeta_tot, me * mcols, mcols, axis=1)
    res["meta_tokens"] = _adamw(g_meta, meta_tokens, m_meta_tokens, v_meta_tokens, "adamw_meta_tokens")

    def packed(src):
        return jnp.concatenate([src[k] for k in _SMALL[:-1]] + [jnp.pad(src["b_forget"], ((0, 0), (0, LANES - n_heads)))],
                               axis=1)

    wp = packed(w)
    sm = _adamw(tot[:, :wp.shape[1]], wp, packed(m), packed(v), "adamw_small")
    off = 0
    for k in _SMALL:
        width = w[k].shape[1]
        res[k] = tuple(o[:, off:off + width] for o in sm)
        off += width if k != "b_forget" else LANES

    last = sm[0]
    update_shards(("ffn1_w_down",))

    outs =[loss_row[0, 0], dx[None]]
    for idx in range(4):
        outs += [_from2d(k, res[k][idx], w[k].shape) for k in _ORDER]
    return tuple(outs)
```

```python
import functools

import jax
import jax.numpy as jnp
from jax import lax
from jax.experimental import pallas as pl
from jax.experimental.pallas import tpu as pltpu

F32 = jnp.float32
BF = jnp.bfloat16
SDS = jax.ShapeDtypeStruct

N_DEV = 8
LANES = 128
SUBLANES = 8
HEAD_DIM = 128
POOL_WINDOWS = (2, 4, 8, 16)
RMS_EPS = 1e-6
NEG_BIG = -1e30
MIB = 1024 * 1024

ADAM_LR = 0.001
ADAM_B1 = 0.9
ADAM_B2 = 0.999
ADAM_EPS = 1e-08
ADAM_WD = 0.01
ADAM_STEP = 10


class _Comm:
    def __init__(self, arrs, out_shape, sems, start, finish, aliases=None):
        self.arrs, self.out_shape, self.sems = list(arrs), list(out_shape), list(sems)
        self.start, self.finish, self.aliases = start, finish, dict(aliases or {})
        self.results = None


def _merge_comm(ops):
    ops = [op for op in ops if op is not None]
    if not ops:
        return None
    na, no, ns = [0], [0], [0]
    for op in ops:
        na.append(na[-1] + len(op.arrs))
        no.append(no[-1] + len(op.out_shape))
        ns.append(ns[-1] + len(op.sems))

    def parts(i, ins, outs, sems):
        return ins[na[i]:na[i + 1]], outs[no[i]:no[i + 1]], sems[ns[i]:ns[i + 1]]

    def start(ins, outs, sems):
        for i, op in enumerate(ops):
            op.start(*parts(i, ins, outs, sems))

    def finish(ins, outs, sems):
        for i, op in enumerate(ops):
            op.finish(*parts(i, ins, outs, sems))

    aliases = {}
    for i, op in enumerate(ops):
        for a, o in op.aliases.items():
            aliases[na[i] + a] = no[i] + o
    merged = _Comm([a for op in ops for a in op.arrs], [s for op in ops for s in op.out_shape],
                   [s for op in ops for s in op.sems], start, finish, aliases)
    merged.children = (ops, no)
    return merged


def _deliver(comm, results):
    comm.results = list(results)
    if hasattr(comm, "children"):
        ops, no = comm.children
        for i, op in enumerate(ops):
            _deliver(op, results[no[i]:no[i + 1]])


def _call(body, *, grid, in_specs, out_specs, out_shape, scratch_shapes=(), vmem_mib, name, comm=None):
    single = not isinstance(out_shape, (list, tuple))
    out_specs = [out_specs] if single else list(out_specs)
    out_shape = [out_shape] if single else list(out_shape)
    in_specs, scratch_shapes = list(in_specs), list(scratch_shapes)
    params = pltpu.CompilerParams(dimension_semantics=("arbitrary",) * len(grid), vmem_limit_bytes=vmem_mib * MIB)
    n_in, n_out, n_scr = len(in_specs), len(out_specs), len(scratch_shapes)

    def run(*args):
        if comm is None:
            res = pl.pallas_call(body, grid=grid, in_specs=in_specs, out_specs=out_specs, out_shape=out_shape,
                                 scratch_shapes=scratch_shapes, compiler_params=params, name=name)(*args)
            return res[0] if single else res
        ci, co = len(comm.arrs), len(comm.out_shape)

        def with_comm(*refs):
            ins, cins = refs[:n_in], refs[n_in:n_in + ci]
            o0 = n_in + ci
            outs, couts = refs[o0:o0 + n_out], refs[o0 + n_out:o0 + n_out + co]
            s0 = o0 + n_out + co
            scr, csems = refs[s0:s0 + n_scr], refs[s0 + n_scr:]
            ids = [pl.program_id(a) for a in range(len(grid))]
            first = functools.reduce(jnp.logical_and, [i == 0 for i in ids])
            last = functools.reduce(jnp.logical_and, [i == g - 1 for i, g in zip(ids, grid)])

            @pl.when(first)
            def _():
                comm.start(cins, couts, csems)

            body(*ins, *outs, *scr)

            @pl.when(last)
            def _():
                comm.finish(cins, couts, csems)

        anyspec = pl.BlockSpec(memory_space=pl.ANY)
        res = pl.pallas_call(
            with_comm, grid=grid, in_specs=in_specs + [anyspec] * ci, out_specs=out_specs + [anyspec] * co,
            out_shape=out_shape + comm.out_shape, scratch_shapes=scratch_shapes + comm.sems,
            input_output_aliases={n_in + a: n_out + o for a, o in comm.aliases.items()},
            compiler_params=params, name=name)(*args, *comm.arrs)
        _deliver(comm, res[n_out:])
        return res[0] if single else res[:n_out]

    return run


def _comm_alone(comm, name):
    def body(*refs):
        ci, co = len(comm.arrs), len(comm.out_shape)
        ins, outs, sems = refs[:ci], refs[ci:ci + co], refs[ci + co:]
        comm.start(ins, outs, sems)
        comm.finish(ins, outs, sems)

    anyspec = pl.BlockSpec(memory_space=pl.ANY)
    res = pl.pallas_call(
        body, in_specs=[anyspec] * len(comm.arrs), out_specs=[anyspec] * len(comm.out_shape),
        out_shape=comm.out_shape, scratch_shapes=comm.sems, input_output_aliases=comm.aliases, name=name)(*comm.arrs)
    _deliver(comm, res)


def _split_start(comm, name):
    na, ns = len(comm.arrs), len(comm.sems)

    def body(*refs):
        comm.start(refs[:na], None, refs[na:na + ns])
        token = refs[-1]
        token[...] = jnp.zeros_like(token)

    hbm = pl.BlockSpec(memory_space=pltpu.HBM)
    res = pl.pallas_call(
        body, name=name,
        out_shape=tuple(comm.sems) + tuple(pltpu.HBM(a.shape, a.dtype) for a in comm.arrs)
        + (SDS((SUBLANES, LANES), F32),),
        in_specs=[hbm] * na,
        out_specs=[pl.BlockSpec(memory_space=pltpu.SEMAPHORE)] * ns + [hbm] * na + [pl.BlockSpec(memory_space=pltpu.VMEM)],
        input_output_aliases={i: ns + i for i in range(na)},
        compiler_params=pltpu.CompilerParams(has_side_effects=pltpu.SideEffectType.DATAFLOW_SIDE_EFFECTING),
    )(*[pltpu.with_memory_space_constraint(a, pltpu.HBM) for a in comm.arrs])
    return (comm, res[:ns], res[ns:ns + na]), res[-1]


def _split_wait(started, afters, name):
    comm, sems, thru = started
    na, ns = len(thru), len(sems)
    afters = list(afters)

    def body(*refs):
        comm.finish(refs[:na], None, refs[na:na + ns])

    hbm = pl.BlockSpec(memory_space=pltpu.HBM)
    res = pl.pallas_call(
        body, name=name, out_shape=tuple(pltpu.HBM(a.shape, a.dtype) for a in thru),
        in_specs=[hbm] * na + [pl.BlockSpec(memory_space=pltpu.SEMAPHORE)] * ns
        + [pl.BlockSpec(memory_space=pl.ANY)] * len(afters),
        out_specs=[hbm] * na, input_output_aliases={i: i for i in range(na)},
        compiler_params=pltpu.CompilerParams(has_side_effects=pltpu.SideEffectType.DATAFLOW_SIDE_EFFECTING),
    )(*thru, *sems, *afters)
    return res[na - len(comm.out_shape):]


def _largest_tile(n, cap, mult):
    if n <= cap:
        return n
    best = None
    for t in range(mult, cap + 1, mult):
        if n % t == 0:
            best = t
    assert best is not None, (n, cap, mult)
    return best


def _dot(a, b):
    return jnp.dot(a, b, preferred_element_type=F32)


def _dot_nt(a, b):
    return lax.dot_general(a, b, (((1,), (1,)), ((), ())), preferred_element_type=F32)


def _dot_tn(a, b):
    return lax.dot_general(a, b, (((0,), (0,)), ((), ())), preferred_element_type=F32)


def _rows8(x):
    t, c = x.shape
    return jnp.sum(x.reshape(t // SUBLANES, SUBLANES, c), axis=0)


def _rstd(x):
    return lax.rsqrt(jnp.mean(x * x, axis=-1, keepdims=True) + RMS_EPS)


def _ffn_fwd(h, g, wg, wu, wd, tm, name, comm=None):
    lp, d = h.shape
    ns, fs, _ = wg.shape

    def body(h_ref, g_ref, wg_ref, wu_ref, wd_ref, out_ref, a_ref, b_ref, u_ref, acc_ref):
        j = pl.program_id(1)

        @pl.when(j == 0)
        def _():
            hh = h_ref[...]
            u_ref[...] = (hh * _rstd(hh) * g_ref[...]).astype(BF)
            acc_ref[...] = jnp.zeros_like(acc_ref)

        u = u_ref[...]
        a = _dot_nt(u, wg_ref[...])
        b = _dot_nt(u, wu_ref[...])
        a_ref[...] = a.astype(BF)
        b_ref[...] = b.astype(BF)
        hid = (a * jax.nn.sigmoid(a) * b).astype(BF)
        acc_ref[...] += _dot(hid, wd_ref[...])

        @pl.when(j == ns - 1)
        def _():
            out_ref[...] = h_ref[...] + 0.5 * acc_ref[...]

    row = pl.BlockSpec((tm, d), lambda i, j: (i, 0))
    act = pl.BlockSpec((None, tm, fs), lambda i, j: (j, i, 0))
    return _call(
        body, grid=(lp // tm, ns),
        in_specs=[row, pl.BlockSpec((1, d), lambda i, j: (0, 0)),
                  pl.BlockSpec((None, fs, d), lambda i, j: (j, 0, 0)),
                  pl.BlockSpec((None, fs, d), lambda i, j: (j, 0, 0)),
                  pl.BlockSpec((None, fs, d), lambda i, j: (j, 0, 0))],
        out_specs=[row, act, act, row],
        out_shape=[SDS((lp, d), F32), SDS((ns, lp, fs), BF), SDS((ns, lp, fs), BF), SDS((lp, d), BF)],
        scratch_shapes=[pltpu.VMEM((tm, d), F32)],
        vmem_mib=56, name=name, comm=comm)(h, g, wg, wu, wd)


def _ffn_fwd_part(h, g, wg, wu, wd, order, carry, tm, name, deps=()):
    lp, d = h.shape
    fs = wg.shape[1]
    k = order.shape[0]
    first = carry is None
    n_in = 5 if first else 8

    def body(order_ref, *refs):
        outs = refs[n_in + len(deps):]
        if first:
            h_ref, g_ref, wg_ref, wu_ref, wd_ref = refs[:n_in]
            out_ref, a_ref, b_ref, u_ref, acc_ref = outs
        else:
            h_ref, acc_in_ref, u_ref, _, _, wg_ref, wu_ref, wd_ref = refs[:n_in]
            out_ref, a_ref, b_ref, acc_ref = outs
        j = pl.program_id(1)

        @pl.when(j == 0)
        def _():
            if first:
                hh = h_ref[...]
                u_ref[...] = (hh * _rstd(hh) * g_ref[...]).astype(BF)
                acc_ref[...] = jnp.zeros_like(acc_ref)
            else:
                acc_ref[...] = acc_in_ref[...]

        u = u_ref[...]
        a = _dot_nt(u, wg_ref[...])
        b = _dot_nt(u, wu_ref[...])
        a_ref[...] = a.astype(BF)
        b_ref[...] = b.astype(BF)
        hid = (a * jax.nn.sigmoid(a) * b).astype(BF)
        acc_ref[...] += _dot(hid, wd_ref[...])

        @pl.when(j == k - 1)
        def _():
            out_ref[...] = acc_ref[...] if first else h_ref[...] + 0.5 * acc_ref[...]

    row = pl.BlockSpec((tm, d), lambda i, j, o: (i, 0))
    act = pl.BlockSpec((None, tm, fs), lambda i, j, o: (o[j], i, 0))
    wsp = pl.BlockSpec((None, fs, d), lambda i, j, o: (o[j], 0, 0))
    anyspec = pl.BlockSpec(memory_space=pl.ANY)
    acts = [SDS((wg.shape[0], lp, fs), BF)] * 2
    if first:
        in_specs = [row, pl.BlockSpec((1, d), lambda i, j, o: (0, 0)), wsp, wsp, wsp]
        out_specs, out_shape = [row, act, act, row], [SDS((lp, d), F32)] + acts + [SDS((lp, d), BF)]
        args, aliases = (h, g, wg, wu, wd), {}
    else:
        acc, a_prev, b_prev, u_prev = carry
        in_specs = [row, row, row, anyspec, anyspec, wsp, wsp, wsp]
        out_specs, out_shape = [row, act, act], [SDS((lp, d), F32)] + acts
        args, aliases = (h, acc, u_prev, a_prev, b_prev, wg, wu, wd), {4: 1, 5: 2}
    return pl.pallas_call(
        body,
        grid_spec=pltpu.PrefetchScalarGridSpec(
            num_scalar_prefetch=1, grid=(lp // tm, k), in_specs=in_specs + [anyspec] * len(deps),
            out_specs=out_specs, scratch_shapes=[pltpu.VMEM((tm, d), F32)]),
        out_shape=out_shape, input_output_aliases=aliases,
        compiler_params=pltpu.CompilerParams(dimension_semantics=("arbitrary",) * 2, vmem_limit_bytes=60 * MIB),
        name=name)(order, *args, *deps)


def _ffn_bwd_act(dob, a, b, wd, tm, name, comm=None):
    lp, d = dob.shape
    ns, fs, _ = wd.shape
    row = pl.BlockSpec((tm, d), lambda i, j: (i, 0))
    act = pl.BlockSpec((None, tm, fs), lambda i, j: (j, i, 0))
    wsp = pl.BlockSpec((None, fs, d), lambda i, j: (j, 0, 0))

    def act_body(do_ref, a_ref, b_ref, wd_ref, da_ref, db_ref, hid_ref):
        dhid = _dot_nt(do_ref[...], wd_ref[...])
        av = a_ref[...].astype(F32)
        bv = b_ref[...].astype(F32)
        sig = jax.nn.sigmoid(av)
        sil = av * sig
        hid_ref[...] = (sil * bv).astype(BF)
        da_ref[...] = (dhid * bv * (sig * (1.0 + av * (1.0 - sig)))).astype(BF)
        db_ref[...] = (dhid * sil).astype(BF)

    return _call(
        act_body, grid=(lp // tm, ns), in_specs=[row, act, act, wsp], out_specs=[act, act, act],
        out_shape=[SDS((ns, lp, fs), BF)] * 3, vmem_mib=40, name=name, comm=comm)(dob, a, b, wd)


def _ffn_bwd_du(da, db, wg, wu, tm, name, comm=None):
    ns, lp, fs = da.shape
    d = wg.shape[2]
    row = pl.BlockSpec((tm, d), lambda i, j: (i, 0))
    act = pl.BlockSpec((None, tm, fs), lambda i, j: (j, i, 0))
    wsp = pl.BlockSpec((None, fs, d), lambda i, j: (j, 0, 0))

    def du_body(da_ref, db_ref, wg_ref, wu_ref, du_ref):
        @pl.when(pl.program_id(1) == 0)
        def _():
            du_ref[...] = jnp.zeros_like(du_ref)

        du_ref[...] += _dot(da_ref[...], wg_ref[...]) + _dot(db_ref[...], wu_ref[...])

    return _call(
        du_body, grid=(lp // tm, ns), in_specs=[act, act, wsp, wsp], out_specs=row,
        out_shape=SDS((lp, d), F32), vmem_mib=48, name=name, comm=comm)(da, db, wg, wu)


def _rms_bwd(du, h, g, dres, bscale, tm, name, comm=None):
    lp, d = h.shape

    def body(du_ref, h_ref, g_ref, dres_ref, dh_ref, dhb_ref, dg_ref):
        @pl.when(pl.program_id(0) == 0)
        def _():
            dg_ref[...] = jnp.zeros_like(dg_ref)

        hh = h_ref[...]
        r = _rstd(hh)
        xhat = hh * r
        duv = du_ref[...]
        dg_ref[...] += _rows8(duv * xhat)
        dxh = duv * g_ref[...]
        dh = dres_ref[...] + r * (dxh - xhat * jnp.mean(dxh * xhat, axis=-1, keepdims=True))
        dh_ref[...] = dh
        dhb_ref[...] = (bscale * dh).astype(BF)

    row = pl.BlockSpec((tm, d), lambda i: (i, 0))
    return _call(
        body, grid=(lp // tm,),
        in_specs=[row, row, pl.BlockSpec((1, d), lambda i: (0, 0)), row],
        out_specs=[row, row, pl.BlockSpec((SUBLANES, d), lambda i: (0, 0))],
        out_shape=[SDS((lp, d), F32), SDS((lp, d), BF), SDS((SUBLANES, d), F32)],
        vmem_mib=48, name=name, comm=comm)(du, h, g, dres)


def _matmul_tn(a, b, tm, tn, name, comm=None):
    a_b, b_b = a.ndim == 3, b.ndim == 3
    ns = a.shape[0] if a_b else (b.shape[0] if b_b else 1)
    l, m = a.shape[-2:]
    n = b.shape[-1]

    def body(a_ref, b_ref, o_ref):
        o_ref[...] = _dot_tn(a_ref[...], b_ref[...]).astype(o_ref.dtype)

    a_spec = (pl.BlockSpec((None, l, tm), lambda s, i, j: (s, 0, i)) if a_b
              else pl.BlockSpec((l, tm), lambda s, i, j: (0, i)))
    b_spec = (pl.BlockSpec((None, l, tn), lambda s, i, j: (s, 0, j)) if b_b
              else pl.BlockSpec((l, tn), lambda s, i, j: (0, j)))
    batched = a_b or b_b
    o_spec = (pl.BlockSpec((None, tm, tn), lambda s, i, j: (s, i, j)) if batched
              else pl.BlockSpec((tm, tn), lambda s, i, j: (i, j)))
    o_shape = SDS((ns, m, n), BF) if batched else SDS((m, n), BF)
    return _call(
        body, grid=(ns, m // tm, n // tn), in_specs=[a_spec, b_spec], out_specs=o_spec, out_shape=o_shape,
        vmem_mib=48, name=name, comm=comm)(a, b)


def _matmul_nt(x, w, tm, tk, out_dtype, name, comm=None):
    l, k = x.shape
    n = w.shape[0]
    nk = k // tk

    def body(x_ref, w_ref, o_ref, acc_ref):
        kk = pl.program_id(1)

        @pl.when(kk == 0)
        def _():
            acc_ref[...] = jnp.zeros_like(acc_ref)

        acc_ref[...] += _dot_nt(x_ref[...], w_ref[...])

        @pl.when(kk == nk - 1)
        def _():
            o_ref[...] = acc_ref[...].astype(o_ref.dtype)

    return _call(
        body, grid=(l // tm, nk),
        in_specs=[pl.BlockSpec((tm, tk), lambda i, kk: (i, kk)), pl.BlockSpec((n, tk), lambda i, kk: (0, kk))],
        out_specs=pl.BlockSpec((tm, n), lambda i, kk: (i, 0)),
        out_shape=SDS((l, n), out_dtype),
        scratch_shapes=[pltpu.VMEM((tm, n), F32)],
        vmem_mib=48, name=name, comm=comm)(x, w)


def _norm_matmul(h, g, w, tm, tn, name, comm=None):
    lp, d = h.shape
    n = w.shape[1]

    def body(h_ref, g_ref, w_ref, z_ref, u_ref):
        @pl.when(pl.program_id(1) == 0)
        def _():
            hh = h_ref[...]
            u_ref[...] = (hh * _rstd(hh) * g_ref[...]).astype(BF)

        z_ref[...] = _dot(u_ref[...], w_ref[...])

    row = pl.BlockSpec((tm, d), lambda i, j: (i, 0))
    return _call(
        body, grid=(lp // tm, n // tn),
        in_specs=[row, pl.BlockSpec((1, d), lambda i, j: (0, 0)), pl.BlockSpec((d, tn), lambda i, j: (0, j))],
        out_specs=[pl.BlockSpec((tm, tn), lambda i, j: (i, j)), row],
        out_shape=[SDS((lp, n), F32), SDS((lp, d), BF)],
        vmem_mib=48, name=name, comm=comm)(h, g, w)


def _out_proj(h, pool_o, att_o, w_out, tm, name, comm=None):
    lp, d = h.shape
    p = pool_o.shape[1]
    dm = w_out.shape[0]

    def body(h_ref, p_ref, a_ref, w_ref, o_ref):
        o_ref[...] = h_ref[...] + _dot(p_ref[...], w_ref[0:p, :]) + _dot(a_ref[...], w_ref[p:dm, :])

    row = pl.BlockSpec((tm, d), lambda i: (i, 0))
    return _call(
        body, grid=(lp // tm,),
        in_specs=[row, pl.BlockSpec((tm, p), lambda i: (i, 0)), pl.BlockSpec((tm, dm - p), lambda i: (i, 0)),
                  pl.BlockSpec((dm, d), lambda i: (0, 0))],
        out_specs=row, out_shape=SDS((lp, d), F32),
        vmem_mib=48, name=name, comm=comm)(h, pool_o, att_o, w_out)


def _loss_head(y, tpad, row0, row1, tm, name, comm=None):
    lp, d = y.shape

    def body(y_ref, t_ref, dy_ref, dob_ref, ls_ref):
        i = pl.program_id(0)

        @pl.when(i == 0)
        def _():
            ls_ref[...] = jnp.zeros_like(ls_ref)

        rows = i * tm + lax.broadcasted_iota(jnp.int32, (tm, d), 0)
        err = jnp.where((rows >= row0) & (rows < row1), y_ref[...] - t_ref[...], 0.0)
        dy = err * (1.0 / d)
        dy_ref[...] = dy
        dob_ref[...] = (0.5 * dy).astype(BF)
        sq = _rows8(err * err)
        acc = sq[:, 0:LANES]
        for c in range(1, d // LANES):
            acc = acc + sq[:, c * LANES:(c + 1) * LANES]
        ls_ref[...] += acc

    row = pl.BlockSpec((tm, d), lambda i: (i, 0))
    return _call(
        body, grid=(lp // tm,), in_specs=[row, row],
        out_specs=[row, row, pl.BlockSpec((SUBLANES, LANES), lambda i: (0, 0))],
        out_shape=[SDS((lp, d), F32), SDS((lp, d), BF), SDS((SUBLANES, LANES), F32)],
        vmem_mib=48, name=name, comm=comm)(y, tpad)


def _window_select(levels, gidx):
    out = levels[-1]
    for k in range(len(levels) - 2, -1, -1):
        out = jnp.where(gidx == k, levels[k], out)
    return out


def _pool_window_mean_minus_id(x, gidx):
    rows = lax.broadcasted_iota(jnp.int32, x.shape, 0)
    levels = []
    s = x
    shift = 1
    while shift < POOL_WINDOWS[-1]:
        s = s + jnp.where(rows >= shift, pltpu.roll(s, shift, 0), 0.0)
        shift *= 2
        if shift in POOL_WINDOWS:
            levels.append(s)
    win = _window_select(levels, gidx)
    cnt = jnp.minimum(rows + 1, _window_select(list(POOL_WINDOWS), gidx)).astype(F32)
    return win / cnt - x, cnt


def _pool_window_transpose(dy, cnt, gidx):
    lp = dy.shape[0]
    rows = lax.broadcasted_iota(jnp.int32, dy.shape, 0)
    levels = []
    s = dy / cnt
    shift = 1
    while shift < POOL_WINDOWS[-1]:
        s = s + jnp.where(rows < lp - shift, pltpu.roll(s, lp - shift, 0), 0.0)
        shift *= 2
        if shift in POOL_WINDOWS:
            levels.append(s)
    return _window_select(levels, gidx) - dy


def _pool_fwd(z, pool_w, pool_scale, name, comm=None):
    lp = z.shape[0]
    ng, gw, _ = pool_w.shape

    def body(p_ref, w_ref, s_ref, o_ref):
        pooled, _ = _pool_window_mean_minus_id(p_ref[...], pl.program_id(0))
        o_ref[...] = (_dot(pooled.astype(BF), w_ref[...]) * s_ref[...]).astype(BF)

    return _call(
        body, grid=(ng,),
        in_specs=[pl.BlockSpec((lp, gw), lambda g: (0, g)), pl.BlockSpec((None, gw, gw), lambda g: (g, 0, 0)),
                  pl.BlockSpec((1, gw), lambda g: (0, g))],
        out_specs=pl.BlockSpec((lp, gw), lambda g: (0, g)), out_shape=SDS((lp, ng * gw), BF),
        vmem_mib=48, name=name, comm=comm)(z, pool_w, pool_scale)


def _pool_bwd(z, dmix, pool_w, pool_scale, name, comm=None):
    lp = z.shape[0]
    ng, gw, _ = pool_w.shape

    def body(p_ref, d_ref, w_ref, s_ref, dz_ref, dw_ref, ds_ref):
        g = pl.program_id(0)
        pooled, cnt = _pool_window_mean_minus_id(p_ref[...], g)
        pooled_b = pooled.astype(BF)
        w = w_ref[...]
        mixed = _dot(pooled_b, w)
        dpo = d_ref[...].astype(F32)
        ds_ref[...] = _rows8(dpo * mixed)
        dmixed = (dpo * s_ref[...]).astype(BF)
        dw_ref[...] = _dot_tn(pooled_b, dmixed)
        dpooled = _dot_nt(dmixed, w)
        dz_ref[...] = _pool_window_transpose(dpooled, cnt, g).astype(BF)

    return _call(
        body, grid=(ng,),
        in_specs=[pl.BlockSpec((lp, gw), lambda g: (0, g)), pl.BlockSpec((lp, gw), lambda g: (0, g)),
                  pl.BlockSpec((None, gw, gw), lambda g: (g, 0, 0)), pl.BlockSpec((1, gw), lambda g: (0, g))],
        out_specs=[pl.BlockSpec((lp, gw), lambda g: (0, g)), pl.BlockSpec((None, gw, gw), lambda g: (g, 0, 0)),
                   pl.BlockSpec((SUBLANES, gw), lambda g: (0, g))],
        out_shape=[SDS((lp, ng * gw), BF), SDS((ng, gw, gw), F32), SDS((SUBLANES, ng * gw), F32)],
        vmem_mib=48, name=name, comm=comm)(z, dmix, pool_w, pool_scale)


def _log_sigmoid(x):
    return jnp.minimum(x, 0.0) - jnp.log(1.0 + jnp.exp(-jnp.abs(x)))


def _fox_prep(z, bfp, fblk, name, comm=None):
    lp = z.shape[0]
    nb = lp // LANES

    def body(f_ref, b_ref, cum_ref):
        r = lax.broadcasted_iota(jnp.int32, (LANES, LANES), 0)
        c = lax.broadcasted_iota(jnp.int32, (LANES, LANES), 1)
        tri = (r >= c).astype(F32)
        carry = jnp.zeros((1, LANES), F32)
        for blk in range(nb):
            sl = slice(blk * LANES, (blk + 1) * LANES)
            lf = _log_sigmoid(f_ref[sl, :] + b_ref[...])
            cb = jnp.dot(tri, lf, preferred_element_type=F32, precision=lax.Precision.HIGHEST) + carry
            cum_ref[sl, :] = cb
            carry = cb[LANES - 1:LANES, :]

    return _call(
        body, grid=(1,),
        in_specs=[pl.BlockSpec((lp, LANES), lambda i: (0, fblk)), pl.BlockSpec((1, LANES), lambda i: (0, 0))],
        out_specs=pl.BlockSpec((lp, LANES), lambda i: (0, 0)), out_shape=SDS((lp, LANES), F32),
        vmem_mib=32, name=name, comm=comm)(z, bfp)


def _fox_bwd(z, bfp, dcum, fblk, name, comm=None):
    lp = z.shape[0]
    nb = lp // LANES

    def body(f_ref, b_ref, dc_ref, dz_ref, db_ref):
        r = lax.broadcasted_iota(jnp.int32, (LANES, LANES), 0)
        c = lax.broadcasted_iota(jnp.int32, (LANES, LANES), 1)
        tri = (r <= c).astype(F32)
        carry = jnp.zeros((1, LANES), F32)
        acc = jnp.zeros((SUBLANES, LANES), F32)
        for blk in range(nb - 1, -1, -1):
            sl = slice(blk * LANES, (blk + 1) * LANES)
            dlf = jnp.dot(tri, dc_ref[sl, :], preferred_element_type=F32, precision=lax.Precision.HIGHEST) + carry
            carry = dlf[0:1, :]
            df = dlf * jax.nn.sigmoid(-(f_ref[sl, :] + b_ref[...]))
            dz_ref[sl, :] = df.astype(BF)
            acc = acc + _rows8(df)
        db_ref[...] = acc

    return _call(
        body, grid=(1,),
        in_specs=[pl.BlockSpec((lp, LANES), lambda i: (0, fblk)), pl.BlockSpec((1, LANES), lambda i: (0, 0)),
                  pl.BlockSpec((lp, LANES), lambda i: (0, 0))],
        out_specs=[pl.BlockSpec((lp, LANES), lambda i: (0, 0)), pl.BlockSpec((SUBLANES, LANES), lambda i: (0, 0))],
        out_shape=[SDS((lp, LANES), BF), SDS((SUBLANES, LANES), F32)],
        vmem_mib=32, name=name, comm=comm)(z, bfp, dcum)


def _att_scores(q_ref, cum_ref, cumt_ref, qw_ref, kn_s, h, i, tq, lk):
    scale = 1.0 / (HEAD_DIM ** 0.5)
    q = q_ref[...]
    rq = _rstd(q)
    qhat = q * rq
    qn = (qhat * qw_ref[...]).astype(BF)
    s = _dot_nt(qn, kn_s[0:lk, :]) * scale
    lane = lax.broadcasted_iota(jnp.int32, (tq, LANES), 1)
    cq = jnp.sum(jnp.where(lane == h, cum_ref[...], 0.0), axis=1, keepdims=True)
    ck = cumt_ref[pl.ds(h, 1), 0:lk]
    s = s + (cq - ck)
    qpos = i * tq + lax.broadcasted_iota(jnp.int32, (tq, lk), 0)
    kpos = lax.broadcasted_iota(jnp.int32, (tq, lk), 1)
    s = jnp.where(qpos >= kpos, s, NEG_BIG)
    e = jnp.exp(s - jnp.max(s, axis=1, keepdims=True))
    p = e * (1.0 / jnp.sum(e, axis=1, keepdims=True))
    return p, qn, qhat, rq


def _per_query_tile(i, nq, tq, lp, fn):
    for t in range(nq):
        lk = min(lp, -(-((t + 1) * tq) // LANES) * LANES)
        pl.when(i == t)(functools.partial(fn, lk))


def _att_fwd(z, cum, cumt, qw, kw, n_heads, qblk0, tq, name, comm=None):
    lp = z.shape[0]
    nh = n_heads

    def body(q_ref, k_ref, v_ref, cum_ref, cumt_ref, qw_ref, kw_ref, o_ref, kn_s, vb_s):
        h, i = pl.program_id(0), pl.program_id(1)

        @pl.when(i == 0)
        def _():
            k = k_ref[...]
            kn_s[...] = (k * _rstd(k) * kw_ref[...]).astype(BF)
            vb_s[...] = v_ref[...].astype(BF)

        def tile(lk):
            p, _, _, _ = _att_scores(q_ref, cum_ref, cumt_ref, qw_ref, kn_s, h, i, tq, lk)
            o_ref[...] = _dot(p.astype(BF), vb_s[0:lk, :]).astype(BF)

        _per_query_tile(i, lp // tq, tq, lp, tile)

    vec = pl.BlockSpec((1, HEAD_DIM), lambda h, i: (0, 0))
    return _call(
        body, grid=(nh, lp // tq),
        in_specs=[pl.BlockSpec((tq, HEAD_DIM), lambda h, i: (i, qblk0 + h)),
                  pl.BlockSpec((lp, HEAD_DIM), lambda h, i: (0, qblk0 + nh + h)),
                  pl.BlockSpec((lp, HEAD_DIM), lambda h, i: (0, qblk0 + 2 * nh + h)),
                  pl.BlockSpec((tq, LANES), lambda h, i: (i, 0)),
                  pl.BlockSpec((nh, lp), lambda h, i: (0, 0)), vec, vec],
        out_specs=pl.BlockSpec((tq, HEAD_DIM), lambda h, i: (i, h)),
        out_shape=SDS((lp, nh * HEAD_DIM), BF),
        scratch_shapes=[pltpu.VMEM((lp, HEAD_DIM), BF), pltpu.VMEM((lp, HEAD_DIM), BF)],
        vmem_mib=48, name=name, comm=comm)(z, z, z, cum, cumt, qw, kw)


def _att_bwd(z, cum, cumt, qw, kw, dmix, n_heads, qblk0, oblk0, tq, name, comm=None):
    lp = z.shape[0]
    nh = n_heads
    nq = lp // tq
    scale = 1.0 / (HEAD_DIM ** 0.5)

    def body(q_ref, k_ref, v_ref, cum_ref, cumt_ref, qw_ref, kw_ref, do_ref,
             dq_ref, dk_ref, dv_ref, dck_ref, dqw_ref, dkw_ref,
             kn_s, vb_s, dkn_s, dv_s, dck_s):
        h, i = pl.program_id(0), pl.program_id(1)

        @pl.when((h == 0) & (i == 0))
        def _():
            dqw_ref[...] = jnp.zeros_like(dqw_ref)
            dkw_ref[...] = jnp.zeros_like(dkw_ref)

        @pl.when(i == 0)
        def _():
            k = k_ref[...]
            kn_s[...] = (k * _rstd(k) * kw_ref[...]).astype(BF)
            vb_s[...] = v_ref[...].astype(BF)
            dkn_s[...] = jnp.zeros_like(dkn_s)
            dv_s[...] = jnp.zeros_like(dv_s)
            dck_s[...] = jnp.zeros_like(dck_s)

        def tile(lk):
            p, qn, qhat, rq = _att_scores(q_ref, cum_ref, cumt_ref, qw_ref, kn_s, h, i, tq, lk)
            dob = do_ref[...]
            dp = _dot_nt(dob, vb_s[0:lk, :])
            ds = p * (dp - jnp.sum(p * dp, axis=1, keepdims=True))
            dsb = ds.astype(BF)
            dv_s[0:lk, :] += _dot_tn(p.astype(BF), dob)
            dkn_s[0:lk, :] += _dot_tn(dsb, qn)
            dck_s[:, 0:lk] += jnp.sum(ds, axis=0, keepdims=True)
            dqn = _dot(dsb, kn_s[0:lk, :]) * scale
            gq = dqn * qw_ref[...]
            dq_ref[...] = (rq * (gq - qhat * jnp.mean(gq * qhat, axis=-1, keepdims=True))).astype(BF)
            dqw_ref[...] += _rows8(dqn * qhat)

        _per_query_tile(i, nq, tq, lp, tile)

        @pl.when(i == nq - 1)
        def _():
            k = k_ref[...]
            rk = _rstd(k)
            khat = k * rk
            dkn = dkn_s[...] * scale
            gk = dkn * kw_ref[...]
            dk_ref[...] = (rk * (gk - khat * jnp.mean(gk * khat, axis=-1, keepdims=True))).astype(BF)
            dkw_ref[...] += _rows8(dkn * khat)
            dv_ref[...] = dv_s[...].astype(BF)
            dck_ref[...] = dck_s[...]

    vec = pl.BlockSpec((1, HEAD_DIM), lambda h, i: (0, 0))
    part = pl.BlockSpec((SUBLANES, LANES), lambda h, i: (0, 0))
    return _call(
        body, grid=(nh, nq),
        in_specs=[pl.BlockSpec((tq, HEAD_DIM), lambda h, i: (i, qblk0 + h)),
                  pl.BlockSpec((lp, HEAD_DIM), lambda h, i: (0, qblk0 + nh + h)),
                  pl.BlockSpec((lp, HEAD_DIM), lambda h, i: (0, qblk0 + 2 * nh + h)),
                  pl.BlockSpec((tq, LANES), lambda h, i: (i, 0)),
                  pl.BlockSpec((nh, lp), lambda h, i: (0, 0)), vec, vec,
                  pl.BlockSpec((tq, HEAD_DIM), lambda h, i: (i, oblk0 + h))],
        out_specs=[pl.BlockSpec((tq, HEAD_DIM), lambda h, i: (i, h)),
                   pl.BlockSpec((lp, HEAD_DIM), lambda h, i: (0, h)),
                   pl.BlockSpec((lp, HEAD_DIM), lambda h, i: (0, h)),
                   pl.BlockSpec((None, 1, lp), lambda h, i: (h, 0, 0)),
                   part, part],
        out_shape=[SDS((lp, nh * HEAD_DIM), BF)] * 3
        + [SDS((nh, 1, lp), F32), SDS((SUBLANES, LANES), F32), SDS((SUBLANES, LANES), F32)],
        scratch_shapes=[pltpu.VMEM((lp, HEAD_DIM), BF), pltpu.VMEM((lp, HEAD_DIM), BF),
                        pltpu.VMEM((lp, HEAD_DIM), F32), pltpu.VMEM((lp, HEAD_DIM), F32),
                        pltpu.VMEM((1, lp), F32)],
        vmem_mib=56, name=name, comm=comm)(z, z, z, cum, cumt, qw, kw, dmix)


def _adamw_math(w, g, m, v):
    m2 = ADAM_B1 * m + (1.0 - ADAM_B1) * g
    v2 = ADAM_B2 * v + (1.0 - ADAM_B2) * (g * g)
    m_hat = m2 / (1.0 - ADAM_B1 ** ADAM_STEP)
    v_hat = v2 / (1.0 - ADAM_B2 ** ADAM_STEP)
    delta = -ADAM_LR * (m_hat / (jnp.sqrt(v_hat) + ADAM_EPS) + ADAM_WD * w)
    return delta, m2, v2


def _adamw(g_in, w, m, v, name, comm=None):
    r, c = w.shape
    partial_sum = g_in.ndim == 3
    lane_padded = -(-c // LANES) * LANES
    tr = _largest_tile(r, max(16, MIB // (4 * lane_padded) // 16 * 16), 16)

    def body(g_ref, w_ref, m_ref, v_ref, go_ref, d_ref, mo_ref, vo_ref):
        if partial_sum:
            g = g_ref[0].astype(F32)
            for k in range(1, g_in.shape[0]):
                g = g + g_ref[k].astype(F32)
        else:
            g = g_ref[...]
        delta, m2, v2 = _adamw_math(w_ref[...], g, m_ref[...], v_ref[...])
        go_ref[...] = g
        d_ref[...] = delta
        mo_ref[...] = m2
        vo_ref[...] = v2

    blk = pl.BlockSpec((tr, c), lambda i: (i, 0))
    g_spec = pl.BlockSpec((g_in.shape[0], tr, c), lambda i: (0, i, 0)) if partial_sum else blk
    return _call(
        body, grid=(r // tr,), in_specs=[g_spec, blk, blk, blk], out_specs=[blk] * 4,
        out_shape=[SDS((r, c), F32)] * 4, vmem_mib=40, name=name, comm=comm)(g_in, w, m, v)


def _peer(x, y, c, k):
    return (1 - x if k & 4 else x, 1 - y if k & 2 else y, 1 - c if k & 1 else c)


_SIBLING = 1
_ICI_RELS = (2, 4, 6)


def _mesh_pos():
    return lax.axis_index("x"), lax.axis_index("y"), lax.axis_index("c")


def _sem_pair(sems, t, j, n_rel, scalars):
    if scalars:
        i = 2 * (t * n_rel + j)
        return sems[i], sems[i + 1]
    return sems[0].at[t, j], sems[1].at[t, j]


def _dev(pos):
    return 4 * pos[0] + 2 * pos[1] + pos[2]


def _gather_ici(shards, landing=None, rels=(_SIBLING,) + _ICI_RELS):
    n = len(shards)

    def remote(ins, outs, sems, arrival):
        x, y, c = _mesh_pos()
        dst = ins[n:] if landing is not None else outs
        cps = []
        for j, k in enumerate(rels):
            peer = _peer(x, y, c, k)
            slot = _dev(peer) if arrival else _dev((x, y, c))
            for t in range(n):
                send_sem, recv_sem = _sem_pair(sems, t, j, len(rels), landing is not None)
                cps.append(pltpu.make_async_remote_copy(
                    src_ref=ins[t], dst_ref=dst[t].at[slot], send_sem=send_sem, recv_sem=recv_sem,
                    device_id=peer, device_id_type=pl.DeviceIdType.MESH))
        return cps

    if landing is not None:
        def start_remote(ins, outs, sems):
            for cp in remote(ins, outs, sems, False):
                cp.start()

        def finish_remote(ins, outs, sems):
            for cp in remote(ins, outs, sems, True):
                cp.wait_recv()
            for cp in remote(ins, outs, sems, False):
                cp.wait_send()

        return _Comm(list(shards) + list(landing), [SDS(a.shape, a.dtype) for a in landing],
                     [pltpu.SemaphoreType.DMA(())] * (2 * n * len(rels)),
                     start_remote, finish_remote, aliases={n + t: t for t in range(n)})

    def local(ins, outs, sems):
        me = _dev(_mesh_pos())
        return [pltpu.make_async_copy(ins[t], outs[t].at[me], sems[2].at[t]) for t in range(n)]

    def start(ins, outs, sems):
        for cp in local(ins, outs, sems) + remote(ins, outs, sems, False):
            cp.start()

    def finish(ins, outs, sems):
        for cp in local(ins, outs, sems):
            cp.wait()
        for cp in remote(ins, outs, sems, True):
            cp.wait_recv()
        for cp in remote(ins, outs, sems, False):
            cp.wait_send()

    return _Comm(shards, [SDS((N_DEV,) + s.shape, s.dtype) for s in shards],
                 [pltpu.SemaphoreType.DMA((n, len(rels))), pltpu.SemaphoreType.DMA((n, len(rels))),
                  pltpu.SemaphoreType.DMA((n,))], start, finish)


def _gather_diagonal(zones):
    n = len(zones)

    def copies(ins, outs, sems, arrival):
        x, y, c = _mesh_pos()
        y_nb, x_nb, diag = _peer(x, y, c, 2), _peer(x, y, c, 4), _peer(x, y, c, 6)
        cps = []
        for j, (to, origin) in enumerate(((y_nb, x_nb), (x_nb, y_nb))):
            slot = _dev(diag) if arrival else _dev(origin)
            for t in range(n):
                half = ins[t].shape[1] // 2
                rows = ins[t].at[slot, pl.ds(j * half, half)]
                send_sem, recv_sem = _sem_pair(sems, t, j, 2, True)
                cps.append(pltpu.make_async_remote_copy(
                    src_ref=rows, dst_ref=rows, send_sem=send_sem, recv_sem=recv_sem,
                    device_id=to, device_id_type=pl.DeviceIdType.MESH))
        return cps

    def start(ins, outs, sems):
        for cp in copies(ins, outs, sems, False):
            cp.start()

    def finish(ins, outs, sems):
        for cp in copies(ins, outs, sems, True):
            cp.wait_recv()
        for cp in copies(ins, outs, sems, False):
            cp.wait_send()

    return _Comm(list(zones), [SDS(a.shape, a.dtype) for a in zones], [pltpu.SemaphoreType.DMA(())] * (4 * n),
                 start, finish, aliases={t: t for t in range(n)})


def _gather_fwd(partial):
    n = len(partial)

    def copies(ins, outs, sems, arrival):
        x, y, c = _mesh_pos()
        sibling = _peer(x, y, c, _SIBLING)
        cps = []
        for j, k in enumerate(_ICI_RELS):
            slot = _dev(_peer(x, y, c, k | _SIBLING if arrival else k))
            for t in range(n):
                cps.append(pltpu.make_async_remote_copy(
                    src_ref=ins[t].at[slot], dst_ref=outs[t].at[slot], send_sem=sems[0].at[t, j],
                    recv_sem=sems[1].at[t, j], device_id=sibling, device_id_type=pl.DeviceIdType.MESH))
        return cps

    def start(ins, outs, sems):
        for cp in copies(ins, outs, sems, False):
            cp.start()

    def finish(ins, outs, sems):
        for cp in copies(ins, outs, sems, True):
            cp.wait_recv()
        for cp in copies(ins, outs, sems, False):
            cp.wait_send()

    return _Comm(partial, [SDS(a.shape, a.dtype) for a in partial],
                 [pltpu.SemaphoreType.DMA((n, len(_ICI_RELS)))] * 2, start, finish,
                 aliases={t: t for t in range(n)})


def _scatter_sibling(slots):
    n = len(slots)

    def copies(ins, outs, sems):
        x, y, c = _mesh_pos()
        return [pltpu.make_async_remote_copy(
            src_ref=ins[t].at[:, 1 - c], dst_ref=outs[t], send_sem=sems[0].at[t], recv_sem=sems[1].at[t],
            device_id=_peer(x, y, c, _SIBLING), device_id_type=pl.DeviceIdType.MESH) for t in range(n)]

    def start(ins, outs, sems):
        for cp in copies(ins, outs, sems):
            cp.start()

    def finish(ins, outs, sems):
        for cp in copies(ins, outs, sems):
            cp.wait()

    return _Comm(slots, [SDS((s.shape[0],) + s.shape[2:], s.dtype) for s in slots],
                 [pltpu.SemaphoreType.DMA((n,))] * 2, start, finish)


def _scatter_ici(chip_sums, landing=None):
    n = len(chip_sums)

    def remote(ins, outs, sems, arrival):
        x, y, c = _mesh_pos()
        dst = ins[n:] if landing is not None else outs
        cps = []
        for j, k in enumerate(_ICI_RELS):
            peer = _peer(x, y, c, k)
            theirs, mine = 2 * peer[0] + peer[1], 2 * x + y
            for t in range(n):
                send_sem, recv_sem = _sem_pair(sems, t, j, len(_ICI_RELS), landing is not None)
                cps.append(pltpu.make_async_remote_copy(
                    src_ref=ins[t].at[theirs], dst_ref=dst[t].at[theirs if arrival else mine],
                    send_sem=send_sem, recv_sem=recv_sem,
                    device_id=peer, device_id_type=pl.DeviceIdType.MESH))
        return cps

    if landing is not None:
        def start_remote(ins, outs, sems):
            for cp in remote(ins, outs, sems, False):
                cp.start()

        def finish_remote(ins, outs, sems):
            for cp in remote(ins, outs, sems, True):
                cp.wait_recv()
            for cp in remote(ins, outs, sems, False):
                cp.wait_send()

        return _Comm(list(chip_sums) + list(landing), [SDS(a.shape, a.dtype) for a in landing],
                     [pltpu.SemaphoreType.DMA(())] * (2 * n * len(_ICI_RELS)), start_remote, finish_remote,
                     aliases={n + t: t for t in range(n)})

    def local(ins, outs, sems):
        x, y, _ = _mesh_pos()
        return [pltpu.make_async_copy(ins[t].at[2 * x + y], outs[t].at[2 * x + y], sems[2].at[t]) for t in range(n)]

    def start(ins, outs, sems):
        for cp in local(ins, outs, sems) + remote(ins, outs, sems, False):
            cp.start()

    def finish(ins, outs, sems):
        for cp in local(ins, outs, sems):
            cp.wait()
        for cp in remote(ins, outs, sems, True):
            cp.wait_recv()
        for cp in remote(ins, outs, sems, False):
            cp.wait_send()

    return _Comm(chip_sums, [SDS(a.shape, a.dtype) for a in chip_sums],
                 [pltpu.SemaphoreType.DMA((n, len(_ICI_RELS))), pltpu.SemaphoreType.DMA((n, len(_ICI_RELS))),
                  pltpu.SemaphoreType.DMA((n,))], start, finish)


def _chip_sum(slots, from_sibling, core, name):
    nq, _, r, c = slots.shape
    tr = _largest_tile(r, 1024, 16)

    def body(core_ref, a_ref, b_ref, o_ref):
        o_ref[...] = (a_ref[...].astype(F32) + b_ref[...].astype(F32)).astype(BF)

    return pl.pallas_call(
        body,
        grid_spec=pltpu.PrefetchScalarGridSpec(
            num_scalar_prefetch=1, grid=(nq, r // tr),
            in_specs=[pl.BlockSpec((None, None, tr, c), lambda q, i, core_ref: (q, core_ref[0], i, 0)),
                      pl.BlockSpec((None, tr, c), lambda q, i, core_ref: (q, i, 0))],
            out_specs=pl.BlockSpec((None, tr, c), lambda q, i, core_ref: (q, i, 0))),
        out_shape=SDS((nq, r, c), BF), compiler_params=pltpu.CompilerParams(vmem_limit_bytes=40 * MIB),
        name=name)(core, slots, from_sibling)


def _small_reduce(pack_g, meta_g, loss_scale, name, comm=None):
    w = pack_g.shape[2]

    def body(p_ref, m_ref, tot_ref, meta_ref, loss_ref):
        acc = p_ref[0]
        macc = m_ref[0]
        for k in range(1, N_DEV):
            acc = acc + p_ref[k]
            macc = macc + m_ref[k]
        tot = jnp.sum(acc, axis=0, keepdims=True)
        tot_ref[...] = tot
        meta_ref[...] = macc
        loss_ref[...] = jnp.full((1, LANES), loss_scale * jnp.sum(tot[:, w - LANES:w]), F32)

    return pl.pallas_call(
        body, out_shape=[SDS((1, w), F32), SDS(meta_g.shape[1:], F32), SDS((1, LANES), F32)],
        compiler_params=pltpu.CompilerParams(vmem_limit_bytes=32 * MIB), name=name)(pack_g, meta_g)


def _local_step(x, target, sw, plan):
    s_len, d = x.shape
    n_heads, n_meta = plan.n_heads, plan.n_meta
    l = n_meta + s_len
    lp = -(-l // LANES) * LANES
    tm = _largest_tile(lp, 544, 16)
    tq = _largest_tile(lp, 272, 16)
    te = _largest_tile(lp, 272, 16)
    tmd = _largest_tile(d, 512, LANES)

    plan.at("start")
    x, target = plan.gate((x, target))
    zmeta, zpad = jnp.zeros((n_meta, d), F32), jnp.zeros((lp - l, d), F32)
    h0 = jnp.concatenate([zmeta, x, zpad], axis=0)
    tpad = jnp.concatenate([zmeta, target, zpad], axis=0)
    plan.at("landed", (h0, tpad))
    h0 = lax.dynamic_update_slice(h0, plan.weights("meta"), (0, 0))

    split = plan.ffn1_split()
    if split is None:
        wg1, wu1, wd1 = plan.weights("ffn1")
        h1, a1, b1, u1 = _ffn_fwd(h0, sw["ffn1_norm"], wg1, wu1, wd1, tm, "ffn1_fwd", plan.comm("ffn1_fwd"))
    else:
        carry = _ffn_fwd_part(h0, sw["ffn1_norm"], *plan.weights("ffn1_landing"), split[0], None, tm, "ffn1_fwd_a",
                              plan.order_tokens())
        plan.at("ffn1_mid", (carry[0],))
        wg1, wu1, wd1 = plan.weights("ffn1")
        h1, a1, b1 = _ffn_fwd_part(h0, sw["ffn1_norm"], wg1, wu1, wd1, split[1], carry, tm, "ffn1_fwd_b",
                                   plan.order_tokens())
        u1 = carry[3]
    fs = wg1.shape[1]
    plan.at("after_ffn1_fwd", (h1,))
    win, pw, wout = plan.weights("mix")
    nz = win.shape[1]
    p_w = sw["pool_scale"].shape[1]
    npb = p_w // LANES
    fblk = nz // LANES - 1
    tnz = _largest_tile(nz, 1408, LANES)
    qw, kw, bfp, ps = sw["q_norm"], sw["k_norm"], sw["b_forget"], sw["pool_scale"]
    z, u2 = _norm_matmul(h1, sw["mix_norm"], win, tm, tnz, "mix_in", plan.comm("mix_in"))
    plan.at("after_mix_in", (u2,))
    cum = _fox_prep(z, bfp, fblk, "fox_prep")
    cumt = cum[:, :n_heads].T
    pool_o = _pool_fwd(z, pw, ps, "pool_fwd")
    att_o = _att_fwd(z, cum, cumt, qw, kw, n_heads, npb, tq, "att_fwd", plan.comm("att_fwd"))
    plan.at("after_att_fwd", (att_o,))
    h2 =_out_proj(h1, pool_o, att_o, wout, tm, "out_proj", plan.comm("out_proj"))
    wg2, wu2, wd2 = plan.weights("ffn2")
    h3, a2, b2, u3 = _ffn_fwd(h2, sw["ffn2_norm"], wg2, wu2, wd2, tm, "ffn2_fwd", plan.comm("ffn2_fwd"))
    dy, dob3, lsq = _loss_head(h3, tpad, n_meta, l, te, "loss_head")

    da2, db2, hid2 = _ffn_bwd_act(dob3, a2, b2, wd2, tm, "ffn2_bwd_act", plan.comm("ffn2_bwd_act"))
    du3 = _ffn_bwd_du(da2, db2, wg2, wu2, tm, "ffn2_bwd_du", plan.comm("ffn2_bwd_du"))
    dh2, dh2b, dn2 = _rms_bwd(du3, h2, sw["ffn2_norm"], dy, 1.0, te, "ffn2_rms_bwd")
    plan.grad("ffn2_w_gate", _matmul_tn(da2, u3, fs, d, "ffn2_dwg", plan.comm("ffn2_dwg")))
    plan.grad("ffn2_w_up", _matmul_tn(db2, u3, fs, d, "ffn2_dwu", plan.comm("ffn2_dwu")))
    plan.grad("ffn2_w_down", _matmul_tn(hid2, dob3, fs, d, "ffn2_dwd", plan.comm("ffn2_dwd")))
    plan.at("after_ffn2_dwd")

    dmix = _matmul_nt(dh2b, wout, tm, d, BF, "out_proj_bwd", plan.comm("out_proj_bwd"))
    plan.at("after_out_proj_bwd")
    tmp = _largest_tile(p_w, 512, LANES)
    plan.grad("w_out", jnp.concatenate([_matmul_tn(pool_o, dh2b, tmp, d, "dwout_pool"),
                                        _matmul_tn(att_o, dh2b, tmp, d, "dwout_att")], axis=0))
    dzp, dpw, dps = _pool_bwd(z, dmix, pw, ps, "pool_bwd")
    plan.grad("pool_w", dpw)
    plan.at("before_att_bwd")
    dq, dk, dv, dck, dqw, dkw = _att_bwd(z, cum, cumt, qw, kw, dmix, n_heads, npb, npb, tq, "att_bwd",
                                              plan.comm("att_bwd"))
    dcum = -dck[:, 0, :].T
    dcum = jnp.pad(dcum, ((0, 0), (0, LANES - n_heads)))
    dzf, dbf = _fox_bwd(z, bfp, dcum, fblk, "fox_bwd")
    dz = jnp.concatenate([dzp, dq, dk, dv, dzf], axis=1)
    plan.grad("w_in", _matmul_tn(u2, dz, tmd, tnz, "dwin", plan.comm("dwin")))
    du2 = _matmul_nt(dz, win, tm, tnz, F32, "mix_in_bwd", plan.comm("mix_in_bwd"))
    plan.at("before_ffn1_bwd_dx")
    dh1, dob1, dnm = _rms_bwd(du2, h1, sw["mix_norm"], dh2, 0.5, te, "mix_rms_bwd")

    da1, db1, hid1 = _ffn_bwd_act(dob1, a1, b1, wd1, tm, "ffn1_bwd_act", plan.comm("ffn1_bwd_act"))
    plan.grad("ffn1_w_gate", _matmul_tn(da1, u1, fs, d, "ffn1_dwg", plan.comm("ffn1_dwg")))
    plan.grad("ffn1_w_up", _matmul_tn(db1, u1, fs, d, "ffn1_dwu", plan.comm("ffn1_dwu")))
    plan.at("before_ffn1_dwd")
    plan.grad("ffn1_w_down", _matmul_tn(hid1, dob1, fs, d, "ffn1_dwd", plan.comm("ffn1_dwd")))
    plan.at("after_ffn1_dwd")
    du1 = _ffn_bwd_du(da1, db1, wg1, wu1, tm, "ffn1_bwd_du", plan.comm("ffn1_bwd_du"))
    dh0, _, dn1 = _rms_bwd(du1, h0, sw["ffn1_norm"], dh1, 1.0, te, "ffn1_rms_bwd", plan.comm("ffn1_rms_bwd"))

    small = [dn1, dnm, dn2, dps, dqw, dkw, dbf, lsq]
    return dh0[n_meta:l], dh0[:n_meta], small


_BIG = ("ffn1_w_gate", "ffn1_w_up", "ffn1_w_down", "w_in", "pool_w", "w_out", "ffn2_w_gate", "ffn2_w_up", "ffn2_w_down")
_SMALL = ("ffn1_norm", "mix_norm", "ffn2_norm", "pool_scale", "q_norm", "k_norm", "b_forget")
_ORDER = ("meta_tokens", "ffn1_norm", "ffn1_w_gate", "ffn1_w_up", "ffn1_w_down", "mix_norm", "w_in", "b_forget",
          "q_norm", "k_norm", "pool_w", "pool_scale", "w_out", "ffn2_norm", "ffn2_w_gate", "ffn2_w_up", "ffn2_w_down")


_FFN1 = ("ffn1_w_gate", "ffn1_w_up", "ffn1_w_down")
_FFN2 = ("ffn2_w_gate", "ffn2_w_up", "ffn2_w_down")
_MIX = ("w_in", "pool_w", "w_out")

_RIDES = {
    "out_proj": (("g2", _FFN2),),
    "ffn2_dwu": (("s1", ("ffn2_w_gate",)),),
    "ffn2_dwd": (("s1", ("ffn2_w_up",)),),
    "out_proj_bwd": (("s1", ("ffn2_w_down",)),),
    "mix_in_bwd": (("s1", _MIX),),
    "ffn1_dwu": (("s1", ("ffn1_w_gate",)),),
    "ffn1_dwd": (("s1", ("ffn1_w_up",)),),
    "ffn1_bwd_du": (("s1", ("ffn1_w_down",)),),
}
_META = ("meta_tokens",)
_POINTS = {
    "start": (("start", "gm", _META), ("start", "g1a", _FFN1), ("gate", _MIX + _FFN2), ("prepare", "g1", _MIX),
              ("prepare", "g1a", _FFN2)),
    "landed": (("wait", "gm", _META), ("wait", "g1a", _FFN1), ("start", "g1b", _FFN1), ("start", "g1", _MIX),
               ("start", "g1a", _FFN2)),
    "ffn1_mid": (("wait", "g1b", _FFN1), ("alone", "g2", _FFN1)),
    "after_ffn1_fwd": (("wait", "g1", _MIX), ("alone", "g2", _MIX)),
    "after_mix_in": (("wait", "g1a", _FFN2), ("start", "g1b", _FFN2)),
    "after_att_fwd": (("wait", "g1b", _FFN2),),
    "after_ffn2_dwd": (("sum", ("ffn2_w_gate",)), ("start", "s2", ("ffn2_w_gate",))),
    "after_out_proj_bwd": (("sum", ("ffn2_w_up",)), ("start", "s2", ("ffn2_w_up",))),
    "before_att_bwd": (("sum", ("ffn2_w_down",)), ("start", "s2", ("ffn2_w_down",))),
    "before_ffn1_bwd_dx": (("sum", _MIX), ("start", "s2", _MIX)),
    "before_ffn1_dwd": (("sum", ("ffn1_w_gate",)), ("start", "s2", ("ffn1_w_gate",))),
    "after_ffn1_dwd": (("sum", ("ffn1_w_up",)), ("start", "s2", ("ffn1_w_up",))),
    "after_ffn1_rms_bwd": (("sum", ("ffn1_w_down",)), ("start", "s2", ("ffn1_w_down",))),
    "before_adamw_ffn2_w_gate": (("wait", "s2", ("ffn2_w_gate",)),),
    "before_adamw_ffn2_w_up": (("wait", "s2", ("ffn2_w_up",)),),
    "before_adamw_ffn2_w_down": (("wait", "s2", ("ffn2_w_down",)),),
    "before_adamw_w_in": (("wait", "s2", _MIX),),
    "before_adamw_ffn1_w_gate": (("wait", "s2", ("ffn1_w_gate",)),),
    "before_adamw_ffn1_w_up": (("wait", "s2", ("ffn1_w_up",)),),
    "before_adamw_ffn1_w_down": (("wait", "s2", ("ffn1_w_down",)),),
}


def _own_slot_filled(block, slot, n_slots):
    zone = lax.empty((n_slots,) + block.shape, block.dtype)
    return lax.dynamic_update_slice(zone, block[None], (slot,) + (0,) * block.ndim)


class _MeshPlan:
    def __init__(self, raw, pos, d, d_in, n_heads):
        self.raw, self.pos = dict(raw), pos
        self.core = pos[2].astype(jnp.int32).reshape(1)
        self.d, self.d_in, self.n_heads, self.n_meta = d, d_in, n_heads, raw["meta_tokens"].shape[0]
        self.partial, self.full, self.slots, self.from_sibling, self.chip_sum, self.received = {}, {}, {}, {}, {}, {}
        self.partial_a, self.pending, self.prepared, self.started, self.tokens = {}, [], {}, {}, []

    def gate(self, arrays):
        gated = lax.optimization_barrier((self.tokens[-1], tuple(arrays)))
        self.tokens[-1] = gated[0]
        return gated[1]

    def _phase(self, kind, names):
        src, dst, make = {"g2": (self.partial, self.full, _gather_fwd),
                          "s1": (self.slots, self.from_sibling, _scatter_sibling),
                          "s2": (self.chip_sum, self.received, _scatter_ici)}[kind]
        op = make([src[n] for n in names])
        self.pending.append((op, dst, names))
        return op

    def _settle(self):
        for op, dst, names in self.pending:
            dst.update(zip(names, op.results))
        self.pending = []

    def _prepare(self, kind, names):
        x, y, c = self.pos
        if kind in ("g1", "g1a", "gm"):
            blocks = [self.raw[n] if kind == "gm" else self.raw[n].astype(BF) for n in names]
            rels = {"g1": (_SIBLING,) + _ICI_RELS, "g1a": (_SIBLING,) + _ICI_RELS[:2], "gm": tuple(range(1, N_DEV))}[kind]
            op = _gather_ici(blocks, [_own_slot_filled(b, 4 * x + 2 * y + c, N_DEV) for b in blocks], rels)
        elif kind == "g1b":
            op = _gather_diagonal([self.partial_a[n] for n in names])
        else:
            sums = [self.chip_sum[n] for n in names]
            mine = [lax.dynamic_index_in_dim(s, 2 * x + y, 0, keepdims=False) for s in sums]
            op = _scatter_ici(sums, [_own_slot_filled(b, 2 * x + y, N_DEV // 2) for b in mine])
        self.prepared[(kind, names)] = op

    def _start(self, kind, names):
        if (kind, names) not in self.prepared:
            self._prepare(kind, names)
        self._launch((kind, names), self.prepared.pop((kind, names)), "_".join(("start", kind, names[0])))

    def _launch(self, key, op, name):
        if self.tokens:
            op.arrs = list(self.gate(op.arrs))
        self.started[key], token = _split_start(op, name)
        self.tokens.append(token)

    def start_small_gather(self, arrays):
        x, y, c = self.pos
        zones = [_own_slot_filled(a, 4 * x + 2 * y + c, N_DEV) for a in arrays]
        self._launch("small", _gather_ici(list(arrays), zones, rels=tuple(range(1, N_DEV))), "start_gather_small")

    def wait_small_gather(self, afters):
        return _split_wait(self.started.pop("small"), afters, "wait_gather_small")

    def _wait(self, kind, names, afters):
        afters = list(afters) + [a for op in self.prepared.values() for a in op.arrs[len(op.arrs) // 2:]]
        landed = _split_wait(self.started.pop((kind, names)), afters, "_".join(("wait", kind, names[0])))
        {"g1": self.partial, "g1a": self.partial_a, "g1b": self.partial, "gm": self.partial,
         "s2": self.received}[kind].update(zip(names, landed))

    def ffn1_split(self):
        x, y, c = self.pos
        first = [(x, y, c), _peer(x, y, c, 1), _peer(x, y, c, 4), _peer(x, y, c, 2)]
        last = [_peer(x, y, c, 6), _peer(x, y, c, 5), _peer(x, y, c, 3), _peer(x, y, c, 7)]
        return tuple(jnp.stack([_dev(p) for p in part]).astype(jnp.int32) for part in (first, last))

    def order_tokens(self):
        tokens, self.tokens = self.tokens, []
        return tokens

    def comm(self, kernel_name):
        self._settle()
        ops = [self._phase(kind, names) for kind, names in _RIDES.get(kernel_name, ())]
        if self.tokens:
            ops.append(_Comm(self.tokens, [], [], lambda *a: None, lambda *a: None))
            self.tokens = []
        return _merge_comm(ops)

    def at(self, point, after=()):
        for step in _POINTS.get(point, ()):
            self._settle()
            if step[0] == "alone":
                _comm_alone(self._phase(step[1], step[2]), "_".join((step[1], point)))
            elif step[0] == "start":
                self._start(step[1], step[2])
            elif step[0] == "prepare":
                self._prepare(step[1], step[2])
            elif step[0] == "gate":
                self.raw.update(zip(step[1], self.gate([self.raw[n] for n in step[1]])))
            elif step[0] == "wait":
                self._wait(step[1], step[2], tuple(after) + tuple(self.tokens[-1:]))
            else:
                for n in step[1]:
                    self.chip_sum[n] = _chip_sum(self.slots[n], self.from_sibling[n], self.core, "chip_sum_" + n)

    def weights(self, group):
        self._settle()
        f, d = self.full, self.d
        if group == "meta":
            g = self.partial["meta_tokens"]
            return g.transpose(1, 0, 2).reshape(g.shape[1], d)
        if group == "ffn1_landing":
            return tuple(self.started[("g1b", _FFN1)][2])
        if group == "ffn1":
            return tuple(f[n] for n in _FFN1)
        if group == "ffn2":
            return tuple(f[n] for n in _FFN2)
        n_main = self.d_in - self.n_heads
        win = f["w_in"].transpose(1, 0, 2).reshape(d, self.d_in)
        win = jnp.concatenate([win[:, :n_main], jnp.pad(win[:, n_main:], ((0, 0), (0, LANES - self.n_heads)))], axis=1)
        pw = f["pool_w"]
        gw = pw.shape[2]
        pw = pw.reshape(N_DEV, -1, gw // N_DEV, gw).transpose(1, 0, 2, 3).reshape(-1, gw, gw)
        return win, pw, f["w_out"].reshape(-1, d)

    def grad(self, name, g):
        d = self.d
        if name == "w_in":
            g = g[:, :self.d_in].reshape(d, N_DEV, -1).transpose(1, 0, 2)
        elif name == "pool_w":
            ng, gw = g.shape[0], g.shape[2]
            g = g.astype(BF).reshape(ng, N_DEV, -1, gw).transpose(1, 0, 2, 3).reshape(N_DEV, -1, gw)
        elif name == "w_out":
            g = g.reshape(N_DEV, -1, d)
        self.slots[name] = g.reshape((N_DEV // 2, 2) + g.shape[1:])

    def gradient_parts(self, name):
        self._settle()
        return self.received[name]


_TRANSPOSED = ("ffn1_w_gate", "ffn1_w_up", "ffn2_w_gate", "ffn2_w_up")


def _as2d(name, a):
    return a[0].T if name in _TRANSPOSED else a.reshape(-1, a.shape[-1])


def _from2d(name, a2d, shape):
    return a2d.T.reshape(shape) if name in _TRANSPOSED else a2d.reshape(shape)


def kernel(x, meta_tokens, ffn1_norm, ffn1_w_gate, ffn1_w_up, ffn1_w_down, mix_norm, w_in, b_forget, q_norm, k_norm, pool_w, pool_scale, w_out, ffn2_norm, ffn2_w_gate, ffn2_w_up, ffn2_w_down, loss_target, m_meta_tokens, m_ffn1_norm, m_ffn1_w_gate, m_ffn1_w_up, m_ffn1_w_down, m_mix_norm, m_w_in, m_b_forget, m_q_norm, m_k_norm, m_pool_w, m_pool_scale, m_w_out, m_ffn2_norm, m_ffn2_w_gate, m_ffn2_w_up, m_ffn2_w_down, v_meta_tokens, v_ffn1_norm, v_ffn1_w_gate, v_ffn1_w_up, v_ffn1_w_down, v_mix_norm, v_w_in, v_b_forget, v_q_norm, v_k_norm, v_pool_w, v_pool_scale, v_w_out, v_ffn2_norm, v_ffn2_w_gate, v_ffn2_w_up, v_ffn2_w_down):
    w = dict(meta_tokens=meta_tokens, ffn1_norm=ffn1_norm, ffn1_w_gate=ffn1_w_gate, ffn1_w_up=ffn1_w_up,
             ffn1_w_down=ffn1_w_down, mix_norm=mix_norm, w_in=w_in, b_forget=b_forget, q_norm=q_norm, k_norm=k_norm,
             pool_w=pool_w, pool_scale=pool_scale, w_out=w_out, ffn2_norm=ffn2_norm, ffn2_w_gate=ffn2_w_gate,
             ffn2_w_up=ffn2_w_up, ffn2_w_down=ffn2_w_down)
    m = dict(meta_tokens=m_meta_tokens, ffn1_norm=m_ffn1_norm, ffn1_w_gate=m_ffn1_w_gate, ffn1_w_up=m_ffn1_w_up,
             ffn1_w_down=m_ffn1_w_down, mix_norm=m_mix_norm, w_in=m_w_in, b_forget=m_b_forget, q_norm=m_q_norm,
             k_norm=m_k_norm, pool_w=m_pool_w, pool_scale=m_pool_scale, w_out=m_w_out, ffn2_norm=m_ffn2_norm,
             ffn2_w_gate=m_ffn2_w_gate, ffn2_w_up=m_ffn2_w_up, ffn2_w_down=m_ffn2_w_down)
    v = dict(meta_tokens=v_meta_tokens, ffn1_norm=v_ffn1_norm, ffn1_w_gate=v_ffn1_w_gate, ffn1_w_up=v_ffn1_w_up,
             ffn1_w_down=v_ffn1_w_down, mix_norm=v_mix_norm, w_in=v_w_in, b_forget=v_b_forget, q_norm=v_q_norm,
             k_norm=v_k_norm, pool_w=v_pool_w, pool_scale=v_pool_scale, w_out=v_w_out, ffn2_norm=v_ffn2_norm,
             ffn2_w_gate=v_ffn2_w_gate, ffn2_w_up=v_ffn2_w_up, ffn2_w_down=v_ffn2_w_down)

    d = x.shape[-1]
    n_heads = b_forget.shape[-1]
    pos = (lax.axis_index("x"), lax.axis_index("y"), lax.axis_index("c"))
    me = 4 * pos[0] + 2 * pos[1] + pos[2]

    raw = {k: _as2d(k, w[k]) for k in _BIG}
    raw["meta_tokens"] = meta_tokens
    plan = _MeshPlan(raw, pos, d, N_DEV * w_in.shape[-1], n_heads)
    sw = {k: w[k] for k in _SMALL}
    sw["b_forget"] = jnp.pad(b_forget, ((0, 0), (0, LANES - n_heads)))
    dx, dmeta, small = _local_step(x[0], loss_target[0], sw, plan)

    res = {}
    last = dx

    plan.start_small_gather([jnp.concatenate(small, axis=1), dmeta])
    plan.at("after_ffn1_rms_bwd")

    def update_shards(names):
        nonlocal last
        for k in names:
            plan.at("before_adamw_" + k, (last,))
            state = [pltpu.with_memory_space_constraint(_as2d(k, t[k]), pltpu.HBM) for t in (w, m, v)]
            res[k] = _adamw(plan.gradient_parts(k), *state, "adamw_" + k, plan.comm("adamw_" + k))
            last = res[k][0]

    update_shards(_FFN2 + _MIX + ("ffn1_w_gate", "ffn1_w_up"))

    pack_g, meta_g = plan.wait_small_gather((last,))
    tot, dmeta_tot, loss_row = _small_reduce(pack_g, meta_g, 0.5 / d, "small_reduce")

    mcols = meta_tokens.shape[1]
    g_meta = lax.dynamic_slice_in_dim(dmeta_tot, me * mcols, mcols, axis=1)
    res["meta_tokens"] = _adamw(g_meta, meta_tokens, m_meta_tokens, v_meta_tokens, "adamw_meta_tokens")

    def packed(src):
        return jnp.concatenate([src[k] for k in _SMALL[:-1]] + [jnp.pad(src["b_forget"], ((0, 0), (0, LANES - n_heads)))],
                               axis=1)

    wp = packed(w)
    sm = _adamw(tot[:, :wp.shape[1]], wp, packed(m), packed(v), "adamw_small")
    off = 0
    for k in _SMALL:
        width = w[k].shape[1]
        res[k] = tuple(o[:, off:off + width] for o in sm)
        off += width if k != "b_forget" else LANES

    last = sm[0]
    update_shards(("ffn1_w_down",))

    outs =[loss_row[0, 0], dx[None]]
    for idx in range(4):
        outs += [_from2d(k, res[k][idx], w[k].shape) for k in _ORDER]
    return tuple(outs)
```

```python
import functools

import jax
import jax.numpy as jnp
from jax import lax
from jax.experimental import pallas as pl
from jax.experimental.pallas import tpu as pltpu

F32 = jnp.float32
BF = jnp.bfloat16
SDS = jax.ShapeDtypeStruct

N_DEV = 8
LANES = 128
SUBLANES = 8
HEAD_DIM = 128
POOL_WINDOWS = (2, 4, 8, 16)
RMS_EPS = 1e-6
NEG_BIG = -1e30
MIB = 1024 * 1024

ADAM_LR = 0.001
ADAM_B1 = 0.9
ADAM_B2 = 0.999
ADAM_EPS = 1e-08
ADAM_WD = 0.01
ADAM_STEP = 10


class _Comm:
    def __init__(self, arrs, out_shape, sems, start, finish, aliases=None):
        self.arrs, self.out_shape, self.sems = list(arrs), list(out_shape), list(sems)
        self.start, self.finish, self.aliases = start, finish, dict(aliases or {})
        self.results = None


def _merge_comm(ops):
    ops = [op for op in ops if op is not None]
    if not ops:
        return None
    na, no, ns = [0], [0], [0]
    for op in ops:
        na.append(na[-1] + len(op.arrs))
        no.append(no[-1] + len(op.out_shape))
        ns.append(ns[-1] + len(op.sems))

    def parts(i, ins, outs, sems):
        return ins[na[i]:na[i + 1]], outs[no[i]:no[i + 1]], sems[ns[i]:ns[i + 1]]

    def start(ins, outs, sems):
        for i, op in enumerate(ops):
            op.start(*parts(i, ins, outs, sems))

    def finish(ins, outs, sems):
        for i, op in enumerate(ops):
            op.finish(*parts(i, ins, outs, sems))

    aliases = {}
    for i, op in enumerate(ops):
        for a, o in op.aliases.items():
            aliases[na[i] + a] = no[i] + o
    merged = _Comm([a for op in ops for a in op.arrs], [s for op in ops for s in op.out_shape],
                   [s for op in ops for s in op.sems], start, finish, aliases)
    merged.children = (ops, no)
    return merged


def _deliver(comm, results):
    comm.results = list(results)
    if hasattr(comm, "children"):
        ops, no = comm.children
        for i, op in enumerate(ops):
            _deliver(op, results[no[i]:no[i + 1]])


def _call(body, *, grid, in_specs, out_specs, out_shape, scratch_shapes=(), vmem_mib, name, comm=None):
    single = not isinstance(out_shape, (list, tuple))
    out_specs = [out_specs] if single else list(out_specs)
    out_shape = [out_shape] if single else list(out_shape)
    in_specs, scratch_shapes = list(in_specs), list(scratch_shapes)
    params = pltpu.CompilerParams(dimension_semantics=("arbitrary",) * len(grid), vmem_limit_bytes=vmem_mib * MIB)
    n_in, n_out, n_scr = len(in_specs), len(out_specs), len(scratch_shapes)

    def run(*args):
        if comm is None:
            res = pl.pallas_call(body, grid=grid, in_specs=in_specs, out_specs=out_specs, out_shape=out_shape,
                                 scratch_shapes=scratch_shapes, compiler_params=params, name=name)(*args)
            return res[0] if single else res
        ci, co = len(comm.arrs), len(comm.out_shape)

        def with_comm(*refs):
            ins, cins = refs[:n_in], refs[n_in:n_in + ci]
            o0 = n_in + ci
            outs, couts = refs[o0:o0 + n_out], refs[o0 + n_out:o0 + n_out + co]
            s0 = o0 + n_out + co
            scr, csems = refs[s0:s0 + n_scr], refs[s0 + n_scr:]
            ids = [pl.program_id(a) for a in range(len(grid))]
            first = functools.reduce(jnp.logical_and, [i == 0 for i in ids])
            last = functools.reduce(jnp.logical_and, [i == g - 1 for i, g in zip(ids, grid)])

            @pl.when(first)
            def _():
                comm.start(cins, couts, csems)

            body(*ins, *outs, *scr)

            @pl.when(last)
            def _():
                comm.finish(cins, couts, csems)

        anyspec = pl.BlockSpec(memory_space=pl.ANY)
        res = pl.pallas_call(
            with_comm, grid=grid, in_specs=in_specs + [anyspec] * ci, out_specs=out_specs + [anyspec] * co,
            out_shape=out_shape + comm.out_shape, scratch_shapes=scratch_shapes + comm.sems,
            input_output_aliases={n_in + a: n_out + o for a, o in comm.aliases.items()},
            compiler_params=params, name=name)(*args, *comm.arrs)
        _deliver(comm, res[n_out:])
        return res[0] if single else res[:n_out]

    return run


def _comm_alone(comm, name):
    def body(*refs):
        ci, co = len(comm.arrs), len(comm.out_shape)
        ins, outs, sems = refs[:ci], refs[ci:ci + co], refs[ci + co:]
        comm.start(ins, outs, sems)
        comm.finish(ins, outs, sems)

    anyspec = pl.BlockSpec(memory_space=pl.ANY)
    res = pl.pallas_call(
        body, in_specs=[anyspec] * len(comm.arrs), out_specs=[anyspec] * len(comm.out_shape),
        out_shape=comm.out_shape, scratch_shapes=comm.sems, input_output_aliases=comm.aliases, name=name)(*comm.arrs)
    _deliver(comm, res)


def _split_start(comm, name):
    na, ns = len(comm.arrs), len(comm.sems)

    def body(*refs):
        comm.start(refs[:na], None, refs[na:na + ns])
        token = refs[-1]
        token[...] = jnp.zeros_like(token)

    hbm = pl.BlockSpec(memory_space=pltpu.HBM)
    res = pl.pallas_call(
        body, name=name,
        out_shape=tuple(comm.sems) + tuple(pltpu.HBM(a.shape, a.dtype) for a in comm.arrs)
        + (SDS((SUBLANES, LANES), F32),),
        in_specs=[hbm] * na,
        out_specs=[pl.BlockSpec(memory_space=pltpu.SEMAPHORE)] * ns + [hbm] * na + [pl.BlockSpec(memory_space=pltpu.VMEM)],
        input_output_aliases={i: ns + i for i in range(na)},
        compiler_params=pltpu.CompilerParams(has_side_effects=pltpu.SideEffectType.DATAFLOW_SIDE_EFFECTING),
    )(*[pltpu.with_memory_space_constraint(a, pltpu.HBM) for a in comm.arrs])
    return (comm, res[:ns], res[ns:ns + na]), res[-1]


def _split_wait(started, afters, name):
    comm, sems, thru = started
    na, ns = len(thru), len(sems)
    afters = list(afters)

    def body(*refs):
        comm.finish(refs[:na], None, refs[na:na + ns])

    hbm = pl.BlockSpec(memory_space=pltpu.HBM)
    res = pl.pallas_call(
        body, name=name, out_shape=tuple(pltpu.HBM(a.shape, a.dtype) for a in thru),
        in_specs=[hbm] * na + [pl.BlockSpec(memory_space=pltpu.SEMAPHORE)] * ns
        + [pl.BlockSpec(memory_space=pl.ANY)] * len(afters),
        out_specs=[hbm] * na, input_output_aliases={i: i for i in range(na)},
        compiler_params=pltpu.CompilerParams(has_side_effects=pltpu.SideEffectType.DATAFLOW_SIDE_EFFECTING),
    )(*thru, *sems, *afters)
    return res[na - len(comm.out_shape):]


def _largest_tile(n, cap, mult):
    if n <= cap:
        return n
    best = None
    for t in range(mult, cap + 1, mult):
        if n % t == 0:
            best = t
    assert best is not None, (n, cap, mult)
    return best


def _dot(a, b):
    return jnp.dot(a, b, preferred_element_type=F32)


def _dot_nt(a, b):
    return lax.dot_general(a, b, (((1,), (1,)), ((), ())), preferred_element_type=F32)


def _dot_tn(a, b):
    return lax.dot_general(a, b, (((0,), (0,)), ((), ())), preferred_element_type=F32)


def _rows8(x):
    t, c = x.shape
    return jnp.sum(x.reshape(t // SUBLANES, SUBLANES, c), axis=0)


def _rstd(x):
    return lax.rsqrt(jnp.mean(x * x, axis=-1, keepdims=True) + RMS_EPS)


def _ffn_fwd(h, g, wg, wu, wd, tm, name, comm=None):
    lp, d = h.shape
    ns, fs, _ = wg.shape

    def body(h_ref, g_ref, wg_ref, wu_ref, wd_ref, out_ref, a_ref, b_ref, u_ref, acc_ref):
        j = pl.program_id(1)

        @pl.when(j == 0)
        def _():
            hh = h_ref[...]
            u_ref[...] = (hh * _rstd(hh) * g_ref[...]).astype(BF)
            acc_ref[...] = jnp.zeros_like(acc_ref)

        u = u_ref[...]
        a = _dot_nt(u, wg_ref[...])
        b = _dot_nt(u, wu_ref[...])
        a_ref[...] = a.astype(BF)
        b_ref[...] = b.astype(BF)
        hid = (a * jax.nn.sigmoid(a) * b).astype(BF)
        acc_ref[...] += _dot(hid, wd_ref[...])

        @pl.when(j == ns - 1)
        def _():
            out_ref[...] = h_ref[...] + 0.5 * acc_ref[...]

    row = pl.BlockSpec((tm, d), lambda i, j: (i, 0))
    act = pl.BlockSpec((None, tm, fs), lambda i, j: (j, i, 0))
    return _call(
        body, grid=(lp // tm, ns),
        in_specs=[row, pl.BlockSpec((1, d), lambda i, j: (0, 0)),
                  pl.BlockSpec((None, fs, d), lambda i, j: (j, 0, 0)),
                  pl.BlockSpec((None, fs, d), lambda i, j: (j, 0, 0)),
                  pl.BlockSpec((None, fs, d), lambda i, j: (j, 0, 0))],
        out_specs=[row, act, act, row],
        out_shape=[SDS((lp, d), F32), SDS((ns, lp, fs), BF), SDS((ns, lp, fs), BF), SDS((lp, d), BF)],
        scratch_shapes=[pltpu.VMEM((tm, d), F32)],
        vmem_mib=56, name=name, comm=comm)(h, g, wg, wu, wd)


def _ffn_fwd_part(h, g, wg, wu, wd, order, carry, tm, name, deps=()):
    lp, d = h.shape
    fs = wg.shape[1]
    k = order.shape[0]
    first = carry is None
    n_in = 5 if first else 8

    def body(order_ref, *refs):
        outs = refs[n_in + len(deps):]
        if first:
            h_ref, g_ref, wg_ref, wu_ref, wd_ref = refs[:n_in]
            out_ref, a_ref, b_ref, u_ref, acc_ref = outs
        else:
            h_ref, acc_in_ref, u_ref, _, _, wg_ref, wu_ref, wd_ref = refs[:n_in]
            out_ref, a_ref, b_ref, acc_ref = outs
        j = pl.program_id(1)

        @pl.when(j == 0)
        def _():
            if first:
                hh = h_ref[...]
                u_ref[...] = (hh * _rstd(hh) * g_ref[...]).astype(BF)
                acc_ref[...] = jnp.zeros_like(acc_ref)
            else:
                acc_ref[...] = acc_in_ref[...]

        u = u_ref[...]
        a = _dot_nt(u, wg_ref[...])
        b = _dot_nt(u, wu_ref[...])
        a_ref[...] = a.astype(BF)
        b_ref[...] = b.astype(BF)
        hid = (a * jax.nn.sigmoid(a) * b).astype(BF)
        acc_ref[...] += _dot(hid, wd_ref[...])

        @pl.when(j == k - 1)
        def _():
            out_ref[...] = acc_ref[...] if first else h_ref[...] + 0.5 * acc_ref[...]

    row = pl.BlockSpec((tm, d), lambda i, j, o: (i, 0))
    act = pl.BlockSpec((None, tm, fs), lambda i, j, o: (o[j], i, 0))
    wsp = pl.BlockSpec((None, fs, d), lambda i, j, o: (o[j], 0, 0))
    anyspec = pl.BlockSpec(memory_space=pl.ANY)
    acts = [SDS((wg.shape[0], lp, fs), BF)] * 2
    if first:
        in_specs = [row, pl.BlockSpec((1, d), lambda i, j, o: (0, 0)), wsp, wsp, wsp]
        out_specs, out_shape = [row, act, act, row], [SDS((lp, d), F32)] + acts + [SDS((lp, d), BF)]
        args, aliases = (h, g, wg, wu, wd), {}
    else:
        acc, a_prev, b_prev, u_prev = carry
        in_specs = [row, row, row, anyspec, anyspec, wsp, wsp, wsp]
        out_specs, out_shape = [row, act, act], [SDS((lp, d), F32)] + acts
        args, aliases = (h, acc, u_prev, a_prev, b_prev, wg, wu, wd), {4: 1, 5: 2}
    return pl.pallas_call(
        body,
        grid_spec=pltpu.PrefetchScalarGridSpec(
            num_scalar_prefetch=1, grid=(lp // tm, k), in_specs=in_specs + [anyspec] * len(deps),
            out_specs=out_specs, scratch_shapes=[pltpu.VMEM((tm, d), F32)]),
        out_shape=out_shape, input_output_aliases=aliases,
        compiler_params=pltpu.CompilerParams(dimension_semantics=("arbitrary",) * 2, vmem_limit_bytes=60 * MIB),
        name=name)(order, *args, *deps)


def _ffn_bwd_act(dob, a, b, wd, tm, name, comm=None):
    lp, d = dob.shape
    ns, fs, _ = wd.shape
    row = pl.BlockSpec((tm, d), lambda i, j: (i, 0))
    act = pl.BlockSpec((None, tm, fs), lambda i, j: (j, i, 0))
    wsp = pl.BlockSpec((None, fs, d), lambda i, j: (j, 0, 0))

    def act_body(do_ref, a_ref, b_ref, wd_ref, da_ref, db_ref, hid_ref):
        dhid = _dot_nt(do_ref[...], wd_ref[...])
        av = a_ref[...].astype(F32)
        bv = b_ref[...].astype(F32)
        sig = jax.nn.sigmoid(av)
        sil = av * sig
        hid_ref[...] = (sil * bv).astype(BF)
        da_ref[...] = (dhid * bv * (sig * (1.0 + av * (1.0 - sig)))).astype(BF)
        db_ref[...] = (dhid * sil).astype(BF)

    return _call(
        act_body, grid=(lp // tm, ns), in_specs=[row, act, act, wsp], out_specs=[act, act, act],
        out_shape=[SDS((ns, lp, fs), BF)] * 3, vmem_mib=40, name=name, comm=comm)(dob, a, b, wd)


def _ffn_bwd_du(da, db, wg, wu, tm, name, comm=None):
    ns, lp, fs = da.shape
    d = wg.shape[2]
    row = pl.BlockSpec((tm, d), lambda i, j: (i, 0))
    act = pl.BlockSpec((None, tm, fs), lambda i, j: (j, i, 0))
    wsp = pl.BlockSpec((None, fs, d), lambda i, j: (j, 0, 0))

    def du_body(da_ref, db_ref, wg_ref, wu_ref, du_ref):
        @pl.when(pl.program_id(1) == 0)
        def _():
            du_ref[...] = jnp.zeros_like(du_ref)

        du_ref[...] += _dot(da_ref[...], wg_ref[...]) + _dot(db_ref[...], wu_ref[...])

    return _call(
        du_body, grid=(lp // tm, ns), in_specs=[act, act, wsp, wsp], out_specs=row,
        out_shape=SDS((lp, d), F32), vmem_mib=48, name=name, comm=comm)(da, db, wg, wu)


def _rms_bwd(du, h, g, dres, bscale, tm, name, comm=None):
    lp, d = h.shape

    def body(du_ref, h_ref, g_ref, dres_ref, dh_ref, dhb_ref, dg_ref):
        @pl.when(pl.program_id(0) == 0)
        def _():
            dg_ref[...] = jnp.zeros_like(dg_ref)

        hh = h_ref[...]
        r = _rstd(hh)
        xhat = hh * r
        duv = du_ref[...]
        dg_ref[...] += _rows8(duv * xhat)
        dxh = duv * g_ref[...]
        dh = dres_ref[...] + r * (dxh - xhat * jnp.mean(dxh * xhat, axis=-1, keepdims=True))
        dh_ref[...] = dh
        dhb_ref[...] = (bscale * dh).astype(BF)

    row = pl.BlockSpec((tm, d), lambda i: (i, 0))
    return _call(
        body, grid=(lp // tm,),
        in_specs=[row, row, pl.BlockSpec((1, d), lambda i: (0, 0)), row],
        out_specs=[row, row, pl.BlockSpec((SUBLANES, d), lambda i: (0, 0))],
        out_shape=[SDS((lp, d), F32), SDS((lp, d), BF), SDS((SUBLANES, d), F32)],
        vmem_mib=48, name=name, comm=comm)(du, h, g, dres)


def _matmul_tn(a, b, tm, tn, name, comm=None):
    a_b, b_b = a.ndim == 3, b.ndim == 3
    ns = a.shape[0] if a_b else (b.shape[0] if b_b else 1)
    l, m = a.shape[-2:]
    n = b.shape[-1]

    def body(a_ref, b_ref, o_ref):
        o_ref[...] = _dot_tn(a_ref[...], b_ref[...]).astype(o_ref.dtype)

    a_spec = (pl.BlockSpec((None, l, tm), lambda s, i, j: (s, 0, i)) if a_b
              else pl.BlockSpec((l, tm), lambda s, i, j: (0, i)))
    b_spec = (pl.BlockSpec((None, l, tn), lambda s, i, j: (s, 0, j)) if b_b
              else pl.BlockSpec((l, tn), lambda s, i, j: (0, j)))
    batched = a_b or b_b
    o_spec = (pl.BlockSpec((None, tm, tn), lambda s, i, j: (s, i, j)) if batched
              else pl.BlockSpec((tm, tn), lambda s, i, j: (i, j)))
    o_shape = SDS((ns, m, n), BF) if batched else SDS((m, n), BF)
    return _call(
        body, grid=(ns, m // tm, n // tn), in_specs=[a_spec, b_spec], out_specs=o_spec, out_shape=o_shape,
        vmem_mib=48, name=name, comm=comm)(a, b)


def _matmul_nt(x, w, tm, tk, out_dtype, name, comm=None):
    l, k = x.shape
    n = w.shape[0]
    nk = k // tk

    def body(x_ref, w_ref, o_ref, acc_ref):
        kk = pl.program_id(1)

        @pl.when(kk == 0)
        def _():
            acc_ref[...] = jnp.zeros_like(acc_ref)

        acc_ref[...] += _dot_nt(x_ref[...], w_ref[...])

        @pl.when(kk == nk - 1)
        def _():
            o_ref[...] = acc_ref[...].astype(o_ref.dtype)

    return _call(
        body, grid=(l // tm, nk),
        in_specs=[pl.BlockSpec((tm, tk), lambda i, kk: (i, kk)), pl.BlockSpec((n, tk), lambda i, kk: (0, kk))],
        out_specs=pl.BlockSpec((tm, n), lambda i, kk: (i, 0)),
        out_shape=SDS((l, n), out_dtype),
        scratch_shapes=[pltpu.VMEM((tm, n), F32)],
        vmem_mib=48, name=name, comm=comm)(x, w)


def _norm_matmul(h, g, w, tm, tn, name, comm=None):
    lp, d = h.shape
    n = w.shape[1]

    def body(h_ref, g_ref, w_ref, z_ref, u_ref):
        @pl.when(pl.program_id(1) == 0)
        def _():
            hh = h_ref[...]
            u_ref[...] = (hh * _rstd(hh) * g_ref[...]).astype(BF)

        z_ref[...] = _dot(u_ref[...], w_ref[...])

    row = pl.BlockSpec((tm, d), lambda i, j: (i, 0))
    return _call(
        body, grid=(lp // tm, n // tn),
        in_specs=[row, pl.BlockSpec((1, d), lambda i, j: (0, 0)), pl.BlockSpec((d, tn), lambda i, j: (0, j))],
        out_specs=[pl.BlockSpec((tm, tn), lambda i, j: (i, j)), row],
        out_shape=[SDS((lp, n), F32), SDS((lp, d), BF)],
        vmem_mib=48, name=name, comm=comm)(h, g, w)


def _out_proj(h, pool_o, att_o, w_out, tm, name, comm=None):
    lp, d = h.shape
    p = pool_o.shape[1]
    dm = w_out.shape[0]

    def body(h_ref, p_ref, a_ref, w_ref, o_ref):
        o_ref[...] = h_ref[...] + _dot(p_ref[...], w_ref[0:p, :]) + _dot(a_ref[...], w_ref[p:dm, :])

    row = pl.BlockSpec((tm, d), lambda i: (i, 0))
    return _call(
        body, grid=(lp // tm,),
        in_specs=[row, pl.BlockSpec((tm, p), lambda i: (i, 0)), pl.BlockSpec((tm, dm - p), lambda i: (i, 0)),
                  pl.BlockSpec((dm, d), lambda i: (0, 0))],
        out_specs=row, out_shape=SDS((lp, d), F32),
        vmem_mib=48, name=name, comm=comm)(h, pool_o, att_o, w_out)


def _loss_head(y, tpad, row0, row1, tm, name, comm=None):
    lp, d = y.shape

    def body(y_ref, t_ref, dy_ref, dob_ref, ls_ref):
        i = pl.program_id(0)

        @pl.when(i == 0)
        def _():
            ls_ref[...] = jnp.zeros_like(ls_ref)

        rows = i * tm + lax.broadcasted_iota(jnp.int32, (tm, d), 0)
        err = jnp.where((rows >= row0) & (rows < row1), y_ref[...] - t_ref[...], 0.0)
        dy = err * (1.0 / d)
        dy_ref[...] = dy
        dob_ref[...] = (0.5 * dy).astype(BF)
        sq = _rows8(err * err)
        acc = sq[:, 0:LANES]
        for c in range(1, d // LANES):
            acc = acc + sq[:, c * LANES:(c + 1) * LANES]
        ls_ref[...] += acc

    row = pl.BlockSpec((tm, d), lambda i: (i, 0))
    return _call(
        body, grid=(lp // tm,), in_specs=[row, row],
        out_specs=[row, row, pl.BlockSpec((SUBLANES, LANES), lambda i: (0, 0))],
        out_shape=[SDS((lp, d), F32), SDS((lp, d), BF), SDS((SUBLANES, LANES), F32)],
        vmem_mib=48, name=name, comm=comm)(y, tpad)


def _window_select(levels, gidx):
    out = levels[-1]
    for k in range(len(levels) - 2, -1, -1):
        out = jnp.where(gidx == k, levels[k], out)
    return out


def _pool_window_mean_minus_id(x, gidx):
    rows = lax.broadcasted_iota(jnp.int32, x.shape, 0)
    levels = []
    s = x
    shift = 1
    while shift < POOL_WINDOWS[-1]:
        s = s + jnp.where(rows >= shift, pltpu.roll(s, shift, 0), 0.0)
        shift *= 2
        if shift in POOL_WINDOWS:
            levels.append(s)
    win = _window_select(levels, gidx)
    cnt = jnp.minimum(rows + 1, _window_select(list(POOL_WINDOWS), gidx)).astype(F32)
    return win / cnt - x, cnt


def _pool_window_transpose(dy, cnt, gidx):
    lp = dy.shape[0]
    rows = lax.broadcasted_iota(jnp.int32, dy.shape, 0)
    levels = []
    s = dy / cnt
    shift = 1
    while shift < POOL_WINDOWS[-1]:
        s = s + jnp.where(rows < lp - shift, pltpu.roll(s, lp - shift, 0), 0.0)
        shift *= 2
        if shift in POOL_WINDOWS:
            levels.append(s)
    return _window_select(levels, gidx) - dy


def _pool_fwd(z, pool_w, pool_scale, name, comm=None):
    lp = z.shape[0]
    ng, gw, _ = pool_w.shape

    def body(p_ref, w_ref, s_ref, o_ref):
        pooled, _ = _pool_window_mean_minus_id(p_ref[...], pl.program_id(0))
        o_ref[...] = (_dot(pooled.astype(BF), w_ref[...]) * s_ref[...]).astype(BF)

    return _call(
        body, grid=(ng,),
        in_specs=[pl.BlockSpec((lp, gw), lambda g: (0, g)), pl.BlockSpec((None, gw, gw), lambda g: (g, 0, 0)),
                  pl.BlockSpec((1, gw), lambda g: (0, g))],
        out_specs=pl.BlockSpec((lp, gw), lambda g: (0, g)), out_shape=SDS((lp, ng * gw), BF),
        vmem_mib=48, name=name, comm=comm)(z, pool_w, pool_scale)


def _pool_bwd(z, dmix, pool_w, pool_scale, name, comm=None):
    lp = z.shape[0]
    ng, gw, _ = pool_w.shape

    def body(p_ref, d_ref, w_ref, s_ref, dz_ref, dw_ref, ds_ref):
        g = pl.program_id(0)
        pooled, cnt = _pool_window_mean_minus_id(p_ref[...], g)
        pooled_b = pooled.astype(BF)
        w = w_ref[...]
        mixed = _dot(pooled_b, w)
        dpo = d_ref[...].astype(F32)
        ds_ref[...] = _rows8(dpo * mixed)
        dmixed = (dpo * s_ref[...]).astype(BF)
        dw_ref[...] = _dot_tn(pooled_b, dmixed)
        dpooled = _dot_nt(dmixed, w)
        dz_ref[...] = _pool_window_transpose(dpooled, cnt, g).astype(BF)

    return _call(
        body, grid=(ng,),
        in_specs=[pl.BlockSpec((lp, gw), lambda g: (0, g)), pl.BlockSpec((lp, gw), lambda g: (0, g)),
                  pl.BlockSpec((None, gw, gw), lambda g: (g, 0, 0)), pl.BlockSpec((1, gw), lambda g: (0, g))],
        out_specs=[pl.BlockSpec((lp, gw), lambda g: (0, g)), pl.BlockSpec((None, gw, gw), lambda g: (g, 0, 0)),
                   pl.BlockSpec((SUBLANES, gw), lambda g: (0, g))],
        out_shape=[SDS((lp, ng * gw), BF), SDS((ng, gw, gw), F32), SDS((SUBLANES, ng * gw), F32)],
        vmem_mib=48, name=name, comm=comm)(z, dmix, pool_w, pool_scale)


def _log_sigmoid(x):
    return jnp.minimum(x, 0.0) - jnp.log(1.0 + jnp.exp(-jnp.abs(x)))


def _fox_prep(z, bfp, fblk, name, comm=None):
    lp = z.shape[0]
    nb = lp // LANES

    def body(f_ref, b_ref, cum_ref):
        r = lax.broadcasted_iota(jnp.int32, (LANES, LANES), 0)
        c = lax.broadcasted_iota(jnp.int32, (LANES, LANES), 1)
        tri = (r >= c).astype(F32)
        carry = jnp.zeros((1, LANES), F32)
        for blk in range(nb):
            sl = slice(blk * LANES, (blk + 1) * LANES)
            lf = _log_sigmoid(f_ref[sl, :] + b_ref[...])
            cb = jnp.dot(tri, lf, preferred_element_type=F32, precision=lax.Precision.HIGHEST) + carry
            cum_ref[sl, :] = cb
            carry = cb[LANES - 1:LANES, :]

    return _call(
        body, grid=(1,),
        in_specs=[pl.BlockSpec((lp, LANES), lambda i: (0, fblk)), pl.BlockSpec((1, LANES), lambda i: (0, 0))],
        out_specs=pl.BlockSpec((lp, LANES), lambda i: (0, 0)), out_shape=SDS((lp, LANES), F32),
        vmem_mib=32, name=name, comm=comm)(z, bfp)


def _fox_bwd(z, bfp, dcum, fblk, name, comm=None):
    lp = z.shape[0]
    nb = lp // LANES

    def body(f_ref, b_ref, dc_ref, dz_ref, db_ref):
        r = lax.broadcasted_iota(jnp.int32, (LANES, LANES), 0)
        c = lax.broadcasted_iota(jnp.int32, (LANES, LANES), 1)
        tri = (r <= c).astype(F32)
        carry = jnp.zeros((1, LANES), F32)
        acc = jnp.zeros((SUBLANES, LANES), F32)
        for blk in range(nb - 1, -1, -1):
            sl = slice(blk * LANES, (blk + 1) * LANES)
            dlf = jnp.dot(tri, dc_ref[sl, :], preferred_element_type=F32, precision=lax.Precision.HIGHEST) + carry
            carry = dlf[0:1, :]
            df = dlf * jax.nn.sigmoid(-(f_ref[sl, :] + b_ref[...]))
            dz_ref[sl, :] = df.astype(BF)
            acc = acc + _rows8(df)
        db_ref[...] = acc

    return _call(
        body, grid=(1,),
        in_specs=[pl.BlockSpec((lp, LANES), lambda i: (0, fblk)), pl.BlockSpec((1, LANES), lambda i: (0, 0)),
                  pl.BlockSpec((lp, LANES), lambda i: (0, 0))],
        out_specs=[pl.BlockSpec((lp, LANES), lambda i: (0, 0)), pl.BlockSpec((SUBLANES, LANES), lambda i: (0, 0))],
        out_shape=[SDS((lp, LANES), BF), SDS((SUBLANES, LANES), F32)],
        vmem_mib=32, name=name, comm=comm)(z, bfp, dcum)


def _att_scores(q_ref, cum_ref, cumt_ref, qw_ref, kn_s, h, i, tq, lk):
    scale = 1.0 / (HEAD_DIM ** 0.5)
    q = q_ref[...]
    rq = _rstd(q)
    qhat = q * rq
    qn = (qhat * qw_ref[...]).astype(BF)
    s = _dot_nt(qn, kn_s[0:lk, :]) * scale
    lane = lax.broadcasted_iota(jnp.int32, (tq, LANES), 1)
    cq = jnp.sum(jnp.where(lane == h, cum_ref[...], 0.0), axis=1, keepdims=True)
    ck = cumt_ref[pl.ds(h, 1), 0:lk]
    s = s + (cq - ck)
    qpos = i * tq + lax.broadcasted_iota(jnp.int32, (tq, lk), 0)
    kpos = lax.broadcasted_iota(jnp.int32, (tq, lk), 1)
    s = jnp.where(qpos >= kpos, s, NEG_BIG)
    e = jnp.exp(s - jnp.max(s, axis=1, keepdims=True))
    p = e * (1.0 / jnp.sum(e, axis=1, keepdims=True))
    return p, qn, qhat, rq


def _per_query_tile(i, nq, tq, lp, fn):
    for t in range(nq):
        lk = min(lp, -(-((t + 1) * tq) // LANES) * LANES)
        pl.when(i == t)(functools.partial(fn, lk))


def _att_fwd(z, cum, cumt, qw, kw, n_heads, qblk0, tq, name, comm=None):
    lp = z.shape[0]
    nh = n_heads

    def body(q_ref, k_ref, v_ref, cum_ref, cumt_ref, qw_ref, kw_ref, o_ref, kn_s, vb_s):
        h, i = pl.program_id(0), pl.program_id(1)

        @pl.when(i == 0)
        def _():
            k = k_ref[...]
            kn_s[...] = (k * _rstd(k) * kw_ref[...]).astype(BF)
            vb_s[...] = v_ref[...].astype(BF)

        def tile(lk):
            p, _, _, _ = _att_scores(q_ref, cum_ref, cumt_ref, qw_ref, kn_s, h, i, tq, lk)
            o_ref[...] = _dot(p.astype(BF), vb_s[0:lk, :]).astype(BF)

        _per_query_tile(i, lp // tq, tq, lp, tile)

    vec = pl.BlockSpec((1, HEAD_DIM), lambda h, i: (0, 0))
    return _call(
        body, grid=(nh, lp // tq),
        in_specs=[pl.BlockSpec((tq, HEAD_DIM), lambda h, i: (i, qblk0 + h)),
                  pl.BlockSpec((lp, HEAD_DIM), lambda h, i: (0, qblk0 + nh + h)),
                  pl.BlockSpec((lp, HEAD_DIM), lambda h, i: (0, qblk0 + 2 * nh + h)),
                  pl.BlockSpec((tq, LANES), lambda h, i: (i, 0)),
                  pl.BlockSpec((nh, lp), lambda h, i: (0, 0)), vec, vec],
        out_specs=pl.BlockSpec((tq, HEAD_DIM), lambda h, i: (i, h)),
        out_shape=SDS((lp, nh * HEAD_DIM), BF),
        scratch_shapes=[pltpu.VMEM((lp, HEAD_DIM), BF), pltpu.VMEM((lp, HEAD_DIM), BF)],
        vmem_mib=48, name=name, comm=comm)(z, z, z, cum, cumt, qw, kw)


def _att_bwd(z, cum, cumt, qw, kw, dmix, n_heads, qblk0, oblk0, tq, name, comm=None):
    lp = z.shape[0]
    nh = n_heads
    nq = lp // tq
    scale = 1.0 / (HEAD_DIM ** 0.5)

    def body(q_ref, k_ref, v_ref, cum_ref, cumt_ref, qw_ref, kw_ref, do_ref,
             dq_ref, dk_ref, dv_ref, dck_ref, dqw_ref, dkw_ref,
             kn_s, vb_s, dkn_s, dv_s, dck_s):
        h, i = pl.program_id(0), pl.program_id(1)

        @pl.when((h == 0) & (i == 0))
        def _():
            dqw_ref[...] = jnp.zeros_like(dqw_ref)
            dkw_ref[...] = jnp.zeros_like(dkw_ref)

        @pl.when(i == 0)
        def _():
            k = k_ref[...]
            kn_s[...] = (k * _rstd(k) * kw_ref[...]).astype(BF)
            vb_s[...] = v_ref[...].astype(BF)
            dkn_s[...] = jnp.zeros_like(dkn_s)
            dv_s[...] = jnp.zeros_like(dv_s)
            dck_s[...] = jnp.zeros_like(dck_s)

        def tile(lk):
            p, qn, qhat, rq = _att_scores(q_ref, cum_ref, cumt_ref, qw_ref, kn_s, h, i, tq, lk)
            dob = do_ref[...]
            dp = _dot_nt(dob, vb_s[0:lk, :])
            ds = p * (dp - jnp.sum(p * dp, axis=1, keepdims=True))
            dsb = ds.astype(BF)
            dv_s[0:lk, :] += _dot_tn(p.astype(BF), dob)
            dkn_s[0:lk, :] += _dot_tn(dsb, qn)
            dck_s[:, 0:lk] += jnp.sum(ds, axis=0, keepdims=True)
            dqn = _dot(dsb, kn_s[0:lk, :]) * scale
            gq = dqn * qw_ref[...]
            dq_ref[...] = (rq * (gq - qhat * jnp.mean(gq * qhat, axis=-1, keepdims=True))).astype(BF)
            dqw_ref[...] += _rows8(dqn * qhat)

        _per_query_tile(i, nq, tq, lp, tile)

        @pl.when(i == nq - 1)
        def _():
            k = k_ref[...]
            rk = _rstd(k)
            khat = k * rk
            dkn = dkn_s[...] * scale
            gk = dkn * kw_ref[...]
            dk_ref[...] = (rk * (gk - khat * jnp.mean(gk * khat, axis=-1, keepdims=True))).astype(BF)
            dkw_ref[...] += _rows8(dkn * khat)
            dv_ref[...] = dv_s[...].astype(BF)
            dck_ref[...] = dck_s[...]

    vec = pl.BlockSpec((1, HEAD_DIM), lambda h, i: (0, 0))
    part = pl.BlockSpec((SUBLANES, LANES), lambda h, i: (0, 0))
    return _call(
        body, grid=(nh, nq),
        in_specs=[pl.BlockSpec((tq, HEAD_DIM), lambda h, i: (i, qblk0 + h)),
                  pl.BlockSpec((lp, HEAD_DIM), lambda h, i: (0, qblk0 + nh + h)),
                  pl.BlockSpec((lp, HEAD_DIM), lambda h, i: (0, qblk0 + 2 * nh + h)),
                  pl.BlockSpec((tq, LANES), lambda h, i: (i, 0)),
                  pl.BlockSpec((nh, lp), lambda h, i: (0, 0)), vec, vec,
                  pl.BlockSpec((tq, HEAD_DIM), lambda h, i: (i, oblk0 + h))],
        out_specs=[pl.BlockSpec((tq, HEAD_DIM), lambda h, i: (i, h)),
                   pl.BlockSpec((lp, HEAD_DIM), lambda h, i: (0, h)),
                   pl.BlockSpec((lp, HEAD_DIM), lambda h, i: (0, h)),
                   pl.BlockSpec((None, 1, lp), lambda h, i: (h, 0, 0)),
                   part, part],
        out_shape=[SDS((lp, nh * HEAD_DIM), BF)] * 3
        + [SDS((nh, 1, lp), F32), SDS((SUBLANES, LANES), F32), SDS((SUBLANES, LANES), F32)],
        scratch_shapes=[pltpu.VMEM((lp, HEAD_DIM), BF), pltpu.VMEM((lp, HEAD_DIM), BF),
                        pltpu.VMEM((lp, HEAD_DIM), F32), pltpu.VMEM((lp, HEAD_DIM), F32),
                        pltpu.VMEM((1, lp), F32)],
        vmem_mib=56, name=name, comm=comm)(z, z, z, cum, cumt, qw, kw, dmix)


def _adamw_math(w, g, m, v):
    m2 = ADAM_B1 * m + (1.0 - ADAM_B1) * g
    v2 = ADAM_B2 * v + (1.0 - ADAM_B2) * (g * g)
    m_hat = m2 / (1.0 - ADAM_B1 ** ADAM_STEP)
    v_hat = v2 / (1.0 - ADAM_B2 ** ADAM_STEP)
    delta = -ADAM_LR * (m_hat / (jnp.sqrt(v_hat) + ADAM_EPS) + ADAM_WD * w)
    return delta, m2, v2


def _adamw(g_in, w, m, v, name, comm=None):
    r, c = w.shape
    partial_sum = g_in.ndim == 3
    lane_padded = -(-c // LANES) * LANES
    tr = _largest_tile(r, max(16, MIB // (4 * lane_padded) // 16 * 16), 16)

    def body(g_ref, w_ref, m_ref, v_ref, go_ref, d_ref, mo_ref, vo_ref):
        if partial_sum:
            g = g_ref[0].astype(F32)
            for k in range(1, g_in.shape[0]):
                g = g + g_ref[k].astype(F32)
        else:
            g = g_ref[...]
        delta, m2, v2 = _adamw_math(w_ref[...], g, m_ref[...], v_ref[...])
        go_ref[...] = g
        d_ref[...] = delta
        mo_ref[...] = m2
        vo_ref[...] = v2

    blk = pl.BlockSpec((tr, c), lambda i: (i, 0))
    g_spec = pl.BlockSpec((g_in.shape[0], tr, c), lambda i: (0, i, 0)) if partial_sum else blk
    return _call(
        body, grid=(r // tr,), in_specs=[g_spec, blk, blk, blk], out_specs=[blk] * 4,
        out_shape=[SDS((r, c), F32)] * 4, vmem_mib=40, name=name, comm=comm)(g_in, w, m, v)


def _peer(x, y, c, k):
    return (1 - x if k & 4 else x, 1 - y if k & 2 else y, 1 - c if k & 1 else c)


_SIBLING = 1
_ICI_RELS = (2, 4, 6)


def _mesh_pos():
    return lax.axis_index("x"), lax.axis_index("y"), lax.axis_index("c")


def _sem_pair(sems, t, j, n_rel, scalars):
    if scalars:
        i = 2 * (t * n_rel + j)
        return sems[i], sems[i + 1]
    return sems[0].at[t, j], sems[1].at[t, j]


def _dev(pos):
    return 4 * pos[0] + 2 * pos[1] + pos[2]


def _gather_ici(shards, landing=None, rels=(_SIBLING,) + _ICI_RELS):
    n = len(shards)

    def remote(ins, outs, sems, arrival):
        x, y, c = _mesh_pos()
        dst = ins[n:] if landing is not None else outs
        cps = []
        for j, k in enumerate(rels):
            peer = _peer(x, y, c, k)
            slot = _dev(peer) if arrival else _dev((x, y, c))
            for t in range(n):
                send_sem, recv_sem = _sem_pair(sems, t, j, len(rels), landing is not None)
                cps.append(pltpu.make_async_remote_copy(
                    src_ref=ins[t], dst_ref=dst[t].at[slot], send_sem=send_sem, recv_sem=recv_sem,
                    device_id=peer, device_id_type=pl.DeviceIdType.MESH))
        return cps

    if landing is not None:
        def start_remote(ins, outs, sems):
            for cp in remote(ins, outs, sems, False):
                cp.start()

        def finish_remote(ins, outs, sems):
            for cp in remote(ins, outs, sems, True):
                cp.wait_recv()
            for cp in remote(ins, outs, sems, False):
                cp.wait_send()

        return _Comm(list(shards) + list(landing), [SDS(a.shape, a.dtype) for a in landing],
                     [pltpu.SemaphoreType.DMA(())] * (2 * n * len(rels)),
                     start_remote, finish_remote, aliases={n + t: t for t in range(n)})

    def local(ins, outs, sems):
        me = _dev(_mesh_pos())
        return [pltpu.make_async_copy(ins[t], outs[t].at[me], sems[2].at[t]) for t in range(n)]

    def start(ins, outs, sems):
        for cp in local(ins, outs, sems) + remote(ins, outs, sems, False):
            cp.start()

    def finish(ins, outs, sems):
        for cp in local(ins, outs, sems):
            cp.wait()
        for cp in remote(ins, outs, sems, True):
            cp.wait_recv()
        for cp in remote(ins, outs, sems, False):
            cp.wait_send()

    return _Comm(shards, [SDS((N_DEV,) + s.shape, s.dtype) for s in shards],
                 [pltpu.SemaphoreType.DMA((n, len(rels))), pltpu.SemaphoreType.DMA((n, len(rels))),
                  pltpu.SemaphoreType.DMA((n,))], start, finish)


def _gather_diagonal(zones):
    n = len(zones)

    def copies(ins, outs, sems, arrival):
        x, y, c = _mesh_pos()
        y_nb, x_nb, diag = _peer(x, y, c, 2), _peer(x, y, c, 4), _peer(x, y, c, 6)
        cps = []
        for j, (to, origin) in enumerate(((y_nb, x_nb), (x_nb, y_nb))):
            slot = _dev(diag) if arrival else _dev(origin)
            for t in range(n):
                half = ins[t].shape[1] // 2
                rows = ins[t].at[slot, pl.ds(j * half, half)]
                send_sem, recv_sem = _sem_pair(sems, t, j, 2, True)
                cps.append(pltpu.make_async_remote_copy(
                    src_ref=rows, dst_ref=rows, send_sem=send_sem, recv_sem=recv_sem,
                    device_id=to, device_id_type=pl.DeviceIdType.MESH))
        return cps

    def start(ins, outs, sems):
        for cp in copies(ins, outs, sems, False):
            cp.start()

    def finish(ins, outs, sems):
        for cp in copies(ins, outs, sems, True):
            cp.wait_recv()
        for cp in copies(ins, outs, sems, False):
            cp.wait_send()

    return _Comm(list(zones), [SDS(a.shape, a.dtype) for a in zones], [pltpu.SemaphoreType.DMA(())] * (4 * n),
                 start, finish, aliases={t: t for t in range(n)})


def _gather_fwd(partial):
    n = len(partial)

    def copies(ins, outs, sems, arrival):
        x, y, c = _mesh_pos()
        sibling = _peer(x, y, c, _SIBLING)
        cps = []
        for j, k in enumerate(_ICI_RELS):
            slot = _dev(_peer(x, y, c, k | _SIBLING if arrival else k))
            for t in range(n):
                cps.append(pltpu.make_async_remote_copy(
                    src_ref=ins[t].at[slot], dst_ref=outs[t].at[slot], send_sem=sems[0].at[t, j],
                    recv_sem=sems[1].at[t, j], device_id=sibling, device_id_type=pl.DeviceIdType.MESH))
        return cps

    def start(ins, outs, sems):
        for cp in copies(ins, outs, sems, False):
            cp.start()

    def finish(ins, outs, sems):
        for cp in copies(ins, outs, sems, True):
            cp.wait_recv()
        for cp in copies(ins, outs, sems, False):
            cp.wait_send()

    return _Comm(partial, [SDS(a.shape, a.dtype) for a in partial],
                 [pltpu.SemaphoreType.DMA((n, len(_ICI_RELS)))] * 2, start, finish,
                 aliases={t: t for t in range(n)})


def _scatter_sibling(slots):
    n = len(slots)

    def copies(ins, outs, sems):
        x, y, c = _mesh_pos()
        return [pltpu.make_async_remote_copy(
            src_ref=ins[t].at[:, 1 - c], dst_ref=outs[t], send_sem=sems[0].at[t], recv_sem=sems[1].at[t],
            device_id=_peer(x, y, c, _SIBLING), device_id_type=pl.DeviceIdType.MESH) for t in range(n)]

    def start(ins, outs, sems):
        for cp in copies(ins, outs, sems):
            cp.start()

    def finish(ins, outs, sems):
        for cp in copies(ins, outs, sems):
            cp.wait()

    return _Comm(slots, [SDS((s.shape[0],) + s.shape[2:], s.dtype) for s in slots],
                 [pltpu.SemaphoreType.DMA((n,))] * 2, start, finish)


def _scatter_ici(chip_sums, landing=None):
    n = len(chip_sums)

    def remote(ins, outs, sems, arrival):
        x, y, c = _mesh_pos()
        dst = ins[n:] if landing is not None else outs
        cps = []
        for j, k in enumerate(_ICI_RELS):
            peer = _peer(x, y, c, k)
            theirs, mine = 2 * peer[0] + peer[1], 2 * x + y
            for t in range(n):
                send_sem, recv_sem = _sem_pair(sems, t, j, len(_ICI_RELS), landing is not None)
                cps.append(pltpu.make_async_remote_copy(
                    src_ref=ins[t].at[theirs], dst_ref=dst[t].at[theirs if arrival else mine],
                    send_sem=send_sem, recv_sem=recv_sem,
                    device_id=peer, device_id_type=pl.DeviceIdType.MESH))
        return cps

    if landing is not None:
        def start_remote(ins, outs, sems):
            for cp in remote(ins, outs, sems, False):
                cp.start()

        def finish_remote(ins, outs, sems):
            for cp in remote(ins, outs, sems, True):
                cp.wait_recv()
            for cp in remote(ins, outs, sems, False):
                cp.wait_send()

        return _Comm(list(chip_sums) + list(landing), [SDS(a.shape, a.dtype) for a in landing],
                     [pltpu.SemaphoreType.DMA(())] * (2 * n * len(_ICI_RELS)), start_remote, finish_remote,
                     aliases={n + t: t for t in range(n)})

    def local(ins, outs, sems):
        x, y, _ = _mesh_pos()
        return [pltpu.make_async_copy(ins[t].at[2 * x + y], outs[t].at[2 * x + y], sems[2].at[t]) for t in range(n)]

    def start(ins, outs, sems):
        for cp in local(ins, outs, sems) + remote(ins, outs, sems, False):
            cp.start()

    def finish(ins, outs, sems):
        for cp in local(ins, outs, sems):
            cp.wait()
        for cp in remote(ins, outs, sems, True):
            cp.wait_recv()
        for cp in remote(ins, outs, sems, False):
            cp.wait_send()

    return _Comm(chip_sums, [SDS(a.shape, a.dtype) for a in chip_sums],
                 [pltpu.SemaphoreType.DMA((n, len(_ICI_RELS))), pltpu.SemaphoreType.DMA((n, len(_ICI_RELS))),
                  pltpu.SemaphoreType.DMA((n,))], start, finish)


def _chip_sum(slots, from_sibling, core, name):
    nq, _, r, c = slots.shape
    tr = _largest_tile(r, 1024, 16)

    def body(core_ref, a_ref, b_ref, o_ref):
        o_ref[...] = (a_ref[...].astype(F32) + b_ref[...].astype(F32)).astype(BF)

    return pl.pallas_call(
        body,
        grid_spec=pltpu.PrefetchScalarGridSpec(
            num_scalar_prefetch=1, grid=(nq, r // tr),
            in_specs=[pl.BlockSpec((None, None, tr, c), lambda q, i, core_ref: (q, core_ref[0], i, 0)),
                      pl.BlockSpec((None, tr, c), lambda q, i, core_ref: (q, i, 0))],
            out_specs=pl.BlockSpec((None, tr, c), lambda q, i, core_ref: (q, i, 0))),
        out_shape=SDS((nq, r, c), BF), compiler_params=pltpu.CompilerParams(vmem_limit_bytes=40 * MIB),
        name=name)(core, slots, from_sibling)


def _small_reduce(pack_g, meta_g, loss_scale, name, comm=None):
    w = pack_g.shape[2]

    def body(p_ref, m_ref, tot_ref, meta_ref, loss_ref):
        acc = p_ref[0]
        macc = m_ref[0]
        for k in range(1, N_DEV):
            acc = acc + p_ref[k]
            macc = macc + m_ref[k]
        tot = jnp.sum(acc, axis=0, keepdims=True)
        tot_ref[...] = tot
        meta_ref[...] = macc
        loss_ref[...] = jnp.full((1, LANES), loss_scale * jnp.sum(tot[:, w - LANES:w]), F32)

    return pl.pallas_call(
        body, out_shape=[SDS((1, w), F32), SDS(meta_g.shape[1:], F32), SDS((1, LANES), F32)],
        compiler_params=pltpu.CompilerParams(vmem_limit_bytes=32 * MIB), name=name)(pack_g, meta_g)


def _local_step(x, target, sw, plan):
    s_len, d = x.shape
    n_heads, n_meta = plan.n_heads, plan.n_meta
    l = n_meta + s_len
    lp = -(-l // LANES) * LANES
    tm = _largest_tile(lp, 544, 16)
    tq = _largest_tile(lp, 272, 16)
    te = _largest_tile(lp, 272, 16)
    tmd = _largest_tile(d, 512, LANES)

    plan.at("start")
    x, target = plan.gate((x, target))
    zmeta, zpad = jnp.zeros((n_meta, d), F32), jnp.zeros((lp - l, d), F32)
    h0 = jnp.concatenate([zmeta, x, zpad], axis=0)
    tpad = jnp.concatenate([zmeta, target, zpad], axis=0)
    plan.at("landed", (h0, tpad))
    h0 = lax.dynamic_update_slice(h0, plan.weights("meta"), (0, 0))

    split = plan.ffn1_split()
    if split is None:
        wg1, wu1, wd1 = plan.weights("ffn1")
        h1, a1, b1, u1 = _ffn_fwd(h0, sw["ffn1_norm"], wg1, wu1, wd1, tm, "ffn1_fwd", plan.comm("ffn1_fwd"))
    else:
        carry = _ffn_fwd_part(h0, sw["ffn1_norm"], *plan.weights("ffn1_landing"), split[0], None, tm, "ffn1_fwd_a",
                              plan.order_tokens())
        plan.at("ffn1_mid", (carry[0],))
        wg1, wu1, wd1 = plan.weights("ffn1")
        h1, a1, b1 = _ffn_fwd_part(h0, sw["ffn1_norm"], wg1, wu1, wd1, split[1], carry, tm, "ffn1_fwd_b",
                                   plan.order_tokens())
        u1 = carry[3]
    fs = wg1.shape[1]
    plan.at("after_ffn1_fwd", (h1,))
    win, pw, wout = plan.weights("mix")
    nz = win.shape[1]
    p_w = sw["pool_scale"].shape[1]
    npb = p_w // LANES
    fblk = nz // LANES - 1
    tnz = _largest_tile(nz, 1408, LANES)
    qw, kw, bfp, ps = sw["q_norm"], sw["k_norm"], sw["b_forget"], sw["pool_scale"]
    z, u2 = _norm_matmul(h1, sw["mix_norm"], win, tm, tnz, "mix_in", plan.comm("mix_in"))
    plan.at("after_mix_in", (u2,))
    cum = _fox_prep(z, bfp, fblk, "fox_prep")
    cumt = cum[:, :n_heads].T
    pool_o = _pool_fwd(z, pw, ps, "pool_fwd")
    att_o = _att_fwd(z, cum, cumt, qw, kw, n_heads, npb, tq, "att_fwd", plan.comm("att_fwd"))
    plan.at("after_att_fwd", (att_o,))
    h2 =_out_proj(h1, pool_o, att_o, wout, tm, "out_proj", plan.comm("out_proj"))
    wg2, wu2, wd2 = plan.weights("ffn2")
    h3, a2, b2, u3 = _ffn_fwd(h2, sw["ffn2_norm"], wg2, wu2, wd2, tm, "ffn2_fwd", plan.comm("ffn2_fwd"))
    dy, dob3, lsq = _loss_head(h3, tpad, n_meta, l, te, "loss_head")

    da2, db2, hid2 = _ffn_bwd_act(dob3, a2, b2, wd2, tm, "ffn2_bwd_act", plan.comm("ffn2_bwd_act"))
    du3 = _ffn_bwd_du(da2, db2, wg2, wu2, tm, "ffn2_bwd_du", plan.comm("ffn2_bwd_du"))
    dh2, dh2b, dn2 = _rms_bwd(du3, h2, sw["ffn2_norm"], dy, 1.0, te, "ffn2_rms_bwd")
    plan.grad("ffn2_w_gate", _matmul_tn(da2, u3, fs, d, "ffn2_dwg", plan.comm("ffn2_dwg")))
    plan.grad("ffn2_w_up", _matmul_tn(db2, u3, fs, d, "ffn2_dwu", plan.comm("ffn2_dwu")))
    plan.grad("ffn2_w_down", _matmul_tn(hid2, dob3, fs, d, "ffn2_dwd", plan.comm("ffn2_dwd")))
    plan.at("after_ffn2_dwd")

    dmix = _matmul_nt(dh2b, wout, tm, d, BF, "out_proj_bwd", plan.comm("out_proj_bwd"))
    plan.at("after_out_proj_bwd")
    tmp = _largest_tile(p_w, 512, LANES)
    plan.grad("w_out", jnp.concatenate([_matmul_tn(pool_o, dh2b, tmp, d, "dwout_pool"),
                                        _matmul_tn(att_o, dh2b, tmp, d, "dwout_att")], axis=0))
    dzp, dpw, dps = _pool_bwd(z, dmix, pw, ps, "pool_bwd")
    plan.grad("pool_w", dpw)
    plan.at("before_att_bwd")
    dq, dk, dv, dck, dqw, dkw = _att_bwd(z, cum, cumt, qw, kw, dmix, n_heads, npb, npb, tq, "att_bwd",
                                              plan.comm("att_bwd"))
    dcum = -dck[:, 0, :].T
    dcum = jnp.pad(dcum, ((0, 0), (0, LANES - n_heads)))
    dzf, dbf = _fox_bwd(z, bfp, dcum, fblk, "fox_bwd")
    dz = jnp.concatenate([dzp, dq, dk, dv, dzf], axis=1)
    plan.grad("w_in", _matmul_tn(u2, dz, tmd, tnz, "dwin", plan.comm("dwin")))
    du2 = _matmul_nt(dz, win, tm, tnz, F32, "mix_in_bwd", plan.comm("mix_in_bwd"))
    plan.at("before_ffn1_bwd_dx")
    dh1, dob1, dnm = _rms_bwd(du2, h1, sw["mix_norm"], dh2, 0.5, te, "mix_rms_bwd")

    da1, db1, hid1 = _ffn_bwd_act(dob1, a1, b1, wd1, tm, "ffn1_bwd_act", plan.comm("ffn1_bwd_act"))
    plan.grad("ffn1_w_gate", _matmul_tn(da1, u1, fs, d, "ffn1_dwg", plan.comm("ffn1_dwg")))
    plan.grad("ffn1_w_up", _matmul_tn(db1, u1, fs, d, "ffn1_dwu", plan.comm("ffn1_dwu")))
    plan.at("before_ffn1_dwd")
    plan.grad("ffn1_w_down", _matmul_tn(hid1, dob1, fs, d, "ffn1_dwd", plan.comm("ffn1_dwd")))
    plan.at("after_ffn1_dwd")
    du1 = _ffn_bwd_du(da1, db1, wg1, wu1, tm, "ffn1_bwd_du", plan.comm("ffn1_bwd_du"))
    dh0, _, dn1 = _rms_bwd(du1, h0, sw["ffn1_norm"], dh1, 1.0, te, "ffn1_rms_bwd", plan.comm("ffn1_rms_bwd"))

    small = [dn1, dnm, dn2, dps, dqw, dkw, dbf, lsq]
    return dh0[n_meta:l], dh0[:n_meta], small


_BIG = ("ffn1_w_gate", "ffn1_w_up", "ffn1_w_down", "w_in", "pool_w", "w_out", "ffn2_w_gate", "ffn2_w_up", "ffn2_w_down")
_SMALL = ("ffn1_norm", "mix_norm", "ffn2_norm", "pool_scale", "q_norm", "k_norm", "b_forget")
_ORDER = ("meta_tokens", "ffn1_norm", "ffn1_w_gate", "ffn1_w_up", "ffn1_w_down", "mix_norm", "w_in", "b_forget",
          "q_norm", "k_norm", "pool_w", "pool_scale", "w_out", "ffn2_norm", "ffn2_w_gate", "ffn2_w_up", "ffn2_w_down")


_FFN1 = ("ffn1_w_gate", "ffn1_w_up", "ffn1_w_down")
_FFN2 = ("ffn2_w_gate", "ffn2_w_up", "ffn2_w_down")
_MIX = ("w_in", "pool_w", "w_out")

_RIDES = {
    "out_proj": (("g2", _FFN2),),
    "ffn2_dwu": (("s1", ("ffn2_w_gate",)),),
    "ffn2_dwd": (("s1", ("ffn2_w_up",)),),
    "out_proj_bwd": (("s1", ("ffn2_w_down",)),),
    "mix_in_bwd": (("s1", _MIX),),
    "ffn1_dwu": (("s1", ("ffn1_w_gate",)),),
    "ffn1_dwd": (("s1", ("ffn1_w_up",)),),
    "ffn1_bwd_du": (("s1", ("ffn1_w_down",)),),
}
_META = ("meta_tokens",)
_POINTS = {
    "start": (("start", "gm", _META), ("start", "g1a", _FFN1), ("gate", _MIX + _FFN2), ("prepare", "g1", _MIX),
              ("prepare", "g1a", _FFN2)),
    "landed": (("wait", "gm", _META), ("wait", "g1a", _FFN1), ("start", "g1b", _FFN1), ("start", "g1", _MIX),
               ("start", "g1a", _FFN2)),
    "ffn1_mid": (("wait", "g1b", _FFN1), ("alone", "g2", _FFN1)),
    "after_ffn1_fwd": (("wait", "g1", _MIX), ("alone", "g2", _MIX)),
    "after_mix_in": (("wait", "g1a", _FFN2), ("start", "g1b", _FFN2)),
    "after_att_fwd": (("wait", "g1b", _FFN2),),
    "after_ffn2_dwd": (("sum", ("ffn2_w_gate",)), ("start", "s2", ("ffn2_w_gate",))),
    "after_out_proj_bwd": (("sum", ("ffn2_w_up",)), ("start", "s2", ("ffn2_w_up",))),
    "before_att_bwd": (("sum", ("ffn2_w_down",)), ("start", "s2", ("ffn2_w_down",))),
    "before_ffn1_bwd_dx": (("sum", _MIX), ("start", "s2", _MIX)),
    "before_ffn1_dwd": (("sum", ("ffn1_w_gate",)), ("start", "s2", ("ffn1_w_gate",))),
    "after_ffn1_dwd": (("sum", ("ffn1_w_up",)), ("start", "s2", ("ffn1_w_up",))),
    "after_ffn1_rms_bwd": (("sum", ("ffn1_w_down",)), ("start", "s2", ("ffn1_w_down",))),
    "before_adamw_ffn2_w_gate": (("wait", "s2", ("ffn2_w_gate",)),),
    "before_adamw_ffn2_w_up": (("wait", "s2", ("ffn2_w_up",)),),
    "before_adamw_ffn2_w_down": (("wait", "s2", ("ffn2_w_down",)),),
    "before_adamw_w_in": (("wait", "s2", _MIX),),
    "before_adamw_ffn1_w_gate": (("wait", "s2", ("ffn1_w_gate",)),),
    "before_adamw_ffn1_w_up": (("wait", "s2", ("ffn1_w_up",)),),
    "before_adamw_ffn1_w_down": (("wait", "s2", ("ffn1_w_down",)),),
}


def _own_slot_filled(block, slot, n_slots):
    zone = lax.empty((n_slots,) + block.shape, block.dtype)
    return lax.dynamic_update_slice(zone, block[None], (slot,) + (0,) * block.ndim)


class _MeshPlan:
    def __init__(self, raw, pos, d, d_in, n_heads):
        self.raw, self.pos = dict(raw), pos
        self.core = pos[2].astype(jnp.int32).reshape(1)
        self.d, self.d_in, self.n_heads, self.n_meta = d, d_in, n_heads, raw["meta_tokens"].shape[0]
        self.partial, self.full, self.slots, self.from_sibling, self.chip_sum, self.received = {}, {}, {}, {}, {}, {}
        self.partial_a, self.pending, self.prepared, self.started, self.tokens = {}, [], {}, {}, []

    def gate(self, arrays):
        gated = lax.optimization_barrier((self.tokens[-1], tuple(arrays)))
        self.tokens[-1] = gated[0]
        return gated[1]

    def _phase(self, kind, names):
        src, dst, make = {"g2": (self.partial, self.full, _gather_fwd),
                          "s1": (self.slots, self.from_sibling, _scatter_sibling),
                          "s2": (self.chip_sum, self.received, _scatter_ici)}[kind]
        op = make([src[n] for n in names])
        self.pending.append((op, dst, names))
        return op

    def _settle(self):
        for op, dst, names in self.pending:
            dst.update(zip(names, op.results))
        self.pending = []

    def _prepare(self, kind, names):
        x, y, c = self.pos
        if kind in ("g1", "g1a", "gm"):
            blocks = [self.raw[n] if kind == "gm" else _as2d(n, self.raw[n]).astype(BF) for n in names]
            rels = {"g1": (_SIBLING,) + _ICI_RELS, "g1a": (_SIBLING,) + _ICI_RELS[:2], "gm": tuple(range(1, N_DEV))}[kind]
            op = _gather_ici(blocks, [_own_slot_filled(b, 4 * x + 2 * y + c, N_DEV) for b in blocks], rels)
        elif kind == "g1b":
            op = _gather_diagonal([self.partial_a[n] for n in names])
        else:
            sums = [self.chip_sum[n] for n in names]
            mine = [lax.dynamic_index_in_dim(s, 2 * x + y, 0, keepdims=False) for s in sums]
            op = _scatter_ici(sums, [_own_slot_filled(b, 2 * x + y, N_DEV // 2) for b in mine])
        self.prepared[(kind, names)] = op

    def _start(self, kind, names):
        if (kind, names) not in self.prepared:
            self._prepare(kind, names)
        self._launch((kind, names), self.prepared.pop((kind, names)), "_".join(("start", kind, names[0])))

    def _launch(self, key, op, name):
        if self.tokens:
            op.arrs = list(self.gate(op.arrs))
        self.started[key], token = _split_start(op, name)
        self.tokens.append(token)

    def start_small_gather(self, arrays):
        x, y, c = self.pos
        zones = [_own_slot_filled(a, 4 * x + 2 * y + c, N_DEV) for a in arrays]
        self._launch("small", _gather_ici(list(arrays), zones, rels=tuple(range(1, N_DEV))), "start_gather_small")

    def wait_small_gather(self, afters):
        return _split_wait(self.started.pop("small"), afters, "wait_gather_small")

    def _wait(self, kind, names, afters):
        afters = list(afters) + [a for op in self.prepared.values() for a in op.arrs[len(op.arrs) // 2:]]
        landed = _split_wait(self.started.pop((kind, names)), afters, "_".join(("wait", kind, names[0])))
        {"g1": self.partial, "g1a": self.partial_a, "g1b": self.partial, "gm": self.partial,
         "s2": self.received}[kind].update(zip(names, landed))

    def ffn1_split(self):
        x, y, c = self.pos
        first = [(x, y, c), _peer(x, y, c, 1), _peer(x, y, c, 4), _peer(x, y, c, 2)]
        last = [_peer(x, y, c, 6), _peer(x, y, c, 5), _peer(x, y, c, 3), _peer(x, y, c, 7)]
        return tuple(jnp.stack([_dev(p) for p in part]).astype(jnp.int32) for part in (first, last))

    def order_tokens(self):
        tokens, self.tokens = self.tokens, []
        return tokens

    def comm(self, kernel_name):
        self._settle()
        ops = [self._phase(kind, names) for kind, names in _RIDES.get(kernel_name, ())]
        if self.tokens:
            ops.append(_Comm(self.tokens, [], [], lambda *a: None, lambda *a: None))
            self.tokens = []
        return _merge_comm(ops)

    def at(self, point, after=()):
        for step in _POINTS.get(point, ()):
            self._settle()
            if step[0] == "alone":
                _comm_alone(self._phase(step[1], step[2]), "_".join((step[1], point)))
            elif step[0] == "start":
                self._start(step[1], step[2])
            elif step[0] == "prepare":
                self._prepare(step[1], step[2])
            elif step[0] == "gate":
                self.raw.update(zip(step[1], self.gate([self.raw[n] for n in step[1]])))
            elif step[0] == "wait":
                self._wait(step[1], step[2], tuple(after) + tuple(self.tokens[-1:]))
            else:
                for n in step[1]:
                    pair = [pltpu.with_memory_space_constraint(a, pltpu.HBM)
                            for a in (self.slots[n], self.from_sibling[n])]
                    self.chip_sum[n] = _chip_sum(*pair, self.core, "chip_sum_" + n)

    def weights(self, group):
        self._settle()
        f, d = self.full, self.d
        if group == "meta":
            g = self.partial["meta_tokens"]
            return g.transpose(1, 0, 2).reshape(g.shape[1], d)
        if group == "ffn1_landing":
            return tuple(self.started[("g1b", _FFN1)][2])
        if group == "ffn1":
            return tuple(f[n] for n in _FFN1)
        if group == "ffn2":
            return tuple(f[n] for n in _FFN2)
        n_main = self.d_in - self.n_heads
        win = f["w_in"].transpose(1, 0, 2).reshape(d, self.d_in)
        win = jnp.concatenate([win[:, :n_main], jnp.pad(win[:, n_main:], ((0, 0), (0, LANES - self.n_heads)))], axis=1)
        pw = f["pool_w"]
        gw = pw.shape[2]
        pw = pw.reshape(N_DEV, -1, gw // N_DEV, gw).transpose(1, 0, 2, 3).reshape(-1, gw, gw)
        return win, pw, f["w_out"].reshape(-1, d)

    def grad(self, name, g):
        d = self.d
        if name == "w_in":
            g = g[:, :self.d_in].reshape(d, N_DEV, -1).transpose(1, 0, 2)
        elif name == "pool_w":
            ng, gw = g.shape[0], g.shape[2]
            g = g.astype(BF).reshape(ng, N_DEV, -1, gw).transpose(1, 0, 2, 3).reshape(N_DEV, -1, gw)
        elif name == "w_out":
            g = g.reshape(N_DEV, -1, d)
        self.slots[name] = g.reshape((N_DEV // 2, 2) + g.shape[1:])

    def gradient_parts(self, name):
        self._settle()
        return self.received[name]


_TRANSPOSED = ("ffn1_w_gate", "ffn1_w_up", "ffn2_w_gate", "ffn2_w_up")


def _as2d(name, a):
    return a[0].T if name in _TRANSPOSED else a.reshape(-1, a.shape[-1])


def _from2d(name, a2d, shape):
    return a2d.T.reshape(shape) if name in _TRANSPOSED else a2d.reshape(shape)


def kernel(x, meta_tokens, ffn1_norm, ffn1_w_gate, ffn1_w_up, ffn1_w_down, mix_norm, w_in, b_forget, q_norm, k_norm, pool_w, pool_scale, w_out, ffn2_norm, ffn2_w_gate, ffn2_w_up, ffn2_w_down, loss_target, m_meta_tokens, m_ffn1_norm, m_ffn1_w_gate, m_ffn1_w_up, m_ffn1_w_down, m_mix_norm, m_w_in, m_b_forget, m_q_norm, m_k_norm, m_pool_w, m_pool_scale, m_w_out, m_ffn2_norm, m_ffn2_w_gate, m_ffn2_w_up, m_ffn2_w_down, v_meta_tokens, v_ffn1_norm, v_ffn1_w_gate, v_ffn1_w_up, v_ffn1_w_down, v_mix_norm, v_w_in, v_b_forget, v_q_norm, v_k_norm, v_pool_w, v_pool_scale, v_w_out, v_ffn2_norm, v_ffn2_w_gate, v_ffn2_w_up, v_ffn2_w_down):
    w = dict(meta_tokens=meta_tokens, ffn1_norm=ffn1_norm, ffn1_w_gate=ffn1_w_gate, ffn1_w_up=ffn1_w_up,
             ffn1_w_down=ffn1_w_down, mix_norm=mix_norm, w_in=w_in, b_forget=b_forget, q_norm=q_norm, k_norm=k_norm,
             pool_w=pool_w, pool_scale=pool_scale, w_out=w_out, ffn2_norm=ffn2_norm, ffn2_w_gate=ffn2_w_gate,
             ffn2_w_up=ffn2_w_up, ffn2_w_down=ffn2_w_down)
    m = dict(meta_tokens=m_meta_tokens, ffn1_norm=m_ffn1_norm, ffn1_w_gate=m_ffn1_w_gate, ffn1_w_up=m_ffn1_w_up,
             ffn1_w_down=m_ffn1_w_down, mix_norm=m_mix_norm, w_in=m_w_in, b_forget=m_b_forget, q_norm=m_q_norm,
             k_norm=m_k_norm, pool_w=m_pool_w, pool_scale=m_pool_scale, w_out=m_w_out, ffn2_norm=m_ffn2_norm,
             ffn2_w_gate=m_ffn2_w_gate, ffn2_w_up=m_ffn2_w_up, ffn2_w_down=m_ffn2_w_down)
    v = dict(meta_tokens=v_meta_tokens, ffn1_norm=v_ffn1_norm, ffn1_w_gate=v_ffn1_w_gate, ffn1_w_up=v_ffn1_w_up,
             ffn1_w_down=v_ffn1_w_down, mix_norm=v_mix_norm, w_in=v_w_in, b_forget=v_b_forget, q_norm=v_q_norm,
             k_norm=v_k_norm, pool_w=v_pool_w, pool_scale=v_pool_scale, w_out=v_w_out, ffn2_norm=v_ffn2_norm,
             ffn2_w_gate=v_ffn2_w_gate, ffn2_w_up=v_ffn2_w_up, ffn2_w_down=v_ffn2_w_down)

    d = x.shape[-1]
    n_heads = b_forget.shape[-1]
    pos = (lax.axis_index("x"), lax.axis_index("y"), lax.axis_index("c"))
    me = 4 * pos[0] + 2 * pos[1] + pos[2]

    raw = {k: w[k] for k in _BIG}
    raw["meta_tokens"] = meta_tokens
    plan = _MeshPlan(raw, pos, d, N_DEV * w_in.shape[-1], n_heads)
    sw = {k: w[k] for k in _SMALL}
    sw["b_forget"] = jnp.pad(b_forget, ((0, 0), (0, LANES - n_heads)))
    dx, dmeta, small = _local_step(x[0], loss_target[0], sw, plan)

    res = {}
    last = dx

    plan.start_small_gather([jnp.concatenate(small, axis=1), dmeta])
    plan.at("after_ffn1_rms_bwd")

    def update_shards(names):
        nonlocal last
        for k in names:
            plan.at("before_adamw_" + k, (last,))
            state = [pltpu.with_memory_space_constraint(_as2d(k, t[k]), pltpu.HBM) for t in (w, m, v)]
            parts = pltpu.with_memory_space_constraint(plan.gradient_parts(k), pltpu.HBM)
            res[k] = _adamw(parts, *state, "adamw_" + k, plan.comm("adamw_" + k))
            last = res[k][0]

    update_shards(_FFN2 + _MIX + ("ffn1_w_gate", "ffn1_w_up"))

    pack_g, meta_g = plan.wait_small_gather((last,))
    tot, dmeta_tot, loss_row = _small_reduce(pack_g, meta_g, 0.5 / d, "small_reduce")

    mcols = meta_tokens.shape[1]
    g_meta = lax.dynamic_slice_in_dim(dmeta_tot, me * mcols, mcols, axis=1)
    res["meta_tokens"] = _adamw(g_meta, meta_tokens, m_meta_tokens, v_meta_tokens, "adamw_meta_tokens")

    def packed(src):
        return jnp.concatenate([src[k] for k in _SMALL[:-1]] + [jnp.pad(src["b_forget"], ((0, 0), (0, LANES - n_heads)))],
                               axis=1)

    wp = packed(w)
    sm = _adamw(tot[:, :wp.shape[1]], wp, packed(m), packed(v), "adamw_small")
    off = 0
    for k in _SMALL:
        width = w[k].shape[1]
        res[k] = tuple(o[:, off:off + width] for o in sm)
        off += width if k != "b_forget" else LANES

    last = sm[0]
    update_shards(("ffn1_w_down",))

    outs =[loss_row[0, 0], dx[None]]
    for idx in range(4):
        outs += [_from2d(k, res[k][idx], w[k].shape) for k in _ORDER]
    return tuple(outs)
```

```python
import functools

import jax
import jax.numpy as jnp
from jax import lax
from jax.experimental import pallas as pl
from jax.experimental.pallas import tpu as pltpu

F32 = jnp.float32
BF = jnp.bfloat16
SDS = jax.ShapeDtypeStruct

N_DEV = 8
LANES = 128
SUBLANES = 8
HEAD_DIM = 128
POOL_WINDOWS = (2, 4, 8, 16)
RMS_EPS = 1e-6
NEG_BIG = -1e30
MIB = 1024 * 1024

ADAM_LR = 0.001
ADAM_B1 = 0.9
ADAM_B2 = 0.999
ADAM_EPS = 1e-08
ADAM_WD = 0.01
ADAM_STEP = 10


class _Comm:
    def __init__(self, arrs, out_shape, sems, start, finish, aliases=None):
        self.arrs, self.out_shape, self.sems = list(arrs), list(out_shape), list(sems)
        self.start, self.finish, self.aliases = start, finish, dict(aliases or {})
        self.results = None


def _merge_comm(ops):
    ops = [op for op in ops if op is not None]
    if not ops:
        return None
    na, no, ns = [0], [0], [0]
    for op in ops:
        na.append(na[-1] + len(op.arrs))
        no.append(no[-1] + len(op.out_shape))
        ns.append(ns[-1] + len(op.sems))

    def parts(i, ins, outs, sems):
        return ins[na[i]:na[i + 1]], outs[no[i]:no[i + 1]], sems[ns[i]:ns[i + 1]]

    def start(ins, outs, sems):
        for i, op in enumerate(ops):
            op.start(*parts(i, ins, outs, sems))

    def finish(ins, outs, sems):
        for i, op in enumerate(ops):
            op.finish(*parts(i, ins, outs, sems))

    aliases = {}
    for i, op in enumerate(ops):
        for a, o in op.aliases.items():
            aliases[na[i] + a] = no[i] + o
    merged = _Comm([a for op in ops for a in op.arrs], [s for op in ops for s in op.out_shape],
                   [s for op in ops for s in op.sems], start, finish, aliases)
    merged.children = (ops, no)
    return merged


def _deliver(comm, results):
    comm.results = list(results)
    if hasattr(comm, "children"):
        ops, no = comm.children
        for i, op in enumerate(ops):
            _deliver(op, results[no[i]:no[i + 1]])


def _call(body, *, grid, in_specs, out_specs, out_shape, scratch_shapes=(), vmem_mib, name, comm=None):
    single = not isinstance(out_shape, (list, tuple))
    out_specs = [out_specs] if single else list(out_specs)
    out_shape = [out_shape] if single else list(out_shape)
    in_specs, scratch_shapes = list(in_specs), list(scratch_shapes)
    params = pltpu.CompilerParams(dimension_semantics=("arbitrary",) * len(grid), vmem_limit_bytes=vmem_mib * MIB)
    n_in, n_out, n_scr = len(in_specs), len(out_specs), len(scratch_shapes)

    def run(*args):
        if comm is None:
            res = pl.pallas_call(body, grid=grid, in_specs=in_specs, out_specs=out_specs, out_shape=out_shape,
                                 scratch_shapes=scratch_shapes, compiler_params=params, name=name)(*args)
            return res[0] if single else res
        ci, co = len(comm.arrs), len(comm.out_shape)

        def with_comm(*refs):
            ins, cins = refs[:n_in], refs[n_in:n_in + ci]
            o0 = n_in + ci
            outs, couts = refs[o0:o0 + n_out], refs[o0 + n_out:o0 + n_out + co]
            s0 = o0 + n_out + co
            scr, csems = refs[s0:s0 + n_scr], refs[s0 + n_scr:]
            ids = [pl.program_id(a) for a in range(len(grid))]
            first = functools.reduce(jnp.logical_and, [i == 0 for i in ids])
            last = functools.reduce(jnp.logical_and, [i == g - 1 for i, g in zip(ids, grid)])

            @pl.when(first)
            def _():
                comm.start(cins, couts, csems)

            body(*ins, *outs, *scr)

            @pl.when(last)
            def _():
                comm.finish(cins, couts, csems)

        anyspec = pl.BlockSpec(memory_space=pl.ANY)
        res = pl.pallas_call(
            with_comm, grid=grid, in_specs=in_specs + [anyspec] * ci, out_specs=out_specs + [anyspec] * co,
            out_shape=out_shape + comm.out_shape, scratch_shapes=scratch_shapes + comm.sems,
            input_output_aliases={n_in + a: n_out + o for a, o in comm.aliases.items()},
            compiler_params=params, name=name)(*args, *comm.arrs)
        _deliver(comm, res[n_out:])
        return res[0] if single else res[:n_out]

    return run


def _comm_alone(comm, name):
    def body(*refs):
        ci, co = len(comm.arrs), len(comm.out_shape)
        ins, outs, sems = refs[:ci], refs[ci:ci + co], refs[ci + co:]
        comm.start(ins, outs, sems)
        comm.finish(ins, outs, sems)

    anyspec = pl.BlockSpec(memory_space=pl.ANY)
    res = pl.pallas_call(
        body, in_specs=[anyspec] * len(comm.arrs), out_specs=[anyspec] * len(comm.out_shape),
        out_shape=comm.out_shape, scratch_shapes=comm.sems, input_output_aliases=comm.aliases, name=name)(*comm.arrs)
    _deliver(comm, res)


def _split_start(comm, name):
    na, ns = len(comm.arrs), len(comm.sems)

    def body(*refs):
        comm.start(refs[:na], None, refs[na:na + ns])
        token = refs[-1]
        token[...] = jnp.zeros_like(token)

    hbm = pl.BlockSpec(memory_space=pltpu.HBM)
    res = pl.pallas_call(
        body, name=name,
        out_shape=tuple(comm.sems) + tuple(pltpu.HBM(a.shape, a.dtype) for a in comm.arrs)
        + (SDS((SUBLANES, LANES), F32),),
        in_specs=[hbm] * na,
        out_specs=[pl.BlockSpec(memory_space=pltpu.SEMAPHORE)] * ns + [hbm] * na + [pl.BlockSpec(memory_space=pltpu.VMEM)],
        input_output_aliases={i: ns + i for i in range(na)},
        compiler_params=pltpu.CompilerParams(has_side_effects=pltpu.SideEffectType.DATAFLOW_SIDE_EFFECTING),
    )(*[pltpu.with_memory_space_constraint(a, pltpu.HBM) for a in comm.arrs])
    return (comm, res[:ns], res[ns:ns + na]), res[-1]


def _split_wait(started, afters, name):
    comm, sems, thru = started
    na, ns = len(thru), len(sems)
    afters = list(afters)

    def body(*refs):
        comm.finish(refs[:na], None, refs[na:na + ns])

    hbm = pl.BlockSpec(memory_space=pltpu.HBM)
    res = pl.pallas_call(
        body, name=name, out_shape=tuple(pltpu.HBM(a.shape, a.dtype) for a in thru),
        in_specs=[hbm] * na + [pl.BlockSpec(memory_space=pltpu.SEMAPHORE)] * ns
        + [pl.BlockSpec(memory_space=pl.ANY)] * len(afters),
        out_specs=[hbm] * na, input_output_aliases={i: i for i in range(na)},
        compiler_params=pltpu.CompilerParams(has_side_effects=pltpu.SideEffectType.DATAFLOW_SIDE_EFFECTING),
    )(*thru, *sems, *afters)
    return res[na - len(comm.out_shape):]


def _largest_tile(n, cap, mult):
    if n <= cap:
        return n
    best = None
    for t in range(mult, cap + 1, mult):
        if n % t == 0:
            best = t
    assert best is not None, (n, cap, mult)
    return best


def _dot(a, b):
    return jnp.dot(a, b, preferred_element_type=F32)


def _dot_nt(a, b):
    return lax.dot_general(a, b, (((1,), (1,)), ((), ())), preferred_element_type=F32)


def _dot_tn(a, b):
    return lax.dot_general(a, b, (((0,), (0,)), ((), ())), preferred_element_type=F32)


def _rows8(x):
    t, c = x.shape
    return jnp.sum(x.reshape(t // SUBLANES, SUBLANES, c), axis=0)


def _rstd(x):
    return lax.rsqrt(jnp.mean(x * x, axis=-1, keepdims=True) + RMS_EPS)


def _ffn_fwd(h, g, wg, wu, wd, tm, name, comm=None):
    lp, d = h.shape
    ns, fs, _ = wg.shape

    def body(h_ref, g_ref, wg_ref, wu_ref, wd_ref, out_ref, a_ref, b_ref, u_ref, acc_ref):
        j = pl.program_id(1)

        @pl.when(j == 0)
        def _():
            hh = h_ref[...]
            u_ref[...] = (hh * _rstd(hh) * g_ref[...]).astype(BF)
            acc_ref[...] = jnp.zeros_like(acc_ref)

        u = u_ref[...]
        a = _dot_nt(u, wg_ref[...])
        b = _dot_nt(u, wu_ref[...])
        a_ref[...] = a.astype(BF)
        b_ref[...] = b.astype(BF)
        hid = (a * jax.nn.sigmoid(a) * b).astype(BF)
        acc_ref[...] += _dot(hid, wd_ref[...])

        @pl.when(j == ns - 1)
        def _():
            out_ref[...] = h_ref[...] + 0.5 * acc_ref[...]

    row = pl.BlockSpec((tm, d), lambda i, j: (i, 0))
    act = pl.BlockSpec((None, tm, fs), lambda i, j: (j, i, 0))
    return _call(
        body, grid=(lp // tm, ns),
        in_specs=[row, pl.BlockSpec((1, d), lambda i, j: (0, 0)),
                  pl.BlockSpec((None, fs, d), lambda i, j: (j, 0, 0)),
                  pl.BlockSpec((None, fs, d), lambda i, j: (j, 0, 0)),
                  pl.BlockSpec((None, fs, d), lambda i, j: (j, 0, 0))],
        out_specs=[row, act, act, row],
        out_shape=[SDS((lp, d), F32), SDS((ns, lp, fs), BF), SDS((ns, lp, fs), BF), SDS((lp, d), BF)],
        scratch_shapes=[pltpu.VMEM((tm, d), F32)],
        vmem_mib=56, name=name, comm=comm)(h, g, wg, wu, wd)


def _ffn_fwd_part(h, g, wg, wu, wd, order, carry, tm, name, deps=()):
    lp, d = h.shape
    fs = wg.shape[1]
    k = order.shape[0]
    first = carry is None
    n_in = 5 if first else 8

    def body(order_ref, *refs):
        outs = refs[n_in + len(deps):]
        if first:
            h_ref, g_ref, wg_ref, wu_ref, wd_ref = refs[:n_in]
            out_ref, a_ref, b_ref, u_ref, acc_ref = outs
        else:
            h_ref, acc_in_ref, u_ref, _, _, wg_ref, wu_ref, wd_ref = refs[:n_in]
            out_ref, a_ref, b_ref, acc_ref = outs
        j = pl.program_id(1)

        @pl.when(j == 0)
        def _():
            if first:
                hh = h_ref[...]
                u_ref[...] = (hh * _rstd(hh) * g_ref[...]).astype(BF)
                acc_ref[...] = jnp.zeros_like(acc_ref)
            else:
                acc_ref[...] = acc_in_ref[...]

        u = u_ref[...]
        a = _dot_nt(u, wg_ref[...])
        b = _dot_nt(u, wu_ref[...])
        a_ref[...] = a.astype(BF)
        b_ref[...] = b.astype(BF)
        hid = (a * jax.nn.sigmoid(a) * b).astype(BF)
        acc_ref[...] += _dot(hid, wd_ref[...])

        @pl.when(j == k - 1)
        def _():
            out_ref[...] = acc_ref[...] if first else h_ref[...] + 0.5 * acc_ref[...]

    row = pl.BlockSpec((tm, d), lambda i, j, o: (i, 0))
    act = pl.BlockSpec((None, tm, fs), lambda i, j, o: (o[j], i, 0))
    wsp = pl.BlockSpec((None, fs, d), lambda i, j, o: (o[j], 0, 0))
    anyspec = pl.BlockSpec(memory_space=pl.ANY)
    acts = [SDS((wg.shape[0], lp, fs), BF)] * 2
    if first:
        in_specs = [row, pl.BlockSpec((1, d), lambda i, j, o: (0, 0)), wsp, wsp, wsp]
        out_specs, out_shape = [row, act, act, row], [SDS((lp, d), F32)] + acts + [SDS((lp, d), BF)]
        args, aliases = (h, g, wg, wu, wd), {}
    else:
        acc, a_prev, b_prev, u_prev = carry
        in_specs = [row, row, row, anyspec, anyspec, wsp, wsp, wsp]
        out_specs, out_shape = [row, act, act], [SDS((lp, d), F32)] + acts
        args, aliases = (h, acc, u_prev, a_prev, b_prev, wg, wu, wd), {4: 1, 5: 2}
    return pl.pallas_call(
        body,
        grid_spec=pltpu.PrefetchScalarGridSpec(
            num_scalar_prefetch=1, grid=(lp // tm, k), in_specs=in_specs + [anyspec] * len(deps),
            out_specs=out_specs, scratch_shapes=[pltpu.VMEM((tm, d), F32)]),
        out_shape=out_shape, input_output_aliases=aliases,
        compiler_params=pltpu.CompilerParams(dimension_semantics=("arbitrary",) * 2, vmem_limit_bytes=60 * MIB),
        name=name)(order, *args, *deps)


def _ffn_bwd_act(dob, a, b, wd, tm, name, comm=None):
    lp, d = dob.shape
    ns, fs, _ = wd.shape
    row = pl.BlockSpec((tm, d), lambda i, j: (i, 0))
    act = pl.BlockSpec((None, tm, fs), lambda i, j: (j, i, 0))
    wsp = pl.BlockSpec((None, fs, d), lambda i, j: (j, 0, 0))

    def act_body(do_ref, a_ref, b_ref, wd_ref, da_ref, db_ref, hid_ref):
        dhid = _dot_nt(do_ref[...], wd_ref[...])
        av = a_ref[...].astype(F32)
        bv = b_ref[...].astype(F32)
        sig = jax.nn.sigmoid(av)
        sil = av * sig
        hid_ref[...] = (sil * bv).astype(BF)
        da_ref[...] = (dhid * bv * (sig * (1.0 + av * (1.0 - sig)))).astype(BF)
        db_ref[...] = (dhid * sil).astype(BF)

    return _call(
        act_body, grid=(lp // tm, ns), in_specs=[row, act, act, wsp], out_specs=[act, act, act],
        out_shape=[SDS((ns, lp, fs), BF)] * 3, vmem_mib=40, name=name, comm=comm)(dob, a, b, wd)


def _ffn_bwd_du(da, db, wg, wu, tm, name, comm=None):
    ns, lp, fs = da.shape
    d = wg.shape[2]
    row = pl.BlockSpec((tm, d), lambda i, j: (i, 0))
    act = pl.BlockSpec((None, tm, fs), lambda i, j: (j, i, 0))
    wsp = pl.BlockSpec((None, fs, d), lambda i, j: (j, 0, 0))

    def du_body(da_ref, db_ref, wg_ref, wu_ref, du_ref):
        @pl.when(pl.program_id(1) == 0)
        def _():
            du_ref[...] = jnp.zeros_like(du_ref)

        du_ref[...] += _dot(da_ref[...], wg_ref[...]) + _dot(db_ref[...], wu_ref[...])

    return _call(
        du_body, grid=(lp // tm, ns), in_specs=[act, act, wsp, wsp], out_specs=row,
        out_shape=SDS((lp, d), F32), vmem_mib=48, name=name, comm=comm)(da, db, wg, wu)


def _rms_bwd(du, h, g, dres, bscale, tm, name, comm=None):
    lp, d = h.shape

    def body(du_ref, h_ref, g_ref, dres_ref, dh_ref, dhb_ref, dg_ref):
        @pl.when(pl.program_id(0) == 0)
        def _():
            dg_ref[...] = jnp.zeros_like(dg_ref)

        hh = h_ref[...]
        r = _rstd(hh)
        xhat = hh * r
        duv = du_ref[...]
        dg_ref[...] += _rows8(duv * xhat)
        dxh = duv * g_ref[...]
        dh = dres_ref[...] + r * (dxh - xhat * jnp.mean(dxh * xhat, axis=-1, keepdims=True))
        dh_ref[...] = dh
        dhb_ref[...] = (bscale * dh).astype(BF)

    row = pl.BlockSpec((tm, d), lambda i: (i, 0))
    return _call(
        body, grid=(lp // tm,),
        in_specs=[row, row, pl.BlockSpec((1, d), lambda i: (0, 0)), row],
        out_specs=[row, row, pl.BlockSpec((SUBLANES, d), lambda i: (0, 0))],
        out_shape=[SDS((lp, d), F32), SDS((lp, d), BF), SDS((SUBLANES, d), F32)],
        vmem_mib=48, name=name, comm=comm)(du, h, g, dres)


def _matmul_tn(a, b, tm, tn, name, comm=None):
    a_b, b_b = a.ndim == 3, b.ndim == 3
    ns = a.shape[0] if a_b else (b.shape[0] if b_b else 1)
    l, m = a.shape[-2:]
    n = b.shape[-1]

    def body(a_ref, b_ref, o_ref):
        o_ref[...] = _dot_tn(a_ref[...], b_ref[...]).astype(o_ref.dtype)

    a_spec = (pl.BlockSpec((None, l, tm), lambda s, i, j: (s, 0, i)) if a_b
              else pl.BlockSpec((l, tm), lambda s, i, j: (0, i)))
    b_spec = (pl.BlockSpec((None, l, tn), lambda s, i, j: (s, 0, j)) if b_b
              else pl.BlockSpec((l, tn), lambda s, i, j: (0, j)))
    batched = a_b or b_b
    o_spec = (pl.BlockSpec((None, tm, tn), lambda s, i, j: (s, i, j)) if batched
              else pl.BlockSpec((tm, tn), lambda s, i, j: (i, j)))
    o_shape = SDS((ns, m, n), BF) if batched else SDS((m, n), BF)
    return _call(
        body, grid=(ns, m // tm, n // tn), in_specs=[a_spec, b_spec], out_specs=o_spec, out_shape=o_shape,
        vmem_mib=48, name=name, comm=comm)(a, b)


def _matmul_nt(x, w, tm, tk, out_dtype, name, comm=None):
    l, k = x.shape
    n = w.shape[0]
    nk = k // tk

    def body(x_ref, w_ref, o_ref, acc_ref):
        kk = pl.program_id(1)

        @pl.when(kk == 0)
        def _():
            acc_ref[...] = jnp.zeros_like(acc_ref)

        acc_ref[...] += _dot_nt(x_ref[...], w_ref[...])

        @pl.when(kk == nk - 1)
        def _():
            o_ref[...] = acc_ref[...].astype(o_ref.dtype)

    return _call(
        body, grid=(l // tm, nk),
        in_specs=[pl.BlockSpec((tm, tk), lambda i, kk: (i, kk)), pl.BlockSpec((n, tk), lambda i, kk: (0, kk))],
        out_specs=pl.BlockSpec((tm, n), lambda i, kk: (i, 0)),
        out_shape=SDS((l, n), out_dtype),
        scratch_shapes=[pltpu.VMEM((tm, n), F32)],
        vmem_mib=48, name=name, comm=comm)(x, w)


def _norm_matmul(h, g, w, tm, tn, name, comm=None):
    lp, d = h.shape
    n = w.shape[1]

    def body(h_ref, g_ref, w_ref, z_ref, u_ref):
        @pl.when(pl.program_id(1) == 0)
        def _():
            hh = h_ref[...]
            u_ref[...] = (hh * _rstd(hh) * g_ref[...]).astype(BF)

        z_ref[...] = _dot(u_ref[...], w_ref[...])

    row = pl.BlockSpec((tm, d), lambda i, j: (i, 0))
    return _call(
        body, grid=(lp // tm, n // tn),
        in_specs=[row, pl.BlockSpec((1, d), lambda i, j: (0, 0)), pl.BlockSpec((d, tn), lambda i, j: (0, j))],
        out_specs=[pl.BlockSpec((tm, tn), lambda i, j: (i, j)), row],
        out_shape=[SDS((lp, n), F32), SDS((lp, d), BF)],
        vmem_mib=48, name=name, comm=comm)(h, g, w)


def _out_proj(h, pool_o, att_o, w_out, tm, name, comm=None):
    lp, d = h.shape
    p = pool_o.shape[1]
    dm = w_out.shape[0]

    def body(h_ref, p_ref, a_ref, w_ref, o_ref):
        o_ref[...] = h_ref[...] + _dot(p_ref[...], w_ref[0:p, :]) + _dot(a_ref[...], w_ref[p:dm, :])

    row = pl.BlockSpec((tm, d), lambda i: (i, 0))
    return _call(
        body, grid=(lp // tm,),
        in_specs=[row, pl.BlockSpec((tm, p), lambda i: (i, 0)), pl.BlockSpec((tm, dm - p), lambda i: (i, 0)),
                  pl.BlockSpec((dm, d), lambda i: (0, 0))],
        out_specs=row, out_shape=SDS((lp, d), F32),
        vmem_mib=48, name=name, comm=comm)(h, pool_o, att_o, w_out)


def _loss_head(y, tpad, row0, row1, tm, name, comm=None):
    lp, d = y.shape

    def body(y_ref, t_ref, dy_ref, dob_ref, ls_ref):
        i = pl.program_id(0)

        @pl.when(i == 0)
        def _():
            ls_ref[...] = jnp.zeros_like(ls_ref)

        rows = i * tm + lax.broadcasted_iota(jnp.int32, (tm, d), 0)
        err = jnp.where((rows >= row0) & (rows < row1), y_ref[...] - t_ref[...], 0.0)
        dy = err * (1.0 / d)
        dy_ref[...] = dy
        dob_ref[...] = (0.5 * dy).astype(BF)
        sq = _rows8(err * err)
        acc = sq[:, 0:LANES]
        for c in range(1, d // LANES):
            acc = acc + sq[:, c * LANES:(c + 1) * LANES]
        ls_ref[...] += acc

    row = pl.BlockSpec((tm, d), lambda i: (i, 0))
    return _call(
        body, grid=(lp // tm,), in_specs=[row, row],
        out_specs=[row, row, pl.BlockSpec((SUBLANES, LANES), lambda i: (0, 0))],
        out_shape=[SDS((lp, d), F32), SDS((lp, d), BF), SDS((SUBLANES, LANES), F32)],
        vmem_mib=48, name=name, comm=comm)(y, tpad)


def _window_select(levels, gidx):
    out = levels[-1]
    for k in range(len(levels) - 2, -1, -1):
        out = jnp.where(gidx == k, levels[k], out)
    return out


def _pool_window_mean_minus_id(x, gidx):
    rows = lax.broadcasted_iota(jnp.int32, x.shape, 0)
    levels = []
    s = x
    shift = 1
    while shift < POOL_WINDOWS[-1]:
        s = s + jnp.where(rows >= shift, pltpu.roll(s, shift, 0), 0.0)
        shift *= 2
        if shift in POOL_WINDOWS:
            levels.append(s)
    win = _window_select(levels, gidx)
    cnt = jnp.minimum(rows + 1, _window_select(list(POOL_WINDOWS), gidx)).astype(F32)
    return win / cnt - x, cnt


def _pool_window_transpose(dy, cnt, gidx):
    lp = dy.shape[0]
    rows = lax.broadcasted_iota(jnp.int32, dy.shape, 0)
    levels = []
    s = dy / cnt
    shift = 1
    while shift < POOL_WINDOWS[-1]:
        s = s + jnp.where(rows < lp - shift, pltpu.roll(s, lp - shift, 0), 0.0)
        shift *= 2
        if shift in POOL_WINDOWS:
            levels.append(s)
    return _window_select(levels, gidx) - dy


def _pool_fwd(z, pool_w, pool_scale, name, comm=None):
    lp = z.shape[0]
    ng, gw, _ = pool_w.shape

    def body(p_ref, w_ref, s_ref, o_ref):
        pooled, _ = _pool_window_mean_minus_id(p_ref[...], pl.program_id(0))
        o_ref[...] = (_dot(pooled.astype(BF), w_ref[...]) * s_ref[...]).astype(BF)

    return _call(
        body, grid=(ng,),
        in_specs=[pl.BlockSpec((lp, gw), lambda g: (0, g)), pl.BlockSpec((None, gw, gw), lambda g: (g, 0, 0)),
                  pl.BlockSpec((1, gw), lambda g: (0, g))],
        out_specs=pl.BlockSpec((lp, gw), lambda g: (0, g)), out_shape=SDS((lp, ng * gw), BF),
        vmem_mib=48, name=name, comm=comm)(z, pool_w, pool_scale)


def _pool_bwd(z, dmix, pool_w, pool_scale, name, comm=None):
    lp = z.shape[0]
    ng, gw, _ = pool_w.shape

    def body(p_ref, d_ref, w_ref, s_ref, dz_ref, dw_ref, ds_ref):
        g = pl.program_id(0)
        pooled, cnt = _pool_window_mean_minus_id(p_ref[...], g)
        pooled_b = pooled.astype(BF)
        w = w_ref[...]
        mixed = _dot(pooled_b, w)
        dpo = d_ref[...].astype(F32)
        ds_ref[...] = _rows8(dpo * mixed)
        dmixed = (dpo * s_ref[...]).astype(BF)
        dw_ref[...] = _dot_tn(pooled_b, dmixed)
        dpooled = _dot_nt(dmixed, w)
        dz_ref[...] = _pool_window_transpose(dpooled, cnt, g).astype(BF)

    return _call(
        body, grid=(ng,),
        in_specs=[pl.BlockSpec((lp, gw), lambda g: (0, g)), pl.BlockSpec((lp, gw), lambda g: (0, g)),
                  pl.BlockSpec((None, gw, gw), lambda g: (g, 0, 0)), pl.BlockSpec((1, gw), lambda g: (0, g))],
        out_specs=[pl.BlockSpec((lp, gw), lambda g: (0, g)), pl.BlockSpec((None, gw, gw), lambda g: (g, 0, 0)),
                   pl.BlockSpec((SUBLANES, gw), lambda g: (0, g))],
        out_shape=[SDS((lp, ng * gw), BF), SDS((ng, gw, gw), F32), SDS((SUBLANES, ng * gw), F32)],
        vmem_mib=48, name=name, comm=comm)(z, dmix, pool_w, pool_scale)


def _log_sigmoid(x):
    return jnp.minimum(x, 0.0) - jnp.log(1.0 + jnp.exp(-jnp.abs(x)))


def _fox_prep(z, bfp, fblk, name, comm=None):
    lp = z.shape[0]
    nb = lp // LANES

    def body(f_ref, b_ref, cum_ref):
        r = lax.broadcasted_iota(jnp.int32, (LANES, LANES), 0)
        c = lax.broadcasted_iota(jnp.int32, (LANES, LANES), 1)
        tri = (r >= c).astype(F32)
        carry = jnp.zeros((1, LANES), F32)
        for blk in range(nb):
            sl = slice(blk * LANES, (blk + 1) * LANES)
            lf = _log_sigmoid(f_ref[sl, :] + b_ref[...])
            cb = jnp.dot(tri, lf, preferred_element_type=F32, precision=lax.Precision.HIGHEST) + carry
            cum_ref[sl, :] = cb
            carry = cb[LANES - 1:LANES, :]

    return _call(
        body, grid=(1,),
        in_specs=[pl.BlockSpec((lp, LANES), lambda i: (0, fblk)), pl.BlockSpec((1, LANES), lambda i: (0, 0))],
        out_specs=pl.BlockSpec((lp, LANES), lambda i: (0, 0)), out_shape=SDS((lp, LANES), F32),
        vmem_mib=32, name=name, comm=comm)(z, bfp)


def _fox_bwd(z, bfp, dcum, fblk, name, comm=None):
    lp = z.shape[0]
    nb = lp // LANES

    def body(f_ref, b_ref, dc_ref, dz_ref, db_ref):
        r = lax.broadcasted_iota(jnp.int32, (LANES, LANES), 0)
        c = lax.broadcasted_iota(jnp.int32, (LANES, LANES), 1)
        tri = (r <= c).astype(F32)
        carry = jnp.zeros((1, LANES), F32)
        acc = jnp.zeros((SUBLANES, LANES), F32)
        for blk in range(nb - 1, -1, -1):
            sl = slice(blk * LANES, (blk + 1) * LANES)
            dlf = jnp.dot(tri, dc_ref[sl, :], preferred_element_type=F32, precision=lax.Precision.HIGHEST) + carry
            carry = dlf[0:1, :]
            df = dlf * jax.nn.sigmoid(-(f_ref[sl, :] + b_ref[...]))
            dz_ref[sl, :] = df.astype(BF)
            acc = acc + _rows8(df)
        db_ref[...] = acc

    return _call(
        body, grid=(1,),
        in_specs=[pl.BlockSpec((lp, LANES), lambda i: (0, fblk)), pl.BlockSpec((1, LANES), lambda i: (0, 0)),
                  pl.BlockSpec((lp, LANES), lambda i: (0, 0))],
        out_specs=[pl.BlockSpec((lp, LANES), lambda i: (0, 0)), pl.BlockSpec((SUBLANES, LANES), lambda i: (0, 0))],
        out_shape=[SDS((lp, LANES), BF), SDS((SUBLANES, LANES), F32)],
        vmem_mib=32, name=name, comm=comm)(z, bfp, dcum)


def _att_scores(q_ref, cum_ref, cumt_ref, qw_ref, kn_s, h, i, tq, lk):
    scale = 1.0 / (HEAD_DIM ** 0.5)
    q = q_ref[...]
    rq = _rstd(q)
    qhat = q * rq
    qn = (qhat * qw_ref[...]).astype(BF)
    s = _dot_nt(qn, kn_s[0:lk, :]) * scale
    lane = lax.broadcasted_iota(jnp.int32, (tq, LANES), 1)
    cq = jnp.sum(jnp.where(lane == h, cum_ref[...], 0.0), axis=1, keepdims=True)
    ck = cumt_ref[pl.ds(h, 1), 0:lk]
    s = s + (cq - ck)
    qpos = i * tq + lax.broadcasted_iota(jnp.int32, (tq, lk), 0)
    kpos = lax.broadcasted_iota(jnp.int32, (tq, lk), 1)
    s = jnp.where(qpos >= kpos, s, NEG_BIG)
    e = jnp.exp(s - jnp.max(s, axis=1, keepdims=True))
    p = e * (1.0 / jnp.sum(e, axis=1, keepdims=True))
    return p, qn, qhat, rq


def _per_query_tile(i, nq, tq, lp, fn):
    for t in range(nq):
        lk = min(lp, -(-((t + 1) * tq) // LANES) * LANES)
        pl.when(i == t)(functools.partial(fn, lk))


def _att_fwd(z, cum, cumt, qw, kw, n_heads, qblk0, tq, name, comm=None):
    lp = z.shape[0]
    nh = n_heads

    def body(q_ref, k_ref, v_ref, cum_ref, cumt_ref, qw_ref, kw_ref, o_ref, kn_s, vb_s):
        h, i = pl.program_id(0), pl.program_id(1)

        @pl.when(i == 0)
        def _():
            k = k_ref[...]
            kn_s[...] = (k * _rstd(k) * kw_ref[...]).astype(BF)
            vb_s[...] = v_ref[...].astype(BF)

        def tile(lk):
            p, _, _, _ = _att_scores(q_ref, cum_ref, cumt_ref, qw_ref, kn_s, h, i, tq, lk)
            o_ref[...] = _dot(p.astype(BF), vb_s[0:lk, :]).astype(BF)

        _per_query_tile(i, lp // tq, tq, lp, tile)

    vec = pl.BlockSpec((1, HEAD_DIM), lambda h, i: (0, 0))
    return _call(
        body, grid=(nh, lp // tq),
        in_specs=[pl.BlockSpec((tq, HEAD_DIM), lambda h, i: (i, qblk0 + h)),
                  pl.BlockSpec((lp, HEAD_DIM), lambda h, i: (0, qblk0 + nh + h)),
                  pl.BlockSpec((lp, HEAD_DIM), lambda h, i: (0, qblk0 + 2 * nh + h)),
                  pl.BlockSpec((tq, LANES), lambda h, i: (i, 0)),
                  pl.BlockSpec((nh, lp), lambda h, i: (0, 0)), vec, vec],
        out_specs=pl.BlockSpec((tq, HEAD_DIM), lambda h, i: (i, h)),
        out_shape=SDS((lp, nh * HEAD_DIM), BF),
        scratch_shapes=[pltpu.VMEM((lp, HEAD_DIM), BF), pltpu.VMEM((lp, HEAD_DIM), BF)],
        vmem_mib=48, name=name, comm=comm)(z, z, z, cum, cumt, qw, kw)


def _att_bwd(z, cum, cumt, qw, kw, dmix, n_heads, qblk0, oblk0, tq, name, comm=None):
    lp = z.shape[0]
    nh = n_heads
    nq = lp // tq
    scale = 1.0 / (HEAD_DIM ** 0.5)

    def body(q_ref, k_ref, v_ref, cum_ref, cumt_ref, qw_ref, kw_ref, do_ref,
             dq_ref, dk_ref, dv_ref, dck_ref, dqw_ref, dkw_ref,
             kn_s, vb_s, dkn_s, dv_s, dck_s):
        h, i = pl.program_id(0), pl.program_id(1)

        @pl.when((h == 0) & (i == 0))
        def _():
            dqw_ref[...] = jnp.zeros_like(dqw_ref)
            dkw_ref[...] = jnp.zeros_like(dkw_ref)

        @pl.when(i == 0)
        def _():
            k = k_ref[...]
            kn_s[...] = (k * _rstd(k) * kw_ref[...]).astype(BF)
            vb_s[...] = v_ref[...].astype(BF)
            dkn_s[...] = jnp.zeros_like(dkn_s)
            dv_s[...] = jnp.zeros_like(dv_s)
            dck_s[...] = jnp.zeros_like(dck_s)

        def tile(lk):
            p, qn, qhat, rq = _att_scores(q_ref, cum_ref, cumt_ref, qw_ref, kn_s, h, i, tq, lk)
            dob = do_ref[...]
            dp = _dot_nt(dob, vb_s[0:lk, :])
            ds = p * (dp - jnp.sum(p * dp, axis=1, keepdims=True))
            dsb = ds.astype(BF)
            dv_s[0:lk, :] += _dot_tn(p.astype(BF), dob)
            dkn_s[0:lk, :] += _dot_tn(dsb, qn)
            dck_s[:, 0:lk] += jnp.sum(ds, axis=0, keepdims=True)
            dqn = _dot(dsb, kn_s[0:lk, :]) * scale
            gq = dqn * qw_ref[...]
            dq_ref[...] = (rq * (gq - qhat * jnp.mean(gq * qhat, axis=-1, keepdims=True))).astype(BF)
            dqw_ref[...] += _rows8(dqn * qhat)

        _per_query_tile(i, nq, tq, lp, tile)

        @pl.when(i == nq - 1)
        def _():
            k = k_ref[...]
            rk = _rstd(k)
            khat = k * rk
            dkn = dkn_s[...] * scale
            gk = dkn * kw_ref[...]
            dk_ref[...] = (rk * (gk - khat * jnp.mean(gk * khat, axis=-1, keepdims=True))).astype(BF)
            dkw_ref[...] += _rows8(dkn * khat)
            dv_ref[...] = dv_s[...].astype(BF)
            dck_ref[...] = dck_s[...]

    vec = pl.BlockSpec((1, HEAD_DIM), lambda h, i: (0, 0))
    part = pl.BlockSpec((SUBLANES, LANES), lambda h, i: (0, 0))
    return _call(
        body, grid=(nh, nq),
        in_specs=[pl.BlockSpec((tq, HEAD_DIM), lambda h, i: (i, qblk0 + h)),
                  pl.BlockSpec((lp, HEAD_DIM), lambda h, i: (0, qblk0 + nh + h)),
                  pl.BlockSpec((lp, HEAD_DIM), lambda h, i: (0, qblk0 + 2 * nh + h)),
                  pl.BlockSpec((tq, LANES), lambda h, i: (i, 0)),
                  pl.BlockSpec((nh, lp), lambda h, i: (0, 0)), vec, vec,
                  pl.BlockSpec((tq, HEAD_DIM), lambda h, i: (i, oblk0 + h))],
        out_specs=[pl.BlockSpec((tq, HEAD_DIM), lambda h, i: (i, h)),
                   pl.BlockSpec((lp, HEAD_DIM), lambda h, i: (0, h)),
                   pl.BlockSpec((lp, HEAD_DIM), lambda h, i: (0, h)),
                   pl.BlockSpec((None, 1, lp), lambda h, i: (h, 0, 0)),
                   part, part],
        out_shape=[SDS((lp, nh * HEAD_DIM), BF)] * 3
        + [SDS((nh, 1, lp), F32), SDS((SUBLANES, LANES), F32), SDS((SUBLANES, LANES), F32)],
        scratch_shapes=[pltpu.VMEM((lp, HEAD_DIM), BF), pltpu.VMEM((lp, HEAD_DIM), BF),
                        pltpu.VMEM((lp, HEAD_DIM), F32), pltpu.VMEM((lp, HEAD_DIM), F32),
                        pltpu.VMEM((1, lp), F32)],
        vmem_mib=56, name=name, comm=comm)(z, z, z, cum, cumt, qw, kw, dmix)


def _adamw_math(w, g, m, v):
    m2 = ADAM_B1 * m + (1.0 - ADAM_B1) * g
    v2 = ADAM_B2 * v + (1.0 - ADAM_B2) * (g * g)
    m_hat = m2 / (1.0 - ADAM_B1 ** ADAM_STEP)
    v_hat = v2 / (1.0 - ADAM_B2 ** ADAM_STEP)
    delta = -ADAM_LR * (m_hat / (jnp.sqrt(v_hat) + ADAM_EPS) + ADAM_WD * w)
    return delta, m2, v2


def _adamw(g_in, w, m, v, name, comm=None):
    r, c = w.shape
    partial_sum = g_in.ndim == 3
    lane_padded = -(-c // LANES) * LANES
    tr = _largest_tile(r, max(16, MIB // (4 * lane_padded) // 16 * 16), 16)

    def body(g_ref, w_ref, m_ref, v_ref, go_ref, d_ref, mo_ref, vo_ref):
        if partial_sum:
            g = g_ref[0].astype(F32)
            for k in range(1, g_in.shape[0]):
                g = g + g_ref[k].astype(F32)
        else:
            g = g_ref[...]
        delta, m2, v2 = _adamw_math(w_ref[...], g, m_ref[...], v_ref[...])
        go_ref[...] = g
        d_ref[...] = delta
        mo_ref[...] = m2
        vo_ref[...] = v2

    blk = pl.BlockSpec((tr, c), lambda i: (i, 0))
    g_spec = pl.BlockSpec((g_in.shape[0], tr, c), lambda i: (0, i, 0)) if partial_sum else blk
    return _call(
        body, grid=(r // tr,), in_specs=[g_spec, blk, blk, blk], out_specs=[blk] * 4,
        out_shape=[SDS((r, c), F32)] * 4, vmem_mib=40, name=name, comm=comm)(g_in, w, m, v)


def _peer(x, y, c, k):
    return (1 - x if k & 4 else x, 1 - y if k & 2 else y, 1 - c if k & 1 else c)


_SIBLING = 1
_ICI_RELS = (2, 4, 6)


def _mesh_pos():
    return lax.axis_index("x"), lax.axis_index("y"), lax.axis_index("c")


def _sem_pair(sems, t, j, n_rel, scalars):
    if scalars:
        i = 2 * (t * n_rel + j)
        return sems[i], sems[i + 1]
    return sems[0].at[t, j], sems[1].at[t, j]


def _dev(pos):
    return 4 * pos[0] + 2 * pos[1] + pos[2]


def _gather_ici(shards, landing=None, rels=(_SIBLING,) + _ICI_RELS):
    n = len(shards)

    def remote(ins, outs, sems, arrival):
        x, y, c = _mesh_pos()
        dst = ins[n:] if landing is not None else outs
        cps = []
        for j, k in enumerate(rels):
            peer = _peer(x, y, c, k)
            slot = _dev(peer) if arrival else _dev((x, y, c))
            for t in range(n):
                send_sem, recv_sem = _sem_pair(sems, t, j, len(rels), landing is not None)
                cps.append(pltpu.make_async_remote_copy(
                    src_ref=ins[t], dst_ref=dst[t].at[slot], send_sem=send_sem, recv_sem=recv_sem,
                    device_id=peer, device_id_type=pl.DeviceIdType.MESH))
        return cps

    if landing is not None:
        def start_remote(ins, outs, sems):
            for cp in remote(ins, outs, sems, False):
                cp.start()

        def finish_remote(ins, outs, sems):
            for cp in remote(ins, outs, sems, True):
                cp.wait_recv()
            for cp in remote(ins, outs, sems, False):
                cp.wait_send()

        return _Comm(list(shards) + list(landing), [SDS(a.shape, a.dtype) for a in landing],
                     [pltpu.SemaphoreType.DMA(())] * (2 * n * len(rels)),
                     start_remote, finish_remote, aliases={n + t: t for t in range(n)})

    def local(ins, outs, sems):
        me = _dev(_mesh_pos())
        return [pltpu.make_async_copy(ins[t], outs[t].at[me], sems[2].at[t]) for t in range(n)]

    def start(ins, outs, sems):
        for cp in local(ins, outs, sems) + remote(ins, outs, sems, False):
            cp.start()

    def finish(ins, outs, sems):
        for cp in local(ins, outs, sems):
            cp.wait()
        for cp in remote(ins, outs, sems, True):
            cp.wait_recv()
        for cp in remote(ins, outs, sems, False):
            cp.wait_send()

    return _Comm(shards, [SDS((N_DEV,) + s.shape, s.dtype) for s in shards],
                 [pltpu.SemaphoreType.DMA((n, len(rels))), pltpu.SemaphoreType.DMA((n, len(rels))),
                  pltpu.SemaphoreType.DMA((n,))], start, finish)


def _gather_diagonal(zones):
    n = len(zones)

    def copies(ins, outs, sems, arrival):
        x, y, c = _mesh_pos()
        y_nb, x_nb, diag = _peer(x, y, c, 2), _peer(x, y, c, 4), _peer(x, y, c, 6)
        cps = []
        for j, (to, origin) in enumerate(((y_nb, x_nb), (x_nb, y_nb))):
            slot = _dev(diag) if arrival else _dev(origin)
            for t in range(n):
                half = ins[t].shape[1] // 2
                rows = ins[t].at[slot, pl.ds(j * half, half)]
                send_sem, recv_sem = _sem_pair(sems, t, j, 2, True)
                cps.append(pltpu.make_async_remote_copy(
                    src_ref=rows, dst_ref=rows, send_sem=send_sem, recv_sem=recv_sem,
                    device_id=to, device_id_type=pl.DeviceIdType.MESH))
        return cps

    def start(ins, outs, sems):
        for cp in copies(ins, outs, sems, False):
            cp.start()

    def finish(ins, outs, sems):
        for cp in copies(ins, outs, sems, True):
            cp.wait_recv()
        for cp in copies(ins, outs, sems, False):
            cp.wait_send()

    return _Comm(list(zones), [SDS(a.shape, a.dtype) for a in zones], [pltpu.SemaphoreType.DMA(())] * (4 * n),
                 start, finish, aliases={t: t for t in range(n)})


def _gather_fwd(partial):
    n = len(partial)

    def copies(ins, outs, sems, arrival):
        x, y, c = _mesh_pos()
        sibling = _peer(x, y, c, _SIBLING)
        cps = []
        for j, k in enumerate(_ICI_RELS):
            slot = _dev(_peer(x, y, c, k | _SIBLING if arrival else k))
            for t in range(n):
                cps.append(pltpu.make_async_remote_copy(
                    src_ref=ins[t].at[slot], dst_ref=outs[t].at[slot], send_sem=sems[0].at[t, j],
                    recv_sem=sems[1].at[t, j], device_id=sibling, device_id_type=pl.DeviceIdType.MESH))
        return cps

    def start(ins, outs, sems):
        for cp in copies(ins, outs, sems, False):
            cp.start()

    def finish(ins, outs, sems):
        for cp in copies(ins, outs, sems, True):
            cp.wait_recv()
        for cp in copies(ins, outs, sems, False):
            cp.wait_send()

    return _Comm(partial, [SDS(a.shape, a.dtype) for a in partial],
                 [pltpu.SemaphoreType.DMA((n, len(_ICI_RELS)))] * 2, start, finish,
                 aliases={t: t for t in range(n)})


def _scatter_sibling(slots):
    n = len(slots)

    def copies(ins, outs, sems):
        x, y, c = _mesh_pos()
        return [pltpu.make_async_remote_copy(
            src_ref=ins[t].at[:, 1 - c], dst_ref=outs[t], send_sem=sems[0].at[t], recv_sem=sems[1].at[t],
            device_id=_peer(x, y, c, _SIBLING), device_id_type=pl.DeviceIdType.MESH) for t in range(n)]

    def start(ins, outs, sems):
        for cp in copies(ins, outs, sems):
            cp.start()

    def finish(ins, outs, sems):
        for cp in copies(ins, outs, sems):
            cp.wait()

    return _Comm(slots, [SDS((s.shape[0],) + s.shape[2:], s.dtype) for s in slots],
                 [pltpu.SemaphoreType.DMA((n,))] * 2, start, finish)


def _scatter_ici(chip_sums, landing=None):
    n = len(chip_sums)

    def remote(ins, outs, sems, arrival):
        x, y, c = _mesh_pos()
        dst = ins[n:] if landing is not None else outs
        cps = []
        for j, k in enumerate(_ICI_RELS):
            peer = _peer(x, y, c, k)
            theirs, mine = 2 * peer[0] + peer[1], 2 * x + y
            for t in range(n):
                send_sem, recv_sem = _sem_pair(sems, t, j, len(_ICI_RELS), landing is not None)
                cps.append(pltpu.make_async_remote_copy(
                    src_ref=ins[t].at[theirs], dst_ref=dst[t].at[theirs if arrival else mine],
                    send_sem=send_sem, recv_sem=recv_sem,
                    device_id=peer, device_id_type=pl.DeviceIdType.MESH))
        return cps

    if landing is not None:
        def start_remote(ins, outs, sems):
            for cp in remote(ins, outs, sems, False):
                cp.start()

        def finish_remote(ins, outs, sems):
            for cp in remote(ins, outs, sems, True):
                cp.wait_recv()
            for cp in remote(ins, outs, sems, False):
                cp.wait_send()

        return _Comm(list(chip_sums) + list(landing), [SDS(a.shape, a.dtype) for a in landing],
                     [pltpu.SemaphoreType.DMA(())] * (2 * n * len(_ICI_RELS)), start_remote, finish_remote,
                     aliases={n + t: t for t in range(n)})

    def local(ins, outs, sems):
        x, y, _ = _mesh_pos()
        return [pltpu.make_async_copy(ins[t].at[2 * x + y], outs[t].at[2 * x + y], sems[2].at[t]) for t in range(n)]

    def start(ins, outs, sems):
        for cp in local(ins, outs, sems) + remote(ins, outs, sems, False):
            cp.start()

    def finish(ins, outs, sems):
        for cp in local(ins, outs, sems):
            cp.wait()
        for cp in remote(ins, outs, sems, True):
            cp.wait_recv()
        for cp in remote(ins, outs, sems, False):
            cp.wait_send()

    return _Comm(chip_sums, [SDS(a.shape, a.dtype) for a in chip_sums],
                 [pltpu.SemaphoreType.DMA((n, len(_ICI_RELS))), pltpu.SemaphoreType.DMA((n, len(_ICI_RELS))),
                  pltpu.SemaphoreType.DMA((n,))], start, finish)


def _chip_sum(slots, from_sibling, core, name):
    nq, _, r, c = slots.shape
    tr = _largest_tile(r, 1024, 16)

    def body(core_ref, a_ref, b_ref, o_ref):
        o_ref[...] = (a_ref[...].astype(F32) + b_ref[...].astype(F32)).astype(BF)

    return pl.pallas_call(
        body,
        grid_spec=pltpu.PrefetchScalarGridSpec(
            num_scalar_prefetch=1, grid=(nq, r // tr),
            in_specs=[pl.BlockSpec((None, None, tr, c), lambda q, i, core_ref: (q, core_ref[0], i, 0)),
                      pl.BlockSpec((None, tr, c), lambda q, i, core_ref: (q, i, 0))],
            out_specs=pl.BlockSpec((None, tr, c), lambda q, i, core_ref: (q, i, 0))),
        out_shape=SDS((nq, r, c), BF), compiler_params=pltpu.CompilerParams(vmem_limit_bytes=40 * MIB),
        name=name)(core, slots, from_sibling)


def _small_reduce(pack_g, meta_g, loss_scale, name):
    w = pack_g.shape[2]

    def body(p_ref, m_ref, tot_ref, meta_ref, loss_ref):
        acc = p_ref[0]
        macc = m_ref[0]
        for k in range(1, N_DEV):
            acc = acc + p_ref[k]
            macc = macc + m_ref[k]
        tot = jnp.sum(acc, axis=0, keepdims=True)
        tot_ref[...] = tot
        meta_ref[...] = macc
        loss_ref[...] = jnp.full((1, LANES), loss_scale * jnp.sum(tot[:, w - LANES:w]), F32)

    return pl.pallas_call(
        body, out_shape=[SDS((1, w), F32), SDS(meta_g.shape[1:], F32), SDS((1, LANES), F32)],
        compiler_params=pltpu.CompilerParams(vmem_limit_bytes=32 * MIB), name=name)(pack_g, meta_g)


def _local_step(x, target, sw, plan):
    s_len, d = x.shape
    n_heads, n_meta = plan.n_heads, plan.n_meta
    l = n_meta + s_len
    lp = -(-l // LANES) * LANES
    tm = _largest_tile(lp, 544, 16)
    tq = _largest_tile(lp, 272, 16)
    te = _largest_tile(lp, 272, 16)
    tmd = _largest_tile(d, 512, LANES)

    plan.at("start")
    x, target = plan.gate((x, target))
    zmeta, zpad = jnp.zeros((n_meta, d), F32), jnp.zeros((lp - l, d), F32)
    h0 = jnp.concatenate([zmeta, x, zpad], axis=0)
    tpad = jnp.concatenate([zmeta, target, zpad], axis=0)
    plan.at("landed", (h0, tpad))
    h0 = lax.dynamic_update_slice(h0, plan.weights("meta"), (0, 0))

    first_shards, last_shards = plan.ffn1_split()
    carry = _ffn_fwd_part(h0, sw["ffn1_norm"], *plan.weights("ffn1_landing"), first_shards, None, tm, "ffn1_fwd_a",
                          plan.order_tokens())
    plan.at("ffn1_mid", (carry[0],))
    wg1, wu1, wd1 = plan.weights("ffn1")
    h1, a1, b1 = _ffn_fwd_part(h0, sw["ffn1_norm"], wg1, wu1, wd1, last_shards, carry, tm, "ffn1_fwd_b",
                               plan.order_tokens())
    u1 = carry[3]
    fs = wg1.shape[1]
    plan.at("after_ffn1_fwd", (h1,))
    win, pw, wout = plan.weights("mix")
    nz = win.shape[1]
    p_w = sw["pool_scale"].shape[1]
    npb = p_w // LANES
    fblk = nz // LANES - 1
    tnz = _largest_tile(nz, 1408, LANES)
    qw, kw, bfp, ps = sw["q_norm"], sw["k_norm"], sw["b_forget"], sw["pool_scale"]
    z, u2 = _norm_matmul(h1, sw["mix_norm"], win, tm, tnz, "mix_in", plan.comm("mix_in"))
    plan.at("after_mix_in", (u2,))
    cum = _fox_prep(z, bfp, fblk, "fox_prep")
    cumt = cum[:, :n_heads].T
    pool_o = _pool_fwd(z, pw, ps, "pool_fwd")
    att_o = _att_fwd(z, cum, cumt, qw, kw, n_heads, npb, tq, "att_fwd", plan.comm("att_fwd"))
    plan.at("after_att_fwd", (att_o,))
    h2 =_out_proj(h1, pool_o, att_o, wout, tm, "out_proj", plan.comm("out_proj"))
    wg2, wu2, wd2 = plan.weights("ffn2")
    h3, a2, b2, u3 = _ffn_fwd(h2, sw["ffn2_norm"], wg2, wu2, wd2, tm, "ffn2_fwd", plan.comm("ffn2_fwd"))
    dy, dob3, lsq = _loss_head(h3, tpad, n_meta, l, te, "loss_head")

    da2, db2, hid2 = _ffn_bwd_act(dob3, a2, b2, wd2, tm, "ffn2_bwd_act", plan.comm("ffn2_bwd_act"))
    du3 = _ffn_bwd_du(da2, db2, wg2, wu2, tm, "ffn2_bwd_du", plan.comm("ffn2_bwd_du"))
    dh2, dh2b, dn2 = _rms_bwd(du3, h2, sw["ffn2_norm"], dy, 1.0, te, "ffn2_rms_bwd")
    plan.grad("ffn2_w_gate", _matmul_tn(da2, u3, fs, d, "ffn2_dwg", plan.comm("ffn2_dwg")))
    plan.grad("ffn2_w_up", _matmul_tn(db2, u3, fs, d, "ffn2_dwu", plan.comm("ffn2_dwu")))
    plan.grad("ffn2_w_down", _matmul_tn(hid2, dob3, fs, d, "ffn2_dwd", plan.comm("ffn2_dwd")))
    plan.at("after_ffn2_dwd")

    dmix = _matmul_nt(dh2b, wout, tm, d, BF, "out_proj_bwd", plan.comm("out_proj_bwd"))
    plan.at("after_out_proj_bwd")
    tmp = _largest_tile(p_w, 512, LANES)
    plan.grad("w_out", jnp.concatenate([_matmul_tn(pool_o, dh2b, tmp, d, "dwout_pool"),
                                        _matmul_tn(att_o, dh2b, tmp, d, "dwout_att")], axis=0))
    dzp, dpw, dps = _pool_bwd(z, dmix, pw, ps, "pool_bwd")
    plan.grad("pool_w", dpw)
    plan.at("before_att_bwd")
    dq, dk, dv, dck, dqw, dkw = _att_bwd(z, cum, cumt, qw, kw, dmix, n_heads, npb, npb, tq, "att_bwd",
                                              plan.comm("att_bwd"))
    dcum = -dck[:, 0, :].T
    dcum = jnp.pad(dcum, ((0, 0), (0, LANES - n_heads)))
    dzf, dbf = _fox_bwd(z, bfp, dcum, fblk, "fox_bwd")
    dz = jnp.concatenate([dzp, dq, dk, dv, dzf], axis=1)
    plan.grad("w_in", _matmul_tn(u2, dz, tmd, tnz, "dwin", plan.comm("dwin")))
    du2 = _matmul_nt(dz, win, tm, tnz, F32, "mix_in_bwd", plan.comm("mix_in_bwd"))
    plan.at("before_ffn1_bwd_dx")
    dh1, dob1, dnm = _rms_bwd(du2, h1, sw["mix_norm"], dh2, 0.5, te, "mix_rms_bwd")

    da1, db1, hid1 = _ffn_bwd_act(dob1, a1, b1, wd1, tm, "ffn1_bwd_act", plan.comm("ffn1_bwd_act"))
    plan.grad("ffn1_w_gate", _matmul_tn(da1, u1, fs, d, "ffn1_dwg", plan.comm("ffn1_dwg")))
    plan.grad("ffn1_w_up", _matmul_tn(db1, u1, fs, d, "ffn1_dwu", plan.comm("ffn1_dwu")))
    plan.at("before_ffn1_dwd")
    plan.grad("ffn1_w_down", _matmul_tn(hid1, dob1, fs, d, "ffn1_dwd", plan.comm("ffn1_dwd")))
    plan.at("after_ffn1_dwd")
    du1 = _ffn_bwd_du(da1, db1, wg1, wu1, tm, "ffn1_bwd_du", plan.comm("ffn1_bwd_du"))
    dh0, _, dn1 = _rms_bwd(du1, h0, sw["ffn1_norm"], dh1, 1.0, te, "ffn1_rms_bwd", plan.comm("ffn1_rms_bwd"))

    small = [dn1, dnm, dn2, dps, dqw, dkw, dbf, lsq]
    return dh0[n_meta:l], dh0[:n_meta], small


_BIG = ("ffn1_w_gate", "ffn1_w_up", "ffn1_w_down", "w_in", "pool_w", "w_out", "ffn2_w_gate", "ffn2_w_up", "ffn2_w_down")
_SMALL = ("ffn1_norm", "mix_norm", "ffn2_norm", "pool_scale", "q_norm", "k_norm", "b_forget")
_ORDER = ("meta_tokens", "ffn1_norm", "ffn1_w_gate", "ffn1_w_up", "ffn1_w_down", "mix_norm", "w_in", "b_forget",
          "q_norm", "k_norm", "pool_w", "pool_scale", "w_out", "ffn2_norm", "ffn2_w_gate", "ffn2_w_up", "ffn2_w_down")


_FFN1 = ("ffn1_w_gate", "ffn1_w_up", "ffn1_w_down")
_FFN2 = ("ffn2_w_gate", "ffn2_w_up", "ffn2_w_down")
_MIX = ("w_in", "pool_w", "w_out")

_RIDES = {
    "out_proj": (("g2", _FFN2),),
    "ffn2_dwu": (("s1", ("ffn2_w_gate",)),),
    "ffn2_dwd": (("s1", ("ffn2_w_up",)),),
    "out_proj_bwd": (("s1", ("ffn2_w_down",)),),
    "mix_in_bwd": (("s1", _MIX),),
    "ffn1_dwu": (("s1", ("ffn1_w_gate",)),),
    "ffn1_dwd": (("s1", ("ffn1_w_up",)),),
    "ffn1_bwd_du": (("s1", ("ffn1_w_down",)),),
}
_META = ("meta_tokens",)
_POINTS = {
    "start": (("start", "gm", _META), ("start", "g1a", _FFN1), ("gate", _MIX + _FFN2), ("prepare", "g1", _MIX),
              ("prepare", "g1a", _FFN2)),
    "landed": (("wait", "gm", _META), ("wait", "g1a", _FFN1), ("start", "g1b", _FFN1), ("start", "g1", _MIX),
               ("start", "g1a", _FFN2)),
    "ffn1_mid": (("wait", "g1b", _FFN1), ("alone", "g2", _FFN1)),
    "after_ffn1_fwd": (("wait", "g1", _MIX), ("alone", "g2", _MIX)),
    "after_mix_in": (("wait", "g1a", _FFN2), ("start", "g1b", _FFN2)),
    "after_att_fwd": (("wait", "g1b", _FFN2),),
    "after_ffn2_dwd": (("sum", ("ffn2_w_gate",)), ("start", "s2", ("ffn2_w_gate",))),
    "after_out_proj_bwd": (("sum", ("ffn2_w_up",)), ("start", "s2", ("ffn2_w_up",))),
    "before_att_bwd": (("sum", ("ffn2_w_down",)), ("start", "s2", ("ffn2_w_down",))),
    "before_ffn1_bwd_dx": (("sum", _MIX), ("start", "s2", _MIX)),
    "before_ffn1_dwd": (("sum", ("ffn1_w_gate",)), ("start", "s2", ("ffn1_w_gate",))),
    "after_ffn1_dwd": (("sum", ("ffn1_w_up",)), ("start", "s2", ("ffn1_w_up",))),
    "after_ffn1_rms_bwd": (("sum", ("ffn1_w_down",)), ("start", "s2", ("ffn1_w_down",))),
    "before_adamw_ffn2_w_gate": (("wait", "s2", ("ffn2_w_gate",)),),
    "before_adamw_ffn2_w_up": (("wait", "s2", ("ffn2_w_up",)),),
    "before_adamw_ffn2_w_down": (("wait", "s2", ("ffn2_w_down",)),),
    "before_adamw_w_in": (("wait", "s2", _MIX),),
    "before_adamw_ffn1_w_gate": (("wait", "s2", ("ffn1_w_gate",)),),
    "before_adamw_ffn1_w_up": (("wait", "s2", ("ffn1_w_up",)),),
    "before_adamw_ffn1_w_down": (("wait", "s2", ("ffn1_w_down",)),),
}


def _own_slot_filled(block, slot, n_slots):
    zone = lax.empty((n_slots,) + block.shape, block.dtype)
    return lax.dynamic_update_slice(zone, block[None], (slot,) + (0,) * block.ndim)


class _MeshPlan:
    def __init__(self, raw, pos, d, d_in, n_heads):
        self.raw, self.pos = dict(raw), pos
        self.core = pos[2].astype(jnp.int32).reshape(1)
        self.d, self.d_in, self.n_heads, self.n_meta = d, d_in, n_heads, raw["meta_tokens"].shape[0]
        self.partial, self.full, self.slots, self.from_sibling, self.chip_sum, self.received = {}, {}, {}, {}, {}, {}
        self.partial_a, self.pending, self.prepared, self.started, self.tokens = {}, [], {}, {}, []

    def gate(self, arrays):
        gated = lax.optimization_barrier((self.tokens[-1], tuple(arrays)))
        self.tokens[-1] = gated[0]
        return gated[1]

    def _phase(self, kind, names):
        src, dst, make = {"g2": (self.partial, self.full, _gather_fwd),
                          "s1": (self.slots, self.from_sibling, _scatter_sibling),
                          "s2": (self.chip_sum, self.received, _scatter_ici)}[kind]
        op = make([src[n] for n in names])
        self.pending.append((op, dst, names))
        return op

    def _settle(self):
        for op, dst, names in self.pending:
            dst.update(zip(names, op.results))
        self.pending = []

    def _prepare(self, kind, names):
        x, y, c = self.pos
        if kind in ("g1", "g1a", "gm"):
            blocks = [self.raw[n] if kind == "gm" else _as2d(n, self.raw[n]).astype(BF) for n in names]
            rels = {"g1": (_SIBLING,) + _ICI_RELS, "g1a": (_SIBLING,) + _ICI_RELS[:2], "gm": tuple(range(1, N_DEV))}[kind]
            op = _gather_ici(blocks, [_own_slot_filled(b, 4 * x + 2 * y + c, N_DEV) for b in blocks], rels)
        elif kind == "g1b":
            op = _gather_diagonal([self.partial_a[n] for n in names])
        else:
            sums = [self.chip_sum[n] for n in names]
            mine = [lax.dynamic_index_in_dim(s, 2 * x + y, 0, keepdims=False) for s in sums]
            op = _scatter_ici(sums, [_own_slot_filled(b, 2 * x + y, N_DEV // 2) for b in mine])
        self.prepared[(kind, names)] = op

    def _start(self, kind, names):
        if (kind, names) not in self.prepared:
            self._prepare(kind, names)
        self._launch((kind, names), self.prepared.pop((kind, names)), "_".join(("start", kind, names[0])))

    def _launch(self, key, op, name):
        if self.tokens:
            op.arrs = list(self.gate(op.arrs))
        self.started[key], token = _split_start(op, name)
        self.tokens.append(token)

    def start_small_gather(self, arrays):
        x, y, c = self.pos
        zones = [_own_slot_filled(a, 4 * x + 2 * y + c, N_DEV) for a in arrays]
        self._launch("small", _gather_ici(list(arrays), zones, rels=tuple(range(1, N_DEV))), "start_gather_small")

    def wait_small_gather(self, afters):
        return _split_wait(self.started.pop("small"), afters, "wait_gather_small")

    def _wait(self, kind, names, afters):
        afters = list(afters) + [a for op in self.prepared.values() for a in op.arrs[len(op.arrs) // 2:]]
        landed = _split_wait(self.started.pop((kind, names)), afters, "_".join(("wait", kind, names[0])))
        {"g1": self.partial, "g1a": self.partial_a, "g1b": self.partial, "gm": self.partial,
         "s2": self.received}[kind].update(zip(names, landed))

    def ffn1_split(self):
        x, y, c = self.pos
        first = [(x, y, c), _peer(x, y, c, 1), _peer(x, y, c, 4), _peer(x, y, c, 2)]
        last = [_peer(x, y, c, 6), _peer(x, y, c, 5), _peer(x, y, c, 3), _peer(x, y, c, 7)]
        return tuple(jnp.stack([_dev(p) for p in part]).astype(jnp.int32) for part in (first, last))

    def order_tokens(self):
        tokens, self.tokens = self.tokens, []
        return tokens

    def comm(self, kernel_name):
        self._settle()
        ops = [self._phase(kind, names) for kind, names in _RIDES.get(kernel_name, ())]
        if self.tokens:
            ops.append(_Comm(self.tokens, [], [], lambda *a: None, lambda *a: None))
            self.tokens = []
        return _merge_comm(ops)

    def at(self, point, after=()):
        for step in _POINTS.get(point, ()):
            self._settle()
            if step[0] == "alone":
                _comm_alone(self._phase(step[1], step[2]), "_".join((step[1], point)))
            elif step[0] == "start":
                self._start(step[1], step[2])
            elif step[0] == "prepare":
                self._prepare(step[1], step[2])
            elif step[0] == "gate":
                self.raw.update(zip(step[1], self.gate([self.raw[n] for n in step[1]])))
            elif step[0] == "wait":
                self._wait(step[1], step[2], tuple(after) + tuple(self.tokens[-1:]))
            else:
                for n in step[1]:
                    self.chip_sum[n] = _chip_sum(self.slots[n], self.from_sibling[n], self.core, "chip_sum_" + n)

    def weights(self, group):
        self._settle()
        f, d = self.full, self.d
        if group == "meta":
            g = self.partial["meta_tokens"]
            return g.transpose(1, 0, 2).reshape(g.shape[1], d)
        if group == "ffn1_landing":
            return tuple(self.started[("g1b", _FFN1)][2])
        if group == "ffn1":
            return tuple(f[n] for n in _FFN1)
        if group == "ffn2":
            return tuple(f[n] for n in _FFN2)
        n_main = self.d_in - self.n_heads
        win = f["w_in"].transpose(1, 0, 2).reshape(d, self.d_in)
        win = jnp.concatenate([win[:, :n_main], jnp.pad(win[:, n_main:], ((0, 0), (0, LANES - self.n_heads)))], axis=1)
        pw = f["pool_w"]
        gw = pw.shape[2]
        pw = pw.reshape(N_DEV, -1, gw // N_DEV, gw).transpose(1, 0, 2, 3).reshape(-1, gw, gw)
        return win, pw, f["w_out"].reshape(-1, d)

    def grad(self, name, g):
        d = self.d
        if name == "w_in":
            g = g[:, :self.d_in].reshape(d, N_DEV, -1).transpose(1, 0, 2)
        elif name == "pool_w":
            ng, gw = g.shape[0], g.shape[2]
            g = g.astype(BF).reshape(ng, N_DEV, -1, gw).transpose(1, 0, 2, 3).reshape(N_DEV, -1, gw)
        elif name == "w_out":
            g = g.reshape(N_DEV, -1, d)
        self.slots[name] = g.reshape((N_DEV // 2, 2) + g.shape[1:])

    def gradient_parts(self, name):
        self._settle()
        return self.received[name]


_TRANSPOSED = ("ffn1_w_gate", "ffn1_w_up", "ffn2_w_gate", "ffn2_w_up")


def _as2d(name, a):
    return a[0].T if name in _TRANSPOSED else a.reshape(-1, a.shape[-1])


def _from2d(name, a2d, shape):
    return a2d.T.reshape(shape) if name in _TRANSPOSED else a2d.reshape(shape)


def kernel(x, meta_tokens, ffn1_norm, ffn1_w_gate, ffn1_w_up, ffn1_w_down, mix_norm, w_in, b_forget, q_norm, k_norm, pool_w, pool_scale, w_out, ffn2_norm, ffn2_w_gate, ffn2_w_up, ffn2_w_down, loss_target, m_meta_tokens, m_ffn1_norm, m_ffn1_w_gate, m_ffn1_w_up, m_ffn1_w_down, m_mix_norm, m_w_in, m_b_forget, m_q_norm, m_k_norm, m_pool_w, m_pool_scale, m_w_out, m_ffn2_norm, m_ffn2_w_gate, m_ffn2_w_up, m_ffn2_w_down, v_meta_tokens, v_ffn1_norm, v_ffn1_w_gate, v_ffn1_w_up, v_ffn1_w_down, v_mix_norm, v_w_in, v_b_forget, v_q_norm, v_k_norm, v_pool_w, v_pool_scale, v_w_out, v_ffn2_norm, v_ffn2_w_gate, v_ffn2_w_up, v_ffn2_w_down):
    w = dict(meta_tokens=meta_tokens, ffn1_norm=ffn1_norm, ffn1_w_gate=ffn1_w_gate, ffn1_w_up=ffn1_w_up,
             ffn1_w_down=ffn1_w_down, mix_norm=mix_norm, w_in=w_in, b_forget=b_forget, q_norm=q_norm, k_norm=k_norm,
             pool_w=pool_w, pool_scale=pool_scale, w_out=w_out, ffn2_norm=ffn2_norm, ffn2_w_gate=ffn2_w_gate,
             ffn2_w_up=ffn2_w_up, ffn2_w_down=ffn2_w_down)
    m = dict(meta_tokens=m_meta_tokens, ffn1_norm=m_ffn1_norm, ffn1_w_gate=m_ffn1_w_gate, ffn1_w_up=m_ffn1_w_up,
             ffn1_w_down=m_ffn1_w_down, mix_norm=m_mix_norm, w_in=m_w_in, b_forget=m_b_forget, q_norm=m_q_norm,
             k_norm=m_k_norm, pool_w=m_pool_w, pool_scale=m_pool_scale, w_out=m_w_out, ffn2_norm=m_ffn2_norm,
             ffn2_w_gate=m_ffn2_w_gate, ffn2_w_up=m_ffn2_w_up, ffn2_w_down=m_ffn2_w_down)
    v = dict(meta_tokens=v_meta_tokens, ffn1_norm=v_ffn1_norm, ffn1_w_gate=v_ffn1_w_gate, ffn1_w_up=v_ffn1_w_up,
             ffn1_w_down=v_ffn1_w_down, mix_norm=v_mix_norm, w_in=v_w_in, b_forget=v_b_forget, q_norm=v_q_norm,
             k_norm=v_k_norm, pool_w=v_pool_w, pool_scale=v_pool_scale, w_out=v_w_out, ffn2_norm=v_ffn2_norm,
             ffn2_w_gate=v_ffn2_w_gate, ffn2_w_up=v_ffn2_w_up, ffn2_w_down=v_ffn2_w_down)

    d = x.shape[-1]
    n_heads = b_forget.shape[-1]
    pos = (lax.axis_index("x"), lax.axis_index("y"), lax.axis_index("c"))
    me = 4 * pos[0] + 2 * pos[1] + pos[2]

    raw = {k: w[k] for k in _BIG}
    raw["meta_tokens"] = meta_tokens
    plan = _MeshPlan(raw, pos, d, N_DEV * w_in.shape[-1], n_heads)
    sw = {k: w[k] for k in _SMALL}
    sw["b_forget"] = jnp.pad(b_forget, ((0, 0), (0, LANES - n_heads)))
    dx, dmeta, small = _local_step(x[0], loss_target[0], sw, plan)

    res = {}
    last = dx

    plan.start_small_gather([jnp.concatenate(small, axis=1), dmeta])
    plan.at("after_ffn1_rms_bwd")

    def update_shards(names):
        nonlocal last
        for k in names:
            plan.at("before_adamw_" + k, (last,))
            res[k] = _adamw(plan.gradient_parts(k), _as2d(k, w[k]), _as2d(k, m[k]), _as2d(k, v[k]), "adamw_" + k,
                            plan.comm("adamw_" + k))
            last = res[k][0]

    update_shards(_FFN2 + _MIX + ("ffn1_w_gate", "ffn1_w_up"))

    pack_g, meta_g = plan.wait_small_gather((last,))
    tot, dmeta_tot, loss_row = _small_reduce(pack_g, meta_g, 0.5 / d, "small_reduce")

    mcols = meta_tokens.shape[1]
    g_meta = lax.dynamic_slice_in_dim(dmeta_tot, me * mcols, mcols, axis=1)
    res["meta_tokens"] = _adamw(g_meta, meta_tokens, m_meta_tokens, v_meta_tokens, "adamw_meta_tokens")

    def packed(src):
        return jnp.concatenate([src[k] for k in _SMALL[:-1]] + [jnp.pad(src["b_forget"], ((0, 0), (0, LANES - n_heads)))],
                               axis=1)

    wp = packed(w)
    sm = _adamw(tot[:, :wp.shape[1]], wp, packed(m), packed(v), "adamw_small")
    off = 0
    for k in _SMALL:
        width = w[k].shape[1]
        res[k] = tuple(o[:, off:off + width] for o in sm)
        off += width if k != "b_forget" else LANES

    last = sm[0]
    update_shards(("ffn1_w_down",))

    outs =[loss_row[0, 0], dx[None]]
    for idx in range(4):
        outs += [_from2d(k, res[k][idx], w[k].shape) for k in _ORDER]
    return tuple(outs)
```

```python
import functools

import jax
import jax.numpy as jnp
from jax import lax
from jax.experimental import pallas as pl
from jax.experimental.pallas import tpu as pltpu

F32 = jnp.float32
BF = jnp.bfloat16
SDS = jax.ShapeDtypeStruct

N_DEV = 8
LANES = 128
SUBLANES = 8
HEAD_DIM = 128
POOL_WINDOWS = (2, 4, 8, 16)
RMS_EPS = 1e-6
NEG_BIG = -1e30
MIB = 1024 * 1024

ADAM_LR = 0.001
ADAM_B1 = 0.9
ADAM_B2 = 0.999
ADAM_EPS = 1e-08
ADAM_WD = 0.01
ADAM_STEP = 10


class _Comm:
    def __init__(self, arrs, out_shape, sems, start, finish, aliases=None):
        self.arrs, self.out_shape, self.sems = list(arrs), list(out_shape), list(sems)
        self.start, self.finish, self.aliases = start, finish, dict(aliases or {})
        self.results = None


def _merge_comm(ops):
    ops = [op for op in ops if op is not None]
    if not ops:
        return None
    na, no, ns = [0], [0], [0]
    for op in ops:
        na.append(na[-1] + len(op.arrs))
        no.append(no[-1] + len(op.out_shape))
        ns.append(ns[-1] + len(op.sems))

    def parts(i, ins, outs, sems):
        return ins[na[i]:na[i + 1]], outs[no[i]:no[i + 1]], sems[ns[i]:ns[i + 1]]

    def start(ins, outs, sems):
        for i, op in enumerate(ops):
            op.start(*parts(i, ins, outs, sems))

    def finish(ins, outs, sems):
        for i, op in enumerate(ops):
            op.finish(*parts(i, ins, outs, sems))

    aliases = {}
    for i, op in enumerate(ops):
        for a, o in op.aliases.items():
            aliases[na[i] + a] = no[i] + o
    merged = _Comm([a for op in ops for a in op.arrs], [s for op in ops for s in op.out_shape],
                   [s for op in ops for s in op.sems], start, finish, aliases)
    merged.children = (ops, no)
    return merged


def _deliver(comm, results):
    comm.results = list(results)
    if hasattr(comm, "children"):
        ops, no = comm.children
        for i, op in enumerate(ops):
            _deliver(op, results[no[i]:no[i + 1]])


def _call(body, *, grid, in_specs, out_specs, out_shape, scratch_shapes=(), vmem_mib, name, comm=None):
    single = not isinstance(out_shape, (list, tuple))
    out_specs = [out_specs] if single else list(out_specs)
    out_shape = [out_shape] if single else list(out_shape)
    in_specs, scratch_shapes = list(in_specs), list(scratch_shapes)
    params = pltpu.CompilerParams(dimension_semantics=("arbitrary",) * len(grid), vmem_limit_bytes=vmem_mib * MIB)
    n_in, n_out, n_scr = len(in_specs), len(out_specs), len(scratch_shapes)

    def run(*args):
        if comm is None:
            res = pl.pallas_call(body, grid=grid, in_specs=in_specs, out_specs=out_specs, out_shape=out_shape,
                                 scratch_shapes=scratch_shapes, compiler_params=params, name=name)(*args)
            return res[0] if single else res
        ci, co = len(comm.arrs), len(comm.out_shape)

        def with_comm(*refs):
            ins, cins = refs[:n_in], refs[n_in:n_in + ci]
            o0 = n_in + ci
            outs, couts = refs[o0:o0 + n_out], refs[o0 + n_out:o0 + n_out + co]
            s0 = o0 + n_out + co
            scr, csems = refs[s0:s0 + n_scr], refs[s0 + n_scr:]
            ids = [pl.program_id(a) for a in range(len(grid))]
            first = functools.reduce(jnp.logical_and, [i == 0 for i in ids])
            last = functools.reduce(jnp.logical_and, [i == g - 1 for i, g in zip(ids, grid)])

            @pl.when(first)
            def _():
                comm.start(cins, couts, csems)

            body(*ins, *outs, *scr)

            @pl.when(last)
            def _():
                comm.finish(cins, couts, csems)

        anyspec = pl.BlockSpec(memory_space=pl.ANY)
        res = pl.pallas_call(
            with_comm, grid=grid, in_specs=in_specs + [anyspec] * ci, out_specs=out_specs + [anyspec] * co,
            out_shape=out_shape + comm.out_shape, scratch_shapes=scratch_shapes + comm.sems,
            input_output_aliases={n_in + a: n_out + o for a, o in comm.aliases.items()},
            compiler_params=params, name=name)(*args, *comm.arrs)
        _deliver(comm, res[n_out:])
        return res[0] if single else res[:n_out]

    return run


def _comm_alone(comm, name):
    def body(*refs):
        ci, co = len(comm.arrs), len(comm.out_shape)
        ins, outs, sems = refs[:ci], refs[ci:ci + co], refs[ci + co:]
        comm.start(ins, outs, sems)
        comm.finish(ins, outs, sems)

    anyspec = pl.BlockSpec(memory_space=pl.ANY)
    res = pl.pallas_call(
        body, in_specs=[anyspec] * len(comm.arrs), out_specs=[anyspec] * len(comm.out_shape),
        out_shape=comm.out_shape, scratch_shapes=comm.sems, input_output_aliases=comm.aliases, name=name)(*comm.arrs)
    _deliver(comm, res)


def _split_start(comm, name):
    na, ns = len(comm.arrs), len(comm.sems)

    def body(*refs):
        comm.start(refs[:na], None, refs[na:na + ns])
        token = refs[-1]
        token[...] = jnp.zeros_like(token)

    hbm = pl.BlockSpec(memory_space=pltpu.HBM)
    res = pl.pallas_call(
        body, name=name,
        out_shape=tuple(comm.sems) + tuple(pltpu.HBM(a.shape, a.dtype) for a in comm.arrs)
        + (SDS((SUBLANES, LANES), F32),),
        in_specs=[hbm] * na,
        out_specs=[pl.BlockSpec(memory_space=pltpu.SEMAPHORE)] * ns + [hbm] * na + [pl.BlockSpec(memory_space=pltpu.VMEM)],
        input_output_aliases={i: ns + i for i in range(na)},
        compiler_params=pltpu.CompilerParams(has_side_effects=pltpu.SideEffectType.DATAFLOW_SIDE_EFFECTING),
    )(*[pltpu.with_memory_space_constraint(a, pltpu.HBM) for a in comm.arrs])
    return (comm, res[:ns], res[ns:ns + na]), res[-1]


def _split_wait(started, afters, name):
    comm, sems, thru = started
    na, ns = len(thru), len(sems)
    afters = list(afters)

    def body(*refs):
        comm.finish(refs[:na], None, refs[na:na + ns])

    hbm = pl.BlockSpec(memory_space=pltpu.HBM)
    res = pl.pallas_call(
        body, name=name, out_shape=tuple(pltpu.HBM(a.shape, a.dtype) for a in thru),
        in_specs=[hbm] * na + [pl.BlockSpec(memory_space=pltpu.SEMAPHORE)] * ns
        + [pl.BlockSpec(memory_space=pl.ANY)] * len(afters),
        out_specs=[hbm] * na, input_output_aliases={i: i for i in range(na)},
        compiler_params=pltpu.CompilerParams(has_side_effects=pltpu.SideEffectType.DATAFLOW_SIDE_EFFECTING),
    )(*thru, *sems, *afters)
    return res[na - len(comm.out_shape):]


def _largest_tile(n, cap, mult):
    if n <= cap:
        return n
    best = None
    for t in range(mult, cap + 1, mult):
        if n % t == 0:
            best = t
    assert best is not None, (n, cap, mult)
    return best


def _dot(a, b):
    return jnp.dot(a, b, preferred_element_type=F32)


def _dot_nt(a, b):
    return lax.dot_general(a, b, (((1,), (1,)), ((), ())), preferred_element_type=F32)


def _dot_tn(a, b):
    return lax.dot_general(a, b, (((0,), (0,)), ((), ())), preferred_element_type=F32)


def _rows8(x):
    t, c = x.shape
    return jnp.sum(x.reshape(t // SUBLANES, SUBLANES, c), axis=0)


def _rstd(x):
    return lax.rsqrt(jnp.mean(x * x, axis=-1, keepdims=True) + RMS_EPS)


def _ffn_fwd(h, g, wg, wu, wd, tm, name, comm=None):
    lp, d = h.shape
    ns, fs, _ = wg.shape

    def body(h_ref, g_ref, wg_ref, wu_ref, wd_ref, out_ref, a_ref, b_ref, u_ref, acc_ref):
        j = pl.program_id(1)

        @pl.when(j == 0)
        def _():
            hh = h_ref[...]
            u_ref[...] = (hh * _rstd(hh) * g_ref[...]).astype(BF)
            acc_ref[...] = jnp.zeros_like(acc_ref)

        u = u_ref[...]
        a = _dot_nt(u, wg_ref[...])
        b = _dot_nt(u, wu_ref[...])
        a_ref[...] = a.astype(BF)
        b_ref[...] = b.astype(BF)
        hid = (a * jax.nn.sigmoid(a) * b).astype(BF)
        acc_ref[...] += _dot(hid, wd_ref[...])

        @pl.when(j == ns - 1)
        def _():
            out_ref[...] = h_ref[...] + 0.5 * acc_ref[...]

    row = pl.BlockSpec((tm, d), lambda i, j: (i, 0))
    act = pl.BlockSpec((None, tm, fs), lambda i, j: (j, i, 0))
    return _call(
        body, grid=(lp // tm, ns),
        in_specs=[row, pl.BlockSpec((1, d), lambda i, j: (0, 0)),
                  pl.BlockSpec((None, fs, d), lambda i, j: (j, 0, 0)),
                  pl.BlockSpec((None, fs, d), lambda i, j: (j, 0, 0)),
                  pl.BlockSpec((None, fs, d), lambda i, j: (j, 0, 0))],
        out_specs=[row, act, act, row],
        out_shape=[SDS((lp, d), F32), SDS((ns, lp, fs), BF), SDS((ns, lp, fs), BF), SDS((lp, d), BF)],
        scratch_shapes=[pltpu.VMEM((tm, d), F32)],
        vmem_mib=56, name=name, comm=comm)(h, g, wg, wu, wd)


def _ffn_fwd_part(h, g, wg, wu, wd, order, carry, tm, name, deps=()):
    lp, d = h.shape
    fs = wg.shape[1]
    k = order.shape[0]
    first = carry is None
    n_in = 5 if first else 8

    def body(order_ref, *refs):
        outs = refs[n_in + len(deps):]
        if first:
            h_ref, g_ref, wg_ref, wu_ref, wd_ref = refs[:n_in]
            out_ref, a_ref, b_ref, u_ref, acc_ref = outs
        else:
            h_ref, acc_in_ref, u_ref, _, _, wg_ref, wu_ref, wd_ref = refs[:n_in]
            out_ref, a_ref, b_ref, acc_ref = outs
        j = pl.program_id(1)

        @pl.when(j == 0)
        def _():
            if first:
                hh = h_ref[...]
                u_ref[...] = (hh * _rstd(hh) * g_ref[...]).astype(BF)
                acc_ref[...] = jnp.zeros_like(acc_ref)
            else:
                acc_ref[...] = acc_in_ref[...]

        u = u_ref[...]
        a = _dot_nt(u, wg_ref[...])
        b = _dot_nt(u, wu_ref[...])
        a_ref[...] = a.astype(BF)
        b_ref[...] = b.astype(BF)
        hid = (a * jax.nn.sigmoid(a) * b).astype(BF)
        acc_ref[...] += _dot(hid, wd_ref[...])

        @pl.when(j == k - 1)
        def _():
            out_ref[...] = acc_ref[...] if first else h_ref[...] + 0.5 * acc_ref[...]

    row = pl.BlockSpec((tm, d), lambda i, j, o: (i, 0))
    act = pl.BlockSpec((None, tm, fs), lambda i, j, o: (o[j], i, 0))
    wsp = pl.BlockSpec((None, fs, d), lambda i, j, o: (o[j], 0, 0))
    anyspec = pl.BlockSpec(memory_space=pl.ANY)
    acts = [SDS((wg.shape[0], lp, fs), BF)] * 2
    if first:
        in_specs = [row, pl.BlockSpec((1, d), lambda i, j, o: (0, 0)), wsp, wsp, wsp]
        out_specs, out_shape = [row, act, act, row], [SDS((lp, d), F32)] + acts + [SDS((lp, d), BF)]
        args, aliases = (h, g, wg, wu, wd), {}
    else:
        acc, a_prev, b_prev, u_prev = carry
        in_specs = [row, row, row, anyspec, anyspec, wsp, wsp, wsp]
        out_specs, out_shape = [row, act, act], [SDS((lp, d), F32)] + acts
        args, aliases = (h, acc, u_prev, a_prev, b_prev, wg, wu, wd), {4: 1, 5: 2}
    return pl.pallas_call(
        body,
        grid_spec=pltpu.PrefetchScalarGridSpec(
            num_scalar_prefetch=1, grid=(lp // tm, k), in_specs=in_specs + [anyspec] * len(deps),
            out_specs=out_specs, scratch_shapes=[pltpu.VMEM((tm, d), F32)]),
        out_shape=out_shape, input_output_aliases=aliases,
        compiler_params=pltpu.CompilerParams(dimension_semantics=("arbitrary",) * 2, vmem_limit_bytes=60 * MIB),
        name=name)(order, *args, *deps)


def _ffn_bwd_act(dob, a, b, wd, tm, name, comm=None):
    lp, d = dob.shape
    ns, fs, _ = wd.shape
    row = pl.BlockSpec((tm, d), lambda i, j: (i, 0))
    act = pl.BlockSpec((None, tm, fs), lambda i, j: (j, i, 0))
    wsp = pl.BlockSpec((None, fs, d), lambda i, j: (j, 0, 0))

    def act_body(do_ref, a_ref, b_ref, wd_ref, da_ref, db_ref, hid_ref):
        dhid = _dot_nt(do_ref[...], wd_ref[...])
        av = a_ref[...].astype(F32)
        bv = b_ref[...].astype(F32)
        sig = jax.nn.sigmoid(av)
        sil = av * sig
        hid_ref[...] = (sil * bv).astype(BF)
        da_ref[...] = (dhid * bv * (sig * (1.0 + av * (1.0 - sig)))).astype(BF)
        db_ref[...] = (dhid * sil).astype(BF)

    return _call(
        act_body, grid=(lp // tm, ns), in_specs=[row, act, act, wsp], out_specs=[act, act, act],
        out_shape=[SDS((ns, lp, fs), BF)] * 3, vmem_mib=40, name=name, comm=comm)(dob, a, b, wd)


def _ffn_bwd_du(da, db, wg, wu, tm, name, comm=None):
    ns, lp, fs = da.shape
    d = wg.shape[2]
    row = pl.BlockSpec((tm, d), lambda i, j: (i, 0))
    act = pl.BlockSpec((None, tm, fs), lambda i, j: (j, i, 0))
    wsp = pl.BlockSpec((None, fs, d), lambda i, j: (j, 0, 0))

    def du_body(da_ref, db_ref, wg_ref, wu_ref, du_ref):
        @pl.when(pl.program_id(1) == 0)
        def _():
            du_ref[...] = jnp.zeros_like(du_ref)

        du_ref[...] += _dot(da_ref[...], wg_ref[...]) + _dot(db_ref[...], wu_ref[...])

    return _call(
        du_body, grid=(lp // tm, ns), in_specs=[act, act, wsp, wsp], out_specs=row,
        out_shape=SDS((lp, d), F32), vmem_mib=48, name=name, comm=comm)(da, db, wg, wu)


def _rms_bwd(du, h, g, dres, bscale, tm, name, comm=None):
    lp, d = h.shape

    def body(du_ref, h_ref, g_ref, dres_ref, dh_ref, dhb_ref, dg_ref):
        @pl.when(pl.program_id(0) == 0)
        def _():
            dg_ref[...] = jnp.zeros_like(dg_ref)

        hh = h_ref[...]
        r = _rstd(hh)
        xhat = hh * r
        duv = du_ref[...]
        dg_ref[...] += _rows8(duv * xhat)
        dxh = duv * g_ref[...]
        dh = dres_ref[...] + r * (dxh - xhat * jnp.mean(dxh * xhat, axis=-1, keepdims=True))
        dh_ref[...] = dh
        dhb_ref[...] = (bscale * dh).astype(BF)

    row = pl.BlockSpec((tm, d), lambda i: (i, 0))
    return _call(
        body, grid=(lp // tm,),
        in_specs=[row, row, pl.BlockSpec((1, d), lambda i: (0, 0)), row],
        out_specs=[row, row, pl.BlockSpec((SUBLANES, d), lambda i: (0, 0))],
        out_shape=[SDS((lp, d), F32), SDS((lp, d), BF), SDS((SUBLANES, d), F32)],
        vmem_mib=48, name=name, comm=comm)(du, h, g, dres)


def _matmul_tn(a, b, tm, tn, name, comm=None):
    a_b, b_b = a.ndim == 3, b.ndim == 3
    ns = a.shape[0] if a_b else (b.shape[0] if b_b else 1)
    l, m = a.shape[-2:]
    n = b.shape[-1]

    def body(a_ref, b_ref, o_ref):
        o_ref[...] = _dot_tn(a_ref[...], b_ref[...]).astype(o_ref.dtype)

    a_spec = (pl.BlockSpec((None, l, tm), lambda s, i, j: (s, 0, i)) if a_b
              else pl.BlockSpec((l, tm), lambda s, i, j: (0, i)))
    b_spec = (pl.BlockSpec((None, l, tn), lambda s, i, j: (s, 0, j)) if b_b
              else pl.BlockSpec((l, tn), lambda s, i, j: (0, j)))
    batched = a_b or b_b
    o_spec = (pl.BlockSpec((None, tm, tn), lambda s, i, j: (s, i, j)) if batched
              else pl.BlockSpec((tm, tn), lambda s, i, j: (i, j)))
    o_shape = SDS((ns, m, n), BF) if batched else SDS((m, n), BF)
    return _call(
        body, grid=(ns, m // tm, n // tn), in_specs=[a_spec, b_spec], out_specs=o_spec, out_shape=o_shape,
        vmem_mib=48, name=name, comm=comm)(a, b)


def _matmul_nt(x, w, tm, tk, out_dtype, name, comm=None):
    l, k = x.shape
    n = w.shape[0]
    nk = k // tk

    def body(x_ref, w_ref, o_ref, acc_ref):
        kk = pl.program_id(1)

        @pl.when(kk == 0)
        def _():
            acc_ref[...] = jnp.zeros_like(acc_ref)

        acc_ref[...] += _dot_nt(x_ref[...], w_ref[...])

        @pl.when(kk == nk - 1)
        def _():
            o_ref[...] = acc_ref[...].astype(o_ref.dtype)

    return _call(
        body, grid=(l // tm, nk),
        in_specs=[pl.BlockSpec((tm, tk), lambda i, kk: (i, kk)), pl.BlockSpec((n, tk), lambda i, kk: (0, kk))],
        out_specs=pl.BlockSpec((tm, n), lambda i, kk: (i, 0)),
        out_shape=SDS((l, n), out_dtype),
        scratch_shapes=[pltpu.VMEM((tm, n), F32)],
        vmem_mib=56, name=name, comm=comm)(x, w)


def _norm_matmul(h, g, w, tm, tn, name, comm=None):
    lp, d = h.shape
    n = w.shape[1]

    def body(h_ref, g_ref, w_ref, z_ref, u_ref):
        @pl.when(pl.program_id(1) == 0)
        def _():
            hh = h_ref[...]
            u_ref[...] = (hh * _rstd(hh) * g_ref[...]).astype(BF)

        z_ref[...] = _dot(u_ref[...], w_ref[...])

    row = pl.BlockSpec((tm, d), lambda i, j: (i, 0))
    return _call(
        body, grid=(lp // tm, n // tn),
        in_specs=[row, pl.BlockSpec((1, d), lambda i, j: (0, 0)), pl.BlockSpec((d, tn), lambda i, j: (0, j))],
        out_specs=[pl.BlockSpec((tm, tn), lambda i, j: (i, j)), row],
        out_shape=[SDS((lp, n), F32), SDS((lp, d), BF)],
        vmem_mib=60, name=name, comm=comm)(h, g, w)


def _out_proj(h, pool_o, att_o, w_out, tm, name, comm=None):
    lp, d = h.shape
    p = pool_o.shape[1]
    dm = w_out.shape[0]

    def body(h_ref, p_ref, a_ref, w_ref, o_ref):
        o_ref[...] = h_ref[...] + _dot(p_ref[...], w_ref[0:p, :]) + _dot(a_ref[...], w_ref[p:dm, :])

    row = pl.BlockSpec((tm, d), lambda i: (i, 0))
    return _call(
        body, grid=(lp // tm,),
        in_specs=[row, pl.BlockSpec((tm, p), lambda i: (i, 0)), pl.BlockSpec((tm, dm - p), lambda i: (i, 0)),
                  pl.BlockSpec((dm, d), lambda i: (0, 0))],
        out_specs=row, out_shape=SDS((lp, d), F32),
        vmem_mib=48, name=name, comm=comm)(h, pool_o, att_o, w_out)


def _loss_head(y, tpad, row0, row1, tm, name, comm=None):
    lp, d = y.shape

    def body(y_ref, t_ref, dy_ref, dob_ref, ls_ref):
        i = pl.program_id(0)

        @pl.when(i == 0)
        def _():
            ls_ref[...] = jnp.zeros_like(ls_ref)

        rows = i * tm + lax.broadcasted_iota(jnp.int32, (tm, d), 0)
        err = jnp.where((rows >= row0) & (rows < row1), y_ref[...] - t_ref[...], 0.0)
        dy = err * (1.0 / d)
        dy_ref[...] = dy
        dob_ref[...] = (0.5 * dy).astype(BF)
        sq = _rows8(err * err)
        acc = sq[:, 0:LANES]
        for c in range(1, d // LANES):
            acc = acc + sq[:, c * LANES:(c + 1) * LANES]
        ls_ref[...] += acc

    row = pl.BlockSpec((tm, d), lambda i: (i, 0))
    return _call(
        body, grid=(lp // tm,), in_specs=[row, row],
        out_specs=[row, row, pl.BlockSpec((SUBLANES, LANES), lambda i: (0, 0))],
        out_shape=[SDS((lp, d), F32), SDS((lp, d), BF), SDS((SUBLANES, LANES), F32)],
        vmem_mib=48, name=name, comm=comm)(y, tpad)


def _window_select(levels, gidx):
    out = levels[-1]
    for k in range(len(levels) - 2, -1, -1):
        out = jnp.where(gidx == k, levels[k], out)
    return out


def _pool_window_mean_minus_id(x, gidx):
    rows = lax.broadcasted_iota(jnp.int32, x.shape, 0)
    levels = []
    s = x
    shift = 1
    while shift < POOL_WINDOWS[-1]:
        s = s + jnp.where(rows >= shift, pltpu.roll(s, shift, 0), 0.0)
        shift *= 2
        if shift in POOL_WINDOWS:
            levels.append(s)
    win = _window_select(levels, gidx)
    cnt = jnp.minimum(rows + 1, _window_select(list(POOL_WINDOWS), gidx)).astype(F32)
    return win / cnt - x, cnt


def _pool_window_transpose(dy, cnt, gidx):
    lp = dy.shape[0]
    rows = lax.broadcasted_iota(jnp.int32, dy.shape, 0)
    levels = []
    s = dy / cnt
    shift = 1
    while shift < POOL_WINDOWS[-1]:
        s = s + jnp.where(rows < lp - shift, pltpu.roll(s, lp - shift, 0), 0.0)
        shift *= 2
        if shift in POOL_WINDOWS:
            levels.append(s)
    return _window_select(levels, gidx) - dy


def _pool_fwd(z, pool_w, pool_scale, name, comm=None):
    lp = z.shape[0]
    ng, gw, _ = pool_w.shape

    def body(p_ref, w_ref, s_ref, o_ref):
        pooled, _ = _pool_window_mean_minus_id(p_ref[...], pl.program_id(0))
        o_ref[...] = (_dot(pooled.astype(BF), w_ref[...]) * s_ref[...]).astype(BF)

    return _call(
        body, grid=(ng,),
        in_specs=[pl.BlockSpec((lp, gw), lambda g: (0, g)), pl.BlockSpec((None, gw, gw), lambda g: (g, 0, 0)),
                  pl.BlockSpec((1, gw), lambda g: (0, g))],
        out_specs=pl.BlockSpec((lp, gw), lambda g: (0, g)), out_shape=SDS((lp, ng * gw), BF),
        vmem_mib=48, name=name, comm=comm)(z, pool_w, pool_scale)


def _pool_bwd(z, dmix, pool_w, pool_scale, name, comm=None):
    lp = z.shape[0]
    ng, gw, _ = pool_w.shape

    def body(p_ref, d_ref, w_ref, s_ref, dz_ref, dw_ref, ds_ref):
        g = pl.program_id(0)
        pooled, cnt = _pool_window_mean_minus_id(p_ref[...], g)
        pooled_b = pooled.astype(BF)
        w = w_ref[...]
        mixed = _dot(pooled_b, w)
        dpo = d_ref[...].astype(F32)
        ds_ref[...] = _rows8(dpo * mixed)
        dmixed = (dpo * s_ref[...]).astype(BF)
        dw_ref[...] = _dot_tn(pooled_b, dmixed)
        dpooled = _dot_nt(dmixed, w)
        dz_ref[...] = _pool_window_transpose(dpooled, cnt, g).astype(BF)

    return _call(
        body, grid=(ng,),
        in_specs=[pl.BlockSpec((lp, gw), lambda g: (0, g)), pl.BlockSpec((lp, gw), lambda g: (0, g)),
                  pl.BlockSpec((None, gw, gw), lambda g: (g, 0, 0)), pl.BlockSpec((1, gw), lambda g: (0, g))],
        out_specs=[pl.BlockSpec((lp, gw), lambda g: (0, g)), pl.BlockSpec((None, gw, gw), lambda g: (g, 0, 0)),
                   pl.BlockSpec((SUBLANES, gw), lambda g: (0, g))],
        out_shape=[SDS((lp, ng * gw), BF), SDS((ng, gw, gw), F32), SDS((SUBLANES, ng * gw), F32)],
        vmem_mib=48, name=name, comm=comm)(z, dmix, pool_w, pool_scale)


def _log_sigmoid(x):
    return jnp.minimum(x, 0.0) - jnp.log(1.0 + jnp.exp(-jnp.abs(x)))


def _fox_prep(z, bfp, fblk, name, comm=None):
    lp = z.shape[0]
    nb = lp // LANES

    def body(f_ref, b_ref, cum_ref):
        r = lax.broadcasted_iota(jnp.int32, (LANES, LANES), 0)
        c = lax.broadcasted_iota(jnp.int32, (LANES, LANES), 1)
        tri = (r >= c).astype(F32)
        carry = jnp.zeros((1, LANES), F32)
        for blk in range(nb):
            sl = slice(blk * LANES, (blk + 1) * LANES)
            lf = _log_sigmoid(f_ref[sl, :] + b_ref[...])
            cb = jnp.dot(tri, lf, preferred_element_type=F32, precision=lax.Precision.HIGHEST) + carry
            cum_ref[sl, :] = cb
            carry = cb[LANES - 1:LANES, :]

    return _call(
        body, grid=(1,),
        in_specs=[pl.BlockSpec((lp, LANES), lambda i: (0, fblk)), pl.BlockSpec((1, LANES), lambda i: (0, 0))],
        out_specs=pl.BlockSpec((lp, LANES), lambda i: (0, 0)), out_shape=SDS((lp, LANES), F32),
        vmem_mib=32, name=name, comm=comm)(z, bfp)


def _fox_bwd(z, bfp, dcum, fblk, name, comm=None):
    lp = z.shape[0]
    nb = lp // LANES

    def body(f_ref, b_ref, dc_ref, dz_ref, db_ref):
        r = lax.broadcasted_iota(jnp.int32, (LANES, LANES), 0)
        c = lax.broadcasted_iota(jnp.int32, (LANES, LANES), 1)
        tri = (r <= c).astype(F32)
        carry = jnp.zeros((1, LANES), F32)
        acc = jnp.zeros((SUBLANES, LANES), F32)
        for blk in range(nb - 1, -1, -1):
            sl = slice(blk * LANES, (blk + 1) * LANES)
            dlf = jnp.dot(tri, dc_ref[sl, :], preferred_element_type=F32, precision=lax.Precision.HIGHEST) + carry
            carry = dlf[0:1, :]
            df = dlf * jax.nn.sigmoid(-(f_ref[sl, :] + b_ref[...]))
            dz_ref[sl, :] = df.astype(BF)
            acc = acc + _rows8(df)
        db_ref[...] = acc

    return _call(
        body, grid=(1,),
        in_specs=[pl.BlockSpec((lp, LANES), lambda i: (0, fblk)), pl.BlockSpec((1, LANES), lambda i: (0, 0)),
                  pl.BlockSpec((lp, LANES), lambda i: (0, 0))],
        out_specs=[pl.BlockSpec((lp, LANES), lambda i: (0, 0)), pl.BlockSpec((SUBLANES, LANES), lambda i: (0, 0))],
        out_shape=[SDS((lp, LANES), BF), SDS((SUBLANES, LANES), F32)],
        vmem_mib=32, name=name, comm=comm)(z, bfp, dcum)


def _att_scores(q_ref, cum_ref, cumt_ref, qw_ref, kn_s, h, i, tq, lk):
    scale = 1.0 / (HEAD_DIM ** 0.5)
    q = q_ref[...]
    rq = _rstd(q)
    qhat = q * rq
    qn = (qhat * qw_ref[...]).astype(BF)
    s = _dot_nt(qn, kn_s[0:lk, :]) * scale
    lane = lax.broadcasted_iota(jnp.int32, (tq, LANES), 1)
    cq = jnp.sum(jnp.where(lane == h, cum_ref[...], 0.0), axis=1, keepdims=True)
    ck = cumt_ref[pl.ds(h, 1), 0:lk]
    s = s + (cq - ck)
    qpos = i * tq + lax.broadcasted_iota(jnp.int32, (tq, lk), 0)
    kpos = lax.broadcasted_iota(jnp.int32, (tq, lk), 1)
    s = jnp.where(qpos >= kpos, s, NEG_BIG)
    e = jnp.exp(s - jnp.max(s, axis=1, keepdims=True))
    p = e * (1.0 / jnp.sum(e, axis=1, keepdims=True))
    return p, qn, qhat, rq


def _per_query_tile(i, nq, tq, lp, fn):
    for t in range(nq):
        lk = min(lp, -(-((t + 1) * tq) // LANES) * LANES)
        pl.when(i == t)(functools.partial(fn, lk))


def _att_fwd(z, cum, cumt, qw, kw, n_heads, qblk0, tq, name, comm=None):
    lp = z.shape[0]
    nh = n_heads

    def body(q_ref, k_ref, v_ref, cum_ref, cumt_ref, qw_ref, kw_ref, o_ref, kn_s, vb_s):
        h, i = pl.program_id(0), pl.program_id(1)

        @pl.when(i == 0)
        def _():
            k = k_ref[...]
            kn_s[...] = (k * _rstd(k) * kw_ref[...]).astype(BF)
            vb_s[...] = v_ref[...].astype(BF)

        def tile(lk):
            p, _, _, _ = _att_scores(q_ref, cum_ref, cumt_ref, qw_ref, kn_s, h, i, tq, lk)
            o_ref[...] = _dot(p.astype(BF), vb_s[0:lk, :]).astype(BF)

        _per_query_tile(i, lp // tq, tq, lp, tile)

    vec = pl.BlockSpec((1, HEAD_DIM), lambda h, i: (0, 0))
    return _call(
        body, grid=(nh, lp // tq),
        in_specs=[pl.BlockSpec((tq, HEAD_DIM), lambda h, i: (i, qblk0 + h)),
                  pl.BlockSpec((lp, HEAD_DIM), lambda h, i: (0, qblk0 + nh + h)),
                  pl.BlockSpec((lp, HEAD_DIM), lambda h, i: (0, qblk0 + 2 * nh + h)),
                  pl.BlockSpec((tq, LANES), lambda h, i: (i, 0)),
                  pl.BlockSpec((nh, lp), lambda h, i: (0, 0)), vec, vec],
        out_specs=pl.BlockSpec((tq, HEAD_DIM), lambda h, i: (i, h)),
        out_shape=SDS((lp, nh * HEAD_DIM), BF),
        scratch_shapes=[pltpu.VMEM((lp, HEAD_DIM), BF), pltpu.VMEM((lp, HEAD_DIM), BF)],
        vmem_mib=48, name=name, comm=comm)(z, z, z, cum, cumt, qw, kw)


def _att_bwd(z, cum, cumt, qw, kw, dmix, n_heads, qblk0, oblk0, tq, name, comm=None):
    lp = z.shape[0]
    nh = n_heads
    nq = lp // tq
    scale = 1.0 / (HEAD_DIM ** 0.5)

    def body(q_ref, k_ref, v_ref, cum_ref, cumt_ref, qw_ref, kw_ref, do_ref,
             dq_ref, dk_ref, dv_ref, dck_ref, dqw_ref, dkw_ref,
             kn_s, vb_s, dkn_s, dv_s, dck_s):
        h, i = pl.program_id(0), pl.program_id(1)

        @pl.when((h == 0) & (i == 0))
        def _():
            dqw_ref[...] = jnp.zeros_like(dqw_ref)
            dkw_ref[...] = jnp.zeros_like(dkw_ref)

        @pl.when(i == 0)
        def _():
            k = k_ref[...]
            kn_s[...] = (k * _rstd(k) * kw_ref[...]).astype(BF)
            vb_s[...] = v_ref[...].astype(BF)
            dkn_s[...] = jnp.zeros_like(dkn_s)
            dv_s[...] = jnp.zeros_like(dv_s)
            dck_s[...] = jnp.zeros_like(dck_s)

        def tile(lk):
            p, qn, qhat, rq = _att_scores(q_ref, cum_ref, cumt_ref, qw_ref, kn_s, h, i, tq, lk)
            dob = do_ref[...]
            dp = _dot_nt(dob, vb_s[0:lk, :])
            ds = p * (dp - jnp.sum(p * dp, axis=1, keepdims=True))
            dsb = ds.astype(BF)
            dv_s[0:lk, :] += _dot_tn(p.astype(BF), dob)
            dkn_s[0:lk, :] += _dot_tn(dsb, qn)
            dck_s[:, 0:lk] += jnp.sum(ds, axis=0, keepdims=True)
            dqn = _dot(dsb, kn_s[0:lk, :]) * scale
            gq = dqn * qw_ref[...]
            dq_ref[...] = (rq * (gq - qhat * jnp.mean(gq * qhat, axis=-1, keepdims=True))).astype(BF)
            dqw_ref[...] += _rows8(dqn * qhat)

        _per_query_tile(i, nq, tq, lp, tile)

        @pl.when(i == nq - 1)
        def _():
            k = k_ref[...]
            rk = _rstd(k)
            khat = k * rk
            dkn = dkn_s[...] * scale
            gk = dkn * kw_ref[...]
            dk_ref[...] = (rk * (gk - khat * jnp.mean(gk * khat, axis=-1, keepdims=True))).astype(BF)
            dkw_ref[...] += _rows8(dkn * khat)
            dv_ref[...] = dv_s[...].astype(BF)
            dck_ref[...] = dck_s[...]

    vec = pl.BlockSpec((1, HEAD_DIM), lambda h, i: (0, 0))
    part = pl.BlockSpec((SUBLANES, LANES), lambda h, i: (0, 0))
    return _call(
        body, grid=(nh, nq),
        in_specs=[pl.BlockSpec((tq, HEAD_DIM), lambda h, i: (i, qblk0 + h)),
                  pl.BlockSpec((lp, HEAD_DIM), lambda h, i: (0, qblk0 + nh + h)),
                  pl.BlockSpec((lp, HEAD_DIM), lambda h, i: (0, qblk0 + 2 * nh + h)),
                  pl.BlockSpec((tq, LANES), lambda h, i: (i, 0)),
                  pl.BlockSpec((nh, lp), lambda h, i: (0, 0)), vec, vec,
                  pl.BlockSpec((tq, HEAD_DIM), lambda h, i: (i, oblk0 + h))],
        out_specs=[pl.BlockSpec((tq, HEAD_DIM), lambda h, i: (i, h)),
                   pl.BlockSpec((lp, HEAD_DIM), lambda h, i: (0, h)),
                   pl.BlockSpec((lp, HEAD_DIM), lambda h, i: (0, h)),
                   pl.BlockSpec((None, 1, lp), lambda h, i: (h, 0, 0)),
                   part, part],
        out_shape=[SDS((lp, nh * HEAD_DIM), BF)] * 3
        + [SDS((nh, 1, lp), F32), SDS((SUBLANES, LANES), F32), SDS((SUBLANES, LANES), F32)],
        scratch_shapes=[pltpu.VMEM((lp, HEAD_DIM), BF), pltpu.VMEM((lp, HEAD_DIM), BF),
                        pltpu.VMEM((lp, HEAD_DIM), F32), pltpu.VMEM((lp, HEAD_DIM), F32),
                        pltpu.VMEM((1, lp), F32)],
        vmem_mib=56, name=name, comm=comm)(z, z, z, cum, cumt, qw, kw, dmix)


def _adamw_math(w, g, m, v):
    m2 = ADAM_B1 * m + (1.0 - ADAM_B1) * g
    v2 = ADAM_B2 * v + (1.0 - ADAM_B2) * (g * g)
    m_hat = m2 / (1.0 - ADAM_B1 ** ADAM_STEP)
    v_hat = v2 / (1.0 - ADAM_B2 ** ADAM_STEP)
    delta = -ADAM_LR * (m_hat / (jnp.sqrt(v_hat) + ADAM_EPS) + ADAM_WD * w)
    return delta, m2, v2


def _adamw(g_in, w, m, v, name, comm=None):
    r, c = w.shape
    partial_sum = g_in.ndim == 3
    lane_padded = -(-c // LANES) * LANES
    tr = _largest_tile(r, max(16, MIB // (4 * lane_padded) // 16 * 16), 16)

    def body(g_ref, w_ref, m_ref, v_ref, go_ref, d_ref, mo_ref, vo_ref):
        if partial_sum:
            g = g_ref[0].astype(F32)
            for k in range(1, g_in.shape[0]):
                g = g + g_ref[k].astype(F32)
        else:
            g = g_ref[...]
        delta, m2, v2 = _adamw_math(w_ref[...], g, m_ref[...], v_ref[...])
        go_ref[...] = g
        d_ref[...] = delta
        mo_ref[...] = m2
        vo_ref[...] = v2

    blk = pl.BlockSpec((tr, c), lambda i: (i, 0))
    g_spec = pl.BlockSpec((g_in.shape[0], tr, c), lambda i: (0, i, 0)) if partial_sum else blk
    return _call(
        body, grid=(r // tr,), in_specs=[g_spec, blk, blk, blk], out_specs=[blk] * 4,
        out_shape=[SDS((r, c), F32)] * 4, vmem_mib=40, name=name, comm=comm)(g_in, w, m, v)


def _peer(x, y, c, k):
    return (1 - x if k & 4 else x, 1 - y if k & 2 else y, 1 - c if k & 1 else c)


_SIBLING = 1
_ICI_RELS = (2, 4, 6)


def _mesh_pos():
    return lax.axis_index("x"), lax.axis_index("y"), lax.axis_index("c")


def _sem_pair(sems, t, j, n_rel, scalars):
    if scalars:
        i = 2 * (t * n_rel + j)
        return sems[i], sems[i + 1]
    return sems[0].at[t, j], sems[1].at[t, j]


def _dev(pos):
    return 4 * pos[0] + 2 * pos[1] + pos[2]


def _gather_ici(shards, landing=None, rels=(_SIBLING,) + _ICI_RELS):
    n = len(shards)

    def remote(ins, outs, sems, arrival):
        x, y, c = _mesh_pos()
        dst = ins[n:] if landing is not None else outs
        cps = []
        for j, k in enumerate(rels):
            peer = _peer(x, y, c, k)
            slot = _dev(peer) if arrival else _dev((x, y, c))
            for t in range(n):
                send_sem, recv_sem = _sem_pair(sems, t, j, len(rels), landing is not None)
                cps.append(pltpu.make_async_remote_copy(
                    src_ref=ins[t], dst_ref=dst[t].at[slot], send_sem=send_sem, recv_sem=recv_sem,
                    device_id=peer, device_id_type=pl.DeviceIdType.MESH))
        return cps

    if landing is not None:
        def start_remote(ins, outs, sems):
            for cp in remote(ins, outs, sems, False):
                cp.start()

        def finish_remote(ins, outs, sems):
            for cp in remote(ins, outs, sems, True):
                cp.wait_recv()
            for cp in remote(ins, outs, sems, False):
                cp.wait_send()

        return _Comm(list(shards) + list(landing), [SDS(a.shape, a.dtype) for a in landing],
                     [pltpu.SemaphoreType.DMA(())] * (2 * n * len(rels)),
                     start_remote, finish_remote, aliases={n + t: t for t in range(n)})

    def local(ins, outs, sems):
        me = _dev(_mesh_pos())
        return [pltpu.make_async_copy(ins[t], outs[t].at[me], sems[2].at[t]) for t in range(n)]

    def start(ins, outs, sems):
        for cp in local(ins, outs, sems) + remote(ins, outs, sems, False):
            cp.start()

    def finish(ins, outs, sems):
        for cp in local(ins, outs, sems):
            cp.wait()
        for cp in remote(ins, outs, sems, True):
            cp.wait_recv()
        for cp in remote(ins, outs, sems, False):
            cp.wait_send()

    return _Comm(shards, [SDS((N_DEV,) + s.shape, s.dtype) for s in shards],
                 [pltpu.SemaphoreType.DMA((n, len(rels))), pltpu.SemaphoreType.DMA((n, len(rels))),
                  pltpu.SemaphoreType.DMA((n,))], start, finish)


def _gather_diagonal(zones):
    n = len(zones)

    def copies(ins, outs, sems, arrival):
        x, y, c = _mesh_pos()
        y_nb, x_nb, diag = _peer(x, y, c, 2), _peer(x, y, c, 4), _peer(x, y, c, 6)
        cps = []
        for j, (to, origin) in enumerate(((y_nb, x_nb), (x_nb, y_nb))):
            slot = _dev(diag) if arrival else _dev(origin)
            for t in range(n):
                half = ins[t].shape[1] // 2
                rows = ins[t].at[slot, pl.ds(j * half, half)]
                send_sem, recv_sem = _sem_pair(sems, t, j, 2, True)
                cps.append(pltpu.make_async_remote_copy(
                    src_ref=rows, dst_ref=rows, send_sem=send_sem, recv_sem=recv_sem,
                    device_id=to, device_id_type=pl.DeviceIdType.MESH))
        return cps

    def start(ins, outs, sems):
        for cp in copies(ins, outs, sems, False):
            cp.start()

    def finish(ins, outs, sems):
        for cp in copies(ins, outs, sems, True):
            cp.wait_recv()
        for cp in copies(ins, outs, sems, False):
            cp.wait_send()

    return _Comm(list(zones), [SDS(a.shape, a.dtype) for a in zones], [pltpu.SemaphoreType.DMA(())] * (4 * n),
                 start, finish, aliases={t: t for t in range(n)})


def _gather_fwd(partial):
    n = len(partial)

    def copies(ins, outs, sems, arrival):
        x, y, c = _mesh_pos()
        sibling = _peer(x, y, c, _SIBLING)
        cps = []
        for j, k in enumerate(_ICI_RELS):
            slot = _dev(_peer(x, y, c, k | _SIBLING if arrival else k))
            for t in range(n):
                cps.append(pltpu.make_async_remote_copy(
                    src_ref=ins[t].at[slot], dst_ref=outs[t].at[slot], send_sem=sems[0].at[t, j],
                    recv_sem=sems[1].at[t, j], device_id=sibling, device_id_type=pl.DeviceIdType.MESH))
        return cps

    def start(ins, outs, sems):
        for cp in copies(ins, outs, sems, False):
            cp.start()

    def finish(ins, outs, sems):
        for cp in copies(ins, outs, sems, True):
            cp.wait_recv()
        for cp in copies(ins, outs, sems, False):
            cp.wait_send()

    return _Comm(partial, [SDS(a.shape, a.dtype) for a in partial],
                 [pltpu.SemaphoreType.DMA((n, len(_ICI_RELS)))] * 2, start, finish,
                 aliases={t: t for t in range(n)})


def _scatter_sibling(slots):
    n = len(slots)

    def copies(ins, outs, sems):
        x, y, c = _mesh_pos()
        return [pltpu.make_async_remote_copy(
            src_ref=ins[t].at[:, 1 - c], dst_ref=outs[t], send_sem=sems[0].at[t], recv_sem=sems[1].at[t],
            device_id=_peer(x, y, c, _SIBLING), device_id_type=pl.DeviceIdType.MESH) for t in range(n)]

    def start(ins, outs, sems):
        for cp in copies(ins, outs, sems):
            cp.start()

    def finish(ins, outs, sems):
        for cp in copies(ins, outs, sems):
            cp.wait()

    return _Comm(slots, [SDS((s.shape[0],) + s.shape[2:], s.dtype) for s in slots],
                 [pltpu.SemaphoreType.DMA((n,))] * 2, start, finish)


def _scatter_ici(chip_sums, landing=None):
    n = len(chip_sums)

    def remote(ins, outs, sems, arrival):
        x, y, c = _mesh_pos()
        dst = ins[n:] if landing is not None else outs
        cps = []
        for j, k in enumerate(_ICI_RELS):
            peer = _peer(x, y, c, k)
            theirs, mine = 2 * peer[0] + peer[1], 2 * x + y
            for t in range(n):
                send_sem, recv_sem = _sem_pair(sems, t, j, len(_ICI_RELS), landing is not None)
                cps.append(pltpu.make_async_remote_copy(
                    src_ref=ins[t].at[theirs], dst_ref=dst[t].at[theirs if arrival else mine],
                    send_sem=send_sem, recv_sem=recv_sem,
                    device_id=peer, device_id_type=pl.DeviceIdType.MESH))
        return cps

    if landing is not None:
        def start_remote(ins, outs, sems):
            for cp in remote(ins, outs, sems, False):
                cp.start()

        def finish_remote(ins, outs, sems):
            for cp in remote(ins, outs, sems, True):
                cp.wait_recv()
            for cp in remote(ins, outs, sems, False):
                cp.wait_send()

        return _Comm(list(chip_sums) + list(landing), [SDS(a.shape, a.dtype) for a in landing],
                     [pltpu.SemaphoreType.DMA(())] * (2 * n * len(_ICI_RELS)), start_remote, finish_remote,
                     aliases={n + t: t for t in range(n)})

    def local(ins, outs, sems):
        x, y, _ = _mesh_pos()
        return [pltpu.make_async_copy(ins[t].at[2 * x + y], outs[t].at[2 * x + y], sems[2].at[t]) for t in range(n)]

    def start(ins, outs, sems):
        for cp in local(ins, outs, sems) + remote(ins, outs, sems, False):
            cp.start()

    def finish(ins, outs, sems):
        for cp in local(ins, outs, sems):
            cp.wait()
        for cp in remote(ins, outs, sems, True):
            cp.wait_recv()
        for cp in remote(ins, outs, sems, False):
            cp.wait_send()

    return _Comm(chip_sums, [SDS(a.shape, a.dtype) for a in chip_sums],
                 [pltpu.SemaphoreType.DMA((n, len(_ICI_RELS))), pltpu.SemaphoreType.DMA((n, len(_ICI_RELS))),
                  pltpu.SemaphoreType.DMA((n,))], start, finish)


def _chip_sum(slots, from_sibling, core, name):
    nq, _, r, c = slots.shape
    tr = _largest_tile(r, 1024, 16)

    def body(core_ref, a_ref, b_ref, o_ref):
        o_ref[...] = (a_ref[...].astype(F32) + b_ref[...].astype(F32)).astype(BF)

    return pl.pallas_call(
        body,
        grid_spec=pltpu.PrefetchScalarGridSpec(
            num_scalar_prefetch=1, grid=(nq, r // tr),
            in_specs=[pl.BlockSpec((None, None, tr, c), lambda q, i, core_ref: (q, core_ref[0], i, 0)),
                      pl.BlockSpec((None, tr, c), lambda q, i, core_ref: (q, i, 0))],
            out_specs=pl.BlockSpec((None, tr, c), lambda q, i, core_ref: (q, i, 0))),
        out_shape=SDS((nq, r, c), BF), compiler_params=pltpu.CompilerParams(vmem_limit_bytes=40 * MIB),
        name=name)(core, slots, from_sibling)


def _small_reduce(pack_g, meta_g, loss_scale, name):
    w = pack_g.shape[2]

    def body(p_ref, m_ref, tot_ref, meta_ref, loss_ref):
        acc = p_ref[0]
        macc = m_ref[0]
        for k in range(1, N_DEV):
            acc = acc + p_ref[k]
            macc = macc + m_ref[k]
        tot = jnp.sum(acc, axis=0, keepdims=True)
        tot_ref[...] = tot
        meta_ref[...] = macc
        loss_ref[...] = jnp.full((1, LANES), loss_scale * jnp.sum(tot[:, w - LANES:w]), F32)

    return pl.pallas_call(
        body, out_shape=[SDS((1, w), F32), SDS(meta_g.shape[1:], F32), SDS((1, LANES), F32)],
        compiler_params=pltpu.CompilerParams(vmem_limit_bytes=32 * MIB), name=name)(pack_g, meta_g)


def _local_step(x, target, sw, plan):
    s_len, d = x.shape
    n_heads, n_meta = plan.n_heads, plan.n_meta
    l = n_meta + s_len
    lp = -(-l // LANES) * LANES
    tm = _largest_tile(lp, 544, 16)
    tq = _largest_tile(lp, 272, 16)
    te = _largest_tile(lp, 272, 16)
    tmd = _largest_tile(d, 512, LANES)

    plan.at("start")
    x, target = plan.gate((x, target))
    zmeta, zpad = jnp.zeros((n_meta, d), F32), jnp.zeros((lp - l, d), F32)
    h0 = jnp.concatenate([zmeta, x, zpad], axis=0)
    tpad = jnp.concatenate([zmeta, target, zpad], axis=0)
    plan.at("landed", (h0, tpad))
    h0 = lax.dynamic_update_slice(h0, plan.weights("meta"), (0, 0))

    first_shards, last_shards = plan.ffn1_split()
    carry = _ffn_fwd_part(h0, sw["ffn1_norm"], *plan.weights("ffn1_landing"), first_shards, None, tm, "ffn1_fwd_a",
                          plan.order_tokens())
    plan.at("ffn1_mid", (carry[0],))
    wg1, wu1, wd1 = plan.weights("ffn1")
    h1, a1, b1 = _ffn_fwd_part(h0, sw["ffn1_norm"], wg1, wu1, wd1, last_shards, carry, tm, "ffn1_fwd_b",
                               plan.order_tokens())
    u1 = carry[3]
    fs = wg1.shape[1]
    plan.at("after_ffn1_fwd", (h1,))
    win, pw, wout = plan.weights("mix")
    nz = win.shape[1]
    p_w = sw["pool_scale"].shape[1]
    npb = p_w // LANES
    fblk = nz // LANES - 1
    tnz = _largest_tile(nz, 1408, LANES)
    qw, kw, bfp, ps = sw["q_norm"], sw["k_norm"], sw["b_forget"], sw["pool_scale"]
    z, u2 = _norm_matmul(h1, sw["mix_norm"], win, te, nz, "mix_in", plan.comm("mix_in"))
    plan.at("after_mix_in", (u2,))
    cum = _fox_prep(z, bfp, fblk, "fox_prep")
    cumt = cum[:, :n_heads].T
    pool_o = _pool_fwd(z, pw, ps, "pool_fwd")
    att_o = _att_fwd(z, cum, cumt, qw, kw, n_heads, npb, tq, "att_fwd", plan.comm("att_fwd"))
    plan.at("after_att_fwd", (att_o,))
    h2 =_out_proj(h1, pool_o, att_o, wout, tm, "out_proj", plan.comm("out_proj"))
    wg2, wu2, wd2 = plan.weights("ffn2")
    h3, a2, b2, u3 = _ffn_fwd(h2, sw["ffn2_norm"], wg2, wu2, wd2, tm, "ffn2_fwd", plan.comm("ffn2_fwd"))
    dy, dob3, lsq = _loss_head(h3, tpad, n_meta, l, te, "loss_head")

    da2, db2, hid2 = _ffn_bwd_act(dob3, a2, b2, wd2, tm, "ffn2_bwd_act", plan.comm("ffn2_bwd_act"))
    du3 = _ffn_bwd_du(da2, db2, wg2, wu2, tm, "ffn2_bwd_du", plan.comm("ffn2_bwd_du"))
    dh2, dh2b, dn2 = _rms_bwd(du3, h2, sw["ffn2_norm"], dy, 1.0, te, "ffn2_rms_bwd")
    plan.grad("ffn2_w_gate", _matmul_tn(da2, u3, fs, d, "ffn2_dwg", plan.comm("ffn2_dwg")))
    plan.grad("ffn2_w_up", _matmul_tn(db2, u3, fs, d, "ffn2_dwu", plan.comm("ffn2_dwu")))
    plan.grad("ffn2_w_down", _matmul_tn(hid2, dob3, fs, d, "ffn2_dwd", plan.comm("ffn2_dwd")))
    plan.at("after_ffn2_dwd")

    dmix = _matmul_nt(dh2b, wout, tm, d, BF, "out_proj_bwd", plan.comm("out_proj_bwd"))
    plan.at("after_out_proj_bwd")
    tmp = _largest_tile(p_w, 512, LANES)
    plan.grad("w_out", jnp.concatenate([_matmul_tn(pool_o, dh2b, tmp, d, "dwout_pool"),
                                        _matmul_tn(att_o, dh2b, tmp, d, "dwout_att")], axis=0))
    dzp, dpw, dps = _pool_bwd(z, dmix, pw, ps, "pool_bwd")
    plan.grad("pool_w", dpw)
    plan.at("before_att_bwd")
    dq, dk, dv, dck, dqw, dkw = _att_bwd(z, cum, cumt, qw, kw, dmix, n_heads, npb, npb, tq, "att_bwd",
                                              plan.comm("att_bwd"))
    dcum = -dck[:, 0, :].T
    dcum = jnp.pad(dcum, ((0, 0), (0, LANES - n_heads)))
    dzf, dbf = _fox_bwd(z, bfp, dcum, fblk, "fox_bwd")
    dz = jnp.concatenate([dzp, dq, dk, dv, dzf], axis=1)
    plan.grad("w_in", _matmul_tn(u2, dz, tmd, tnz, "dwin", plan.comm("dwin")))
    du2 = _matmul_nt(dz, win, te, nz, F32, "mix_in_bwd", plan.comm("mix_in_bwd"))
    plan.at("before_ffn1_bwd_dx")
    dh1, dob1, dnm = _rms_bwd(du2, h1, sw["mix_norm"], dh2, 0.5, te, "mix_rms_bwd")

    da1, db1, hid1 = _ffn_bwd_act(dob1, a1, b1, wd1, tm, "ffn1_bwd_act", plan.comm("ffn1_bwd_act"))
    plan.grad("ffn1_w_gate", _matmul_tn(da1, u1, fs, d, "ffn1_dwg", plan.comm("ffn1_dwg")))
    plan.grad("ffn1_w_up", _matmul_tn(db1, u1, fs, d, "ffn1_dwu", plan.comm("ffn1_dwu")))
    plan.at("before_ffn1_dwd")
    plan.grad("ffn1_w_down", _matmul_tn(hid1, dob1, fs, d, "ffn1_dwd", plan.comm("ffn1_dwd")))
    plan.at("after_ffn1_dwd")
    du1 = _ffn_bwd_du(da1, db1, wg1, wu1, tm, "ffn1_bwd_du", plan.comm("ffn1_bwd_du"))
    dh0, _, dn1 = _rms_bwd(du1, h0, sw["ffn1_norm"], dh1, 1.0, te, "ffn1_rms_bwd", plan.comm("ffn1_rms_bwd"))

    small = [dn1, dnm, dn2, dps, dqw, dkw, dbf, lsq]
    return dh0[n_meta:l], dh0[:n_meta], small


_BIG = ("ffn1_w_gate", "ffn1_w_up", "ffn1_w_down", "w_in", "pool_w", "w_out", "ffn2_w_gate", "ffn2_w_up", "ffn2_w_down")
_SMALL = ("ffn1_norm", "mix_norm", "ffn2_norm", "pool_scale", "q_norm", "k_norm", "b_forget")
_ORDER = ("meta_tokens", "ffn1_norm", "ffn1_w_gate", "ffn1_w_up", "ffn1_w_down", "mix_norm", "w_in", "b_forget",
          "q_norm", "k_norm", "pool_w", "pool_scale", "w_out", "ffn2_norm", "ffn2_w_gate", "ffn2_w_up", "ffn2_w_down")


_FFN1 = ("ffn1_w_gate", "ffn1_w_up", "ffn1_w_down")
_FFN2 = ("ffn2_w_gate", "ffn2_w_up", "ffn2_w_down")
_MIX = ("w_in", "pool_w", "w_out")

_RIDES = {
    "out_proj": (("g2", _FFN2),),
    "ffn2_dwu": (("s1", ("ffn2_w_gate",)),),
    "ffn2_dwd": (("s1", ("ffn2_w_up",)),),
    "out_proj_bwd": (("s1", ("ffn2_w_down",)),),
    "mix_in_bwd": (("s1", _MIX),),
    "ffn1_dwu": (("s1", ("ffn1_w_gate",)),),
    "ffn1_dwd": (("s1", ("ffn1_w_up",)),),
    "ffn1_bwd_du": (("s1", ("ffn1_w_down",)),),
}
_META = ("meta_tokens",)
_POINTS = {
    "start": (("start", "gm", _META), ("start", "g1a", _FFN1), ("gate", _MIX + _FFN2), ("prepare", "g1", _MIX),
              ("prepare", "g1a", _FFN2)),
    "landed": (("wait", "gm", _META), ("wait", "g1a", _FFN1), ("start", "g1b", _FFN1), ("start", "g1", _MIX),
               ("start", "g1a", _FFN2)),
    "ffn1_mid": (("wait", "g1b", _FFN1), ("alone", "g2", _FFN1)),
    "after_ffn1_fwd": (("wait", "g1", _MIX), ("alone", "g2", _MIX)),
    "after_mix_in": (("wait", "g1a", _FFN2), ("start", "g1b", _FFN2)),
    "after_att_fwd": (("wait", "g1b", _FFN2),),
    "after_ffn2_dwd": (("sum", ("ffn2_w_gate",)), ("start", "s2", ("ffn2_w_gate",))),
    "after_out_proj_bwd": (("sum", ("ffn2_w_up",)), ("start", "s2", ("ffn2_w_up",))),
    "before_att_bwd": (("sum", ("ffn2_w_down",)), ("start", "s2", ("ffn2_w_down",))),
    "before_ffn1_bwd_dx": (("sum", _MIX), ("start", "s2", _MIX)),
    "before_ffn1_dwd": (("sum", ("ffn1_w_gate",)), ("start", "s2", ("ffn1_w_gate",))),
    "after_ffn1_dwd": (("sum", ("ffn1_w_up",)), ("start", "s2", ("ffn1_w_up",))),
    "after_ffn1_rms_bwd": (("sum", ("ffn1_w_down",)), ("start", "s2", ("ffn1_w_down",))),
    "before_adamw_ffn2_w_gate": (("wait", "s2", ("ffn2_w_gate",)),),
    "before_adamw_ffn2_w_up": (("wait", "s2", ("ffn2_w_up",)),),
    "before_adamw_ffn2_w_down": (("wait", "s2", ("ffn2_w_down",)),),
    "before_adamw_w_in": (("wait", "s2", _MIX),),
    "before_adamw_ffn1_w_gate": (("wait", "s2", ("ffn1_w_gate",)),),
    "before_adamw_ffn1_w_up": (("wait", "s2", ("ffn1_w_up",)),),
    "before_adamw_ffn1_w_down": (("wait", "s2", ("ffn1_w_down",)),),
}


def _own_slot_filled(block, slot, n_slots):
    zone = lax.empty((n_slots,) + block.shape, block.dtype)
    return lax.dynamic_update_slice(zone, block[None], (slot,) + (0,) * block.ndim)


class _MeshPlan:
    def __init__(self, raw, pos, d, d_in, n_heads):
        self.raw, self.pos = dict(raw), pos
        self.core = pos[2].astype(jnp.int32).reshape(1)
        self.d, self.d_in, self.n_heads, self.n_meta = d, d_in, n_heads, raw["meta_tokens"].shape[0]
        self.partial, self.full, self.slots, self.from_sibling, self.chip_sum, self.received = {}, {}, {}, {}, {}, {}
        self.partial_a, self.pending, self.prepared, self.started, self.tokens = {}, [], {}, {}, []

    def gate(self, arrays):
        gated = lax.optimization_barrier((self.tokens[-1], tuple(arrays)))
        self.tokens[-1] = gated[0]
        return gated[1]

    def _phase(self, kind, names):
        src, dst, make = {"g2": (self.partial, self.full, _gather_fwd),
                          "s1": (self.slots, self.from_sibling, _scatter_sibling),
                          "s2": (self.chip_sum, self.received, _scatter_ici)}[kind]
        op = make([src[n] for n in names])
        self.pending.append((op, dst, names))
        return op

    def _settle(self):
        for op, dst, names in self.pending:
            dst.update(zip(names, op.results))
        self.pending = []

    def _prepare(self, kind, names):
        x, y, c = self.pos
        if kind in ("g1", "g1a", "gm"):
            blocks = [self.raw[n] if kind == "gm" else _as2d(n, self.raw[n]).astype(BF) for n in names]
            rels = {"g1": (_SIBLING,) + _ICI_RELS, "g1a": (_SIBLING,) + _ICI_RELS[:2], "gm": tuple(range(1, N_DEV))}[kind]
            op = _gather_ici(blocks, [_own_slot_filled(b, 4 * x + 2 * y + c, N_DEV) for b in blocks], rels)
        elif kind == "g1b":
            op = _gather_diagonal([self.partial_a[n] for n in names])
        else:
            sums = [self.chip_sum[n] for n in names]
            mine = [lax.dynamic_index_in_dim(s, 2 * x + y, 0, keepdims=False) for s in sums]
            op = _scatter_ici(sums, [_own_slot_filled(b, 2 * x + y, N_DEV // 2) for b in mine])
        self.prepared[(kind, names)] = op

    def _start(self, kind, names):
        if (kind, names) not in self.prepared:
            self._prepare(kind, names)
        self._launch((kind, names), self.prepared.pop((kind, names)), "_".join(("start", kind, names[0])))

    def _launch(self, key, op, name):
        if self.tokens:
            op.arrs = list(self.gate(op.arrs))
        self.started[key], token = _split_start(op, name)
        self.tokens.append(token)

    def start_small_gather(self, arrays):
        x, y, c = self.pos
        zones = [_own_slot_filled(a, 4 * x + 2 * y + c, N_DEV) for a in arrays]
        self._launch("small", _gather_ici(list(arrays), zones, rels=tuple(range(1, N_DEV))), "start_gather_small")

    def wait_small_gather(self, afters):
        return _split_wait(self.started.pop("small"), afters, "wait_gather_small")

    def _wait(self, kind, names, afters):
        afters = list(afters) + [a for op in self.prepared.values() for a in op.arrs[len(op.arrs) // 2:]]
        landed = _split_wait(self.started.pop((kind, names)), afters, "_".join(("wait", kind, names[0])))
        {"g1": self.partial, "g1a": self.partial_a, "g1b": self.partial, "gm": self.partial,
         "s2": self.received}[kind].update(zip(names, landed))

    def ffn1_split(self):
        x, y, c = self.pos
        first = [(x, y, c), _peer(x, y, c, 1), _peer(x, y, c, 4), _peer(x, y, c, 2)]
        last = [_peer(x, y, c, 6), _peer(x, y, c, 5), _peer(x, y, c, 3), _peer(x, y, c, 7)]
        return tuple(jnp.stack([_dev(p) for p in part]).astype(jnp.int32) for part in (first, last))

    def order_tokens(self):
        tokens, self.tokens = self.tokens, []
        return tokens

    def comm(self, kernel_name):
        self._settle()
        ops = [self._phase(kind, names) for kind, names in _RIDES.get(kernel_name, ())]
        if self.tokens:
            ops.append(_Comm(self.tokens, [], [], lambda *a: None, lambda *a: None))
            self.tokens = []
        return _merge_comm(ops)

    def at(self, point, after=()):
        for step in _POINTS.get(point, ()):
            self._settle()
            if step[0] == "alone":
                _comm_alone(self._phase(step[1], step[2]), "_".join((step[1], point)))
            elif step[0] == "start":
                self._start(step[1], step[2])
            elif step[0] == "prepare":
                self._prepare(step[1], step[2])
            elif step[0] == "gate":
                self.raw.update(zip(step[1], self.gate([self.raw[n] for n in step[1]])))
            elif step[0] == "wait":
                self._wait(step[1], step[2], tuple(after) + tuple(self.tokens[-1:]))
            else:
                for n in step[1]:
                    pair = [pltpu.with_memory_space_constraint(a, pltpu.HBM)
                            for a in (self.slots[n], self.from_sibling[n])]
                    self.chip_sum[n] = _chip_sum(*pair, self.core, "chip_sum_" + n)

    def weights(self, group):
        self._settle()
        f, d = self.full, self.d
        if group == "meta":
            g = self.partial["meta_tokens"]
            return g.transpose(1, 0, 2).reshape(g.shape[1], d)
        if group == "ffn1_landing":
            return tuple(self.started[("g1b", _FFN1)][2])
        if group == "ffn1":
            return tuple(f[n] for n in _FFN1)
        if group == "ffn2":
            return tuple(f[n] for n in _FFN2)
        n_main = self.d_in - self.n_heads
        win = f["w_in"].transpose(1, 0, 2).reshape(d, self.d_in)
        win = jnp.concatenate([win[:, :n_main], jnp.pad(win[:, n_main:], ((0, 0), (0, LANES - self.n_heads)))], axis=1)
        pw = f["pool_w"]
        gw = pw.shape[2]
        pw = pw.reshape(N_DEV, -1, gw // N_DEV, gw).transpose(1, 0, 2, 3).reshape(-1, gw, gw)
        return win, pw, f["w_out"].reshape(-1, d)

    def grad(self, name, g):
        d = self.d
        if name == "w_in":
            g = g[:, :self.d_in].reshape(d, N_DEV, -1).transpose(1, 0, 2)
        elif name == "pool_w":
            ng, gw = g.shape[0], g.shape[2]
            g = g.astype(BF).reshape(ng, N_DEV, -1, gw).transpose(1, 0, 2, 3).reshape(N_DEV, -1, gw)
        elif name == "w_out":
            g = g.reshape(N_DEV, -1, d)
        self.slots[name] = g.reshape((N_DEV // 2, 2) + g.shape[1:])

    def gradient_parts(self, name):
        self._settle()
        return self.received[name]


_TRANSPOSED = ("ffn1_w_gate", "ffn1_w_up", "ffn2_w_gate", "ffn2_w_up")


def _as2d(name, a):
    return a[0].T if name in _TRANSPOSED else a.reshape(-1, a.shape[-1])


def _from2d(name, a2d, shape):
    return a2d.T.reshape(shape) if name in _TRANSPOSED else a2d.reshape(shape)


def kernel(x, meta_tokens, ffn1_norm, ffn1_w_gate, ffn1_w_up, ffn1_w_down, mix_norm, w_in, b_forget, q_norm, k_norm, pool_w, pool_scale, w_out, ffn2_norm, ffn2_w_gate, ffn2_w_up, ffn2_w_down, loss_target, m_meta_tokens, m_ffn1_norm, m_ffn1_w_gate, m_ffn1_w_up, m_ffn1_w_down, m_mix_norm, m_w_in, m_b_forget, m_q_norm, m_k_norm, m_pool_w, m_pool_scale, m_w_out, m_ffn2_norm, m_ffn2_w_gate, m_ffn2_w_up, m_ffn2_w_down, v_meta_tokens, v_ffn1_norm, v_ffn1_w_gate, v_ffn1_w_up, v_ffn1_w_down, v_mix_norm, v_w_in, v_b_forget, v_q_norm, v_k_norm, v_pool_w, v_pool_scale, v_w_out, v_ffn2_norm, v_ffn2_w_gate, v_ffn2_w_up, v_ffn2_w_down):
    w = dict(meta_tokens=meta_tokens, ffn1_norm=ffn1_norm, ffn1_w_gate=ffn1_w_gate, ffn1_w_up=ffn1_w_up,
             ffn1_w_down=ffn1_w_down, mix_norm=mix_norm, w_in=w_in, b_forget=b_forget, q_norm=q_norm, k_norm=k_norm,
             pool_w=pool_w, pool_scale=pool_scale, w_out=w_out, ffn2_norm=ffn2_norm, ffn2_w_gate=ffn2_w_gate,
             ffn2_w_up=ffn2_w_up, ffn2_w_down=ffn2_w_down)
    m = dict(meta_tokens=m_meta_tokens, ffn1_norm=m_ffn1_norm, ffn1_w_gate=m_ffn1_w_gate, ffn1_w_up=m_ffn1_w_up,
             ffn1_w_down=m_ffn1_w_down, mix_norm=m_mix_norm, w_in=m_w_in, b_forget=m_b_forget, q_norm=m_q_norm,
             k_norm=m_k_norm, pool_w=m_pool_w, pool_scale=m_pool_scale, w_out=m_w_out, ffn2_norm=m_ffn2_norm,
             ffn2_w_gate=m_ffn2_w_gate, ffn2_w_up=m_ffn2_w_up, ffn2_w_down=m_ffn2_w_down)
    v = dict(meta_tokens=v_meta_tokens, ffn1_norm=v_ffn1_norm, ffn1_w_gate=v_ffn1_w_gate, ffn1_w_up=v_ffn1_w_up,
             ffn1_w_down=v_ffn1_w_down, mix_norm=v_mix_norm, w_in=v_w_in, b_forget=v_b_forget, q_norm=v_q_norm,
             k_norm=v_k_norm, pool_w=v_pool_w, pool_scale=v_pool_scale, w_out=v_w_out, ffn2_norm=v_ffn2_norm,
             ffn2_w_gate=v_ffn2_w_gate, ffn2_w_up=v_ffn2_w_up, ffn2_w_down=v_ffn2_w_down)

    d = x.shape[-1]
    n_heads = b_forget.shape[-1]
    pos = (lax.axis_index("x"), lax.axis_index("y"), lax.axis_index("c"))
    me = 4 * pos[0] + 2 * pos[1] + pos[2]

    raw = {k: w[k] for k in _BIG}
    raw["meta_tokens"] = meta_tokens
    plan = _MeshPlan(raw, pos, d, N_DEV * w_in.shape[-1], n_heads)
    sw = {k: w[k] for k in _SMALL}
    sw["b_forget"] = jnp.pad(b_forget, ((0, 0), (0, LANES - n_heads)))
    dx, dmeta, small = _local_step(x[0], loss_target[0], sw, plan)

    res = {}
    last = dx

    plan.start_small_gather([jnp.concatenate(small, axis=1), dmeta])
    plan.at("after_ffn1_rms_bwd")

    def update_shards(names):
        nonlocal last
        for k in names:
            plan.at("before_adamw_" + k, (last,))
            operands = [pltpu.with_memory_space_constraint(a, pltpu.HBM) for a in
                        (plan.gradient_parts(k), _as2d(k, w[k]), _as2d(k, m[k]), _as2d(k, v[k]))]
            res[k] = _adamw(*operands, "adamw_" + k, plan.comm("adamw_" + k))
            last = res[k][0]

    update_shards(_FFN2 + _MIX + ("ffn1_w_gate", "ffn1_w_up"))

    pack_g, meta_g = plan.wait_small_gather((last,))
    tot, dmeta_tot, loss_row = _small_reduce(pack_g, meta_g, 0.5 / d, "small_reduce")

    mcols = meta_tokens.shape[1]
    g_meta = lax.dynamic_slice_in_dim(dmeta_tot, me * mcols, mcols, axis=1)
    res["meta_tokens"] = _adamw(g_meta, meta_tokens, m_meta_tokens, v_meta_tokens, "adamw_meta_tokens")

    def packed(src):
        return jnp.concatenate([src[k] for k in _SMALL[:-1]] + [jnp.pad(src["b_forget"], ((0, 0), (0, LANES - n_heads)))],
                               axis=1)

    wp = packed(w)
    sm = _adamw(tot[:, :wp.shape[1]], wp, packed(m), packed(v), "adamw_small")
    off = 0
    for k in _SMALL:
        width = w[k].shape[1]
        res[k] = tuple(o[:, off:off + width] for o in sm)
        off += width if k != "b_forget" else LANES

    last = sm[0]
    update_shards(("ffn1_w_down",))

    outs =[loss_row[0, 0], dx[None]]
    for idx in range(4):
        outs += [_from2d(k, res[k][idx], w[k].shape) for k in _ORDER]
    return tuple(outs)
```

```python
import functools

import jax
import jax.numpy as jnp
from jax import lax
from jax.experimental import pallas as pl
from jax.experimental.pallas import tpu as pltpu

F32 = jnp.float32
BF = jnp.bfloat16
SDS = jax.ShapeDtypeStruct

N_DEV = 8
LANES = 128
SUBLANES = 8
HEAD_DIM = 128
POOL_WINDOWS = (2, 4, 8, 16)
RMS_EPS = 1e-6
NEG_BIG = -1e30
MIB = 1024 * 1024

ADAM_LR = 0.001
ADAM_B1 = 0.9
ADAM_B2 = 0.999
ADAM_EPS = 1e-08
ADAM_WD = 0.01
ADAM_STEP = 10


class _Comm:
    def __init__(self, arrs, out_shape, sems, start, finish, aliases=None):
        self.arrs, self.out_shape, self.sems = list(arrs), list(out_shape), list(sems)
        self.start, self.finish, self.aliases = start, finish, dict(aliases or {})
        self.results = None


def _merge_comm(ops):
    ops = [op for op in ops if op is not None]
    if not ops:
        return None
    na, no, ns = [0], [0], [0]
    for op in ops:
        na.append(na[-1] + len(op.arrs))
        no.append(no[-1] + len(op.out_shape))
        ns.append(ns[-1] + len(op.sems))

    def parts(i, ins, outs, sems):
        return ins[na[i]:na[i + 1]], outs[no[i]:no[i + 1]], sems[ns[i]:ns[i + 1]]

    def start(ins, outs, sems):
        for i, op in enumerate(ops):
            op.start(*parts(i, ins, outs, sems))

    def finish(ins, outs, sems):
        for i, op in enumerate(ops):
            op.finish(*parts(i, ins, outs, sems))

    aliases = {}
    for i, op in enumerate(ops):
        for a, o in op.aliases.items():
            aliases[na[i] + a] = no[i] + o
    merged = _Comm([a for op in ops for a in op.arrs], [s for op in ops for s in op.out_shape],
                   [s for op in ops for s in op.sems], start, finish, aliases)
    merged.children = (ops, no)
    return merged


def _deliver(comm, results):
    comm.results = list(results)
    if hasattr(comm, "children"):
        ops, no = comm.children
        for i, op in enumerate(ops):
            _deliver(op, results[no[i]:no[i + 1]])


def _call(body, *, grid, in_specs, out_specs, out_shape, scratch_shapes=(), vmem_mib, name, comm=None):
    single = not isinstance(out_shape, (list, tuple))
    out_specs = [out_specs] if single else list(out_specs)
    out_shape = [out_shape] if single else list(out_shape)
    in_specs, scratch_shapes = list(in_specs), list(scratch_shapes)
    params = pltpu.CompilerParams(dimension_semantics=("arbitrary",) * len(grid), vmem_limit_bytes=vmem_mib * MIB)
    n_in, n_out, n_scr = len(in_specs), len(out_specs), len(scratch_shapes)

    def run(*args):
        if comm is None:
            res = pl.pallas_call(body, grid=grid, in_specs=in_specs, out_specs=out_specs, out_shape=out_shape,
                                 scratch_shapes=scratch_shapes, compiler_params=params, name=name)(*args)
            return res[0] if single else res
        ci, co = len(comm.arrs), len(comm.out_shape)

        def with_comm(*refs):
            ins, cins = refs[:n_in], refs[n_in:n_in + ci]
            o0 = n_in + ci
            outs, couts = refs[o0:o0 + n_out], refs[o0 + n_out:o0 + n_out + co]
            s0 = o0 + n_out + co
            scr, csems = refs[s0:s0 + n_scr], refs[s0 + n_scr:]
            ids = [pl.program_id(a) for a in range(len(grid))]
            first = functools.reduce(jnp.logical_and, [i == 0 for i in ids])
            last = functools.reduce(jnp.logical_and, [i == g - 1 for i, g in zip(ids, grid)])

            @pl.when(first)
            def _():
                comm.start(cins, couts, csems)

            body(*ins, *outs, *scr)

            @pl.when(last)
            def _():
                comm.finish(cins, couts, csems)

        anyspec = pl.BlockSpec(memory_space=pl.ANY)
        res = pl.pallas_call(
            with_comm, grid=grid, in_specs=in_specs + [anyspec] * ci, out_specs=out_specs + [anyspec] * co,
            out_shape=out_shape + comm.out_shape, scratch_shapes=scratch_shapes + comm.sems,
            input_output_aliases={n_in + a: n_out + o for a, o in comm.aliases.items()},
            compiler_params=params, name=name)(*args, *comm.arrs)
        _deliver(comm, res[n_out:])
        return res[0] if single else res[:n_out]

    return run


def _comm_alone(comm, name):
    def body(*refs):
        ci, co = len(comm.arrs), len(comm.out_shape)
        ins, outs, sems = refs[:ci], refs[ci:ci + co], refs[ci + co:]
        comm.start(ins, outs, sems)
        comm.finish(ins, outs, sems)

    anyspec = pl.BlockSpec(memory_space=pl.ANY)
    res = pl.pallas_call(
        body, in_specs=[anyspec] * len(comm.arrs), out_specs=[anyspec] * len(comm.out_shape),
        out_shape=comm.out_shape, scratch_shapes=comm.sems, input_output_aliases=comm.aliases, name=name)(*comm.arrs)
    _deliver(comm, res)


def _split_start(comm, name):
    na, ns = len(comm.arrs), len(comm.sems)

    def body(*refs):
        comm.start(refs[:na], None, refs[na:na + ns])
        token = refs[-1]
        token[...] = jnp.zeros_like(token)

    hbm = pl.BlockSpec(memory_space=pltpu.HBM)
    res = pl.pallas_call(
        body, name=name,
        out_shape=tuple(comm.sems) + tuple(pltpu.HBM(a.shape, a.dtype) for a in comm.arrs)
        + (SDS((SUBLANES, LANES), F32),),
        in_specs=[hbm] * na,
        out_specs=[pl.BlockSpec(memory_space=pltpu.SEMAPHORE)] * ns + [hbm] * na + [pl.BlockSpec(memory_space=pltpu.VMEM)],
        input_output_aliases={i: ns + i for i in range(na)},
        compiler_params=pltpu.CompilerParams(has_side_effects=pltpu.SideEffectType.DATAFLOW_SIDE_EFFECTING),
    )(*[pltpu.with_memory_space_constraint(a, pltpu.HBM) for a in comm.arrs])
    return (comm, res[:ns], res[ns:ns + na]), res[-1]


def _split_wait(started, afters, name):
    comm, sems, thru = started
    na, ns = len(thru), len(sems)
    afters = list(afters)

    def body(*refs):
        comm.finish(refs[:na], None, refs[na:na + ns])

    hbm = pl.BlockSpec(memory_space=pltpu.HBM)
    res = pl.pallas_call(
        body, name=name, out_shape=tuple(pltpu.HBM(a.shape, a.dtype) for a in thru),
        in_specs=[hbm] * na + [pl.BlockSpec(memory_space=pltpu.SEMAPHORE)] * ns
        + [pl.BlockSpec(memory_space=pl.ANY)] * len(afters),
        out_specs=[hbm] * na, input_output_aliases={i: i for i in range(na)},
        compiler_params=pltpu.CompilerParams(has_side_effects=pltpu.SideEffectType.DATAFLOW_SIDE_EFFECTING),
    )(*thru, *sems, *afters)
    return res[na - len(comm.out_shape):]


def _largest_tile(n, cap, mult):
    if n <= cap:
        return n
    best = None
    for t in range(mult, cap + 1, mult):
        if n % t == 0:
            best = t
    assert best is not None, (n, cap, mult)
    return best


def _dot(a, b):
    return jnp.dot(a, b, preferred_element_type=F32)


def _dot_nt(a, b):
    return lax.dot_general(a, b, (((1,), (1,)), ((), ())), preferred_element_type=F32)


def _dot_tn(a, b):
    return lax.dot_general(a, b, (((0,), (0,)), ((), ())), preferred_element_type=F32)


def _rows8(x):
    t, c = x.shape
    return jnp.sum(x.reshape(t // SUBLANES, SUBLANES, c), axis=0)


def _rstd(x):
    return lax.rsqrt(jnp.mean(x * x, axis=-1, keepdims=True) + RMS_EPS)


def _ffn_fwd(h, g, wg, wu, wd, tm, name, comm=None):
    lp, d = h.shape
    ns, fs, _ = wg.shape

    def body(h_ref, g_ref, wg_ref, wu_ref, wd_ref, out_ref, a_ref, b_ref, u_ref, acc_ref):
        j = pl.program_id(1)

        @pl.when(j == 0)
        def _():
            hh = h_ref[...]
            u_ref[...] = (hh * _rstd(hh) * g_ref[...]).astype(BF)
            acc_ref[...] = jnp.zeros_like(acc_ref)

        u = u_ref[...]
        a = _dot_nt(u, wg_ref[...])
        b = _dot_nt(u, wu_ref[...])
        a_ref[...] = a.astype(BF)
        b_ref[...] = b.astype(BF)
        hid = (a * jax.nn.sigmoid(a) * b).astype(BF)
        acc_ref[...] += _dot(hid, wd_ref[...])

        @pl.when(j == ns - 1)
        def _():
            out_ref[...] = h_ref[...] + 0.5 * acc_ref[...]

    row = pl.BlockSpec((tm, d), lambda i, j: (i, 0))
    act = pl.BlockSpec((None, tm, fs), lambda i, j: (j, i, 0))
    return _call(
        body, grid=(lp // tm, ns),
        in_specs=[row, pl.BlockSpec((1, d), lambda i, j: (0, 0)),
                  pl.BlockSpec((None, fs, d), lambda i, j: (j, 0, 0)),
                  pl.BlockSpec((None, fs, d), lambda i, j: (j, 0, 0)),
                  pl.BlockSpec((None, fs, d), lambda i, j: (j, 0, 0))],
        out_specs=[row, act, act, row],
        out_shape=[SDS((lp, d), F32), SDS((ns, lp, fs), BF), SDS((ns, lp, fs), BF), SDS((lp, d), BF)],
        scratch_shapes=[pltpu.VMEM((tm, d), F32)],
        vmem_mib=56, name=name, comm=comm)(h, g, wg, wu, wd)


def _ffn_fwd_two_calls(h, g, wg, wu, wd, tm, name, comm=None):
    lp, d = h.shape
    ns, fs, _ = wg.shape
    row = pl.BlockSpec((tm, d), lambda i, j: (i, 0))
    act = pl.BlockSpec((None, tm, fs), lambda i, j: (j, i, 0))
    wsp = pl.BlockSpec((None, fs, d), lambda i, j: (j, 0, 0))

    def act_body(h_ref, g_ref, wg_ref, wu_ref, a_ref, b_ref, hid_ref, u_ref):
        @pl.when(pl.program_id(1) == 0)
        def _():
            hh = h_ref[...]
            u_ref[...] = (hh * _rstd(hh) * g_ref[...]).astype(BF)

        u = u_ref[...]
        a = _dot_nt(u, wg_ref[...])
        b = _dot_nt(u, wu_ref[...])
        a_ref[...] = a.astype(BF)
        b_ref[...] = b.astype(BF)
        hid_ref[...] = (a * jax.nn.sigmoid(a) * b).astype(BF)

    a, b, hid, u = _call(
        act_body, grid=(lp // tm, ns), in_specs=[row, pl.BlockSpec((1, d), lambda i, j: (0, 0)), wsp, wsp],
        out_specs=[act, act, act, row], out_shape=[SDS((ns, lp, fs), BF)] * 3 + [SDS((lp, d), BF)],
        vmem_mib=48, name=name + "_act", comm=comm)(h, g, wg, wu)

    def down_body(h_ref, hid_ref, wd_ref, out_ref, acc_ref):
        j = pl.program_id(1)

        @pl.when(j == 0)
        def _():
            acc_ref[...] = jnp.zeros_like(acc_ref)

        acc_ref[...] += _dot(hid_ref[...], wd_ref[...])

        @pl.when(j == ns - 1)
        def _():
            out_ref[...] = h_ref[...] + 0.5 * acc_ref[...]

    out = _call(
        down_body, grid=(lp // tm, ns), in_specs=[row, act, wsp], out_specs=row, out_shape=SDS((lp, d), F32),
        scratch_shapes=[pltpu.VMEM((tm, d), F32)], vmem_mib=48, name=name + "_down")(h, hid, wd)
    return out, a, b, u


def _ffn_fwd_part(h, g, wg, wu, wd, order, carry, tm, name, deps=()):
    lp, d = h.shape
    fs = wg.shape[1]
    k = order.shape[0]
    first = carry is None
    n_in = 5 if first else 8

    def body(order_ref, *refs):
        outs = refs[n_in + len(deps):]
        if first:
            h_ref, g_ref, wg_ref, wu_ref, wd_ref = refs[:n_in]
            out_ref, a_ref, b_ref, u_ref, acc_ref = outs
        else:
            h_ref, acc_in_ref, u_ref, _, _, wg_ref, wu_ref, wd_ref = refs[:n_in]
            out_ref, a_ref, b_ref, acc_ref = outs
        j = pl.program_id(1)

        @pl.when(j == 0)
        def _():
            if first:
                hh = h_ref[...]
                u_ref[...] = (hh * _rstd(hh) * g_ref[...]).astype(BF)
                acc_ref[...] = jnp.zeros_like(acc_ref)
            else:
                acc_ref[...] = acc_in_ref[...]

        u = u_ref[...]
        a = _dot_nt(u, wg_ref[...])
        b = _dot_nt(u, wu_ref[...])
        a_ref[...] = a.astype(BF)
        b_ref[...] = b.astype(BF)
        hid = (a * jax.nn.sigmoid(a) * b).astype(BF)
        acc_ref[...] += _dot(hid, wd_ref[...])

        @pl.when(j == k - 1)
        def _():
            out_ref[...] = acc_ref[...] if first else h_ref[...] + 0.5 * acc_ref[...]

    row = pl.BlockSpec((tm, d), lambda i, j, o: (i, 0))
    act = pl.BlockSpec((None, tm, fs), lambda i, j, o: (o[j], i, 0))
    wsp = pl.BlockSpec((None, fs, d), lambda i, j, o: (o[j], 0, 0))
    anyspec = pl.BlockSpec(memory_space=pl.ANY)
    acts = [SDS((wg.shape[0], lp, fs), BF)] * 2
    if first:
        in_specs = [row, pl.BlockSpec((1, d), lambda i, j, o: (0, 0)), wsp, wsp, wsp]
        out_specs, out_shape = [row, act, act, row], [SDS((lp, d), F32)] + acts + [SDS((lp, d), BF)]
        args, aliases = (h, g, wg, wu, wd), {}
    else:
        acc, a_prev, b_prev, u_prev = carry
        in_specs = [row, row, row, anyspec, anyspec, wsp, wsp, wsp]
        out_specs, out_shape = [row, act, act], [SDS((lp, d), F32)] + acts
        args, aliases = (h, acc, u_prev, a_prev, b_prev, wg, wu, wd), {4: 1, 5: 2}
    return pl.pallas_call(
        body,
        grid_spec=pltpu.PrefetchScalarGridSpec(
            num_scalar_prefetch=1, grid=(lp // tm, k), in_specs=in_specs + [anyspec] * len(deps),
            out_specs=out_specs, scratch_shapes=[pltpu.VMEM((tm, d), F32)]),
        out_shape=out_shape, input_output_aliases=aliases,
        compiler_params=pltpu.CompilerParams(dimension_semantics=("arbitrary",) * 2, vmem_limit_bytes=60 * MIB),
        name=name)(order, *args, *deps)


def _ffn_bwd_act(dob, a, b, wd, tm, name, comm=None):
    lp, d = dob.shape
    ns, fs, _ = wd.shape
    row = pl.BlockSpec((tm, d), lambda i, j: (i, 0))
    act = pl.BlockSpec((None, tm, fs), lambda i, j: (j, i, 0))
    wsp = pl.BlockSpec((None, fs, d), lambda i, j: (j, 0, 0))

    def act_body(do_ref, a_ref, b_ref, wd_ref, da_ref, db_ref, hid_ref):
        dhid = _dot_nt(do_ref[...], wd_ref[...])
        av = a_ref[...].astype(F32)
        bv = b_ref[...].astype(F32)
        sig = jax.nn.sigmoid(av)
        sil = av * sig
        hid_ref[...] = (sil * bv).astype(BF)
        da_ref[...] = (dhid * bv * (sig * (1.0 + av * (1.0 - sig)))).astype(BF)
        db_ref[...] = (dhid * sil).astype(BF)

    return _call(
        act_body, grid=(lp // tm, ns), in_specs=[row, act, act, wsp], out_specs=[act, act, act],
        out_shape=[SDS((ns, lp, fs), BF)] * 3, vmem_mib=40, name=name, comm=comm)(dob, a, b, wd)


def _ffn_bwd_du(da, db, wg, wu, tm, name, comm=None):
    ns, lp, fs = da.shape
    d = wg.shape[2]
    row = pl.BlockSpec((tm, d), lambda i, j: (i, 0))
    act = pl.BlockSpec((None, tm, fs), lambda i, j: (j, i, 0))
    wsp = pl.BlockSpec((None, fs, d), lambda i, j: (j, 0, 0))

    def du_body(da_ref, db_ref, wg_ref, wu_ref, du_ref):
        @pl.when(pl.program_id(1) == 0)
        def _():
            du_ref[...] = jnp.zeros_like(du_ref)

        du_ref[...] += _dot(da_ref[...], wg_ref[...]) + _dot(db_ref[...], wu_ref[...])

    return _call(
        du_body, grid=(lp // tm, ns), in_specs=[act, act, wsp, wsp], out_specs=row,
        out_shape=SDS((lp, d), F32), vmem_mib=48, name=name, comm=comm)(da, db, wg, wu)


def _rms_bwd(du, h, g, dres, bscale, tm, name, comm=None):
    lp, d = h.shape

    def body(du_ref, h_ref, g_ref, dres_ref, dh_ref, dhb_ref, dg_ref):
        @pl.when(pl.program_id(0) == 0)
        def _():
            dg_ref[...] = jnp.zeros_like(dg_ref)

        hh = h_ref[...]
        r = _rstd(hh)
        xhat = hh * r
        duv = du_ref[...]
        dg_ref[...] += _rows8(duv * xhat)
        dxh = duv * g_ref[...]
        dh = dres_ref[...] + r * (dxh - xhat * jnp.mean(dxh * xhat, axis=-1, keepdims=True))
        dh_ref[...] = dh
        dhb_ref[...] = (bscale * dh).astype(BF)

    row = pl.BlockSpec((tm, d), lambda i: (i, 0))
    return _call(
        body, grid=(lp // tm,),
        in_specs=[row, row, pl.BlockSpec((1, d), lambda i: (0, 0)), row],
        out_specs=[row, row, pl.BlockSpec((SUBLANES, d), lambda i: (0, 0))],
        out_shape=[SDS((lp, d), F32), SDS((lp, d), BF), SDS((SUBLANES, d), F32)],
        vmem_mib=48, name=name, comm=comm)(du, h, g, dres)


def _matmul_tn(a, b, tm, tn, name, comm=None):
    a_b, b_b = a.ndim == 3, b.ndim == 3
    ns = a.shape[0] if a_b else (b.shape[0] if b_b else 1)
    l, m = a.shape[-2:]
    n = b.shape[-1]

    def body(a_ref, b_ref, o_ref):
        o_ref[...] = _dot_tn(a_ref[...], b_ref[...]).astype(o_ref.dtype)

    a_spec = (pl.BlockSpec((None, l, tm), lambda s, i, j: (s, 0, i)) if a_b
              else pl.BlockSpec((l, tm), lambda s, i, j: (0, i)))
    b_spec = (pl.BlockSpec((None, l, tn), lambda s, i, j: (s, 0, j)) if b_b
              else pl.BlockSpec((l, tn), lambda s, i, j: (0, j)))
    batched = a_b or b_b
    o_spec = (pl.BlockSpec((None, tm, tn), lambda s, i, j: (s, i, j)) if batched
              else pl.BlockSpec((tm, tn), lambda s, i, j: (i, j)))
    o_shape = SDS((ns, m, n), BF) if batched else SDS((m, n), BF)
    return _call(
        body, grid=(ns, m // tm, n // tn), in_specs=[a_spec, b_spec], out_specs=o_spec, out_shape=o_shape,
        vmem_mib=48, name=name, comm=comm)(a, b)


def _matmul_nt(x, w, tm, tk, out_dtype, name, comm=None):
    l, k = x.shape
    n = w.shape[0]
    nk = k // tk

    def body(x_ref, w_ref, o_ref, acc_ref):
        kk = pl.program_id(1)

        @pl.when(kk == 0)
        def _():
            acc_ref[...] = jnp.zeros_like(acc_ref)

        acc_ref[...] += _dot_nt(x_ref[...], w_ref[...])

        @pl.when(kk == nk - 1)
        def _():
            o_ref[...] = acc_ref[...].astype(o_ref.dtype)

    return _call(
        body, grid=(l // tm, nk),
        in_specs=[pl.BlockSpec((tm, tk), lambda i, kk: (i, kk)), pl.BlockSpec((n, tk), lambda i, kk: (0, kk))],
        out_specs=pl.BlockSpec((tm, n), lambda i, kk: (i, 0)),
        out_shape=SDS((l, n), out_dtype),
        scratch_shapes=[pltpu.VMEM((tm, n), F32)],
        vmem_mib=56, name=name, comm=comm)(x, w)


def _norm_matmul(h, g, w, tm, tn, name, comm=None):
    lp, d = h.shape
    n = w.shape[1]

    def body(h_ref, g_ref, w_ref, z_ref, u_ref):
        @pl.when(pl.program_id(1) == 0)
        def _():
            hh = h_ref[...]
            u_ref[...] = (hh * _rstd(hh) * g_ref[...]).astype(BF)

        z_ref[...] = _dot(u_ref[...], w_ref[...])

    row = pl.BlockSpec((tm, d), lambda i, j: (i, 0))
    return _call(
        body, grid=(lp // tm, n // tn),
        in_specs=[row, pl.BlockSpec((1, d), lambda i, j: (0, 0)), pl.BlockSpec((d, tn), lambda i, j: (0, j))],
        out_specs=[pl.BlockSpec((tm, tn), lambda i, j: (i, j)), row],
        out_shape=[SDS((lp, n), F32), SDS((lp, d), BF)],
        vmem_mib=60, name=name, comm=comm)(h, g, w)


def _out_proj(h, pool_o, att_o, w_out, tm, name, comm=None):
    lp, d = h.shape
    p = pool_o.shape[1]
    dm = w_out.shape[0]

    def body(h_ref, p_ref, a_ref, w_ref, o_ref):
        o_ref[...] = h_ref[...] + _dot(p_ref[...], w_ref[0:p, :]) + _dot(a_ref[...], w_ref[p:dm, :])

    row = pl.BlockSpec((tm, d), lambda i: (i, 0))
    return _call(
        body, grid=(lp // tm,),
        in_specs=[row, pl.BlockSpec((tm, p), lambda i: (i, 0)), pl.BlockSpec((tm, dm - p), lambda i: (i, 0)),
                  pl.BlockSpec((dm, d), lambda i: (0, 0))],
        out_specs=row, out_shape=SDS((lp, d), F32),
        vmem_mib=48, name=name, comm=comm)(h, pool_o, att_o, w_out)


def _loss_head(y, tpad, row0, row1, tm, name, comm=None):
    lp, d = y.shape

    def body(y_ref, t_ref, dy_ref, dob_ref, ls_ref):
        i = pl.program_id(0)

        @pl.when(i == 0)
        def _():
            ls_ref[...] = jnp.zeros_like(ls_ref)

        rows = i * tm + lax.broadcasted_iota(jnp.int32, (tm, d), 0)
        err = jnp.where((rows >= row0) & (rows < row1), y_ref[...] - t_ref[...], 0.0)
        dy = err * (1.0 / d)
        dy_ref[...] = dy
        dob_ref[...] = (0.5 * dy).astype(BF)
        sq = _rows8(err * err)
        acc = sq[:, 0:LANES]
        for c in range(1, d // LANES):
            acc = acc + sq[:, c * LANES:(c + 1) * LANES]
        ls_ref[...] += acc

    row = pl.BlockSpec((tm, d), lambda i: (i, 0))
    return _call(
        body, grid=(lp // tm,), in_specs=[row, row],
        out_specs=[row, row, pl.BlockSpec((SUBLANES, LANES), lambda i: (0, 0))],
        out_shape=[SDS((lp, d), F32), SDS((lp, d), BF), SDS((SUBLANES, LANES), F32)],
        vmem_mib=48, name=name, comm=comm)(y, tpad)


def _window_select(levels, gidx):
    out = levels[-1]
    for k in range(len(levels) - 2, -1, -1):
        out = jnp.where(gidx == k, levels[k], out)
    return out


def _pool_window_mean_minus_id(x, gidx):
    rows = lax.broadcasted_iota(jnp.int32, x.shape, 0)
    levels = []
    s = x
    shift = 1
    while shift < POOL_WINDOWS[-1]:
        s = s + jnp.where(rows >= shift, pltpu.roll(s, shift, 0), 0.0)
        shift *= 2
        if shift in POOL_WINDOWS:
            levels.append(s)
    win = _window_select(levels, gidx)
    cnt = jnp.minimum(rows + 1, _window_select(list(POOL_WINDOWS), gidx)).astype(F32)
    return win / cnt - x, cnt


def _pool_window_transpose(dy, cnt, gidx):
    lp = dy.shape[0]
    rows = lax.broadcasted_iota(jnp.int32, dy.shape, 0)
    levels = []
    s = dy / cnt
    shift = 1
    while shift < POOL_WINDOWS[-1]:
        s = s + jnp.where(rows < lp - shift, pltpu.roll(s, lp - shift, 0), 0.0)
        shift *= 2
        if shift in POOL_WINDOWS:
            levels.append(s)
    return _window_select(levels, gidx) - dy


def _pool_fwd(z, pool_w, pool_scale, name, comm=None):
    lp = z.shape[0]
    ng, gw, _ = pool_w.shape

    def body(p_ref, w_ref, s_ref, o_ref):
        pooled, _ = _pool_window_mean_minus_id(p_ref[...], pl.program_id(0))
        o_ref[...] = (_dot(pooled.astype(BF), w_ref[...]) * s_ref[...]).astype(BF)

    return _call(
        body, grid=(ng,),
        in_specs=[pl.BlockSpec((lp, gw), lambda g: (0, g)), pl.BlockSpec((None, gw, gw), lambda g: (g, 0, 0)),
                  pl.BlockSpec((1, gw), lambda g: (0, g))],
        out_specs=pl.BlockSpec((lp, gw), lambda g: (0, g)), out_shape=SDS((lp, ng * gw), BF),
        vmem_mib=48, name=name, comm=comm)(z, pool_w, pool_scale)


def _pool_bwd(z, dmix, pool_w, pool_scale, name, comm=None):
    lp = z.shape[0]
    ng, gw, _ = pool_w.shape

    def body(p_ref, d_ref, w_ref, s_ref, dz_ref, dw_ref, ds_ref):
        g = pl.program_id(0)
        pooled, cnt = _pool_window_mean_minus_id(p_ref[...], g)
        pooled_b = pooled.astype(BF)
        w = w_ref[...]
        mixed = _dot(pooled_b, w)
        dpo = d_ref[...].astype(F32)
        ds_ref[...] = _rows8(dpo * mixed)
        dmixed = (dpo * s_ref[...]).astype(BF)
        dw_ref[...] = _dot_tn(pooled_b, dmixed)
        dpooled = _dot_nt(dmixed, w)
        dz_ref[...] = _pool_window_transpose(dpooled, cnt, g).astype(BF)

    return _call(
        body, grid=(ng,),
        in_specs=[pl.BlockSpec((lp, gw), lambda g: (0, g)), pl.BlockSpec((lp, gw), lambda g: (0, g)),
                  pl.BlockSpec((None, gw, gw), lambda g: (g, 0, 0)), pl.BlockSpec((1, gw), lambda g: (0, g))],
        out_specs=[pl.BlockSpec((lp, gw), lambda g: (0, g)), pl.BlockSpec((None, gw, gw), lambda g: (g, 0, 0)),
                   pl.BlockSpec((SUBLANES, gw), lambda g: (0, g))],
        out_shape=[SDS((lp, ng * gw), BF), SDS((ng, gw, gw), F32), SDS((SUBLANES, ng * gw), F32)],
        vmem_mib=48, name=name, comm=comm)(z, dmix, pool_w, pool_scale)


def _log_sigmoid(x):
    return jnp.minimum(x, 0.0) - jnp.log(1.0 + jnp.exp(-jnp.abs(x)))


def _fox_prep(z, bfp, fblk, name, comm=None):
    lp = z.shape[0]
    nb = lp // LANES

    def body(f_ref, b_ref, cum_ref):
        r = lax.broadcasted_iota(jnp.int32, (LANES, LANES), 0)
        c = lax.broadcasted_iota(jnp.int32, (LANES, LANES), 1)
        tri = (r >= c).astype(F32)
        carry = jnp.zeros((1, LANES), F32)
        for blk in range(nb):
            sl = slice(blk * LANES, (blk + 1) * LANES)
            lf = _log_sigmoid(f_ref[sl, :] + b_ref[...])
            cb = jnp.dot(tri, lf, preferred_element_type=F32, precision=lax.Precision.HIGHEST) + carry
            cum_ref[sl, :] = cb
            carry = cb[LANES - 1:LANES, :]

    return _call(
        body, grid=(1,),
        in_specs=[pl.BlockSpec((lp, LANES), lambda i: (0, fblk)), pl.BlockSpec((1, LANES), lambda i: (0, 0))],
        out_specs=pl.BlockSpec((lp, LANES), lambda i: (0, 0)), out_shape=SDS((lp, LANES), F32),
        vmem_mib=32, name=name, comm=comm)(z, bfp)


def _fox_bwd(z, bfp, dcum, fblk, name, comm=None):
    lp = z.shape[0]
    nb = lp // LANES

    def body(f_ref, b_ref, dc_ref, dz_ref, db_ref):
        r = lax.broadcasted_iota(jnp.int32, (LANES, LANES), 0)
        c = lax.broadcasted_iota(jnp.int32, (LANES, LANES), 1)
        tri = (r <= c).astype(F32)
        carry = jnp.zeros((1, LANES), F32)
        acc = jnp.zeros((SUBLANES, LANES), F32)
        for blk in range(nb - 1, -1, -1):
            sl = slice(blk * LANES, (blk + 1) * LANES)
            dlf = jnp.dot(tri, dc_ref[sl, :], preferred_element_type=F32, precision=lax.Precision.HIGHEST) + carry
            carry = dlf[0:1, :]
            df = dlf * jax.nn.sigmoid(-(f_ref[sl, :] + b_ref[...]))
            dz_ref[sl, :] = df.astype(BF)
            acc = acc + _rows8(df)
        db_ref[...] = acc

    return _call(
        body, grid=(1,),
        in_specs=[pl.BlockSpec((lp, LANES), lambda i: (0, fblk)), pl.BlockSpec((1, LANES), lambda i: (0, 0)),
                  pl.BlockSpec((lp, LANES), lambda i: (0, 0))],
        out_specs=[pl.BlockSpec((lp, LANES), lambda i: (0, 0)), pl.BlockSpec((SUBLANES, LANES), lambda i: (0, 0))],
        out_shape=[SDS((lp, LANES), BF), SDS((SUBLANES, LANES), F32)],
        vmem_mib=32, name=name, comm=comm)(z, bfp, dcum)


def _att_scores(q_ref, cum_ref, cumt_ref, qw_ref, kn_s, h, i, tq, lk):
    scale = 1.0 / (HEAD_DIM ** 0.5)
    q = q_ref[...]
    rq = _rstd(q)
    qhat = q * rq
    qn = (qhat * qw_ref[...]).astype(BF)
    s = _dot_nt(qn, kn_s[0:lk, :]) * scale
    lane = lax.broadcasted_iota(jnp.int32, (tq, LANES), 1)
    cq = jnp.sum(jnp.where(lane == h, cum_ref[...], 0.0), axis=1, keepdims=True)
    ck = cumt_ref[pl.ds(h, 1), 0:lk]
    s = s + (cq - ck)
    qpos = i * tq + lax.broadcasted_iota(jnp.int32, (tq, lk), 0)
    kpos = lax.broadcasted_iota(jnp.int32, (tq, lk), 1)
    s = jnp.where(qpos >= kpos, s, NEG_BIG)
    e = jnp.exp(s - jnp.max(s, axis=1, keepdims=True))
    p = e * (1.0 / jnp.sum(e, axis=1, keepdims=True))
    return p, qn, qhat, rq


def _per_query_tile(i, nq, tq, lp, fn):
    for t in range(nq):
        lk = min(lp, -(-((t + 1) * tq) // LANES) * LANES)
        pl.when(i == t)(functools.partial(fn, lk))


def _att_fwd(z, cum, cumt, qw, kw, n_heads, qblk0, tq, name, comm=None):
    lp = z.shape[0]
    nh = n_heads

    def body(q_ref, k_ref, v_ref, cum_ref, cumt_ref, qw_ref, kw_ref, o_ref, kn_s, vb_s):
        h, i = pl.program_id(0), pl.program_id(1)

        @pl.when(i == 0)
        def _():
            k = k_ref[...]
            kn_s[...] = (k * _rstd(k) * kw_ref[...]).astype(BF)
            vb_s[...] = v_ref[...].astype(BF)

        def tile(lk):
            p, _, _, _ = _att_scores(q_ref, cum_ref, cumt_ref, qw_ref, kn_s, h, i, tq, lk)
            o_ref[...] = _dot(p.astype(BF), vb_s[0:lk, :]).astype(BF)

        _per_query_tile(i, lp // tq, tq, lp, tile)

    vec = pl.BlockSpec((1, HEAD_DIM), lambda h, i: (0, 0))
    return _call(
        body, grid=(nh, lp // tq),
        in_specs=[pl.BlockSpec((tq, HEAD_DIM), lambda h, i: (i, qblk0 + h)),
                  pl.BlockSpec((lp, HEAD_DIM), lambda h, i: (0, qblk0 + nh + h)),
                  pl.BlockSpec((lp, HEAD_DIM), lambda h, i: (0, qblk0 + 2 * nh + h)),
                  pl.BlockSpec((tq, LANES), lambda h, i: (i, 0)),
                  pl.BlockSpec((nh, lp), lambda h, i: (0, 0)), vec, vec],
        out_specs=pl.BlockSpec((tq, HEAD_DIM), lambda h, i: (i, h)),
        out_shape=SDS((lp, nh * HEAD_DIM), BF),
        scratch_shapes=[pltpu.VMEM((lp, HEAD_DIM), BF), pltpu.VMEM((lp, HEAD_DIM), BF)],
        vmem_mib=48, name=name, comm=comm)(z, z, z, cum, cumt, qw, kw)


def _att_bwd(z, cum, cumt, qw, kw, dmix, n_heads, qblk0, oblk0, tq, name, comm=None):
    lp = z.shape[0]
    nh = n_heads
    nq = lp // tq
    scale = 1.0 / (HEAD_DIM ** 0.5)

    def body(q_ref, k_ref, v_ref, cum_ref, cumt_ref, qw_ref, kw_ref, do_ref,
             dq_ref, dk_ref, dv_ref, dck_ref, dqw_ref, dkw_ref,
             kn_s, vb_s, dkn_s, dv_s, dck_s):
        h, i = pl.program_id(0), pl.program_id(1)

        @pl.when((h == 0) & (i == 0))
        def _():
            dqw_ref[...] = jnp.zeros_like(dqw_ref)
            dkw_ref[...] = jnp.zeros_like(dkw_ref)

        @pl.when(i == 0)
        def _():
            k = k_ref[...]
            kn_s[...] = (k * _rstd(k) * kw_ref[...]).astype(BF)
            vb_s[...] = v_ref[...].astype(BF)
            dkn_s[...] = jnp.zeros_like(dkn_s)
            dv_s[...] = jnp.zeros_like(dv_s)
            dck_s[...] = jnp.zeros_like(dck_s)

        def tile(lk):
            p, qn, qhat, rq = _att_scores(q_ref, cum_ref, cumt_ref, qw_ref, kn_s, h, i, tq, lk)
            dob = do_ref[...]
            dp = _dot_nt(dob, vb_s[0:lk, :])
            ds = p * (dp - jnp.sum(p * dp, axis=1, keepdims=True))
            dsb = ds.astype(BF)
            dv_s[0:lk, :] += _dot_tn(p.astype(BF), dob)
            dkn_s[0:lk, :] += _dot_tn(dsb, qn)
            dck_s[:, 0:lk] += jnp.sum(ds, axis=0, keepdims=True)
            dqn = _dot(dsb, kn_s[0:lk, :]) * scale
            gq = dqn * qw_ref[...]
            dq_ref[...] = (rq * (gq - qhat * jnp.mean(gq * qhat, axis=-1, keepdims=True))).astype(BF)
            dqw_ref[...] += _rows8(dqn * qhat)

        _per_query_tile(i, nq, tq, lp, tile)

        @pl.when(i == nq - 1)
        def _():
            k = k_ref[...]
            rk = _rstd(k)
            khat = k * rk
            dkn = dkn_s[...] * scale
            gk = dkn * kw_ref[...]
            dk_ref[...] = (rk * (gk - khat * jnp.mean(gk * khat, axis=-1, keepdims=True))).astype(BF)
            dkw_ref[...] += _rows8(dkn * khat)
            dv_ref[...] = dv_s[...].astype(BF)
            dck_ref[...] = dck_s[...]

    vec = pl.BlockSpec((1, HEAD_DIM), lambda h, i: (0, 0))
    part = pl.BlockSpec((SUBLANES, LANES), lambda h, i: (0, 0))
    return _call(
        body, grid=(nh, nq),
        in_specs=[pl.BlockSpec((tq, HEAD_DIM), lambda h, i: (i, qblk0 + h)),
                  pl.BlockSpec((lp, HEAD_DIM), lambda h, i: (0, qblk0 + nh + h)),
                  pl.BlockSpec((lp, HEAD_DIM), lambda h, i: (0, qblk0 + 2 * nh + h)),
                  pl.BlockSpec((tq, LANES), lambda h, i: (i, 0)),
                  pl.BlockSpec((nh, lp), lambda h, i: (0, 0)), vec, vec,
                  pl.BlockSpec((tq, HEAD_DIM), lambda h, i: (i, oblk0 + h))],
        out_specs=[pl.BlockSpec((tq, HEAD_DIM), lambda h, i: (i, h)),
                   pl.BlockSpec((lp, HEAD_DIM), lambda h, i: (0, h)),
                   pl.BlockSpec((lp, HEAD_DIM), lambda h, i: (0, h)),
                   pl.BlockSpec((None, 1, lp), lambda h, i: (h, 0, 0)),
                   part, part],
        out_shape=[SDS((lp, nh * HEAD_DIM), BF)] * 3
        + [SDS((nh, 1, lp), F32), SDS((SUBLANES, LANES), F32), SDS((SUBLANES, LANES), F32)],
        scratch_shapes=[pltpu.VMEM((lp, HEAD_DIM), BF), pltpu.VMEM((lp, HEAD_DIM), BF),
                        pltpu.VMEM((lp, HEAD_DIM), F32), pltpu.VMEM((lp, HEAD_DIM), F32),
                        pltpu.VMEM((1, lp), F32)],
        vmem_mib=56, name=name, comm=comm)(z, z, z, cum, cumt, qw, kw, dmix)


def _adamw_math(w, g, m, v):
    m2 = ADAM_B1 * m + (1.0 - ADAM_B1) * g
    v2 = ADAM_B2 * v + (1.0 - ADAM_B2) * (g * g)
    m_hat = m2 / (1.0 - ADAM_B1 ** ADAM_STEP)
    v_hat = v2 / (1.0 - ADAM_B2 ** ADAM_STEP)
    delta = -ADAM_LR * (m_hat / (jnp.sqrt(v_hat) + ADAM_EPS) + ADAM_WD * w)
    return delta, m2, v2


def _adamw(g_in, w, m, v, name, comm=None):
    r, c = w.shape
    partial_sum = g_in.ndim == 3
    lane_padded = -(-c // LANES) * LANES
    tr = _largest_tile(r, max(16, MIB // (4 * lane_padded) // 16 * 16), 16)

    def body(g_ref, w_ref, m_ref, v_ref, go_ref, d_ref, mo_ref, vo_ref):
        if partial_sum:
            g = g_ref[0].astype(F32)
            for k in range(1, g_in.shape[0]):
                g = g + g_ref[k].astype(F32)
        else:
            g = g_ref[...]
        delta, m2, v2 = _adamw_math(w_ref[...], g, m_ref[...], v_ref[...])
        go_ref[...] = g
        d_ref[...] = delta
        mo_ref[...] = m2
        vo_ref[...] = v2

    blk = pl.BlockSpec((tr, c), lambda i: (i, 0))
    g_spec = pl.BlockSpec((g_in.shape[0], tr, c), lambda i: (0, i, 0)) if partial_sum else blk
    return _call(
        body, grid=(r // tr,), in_specs=[g_spec, blk, blk, blk], out_specs=[blk] * 4,
        out_shape=[SDS((r, c), F32)] * 4, vmem_mib=40, name=name, comm=comm)(g_in, w, m, v)


def _peer(x, y, c, k):
    return (1 - x if k & 4 else x, 1 - y if k & 2 else y, 1 - c if k & 1 else c)


_SIBLING = 1
_ICI_RELS = (2, 4, 6)


def _mesh_pos():
    return lax.axis_index("x"), lax.axis_index("y"), lax.axis_index("c")


def _sem_pair(sems, t, j, n_rel, scalars):
    if scalars:
        i = 2 * (t * n_rel + j)
        return sems[i], sems[i + 1]
    return sems[0].at[t, j], sems[1].at[t, j]


def _dev(pos):
    return 4 * pos[0] + 2 * pos[1] + pos[2]


def _gather_ici(shards, landing=None, rels=(_SIBLING,) + _ICI_RELS):
    n = len(shards)

    def remote(ins, outs, sems, arrival):
        x, y, c = _mesh_pos()
        dst = ins[n:] if landing is not None else outs
        cps = []
        for j, k in enumerate(rels):
            peer = _peer(x, y, c, k)
            slot = _dev(peer) if arrival else _dev((x, y, c))
            for t in range(n):
                send_sem, recv_sem = _sem_pair(sems, t, j, len(rels), landing is not None)
                cps.append(pltpu.make_async_remote_copy(
                    src_ref=ins[t], dst_ref=dst[t].at[slot], send_sem=send_sem, recv_sem=recv_sem,
                    device_id=peer, device_id_type=pl.DeviceIdType.MESH))
        return cps

    if landing is not None:
        def start_remote(ins, outs, sems):
            for cp in remote(ins, outs, sems, False):
                cp.start()

        def finish_remote(ins, outs, sems):
            for cp in remote(ins, outs, sems, True):
                cp.wait_recv()
            for cp in remote(ins, outs, sems, False):
                cp.wait_send()

        return _Comm(list(shards) + list(landing), [SDS(a.shape, a.dtype) for a in landing],
                     [pltpu.SemaphoreType.DMA(())] * (2 * n * len(rels)),
                     start_remote, finish_remote, aliases={n + t: t for t in range(n)})

    def local(ins, outs, sems):
        me = _dev(_mesh_pos())
        return [pltpu.make_async_copy(ins[t], outs[t].at[me], sems[2].at[t]) for t in range(n)]

    def start(ins, outs, sems):
        for cp in local(ins, outs, sems) + remote(ins, outs, sems, False):
            cp.start()

    def finish(ins, outs, sems):
        for cp in local(ins, outs, sems):
            cp.wait()
        for cp in remote(ins, outs, sems, True):
            cp.wait_recv()
        for cp in remote(ins, outs, sems, False):
            cp.wait_send()

    return _Comm(shards, [SDS((N_DEV,) + s.shape, s.dtype) for s in shards],
                 [pltpu.SemaphoreType.DMA((n, len(rels))), pltpu.SemaphoreType.DMA((n, len(rels))),
                  pltpu.SemaphoreType.DMA((n,))], start, finish)


def _gather_diagonal(zones):
    n = len(zones)

    def copies(ins, outs, sems, arrival):
        x, y, c = _mesh_pos()
        y_nb, x_nb, diag = _peer(x, y, c, 2), _peer(x, y, c, 4), _peer(x, y, c, 6)
        cps = []
        for j, (to, origin) in enumerate(((y_nb, x_nb), (x_nb, y_nb))):
            slot = _dev(diag) if arrival else _dev(origin)
            for t in range(n):
                half = ins[t].shape[1] // 2
                rows = ins[t].at[slot, pl.ds(j * half, half)]
                send_sem, recv_sem = _sem_pair(sems, t, j, 2, True)
                cps.append(pltpu.make_async_remote_copy(
                    src_ref=rows, dst_ref=rows, send_sem=send_sem, recv_sem=recv_sem,
                    device_id=to, device_id_type=pl.DeviceIdType.MESH))
        return cps

    def start(ins, outs, sems):
        for cp in copies(ins, outs, sems, False):
            cp.start()

    def finish(ins, outs, sems):
        for cp in copies(ins, outs, sems, True):
            cp.wait_recv()
        for cp in copies(ins, outs, sems, False):
            cp.wait_send()

    return _Comm(list(zones), [SDS(a.shape, a.dtype) for a in zones], [pltpu.SemaphoreType.DMA(())] * (4 * n),
                 start, finish, aliases={t: t for t in range(n)})


def _gather_fwd(partial):
    n = len(partial)

    def copies(ins, outs, sems, arrival):
        x, y, c = _mesh_pos()
        sibling = _peer(x, y, c, _SIBLING)
        cps = []
        for j, k in enumerate(_ICI_RELS):
            slot = _dev(_peer(x, y, c, k | _SIBLING if arrival else k))
            for t in range(n):
                cps.append(pltpu.make_async_remote_copy(
                    src_ref=ins[t].at[slot], dst_ref=outs[t].at[slot], send_sem=sems[0].at[t, j],
                    recv_sem=sems[1].at[t, j], device_id=sibling, device_id_type=pl.DeviceIdType.MESH))
        return cps

    def start(ins, outs, sems):
        for cp in copies(ins, outs, sems, False):
            cp.start()

    def finish(ins, outs, sems):
        for cp in copies(ins, outs, sems, True):
            cp.wait_recv()
        for cp in copies(ins, outs, sems, False):
            cp.wait_send()

    return _Comm(partial, [SDS(a.shape, a.dtype) for a in partial],
                 [pltpu.SemaphoreType.DMA((n, len(_ICI_RELS)))] * 2, start, finish,
                 aliases={t: t for t in range(n)})


def _scatter_sibling(slots):
    n = len(slots)

    def copies(ins, outs, sems):
        x, y, c = _mesh_pos()
        return [pltpu.make_async_remote_copy(
            src_ref=ins[t].at[:, 1 - c], dst_ref=outs[t], send_sem=sems[0].at[t], recv_sem=sems[1].at[t],
            device_id=_peer(x, y, c, _SIBLING), device_id_type=pl.DeviceIdType.MESH) for t in range(n)]

    def start(ins, outs, sems):
        for cp in copies(ins, outs, sems):
            cp.start()

    def finish(ins, outs, sems):
        for cp in copies(ins, outs, sems):
            cp.wait()

    return _Comm(slots, [SDS((s.shape[0],) + s.shape[2:], s.dtype) for s in slots],
                 [pltpu.SemaphoreType.DMA((n,))] * 2, start, finish)


def _scatter_ici(chip_sums, landing=None):
    n = len(chip_sums)

    def remote(ins, outs, sems, arrival):
        x, y, c = _mesh_pos()
        dst = ins[n:] if landing is not None else outs
        cps = []
        for j, k in enumerate(_ICI_RELS):
            peer = _peer(x, y, c, k)
            theirs, mine = 2 * peer[0] + peer[1], 2 * x + y
            for t in range(n):
                send_sem, recv_sem = _sem_pair(sems, t, j, len(_ICI_RELS), landing is not None)
                cps.append(pltpu.make_async_remote_copy(
                    src_ref=ins[t].at[theirs], dst_ref=dst[t].at[theirs if arrival else mine],
                    send_sem=send_sem, recv_sem=recv_sem,
                    device_id=peer, device_id_type=pl.DeviceIdType.MESH))
        return cps

    if landing is not None:
        def start_remote(ins, outs, sems):
            for cp in remote(ins, outs, sems, False):
                cp.start()

        def finish_remote(ins, outs, sems):
            for cp in remote(ins, outs, sems, True):
                cp.wait_recv()
            for cp in remote(ins, outs, sems, False):
                cp.wait_send()

        return _Comm(list(chip_sums) + list(landing), [SDS(a.shape, a.dtype) for a in landing],
                     [pltpu.SemaphoreType.DMA(())] * (2 * n * len(_ICI_RELS)), start_remote, finish_remote,
                     aliases={n + t: t for t in range(n)})

    def local(ins, outs, sems):
        x, y, _ = _mesh_pos()
        return [pltpu.make_async_copy(ins[t].at[2 * x + y], outs[t].at[2 * x + y], sems[2].at[t]) for t in range(n)]

    def start(ins, outs, sems):
        for cp in local(ins, outs, sems) + remote(ins, outs, sems, False):
            cp.start()

    def finish(ins, outs, sems):
        for cp in local(ins, outs, sems):
            cp.wait()
        for cp in remote(ins, outs, sems, True):
            cp.wait_recv()
        for cp in remote(ins, outs, sems, False):
            cp.wait_send()

    return _Comm(chip_sums, [SDS(a.shape, a.dtype) for a in chip_sums],
                 [pltpu.SemaphoreType.DMA((n, len(_ICI_RELS))), pltpu.SemaphoreType.DMA((n, len(_ICI_RELS))),
                  pltpu.SemaphoreType.DMA((n,))], start, finish)


def _chip_sum(slots, from_sibling, core, name):
    nq, _, r, c = slots.shape
    tr = _largest_tile(r, 1024, 16)

    def body(core_ref, a_ref, b_ref, o_ref):
        o_ref[...] = (a_ref[...].astype(F32) + b_ref[...].astype(F32)).astype(BF)

    return pl.pallas_call(
        body,
        grid_spec=pltpu.PrefetchScalarGridSpec(
            num_scalar_prefetch=1, grid=(nq, r // tr),
            in_specs=[pl.BlockSpec((None, None, tr, c), lambda q, i, core_ref: (q, core_ref[0], i, 0)),
                      pl.BlockSpec((None, tr, c), lambda q, i, core_ref: (q, i, 0))],
            out_specs=pl.BlockSpec((None, tr, c), lambda q, i, core_ref: (q, i, 0))),
        out_shape=SDS((nq, r, c), BF), compiler_params=pltpu.CompilerParams(vmem_limit_bytes=40 * MIB),
        name=name)(core, slots, from_sibling)


def _small_reduce(pack_g, meta_g, loss_scale, name):
    w = pack_g.shape[2]

    def body(p_ref, m_ref, tot_ref, meta_ref, loss_ref):
        acc = p_ref[0]
        macc = m_ref[0]
        for k in range(1, N_DEV):
            acc = acc + p_ref[k]
            macc = macc + m_ref[k]
        tot = jnp.sum(acc, axis=0, keepdims=True)
        tot_ref[...] = tot
        meta_ref[...] = macc
        loss_ref[...] = jnp.full((1, LANES), loss_scale * jnp.sum(tot[:, w - LANES:w]), F32)

    return pl.pallas_call(
        body, out_shape=[SDS((1, w), F32), SDS(meta_g.shape[1:], F32), SDS((1, LANES), F32)],
        compiler_params=pltpu.CompilerParams(vmem_limit_bytes=32 * MIB), name=name)(pack_g, meta_g)


def _local_step(x, target, sw, plan):
    s_len, d = x.shape
    n_heads, n_meta = plan.n_heads, plan.n_meta
    l = n_meta + s_len
    lp = -(-l // LANES) * LANES
    tm = _largest_tile(lp, 544, 16)
    tq = _largest_tile(lp, 272, 16)
    te = _largest_tile(lp, 272, 16)
    tmd = _largest_tile(d, 512, LANES)

    plan.at("start")
    x, target = plan.gate((x, target))
    zmeta, zpad = jnp.zeros((n_meta, d), F32), jnp.zeros((lp - l, d), F32)
    h0 = jnp.concatenate([zmeta, x, zpad], axis=0)
    tpad = jnp.concatenate([zmeta, target, zpad], axis=0)
    plan.at("landed", (h0, tpad))
    h0 = lax.dynamic_update_slice(h0, plan.weights("meta"), (0, 0))

    first_shards, last_shards = plan.ffn1_split()
    carry = _ffn_fwd_part(h0, sw["ffn1_norm"], *plan.weights("ffn1_landing"), first_shards, None, tm, "ffn1_fwd_a",
                          plan.order_tokens())
    plan.at("ffn1_mid", (carry[0],))
    wg1, wu1, wd1 = plan.weights("ffn1")
    h1, a1, b1 = _ffn_fwd_part(h0, sw["ffn1_norm"], wg1, wu1, wd1, last_shards, carry, tm, "ffn1_fwd_b",
                               plan.order_tokens())
    u1 = carry[3]
    fs = wg1.shape[1]
    plan.at("after_ffn1_fwd", (h1,))
    win, pw, wout = plan.weights("mix")
    nz = win.shape[1]
    p_w = sw["pool_scale"].shape[1]
    npb = p_w // LANES
    fblk = nz // LANES - 1
    tnz = _largest_tile(nz, 1408, LANES)
    qw, kw, bfp, ps = sw["q_norm"], sw["k_norm"], sw["b_forget"], sw["pool_scale"]
    z, u2 = _norm_matmul(h1, sw["mix_norm"], win, te, nz, "mix_in", plan.comm("mix_in"))
    plan.at("after_mix_in", (u2,))
    cum = _fox_prep(z, bfp, fblk, "fox_prep")
    cumt = cum[:, :n_heads].T
    pool_o = _pool_fwd(z, pw, ps, "pool_fwd")
    att_o = _att_fwd(z, cum, cumt, qw, kw, n_heads, npb, tq, "att_fwd", plan.comm("att_fwd"))
    plan.at("after_att_fwd", (att_o,))
    h2 =_out_proj(h1, pool_o, att_o, wout, tm, "out_proj", plan.comm("out_proj"))
    wg2, wu2, wd2 = plan.weights("ffn2")
    h3, a2, b2, u3 = _ffn_fwd_two_calls(h2, sw["ffn2_norm"], wg2, wu2, wd2, tm, "ffn2_fwd", plan.comm("ffn2_fwd"))
    dy, dob3, lsq = _loss_head(h3, tpad, n_meta, l, te, "loss_head")

    da2, db2, hid2 = _ffn_bwd_act(dob3, a2, b2, wd2, tm, "ffn2_bwd_act", plan.comm("ffn2_bwd_act"))
    du3 = _ffn_bwd_du(da2, db2, wg2, wu2, tm, "ffn2_bwd_du", plan.comm("ffn2_bwd_du"))
    dh2, dh2b, dn2 = _rms_bwd(du3, h2, sw["ffn2_norm"], dy, 1.0, te, "ffn2_rms_bwd")
    plan.grad("ffn2_w_gate", _matmul_tn(da2, u3, fs, d, "ffn2_dwg", plan.comm("ffn2_dwg")))
    plan.grad("ffn2_w_up", _matmul_tn(db2, u3, fs, d, "ffn2_dwu", plan.comm("ffn2_dwu")))
    plan.grad("ffn2_w_down", _matmul_tn(hid2, dob3, fs, d, "ffn2_dwd", plan.comm("ffn2_dwd")))
    plan.at("after_ffn2_dwd")

    dmix = _matmul_nt(dh2b, wout, tm, d, BF, "out_proj_bwd", plan.comm("out_proj_bwd"))
    plan.at("after_out_proj_bwd")
    tmp = _largest_tile(p_w, 512, LANES)
    plan.grad("w_out", jnp.concatenate([_matmul_tn(pool_o, dh2b, tmp, d, "dwout_pool"),
                                        _matmul_tn(att_o, dh2b, tmp, d, "dwout_att")], axis=0))
    dzp, dpw, dps = _pool_bwd(z, dmix, pw, ps, "pool_bwd")
    plan.grad("pool_w", dpw)
    plan.at("before_att_bwd")
    dq, dk, dv, dck, dqw, dkw = _att_bwd(z, cum, cumt, qw, kw, dmix, n_heads, npb, npb, tq, "att_bwd",
                                              plan.comm("att_bwd"))
    dcum = -dck[:, 0, :].T
    dcum = jnp.pad(dcum, ((0, 0), (0, LANES - n_heads)))
    dzf, dbf = _fox_bwd(z, bfp, dcum, fblk, "fox_bwd")
    dz = jnp.concatenate([dzp, dq, dk, dv, dzf], axis=1)
    plan.grad("w_in", _matmul_tn(u2, dz, tmd, tnz, "dwin", plan.comm("dwin")))
    du2 = _matmul_nt(dz, win, te, nz, F32, "mix_in_bwd", plan.comm("mix_in_bwd"))
    plan.at("before_ffn1_bwd_dx")
    dh1, dob1, dnm = _rms_bwd(du2, h1, sw["mix_norm"], dh2, 0.5, te, "mix_rms_bwd")

    da1, db1, hid1 = _ffn_bwd_act(dob1, a1, b1, wd1, tm, "ffn1_bwd_act", plan.comm("ffn1_bwd_act"))
    plan.grad("ffn1_w_gate", _matmul_tn(da1, u1, fs, d, "ffn1_dwg", plan.comm("ffn1_dwg")))
    plan.grad("ffn1_w_up", _matmul_tn(db1, u1, fs, d, "ffn1_dwu", plan.comm("ffn1_dwu")))
    plan.at("before_ffn1_dwd")
    plan.grad("ffn1_w_down", _matmul_tn(hid1, dob1, fs, d, "ffn1_dwd", plan.comm("ffn1_dwd")))
    plan.at("after_ffn1_dwd")
    du1 = _ffn_bwd_du(da1, db1, wg1, wu1, tm, "ffn1_bwd_du", plan.comm("ffn1_bwd_du"))
    dh0, _, dn1 = _rms_bwd(du1, h0, sw["ffn1_norm"], dh1, 1.0, te, "ffn1_rms_bwd", plan.comm("ffn1_rms_bwd"))

    small = [dn1, dnm, dn2, dps, dqw, dkw, dbf, lsq]
    return dh0[n_meta:l], dh0[:n_meta], small


_BIG = ("ffn1_w_gate", "ffn1_w_up", "ffn1_w_down", "w_in", "pool_w", "w_out", "ffn2_w_gate", "ffn2_w_up", "ffn2_w_down")
_SMALL = ("ffn1_norm", "mix_norm", "ffn2_norm", "pool_scale", "q_norm", "k_norm", "b_forget")
_ORDER = ("meta_tokens", "ffn1_norm", "ffn1_w_gate", "ffn1_w_up", "ffn1_w_down", "mix_norm", "w_in", "b_forget",
          "q_norm", "k_norm", "pool_w", "pool_scale", "w_out", "ffn2_norm", "ffn2_w_gate", "ffn2_w_up", "ffn2_w_down")


_FFN1 = ("ffn1_w_gate", "ffn1_w_up", "ffn1_w_down")
_FFN2 = ("ffn2_w_gate", "ffn2_w_up", "ffn2_w_down")
_MIX = ("w_in", "pool_w", "w_out")

_RIDES = {
    "out_proj": (("g2", _FFN2),),
    "ffn2_dwu": (("s1", ("ffn2_w_gate",)),),
    "ffn2_dwd": (("s1", ("ffn2_w_up",)),),
    "out_proj_bwd": (("s1", ("ffn2_w_down",)),),
    "mix_in_bwd": (("s1", _MIX),),
    "ffn1_dwu": (("s1", ("ffn1_w_gate",)),),
    "ffn1_dwd": (("s1", ("ffn1_w_up",)),),
    "ffn1_bwd_du": (("s1", ("ffn1_w_down",)),),
}
_META = ("meta_tokens",)
_POINTS = {
    "start": (("start", "gm", _META), ("start", "g1a", _FFN1), ("gate", _MIX + _FFN2), ("prepare", "g1", _MIX),
              ("prepare", "g1a", _FFN2)),
    "landed": (("wait", "gm", _META), ("wait", "g1a", _FFN1), ("start", "g1b", _FFN1), ("start", "g1", _MIX),
               ("start", "g1a", _FFN2)),
    "ffn1_mid": (("wait", "g1b", _FFN1), ("alone", "g2", _FFN1)),
    "after_ffn1_fwd": (("wait", "g1", _MIX), ("alone", "g2", _MIX)),
    "after_mix_in": (("wait", "g1a", _FFN2), ("start", "g1b", _FFN2)),
    "after_att_fwd": (("wait", "g1b", _FFN2),),
    "after_ffn2_dwd": (("sum", ("ffn2_w_gate",)), ("start", "s2", ("ffn2_w_gate",))),
    "after_out_proj_bwd": (("sum", ("ffn2_w_up",)), ("start", "s2", ("ffn2_w_up",))),
    "before_att_bwd": (("sum", ("ffn2_w_down",)), ("start", "s2", ("ffn2_w_down",))),
    "before_ffn1_bwd_dx": (("sum", _MIX), ("start", "s2", _MIX)),
    "before_ffn1_dwd": (("sum", ("ffn1_w_gate",)), ("start", "s2", ("ffn1_w_gate",))),
    "after_ffn1_dwd": (("sum", ("ffn1_w_up",)), ("start", "s2", ("ffn1_w_up",))),
    "after_ffn1_rms_bwd": (("sum", ("ffn1_w_down",)), ("start", "s2", ("ffn1_w_down",))),
    "before_adamw_ffn2_w_gate": (("wait", "s2", ("ffn2_w_gate",)),),
    "before_adamw_ffn2_w_up": (("wait", "s2", ("ffn2_w_up",)),),
    "before_adamw_ffn2_w_down": (("wait", "s2", ("ffn2_w_down",)),),
    "before_adamw_w_in": (("wait", "s2", _MIX),),
    "before_adamw_ffn1_w_gate": (("wait", "s2", ("ffn1_w_gate",)),),
    "before_adamw_ffn1_w_up": (("wait", "s2", ("ffn1_w_up",)),),
    "before_adamw_ffn1_w_down": (("wait", "s2", ("ffn1_w_down",)),),
}


def _own_slot_filled(block, slot, n_slots):
    zone = lax.empty((n_slots,) + block.shape, block.dtype)
    return lax.dynamic_update_slice(zone, block[None], (slot,) + (0,) * block.ndim)


class _MeshPlan:
    def __init__(self, raw, pos, d, d_in, n_heads):
        self.raw, self.pos = dict(raw), pos
        self.core = pos[2].astype(jnp.int32).reshape(1)
        self.d, self.d_in, self.n_heads, self.n_meta = d, d_in, n_heads, raw["meta_tokens"].shape[0]
        self.partial, self.full, self.slots, self.from_sibling, self.chip_sum, self.received = {}, {}, {}, {}, {}, {}
        self.partial_a, self.pending, self.prepared, self.started, self.tokens = {}, [], {}, {}, []

    def gate(self, arrays):
        gated = lax.optimization_barrier((self.tokens[-1], tuple(arrays)))
        self.tokens[-1] = gated[0]
        return gated[1]

    def _phase(self, kind, names):
        src, dst, make = {"g2": (self.partial, self.full, _gather_fwd),
                          "s1": (self.slots, self.from_sibling, _scatter_sibling),
                          "s2": (self.chip_sum, self.received, _scatter_ici)}[kind]
        op = make([src[n] for n in names])
        self.pending.append((op, dst, names))
        return op

    def _settle(self):
        for op, dst, names in self.pending:
            dst.update(zip(names, op.results))
        self.pending = []

    def _prepare(self, kind, names):
        x, y, c = self.pos
        if kind in ("g1", "g1a", "gm"):
            blocks = [self.raw[n] if kind == "gm" else _as2d(n, self.raw[n]).astype(BF) for n in names]
            rels = {"g1": (_SIBLING,) + _ICI_RELS, "g1a": (_SIBLING,) + _ICI_RELS[:2], "gm": tuple(range(1, N_DEV))}[kind]
            op = _gather_ici(blocks, [_own_slot_filled(b, 4 * x + 2 * y + c, N_DEV) for b in blocks], rels)
        elif kind == "g1b":
            op = _gather_diagonal([self.partial_a[n] for n in names])
        else:
            sums = [self.chip_sum[n] for n in names]
            mine = [lax.dynamic_index_in_dim(s, 2 * x + y, 0, keepdims=False) for s in sums]
            op = _scatter_ici(sums, [_own_slot_filled(b, 2 * x + y, N_DEV // 2) for b in mine])
        self.prepared[(kind, names)] = op

    def _start(self, kind, names):
        if (kind, names) not in self.prepared:
            self._prepare(kind, names)
        self._launch((kind, names), self.prepared.pop((kind, names)), "_".join(("start", kind, names[0])))

    def _launch(self, key, op, name):
        if self.tokens:
            op.arrs = list(self.gate(op.arrs))
        self.started[key], token = _split_start(op, name)
        self.tokens.append(token)

    def start_small_gather(self, arrays):
        x, y, c = self.pos
        zones = [_own_slot_filled(a, 4 * x + 2 * y + c, N_DEV) for a in arrays]
        self._launch("small", _gather_ici(list(arrays), zones, rels=tuple(range(1, N_DEV))), "start_gather_small")

    def wait_small_gather(self, afters):
        return _split_wait(self.started.pop("small"), afters, "wait_gather_small")

    def _wait(self, kind, names, afters):
        afters = list(afters) + [a for op in self.prepared.values() for a in op.arrs[len(op.arrs) // 2:]]
        landed = _split_wait(self.started.pop((kind, names)), afters, "_".join(("wait", kind, names[0])))
        {"g1": self.partial, "g1a": self.partial_a, "g1b": self.partial, "gm": self.partial,
         "s2": self.received}[kind].update(zip(names, landed))

    def ffn1_split(self):
        x, y, c = self.pos
        first = [(x, y, c), _peer(x, y, c, 1), _peer(x, y, c, 4), _peer(x, y, c, 2)]
        last = [_peer(x, y, c, 6), _peer(x, y, c, 5), _peer(x, y, c, 3), _peer(x, y, c, 7)]
        return tuple(jnp.stack([_dev(p) for p in part]).astype(jnp.int32) for part in (first, last))

    def order_tokens(self):
        tokens, self.tokens = self.tokens, []
        return tokens

    def comm(self, kernel_name):
        self._settle()
        ops = [self._phase(kind, names) for kind, names in _RIDES.get(kernel_name, ())]
        if self.tokens:
            ops.append(_Comm(self.tokens, [], [], lambda *a: None, lambda *a: None))
            self.tokens = []
        return _merge_comm(ops)

    def at(self, point, after=()):
        for step in _POINTS.get(point, ()):
            self._settle()
            if step[0] == "alone":
                _comm_alone(self._phase(step[1], step[2]), "_".join((step[1], point)))
            elif step[0] == "start":
                self._start(step[1], step[2])
            elif step[0] == "prepare":
                self._prepare(step[1], step[2])
            elif step[0] == "gate":
                self.raw.update(zip(step[1], self.gate([self.raw[n] for n in step[1]])))
            elif step[0] == "wait":
                self._wait(step[1], step[2], tuple(after) + tuple(self.tokens[-1:]))
            else:
                for n in step[1]:
                    pair = [pltpu.with_memory_space_constraint(a, pltpu.HBM)
                            for a in (self.slots[n], self.from_sibling[n])]
                    self.chip_sum[n] = _chip_sum(*pair, self.core, "chip_sum_" + n)

    def weights(self, group):
        self._settle()
        f, d = self.full, self.d
        if group == "meta":
            g = self.partial["meta_tokens"]
            return g.transpose(1, 0, 2).reshape(g.shape[1], d)
        if group == "ffn1_landing":
            return tuple(self.started[("g1b", _FFN1)][2])
        if group == "ffn1":
            return tuple(f[n] for n in _FFN1)
        if group == "ffn2":
            return tuple(f[n] for n in _FFN2)
        n_main = self.d_in - self.n_heads
        win = f["w_in"].transpose(1, 0, 2).reshape(d, self.d_in)
        win = jnp.concatenate([win[:, :n_main], jnp.pad(win[:, n_main:], ((0, 0), (0, LANES - self.n_heads)))], axis=1)
        pw = f["pool_w"]
        gw = pw.shape[2]
        pw = pw.reshape(N_DEV, -1, gw // N_DEV, gw).transpose(1, 0, 2, 3).reshape(-1, gw, gw)
        return win, pw, f["w_out"].reshape(-1, d)

    def grad(self, name, g):
        d = self.d
        if name == "w_in":
            g = g[:, :self.d_in].reshape(d, N_DEV, -1).transpose(1, 0, 2)
        elif name == "pool_w":
            ng, gw = g.shape[0], g.shape[2]
            g = g.astype(BF).reshape(ng, N_DEV, -1, gw).transpose(1, 0, 2, 3).reshape(N_DEV, -1, gw)
        elif name == "w_out":
            g = g.reshape(N_DEV, -1, d)
        self.slots[name] = g.reshape((N_DEV // 2, 2) + g.shape[1:])

    def gradient_parts(self, name):
        self._settle()
        return self.received[name]


_TRANSPOSED = ("ffn1_w_gate", "ffn1_w_up", "ffn2_w_gate", "ffn2_w_up")


def _as2d(name, a):
    return a[0].T if name in _TRANSPOSED else a.reshape(-1, a.shape[-1])


def _from2d(name, a2d, shape):
    return a2d.T.reshape(shape) if name in _TRANSPOSED else a2d.reshape(shape)


def kernel(x, meta_tokens, ffn1_norm, ffn1_w_gate, ffn1_w_up, ffn1_w_down, mix_norm, w_in, b_forget, q_norm, k_norm, pool_w, pool_scale, w_out, ffn2_norm, ffn2_w_gate, ffn2_w_up, ffn2_w_down, loss_target, m_meta_tokens, m_ffn1_norm, m_ffn1_w_gate, m_ffn1_w_up, m_ffn1_w_down, m_mix_norm, m_w_in, m_b_forget, m_q_norm, m_k_norm, m_pool_w, m_pool_scale, m_w_out, m_ffn2_norm, m_ffn2_w_gate, m_ffn2_w_up, m_ffn2_w_down, v_meta_tokens, v_ffn1_norm, v_ffn1_w_gate, v_ffn1_w_up, v_ffn1_w_down, v_mix_norm, v_w_in, v_b_forget, v_q_norm, v_k_norm, v_pool_w, v_pool_scale, v_w_out, v_ffn2_norm, v_ffn2_w_gate, v_ffn2_w_up, v_ffn2_w_down):
    w = dict(meta_tokens=meta_tokens, ffn1_norm=ffn1_norm, ffn1_w_gate=ffn1_w_gate, ffn1_w_up=ffn1_w_up,
             ffn1_w_down=ffn1_w_down, mix_norm=mix_norm, w_in=w_in, b_forget=b_forget, q_norm=q_norm, k_norm=k_norm,
             pool_w=pool_w, pool_scale=pool_scale, w_out=w_out, ffn2_norm=ffn2_norm, ffn2_w_gate=ffn2_w_gate,
             ffn2_w_up=ffn2_w_up, ffn2_w_down=ffn2_w_down)
    m = dict(meta_tokens=m_meta_tokens, ffn1_norm=m_ffn1_norm, ffn1_w_gate=m_ffn1_w_gate, ffn1_w_up=m_ffn1_w_up,
             ffn1_w_down=m_ffn1_w_down, mix_norm=m_mix_norm, w_in=m_w_in, b_forget=m_b_forget, q_norm=m_q_norm,
             k_norm=m_k_norm, pool_w=m_pool_w, pool_scale=m_pool_scale, w_out=m_w_out, ffn2_norm=m_ffn2_norm,
             ffn2_w_gate=m_ffn2_w_gate, ffn2_w_up=m_ffn2_w_up, ffn2_w_down=m_ffn2_w_down)
    v = dict(meta_tokens=v_meta_tokens, ffn1_norm=v_ffn1_norm, ffn1_w_gate=v_ffn1_w_gate, ffn1_w_up=v_ffn1_w_up,
             ffn1_w_down=v_ffn1_w_down, mix_norm=v_mix_norm, w_in=v_w_in, b_forget=v_b_forget, q_norm=v_q_norm,
             k_norm=v_k_norm, pool_w=v_pool_w, pool_scale=v_pool_scale, w_out=v_w_out, ffn2_norm=v_ffn2_norm,
             ffn2_w_gate=v_ffn2_w_gate, ffn2_w_up=v_ffn2_w_up, ffn2_w_down=v_ffn2_w_down)

    d = x.shape[-1]
    n_heads = b_forget.shape[-1]
    pos = (lax.axis_index("x"), lax.axis_index("y"), lax.axis_index("c"))
    me = 4 * pos[0] + 2 * pos[1] + pos[2]

    raw = {k: w[k] for k in _BIG}
    raw["meta_tokens"] = meta_tokens
    plan = _MeshPlan(raw, pos, d, N_DEV * w_in.shape[-1], n_heads)
    sw = {k: w[k] for k in _SMALL}
    sw["b_forget"] = jnp.pad(b_forget, ((0, 0), (0, LANES - n_heads)))
    dx, dmeta, small = _local_step(x[0], loss_target[0], sw, plan)

    res = {}
    last = dx

    plan.start_small_gather([jnp.concatenate(small, axis=1), dmeta])
    plan.at("after_ffn1_rms_bwd")

    def update_shards(names):
        nonlocal last
        for k in names:
            plan.at("before_adamw_" + k, (last,))
            operands = [pltpu.with_memory_space_constraint(a, pltpu.HBM) for a in
                        (plan.gradient_parts(k), _as2d(k, w[k]), _as2d(k, m[k]), _as2d(k, v[k]))]
            res[k] = _adamw(*operands, "adamw_" + k, plan.comm("adamw_" + k))
            last = res[k][0]

    update_shards(_FFN2 + _MIX + ("ffn1_w_gate", "ffn1_w_up"))

    pack_g, meta_g = plan.wait_small_gather((last,))
    tot, dmeta_tot, loss_row = _small_reduce(pack_g, meta_g, 0.5 / d, "small_reduce")

    mcols = meta_tokens.shape[1]
    g_meta = lax.dynamic_slice_in_dim(dmeta_tot, me * mcols, mcols, axis=1)
    res["meta_tokens"] = _adamw(g_meta, meta_tokens, m_meta_tokens, v_meta_tokens, "adamw_meta_tokens")

    def packed(src):
        return jnp.concatenate([src[k] for k in _SMALL[:-1]] + [jnp.pad(src["b_forget"], ((0, 0), (0, LANES - n_heads)))],
                               axis=1)

    wp = packed(w)
    sm = _adamw(tot[:, :wp.shape[1]], wp, packed(m), packed(v), "adamw_small")
    off = 0
    for k in _SMALL:
        width = w[k].shape[1]
        res[k] = tuple(o[:, off:off + width] for o in sm)
        off += width if k != "b_forget" else LANES

    last = sm[0]
    update_shards(("ffn1_w_down",))

    outs =[loss_row[0, 0], dx[None]]
    for idx in range(4):
        outs += [_from2d(k, res[k][idx], w[k].shape) for k in _ORDER]
    return tuple(outs)
```
